```python
import math
import jax
import jax.numpy as jnp
from jax import lax
import numpy as np

D_MODEL = 1024
BATCH = 4
SEQ = 8192
DEPTH = 4

GRID_W = 64
CTX_LEN = 256
N_EVEN = (DEPTH + 1) // 2
N_ODD = DEPTH // 2
N_MOD = 6
EPS = 1e-6

LRU_WIDTH = D_MODEL // 2
LRU_BLOCKS = 8
LRU_BLOCK = LRU_WIDTH // LRU_BLOCKS
CONV_W = 4
CONV_LEFT = 2
LRU_C = 8.0
ATT_HEADS = 8
ATT_KV_HEADS = 2
ATT_GROUP = ATT_HEADS // ATT_KV_HEADS
HEAD_DIM = 64
ATT_WIDTH = ATT_HEADS * HEAD_DIM
KV_WIDTH = ATT_KV_HEADS * HEAD_DIM
WINDOW = 128
BLOCK_Q = 128
ROPE_FREQS = HEAD_DIM // 4
ROPE_BASE = 10000.0
S5_WIDTH = D_MODEL // 2
S5_GROUP = 16
S5_GROUPS = S5_WIDTH // S5_GROUP
S5_STATE = 64
HG_HEADS = 4
HG_DK = 128
HG_DV = 128
HG_WIDTH = HG_HEADS * HG_DK
HG_CHUNK = 64
FFN_DIM = 2816
N_EXPERTS = 8
TOP_K = 2
EXPERT_DIM = 1408

IN_AB = 2 * LRU_WIDTH + ATT_WIDTH + 2 * KV_WIDTH
MIX_AB = LRU_WIDTH + ATT_WIDTH
IN_CD = S5_WIDTH + 5 * HG_WIDTH
MIX_CD = S5_WIDTH + HG_WIDTH

kernel_name = 'hybrid_rglru_swa_s5_hgrn2_moe_dit'


def rms_norm(x, g):
    xf = x.astype(jnp.float32)
    y = xf * lax.rsqrt(jnp.mean(xf * xf, axis=-1, keepdims=True) + EPS)
    return (y * g.astype(jnp.float32)).astype(x.dtype)


def axial_rope_tables(rows):
    row = jnp.repeat(jnp.arange(rows, dtype=jnp.float32), GRID_W)
    col = jnp.tile(jnp.arange(GRID_W, dtype=jnp.float32), rows)
    inv = ROPE_BASE ** (-jnp.arange(ROPE_FREQS, dtype=jnp.float32) / ROPE_FREQS)
    ang = jnp.stack([row[:, None] * inv, col[:, None] * inv], axis=1)
    return jnp.cos(ang), jnp.sin(ang)


def apply_axial_rope(x, cos, sin):
    shp = x.shape
    xr = x.reshape(shp[:-1] + (2, 2, ROPE_FREQS))
    bshape = (shp[1],) + (1,) * (x.ndim - 3) + (2, ROPE_FREQS)
    c = cos.reshape(bshape).astype(x.dtype)
    s = sin.reshape(bshape).astype(x.dtype)
    x1, x2 = xr[..., 0, :], xr[..., 1, :]
    return jnp.stack([x1 * c - x2 * s, x2 * c + x1 * s], axis=-2).reshape(shp)


def centred_depthwise_conv(x, w, b):
    T = x.shape[1]
    xp = jnp.pad(x, ((0, 0), (CONV_LEFT, CONV_W - 1 - CONV_LEFT), (0, 0)))
    y = b + xp[:, 0:T] * w[0]
    for j in range(1, CONV_W):
        y = y + xp[:, j:j + T] * w[j]
    return y


def block_diag(x, w):
    B, T, _ = x.shape
    xb = x.reshape(B, T, w.shape[0], w.shape[1])
    return jnp.einsum('btnh,nhk->btnk', xb, w).reshape(B, T, -1)


def linear_scan(a, b, h0, reverse):
    def combine(e1, e2):
        a1, b1 = e1
        a2, b2 = e2
        return a1 * a2, a2 * b1 + b2
    a_cum, h = lax.associative_scan(combine, (a, b), axis=1, reverse=reverse)
    if h0 is not None:
        h = h + a_cum * h0[:, None]
    return h


def rglru_coeffs(u, wa, ba, wx, bx, lam):
    r = jax.nn.sigmoid(block_diag(u, wa) + ba)
    i = jax.nn.sigmoid(block_diag(u, wx) + bx)
    log_a = -LRU_C * r * jax.nn.softplus(-lam)
    a = jnp.exp(log_a)
    mult = jnp.sqrt(-jnp.expm1(2.0 * log_a))
    return a, mult * (i * u)


def rglru_bidir(u_x, u_c, need_ctx, wa, ba, wx, bx, lam):
    hx_dirs, hc_dirs = [], []
    for d, rev in enumerate((False, True)):
        a_c, b_c = rglru_coeffs(u_c, wa[d], ba[d], wx[d], bx[d], lam[d])
        h_c = linear_scan(a_c, b_c, None, rev)
        h0 = h_c[:, 0] if rev else h_c[:, -1]
        a_x, b_x = rglru_coeffs(u_x, wa[d], ba[d], wx[d], bx[d], lam[d])
        hx_dirs.append(linear_scan(a_x, b_x, h0, rev))
        hc_dirs.append(h_c)
    h_x = hx_dirs[0] + hx_dirs[1]
    h_c = (hc_dirs[0] + hc_dirs[1]) if need_ctx else None
    return h_x, h_c


def band_mask(nb, T):
    n = jnp.arange(nb)[:, None, None]
    qpos = n * BLOCK_Q + jnp.arange(BLOCK_Q)[None, :, None]
    kpos = (n - 1) * BLOCK_Q + jnp.arange(3 * BLOCK_Q)[None, None, :]
    return (jnp.abs(qpos - kpos) <= WINDOW) & (kpos >= 0) & (kpos < T)


def banded(t):
    pad = [(0, 0), (1, 1)] + [(0, 0)] * (t.ndim - 2)
    tp = jnp.pad(t, pad)
    return jnp.concatenate([tp[:, :-2], tp[:, 1:-1], tp[:, 2:]], axis=2)


def window_gqa_with_sink(q_x, k_x, v_x, q_c, k_c, v_c, sink, need_ctx):
    B, T = q_x.shape[:2]
    nb = T // BLOCK_Q
    n_loc = 3 * BLOCK_Q
    n_ctx = k_c.shape[1]
    scale = HEAD_DIM ** -0.5
    qb = q_x.reshape(B, nb, BLOCK_Q, ATT_KV_HEADS, ATT_GROUP, HEAD_DIM)
    kb = banded(k_x.reshape(B, nb, BLOCK_Q, ATT_KV_HEADS, HEAD_DIM))
    vb = banded(v_x.reshape(B, nb, BLOCK_Q, ATT_KV_HEADS, HEAD_DIM))
    s_loc = jnp.einsum('bnqhgd,bnkhd->bnhgqk', qb, kb).astype(jnp.float32) * scale
    s_loc = jnp.where(band_mask(nb, T)[None, :, None, None], s_loc, -jnp.inf)
    s_ctx = jnp.einsum('bnqhgd,bchd->bnhgqc', qb, k_c).astype(jnp.float32) * scale
    sink_hg = sink.astype(jnp.float32).reshape(ATT_KV_HEADS, ATT_GROUP, 1, 1)
    s_sink = jnp.broadcast_to(sink_hg, s_loc.shape[:-1] + (1,))
    p = jax.nn.softmax(jnp.concatenate([s_loc, s_ctx, s_sink], axis=-1), axis=-1).astype(v_x.dtype)
    o = (jnp.einsum('bnhgqk,bnkhd->bnqhgd', p[..., :n_loc], vb)
         + jnp.einsum('bnhgqc,bchd->bnqhgd', p[..., n_loc:n_loc + n_ctx], v_c))
    o_x = o.reshape(B, T, ATT_WIDTH)
    if not need_ctx:
        return o_x, None
    s_cc = jnp.einsum('bqhgd,bchd->bhgqc', q_c, k_c).astype(jnp.float32) * scale
    s_sink_c = jnp.broadcast_to(sink_hg, s_cc.shape[:-1] + (1,))
    p_c = jax.nn.softmax(jnp.concatenate([s_cc, s_sink_c], axis=-1), axis=-1).astype(v_c.dtype)
    o_c = jnp.einsum('bhgqc,bchd->bqhgd', p_c[..., :n_ctx], v_c).reshape(B, n_ctx, ATT_WIDTH)
    return o_x, o_c


def mixer_ab(hx, hc, need_ctx, rope, w_in, conv_w, conv_b, wa, ba, wx, bx, lam, sink, w_out):
    B, T, _ = hx.shape
    Tc = hc.shape[1]
    splits = [LRU_WIDTH, 2 * LRU_WIDTH, 2 * LRU_WIDTH + ATT_WIDTH, 2 * LRU_WIDTH + ATT_WIDTH + KV_WIDTH]
    gx, ux, qx, kx, vx = jnp.split(hx @ w_in, splits, axis=-1)
    gc, uc, qc, kc, vc = jnp.split(hc @ w_in, splits, axis=-1)
    ux = centred_depthwise_conv(ux, conv_w, conv_b)
    uc = centred_depthwise_conv(uc, conv_w, conv_b)
    lru_x, lru_c = rglru_bidir(ux, uc, need_ctx, wa, ba, wx, bx, lam)
    cos, sin = rope
    qx = apply_axial_rope(qx.reshape(B, T, ATT_KV_HEADS, ATT_GROUP, HEAD_DIM), cos, sin)
    kx = apply_axial_rope(kx.reshape(B, T, ATT_KV_HEADS, HEAD_DIM), cos, sin)
    vx = vx.reshape(B, T, ATT_KV_HEADS, HEAD_DIM)
    qc = qc.reshape(B, Tc, ATT_KV_HEADS, ATT_GROUP, HEAD_DIM)
    kc = kc.reshape(B, Tc, ATT_KV_HEADS, HEAD_DIM)
    vc = vc.reshape(B, Tc, ATT_KV_HEADS, HEAD_DIM)
    att_x, att_c = window_gqa_with_sink(qx, kx, vx, qc, kc, vc, sink, need_ctx)
    dx = jnp.concatenate([lru_x * jax.nn.gelu(gx), att_x], axis=-1) @ w_out
    dc = (jnp.concatenate([lru_c * jax.nn.gelu(gc), att_c], axis=-1) @ w_out) if need_ctx else None
    return dx, dc


def s5_discretise(a_re, a_im, log_step, b_re, b_im):
    lr = jnp.minimum(a_re, -1e-4)
    li = a_im
    dt = jnp.exp(log_step)[:, None]
    mag = jnp.exp(lr * dt)
    ang = li * dt
    lbr, lbi = mag * jnp.cos(ang), mag * jnp.sin(ang)
    zr, zi = lbr - 1.0, lbi
    den = lr * lr + li * li
    fr = (zr * lr + zi * li) / den
    fi = (zi * lr - zr * li) / den
    bbr = fr[..., None] * b_re - fi[..., None] * b_im
    bbi = fr[..., None] * b_im + fi[..., None] * b_re
    return lbr, lbi, bbr, bbi


def _complex_combine(e1, e2):
    a1r, a1i, b1r, b1i = e1
    a2r, a2i, b2r, b2i = e2
    return (a2r * a1r - a2i * a1i, a2r * a1i + a2i * a1r,
            a2r * b1r - a2i * b1i + b2r, a2r * b1i + a2i * b1r + b2i)


def s5_scan(u, disc, h0, reverse):
    lbr, lbi, bbr, bbi = disc
    B, T, _ = u.shape
    ug = u.reshape(B, T, S5_GROUPS, S5_GROUP)
    br = jnp.einsum('btgc,gnc->btgn', ug, bbr)
    bi = jnp.einsum('btgc,gnc->btgn', ug, bbi)
    ar = jnp.broadcast_to(lbr, br.shape)
    ai = jnp.broadcast_to(lbi, bi.shape)
    a_r, a_i, hr, hi = lax.associative_scan(_complex_combine, (ar, ai, br, bi), axis=1, reverse=reverse)
    if h0 is not None:
        h0r, h0i = h0[0][:, None], h0[1][:, None]
        hr, hi = hr + a_r * h0r - a_i * h0i, hi + a_r * h0i + a_i * h0r
    return hr, hi


def s5_readout(hr, hi, c_re, c_im):
    y = jnp.einsum('btgn,gcn->btgc', hr, c_re) - jnp.einsum('btgn,gcn->btgc', hi, c_im)
    return y.reshape(hr.shape[0], hr.shape[1], S5_WIDTH)


def s5_bidir(u_x, u_c, need_ctx, a_re, a_im, log_step, b_re, b_im, c_re, c_im, d, w_glu, b_glu):
    y_x = d * u_x
    y_c = d * u_c if need_ctx else None
    for dr, rev in enumerate((False, True)):
        disc = s5_discretise(a_re[dr], a_im[dr], log_step[dr], b_re[dr], b_im[dr])
        hr_c, hi_c = s5_scan(u_c, disc, None, rev)
        edge = 0 if rev else -1
        hr_x, hi_x = s5_scan(u_x, disc, (hr_c[:, edge], hi_c[:, edge]), rev)
        y_x = y_x + s5_readout(hr_x, hi_x, c_re[dr], c_im[dr])
        if need_ctx:
            y_c = y_c + s5_readout(hr_c, hi_c, c_re[dr], c_im[dr])

    def glu(y):
        y = jax.nn.gelu(y)
        return y * jax.nn.sigmoid(y @ w_glu + b_glu)

    return glu(y_x), (glu(y_c) if need_ctx else None)


def gla_chunk_scan(q, k, v, log_f, s0):
    B, T, H, K = q.shape
    V = v.shape[-1]
    nc = T // HG_CHUNK

    def to_chunks(t):
        return t.reshape(B, nc, HG_CHUNK, H, t.shape[-1]).transpose(1, 0, 3, 2, 4)

    lower = jnp.tril(jnp.ones((HG_CHUNK, HG_CHUNK), dtype=bool))

    def step(s, inp):
        qc, kc, vc, gc = inp
        bcum = jnp.cumsum(gc, axis=2)
        o_inter = jnp.einsum('bhtk,bhkv->bhtv', qc * jnp.exp(bcum), s)
        diff = jnp.where(lower[:, :, None], bcum[:, :, :, None, :] - bcum[:, :, None, :, :], -jnp.inf)
        attn = jnp.einsum('bhtk,bhsk,bhtsk->bhts', qc, kc, jnp.exp(diff))
        o = o_inter + jnp.einsum('bhts,bhsv->bhtv', attn, vc)
        b_last = bcum[:, :, -1:, :]
        s = (jnp.exp(b_last[:, :, 0, :, None]) * s
             + jnp.einsum('bhsk,bhsv->bhkv', kc * jnp.exp(b_last - bcum), vc))
        return s, o

    s_fin, o = lax.scan(step, s0, (to_chunks(q), to_chunks(k), to_chunks(v), to_chunks(log_f)))
    return o.transpose(1, 0, 3, 2, 4).reshape(B, T, H, V), s_fin


def _heads(t):
    return t.reshape(t.shape[:2] + (HG_HEADS, -1)).astype(jnp.float32)


def hgrn_dir(q, v, zf, lbh, s0, rev):
    f = lbh + (1.0 - lbh) * jax.nn.sigmoid(zf)
    log_f = jnp.log(f)
    k = 1.0 - f
    if rev:
        q, k, v, log_f = (jnp.flip(t, axis=1) for t in (q, k, v, log_f))
    o, s = gla_chunk_scan(q, k, v, log_f, s0)
    if rev:
        o = jnp.flip(o, axis=1)
    return o, s


def hgrn2_bidir(zx, zc, need_ctx, lb, norm_g):
    qx, ffx, fbx, ix, gx = zx
    qc, ffc, fbc, ic, gc = zc
    q_x, v_x = _heads(jax.nn.silu(qx)), _heads(ix)
    q_c, v_c = _heads(jax.nn.silu(qc)), _heads(ic)
    lbh = lb.reshape(HG_HEADS, HG_DK)
    s0 = jnp.zeros((qx.shape[0], HG_HEADS, HG_DK, HG_DV), jnp.float32)
    ox, oc = [], []
    for zf_x, zf_c, rev in ((ffx, ffc, False), (fbx, fbc, True)):
        o_c, s_c = hgrn_dir(q_c, v_c, _heads(zf_c), lbh, s0, rev)
        o_x, _ = hgrn_dir(q_x, v_x, _heads(zf_x), lbh, s_c, rev)
        ox.append(o_x)
        oc.append(o_c)

    def readout(o, g):
        return (rms_norm(o, norm_g) * jax.nn.silu(_heads(g))).reshape(g.shape).astype(g.dtype)

    out_x = readout(ox[0] + ox[1], gx)
    out_c = readout(oc[0] + oc[1], gc) if need_ctx else None
    return out_x, out_c


def mixer_cd(hx, hc, need_ctx, w_in, a_re, a_im, log_step, b_re, b_im, c_re, c_im, d, w_glu, b_glu,
             lb, hg_g, w_out):
    splits = [S5_WIDTH + j * HG_WIDTH for j in range(5)]
    ux, qx, ffx, fbx, ix, gx = jnp.split(hx @ w_in, splits, axis=-1)
    uc, qc, ffc, fbc, ic, gc = jnp.split(hc @ w_in, splits, axis=-1)
    s5_x, s5_c = s5_bidir(ux, uc, need_ctx, a_re, a_im, log_step, b_re, b_im, c_re, c_im, d, w_glu, b_glu)
    hg_x, hg_c = hgrn2_bidir((qx, ffx, fbx, ix, gx), (qc, ffc, fbc, ic, gc), need_ctx, lb, hg_g)
    dx = jnp.concatenate([s5_x, hg_x], axis=-1) @ w_out
    dc = (jnp.concatenate([s5_c, hg_c], axis=-1) @ w_out) if need_ctx else None
    return dx, dc


def swiglu(h, w1, w3, w2):
    return (jax.nn.silu(h @ w1) * (h @ w3)) @ w2


def moe_swiglu(h, router, w1, w3, w2):
    logits = (h @ router).astype(jnp.float32)
    top_v, top_i = lax.top_k(logits, TOP_K)
    gates = jax.nn.softmax(top_v, axis=-1)
    weight = jnp.sum(jax.nn.one_hot(top_i, N_EXPERTS, dtype=jnp.float32) * gates[..., None], axis=-2)
    weight = weight.astype(h.dtype)
    out = jnp.zeros_like(h)
    for e in range(N_EXPERTS):
        out = out + weight[..., e:e + 1] * swiglu(h, w1[e], w3[e], w2[e])
    return out


def setup_inputs(seed: int = 0) -> dict:
    key = jax.random.key(seed)
    ks = iter(jax.random.split(key, 64))
    D = D_MODEL

    def nrm(shape, scale):
        return scale * jax.random.normal(next(ks), shape, jnp.float32)

    def gain(shape):
        return 1.0 + nrm(shape, 0.01)

    a8 = jax.random.uniform(next(ks), (N_EVEN, 2, LRU_WIDTH), jnp.float32, 0.9, 0.999)
    s = a8 ** (1.0 / LRU_C)
    lru_lam = jnp.log(s) - jnp.log1p(-s)
    s5_log_step = jax.random.uniform(next(ks), (N_ODD, 2, S5_GROUPS), jnp.float32,
                                     math.log(1e-3), math.log(1e-1))
    s5_a_re = -0.5 + nrm((N_ODD, 2, S5_GROUPS, S5_STATE), 0.01)
    s5_a_im = math.pi * jnp.arange(S5_STATE, dtype=jnp.float32) + nrm((N_ODD, 2, S5_GROUPS, S5_STATE), 0.01)
    return {
        'x': nrm((BATCH, SEQ, D), 1.0),
        'c': nrm((BATCH, D), 1.0),
        'ctx': nrm((BATCH, CTX_LEN, D), 1.0),
        'c_ctx': nrm((D,), 1.0),
        'w_mod': nrm((DEPTH, D, N_MOD * D), 0.5 * D ** -0.5),
        'b_mod': nrm((DEPTH, N_MOD * D), 0.01),
        'norm_mix': gain((DEPTH, D)),
        'norm_ffn': gain((DEPTH, D)),
        'final_norm': gain((D,)),
        'w_in_ab': nrm((N_EVEN, D, IN_AB), D ** -0.5),
        'lru_conv_w': nrm((N_EVEN, CONV_W, LRU_WIDTH), CONV_W ** -0.5),
        'lru_conv_b': nrm((N_EVEN, LRU_WIDTH), 0.01),
        'lru_wa': nrm((N_EVEN, 2, LRU_BLOCKS, LRU_BLOCK, LRU_BLOCK), LRU_BLOCK ** -0.5),
        'lru_ba': nrm((N_EVEN, 2, LRU_WIDTH), 0.01),
        'lru_wx': nrm((N_EVEN, 2, LRU_BLOCKS, LRU_BLOCK, LRU_BLOCK), LRU_BLOCK ** -0.5),
        'lru_bx': nrm((N_EVEN, 2, LRU_WIDTH), 0.01),
        'lru_lam': lru_lam,
        'attn_sink': nrm((N_EVEN, ATT_HEADS), 1.0),
        'w_out_ab': nrm((N_EVEN, MIX_AB, D), MIX_AB ** -0.5),
        'ffn_w1': nrm((N_EVEN, D, FFN_DIM), D ** -0.5),
        'ffn_w3': nrm((N_EVEN, D, FFN_DIM), D ** -0.5),
        'ffn_w2': nrm((N_EVEN, FFN_DIM, D), FFN_DIM ** -0.5),
        'w_in_cd': nrm((N_ODD, D, IN_CD), D ** -0.5),
        's5_a_re': s5_a_re,
        's5_a_im': s5_a_im,
        's5_log_step': s5_log_step,
        's5_b_re': nrm((N_ODD, 2, S5_GROUPS, S5_STATE, S5_GROUP), S5_GROUP ** -0.5),
        's5_b_im': nrm((N_ODD, 2, S5_GROUPS, S5_STATE, S5_GROUP), S5_GROUP ** -0.5),
        's5_c_re': nrm((N_ODD, 2, S5_GROUPS, S5_GROUP, S5_STATE), S5_STATE ** -0.5),
        's5_c_im': nrm((N_ODD, 2, S5_GROUPS, S5_GROUP, S5_STATE), S5_STATE ** -0.5),
        's5_d': nrm((N_ODD, S5_WIDTH), 1.0),
        's5_w_glu': nrm((N_ODD, S5_WIDTH, S5_WIDTH), S5_WIDTH ** -0.5),
        's5_b_glu': nrm((N_ODD, S5_WIDTH), 0.01),
        'hg_lb_raw': nrm((N_ODD, HG_WIDTH), 1.0),
        'hg_norm': gain((N_ODD, HG_DV)),
        'w_out_cd': nrm((N_ODD, MIX_CD, D), MIX_CD ** -0.5),
        'moe_router': nrm((N_ODD, D, N_EXPERTS), D ** -0.5),
        'moe_w1': nrm((N_ODD, N_EXPERTS, D, EXPERT_DIM), D ** -0.5),
        'moe_w3': nrm((N_ODD, N_EXPERTS, D, EXPERT_DIM), D ** -0.5),
        'moe_w2': nrm((N_ODD, N_EXPERTS, EXPERT_DIM, D), EXPERT_DIM ** -0.5),
    }


def reference(x, c, ctx, c_ctx, w_mod, b_mod, norm_mix, norm_ffn, final_norm,
              w_in_ab, lru_conv_w, lru_conv_b, lru_wa, lru_ba, lru_wx, lru_bx, lru_lam, attn_sink, w_out_ab,
              ffn_w1, ffn_w3, ffn_w2,
              w_in_cd, s5_a_re, s5_a_im, s5_log_step, s5_b_re, s5_b_im, s5_c_re, s5_c_im, s5_d, s5_w_glu, s5_b_glu,
              hg_lb_raw, hg_norm, w_out_cd,
              moe_router, moe_w1, moe_w3, moe_w2):
    rows = x.shape[1] // GRID_W
    rope = axial_rope_tables(rows)
    lb_soft = jax.nn.softmax(hg_lb_raw.astype(jnp.float32), axis=0)
    lb_table = jnp.cumsum(lb_soft, axis=0) - lb_soft[0:1]
    silu_c = jax.nn.silu(c)
    silu_cc = jax.nn.silu(c_ctx)
    xc = ctx
    for l in range(DEPTH):
        need_ctx = l < DEPTH - 1
        j = l // 2
        mod_x = jnp.split((silu_c @ w_mod[l] + b_mod[l])[:, None, :], N_MOD, axis=-1)
        mod_c = jnp.split(silu_cc @ w_mod[l] + b_mod[l], N_MOD, axis=-1)
        hx = rms_norm(x, norm_mix[l]) * (1.0 + mod_x[1]) + mod_x[0]
        hc = rms_norm(xc, norm_mix[l]) * (1.0 + mod_c[1]) + mod_c[0]
        if l % 2 == 0:
            dx, dc = mixer_ab(hx, hc, need_ctx, rope, w_in_ab[j], lru_conv_w[j], lru_conv_b[j],
                              lru_wa[j], lru_ba[j], lru_wx[j], lru_bx[j], lru_lam[j], attn_sink[j], w_out_ab[j])
        else:
            dx, dc = mixer_cd(hx, hc, need_ctx, w_in_cd[j], s5_a_re[j], s5_a_im[j], s5_log_step[j],
                              s5_b_re[j], s5_b_im[j], s5_c_re[j], s5_c_im[j], s5_d[j], s5_w_glu[j], s5_b_glu[j],
                              lb_table[j], hg_norm[j], w_out_cd[j])
        x = x + mod_x[2] * dx
        hx = rms_norm(x, norm_ffn[l]) * (1.0 + mod_x[4]) + mod_x[3]
        if l % 2 == 0:
            x = x + mod_x[5] * swiglu(hx, ffn_w1[j], ffn_w3[j], ffn_w2[j])
        else:
            x = x + mod_x[5] * moe_swiglu(hx, moe_router[j], moe_w1[j], moe_w3[j], moe_w2[j])
        if need_ctx:
            xc = xc + mod_c[2] * dc
            hc = rms_norm(xc, norm_ffn[l]) * (1.0 + mod_c[4]) + mod_c[3]
            if l % 2 == 0:
                xc = xc + mod_c[5] * swiglu(hc, ffn_w1[j], ffn_w3[j], ffn_w2[j])
            else:
                xc = xc + mod_c[5] * moe_swiglu(hc, moe_router[j], moe_w1[j], moe_w3[j], moe_w2[j])
    return rms_norm(x, final_norm)
```

```python
import functools
import math

import jax
import jax.numpy as jnp
from jax import lax
from jax.experimental import pallas as pl
from jax.experimental.pallas import tpu as pltpu

F32 = jnp.float32
BF16 = jnp.bfloat16

EPS = 1e-6
GRID_W = 64
LRU_BLOCKS = 8
LRU_C = 8.0
CONV_W = 4
ATT_HEADS = 8
ATT_KV_HEADS = 2
ATT_GROUP = ATT_HEADS // ATT_KV_HEADS
HEAD_DIM = 64
WINDOW = 128
BLOCK_Q = 128
ROPE_FREQS = HEAD_DIM // 4
ROPE_BASE = 10000.0
S5_GROUP = 16
S5_STATE = 64
S5_CHUNK = 8
HG_HEADS = 4
HG_DK = 128
HG_CHUNK = 64
SEQ_BLOCK = 256
N_MOD = 6
TOP_K = 2
LANES = 128
SUBLANES = 8
VMEM_LIMIT = 56 * 1024 * 1024
NEG = -1e30


def _cparams(sem):
    return pltpu.CompilerParams(dimension_semantics=sem, vmem_limit_bytes=VMEM_LIMIT)


def _row_tile(ttot):
    for tm in (768, 1024, 512, 256):
        if ttot % tm == 0:
            return tm
    raise ValueError(f"unsupported token count {ttot}")


def _sigmoid(x):
    return 1.0 / (1.0 + jnp.exp(-x))


def _silu(x):
    return x * _sigmoid(x)


def _gelu(x):
    return 0.5 * x * (1.0 + jnp.tanh(math.sqrt(2.0 / math.pi) * (x + 0.044715 * (x * x * x))))


def _norm_mod(x, g, mb, mc, row0, tc, shift_idx, scale_idx):
    ms = jnp.mean(x * x, axis=-1, keepdims=True)
    y = x * lax.rsqrt(ms + EPS) * g
    rows = row0 + lax.broadcasted_iota(jnp.int32, (x.shape[0], 1), 0)
    is_ctx = rows < tc
    scale = jnp.where(is_ctx, mc[scale_idx:scale_idx + 1], mb[scale_idx:scale_idx + 1])
    shift = jnp.where(is_ctx, mc[shift_idx:shift_idx + 1], mb[shift_idx:shift_idx + 1])
    return y * (1.0 + scale) + shift


def _gate_mod(mb, mc, idx, row0, tc, n):
    rows = row0 + lax.broadcasted_iota(jnp.int32, (n, 1), 0)
    return jnp.where(rows < tc, mc[idx:idx + 1], mb[idx:idx + 1])


def _mod_kernel(c_ref, w_ref, b_ref, o_ref):
    s = _silu(c_ref[...])
    o_ref[...] = jnp.dot(s, w_ref[...], preferred_element_type=F32,
                         precision=lax.Precision.HIGHEST) + b_ref[...]


def _modulation(cvec, w_mod, b_mod):
    depth, d, n = w_mod.shape
    tn = 1536 if n % 1536 == 0 else n
    rows = cvec.shape[0]
    return pl.pallas_call(
        _mod_kernel,
        grid=(depth, n // tn),
        in_specs=[pl.BlockSpec((rows, d), lambda l, j: (0, 0)),
                  pl.BlockSpec((None, d, tn), lambda l, j: (l, 0, j)),
                  pl.BlockSpec((None, 1, tn), lambda l, j: (l, 0, j))],
        out_specs=pl.BlockSpec((None, rows, tn), lambda l, j: (l, 0, j)),
        out_shape=jax.ShapeDtypeStruct((depth, rows, n), F32),
        compiler_params=_cparams(("arbitrary", "arbitrary")),
        name="modulation",
    )(cvec, w_mod, b_mod.reshape(depth, 1, n))


def _inproj_kernel(x_ref, mb_ref, mc_ref, g_ref, w_ref, o_ref, *, tm, tc):
    i = pl.program_id(1)
    h = _norm_mod(x_ref[...], g_ref[...], mb_ref[...], mc_ref[...], i * tm, tc, 0, 1)
    o_ref[...] = jnp.dot(h.astype(BF16), w_ref[...], preferred_element_type=F32)


def _inproj(xa, mod_l, g, w, tc, nb):
    b, ttot, d = xa.shape
    n = w.shape[1]
    tm = _row_tile(ttot)
    return pl.pallas_call(
        functools.partial(_inproj_kernel, tm=tm, tc=tc),
        grid=(b, ttot // tm),
        in_specs=[pl.BlockSpec((None, tm, d), lambda bi, i: (bi, i, 0)),
                  pl.BlockSpec((None, N_MOD, d), lambda bi, i: (bi, 0, 0)),
                  pl.BlockSpec((None, N_MOD, d), lambda bi, i: (nb, 0, 0)),
                  pl.BlockSpec((1, d), lambda bi, i: (0, 0)),
                  pl.BlockSpec((d, n), lambda bi, i: (0, 0))],
        out_specs=pl.BlockSpec((None, tm, n), lambda bi, i: (bi, i, 0)),
        out_shape=jax.ShapeDtypeStruct((b, ttot, n), F32),
        compiler_params=_cparams(("parallel", "parallel")),
        name="inproj",
    )(xa, mod_l, mod_l, g.reshape(1, d), w)


def _seq_block(step, n_ctx, n_all, reverse):
    if not reverse:
        return step
    return jnp.where(step < n_ctx, n_ctx - 1 - step, n_all - 1 - step + n_ctx)


def _lru_kernel(u_ref, hp_ref, hn_ref, cw_ref, cb_ref, wa_ref, ba_ref, wx_ref, bx_ref, sp_ref,
                o_ref, uc_ref, a_ref, b_ref, h_ref, *, lc, n_ctx, n_all, reverse):
    step = pl.program_id(1)
    blk = _seq_block(step, n_ctx, n_all, reverse)
    has_prev = jnp.logical_and(blk != 0, blk != n_ctx)
    has_next = jnp.logical_and(blk != n_ctx - 1, blk != n_all - 1)

    @pl.when(step == 0)
    def _():
        h_ref[...] = jnp.zeros_like(h_ref)

    w0, w1, w2, w3 = (cw_ref[j:j + 1, :] for j in range(CONV_W))
    cb = cb_ref[...]
    u = u_ref[...]
    uc_ref[...] = (cb + pltpu.roll(u, 2, 0) * w0 + pltpu.roll(u, 1, 0) * w1 + u * w2
                   + pltpu.roll(u, lc - 1, 0) * w3)
    row = lax.broadcasted_iota(jnp.int32, (SUBLANES, 1), 0)
    hp = jnp.where(has_prev, hp_ref[...], 0.0)
    hn = jnp.where(has_next, hn_ref[...], 0.0)
    u0 = u_ref[0:SUBLANES, :]
    u1 = u_ref[SUBLANES:2 * SUBLANES, :]
    uc_ref[0:SUBLANES, :] = (
        cb + jnp.where(row < 2, pltpu.roll(hp, 2, 0), pltpu.roll(u0, 2, 0)) * w0
        + jnp.where(row < 1, pltpu.roll(hp, 1, 0), pltpu.roll(u0, 1, 0)) * w1 + u0 * w2
        + jnp.where(row < SUBLANES - 1, pltpu.roll(u0, SUBLANES - 1, 0),
                    pltpu.roll(u1, SUBLANES - 1, 0)) * w3)
    ul = u_ref[lc - SUBLANES:lc, :]
    um = u_ref[lc - 2 * SUBLANES:lc - SUBLANES, :]
    uc_ref[lc - SUBLANES:lc, :] = (
        cb + jnp.where(row < 2, pltpu.roll(um, 2, 0), pltpu.roll(ul, 2, 0)) * w0
        + jnp.where(row < 1, pltpu.roll(um, 1, 0), pltpu.roll(ul, 1, 0)) * w1 + ul * w2
        + jnp.where(row < SUBLANES - 1, pltpu.roll(ul, SUBLANES - 1, 0),
                    pltpu.roll(hn, SUBLANES - 1, 0)) * w3)

    uc = uc_ref[...]
    ub = uc.astype(BF16)
    r = _sigmoid(jnp.dot(ub, wa_ref[...], preferred_element_type=F32) + ba_ref[...])
    gi = _sigmoid(jnp.dot(ub, wx_ref[...], preferred_element_type=F32) + bx_ref[...])
    log_a = -LRU_C * r * sp_ref[...]
    a_ref[...] = jnp.exp(log_a)
    b_ref[...] = jnp.sqrt(1.0 - jnp.exp(2.0 * log_a)) * (gi * uc)

    def body(t, h):
        tt = (lc - 1 - t) if reverse else t
        h = a_ref[pl.ds(tt, 1), :] * h + b_ref[pl.ds(tt, 1), :]
        o_ref[pl.ds(tt, 1), :] = h
        return h

    h_ref[0:1, :] = lax.fori_loop(0, lc, body, h_ref[0:1, :], unroll=8)


def _lru(z, conv_w, conv_b, wa_bd, ba, wx_bd, bx, sp, tc, reverse):
    b, ttot, _ = z.shape
    w = conv_w.shape[1]
    lc = SEQ_BLOCK
    n_all, n_ctx = ttot // lc, tc // lc
    hb = lc // SUBLANES
    n_h = ttot // SUBLANES
    col = 1

    def blk(s):
        return _seq_block(s, n_ctx, n_all, reverse)

    vec = lambda: pl.BlockSpec((1, w), lambda bi, s: (0, 0))
    return pl.pallas_call(
        functools.partial(_lru_kernel, lc=lc, n_ctx=n_ctx, n_all=n_all, reverse=reverse),
        grid=(b, n_all),
        in_specs=[pl.BlockSpec((None, lc, w), lambda bi, s: (bi, blk(s), col)),
                  pl.BlockSpec((None, SUBLANES, w),
                               lambda bi, s: (bi, jnp.maximum(blk(s) * hb - 1, 0), col)),
                  pl.BlockSpec((None, SUBLANES, w),
                               lambda bi, s: (bi, jnp.minimum((blk(s) + 1) * hb, n_h - 1), col)),
                  pl.BlockSpec((CONV_W, w), lambda bi, s: (0, 0)),
                  vec(),
                  pl.BlockSpec((w, w), lambda bi, s: (0, 0)),
                  vec(),
                  pl.BlockSpec((w, w), lambda bi, s: (0, 0)),
                  vec(), vec()],
        out_specs=pl.BlockSpec((None, lc, w), lambda bi, s: (bi, blk(s), 0)),
        out_shape=jax.ShapeDtypeStruct((b, ttot, w), F32),
        scratch_shapes=[pltpu.VMEM((lc, w), F32), pltpu.VMEM((lc, w), F32),
                        pltpu.VMEM((lc, w), F32), pltpu.VMEM((SUBLANES, w), F32)],
        compiler_params=_cparams(("parallel", "arbitrary")),
        name="rglru_rev" if reverse else "rglru_fwd",
    )(z, z, z, conv_w, conv_b.reshape(1, w), wa_bd, ba.reshape(1, w), wx_bd, bx.reshape(1, w),
      sp.reshape(1, w))


def _rope(x, cos, sin_signed):
    lane = lax.broadcasted_iota(jnp.int32, x.shape, 1)
    swapped = jnp.where(lane % 32 < 16, pltpu.roll(x, LANES - 16, 1), pltpu.roll(x, 16, 1))
    return x * cos + swapped * sin_signed


def _attend(q, keys, vals, valid, sink_row, o_ref):
    scale = HEAD_DIM ** -0.5
    for hd in range(ATT_HEADS):
        kv = hd // ATT_GROUP
        qh = q[:, hd * HEAD_DIM:(hd + 1) * HEAD_DIM].astype(BF16)
        kh = keys[:, kv * HEAD_DIM:(kv + 1) * HEAD_DIM]
        vh = vals[:, kv * HEAD_DIM:(kv + 1) * HEAD_DIM]
        s = lax.dot_general(qh, kh, (((1,), (1,)), ((), ())), preferred_element_type=F32) * scale
        if valid is not None:
            s = jnp.where(valid, s, NEG)
        sk = sink_row[:, hd:hd + 1]
        m = jnp.maximum(jnp.max(s, axis=-1, keepdims=True), sk)
        p = jnp.exp(s - m)
        denom = jnp.sum(p, axis=-1, keepdims=True) + jnp.exp(sk - m)
        o = jnp.dot(p.astype(BF16), vh, preferred_element_type=F32)
        o_ref[:, hd * HEAD_DIM:(hd + 1) * HEAD_DIM] = o / denom


def _attn_kernel(q_ref, kp_ref, k0_ref, kn_ref, vp_ref, v0_ref, vn_ref, kc_ref, vc_ref,
                 cq_ref, sq_ref, cp_ref, sp_ref, cn_ref, sn_ref, sink_ref, o_ref, *, n_ctx, n_lat):
    i = pl.program_id(1)
    sink_row = sink_ref[...]
    kc = kc_ref[...].astype(BF16)
    vc = vc_ref[...].astype(BF16)

    @pl.when(i < n_ctx)
    def _():
        _attend(q_ref[...], kc, vc, None, sink_row, o_ref)

    @pl.when(i >= n_ctx)
    def _():
        n = i - n_ctx
        cq, sq = cq_ref[...], sq_ref[...]
        q = jnp.concatenate(
            [_rope(q_ref[:, g * LANES:(g + 1) * LANES], cq, sq) for g in range(ATT_HEADS // 2)],
            axis=1)
        kp = _rope(kp_ref[...], cp_ref[...], sp_ref[...])
        k0 = _rope(k0_ref[...], cq, sq)
        kn = _rope(kn_ref[...], cn_ref[...], sn_ref[...])
        keys = jnp.concatenate([kp, k0, kn, kc_ref[...]], axis=0).astype(BF16)
        vals = jnp.concatenate([vp_ref[...], v0_ref[...], vn_ref[...], vc_ref[...]],
                               axis=0).astype(BF16)
        nk = keys.shape[0]
        qpos = n * BLOCK_Q + lax.broadcasted_iota(jnp.int32, (BLOCK_Q, nk), 0)
        col = lax.broadcasted_iota(jnp.int32, (BLOCK_Q, nk), 1)
        kpos = (n - 1) * BLOCK_Q + col
        local = (jnp.abs(qpos - kpos) <= WINDOW) & (kpos >= 0) & (kpos < n_lat * BLOCK_Q)
        valid = local | (col >= 3 * BLOCK_Q)
        _attend(q, keys, vals, valid, sink_row, o_ref)


def _attention(z, cos_t, sin_t, sink, tc):
    b, ttot, _ = z.shape
    n_all, n_ctx = ttot // BLOCK_Q, tc // BLOCK_Q
    n_lat = n_all - n_ctx
    kcol = (2 * 512 + ATT_HEADS * HEAD_DIM) // LANES
    vcol = kcol + 1
    qcol = 2 * 512 // (ATT_HEADS * HEAD_DIM)

    def prev(i):
        return jnp.maximum(i - 1, 0)

    def nxt(i):
        return jnp.minimum(i + 1, n_all - 1)

    def rows(col, f):
        return pl.BlockSpec((None, BLOCK_Q, LANES), lambda bi, i: (bi, f(i), col))

    def tab(f):
        return pl.BlockSpec((BLOCK_Q, LANES), lambda bi, i: (f(i), 0))

    same = lambda i: i
    return pl.pallas_call(
        functools.partial(_attn_kernel, n_ctx=n_ctx, n_lat=n_lat),
        grid=(b, n_all),
        in_specs=[pl.BlockSpec((None, BLOCK_Q, ATT_HEADS * HEAD_DIM), lambda bi, i: (bi, i, qcol)),
                  rows(kcol, prev), rows(kcol, same), rows(kcol, nxt),
                  rows(vcol, prev), rows(vcol, same), rows(vcol, nxt),
                  pl.BlockSpec((None, tc, LANES), lambda bi, i: (bi, 0, kcol)),
                  pl.BlockSpec((None, tc, LANES), lambda bi, i: (bi, 0, vcol)),
                  tab(same), tab(same), tab(prev), tab(prev), tab(nxt), tab(nxt),
                  pl.BlockSpec((1, LANES), lambda bi, i: (0, 0))],
        out_specs=pl.BlockSpec((None, BLOCK_Q, ATT_HEADS * HEAD_DIM), lambda bi, i: (bi, i, 0)),
        out_shape=jax.ShapeDtypeStruct((b, ttot, ATT_HEADS * HEAD_DIM), F32),
        compiler_params=_cparams(("parallel", "parallel")),
        name="window_gqa",
    )(z, z, z, z, z, z, z, z, z, cos_t, sin_t, cos_t, sin_t, cos_t, sin_t, sink)


def _outproj_ab_kernel(x_ref, mb_ref, mc_ref, lf_ref, lb_ref, g_ref, at_ref, w_ref, o_ref, *, tm, tc):
    i = pl.program_id(1)
    lru = (lf_ref[...] + lb_ref[...]) * _gelu(g_ref[...])
    mix = jnp.concatenate([lru.astype(BF16), at_ref[...].astype(BF16)], axis=1)
    dx = jnp.dot(mix, w_ref[...], preferred_element_type=F32)
    gate = _gate_mod(mb_ref[...], mc_ref[...], 2, i * tm, tc, tm)
    o_ref[...] = x_ref[...] + gate * dx


def _outproj_ab(xa, mod_l, lru_f, lru_b, z, att, w_out, tc, nb):
    b, ttot, d = xa.shape
    w = lru_f.shape[2]
    tm = _row_tile(ttot)
    tile = lambda width, col: pl.BlockSpec((None, tm, width), lambda bi, i: (bi, i, col))
    return pl.pallas_call(
        functools.partial(_outproj_ab_kernel, tm=tm, tc=tc),
        grid=(b, ttot // tm),
        in_specs=[tile(d, 0),
                  pl.BlockSpec((None, N_MOD, d), lambda bi, i: (bi, 0, 0)),
                  pl.BlockSpec((None, N_MOD, d), lambda bi, i: (nb, 0, 0)),
                  tile(w, 0), tile(w, 0), tile(w, 0), tile(att.shape[2], 0),
                  pl.BlockSpec(w_out.shape, lambda bi, i: (0, 0))],
        out_specs=tile(d, 0),
        out_shape=jax.ShapeDtypeStruct(xa.shape, F32),
        compiler_params=_cparams(("parallel", "parallel")),
        name="outproj_ab",
    )(xa, mod_l, mod_l, lru_f, lru_b, z, att, w_out)


def _outproj_cd_kernel(x_ref, mb_ref, mc_ref, y_ref, of_ref, ob_ref, g_ref, wg_ref, bg_ref, gn_ref,
                       w_ref, o_ref, *, tm, tc):
    i = pl.program_id(1)
    y = _gelu(y_ref[...])
    s5 = y * _sigmoid(jnp.dot(y.astype(BF16), wg_ref[...], preferred_element_type=F32) + bg_ref[...])
    o = of_ref[...] + ob_ref[...]
    parts = []
    for h in range(HG_HEADS):
        oh = o[:, h * HG_DK:(h + 1) * HG_DK]
        ms = jnp.mean(oh * oh, axis=-1, keepdims=True)
        parts.append(oh * lax.rsqrt(ms + EPS) * gn_ref[...])
    hg = jnp.concatenate(parts, axis=1) * _silu(g_ref[...])
    mix = jnp.concatenate([s5.astype(BF16), hg.astype(BF16)], axis=1)
    dx = jnp.dot(mix, w_ref[...], preferred_element_type=F32)
    gate = _gate_mod(mb_ref[...], mc_ref[...], 2, i * tm, tc, tm)
    o_ref[...] = x_ref[...] + gate * dx


def _outproj_cd(xa, mod_l, y5, o_f, o_b, z, w_glu, b_glu, hg_norm, w_out, tc, nb):
    b, ttot, d = xa.shape
    w = y5.shape[2]
    tm = _row_tile(ttot)
    gcol = z.shape[2] // w - 1
    tile = lambda width, col: pl.BlockSpec((None, tm, width), lambda bi, i: (bi, i, col))
    return pl.pallas_call(
        functools.partial(_outproj_cd_kernel, tm=tm, tc=tc),
        grid=(b, ttot // tm),
        in_specs=[tile(d, 0),
                  pl.BlockSpec((None, N_MOD, d), lambda bi, i: (bi, 0, 0)),
                  pl.BlockSpec((None, N_MOD, d), lambda bi, i: (nb, 0, 0)),
                  tile(w, 0), tile(w, 0), tile(w, 0), tile(w, gcol),
                  pl.BlockSpec(w_glu.shape, lambda bi, i: (0, 0)),
                  pl.BlockSpec((1, w), lambda bi, i: (0, 0)),
                  pl.BlockSpec((1, HG_DK), lambda bi, i: (0, 0)),
                  pl.BlockSpec(w_out.shape, lambda bi, i: (0, 0))],
        out_specs=tile(d, 0),
        out_shape=jax.ShapeDtypeStruct(xa.shape, F32),
        compiler_params=_cparams(("parallel", "parallel")),
        name="outproj_cd",
    )(xa, mod_l, mod_l, y5, o_f, o_b, z, w_glu, b_glu.reshape(1, w), hg_norm.reshape(1, HG_DK), w_out)


def _ffn_kernel(x_ref, mb_ref, mc_ref, g_ref, w1_ref, w3_ref, w2_ref, o_ref, h_ref, acc_ref, *, tm, tc):
    i = pl.program_id(1)
    k = pl.program_id(2)

    @pl.when(k == 0)
    def _():
        h = _norm_mod(x_ref[...], g_ref[...], mb_ref[...], mc_ref[...], i * tm, tc, 3, 4)
        h_ref[...] = h.astype(BF16)
        acc_ref[...] = jnp.zeros_like(acc_ref)

    h = h_ref[...]
    a = jnp.dot(h, w1_ref[...], preferred_element_type=F32)
    c = jnp.dot(h, w3_ref[...], preferred_element_type=F32)
    acc_ref[...] += jnp.dot((_silu(a) * c).astype(BF16), w2_ref[...], preferred_element_type=F32)

    @pl.when(k == pl.num_programs(2) - 1)
    def _():
        gate = _gate_mod(mb_ref[...], mc_ref[...], 5, i * tm, tc, tm)
        o_ref[...] = x_ref[...] + gate * acc_ref[...]


def _ffn(xa, mod_l, g, w1, w3, w2, tc, nb):
    b, ttot, d = xa.shape
    f = w1.shape[1]
    tm = _row_tile(ttot)
    tf = f // 2 if (f // 2) % LANES == 0 else f
    return pl.pallas_call(
        functools.partial(_ffn_kernel, tm=tm, tc=tc),
        grid=(b, ttot // tm, f // tf),
        in_specs=[pl.BlockSpec((None, tm, d), lambda bi, i, k: (bi, i, 0)),
                  pl.BlockSpec((None, N_MOD, d), lambda bi, i, k: (bi, 0, 0)),
                  pl.BlockSpec((None, N_MOD, d), lambda bi, i, k: (nb, 0, 0)),
                  pl.BlockSpec((1, d), lambda bi, i, k: (0, 0)),
                  pl.BlockSpec((d, tf), lambda bi, i, k: (0, k)),
                  pl.BlockSpec((d, tf), lambda bi, i, k: (0, k)),
                  pl.BlockSpec((tf, d), lambda bi, i, k: (k, 0))],
        out_specs=pl.BlockSpec((None, tm, d), lambda bi, i, k: (bi, i, 0)),
        out_shape=jax.ShapeDtypeStruct(xa.shape, F32),
        scratch_shapes=[pltpu.VMEM((tm, d), BF16), pltpu.VMEM((tm, d), F32)],
        compiler_params=_cparams(("parallel", "parallel", "arbitrary")),
        name="ffn_swiglu",
    )(xa, mod_l, mod_l, g.reshape(1, d), w1, w3, w2)


def _moe_kernel(x_ref, mb_ref, mc_ref, g_ref, r_ref, w1_ref, w3_ref, w2_ref, o_ref,
                h_ref, wt_ref, acc_ref, *, tm, tc, n_exp):
    i = pl.program_id(1)
    e = pl.program_id(2)
    lane = lax.broadcasted_iota(jnp.int32, (tm, LANES), 1)

    @pl.when(e == 0)
    def _():
        h = _norm_mod(x_ref[...], g_ref[...], mb_ref[...], mc_ref[...], i * tm, tc, 3, 4)
        h_ref[...] = h.astype(BF16)
        acc_ref[...] = jnp.zeros_like(acc_ref)
        logits = jnp.dot(h, r_ref[...], preferred_element_type=F32, precision=lax.Precision.HIGHEST)
        logits = jnp.where(lane < n_exp, logits, NEG)
        m1 = jnp.max(logits, axis=-1, keepdims=True)
        i1 = jnp.min(jnp.where(logits == m1, lane, LANES), axis=-1, keepdims=True)
        rest = jnp.where(lane == i1, NEG, logits)
        m2 = jnp.max(rest, axis=-1, keepdims=True)
        i2 = jnp.min(jnp.where(rest == m2, lane, LANES), axis=-1, keepdims=True)
        e2 = jnp.exp(m2 - m1)
        g1 = 1.0 / (1.0 + e2)
        wt_ref[...] = jnp.where(lane == i1, g1, 0.0) + jnp.where(lane == i2, e2 * g1, 0.0)

    h = h_ref[...]
    a = jnp.dot(h, w1_ref[...], preferred_element_type=F32)
    c = jnp.dot(h, w3_ref[...], preferred_element_type=F32)
    y = jnp.dot((_silu(a) * c).astype(BF16), w2_ref[...], preferred_element_type=F32)
    w_e = jnp.sum(jnp.where(lane == e, wt_ref[...], 0.0), axis=-1, keepdims=True)
    acc_ref[...] += w_e * y

    @pl.when(e == n_exp - 1)
    def _():
        gate = _gate_mod(mb_ref[...], mc_ref[...], 5, i * tm, tc, tm)
        o_ref[...] = x_ref[...] + gate * acc_ref[...]


def _moe(xa, mod_l, g, router_p, w1, w3, w2, tc, nb):
    b, ttot, d = xa.shape
    n_exp, _, f = w1.shape
    tm = _row_tile(ttot)
    return pl.pallas_call(
        functools.partial(_moe_kernel, tm=tm, tc=tc, n_exp=n_exp),
        grid=(b, ttot // tm, n_exp),
        in_specs=[pl.BlockSpec((None, tm, d), lambda bi, i, e: (bi, i, 0)),
                  pl.BlockSpec((None, N_MOD, d), lambda bi, i, e: (bi, 0, 0)),
                  pl.BlockSpec((None, N_MOD, d), lambda bi, i, e: (nb, 0, 0)),
                  pl.BlockSpec((1, d), lambda bi, i, e: (0, 0)),
                  pl.BlockSpec((d, LANES), lambda bi, i, e: (0, 0)),
                  pl.BlockSpec((None, d, f), lambda bi, i, e: (e, 0, 0)),
                  pl.BlockSpec((None, d, f), lambda bi, i, e: (e, 0, 0)),
                  pl.BlockSpec((None, f, d), lambda bi, i, e: (e, 0, 0))],
        out_specs=pl.BlockSpec((None, tm, d), lambda bi, i, e: (bi, i, 0)),
        out_shape=jax.ShapeDtypeStruct(xa.shape, F32),
        scratch_shapes=[pltpu.VMEM((tm, d), BF16), pltpu.VMEM((tm, LANES), F32),
                        pltpu.VMEM((tm, d), F32)],
        compiler_params=_cparams(("parallel", "parallel", "arbitrary")),
        name="moe_swiglu",
    )(xa, mod_l, mod_l, g.reshape(1, d), router_p, w1, w3, w2)


def _s5_weights(a_re, a_im, log_step, b_re, b_im, c_re, c_im):
    L = S5_CHUNK
    lr = jnp.minimum(a_re, -1e-4)
    li = a_im
    dt = jnp.exp(log_step)[..., None]
    mag, ang = lr * dt, li * dt
    lbr, lbi = jnp.exp(mag) * jnp.cos(ang), jnp.exp(mag) * jnp.sin(ang)
    zr, zi = lbr - 1.0, lbi
    den = lr * lr + li * li
    fr = (zr * lr + zi * li) / den
    fi = (zi * lr - zr * li) / den
    bbr = fr[..., None] * b_re - fi[..., None] * b_im
    bbi = fr[..., None] * b_im + fi[..., None] * b_re

    def power(p):
        p = p[..., None, None, None].astype(F32)
        return jnp.exp(mag * p) * jnp.cos(ang * p), jnp.exp(mag * p) * jnp.sin(ang * p)

    n_dir, n_grp, n_st = a_re.shape
    gpb = LANES // S5_GROUP
    n_blk = n_grp // gpb
    s = jnp.arange(L)
    outs = []
    for d in range(n_dir):
        pr, pi = power(jnp.arange(L))
        pr, pi = pr[:, d], pi[:, d]
        cbr = (jnp.einsum('gcn,tgn,gnk->tgck', c_re[d], pr, bbr[d])
               - jnp.einsum('gcn,tgn,gnk->tgck', c_re[d], pi, bbi[d])
               - jnp.einsum('gcn,tgn,gnk->tgck', c_im[d], pr, bbi[d])
               - jnp.einsum('gcn,tgn,gnk->tgck', c_im[d], pi, bbr[d]))
        lag = (s[None, :] - s[:, None]) if d == 0 else (s[:, None] - s[None, :])
        kern = jnp.where((lag >= 0)[..., None, None, None], cbr[jnp.clip(lag, 0, L - 1)], 0.0)
        eye = jnp.eye(gpb, dtype=F32)
        kern = kern.reshape(L, L, n_blk, gpb, S5_GROUP, S5_GROUP)
        m = jnp.einsum('stbgck,gh->bsgkthc', kern, eye).reshape(n_blk, L * LANES, L * LANES)
        qr, qi = power((L - 1 - s) if d == 0 else s)
        qr, qi = qr[:, d], qi[:, d]
        str_ = qr[..., None] * bbr[d] - qi[..., None] * bbi[d]
        sti = qr[..., None] * bbi[d] + qi[..., None] * bbr[d]

        def to_state(x):
            x = x.reshape(L, n_blk, gpb, n_st, S5_GROUP)
            return jnp.einsum('sbgnk,gh->bsgkhn', x, eye).reshape(n_blk, L * LANES, gpb * n_st)

        wst = jnp.concatenate([to_state(str_), to_state(sti)], axis=2)
        rr, ri = power((s + 1) if d == 0 else (L - s))
        rr, ri = rr[:, d], ri[:, d]
        wr = c_re[d][None] * rr[:, :, None, :] - c_im[d][None] * ri[:, :, None, :]
        wi = -(c_re[d][None] * ri[:, :, None, :] + c_im[d][None] * rr[:, :, None, :])

        def from_state(x):
            x = x.reshape(L, n_blk, gpb, S5_GROUP, n_st)
            return jnp.einsum('tbgcn,gh->bgnthc', x, eye).reshape(n_blk, gpb * n_st, L * LANES)

        wout = jnp.concatenate([from_state(wr), from_state(wi)], axis=1)
        ler, lei = jnp.exp(mag[d] * L) * jnp.cos(ang[d] * L), jnp.exp(mag[d] * L) * jnp.sin(ang[d] * L)
        lam_l = jnp.concatenate([ler.reshape(n_blk, 1, gpb * n_st), lei.reshape(n_blk, 1, gpb * n_st)],
                                axis=2)
        outs.append((m.astype(BF16), wst.astype(BF16), wout.astype(BF16), lam_l))
    return tuple(jnp.stack([o[k] for o in outs]) for k in range(4))


def _s5_kernel(u_ref, m_ref, wst_ref, wout_ref, lam_ref, d_ref, o_ref, x_ref, hp_ref, *, rows, rows_ctx):
    L = S5_CHUNK
    dr = pl.program_id(2)
    ns = lam_ref.shape[1] // 2
    u = jnp.concatenate([u_ref[pl.ds(s, rows, stride=L), :] for s in range(L)], axis=1)
    ub = u.astype(BF16)
    x_ref[...] = jnp.dot(ub, wst_ref[...], preferred_element_type=F32)
    lr = lam_ref[:, 0:ns]
    li = lam_ref[:, ns:2 * ns]

    def visit(r, carry):
        hr, hi = carry
        hp_ref[pl.ds(r, 1), 0:ns] = hr
        hp_ref[pl.ds(r, 1), ns:2 * ns] = hi
        xr = x_ref[pl.ds(r, 1), 0:ns]
        xi = x_ref[pl.ds(r, 1), ns:2 * ns]
        return lr * hr - li * hi + xr, lr * hi + li * hr + xi

    zero = (jnp.zeros((1, ns), F32), jnp.zeros((1, ns), F32))

    @pl.when(dr == 0)
    def _():
        lax.fori_loop(0, rows, visit, zero, unroll=4)

    @pl.when(dr == 1)
    def _():
        c = lax.fori_loop(0, rows_ctx, lambda t, c: visit(rows_ctx - 1 - t, c), zero, unroll=4)
        lax.fori_loop(0, rows - rows_ctx, lambda t, c: visit(rows - 1 - t, c), c, unroll=4)

    y = (jnp.dot(ub, m_ref[...], preferred_element_type=F32)
         + jnp.dot(hp_ref[...].astype(BF16), wout_ref[...], preferred_element_type=F32))

    @pl.when(dr == 0)
    def _():
        for s in range(L):
            o_ref[pl.ds(s, rows, stride=L), :] = (y[:, s * LANES:(s + 1) * LANES]
                                                  + d_ref[...] * u[:, s * LANES:(s + 1) * LANES])

    @pl.when(dr == 1)
    def _():
        for s in range(L):
            o_ref[pl.ds(s, rows, stride=L), :] += y[:, s * LANES:(s + 1) * LANES]


def _s5(z, weights, dvec, tc, width):
    b, ttot, _ = z.shape
    m, wst, wout, lam_l = weights
    n_dir, n_blk = m.shape[0], m.shape[1]
    rows, rows_ctx = ttot // S5_CHUNK, tc // S5_CHUNK
    wspec = lambda a: pl.BlockSpec((None, None) + a.shape[2:], lambda bi, j, dr: (dr, j, 0, 0))
    return pl.pallas_call(
        functools.partial(_s5_kernel, rows=rows, rows_ctx=rows_ctx),
        grid=(b, n_blk, n_dir),
        in_specs=[pl.BlockSpec((None, ttot, LANES), lambda bi, j, dr: (bi, 0, j)),
                  wspec(m), wspec(wst), wspec(wout), wspec(lam_l),
                  pl.BlockSpec((1, LANES), lambda bi, j, dr: (0, j))],
        out_specs=pl.BlockSpec((None, ttot, LANES), lambda bi, j, dr: (bi, 0, j)),
        out_shape=jax.ShapeDtypeStruct((b, ttot, width), F32),
        scratch_shapes=[pltpu.VMEM((rows, lam_l.shape[3]), F32), pltpu.VMEM((rows, lam_l.shape[3]), F32)],
        compiler_params=_cparams(("parallel", "parallel", "arbitrary")),
        name="s5_bidir",
    )(z, m, wst, wout, lam_l, dvec.reshape(1, width))


def _gla_kernel(q_ref, f_ref, v_ref, lb_ref, o_ref, st_ref, *, lc, reverse):
    step = pl.program_id(1)

    @pl.when(step == 0)
    def _():
        st_ref[...] = jnp.zeros_like(st_ref)

    c = HG_CHUNK
    r_i = lax.broadcasted_iota(jnp.int32, (c, c), 0)
    c_i = lax.broadcasted_iota(jnp.int32, (c, c), 1)
    keep = (c_i >= r_i) if reverse else (c_i <= r_i)
    tri = jnp.where(keep, 1.0, 0.0).astype(BF16)
    lb = lb_ref[...]
    n_chunks = lc // c
    order = range(n_chunks - 1, -1, -1) if reverse else range(n_chunks)
    for ci in order:
        rows = pl.ds(ci * c, c)
        qraw = q_ref[rows, :]
        q = _silu(qraw)
        f = lb + (1.0 - lb) * _sigmoid(f_ref[rows, :])
        k = 1.0 - f
        logf = jnp.log(f)
        v = v_ref[rows, :].astype(BF16)
        hi = logf.astype(BF16)
        r1 = logf - hi.astype(F32)
        mid = r1.astype(BF16)
        lo = (r1 - mid.astype(F32)).astype(BF16)
        cum = (jnp.dot(tri, hi, preferred_element_type=F32) + jnp.dot(tri, mid, preferred_element_type=F32)
               + jnp.dot(tri, lo, preferred_element_type=F32))
        total = cum[0:1, :] if reverse else cum[c - 1:c, :]
        mid_ref = cum[c // 2:c // 2 + 1, :]
        q_in = (q * jnp.exp(cum)).astype(BF16)
        k_out = (k * jnp.exp(total - cum)).astype(BF16)
        q_loc = (q * jnp.exp(cum - mid_ref)).astype(BF16)
        k_loc = (k * jnp.exp(mid_ref - cum)).astype(BF16)
        decay = jnp.exp(total)
        for h in range(HG_HEADS):
            sl = slice(h * HG_DK, (h + 1) * HG_DK)
            st = st_ref[h]
            o = lax.dot_general(q_in[:, sl], st.astype(BF16), (((1,), (1,)), ((), ())),
                                preferred_element_type=F32)
            att = lax.dot_general(q_loc[:, sl], k_loc[:, sl], (((1,), (1,)), ((), ())),
                                  preferred_element_type=F32)
            att = jnp.where(keep, att, 0.0)
            o = o + jnp.dot(att.astype(BF16), v[:, sl], preferred_element_type=F32)
            o_ref[rows, sl] = o
            st_ref[h] = st * decay[:, sl] + lax.dot_general(
                v[:, sl], k_out[:, sl], (((0,), (0,)), ((), ())), preferred_element_type=F32)


def _gla(z, lb, tc, fcol, reverse):
    b, ttot, _ = z.shape
    w = HG_HEADS * HG_DK
    lc = SEQ_BLOCK
    n_all, n_ctx = ttot // lc, tc // lc

    def blk(s):
        return _seq_block(s, n_ctx, n_all, reverse)

    col = lambda cidx: pl.BlockSpec((None, lc, w), lambda bi, s: (bi, blk(s), cidx))
    return pl.pallas_call(
        functools.partial(_gla_kernel, lc=lc, reverse=reverse),
        grid=(b, n_all),
        in_specs=[col(1), col(fcol), col(4), pl.BlockSpec((1, w), lambda bi, s: (0, 0))],
        out_specs=pl.BlockSpec((None, lc, w), lambda bi, s: (bi, blk(s), 0)),
        out_shape=jax.ShapeDtypeStruct((b, ttot, w), F32),
        scratch_shapes=[pltpu.VMEM((HG_HEADS, HG_DK, HG_DK), F32)],
        compiler_params=_cparams(("parallel", "arbitrary")),
        name="hgrn2_rev" if reverse else "hgrn2_fwd",
    )(z, z, z, lb.reshape(1, w))


def _final_kernel(x_ref, g_ref, o_ref):
    x = x_ref[...]
    ms = jnp.mean(x * x, axis=-1, keepdims=True)
    o_ref[...] = x * lax.rsqrt(ms + EPS) * g_ref[...]


def _final_norm(xa, g, tc):
    b, ttot, d = xa.shape
    t = ttot - tc
    tm = SEQ_BLOCK
    off = tc // tm
    return pl.pallas_call(
        _final_kernel,
        grid=(b, t // tm),
        in_specs=[pl.BlockSpec((None, tm, d), lambda bi, i: (bi, i + off, 0)),
                  pl.BlockSpec((1, d), lambda bi, i: (0, 0))],
        out_specs=pl.BlockSpec((None, tm, d), lambda bi, i: (bi, i, 0)),
        out_shape=jax.ShapeDtypeStruct((b, t, d), F32),
        compiler_params=_cparams(("parallel", "parallel")),
        name="final_norm",
    )(xa, g.reshape(1, d))


def _rope_tables(t, tc):
    pos = jnp.arange(t)
    row = (pos // GRID_W).astype(F32)
    col = (pos % GRID_W).astype(F32)
    inv = ROPE_BASE ** (-jnp.arange(ROPE_FREQS, dtype=F32) / ROPE_FREQS)
    ar, ac = row[:, None] * inv, col[:, None] * inv
    cos = jnp.concatenate([jnp.cos(ar), jnp.cos(ar), jnp.cos(ac), jnp.cos(ac)], axis=1)
    sin = jnp.concatenate([-jnp.sin(ar), jnp.sin(ar), -jnp.sin(ac), jnp.sin(ac)], axis=1)
    cos = jnp.concatenate([jnp.ones((tc, HEAD_DIM), F32), cos], axis=0)
    sin = jnp.concatenate([jnp.zeros((tc, HEAD_DIM), F32), sin], axis=0)
    return jnp.tile(cos, (1, LANES // HEAD_DIM)), jnp.tile(sin, (1, LANES // HEAD_DIM))


def _block_diag_dense(w):
    nblk, h, k = w.shape
    return jnp.einsum('nhk,nm->nhmk', w, jnp.eye(nblk, dtype=w.dtype)).reshape(nblk * h, nblk * k)


def kernel(x, c, ctx, c_ctx, w_mod, b_mod, norm_mix, norm_ffn, final_norm, w_in_ab, lru_conv_w, lru_conv_b, lru_wa, lru_ba, lru_wx, lru_bx, lru_lam, attn_sink, w_out_ab, ffn_w1, ffn_w3, ffn_w2, w_in_cd, s5_a_re, s5_a_im, s5_log_step, s5_b_re, s5_b_im, s5_c_re, s5_c_im, s5_d, s5_w_glu, s5_b_glu, hg_lb_raw, hg_norm, w_out_cd, moe_router, moe_w1, moe_w3, moe_w2):
    nb, t, d = x.shape
    tc = ctx.shape[1]
    depth = w_mod.shape[0]
    assert tc % SEQ_BLOCK == 0 and t % SEQ_BLOCK == 0 and t % GRID_W == 0

    xa = jnp.concatenate([ctx, x], axis=1)
    mod_rows = -(-(nb + 1) // SUBLANES) * SUBLANES
    cvec = jnp.zeros((mod_rows, d), F32).at[:nb].set(c).at[nb].set(c_ctx)
    mod = _modulation(cvec, w_mod, b_mod).reshape(depth, mod_rows, N_MOD, d)

    cos_t, sin_t = _rope_tables(t, tc)
    lb_soft = jax.nn.softmax(hg_lb_raw.astype(F32), axis=0)
    lb_table = jnp.cumsum(lb_soft, axis=0) - lb_soft[0:1]
    n_exp = moe_router.shape[2]

    for l in range(depth):
        j = l // 2
        mod_l = mod[l]
        if l % 2 == 0:
            z = _inproj(xa, mod_l, norm_mix[l], w_in_ab[j].astype(BF16), tc, nb)
            sp = jax.nn.softplus(-lru_lam[j])
            lru = []
            for dr, rev in enumerate((False, True)):
                lru.append(_lru(z, lru_conv_w[j], lru_conv_b[j],
                                _block_diag_dense(lru_wa[j, dr]).astype(BF16), lru_ba[j, dr],
                                _block_diag_dense(lru_wx[j, dr]).astype(BF16), lru_bx[j, dr],
                                sp[dr], tc, rev))
            sink = jnp.zeros((1, LANES), F32).at[0, :ATT_HEADS].set(attn_sink[j])
            att = _attention(z, cos_t, sin_t, sink, tc)
            xa = _outproj_ab(xa, mod_l, lru[0], lru[1], z, att, w_out_ab[j].astype(BF16), tc, nb)
            xa = _ffn(xa, mod_l, norm_ffn[l], ffn_w1[j].astype(BF16), ffn_w3[j].astype(BF16),
                      ffn_w2[j].astype(BF16), tc, nb)
        else:
            z = _inproj(xa, mod_l, norm_mix[l], w_in_cd[j].astype(BF16), tc, nb)
            s5w = _s5_weights(s5_a_re[j], s5_a_im[j], s5_log_step[j], s5_b_re[j], s5_b_im[j],
                              s5_c_re[j], s5_c_im[j])
            y5 = _s5(z, s5w, s5_d[j], tc, s5_d.shape[1])
            o_f = _gla(z, lb_table[j], tc, 2, False)
            o_b = _gla(z, lb_table[j], tc, 3, True)
            xa = _outproj_cd(xa, mod_l, y5, o_f, o_b, z, s5_w_glu[j].astype(BF16), s5_b_glu[j],
                             hg_norm[j], w_out_cd[j].astype(BF16), tc, nb)
            router_p = jnp.zeros((d, LANES), F32).at[:, :n_exp].set(moe_router[j])
            xa = _moe(xa, mod_l, norm_ffn[l], router_p, moe_w1[j].astype(BF16),
                      moe_w3[j].astype(BF16), moe_w2[j].astype(BF16), tc, nb)
    return _final_norm(xa, final_norm, tc)
```

```python
import functools
import math

import jax
import jax.numpy as jnp
from jax import lax
from jax.experimental import pallas as pl
from jax.experimental.pallas import tpu as pltpu

F32 = jnp.float32
BF16 = jnp.bfloat16

EPS = 1e-6
GRID_W = 64
LRU_BLOCKS = 8
LRU_C = 8.0
CONV_W = 4
ATT_HEADS = 8
ATT_KV_HEADS = 2
ATT_GROUP = ATT_HEADS // ATT_KV_HEADS
HEAD_DIM = 64
WINDOW = 128
BLOCK_Q = 128
ROPE_FREQS = HEAD_DIM // 4
ROPE_BASE = 10000.0
S5_GROUP = 16
S5_STATE = 64
S5_CHUNK = 8
HG_HEADS = 4
HG_DK = 128
HG_CHUNK = 64
SEQ_BLOCK = 256
N_MOD = 6
MOE_CAP = 256
LANES = 128
SUBLANES = 8
VMEM_LIMIT = 56 * 1024 * 1024
NEG = -1e30
LOG2E = math.log2(math.e)


def _cparams(sem):
    return pltpu.CompilerParams(dimension_semantics=sem, vmem_limit_bytes=VMEM_LIMIT)


def _row_tile(ttot):
    for tm in (768, 1024, 512, 256):
        if ttot % tm == 0:
            return tm
    raise ValueError(f"unsupported token count {ttot}")


def _sigmoid(x):
    return 1.0 / (1.0 + jnp.exp(-x))


def _silu(x):
    return x * _sigmoid(x)


def _gelu(x):
    return 0.5 * x * (1.0 + jnp.tanh(math.sqrt(2.0 / math.pi) * (x + 0.044715 * (x * x * x))))


def _norm_mod(x, g, mb, mc, row0, tc, shift_idx, scale_idx):
    ms = jnp.mean(x * x, axis=-1, keepdims=True)
    y = x * lax.rsqrt(ms + EPS) * g
    rows = row0 + lax.broadcasted_iota(jnp.int32, (x.shape[0], 1), 0)
    is_ctx = rows < tc
    scale = jnp.where(is_ctx, mc[scale_idx:scale_idx + 1], mb[scale_idx:scale_idx + 1])
    shift = jnp.where(is_ctx, mc[shift_idx:shift_idx + 1], mb[shift_idx:shift_idx + 1])
    return y * (1.0 + scale) + shift


def _gate_mod(mb, mc, idx, row0, tc, n):
    rows = row0 + lax.broadcasted_iota(jnp.int32, (n, 1), 0)
    return jnp.where(rows < tc, mc[idx:idx + 1], mb[idx:idx + 1])


def _mod_kernel(c_ref, w_ref, b_ref, o_ref):
    s = _silu(c_ref[...])
    o_ref[...] = jnp.dot(s, w_ref[...], preferred_element_type=F32,
                         precision=lax.Precision.HIGHEST) + b_ref[...]


def _modulation(cvec, w_mod, b_mod):
    depth, d, n = w_mod.shape
    tn = 1536 if n % 1536 == 0 else n
    rows = cvec.shape[0]
    return pl.pallas_call(
        _mod_kernel,
        grid=(depth, n // tn),
        in_specs=[pl.BlockSpec((rows, d), lambda l, j: (0, 0)),
                  pl.BlockSpec((None, d, tn), lambda l, j: (l, 0, j)),
                  pl.BlockSpec((None, 1, tn), lambda l, j: (l, 0, j))],
        out_specs=pl.BlockSpec((None, rows, tn), lambda l, j: (l, 0, j)),
        out_shape=jax.ShapeDtypeStruct((depth, rows, n), F32),
        compiler_params=_cparams(("arbitrary", "arbitrary")),
        name="modulation",
    )(cvec, w_mod, b_mod.reshape(depth, 1, n))


def _inproj_kernel(x_ref, mb_ref, mc_ref, g_ref, w_ref, o_ref, *, tm, tc):
    i = pl.program_id(1)
    h = _norm_mod(x_ref[...], g_ref[...], mb_ref[...], mc_ref[...], i * tm, tc, 0, 1)
    o_ref[...] = jnp.dot(h.astype(BF16), w_ref[...], preferred_element_type=F32)


def _inproj(xa, mod_l, g, w, tc, nb):
    b, ttot, d = xa.shape
    n = w.shape[1]
    tm = _row_tile(ttot)
    return pl.pallas_call(
        functools.partial(_inproj_kernel, tm=tm, tc=tc),
        grid=(b, ttot // tm),
        in_specs=[pl.BlockSpec((None, tm, d), lambda bi, i: (bi, i, 0)),
                  pl.BlockSpec((None, N_MOD, d), lambda bi, i: (bi, 0, 0)),
                  pl.BlockSpec((None, N_MOD, d), lambda bi, i: (nb, 0, 0)),
                  pl.BlockSpec((1, d), lambda bi, i: (0, 0)),
                  pl.BlockSpec((d, n), lambda bi, i: (0, 0))],
        out_specs=pl.BlockSpec((None, tm, n), lambda bi, i: (bi, i, 0)),
        out_shape=jax.ShapeDtypeStruct((b, ttot, n), F32),
        compiler_params=_cparams(("parallel", "parallel")),
        name="inproj",
    )(xa, mod_l, mod_l, g.reshape(1, d), w)


def _seq_block(step, n_ctx, n_all, reverse):
    if not reverse:
        return step
    return jnp.where(step < n_ctx, n_ctx - 1 - step, n_all - 1 - step + n_ctx)


def _lru_kernel(u_ref, hp_ref, hn_ref, cw_ref, cb_ref, wa_ref, ba_ref, wx_ref, bx_ref, sp_ref,
                o_ref, uc_ref, a_ref, b_ref, h_ref, *, lc, n_ctx, n_all, reverse):
    step = pl.program_id(1)
    blk = _seq_block(step, n_ctx, n_all, reverse)
    has_prev = jnp.logical_and(blk != 0, blk != n_ctx)
    has_next = jnp.logical_and(blk != n_ctx - 1, blk != n_all - 1)

    @pl.when(step == 0)
    def _():
        h_ref[...] = jnp.zeros_like(h_ref)

    w0, w1, w2, w3 = (cw_ref[j:j + 1, :] for j in range(CONV_W))
    cb = cb_ref[...]
    u = u_ref[...]
    uc_ref[...] = (cb + pltpu.roll(u, 2, 0) * w0 + pltpu.roll(u, 1, 0) * w1 + u * w2
                   + pltpu.roll(u, lc - 1, 0) * w3)
    row = lax.broadcasted_iota(jnp.int32, (SUBLANES, 1), 0)
    hp = jnp.where(has_prev, hp_ref[...], 0.0)
    hn = jnp.where(has_next, hn_ref[...], 0.0)
    u0 = u_ref[0:SUBLANES, :]
    u1 = u_ref[SUBLANES:2 * SUBLANES, :]
    uc_ref[0:SUBLANES, :] = (
        cb + jnp.where(row < 2, pltpu.roll(hp, 2, 0), pltpu.roll(u0, 2, 0)) * w0
        + jnp.where(row < 1, pltpu.roll(hp, 1, 0), pltpu.roll(u0, 1, 0)) * w1 + u0 * w2
        + jnp.where(row < SUBLANES - 1, pltpu.roll(u0, SUBLANES - 1, 0),
                    pltpu.roll(u1, SUBLANES - 1, 0)) * w3)
    ul = u_ref[lc - SUBLANES:lc, :]
    um = u_ref[lc - 2 * SUBLANES:lc - SUBLANES, :]
    uc_ref[lc - SUBLANES:lc, :] = (
        cb + jnp.where(row < 2, pltpu.roll(um, 2, 0), pltpu.roll(ul, 2, 0)) * w0
        + jnp.where(row < 1, pltpu.roll(um, 1, 0), pltpu.roll(ul, 1, 0)) * w1 + ul * w2
        + jnp.where(row < SUBLANES - 1, pltpu.roll(ul, SUBLANES - 1, 0),
                    pltpu.roll(hn, SUBLANES - 1, 0)) * w3)

    uc = uc_ref[...]
    ub = uc.astype(BF16)
    r = _sigmoid(jnp.dot(ub, wa_ref[...], preferred_element_type=F32) + ba_ref[...])
    gi = _sigmoid(jnp.dot(ub, wx_ref[...], preferred_element_type=F32) + bx_ref[...])
    log_a = -LRU_C * r * sp_ref[...]
    a_ref[...] = jnp.exp(log_a)
    b_ref[...] = jnp.sqrt(1.0 - jnp.exp(2.0 * log_a)) * (gi * uc)

    def body(t, h):
        tt = (lc - 1 - t) if reverse else t
        h = a_ref[pl.ds(tt, 1), :] * h + b_ref[pl.ds(tt, 1), :]
        o_ref[pl.ds(tt, 1), :] = h
        return h

    h_ref[0:1, :] = lax.fori_loop(0, lc, body, h_ref[0:1, :], unroll=8)


def _lru(z, conv_w, conv_b, wa_bd, ba, wx_bd, bx, sp, tc, reverse):
    b, ttot, _ = z.shape
    w = conv_w.shape[1]
    lc = SEQ_BLOCK
    n_all, n_ctx = ttot // lc, tc // lc
    hb = lc // SUBLANES
    n_h = ttot // SUBLANES
    col = 1

    def blk(s):
        return _seq_block(s, n_ctx, n_all, reverse)

    vec = lambda: pl.BlockSpec((1, w), lambda bi, s: (0, 0))
    return pl.pallas_call(
        functools.partial(_lru_kernel, lc=lc, n_ctx=n_ctx, n_all=n_all, reverse=reverse),
        grid=(b, n_all),
        in_specs=[pl.BlockSpec((None, lc, w), lambda bi, s: (bi, blk(s), col)),
                  pl.BlockSpec((None, SUBLANES, w),
                               lambda bi, s: (bi, jnp.maximum(blk(s) * hb - 1, 0), col)),
                  pl.BlockSpec((None, SUBLANES, w),
                               lambda bi, s: (bi, jnp.minimum((blk(s) + 1) * hb, n_h - 1), col)),
                  pl.BlockSpec((CONV_W, w), lambda bi, s: (0, 0)),
                  vec(),
                  pl.BlockSpec((w, w), lambda bi, s: (0, 0)),
                  vec(),
                  pl.BlockSpec((w, w), lambda bi, s: (0, 0)),
                  vec(), vec()],
        out_specs=pl.BlockSpec((None, lc, w), lambda bi, s: (bi, blk(s), 0)),
        out_shape=jax.ShapeDtypeStruct((b, ttot, w), F32),
        scratch_shapes=[pltpu.VMEM((lc, w), F32), pltpu.VMEM((lc, w), F32),
                        pltpu.VMEM((lc, w), F32), pltpu.VMEM((SUBLANES, w), F32)],
        compiler_params=_cparams(("parallel", "arbitrary")),
        name="rglru_rev" if reverse else "rglru_fwd",
    )(z, z, z, conv_w, conv_b.reshape(1, w), wa_bd, ba.reshape(1, w), wx_bd, bx.reshape(1, w),
      sp.reshape(1, w))


def _rope(x, cos, sin_signed):
    lane = lax.broadcasted_iota(jnp.int32, x.shape, 1)
    swapped = jnp.where(lane % 32 < 16, pltpu.roll(x, LANES - 16, 1), pltpu.roll(x, 16, 1))
    return x * cos + swapped * sin_signed


def _pair_layout(x, x_sw, head):
    lo = lax.broadcasted_iota(jnp.int32, x.shape, 1) < HEAD_DIM
    first, second = (x, x_sw) if head == 0 else (x_sw, x)
    return jnp.concatenate([jnp.where(lo, first, 0.0), jnp.where(lo, 0.0, second)],
                           axis=0).astype(BF16)


def _fold_lanes(x, op):
    acc = x[:, 0:LANES]
    for j in range(1, x.shape[1] // LANES):
        acc = op(acc, x[:, j * LANES:(j + 1) * LANES])
    return acc


def _attend(q_block, keys, vals, bias, sink_row, o_ref):
    nk = keys.shape[0]
    keys_sw = pltpu.roll(keys, HEAD_DIM, 1)
    vals_sw = pltpu.roll(vals, HEAD_DIM, 1)
    lo = lax.broadcasted_iota(jnp.int32, (BLOCK_Q, LANES), 1) < HEAD_DIM
    pairs_per_kv = ATT_GROUP // 2
    for kv in range(ATT_KV_HEADS):
        kab = _pair_layout(keys, keys_sw, kv)
        vab = _pair_layout(vals, vals_sw, kv)
        for g in range(kv * pairs_per_kv, (kv + 1) * pairs_per_kv):
            s = lax.dot_general(q_block(g), kab, (((1,), (1,)), ((), ())),
                                preferred_element_type=F32)
            halves = []
            recips = []
            for half in range(2):
                sh = s[:, half * nk:(half + 1) * nk]
                if bias is not None:
                    sh = sh + bias
                sk = sink_row[:, 2 * g + half:2 * g + half + 1]
                m = jnp.maximum(jnp.max(_fold_lanes(sh, jnp.maximum), axis=-1, keepdims=True), sk)
                p = jnp.exp2(sh - m)
                recips.append(1.0 / (jnp.sum(_fold_lanes(p, jnp.add), axis=-1, keepdims=True)
                                     + jnp.exp2(sk - m)))
                halves.append(p.astype(BF16))
            o = jnp.dot(jnp.concatenate(halves, axis=1), vab, preferred_element_type=F32)
            o_ref[:, g * LANES:(g + 1) * LANES] = o * jnp.where(lo, recips[0], recips[1])


def _attn_kernel(q_ref, kp_ref, k0_ref, kn_ref, vp_ref, v0_ref, vn_ref, kc_ref, vc_ref,
                 cq_ref, sq_ref, cp_ref, sp_ref, cn_ref, sn_ref, sink_ref, o_ref, *, n_ctx, n_lat):
    i = pl.program_id(1)
    qscale = HEAD_DIM ** -0.5 * LOG2E
    sink_row = sink_ref[...] * LOG2E

    @pl.when(i < n_ctx)
    def _():
        def q_block(g):
            return (q_ref[:, g * LANES:(g + 1) * LANES] * qscale).astype(BF16)
        _attend(q_block, kc_ref[...], vc_ref[...], None, sink_row, o_ref)

    @pl.when(i >= n_ctx)
    def _():
        n = i - n_ctx
        cq, sq = cq_ref[...], sq_ref[...]

        def q_block(g):
            return (_rope(q_ref[:, g * LANES:(g + 1) * LANES], cq, sq) * qscale).astype(BF16)

        kp = _rope(kp_ref[...], cp_ref[...], sp_ref[...])
        k0 = _rope(k0_ref[...], cq, sq)
        kn = _rope(kn_ref[...], cn_ref[...], sn_ref[...])
        keys = jnp.concatenate([kp, k0, kn, kc_ref[...]], axis=0)
        vals = jnp.concatenate([vp_ref[...], v0_ref[...], vn_ref[...], vc_ref[...]], axis=0)
        nk = keys.shape[0]
        qpos = n * BLOCK_Q + lax.broadcasted_iota(jnp.int32, (BLOCK_Q, nk), 0)
        col = lax.broadcasted_iota(jnp.int32, (BLOCK_Q, nk), 1)
        kpos = (n - 1) * BLOCK_Q + col
        local = (jnp.abs(qpos - kpos) <= WINDOW) & (kpos >= 0) & (kpos < n_lat * BLOCK_Q)
        bias = jnp.where(local | (col >= 3 * BLOCK_Q), 0.0, NEG)
        _attend(q_block, keys, vals, bias, sink_row, o_ref)


def _attention(z, cos_t, sin_t, sink, tc):
    b, ttot, _ = z.shape
    n_all, n_ctx = ttot // BLOCK_Q, tc // BLOCK_Q
    n_lat = n_all - n_ctx
    kcol = (2 * 512 + ATT_HEADS * HEAD_DIM) // LANES
    vcol = kcol + 1
    qcol = 2 * 512 // (ATT_HEADS * HEAD_DIM)

    def prev(i):
        return jnp.maximum(i - 1, 0)

    def nxt(i):
        return jnp.minimum(i + 1, n_all - 1)

    def rows(col, f):
        return pl.BlockSpec((None, BLOCK_Q, LANES), lambda bi, i: (bi, f(i), col))

    def tab(f):
        return pl.BlockSpec((BLOCK_Q, LANES), lambda bi, i: (f(i), 0))

    same = lambda i: i
    return pl.pallas_call(
        functools.partial(_attn_kernel, n_ctx=n_ctx, n_lat=n_lat),
        grid=(b, n_all),
        in_specs=[pl.BlockSpec((None, BLOCK_Q, ATT_HEADS * HEAD_DIM), lambda bi, i: (bi, i, qcol)),
                  rows(kcol, prev), rows(kcol, same), rows(kcol, nxt),
                  rows(vcol, prev), rows(vcol, same), rows(vcol, nxt),
                  pl.BlockSpec((None, tc, LANES), lambda bi, i: (bi, 0, kcol)),
                  pl.BlockSpec((None, tc, LANES), lambda bi, i: (bi, 0, vcol)),
                  tab(same), tab(same), tab(prev), tab(prev), tab(nxt), tab(nxt),
                  pl.BlockSpec((1, LANES), lambda bi, i: (0, 0))],
        out_specs=pl.BlockSpec((None, BLOCK_Q, ATT_HEADS * HEAD_DIM), lambda bi, i: (bi, i, 0)),
        out_shape=jax.ShapeDtypeStruct((b, ttot, ATT_HEADS * HEAD_DIM), F32),
        compiler_params=_cparams(("parallel", "parallel")),
        name="window_gqa",
    )(z, z, z, z, z, z, z, z, z, cos_t, sin_t, cos_t, sin_t, cos_t, sin_t, sink)


def _outproj_ab_kernel(x_ref, mb_ref, mc_ref, lf_ref, lb_ref, g_ref, at_ref, w_ref, o_ref, *, tm, tc):
    i = pl.program_id(1)
    lru = (lf_ref[...] + lb_ref[...]) * _gelu(g_ref[...])
    mix = jnp.concatenate([lru.astype(BF16), at_ref[...].astype(BF16)], axis=1)
    dx = jnp.dot(mix, w_ref[...], preferred_element_type=F32)
    gate = _gate_mod(mb_ref[...], mc_ref[...], 2, i * tm, tc, tm)
    o_ref[...] = x_ref[...] + gate * dx


def _outproj_ab(xa, mod_l, lru_f, lru_b, z, att, w_out, tc, nb):
    b, ttot, d = xa.shape
    w = lru_f.shape[2]
    tm = _row_tile(ttot)
    tile = lambda width, col: pl.BlockSpec((None, tm, width), lambda bi, i: (bi, i, col))
    return pl.pallas_call(
        functools.partial(_outproj_ab_kernel, tm=tm, tc=tc),
        grid=(b, ttot // tm),
        in_specs=[tile(d, 0),
                  pl.BlockSpec((None, N_MOD, d), lambda bi, i: (bi, 0, 0)),
                  pl.BlockSpec((None, N_MOD, d), lambda bi, i: (nb, 0, 0)),
                  tile(w, 0), tile(w, 0), tile(w, 0), tile(att.shape[2], 0),
                  pl.BlockSpec(w_out.shape, lambda bi, i: (0, 0))],
        out_specs=tile(d, 0),
        out_shape=jax.ShapeDtypeStruct(xa.shape, F32),
        compiler_params=_cparams(("parallel", "parallel")),
        name="outproj_ab",
    )(xa, mod_l, mod_l, lru_f, lru_b, z, att, w_out)


def _outproj_cd_kernel(x_ref, mb_ref, mc_ref, y_ref, of_ref, ob_ref, g_ref, wg_ref, bg_ref, gn_ref,
                       w_ref, o_ref, *, tm, tc):
    i = pl.program_id(1)
    y = _gelu(y_ref[...])
    s5 = y * _sigmoid(jnp.dot(y.astype(BF16), wg_ref[...], preferred_element_type=F32) + bg_ref[...])
    o = of_ref[...] + ob_ref[...]
    parts = []
    for h in range(HG_HEADS):
        oh = o[:, h * HG_DK:(h + 1) * HG_DK]
        ms = jnp.mean(oh * oh, axis=-1, keepdims=True)
        parts.append(oh * lax.rsqrt(ms + EPS) * gn_ref[...])
    hg = jnp.concatenate(parts, axis=1) * _silu(g_ref[...])
    mix = jnp.concatenate([s5.astype(BF16), hg.astype(BF16)], axis=1)
    dx = jnp.dot(mix, w_ref[...], preferred_element_type=F32)
    gate = _gate_mod(mb_ref[...], mc_ref[...], 2, i * tm, tc, tm)
    o_ref[...] = x_ref[...] + gate * dx


def _outproj_cd(xa, mod_l, y5, o_f, o_b, z, w_glu, b_glu, hg_norm, w_out, tc, nb):
    b, ttot, d = xa.shape
    w = y5.shape[2]
    tm = _row_tile(ttot)
    gcol = z.shape[2] // w - 1
    tile = lambda width, col: pl.BlockSpec((None, tm, width), lambda bi, i: (bi, i, col))
    return pl.pallas_call(
        functools.partial(_outproj_cd_kernel, tm=tm, tc=tc),
        grid=(b, ttot // tm),
        in_specs=[tile(d, 0),
                  pl.BlockSpec((None, N_MOD, d), lambda bi, i: (bi, 0, 0)),
                  pl.BlockSpec((None, N_MOD, d), lambda bi, i: (nb, 0, 0)),
                  tile(w, 0), tile(w, 0), tile(w, 0), tile(w, gcol),
                  pl.BlockSpec(w_glu.shape, lambda bi, i: (0, 0)),
                  pl.BlockSpec((1, w), lambda bi, i: (0, 0)),
                  pl.BlockSpec((1, HG_DK), lambda bi, i: (0, 0)),
                  pl.BlockSpec(w_out.shape, lambda bi, i: (0, 0))],
        out_specs=tile(d, 0),
        out_shape=jax.ShapeDtypeStruct(xa.shape, F32),
        compiler_params=_cparams(("parallel", "parallel")),
        name="outproj_cd",
    )(xa, mod_l, mod_l, y5, o_f, o_b, z, w_glu, b_glu.reshape(1, w), hg_norm.reshape(1, HG_DK), w_out)


def _ffn_kernel(x_ref, mb_ref, mc_ref, g_ref, w1_ref, w3_ref, w2_ref, o_ref, h_ref, acc_ref, *, tm, tc):
    i = pl.program_id(1)
    k = pl.program_id(2)

    @pl.when(k == 0)
    def _():
        h = _norm_mod(x_ref[...], g_ref[...], mb_ref[...], mc_ref[...], i * tm, tc, 3, 4)
        h_ref[...] = h.astype(BF16)
        acc_ref[...] = jnp.zeros_like(acc_ref)

    h = h_ref[...]
    a = jnp.dot(h, w1_ref[...], preferred_element_type=F32)
    c = jnp.dot(h, w3_ref[...], preferred_element_type=F32)
    acc_ref[...] += jnp.dot((_silu(a) * c).astype(BF16), w2_ref[...], preferred_element_type=F32)

    @pl.when(k == pl.num_programs(2) - 1)
    def _():
        gate = _gate_mod(mb_ref[...], mc_ref[...], 5, i * tm, tc, tm)
        o_ref[...] = x_ref[...] + gate * acc_ref[...]


def _ffn(xa, mod_l, g, w1, w3, w2, tc, nb):
    b, ttot, d = xa.shape
    f = w1.shape[1]
    tm = _row_tile(ttot)
    tf = f // 2 if (f // 2) % LANES == 0 else f
    return pl.pallas_call(
        functools.partial(_ffn_kernel, tm=tm, tc=tc),
        grid=(b, ttot // tm, f // tf),
        in_specs=[pl.BlockSpec((None, tm, d), lambda bi, i, k: (bi, i, 0)),
                  pl.BlockSpec((None, N_MOD, d), lambda bi, i, k: (bi, 0, 0)),
                  pl.BlockSpec((None, N_MOD, d), lambda bi, i, k: (nb, 0, 0)),
                  pl.BlockSpec((1, d), lambda bi, i, k: (0, 0)),
                  pl.BlockSpec((d, tf), lambda bi, i, k: (0, k)),
                  pl.BlockSpec((d, tf), lambda bi, i, k: (0, k)),
                  pl.BlockSpec((tf, d), lambda bi, i, k: (k, 0))],
        out_specs=pl.BlockSpec((None, tm, d), lambda bi, i, k: (bi, i, 0)),
        out_shape=jax.ShapeDtypeStruct(xa.shape, F32),
        scratch_shapes=[pltpu.VMEM((tm, d), BF16), pltpu.VMEM((tm, d), F32)],
        compiler_params=_cparams(("parallel", "parallel", "arbitrary")),
        name="ffn_swiglu",
    )(xa, mod_l, mod_l, g.reshape(1, d), w1, w3, w2)


def _router_kernel(x_ref, mb_ref, mc_ref, g_ref, r_ref, h_ref, wt_ref, cnt_ref, *, tm, tc, n_exp):
    i = pl.program_id(1)
    h = _norm_mod(x_ref[...], g_ref[...], mb_ref[...], mc_ref[...], i * tm, tc, 3, 4)
    h_ref[...] = h.astype(BF16)
    logits = lax.dot_general(r_ref[...], h, (((1,), (1,)), ((), ())), preferred_element_type=F32,
                             precision=lax.Precision.HIGHEST)
    sub = lax.broadcasted_iota(jnp.int32, logits.shape, 0)
    logits = jnp.where(sub < n_exp, logits, NEG)
    m1 = jnp.max(logits, axis=0, keepdims=True)
    i1 = jnp.min(jnp.where(logits == m1, sub, SUBLANES), axis=0, keepdims=True)
    rest = jnp.where(sub == i1, NEG, logits)
    m2 = jnp.max(rest, axis=0, keepdims=True)
    i2 = jnp.min(jnp.where(rest == m2, sub, SUBLANES), axis=0, keepdims=True)
    e2 = jnp.exp(m2 - m1)
    g1 = 1.0 / (1.0 + e2)
    wt = jnp.where(sub == i1, g1, 0.0) + jnp.where(sub == i2, e2 * g1, 0.0)
    wt_ref[...] = wt
    cnt = jnp.sum(jnp.where(wt > 0.0, 1.0, 0.0), axis=1, keepdims=True)
    cnt_ref[...] = jnp.broadcast_to(cnt, cnt_ref.shape).astype(jnp.int32)


def _router(xa, mod_l, g, router_t, tc, nb, n_exp):
    b, ttot, d = xa.shape
    tm = _row_tile(ttot)
    n_t = ttot // tm
    return pl.pallas_call(
        functools.partial(_router_kernel, tm=tm, tc=tc, n_exp=n_exp),
        grid=(b, n_t),
        in_specs=[pl.BlockSpec((None, tm, d), lambda bi, i: (bi, i, 0)),
                  pl.BlockSpec((None, N_MOD, d), lambda bi, i: (bi, 0, 0)),
                  pl.BlockSpec((None, N_MOD, d), lambda bi, i: (nb, 0, 0)),
                  pl.BlockSpec((1, d), lambda bi, i: (0, 0)),
                  pl.BlockSpec((SUBLANES, d), lambda bi, i: (0, 0))],
        out_specs=[pl.BlockSpec((None, tm, d), lambda bi, i: (bi, i, 0)),
                   pl.BlockSpec((None, SUBLANES, tm), lambda bi, i: (bi, 0, i)),
                   pl.BlockSpec((None, None, SUBLANES, LANES), lambda bi, i: (bi, i, 0, 0))],
        out_shape=[jax.ShapeDtypeStruct((b, ttot, d), BF16),
                   jax.ShapeDtypeStruct((b, SUBLANES, ttot), F32),
                   jax.ShapeDtypeStruct((b, n_t, SUBLANES, LANES), jnp.int32)],
        compiler_params=_cparams(("parallel", "parallel")),
        name="moe_router",
    )(xa, mod_l, mod_l, g.reshape(1, d), router_t)


def _moe_kernel(cnt_ref, x_ref, mb_ref, mc_ref, h_ref, wt_ref, w1_ref, w3_ref, w2_ref, o_ref,
                rank_ref, acc_ref, *, tm, tc, n_exp, cap):
    bi = pl.program_id(0)
    i = pl.program_id(1)
    e = pl.program_id(2)

    @pl.when(e == 0)
    def _():
        acc_ref[...] = jnp.zeros_like(acc_ref)
        sel = wt_ref[...] > 0.0
        r_i = lax.broadcasted_iota(jnp.int32, (tm, tm), 0)
        c_i = lax.broadcasted_iota(jnp.int32, (tm, tm), 1)
        before = jnp.where(r_i < c_i, 1.0, 0.0).astype(BF16)
        rank = jnp.dot(jnp.where(sel, 1.0, 0.0).astype(BF16), before, preferred_element_type=F32)
        rank_ref[...] = jnp.where(sel, rank, -1.0)

    n_e = cnt_ref[(bi * pl.num_programs(1) + i) * n_exp + e]
    w_row = wt_ref[pl.ds(e, 1), :]
    r_row = rank_ref[pl.ds(e, 1), :]
    for blk in range(tm // cap):
        @pl.when(n_e > blk * cap)
        def _():
            slot = (lax.broadcasted_iota(jnp.int32, (cap, tm), 0) + blk * cap).astype(F32)
            hit = r_row == slot
            onehot = jnp.where(hit, 1.0, 0.0).astype(BF16)
            xe = jnp.dot(onehot, h_ref[...], preferred_element_type=F32).astype(BF16)
            gate = jnp.sum(jnp.where(hit, w_row, 0.0), axis=1, keepdims=True)
            a = jnp.dot(xe, w1_ref[...], preferred_element_type=F32)
            c = jnp.dot(xe, w3_ref[...], preferred_element_type=F32)
            y = jnp.dot((_silu(a) * c * gate).astype(BF16), w2_ref[...], preferred_element_type=F32)
            acc_ref[...] += lax.dot_general(onehot, y.astype(BF16), (((0,), (0,)), ((), ())),
                                            preferred_element_type=F32)

    @pl.when(e == n_exp - 1)
    def _():
        gate5 = _gate_mod(mb_ref[...], mc_ref[...], 5, i * tm, tc, tm)
        o_ref[...] = x_ref[...] + gate5 * acc_ref[...]


def _moe(xa, mod_l, g, router_t, w1, w3, w2, tc, nb):
    b, ttot, d = xa.shape
    n_exp, _, f = w1.shape
    tm = _row_tile(ttot)
    n_t = ttot // tm
    cap = MOE_CAP
    assert tm % cap == 0
    h, wt, cnt = _router(xa, mod_l, g, router_t, tc, nb, n_exp)
    cnt = cnt[:, :, :n_exp, 0].reshape(-1)
    grid_spec = pltpu.PrefetchScalarGridSpec(
        num_scalar_prefetch=1,
        grid=(b, n_t, n_exp),
        in_specs=[pl.BlockSpec((None, tm, d), lambda bi, i, e, c: (bi, i, 0)),
                  pl.BlockSpec((None, N_MOD, d), lambda bi, i, e, c: (bi, 0, 0)),
                  pl.BlockSpec((None, N_MOD, d), lambda bi, i, e, c: (nb, 0, 0)),
                  pl.BlockSpec((None, tm, d), lambda bi, i, e, c: (bi, i, 0)),
                  pl.BlockSpec((None, SUBLANES, tm), lambda bi, i, e, c: (bi, 0, i)),
                  pl.BlockSpec((None, d, f), lambda bi, i, e, c: (e, 0, 0)),
                  pl.BlockSpec((None, d, f), lambda bi, i, e, c: (e, 0, 0)),
                  pl.BlockSpec((None, f, d), lambda bi, i, e, c: (e, 0, 0))],
        out_specs=pl.BlockSpec((None, tm, d), lambda bi, i, e, c: (bi, i, 0)),
        scratch_shapes=[pltpu.VMEM((SUBLANES, tm), F32), pltpu.VMEM((tm, d), F32)])
    return pl.pallas_call(
        functools.partial(_moe_kernel, tm=tm, tc=tc, n_exp=n_exp, cap=cap),
        grid_spec=grid_spec,
        out_shape=jax.ShapeDtypeStruct(xa.shape, F32),
        compiler_params=_cparams(("parallel", "parallel", "arbitrary")),
        name="moe_experts",
    )(cnt, xa, mod_l, mod_l, h, wt, w1, w3, w2)


def _s5_weights(a_re, a_im, log_step, b_re, b_im, c_re, c_im):
    L = S5_CHUNK
    lr = jnp.minimum(a_re, -1e-4)
    li = a_im
    dt = jnp.exp(log_step)[..., None]
    mag, ang = lr * dt, li * dt
    lbr, lbi = jnp.exp(mag) * jnp.cos(ang), jnp.exp(mag) * jnp.sin(ang)
    zr, zi = lbr - 1.0, lbi
    den = lr * lr + li * li
    fr = (zr * lr + zi * li) / den
    fi = (zi * lr - zr * li) / den
    bbr = fr[..., None] * b_re - fi[..., None] * b_im
    bbi = fr[..., None] * b_im + fi[..., None] * b_re

    def power(p):
        p = p[..., None, None, None].astype(F32)
        return jnp.exp(mag * p) * jnp.cos(ang * p), jnp.exp(mag * p) * jnp.sin(ang * p)

    n_dir, n_grp, n_st = a_re.shape
    gpb = LANES // S5_GROUP
    n_blk = n_grp // gpb
    s = jnp.arange(L)
    outs = []
    for d in range(n_dir):
        pr, pi = power(jnp.arange(L))
        pr, pi = pr[:, d], pi[:, d]
        cbr = (jnp.einsum('gcn,tgn,gnk->tgck', c_re[d], pr, bbr[d])
               - jnp.einsum('gcn,tgn,gnk->tgck', c_re[d], pi, bbi[d])
               - jnp.einsum('gcn,tgn,gnk->tgck', c_im[d], pr, bbi[d])
               - jnp.einsum('gcn,tgn,gnk->tgck', c_im[d], pi, bbr[d]))
        lag = (s[None, :] - s[:, None]) if d == 0 else (s[:, None] - s[None, :])
        kern = jnp.where((lag >= 0)[..., None, None, None], cbr[jnp.clip(lag, 0, L - 1)], 0.0)
        eye = jnp.eye(gpb, dtype=F32)
        kern = kern.reshape(L, L, n_blk, gpb, S5_GROUP, S5_GROUP)
        m = jnp.einsum('stbgck,gh->bsgkthc', kern, eye).reshape(n_blk, L * LANES, L * LANES)
        qr, qi = power((L - 1 - s) if d == 0 else s)
        qr, qi = qr[:, d], qi[:, d]
        str_ = qr[..., None] * bbr[d] - qi[..., None] * bbi[d]
        sti = qr[..., None] * bbi[d] + qi[..., None] * bbr[d]

        def to_state(x):
            x = x.reshape(L, n_blk, gpb, n_st, S5_GROUP)
            return jnp.einsum('sbgnk,gh->bsgkhn', x, eye).reshape(n_blk, L * LANES, gpb * n_st)

        wst = jnp.concatenate([to_state(str_), to_state(sti)], axis=2)
        rr, ri = power((s + 1) if d == 0 else (L - s))
        rr, ri = rr[:, d], ri[:, d]
        wr = c_re[d][None] * rr[:, :, None, :] - c_im[d][None] * ri[:, :, None, :]
        wi = -(c_re[d][None] * ri[:, :, None, :] + c_im[d][None] * rr[:, :, None, :])

        def from_state(x):
            x = x.reshape(L, n_blk, gpb, S5_GROUP, n_st)
            return jnp.einsum('tbgcn,gh->bgnthc', x, eye).reshape(n_blk, gpb * n_st, L * LANES)

        wout = jnp.concatenate([from_state(wr), from_state(wi)], axis=1)
        ler, lei = jnp.exp(mag[d] * L) * jnp.cos(ang[d] * L), jnp.exp(mag[d] * L) * jnp.sin(ang[d] * L)
        lam_l = jnp.concatenate([ler.reshape(n_blk, 1, gpb * n_st), lei.reshape(n_blk, 1, gpb * n_st)],
                                axis=2)
        outs.append((m.astype(BF16), wst.astype(BF16), wout.astype(BF16), lam_l))
    return tuple(jnp.stack([o[k] for o in outs]) for k in range(4))


def _s5_kernel(u_ref, m_ref, wst_ref, wout_ref, lam_ref, d_ref, o_ref, x_ref, hp_ref, *, rows, rows_ctx):
    L = S5_CHUNK
    dr = pl.program_id(2)
    ns = lam_ref.shape[1] // 2
    u = jnp.concatenate([u_ref[pl.ds(s, rows, stride=L), :] for s in range(L)], axis=1)
    ub = u.astype(BF16)
    x_ref[...] = jnp.dot(ub, wst_ref[...], preferred_element_type=F32)
    lr = lam_ref[:, 0:ns]
    li = lam_ref[:, ns:2 * ns]

    def visit(r, carry):
        hr, hi = carry
        hp_ref[pl.ds(r, 1), 0:ns] = hr
        hp_ref[pl.ds(r, 1), ns:2 * ns] = hi
        xr = x_ref[pl.ds(r, 1), 0:ns]
        xi = x_ref[pl.ds(r, 1), ns:2 * ns]
        return lr * hr - li * hi + xr, lr * hi + li * hr + xi

    zero = (jnp.zeros((1, ns), F32), jnp.zeros((1, ns), F32))

    @pl.when(dr == 0)
    def _():
        lax.fori_loop(0, rows, visit, zero, unroll=4)

    @pl.when(dr == 1)
    def _():
        c = lax.fori_loop(0, rows_ctx, lambda t, c: visit(rows_ctx - 1 - t, c), zero, unroll=4)
        lax.fori_loop(0, rows - rows_ctx, lambda t, c: visit(rows - 1 - t, c), c, unroll=4)

    y = (jnp.dot(ub, m_ref[...], preferred_element_type=F32)
         + jnp.dot(hp_ref[...].astype(BF16), wout_ref[...], preferred_element_type=F32))

    @pl.when(dr == 0)
    def _():
        for s in range(L):
            o_ref[pl.ds(s, rows, stride=L), :] = (y[:, s * LANES:(s + 1) * LANES]
                                                  + d_ref[...] * u[:, s * LANES:(s + 1) * LANES])

    @pl.when(dr == 1)
    def _():
        for s in range(L):
            o_ref[pl.ds(s, rows, stride=L), :] += y[:, s * LANES:(s + 1) * LANES]


def _s5(z, weights, dvec, tc, width):
    b, ttot, _ = z.shape
    m, wst, wout, lam_l = weights
    n_dir, n_blk = m.shape[0], m.shape[1]
    rows, rows_ctx = ttot // S5_CHUNK, tc // S5_CHUNK
    wspec = lambda a: pl.BlockSpec((None, None) + a.shape[2:], lambda bi, j, dr: (dr, j, 0, 0))
    return pl.pallas_call(
        functools.partial(_s5_kernel, rows=rows, rows_ctx=rows_ctx),
        grid=(b, n_blk, n_dir),
        in_specs=[pl.BlockSpec((None, ttot, LANES), lambda bi, j, dr: (bi, 0, j)),
                  wspec(m), wspec(wst), wspec(wout), wspec(lam_l),
                  pl.BlockSpec((1, LANES), lambda bi, j, dr: (0, j))],
        out_specs=pl.BlockSpec((None, ttot, LANES), lambda bi, j, dr: (bi, 0, j)),
        out_shape=jax.ShapeDtypeStruct((b, ttot, width), F32),
        scratch_shapes=[pltpu.VMEM((rows, lam_l.shape[3]), F32), pltpu.VMEM((rows, lam_l.shape[3]), F32)],
        compiler_params=_cparams(("parallel", "parallel", "arbitrary")),
        name="s5_bidir",
    )(z, m, wst, wout, lam_l, dvec.reshape(1, width))


def _gla_kernel(q_ref, f_ref, v_ref, lb_ref, o_ref, st_ref, *, lc, reverse):
    step = pl.program_id(1)

    @pl.when(step == 0)
    def _():
        st_ref[...] = jnp.zeros_like(st_ref)

    c = HG_CHUNK
    r_i = lax.broadcasted_iota(jnp.int32, (c, c), 0)
    c_i = lax.broadcasted_iota(jnp.int32, (c, c), 1)
    keep = (c_i >= r_i) if reverse else (c_i <= r_i)
    tri = jnp.where(keep, 1.0, 0.0).astype(BF16)
    lb = lb_ref[...]
    n_chunks = lc // c
    order = range(n_chunks - 1, -1, -1) if reverse else range(n_chunks)
    for ci in order:
        rows = pl.ds(ci * c, c)
        qraw = q_ref[rows, :]
        q = _silu(qraw)
        f = lb + (1.0 - lb) * _sigmoid(f_ref[rows, :])
        k = 1.0 - f
        logf = jnp.log(f)
        v = v_ref[rows, :].astype(BF16)
        hi = logf.astype(BF16)
        r1 = logf - hi.astype(F32)
        mid = r1.astype(BF16)
        lo = (r1 - mid.astype(F32)).astype(BF16)
        cum = (jnp.dot(tri, hi, preferred_element_type=F32) + jnp.dot(tri, mid, preferred_element_type=F32)
               + jnp.dot(tri, lo, preferred_element_type=F32))
        total = cum[0:1, :] if reverse else cum[c - 1:c, :]
        mid_ref = cum[c // 2:c // 2 + 1, :]
        q_in = (q * jnp.exp(cum)).astype(BF16)
        k_out = (k * jnp.exp(total - cum)).astype(BF16)
        q_loc = (q * jnp.exp(cum - mid_ref)).astype(BF16)
        k_loc = (k * jnp.exp(mid_ref - cum)).astype(BF16)
        decay = jnp.exp(total)
        for h in range(HG_HEADS):
            sl = slice(h * HG_DK, (h + 1) * HG_DK)
            st = st_ref[h]
            o = lax.dot_general(q_in[:, sl], st.astype(BF16), (((1,), (1,)), ((), ())),
                                preferred_element_type=F32)
            att = lax.dot_general(q_loc[:, sl], k_loc[:, sl], (((1,), (1,)), ((), ())),
                                  preferred_element_type=F32)
            att = jnp.where(keep, att, 0.0)
            o = o + jnp.dot(att.astype(BF16), v[:, sl], preferred_element_type=F32)
            o_ref[rows, sl] = o
            st_ref[h] = st * decay[:, sl] + lax.dot_general(
                v[:, sl], k_out[:, sl], (((0,), (0,)), ((), ())), preferred_element_type=F32)


def _gla(z, lb, tc, fcol, reverse):
    b, ttot, _ = z.shape
    w = HG_HEADS * HG_DK
    lc = SEQ_BLOCK
    n_all, n_ctx = ttot // lc, tc // lc

    def blk(s):
        return _seq_block(s, n_ctx, n_all, reverse)

    col = lambda cidx: pl.BlockSpec((None, lc, w), lambda bi, s: (bi, blk(s), cidx))
    return pl.pallas_call(
        functools.partial(_gla_kernel, lc=lc, reverse=reverse),
        grid=(b, n_all),
        in_specs=[col(1), col(fcol), col(4), pl.BlockSpec((1, w), lambda bi, s: (0, 0))],
        out_specs=pl.BlockSpec((None, lc, w), lambda bi, s: (bi, blk(s), 0)),
        out_shape=jax.ShapeDtypeStruct((b, ttot, w), F32),
        scratch_shapes=[pltpu.VMEM((HG_HEADS, HG_DK, HG_DK), F32)],
        compiler_params=_cparams(("parallel", "arbitrary")),
        name="hgrn2_rev" if reverse else "hgrn2_fwd",
    )(z, z, z, lb.reshape(1, w))


def _final_kernel(x_ref, g_ref, o_ref):
    x = x_ref[...]
    ms = jnp.mean(x * x, axis=-1, keepdims=True)
    o_ref[...] = x * lax.rsqrt(ms + EPS) * g_ref[...]


def _final_norm(xa, g, tc):
    b, ttot, d = xa.shape
    t = ttot - tc
    tm = SEQ_BLOCK
    off = tc // tm
    return pl.pallas_call(
        _final_kernel,
        grid=(b, t // tm),
        in_specs=[pl.BlockSpec((None, tm, d), lambda bi, i: (bi, i + off, 0)),
                  pl.BlockSpec((1, d), lambda bi, i: (0, 0))],
        out_specs=pl.BlockSpec((None, tm, d), lambda bi, i: (bi, i, 0)),
        out_shape=jax.ShapeDtypeStruct((b, t, d), F32),
        compiler_params=_cparams(("parallel", "parallel")),
        name="final_norm",
    )(xa, g.reshape(1, d))


def _rope_tables(t, tc):
    pos = jnp.arange(t)
    row = (pos // GRID_W).astype(F32)
    col = (pos % GRID_W).astype(F32)
    inv = ROPE_BASE ** (-jnp.arange(ROPE_FREQS, dtype=F32) / ROPE_FREQS)
    ar, ac = row[:, None] * inv, col[:, None] * inv
    cos = jnp.concatenate([jnp.cos(ar), jnp.cos(ar), jnp.cos(ac), jnp.cos(ac)], axis=1)
    sin = jnp.concatenate([-jnp.sin(ar), jnp.sin(ar), -jnp.sin(ac), jnp.sin(ac)], axis=1)
    cos = jnp.concatenate([jnp.ones((tc, HEAD_DIM), F32), cos], axis=0)
    sin = jnp.concatenate([jnp.zeros((tc, HEAD_DIM), F32), sin], axis=0)
    return jnp.tile(cos, (1, LANES // HEAD_DIM)), jnp.tile(sin, (1, LANES // HEAD_DIM))


def _block_diag_dense(w):
    nblk, h, k = w.shape
    return jnp.einsum('nhk,nm->nhmk', w, jnp.eye(nblk, dtype=w.dtype)).reshape(nblk * h, nblk * k)


def kernel(x, c, ctx, c_ctx, w_mod, b_mod, norm_mix, norm_ffn, final_norm, w_in_ab, lru_conv_w, lru_conv_b, lru_wa, lru_ba, lru_wx, lru_bx, lru_lam, attn_sink, w_out_ab, ffn_w1, ffn_w3, ffn_w2, w_in_cd, s5_a_re, s5_a_im, s5_log_step, s5_b_re, s5_b_im, s5_c_re, s5_c_im, s5_d, s5_w_glu, s5_b_glu, hg_lb_raw, hg_norm, w_out_cd, moe_router, moe_w1, moe_w3, moe_w2):
    nb, t, d = x.shape
    tc = ctx.shape[1]
    depth = w_mod.shape[0]
    assert tc % SEQ_BLOCK == 0 and t % SEQ_BLOCK == 0 and t % GRID_W == 0

    xa = jnp.concatenate([ctx, x], axis=1)
    mod_rows = -(-(nb + 1) // SUBLANES) * SUBLANES
    cvec = jnp.zeros((mod_rows, d), F32).at[:nb].set(c).at[nb].set(c_ctx)
    mod = _modulation(cvec, w_mod, b_mod).reshape(depth, mod_rows, N_MOD, d)

    cos_t, sin_t = _rope_tables(t, tc)
    lb_soft = jax.nn.softmax(hg_lb_raw.astype(F32), axis=0)
    lb_table = jnp.cumsum(lb_soft, axis=0) - lb_soft[0:1]
    n_exp = moe_router.shape[2]

    for l in range(depth):
        j = l // 2
        mod_l = mod[l]
        if l % 2 == 0:
            z = _inproj(xa, mod_l, norm_mix[l], w_in_ab[j].astype(BF16), tc, nb)
            sp = jax.nn.softplus(-lru_lam[j])
            lru = []
            for dr, rev in enumerate((False, True)):
                lru.append(_lru(z, lru_conv_w[j], lru_conv_b[j],
                                _block_diag_dense(lru_wa[j, dr]).astype(BF16), lru_ba[j, dr],
                                _block_diag_dense(lru_wx[j, dr]).astype(BF16), lru_bx[j, dr],
                                sp[dr], tc, rev))
            sink = jnp.zeros((1, LANES), F32).at[0, :ATT_HEADS].set(attn_sink[j])
            att = _attention(z, cos_t, sin_t, sink, tc)
            xa = _outproj_ab(xa, mod_l, lru[0], lru[1], z, att, w_out_ab[j].astype(BF16), tc, nb)
            xa = _ffn(xa, mod_l, norm_ffn[l], ffn_w1[j].astype(BF16), ffn_w3[j].astype(BF16),
                      ffn_w2[j].astype(BF16), tc, nb)
        else:
            z = _inproj(xa, mod_l, norm_mix[l], w_in_cd[j].astype(BF16), tc, nb)
            s5w = _s5_weights(s5_a_re[j], s5_a_im[j], s5_log_step[j], s5_b_re[j], s5_b_im[j],
                              s5_c_re[j], s5_c_im[j])
            y5 = _s5(z, s5w, s5_d[j], tc, s5_d.shape[1])
            o_f = _gla(z, lb_table[j], tc, 2, False)
            o_b = _gla(z, lb_table[j], tc, 3, True)
            xa = _outproj_cd(xa, mod_l, y5, o_f, o_b, z, s5_w_glu[j].astype(BF16), s5_b_glu[j],
                             hg_norm[j], w_out_cd[j].astype(BF16), tc, nb)
            router_t = jnp.zeros((SUBLANES, d), F32).at[:n_exp].set(moe_router[j].T)
            xa = _moe(xa, mod_l, norm_ffn[l], router_t, moe_w1[j].astype(BF16),
                      moe_w3[j].astype(BF16), moe_w2[j].astype(BF16), tc, nb)
    return _final_norm(xa, final_norm, tc)
```

```python
import functools
import math

import jax
import jax.numpy as jnp
from jax import lax
from jax.experimental import pallas as pl
from jax.experimental.pallas import tpu as pltpu

F32 = jnp.float32
BF16 = jnp.bfloat16

EPS = 1e-6
GRID_W = 64
LRU_BLOCKS = 8
LRU_C = 8.0
CONV_W = 4
ATT_HEADS = 8
ATT_KV_HEADS = 2
ATT_GROUP = ATT_HEADS // ATT_KV_HEADS
HEAD_DIM = 64
WINDOW = 128
BLOCK_Q = 128
ROPE_FREQS = HEAD_DIM // 4
ROPE_BASE = 10000.0
S5_GROUP = 16
S5_STATE = 64
S5_CHUNK = 8
HG_HEADS = 4
HG_DK = 128
HG_CHUNK = 64
SEQ_BLOCK = 256
N_MOD = 6
MOE_CAP = 224
LANES = 128
SUBLANES = 8
VMEM_LIMIT = 56 * 1024 * 1024
NEG = -1e30
LOG2E = math.log2(math.e)


def _cparams(sem):
    return pltpu.CompilerParams(dimension_semantics=sem, vmem_limit_bytes=VMEM_LIMIT)


def _row_tile(ttot):
    for tm in (768, 1024, 512, 256):
        if ttot % tm == 0:
            return tm
    raise ValueError(f"unsupported token count {ttot}")


def _sigmoid(x):
    return 0.5 * jnp.tanh(0.5 * x) + 0.5


def _silu(x):
    return x * _sigmoid(x)


def _gelu(x):
    return 0.5 * x * (1.0 + jnp.tanh(math.sqrt(2.0 / math.pi) * (x + 0.044715 * (x * x * x))))


def _norm_mod(x, g, mb, mc, row0, tc, shift_idx, scale_idx):
    ms = jnp.mean(x * x, axis=-1, keepdims=True)
    y = x * lax.rsqrt(ms + EPS) * g
    rows = row0 + lax.broadcasted_iota(jnp.int32, (x.shape[0], 1), 0)
    is_ctx = rows < tc
    scale = jnp.where(is_ctx, mc[scale_idx:scale_idx + 1], mb[scale_idx:scale_idx + 1])
    shift = jnp.where(is_ctx, mc[shift_idx:shift_idx + 1], mb[shift_idx:shift_idx + 1])
    return y * (1.0 + scale) + shift


def _gate_mod(mb, mc, idx, row0, tc, n):
    rows = row0 + lax.broadcasted_iota(jnp.int32, (n, 1), 0)
    return jnp.where(rows < tc, mc[idx:idx + 1], mb[idx:idx + 1])


def _mod_kernel(c_ref, w_ref, b_ref, o_ref):
    s = _silu(c_ref[...])
    o_ref[...] = jnp.dot(s, w_ref[...], preferred_element_type=F32,
                         precision=lax.Precision.HIGHEST) + b_ref[...]


def _modulation(cvec, w_mod, b_mod):
    depth, d, n = w_mod.shape
    tn = 1536 if n % 1536 == 0 else n
    rows = cvec.shape[0]
    return pl.pallas_call(
        _mod_kernel,
        grid=(depth, n // tn),
        in_specs=[pl.BlockSpec((rows, d), lambda l, j: (0, 0)),
                  pl.BlockSpec((None, d, tn), lambda l, j: (l, 0, j)),
                  pl.BlockSpec((None, 1, tn), lambda l, j: (l, 0, j))],
        out_specs=pl.BlockSpec((None, rows, tn), lambda l, j: (l, 0, j)),
        out_shape=jax.ShapeDtypeStruct((depth, rows, n), F32),
        compiler_params=_cparams(("arbitrary", "arbitrary")),
        name="modulation",
    )(cvec, w_mod, b_mod.reshape(depth, 1, n))


def _inproj_kernel(x_ref, mb_ref, mc_ref, g_ref, w_ref, o_ref, *, tm, tc):
    i = pl.program_id(1)
    h = _norm_mod(x_ref[...], g_ref[...], mb_ref[...], mc_ref[...], i * tm, tc, 0, 1)
    o_ref[...] = jnp.dot(h.astype(BF16), w_ref[...], preferred_element_type=F32)


def _inproj(xa, mod_l, g, w, tc, nb):
    b, ttot, d = xa.shape
    n = w.shape[1]
    tm = _row_tile(ttot)
    return pl.pallas_call(
        functools.partial(_inproj_kernel, tm=tm, tc=tc),
        grid=(b, ttot // tm),
        in_specs=[pl.BlockSpec((None, tm, d), lambda bi, i: (bi, i, 0)),
                  pl.BlockSpec((None, N_MOD, d), lambda bi, i: (bi, 0, 0)),
                  pl.BlockSpec((None, N_MOD, d), lambda bi, i: (nb, 0, 0)),
                  pl.BlockSpec((1, d), lambda bi, i: (0, 0)),
                  pl.BlockSpec((d, n), lambda bi, i: (0, 0))],
        out_specs=pl.BlockSpec((None, tm, n), lambda bi, i: (bi, i, 0)),
        out_shape=jax.ShapeDtypeStruct((b, ttot, n), F32),
        compiler_params=_cparams(("parallel", "parallel")),
        name="inproj",
    )(xa, mod_l, mod_l, g.reshape(1, d), w)


def _seq_block(step, n_ctx, n_all, reverse):
    if not reverse:
        return step
    return jnp.where(step < n_ctx, n_ctx - 1 - step, n_all - 1 - step + n_ctx)


def _lru_kernel(u_ref, hp_ref, hn_ref, cw_ref, cb_ref, wa_ref, ba_ref, wx_ref, bx_ref, sp_ref,
                o_ref, uc_ref, a_ref, b_ref, h_ref, *, lc, n_ctx, n_all, reverse):
    step = pl.program_id(1)
    blk = _seq_block(step, n_ctx, n_all, reverse)
    has_prev = jnp.logical_and(blk != 0, blk != n_ctx)
    has_next = jnp.logical_and(blk != n_ctx - 1, blk != n_all - 1)

    @pl.when(step == 0)
    def _():
        h_ref[...] = jnp.zeros_like(h_ref)

    w0, w1, w2, w3 = (cw_ref[j:j + 1, :] for j in range(CONV_W))
    cb = cb_ref[...]
    u = u_ref[...]
    uc_ref[...] = (cb + pltpu.roll(u, 2, 0) * w0 + pltpu.roll(u, 1, 0) * w1 + u * w2
                   + pltpu.roll(u, lc - 1, 0) * w3)
    row = lax.broadcasted_iota(jnp.int32, (SUBLANES, 1), 0)
    hp = jnp.where(has_prev, hp_ref[...], 0.0)
    hn = jnp.where(has_next, hn_ref[...], 0.0)
    u0 = u_ref[0:SUBLANES, :]
    u1 = u_ref[SUBLANES:2 * SUBLANES, :]
    uc_ref[0:SUBLANES, :] = (
        cb + jnp.where(row < 2, pltpu.roll(hp, 2, 0), pltpu.roll(u0, 2, 0)) * w0
        + jnp.where(row < 1, pltpu.roll(hp, 1, 0), pltpu.roll(u0, 1, 0)) * w1 + u0 * w2
        + jnp.where(row < SUBLANES - 1, pltpu.roll(u0, SUBLANES - 1, 0),
                    pltpu.roll(u1, SUBLANES - 1, 0)) * w3)
    ul = u_ref[lc - SUBLANES:lc, :]
    um = u_ref[lc - 2 * SUBLANES:lc - SUBLANES, :]
    uc_ref[lc - SUBLANES:lc, :] = (
        cb + jnp.where(row < 2, pltpu.roll(um, 2, 0), pltpu.roll(ul, 2, 0)) * w0
        + jnp.where(row < 1, pltpu.roll(um, 1, 0), pltpu.roll(ul, 1, 0)) * w1 + ul * w2
        + jnp.where(row < SUBLANES - 1, pltpu.roll(ul, SUBLANES - 1, 0),
                    pltpu.roll(hn, SUBLANES - 1, 0)) * w3)

    uc = uc_ref[...]
    ub = uc.astype(BF16)
    r = _sigmoid(jnp.dot(ub, wa_ref[...], preferred_element_type=F32) + ba_ref[...])
    gi = _sigmoid(jnp.dot(ub, wx_ref[...], preferred_element_type=F32) + bx_ref[...])
    log_a = -LRU_C * r * sp_ref[...]
    a = jnp.exp(log_a)
    b = jnp.sqrt(1.0 - jnp.exp(2.0 * log_a)) * (gi * uc)
    sub = lax.broadcasted_iota(jnp.int32, (lc, 1), 0) % SUBLANES
    dist = 1
    while dist < SUBLANES:
        inside = (sub < SUBLANES - dist) if reverse else (sub >= dist)
        shift = (lc - dist) if reverse else dist
        a_sh = jnp.where(inside, pltpu.roll(a, shift, 0), 1.0)
        b_sh = jnp.where(inside, pltpu.roll(b, shift, 0), 0.0)
        b = a * b_sh + b
        a = a * a_sh
        dist *= 2
    a_ref[...] = a
    b_ref[...] = b
    n_grp = lc // SUBLANES

    def body(i, carry):
        t = (n_grp - 1 - i) if reverse else i
        rows = pl.ds(pl.multiple_of(t * SUBLANES, SUBLANES), SUBLANES)
        h = a_ref[rows, :] * carry + b_ref[rows, :]
        o_ref[rows, :] = h
        return h[0:1, :] if reverse else h[SUBLANES - 1:SUBLANES, :]

    h_ref[0:1, :] = lax.fori_loop(0, n_grp, body, h_ref[0:1, :], unroll=4)


def _lru(z, conv_w, conv_b, wa_bd, ba, wx_bd, bx, sp, tc, reverse):
    b, ttot, _ = z.shape
    w = conv_w.shape[1]
    lc = SEQ_BLOCK
    n_all, n_ctx = ttot // lc, tc // lc
    hb = lc // SUBLANES
    n_h = ttot // SUBLANES
    col = 1

    def blk(s):
        return _seq_block(s, n_ctx, n_all, reverse)

    vec = lambda: pl.BlockSpec((1, w), lambda bi, s: (0, 0))
    return pl.pallas_call(
        functools.partial(_lru_kernel, lc=lc, n_ctx=n_ctx, n_all=n_all, reverse=reverse),
        grid=(b, n_all),
        in_specs=[pl.BlockSpec((None, lc, w), lambda bi, s: (bi, blk(s), col)),
                  pl.BlockSpec((None, SUBLANES, w),
                               lambda bi, s: (bi, jnp.maximum(blk(s) * hb - 1, 0), col)),
                  pl.BlockSpec((None, SUBLANES, w),
                               lambda bi, s: (bi, jnp.minimum((blk(s) + 1) * hb, n_h - 1), col)),
                  pl.BlockSpec((CONV_W, w), lambda bi, s: (0, 0)),
                  vec(),
                  pl.BlockSpec((w, w), lambda bi, s: (0, 0)),
                  vec(),
                  pl.BlockSpec((w, w), lambda bi, s: (0, 0)),
                  vec(), vec()],
        out_specs=pl.BlockSpec((None, lc, w), lambda bi, s: (bi, blk(s), 0)),
        out_shape=jax.ShapeDtypeStruct((b, ttot, w), F32),
        scratch_shapes=[pltpu.VMEM((lc, w), F32), pltpu.VMEM((lc, w), F32),
                        pltpu.VMEM((lc, w), F32), pltpu.VMEM((SUBLANES, w), F32)],
        compiler_params=_cparams(("parallel", "arbitrary")),
        name="rglru_rev" if reverse else "rglru_fwd",
    )(z, z, z, conv_w, conv_b.reshape(1, w), wa_bd, ba.reshape(1, w), wx_bd, bx.reshape(1, w),
      sp.reshape(1, w))


def _rope(x, cos, sin_signed):
    lane = lax.broadcasted_iota(jnp.int32, x.shape, 1)
    swapped = jnp.where(lane % 32 < 16, pltpu.roll(x, LANES - 16, 1), pltpu.roll(x, 16, 1))
    return x * cos + swapped * sin_signed


def _pair_layout(x, x_sw, head):
    lo = lax.broadcasted_iota(jnp.int32, x.shape, 1) < HEAD_DIM
    first, second = (x, x_sw) if head == 0 else (x_sw, x)
    return jnp.concatenate([jnp.where(lo, first, 0.0), jnp.where(lo, 0.0, second)],
                           axis=0).astype(BF16)


def _fold_lanes(x, op):
    acc = x[:, 0:LANES]
    for j in range(1, x.shape[1] // LANES):
        acc = op(acc, x[:, j * LANES:(j + 1) * LANES])
    return acc


def _attend(q_block, keys, vals, bias, sink_row, o_ref):
    nk = keys.shape[0]
    keys_sw = pltpu.roll(keys, HEAD_DIM, 1)
    vals_sw = pltpu.roll(vals, HEAD_DIM, 1)
    lo = lax.broadcasted_iota(jnp.int32, (BLOCK_Q, LANES), 1) < HEAD_DIM
    pairs_per_kv = ATT_GROUP // 2
    for kv in range(ATT_KV_HEADS):
        kab = _pair_layout(keys, keys_sw, kv)
        vab = _pair_layout(vals, vals_sw, kv)
        for g in range(kv * pairs_per_kv, (kv + 1) * pairs_per_kv):
            s = lax.dot_general(q_block(g), kab, (((1,), (1,)), ((), ())),
                                preferred_element_type=F32)
            halves = []
            recips = []
            for half in range(2):
                sh = s[:, half * nk:(half + 1) * nk]
                if bias is not None:
                    sh = sh + bias
                sk = sink_row[:, 2 * g + half:2 * g + half + 1]
                m = jnp.maximum(jnp.max(_fold_lanes(sh, jnp.maximum), axis=-1, keepdims=True), sk)
                p = jnp.exp2(sh - m)
                recips.append(1.0 / (jnp.sum(_fold_lanes(p, jnp.add), axis=-1, keepdims=True)
                                     + jnp.exp2(sk - m)))
                halves.append(p.astype(BF16))
            o = jnp.dot(jnp.concatenate(halves, axis=1), vab, preferred_element_type=F32)
            o_ref[:, g * LANES:(g + 1) * LANES] = o * jnp.where(lo, recips[0], recips[1])


def _attn_kernel(q_ref, kp_ref, k0_ref, kn_ref, vp_ref, v0_ref, vn_ref, kc_ref, vc_ref,
                 cq_ref, sq_ref, cp_ref, sp_ref, cn_ref, sn_ref, sink_ref, o_ref, *, n_ctx, n_lat):
    i = pl.program_id(1)
    qscale = HEAD_DIM ** -0.5 * LOG2E
    sink_row = sink_ref[...] * LOG2E

    @pl.when(i < n_ctx)
    def _():
        def q_block(g):
            return (q_ref[:, g * LANES:(g + 1) * LANES] * qscale).astype(BF16)
        _attend(q_block, kc_ref[...], vc_ref[...], None, sink_row, o_ref)

    @pl.when(i >= n_ctx)
    def _():
        n = i - n_ctx
        cq, sq = cq_ref[...], sq_ref[...]

        def q_block(g):
            return (_rope(q_ref[:, g * LANES:(g + 1) * LANES], cq, sq) * qscale).astype(BF16)

        kp = _rope(kp_ref[...], cp_ref[...], sp_ref[...])
        k0 = _rope(k0_ref[...], cq, sq)
        kn = _rope(kn_ref[...], cn_ref[...], sn_ref[...])
        keys = jnp.concatenate([kp, k0, kn, kc_ref[...]], axis=0)
        vals = jnp.concatenate([vp_ref[...], v0_ref[...], vn_ref[...], vc_ref[...]], axis=0)
        nk = keys.shape[0]
        qpos = n * BLOCK_Q + lax.broadcasted_iota(jnp.int32, (BLOCK_Q, nk), 0)
        col = lax.broadcasted_iota(jnp.int32, (BLOCK_Q, nk), 1)
        kpos = (n - 1) * BLOCK_Q + col
        local = (jnp.abs(qpos - kpos) <= WINDOW) & (kpos >= 0) & (kpos < n_lat * BLOCK_Q)
        bias = jnp.where(local | (col >= 3 * BLOCK_Q), 0.0, NEG)
        _attend(q_block, keys, vals, bias, sink_row, o_ref)


def _attention(z, cos_t, sin_t, sink, tc):
    b, ttot, _ = z.shape
    n_all, n_ctx = ttot // BLOCK_Q, tc // BLOCK_Q
    n_lat = n_all - n_ctx
    kcol = (2 * 512 + ATT_HEADS * HEAD_DIM) // LANES
    vcol = kcol + 1
    qcol = 2 * 512 // (ATT_HEADS * HEAD_DIM)

    def prev(i):
        return jnp.maximum(i - 1, 0)

    def nxt(i):
        return jnp.minimum(i + 1, n_all - 1)

    def rows(col, f):
        return pl.BlockSpec((None, BLOCK_Q, LANES), lambda bi, i: (bi, f(i), col))

    def tab(f):
        return pl.BlockSpec((BLOCK_Q, LANES), lambda bi, i: (f(i), 0))

    same = lambda i: i
    return pl.pallas_call(
        functools.partial(_attn_kernel, n_ctx=n_ctx, n_lat=n_lat),
        grid=(b, n_all),
        in_specs=[pl.BlockSpec((None, BLOCK_Q, ATT_HEADS * HEAD_DIM), lambda bi, i: (bi, i, qcol)),
                  rows(kcol, prev), rows(kcol, same), rows(kcol, nxt),
                  rows(vcol, prev), rows(vcol, same), rows(vcol, nxt),
                  pl.BlockSpec((None, tc, LANES), lambda bi, i: (bi, 0, kcol)),
                  pl.BlockSpec((None, tc, LANES), lambda bi, i: (bi, 0, vcol)),
                  tab(same), tab(same), tab(prev), tab(prev), tab(nxt), tab(nxt),
                  pl.BlockSpec((1, LANES), lambda bi, i: (0, 0))],
        out_specs=pl.BlockSpec((None, BLOCK_Q, ATT_HEADS * HEAD_DIM), lambda bi, i: (bi, i, 0)),
        out_shape=jax.ShapeDtypeStruct((b, ttot, ATT_HEADS * HEAD_DIM), F32),
        compiler_params=_cparams(("parallel", "parallel")),
        name="window_gqa",
    )(z, z, z, z, z, z, z, z, z, cos_t, sin_t, cos_t, sin_t, cos_t, sin_t, sink)


def _outproj_ab_kernel(x_ref, mb_ref, mc_ref, lf_ref, lb_ref, g_ref, at_ref, w_ref, o_ref, *, tm, tc):
    i = pl.program_id(1)
    lru = (lf_ref[...] + lb_ref[...]) * _gelu(g_ref[...])
    mix = jnp.concatenate([lru.astype(BF16), at_ref[...].astype(BF16)], axis=1)
    dx = jnp.dot(mix, w_ref[...], preferred_element_type=F32)
    gate = _gate_mod(mb_ref[...], mc_ref[...], 2, i * tm, tc, tm)
    o_ref[...] = x_ref[...] + gate * dx


def _outproj_ab(xa, mod_l, lru_f, lru_b, z, att, w_out, tc, nb):
    b, ttot, d = xa.shape
    w = lru_f.shape[2]
    tm = _row_tile(ttot)
    tile = lambda width, col: pl.BlockSpec((None, tm, width), lambda bi, i: (bi, i, col))
    return pl.pallas_call(
        functools.partial(_outproj_ab_kernel, tm=tm, tc=tc),
        grid=(b, ttot // tm),
        in_specs=[tile(d, 0),
                  pl.BlockSpec((None, N_MOD, d), lambda bi, i: (bi, 0, 0)),
                  pl.BlockSpec((None, N_MOD, d), lambda bi, i: (nb, 0, 0)),
                  tile(w, 0), tile(w, 0), tile(w, 0), tile(att.shape[2], 0),
                  pl.BlockSpec(w_out.shape, lambda bi, i: (0, 0))],
        out_specs=tile(d, 0),
        out_shape=jax.ShapeDtypeStruct(xa.shape, F32),
        compiler_params=_cparams(("parallel", "parallel")),
        name="outproj_ab",
    )(xa, mod_l, mod_l, lru_f, lru_b, z, att, w_out)


def _outproj_cd_kernel(x_ref, mb_ref, mc_ref, y_ref, of_ref, ob_ref, g_ref, wg_ref, bg_ref, gn_ref,
                       w_ref, o_ref, *, tm, tc):
    i = pl.program_id(1)
    y = _gelu(y_ref[...])
    s5 = y * _sigmoid(jnp.dot(y.astype(BF16), wg_ref[...], preferred_element_type=F32) + bg_ref[...])
    o = of_ref[...] + ob_ref[...]
    parts = []
    for h in range(HG_HEADS):
        oh = o[:, h * HG_DK:(h + 1) * HG_DK]
        ms = jnp.mean(oh * oh, axis=-1, keepdims=True)
        parts.append(oh * lax.rsqrt(ms + EPS) * gn_ref[...])
    hg = jnp.concatenate(parts, axis=1) * _silu(g_ref[...])
    mix = jnp.concatenate([s5.astype(BF16), hg.astype(BF16)], axis=1)
    dx = jnp.dot(mix, w_ref[...], preferred_element_type=F32)
    gate = _gate_mod(mb_ref[...], mc_ref[...], 2, i * tm, tc, tm)
    o_ref[...] = x_ref[...] + gate * dx


def _outproj_cd(xa, mod_l, y5, o_f, o_b, z, w_glu, b_glu, hg_norm, w_out, tc, nb):
    b, ttot, d = xa.shape
    w = y5.shape[2]
    tm = _row_tile(ttot)
    gcol = z.shape[2] // w - 1
    tile = lambda width, col: pl.BlockSpec((None, tm, width), lambda bi, i: (bi, i, col))
    return pl.pallas_call(
        functools.partial(_outproj_cd_kernel, tm=tm, tc=tc),
        grid=(b, ttot // tm),
        in_specs=[tile(d, 0),
                  pl.BlockSpec((None, N_MOD, d), lambda bi, i: (bi, 0, 0)),
                  pl.BlockSpec((None, N_MOD, d), lambda bi, i: (nb, 0, 0)),
                  tile(w, 0), tile(w, 0), tile(w, 0), tile(w, gcol),
                  pl.BlockSpec(w_glu.shape, lambda bi, i: (0, 0)),
                  pl.BlockSpec((1, w), lambda bi, i: (0, 0)),
                  pl.BlockSpec((1, HG_DK), lambda bi, i: (0, 0)),
                  pl.BlockSpec(w_out.shape, lambda bi, i: (0, 0))],
        out_specs=tile(d, 0),
        out_shape=jax.ShapeDtypeStruct(xa.shape, F32),
        compiler_params=_cparams(("parallel", "parallel")),
        name="outproj_cd",
    )(xa, mod_l, mod_l, y5, o_f, o_b, z, w_glu, b_glu.reshape(1, w), hg_norm.reshape(1, HG_DK), w_out)


def _ffn_kernel(x_ref, mb_ref, mc_ref, g_ref, w1_ref, w3_ref, w2_ref, o_ref, h_ref, acc_ref, *, tm, tc):
    i = pl.program_id(1)
    k = pl.program_id(2)

    @pl.when(k == 0)
    def _():
        h = _norm_mod(x_ref[...], g_ref[...], mb_ref[...], mc_ref[...], i * tm, tc, 3, 4)
        h_ref[...] = h.astype(BF16)
        acc_ref[...] = jnp.zeros_like(acc_ref)

    h = h_ref[...]
    a = jnp.dot(h, w1_ref[...], preferred_element_type=F32)
    c = jnp.dot(h, w3_ref[...], preferred_element_type=F32)
    acc_ref[...] += jnp.dot((_silu(a) * c).astype(BF16), w2_ref[...], preferred_element_type=F32)

    @pl.when(k == pl.num_programs(2) - 1)
    def _():
        gate = _gate_mod(mb_ref[...], mc_ref[...], 5, i * tm, tc, tm)
        o_ref[...] = x_ref[...] + gate * acc_ref[...]


def _ffn(xa, mod_l, g, w1, w3, w2, tc, nb):
    b, ttot, d = xa.shape
    f = w1.shape[1]
    tm = _row_tile(ttot)
    tf = f // 2 if (f // 2) % LANES == 0 else f
    return pl.pallas_call(
        functools.partial(_ffn_kernel, tm=tm, tc=tc),
        grid=(b, ttot // tm, f // tf),
        in_specs=[pl.BlockSpec((None, tm, d), lambda bi, i, k: (bi, i, 0)),
                  pl.BlockSpec((None, N_MOD, d), lambda bi, i, k: (bi, 0, 0)),
                  pl.BlockSpec((None, N_MOD, d), lambda bi, i, k: (nb, 0, 0)),
                  pl.BlockSpec((1, d), lambda bi, i, k: (0, 0)),
                  pl.BlockSpec((d, tf), lambda bi, i, k: (0, k)),
                  pl.BlockSpec((d, tf), lambda bi, i, k: (0, k)),
                  pl.BlockSpec((tf, d), lambda bi, i, k: (k, 0))],
        out_specs=pl.BlockSpec((None, tm, d), lambda bi, i, k: (bi, i, 0)),
        out_shape=jax.ShapeDtypeStruct(xa.shape, F32),
        scratch_shapes=[pltpu.VMEM((tm, d), BF16), pltpu.VMEM((tm, d), F32)],
        compiler_params=_cparams(("parallel", "parallel", "arbitrary")),
        name="ffn_swiglu",
    )(xa, mod_l, mod_l, g.reshape(1, d), w1, w3, w2)


def _router_kernel(x_ref, mb_ref, mc_ref, g_ref, r_ref, h_ref, wt_ref, cnt_ref, *, tm, tc, n_exp):
    i = pl.program_id(1)
    h = _norm_mod(x_ref[...], g_ref[...], mb_ref[...], mc_ref[...], i * tm, tc, 3, 4)
    h_ref[...] = h.astype(BF16)
    logits = lax.dot_general(r_ref[...], h, (((1,), (1,)), ((), ())), preferred_element_type=F32,
                             precision=lax.Precision.HIGHEST)
    sub = lax.broadcasted_iota(jnp.int32, logits.shape, 0)
    logits = jnp.where(sub < n_exp, logits, NEG)
    m1 = jnp.max(logits, axis=0, keepdims=True)
    i1 = jnp.min(jnp.where(logits == m1, sub, SUBLANES), axis=0, keepdims=True)
    rest = jnp.where(sub == i1, NEG, logits)
    m2 = jnp.max(rest, axis=0, keepdims=True)
    i2 = jnp.min(jnp.where(rest == m2, sub, SUBLANES), axis=0, keepdims=True)
    e2 = jnp.exp(m2 - m1)
    g1 = 1.0 / (1.0 + e2)
    wt = jnp.where(sub == i1, g1, 0.0) + jnp.where(sub == i2, e2 * g1, 0.0)
    wt_ref[...] = wt
    cnt = jnp.sum(jnp.where(wt > 0.0, 1.0, 0.0), axis=1, keepdims=True)
    cnt_ref[...] = jnp.broadcast_to(cnt, cnt_ref.shape).astype(jnp.int32)


def _router(xa, mod_l, g, router_t, tc, nb, n_exp):
    b, ttot, d = xa.shape
    tm = _row_tile(ttot)
    n_t = ttot // tm
    return pl.pallas_call(
        functools.partial(_router_kernel, tm=tm, tc=tc, n_exp=n_exp),
        grid=(b, n_t),
        in_specs=[pl.BlockSpec((None, tm, d), lambda bi, i: (bi, i, 0)),
                  pl.BlockSpec((None, N_MOD, d), lambda bi, i: (bi, 0, 0)),
                  pl.BlockSpec((None, N_MOD, d), lambda bi, i: (nb, 0, 0)),
                  pl.BlockSpec((1, d), lambda bi, i: (0, 0)),
                  pl.BlockSpec((SUBLANES, d), lambda bi, i: (0, 0))],
        out_specs=[pl.BlockSpec((None, tm, d), lambda bi, i: (bi, i, 0)),
                   pl.BlockSpec((None, SUBLANES, tm), lambda bi, i: (bi, 0, i)),
                   pl.BlockSpec((None, None, SUBLANES, LANES), lambda bi, i: (bi, i, 0, 0))],
        out_shape=[jax.ShapeDtypeStruct((b, ttot, d), BF16),
                   jax.ShapeDtypeStruct((b, SUBLANES, ttot), F32),
                   jax.ShapeDtypeStruct((b, n_t, SUBLANES, LANES), jnp.int32)],
        compiler_params=_cparams(("parallel", "parallel")),
        name="moe_router",
    )(xa, mod_l, mod_l, g.reshape(1, d), router_t)


def _moe_kernel(cnt_ref, x_ref, mb_ref, mc_ref, h_ref, wt_ref, w1_ref, w3_ref, w2_ref, o_ref,
                rank_ref, acc_ref, *, tm, tc, n_exp, cap):
    bi = pl.program_id(0)
    i = pl.program_id(1)
    e = pl.program_id(2)

    @pl.when(e == 0)
    def _():
        acc_ref[...] = jnp.zeros_like(acc_ref)
        sel = wt_ref[...] > 0.0
        r_i = lax.broadcasted_iota(jnp.int32, (tm, tm), 0)
        c_i = lax.broadcasted_iota(jnp.int32, (tm, tm), 1)
        before = jnp.where(r_i < c_i, 1.0, 0.0).astype(BF16)
        rank = jnp.dot(jnp.where(sel, 1.0, 0.0).astype(BF16), before, preferred_element_type=F32)
        rank_ref[...] = jnp.where(sel, rank, -1.0)

    n_e = cnt_ref[(bi * pl.num_programs(1) + i) * n_exp + e]
    w_row = wt_ref[pl.ds(e, 1), :]
    r_row = rank_ref[pl.ds(e, 1), :]
    for blk in range(-(-tm // cap)):
        @pl.when(n_e > blk * cap)
        def _():
            slot = (lax.broadcasted_iota(jnp.int32, (cap, tm), 0) + blk * cap).astype(F32)
            hit = r_row == slot
            onehot = jnp.where(hit, 1.0, 0.0).astype(BF16)
            xe = jnp.dot(onehot, h_ref[...], preferred_element_type=F32).astype(BF16)
            gate = jnp.sum(jnp.where(hit, w_row, 0.0), axis=1, keepdims=True)
            a = jnp.dot(xe, w1_ref[...], preferred_element_type=F32)
            c = jnp.dot(xe, w3_ref[...], preferred_element_type=F32)
            y = jnp.dot((_silu(a) * c * gate).astype(BF16), w2_ref[...], preferred_element_type=F32)
            acc_ref[...] += lax.dot_general(onehot, y.astype(BF16), (((0,), (0,)), ((), ())),
                                            preferred_element_type=F32)

    @pl.when(e == n_exp - 1)
    def _():
        gate5 = _gate_mod(mb_ref[...], mc_ref[...], 5, i * tm, tc, tm)
        o_ref[...] = x_ref[...] + gate5 * acc_ref[...]


def _moe(xa, mod_l, g, router_t, w1, w3, w2, tc, nb):
    b, ttot, d = xa.shape
    n_exp, _, f = w1.shape
    tm = _row_tile(ttot)
    n_t = ttot // tm
    cap = MOE_CAP
    h, wt, cnt = _router(xa, mod_l, g, router_t, tc, nb, n_exp)
    cnt = cnt[:, :, :n_exp, 0].reshape(-1)
    grid_spec = pltpu.PrefetchScalarGridSpec(
        num_scalar_prefetch=1,
        grid=(b, n_t, n_exp),
        in_specs=[pl.BlockSpec((None, tm, d), lambda bi, i, e, c: (bi, i, 0)),
                  pl.BlockSpec((None, N_MOD, d), lambda bi, i, e, c: (bi, 0, 0)),
                  pl.BlockSpec((None, N_MOD, d), lambda bi, i, e, c: (nb, 0, 0)),
                  pl.BlockSpec((None, tm, d), lambda bi, i, e, c: (bi, i, 0)),
                  pl.BlockSpec((None, SUBLANES, tm), lambda bi, i, e, c: (bi, 0, i)),
                  pl.BlockSpec((None, d, f), lambda bi, i, e, c: (e, 0, 0)),
                  pl.BlockSpec((None, d, f), lambda bi, i, e, c: (e, 0, 0)),
                  pl.BlockSpec((None, f, d), lambda bi, i, e, c: (e, 0, 0))],
        out_specs=pl.BlockSpec((None, tm, d), lambda bi, i, e, c: (bi, i, 0)),
        scratch_shapes=[pltpu.VMEM((SUBLANES, tm), F32), pltpu.VMEM((tm, d), F32)])
    return pl.pallas_call(
        functools.partial(_moe_kernel, tm=tm, tc=tc, n_exp=n_exp, cap=cap),
        grid_spec=grid_spec,
        out_shape=jax.ShapeDtypeStruct(xa.shape, F32),
        compiler_params=_cparams(("parallel", "parallel", "arbitrary")),
        name="moe_experts",
    )(cnt, xa, mod_l, mod_l, h, wt, w1, w3, w2)


def _own_group(x, g_axis, at):
    n = x.shape[g_axis]
    x = jnp.expand_dims(x, at)
    shape_g = [1] * x.ndim
    shape_g[g_axis] = n
    shape_h = [1] * x.ndim
    shape_h[at] = n
    same = jnp.arange(n).reshape(shape_g) == jnp.arange(n).reshape(shape_h)
    return jnp.where(same, x, 0.0)


def _s5_weights(a_re, a_im, log_step, b_re, b_im, c_re, c_im):
    L = S5_CHUNK
    lr = jnp.minimum(a_re, -1e-4)
    li = a_im
    dt = jnp.exp(log_step)[..., None]
    mag, ang = lr * dt, li * dt
    lbr, lbi = jnp.exp(mag) * jnp.cos(ang), jnp.exp(mag) * jnp.sin(ang)
    zr, zi = lbr - 1.0, lbi
    den = lr * lr + li * li
    fr = (zr * lr + zi * li) / den
    fi = (zi * lr - zr * li) / den
    bbr = fr[..., None] * b_re - fi[..., None] * b_im
    bbi = fr[..., None] * b_im + fi[..., None] * b_re

    def power(p):
        p = p[..., None, None, None].astype(F32)
        return jnp.exp(mag * p) * jnp.cos(ang * p), jnp.exp(mag * p) * jnp.sin(ang * p)

    n_dir, n_grp, n_st = a_re.shape
    gpb = LANES // S5_GROUP
    n_blk = n_grp // gpb
    s = jnp.arange(L)
    outs = []
    for d in range(n_dir):
        pr, pi = power(jnp.arange(L))
        pr, pi = pr[:, d], pi[:, d]
        cbr = (jnp.einsum('gcn,tgn,gnk->tgck', c_re[d], pr, bbr[d])
               - jnp.einsum('gcn,tgn,gnk->tgck', c_re[d], pi, bbi[d])
               - jnp.einsum('gcn,tgn,gnk->tgck', c_im[d], pr, bbi[d])
               - jnp.einsum('gcn,tgn,gnk->tgck', c_im[d], pi, bbr[d]))
        lag = (s[None, :] - s[:, None]) if d == 0 else (s[:, None] - s[None, :])
        kern = jnp.where((lag >= 0)[..., None, None, None], cbr[jnp.clip(lag, 0, L - 1)], 0.0)
        kern = kern.reshape(L, L, n_blk, gpb, S5_GROUP, S5_GROUP).transpose(2, 0, 3, 5, 1, 4)
        m = _own_group(kern, 2, 5).reshape(n_blk, L * LANES, L * LANES)
        qr, qi = power((L - 1 - s) if d == 0 else s)
        qr, qi = qr[:, d], qi[:, d]
        str_ = qr[..., None] * bbr[d] - qi[..., None] * bbi[d]
        sti = qr[..., None] * bbi[d] + qi[..., None] * bbr[d]

        def to_state(x):
            x = x.reshape(L, n_blk, gpb, n_st, S5_GROUP).transpose(1, 0, 2, 4, 3)
            return _own_group(x, 2, 4).reshape(n_blk, L * LANES, gpb * n_st)

        wst = jnp.concatenate([to_state(str_), to_state(sti)], axis=2)
        rr, ri = power((s + 1) if d == 0 else (L - s))
        rr, ri = rr[:, d], ri[:, d]
        wr = c_re[d][None] * rr[:, :, None, :] - c_im[d][None] * ri[:, :, None, :]
        wi = -(c_re[d][None] * ri[:, :, None, :] + c_im[d][None] * rr[:, :, None, :])

        def from_state(x):
            x = x.reshape(L, n_blk, gpb, S5_GROUP, n_st).transpose(1, 2, 4, 0, 3)
            return _own_group(x, 1, 4).reshape(n_blk, gpb * n_st, L * LANES)

        wout = jnp.concatenate([from_state(wr), from_state(wi)], axis=1)
        ler, lei = jnp.exp(mag[d] * L) * jnp.cos(ang[d] * L), jnp.exp(mag[d] * L) * jnp.sin(ang[d] * L)
        lam_l = jnp.concatenate([ler.reshape(n_blk, 1, gpb * n_st), lei.reshape(n_blk, 1, gpb * n_st)],
                                axis=2)
        outs.append((m.astype(BF16), wst.astype(BF16), wout.astype(BF16), lam_l))
    return tuple(jnp.stack([o[k] for o in outs]) for k in range(4))


def _s5_kernel(u_ref, m_ref, wst_ref, wout_ref, lam_ref, d_ref, o_ref, x_ref, hp_ref, *, rows, rows_ctx):
    L = S5_CHUNK
    dr = pl.program_id(2)
    ns = lam_ref.shape[1] // 2
    u = jnp.concatenate([u_ref[pl.ds(s, rows, stride=L), :] for s in range(L)], axis=1)
    ub = u.astype(BF16)
    x_ref[...] = jnp.dot(ub, wst_ref[...], preferred_element_type=F32)
    lr = lam_ref[:, 0:ns]
    li = lam_ref[:, ns:2 * ns]

    def visit(r, carry):
        hr, hi = carry
        hp_ref[pl.ds(r, 1), 0:ns] = hr
        hp_ref[pl.ds(r, 1), ns:2 * ns] = hi
        xr = x_ref[pl.ds(r, 1), 0:ns]
        xi = x_ref[pl.ds(r, 1), ns:2 * ns]
        return lr * hr - li * hi + xr, lr * hi + li * hr + xi

    zero = (jnp.zeros((1, ns), F32), jnp.zeros((1, ns), F32))

    @pl.when(dr == 0)
    def _():
        lax.fori_loop(0, rows, visit, zero, unroll=4)

    @pl.when(dr == 1)
    def _():
        c = lax.fori_loop(0, rows_ctx, lambda t, c: visit(rows_ctx - 1 - t, c), zero, unroll=4)
        lax.fori_loop(0, rows - rows_ctx, lambda t, c: visit(rows - 1 - t, c), c, unroll=4)

    y = (jnp.dot(ub, m_ref[...], preferred_element_type=F32)
         + jnp.dot(hp_ref[...].astype(BF16), wout_ref[...], preferred_element_type=F32))

    @pl.when(dr == 0)
    def _():
        for s in range(L):
            o_ref[pl.ds(s, rows, stride=L), :] = (y[:, s * LANES:(s + 1) * LANES]
                                                  + d_ref[...] * u[:, s * LANES:(s + 1) * LANES])

    @pl.when(dr == 1)
    def _():
        for s in range(L):
            o_ref[pl.ds(s, rows, stride=L), :] += y[:, s * LANES:(s + 1) * LANES]


def _s5(z, weights, dvec, tc, width):
    b, ttot, _ = z.shape
    m, wst, wout, lam_l = weights
    n_dir, n_blk = m.shape[0], m.shape[1]
    rows, rows_ctx = ttot // S5_CHUNK, tc // S5_CHUNK
    wspec = lambda a: pl.BlockSpec((None, None) + a.shape[2:], lambda bi, j, dr: (dr, j, 0, 0))
    return pl.pallas_call(
        functools.partial(_s5_kernel, rows=rows, rows_ctx=rows_ctx),
        grid=(b, n_blk, n_dir),
        in_specs=[pl.BlockSpec((None, ttot, LANES), lambda bi, j, dr: (bi, 0, j)),
                  wspec(m), wspec(wst), wspec(wout), wspec(lam_l),
                  pl.BlockSpec((1, LANES), lambda bi, j, dr: (0, j))],
        out_specs=pl.BlockSpec((None, ttot, LANES), lambda bi, j, dr: (bi, 0, j)),
        out_shape=jax.ShapeDtypeStruct((b, ttot, width), F32),
        scratch_shapes=[pltpu.VMEM((rows, lam_l.shape[3]), F32), pltpu.VMEM((rows, lam_l.shape[3]), F32)],
        compiler_params=_cparams(("parallel", "parallel", "arbitrary")),
        name="s5_bidir",
    )(z, m, wst, wout, lam_l, dvec.reshape(1, width))


def _gla_kernel(q_ref, f_ref, v_ref, lb_ref, o_ref, st_ref, *, lc, reverse):
    step = pl.program_id(1)

    @pl.when(step == 0)
    def _():
        st_ref[...] = jnp.zeros_like(st_ref)

    c = HG_CHUNK
    r_i = lax.broadcasted_iota(jnp.int32, (c, c), 0)
    c_i = lax.broadcasted_iota(jnp.int32, (c, c), 1)
    keep = (c_i >= r_i) if reverse else (c_i <= r_i)
    tri = jnp.where(keep, 1.0, 0.0).astype(BF16)
    lb = lb_ref[...]
    n_chunks = lc // c
    order = range(n_chunks - 1, -1, -1) if reverse else range(n_chunks)
    states = [st_ref[h] for h in range(HG_HEADS)]
    for ci in order:
        rows = pl.ds(ci * c, c)
        qraw = q_ref[rows, :]
        q = _silu(qraw)
        f = lb + (1.0 - lb) / (1.0 + jnp.exp(-f_ref[rows, :]))
        k = 1.0 - f
        logf = jnp.log(f)
        v = v_ref[rows, :].astype(BF16)
        hi = logf.astype(BF16)
        r1 = logf - hi.astype(F32)
        mid = r1.astype(BF16)
        lo = (r1 - mid.astype(F32)).astype(BF16)
        cum = (jnp.dot(tri, hi, preferred_element_type=F32) + jnp.dot(tri, mid, preferred_element_type=F32)
               + jnp.dot(tri, lo, preferred_element_type=F32))
        total = cum[0:1, :] if reverse else cum[c - 1:c, :]
        mid_ref = cum[c // 2:c // 2 + 1, :]
        q_in = (q * jnp.exp(cum)).astype(BF16)
        k_out = (k * jnp.exp(total - cum)).astype(BF16)
        q_loc = (q * jnp.exp(cum - mid_ref)).astype(BF16)
        k_loc = (k * jnp.exp(mid_ref - cum)).astype(BF16)
        decay = jnp.exp(total)
        for h in range(HG_HEADS):
            sl = slice(h * HG_DK, (h + 1) * HG_DK)
            st = states[h]
            o = lax.dot_general(q_in[:, sl], st.astype(BF16), (((1,), (1,)), ((), ())),
                                preferred_element_type=F32)
            att = lax.dot_general(q_loc[:, sl], k_loc[:, sl], (((1,), (1,)), ((), ())),
                                  preferred_element_type=F32)
            att = jnp.where(keep, att, 0.0)
            o = o + jnp.dot(att.astype(BF16), v[:, sl], preferred_element_type=F32)
            o_ref[rows, sl] = o
            states[h] = st * decay[:, sl] + lax.dot_general(
                v[:, sl], k_out[:, sl], (((0,), (0,)), ((), ())), preferred_element_type=F32)
    for h in range(HG_HEADS):
        st_ref[h] = states[h]


def _gla(z, lb, tc, fcol, reverse):
    b, ttot, _ = z.shape
    w = HG_HEADS * HG_DK
    lc = SEQ_BLOCK
    n_all, n_ctx = ttot // lc, tc // lc

    def blk(s):
        return _seq_block(s, n_ctx, n_all, reverse)

    col = lambda cidx: pl.BlockSpec((None, lc, w), lambda bi, s: (bi, blk(s), cidx))
    return pl.pallas_call(
        functools.partial(_gla_kernel, lc=lc, reverse=reverse),
        grid=(b, n_all),
        in_specs=[col(1), col(fcol), col(4), pl.BlockSpec((1, w), lambda bi, s: (0, 0))],
        out_specs=pl.BlockSpec((None, lc, w), lambda bi, s: (bi, blk(s), 0)),
        out_shape=jax.ShapeDtypeStruct((b, ttot, w), F32),
        scratch_shapes=[pltpu.VMEM((HG_HEADS, HG_DK, HG_DK), F32)],
        compiler_params=_cparams(("parallel", "arbitrary")),
        name="hgrn2_rev" if reverse else "hgrn2_fwd",
    )(z, z, z, lb.reshape(1, w))


def _final_kernel(x_ref, g_ref, o_ref):
    x = x_ref[...]
    ms = jnp.mean(x * x, axis=-1, keepdims=True)
    o_ref[...] = x * lax.rsqrt(ms + EPS) * g_ref[...]


def _final_norm(xa, g, tc):
    b, ttot, d = xa.shape
    t = ttot - tc
    tm = SEQ_BLOCK
    off = tc // tm
    return pl.pallas_call(
        _final_kernel,
        grid=(b, t // tm),
        in_specs=[pl.BlockSpec((None, tm, d), lambda bi, i: (bi, i + off, 0)),
                  pl.BlockSpec((1, d), lambda bi, i: (0, 0))],
        out_specs=pl.BlockSpec((None, tm, d), lambda bi, i: (bi, i, 0)),
        out_shape=jax.ShapeDtypeStruct((b, t, d), F32),
        compiler_params=_cparams(("parallel", "parallel")),
        name="final_norm",
    )(xa, g.reshape(1, d))


def _rope_tables(t, tc):
    pos = jnp.arange(t)
    row = (pos // GRID_W).astype(F32)
    col = (pos % GRID_W).astype(F32)
    inv = ROPE_BASE ** (-jnp.arange(ROPE_FREQS, dtype=F32) / ROPE_FREQS)
    ar, ac = row[:, None] * inv, col[:, None] * inv
    cos = jnp.concatenate([jnp.cos(ar), jnp.cos(ar), jnp.cos(ac), jnp.cos(ac)], axis=1)
    sin = jnp.concatenate([-jnp.sin(ar), jnp.sin(ar), -jnp.sin(ac), jnp.sin(ac)], axis=1)
    cos = jnp.concatenate([jnp.ones((tc, HEAD_DIM), F32), cos], axis=0)
    sin = jnp.concatenate([jnp.zeros((tc, HEAD_DIM), F32), sin], axis=0)
    return jnp.tile(cos, (1, LANES // HEAD_DIM)), jnp.tile(sin, (1, LANES // HEAD_DIM))


def _block_diag_dense(w):
    nblk, h, k = w.shape
    return jnp.einsum('nhk,nm->nhmk', w, jnp.eye(nblk, dtype=w.dtype)).reshape(nblk * h, nblk * k)


def kernel(x, c, ctx, c_ctx, w_mod, b_mod, norm_mix, norm_ffn, final_norm, w_in_ab, lru_conv_w, lru_conv_b, lru_wa, lru_ba, lru_wx, lru_bx, lru_lam, attn_sink, w_out_ab, ffn_w1, ffn_w3, ffn_w2, w_in_cd, s5_a_re, s5_a_im, s5_log_step, s5_b_re, s5_b_im, s5_c_re, s5_c_im, s5_d, s5_w_glu, s5_b_glu, hg_lb_raw, hg_norm, w_out_cd, moe_router, moe_w1, moe_w3, moe_w2):
    nb, t, d = x.shape
    tc = ctx.shape[1]
    depth = w_mod.shape[0]
    assert tc % SEQ_BLOCK == 0 and t % SEQ_BLOCK == 0 and t % GRID_W == 0

    xa = jnp.concatenate([ctx, x], axis=1)
    mod_rows = -(-(nb + 1) // SUBLANES) * SUBLANES
    cvec = jnp.zeros((mod_rows, d), F32).at[:nb].set(c).at[nb].set(c_ctx)
    mod = _modulation(cvec, w_mod, b_mod).reshape(depth, mod_rows, N_MOD, d)

    cos_t, sin_t = _rope_tables(t, tc)
    lb_soft = jax.nn.softmax(hg_lb_raw.astype(F32), axis=0)
    lb_table = jnp.cumsum(lb_soft, axis=0) - lb_soft[0:1]
    n_exp = moe_router.shape[2]

    for l in range(depth):
        j = l // 2
        mod_l = mod[l]
        if l % 2 == 0:
            z = _inproj(xa, mod_l, norm_mix[l], w_in_ab[j].astype(BF16), tc, nb)
            sp = jax.nn.softplus(-lru_lam[j])
            lru = []
            for dr, rev in enumerate((False, True)):
                lru.append(_lru(z, lru_conv_w[j], lru_conv_b[j],
                                _block_diag_dense(lru_wa[j, dr]).astype(BF16), lru_ba[j, dr],
                                _block_diag_dense(lru_wx[j, dr]).astype(BF16), lru_bx[j, dr],
                                sp[dr], tc, rev))
            sink = jnp.zeros((1, LANES), F32).at[0, :ATT_HEADS].set(attn_sink[j])
            att = _attention(z, cos_t, sin_t, sink, tc)
            xa = _outproj_ab(xa, mod_l, lru[0], lru[1], z, att, w_out_ab[j].astype(BF16), tc, nb)
            xa = _ffn(xa, mod_l, norm_ffn[l], ffn_w1[j].astype(BF16), ffn_w3[j].astype(BF16),
                      ffn_w2[j].astype(BF16), tc, nb)
        else:
            z = _inproj(xa, mod_l, norm_mix[l], w_in_cd[j].astype(BF16), tc, nb)
            s5w = _s5_weights(s5_a_re[j], s5_a_im[j], s5_log_step[j], s5_b_re[j], s5_b_im[j],
                              s5_c_re[j], s5_c_im[j])
            y5 = _s5(z, s5w, s5_d[j], tc, s5_d.shape[1])
            o_f = _gla(z, lb_table[j], tc, 2, False)
            o_b = _gla(z, lb_table[j], tc, 3, True)
            xa = _outproj_cd(xa, mod_l, y5, o_f, o_b, z, s5_w_glu[j].astype(BF16), s5_b_glu[j],
                             hg_norm[j], w_out_cd[j].astype(BF16), tc, nb)
            router_t = jnp.zeros((SUBLANES, d), F32).at[:n_exp].set(moe_router[j].T)
            xa = _moe(xa, mod_l, norm_ffn[l], router_t, moe_w1[j].astype(BF16),
                      moe_w3[j].astype(BF16), moe_w2[j].astype(BF16), tc, nb)
    return _final_norm(xa, final_norm, tc)
```

```python
import functools
import math

import jax
import jax.numpy as jnp
from jax import lax
from jax.experimental import pallas as pl
from jax.experimental.pallas import tpu as pltpu

F32 = jnp.float32
BF16 = jnp.bfloat16

EPS = 1e-6
GRID_W = 64
LRU_BLOCKS = 8
LRU_C = 8.0
CONV_W = 4
ATT_HEADS = 8
ATT_KV_HEADS = 2
ATT_GROUP = ATT_HEADS // ATT_KV_HEADS
HEAD_DIM = 64
WINDOW = 128
BLOCK_Q = 128
ROPE_FREQS = HEAD_DIM // 4
ROPE_BASE = 10000.0
S5_GROUP = 16
S5_STATE = 64
S5_CHUNK = 8
HG_HEADS = 4
HG_DK = 128
HG_CHUNK = 64
SEQ_BLOCK = 256
N_MOD = 6
MOE_CAP = 224
LANES = 128
SUBLANES = 8
VMEM_LIMIT = 56 * 1024 * 1024
NEG = -1e30
LOG2E = math.log2(math.e)


def _cparams(sem):
    return pltpu.CompilerParams(dimension_semantics=sem, vmem_limit_bytes=VMEM_LIMIT)


def _row_tile(ttot):
    for tm in (768, 1024, 512, 256):
        if ttot % tm == 0:
            return tm
    raise ValueError(f"unsupported token count {ttot}")


def _sigmoid(x):
    return 0.5 * jnp.tanh(0.5 * x) + 0.5


def _silu(x):
    return x * _sigmoid(x)


def _gelu(x):
    return 0.5 * x * (1.0 + jnp.tanh(math.sqrt(2.0 / math.pi) * (x + 0.044715 * (x * x * x))))


def _norm_mod(x, g, mb, mc, row0, tc, shift_idx, scale_idx):
    ms = jnp.mean(x * x, axis=-1, keepdims=True)
    y = x * lax.rsqrt(ms + EPS) * g
    rows = row0 + lax.broadcasted_iota(jnp.int32, (x.shape[0], 1), 0)
    is_ctx = rows < tc
    scale = jnp.where(is_ctx, mc[scale_idx:scale_idx + 1], mb[scale_idx:scale_idx + 1])
    shift = jnp.where(is_ctx, mc[shift_idx:shift_idx + 1], mb[shift_idx:shift_idx + 1])
    return y * (1.0 + scale) + shift


def _gate_mod(mb, mc, idx, row0, tc, n):
    rows = row0 + lax.broadcasted_iota(jnp.int32, (n, 1), 0)
    return jnp.where(rows < tc, mc[idx:idx + 1], mb[idx:idx + 1])


def _mod_kernel(c_ref, w_ref, b_ref, o_ref):
    s = _silu(c_ref[...])
    o_ref[...] = jnp.dot(s, w_ref[...], preferred_element_type=F32,
                         precision=lax.Precision.HIGHEST) + b_ref[...]


def _modulation(cvec, w_mod, b_mod):
    depth, d, n = w_mod.shape
    tn = 1536 if n % 1536 == 0 else n
    rows = cvec.shape[0]
    return pl.pallas_call(
        _mod_kernel,
        grid=(depth, n // tn),
        in_specs=[pl.BlockSpec((rows, d), lambda l, j: (0, 0)),
                  pl.BlockSpec((None, d, tn), lambda l, j: (l, 0, j)),
                  pl.BlockSpec((None, 1, tn), lambda l, j: (l, 0, j))],
        out_specs=pl.BlockSpec((None, rows, tn), lambda l, j: (l, 0, j)),
        out_shape=jax.ShapeDtypeStruct((depth, rows, n), F32),
        compiler_params=_cparams(("arbitrary", "arbitrary")),
        name="modulation",
    )(cvec, w_mod, b_mod.reshape(depth, 1, n))


def _inproj_kernel(x_ref, mb_ref, mc_ref, g_ref, w_ref, o_ref, *, tm, tc):
    i = pl.program_id(1)
    h = _norm_mod(x_ref[...], g_ref[...], mb_ref[...], mc_ref[...], i * tm, tc, 0, 1)
    o_ref[...] = jnp.dot(h.astype(BF16), w_ref[...], preferred_element_type=F32)


def _inproj(xa, mod_l, g, w, tc, nb):
    b, ttot, d = xa.shape
    n = w.shape[1]
    tm = _row_tile(ttot)
    return pl.pallas_call(
        functools.partial(_inproj_kernel, tm=tm, tc=tc),
        grid=(b, ttot // tm),
        in_specs=[pl.BlockSpec((None, tm, d), lambda bi, i: (bi, i, 0)),
                  pl.BlockSpec((None, N_MOD, d), lambda bi, i: (bi, 0, 0)),
                  pl.BlockSpec((None, N_MOD, d), lambda bi, i: (nb, 0, 0)),
                  pl.BlockSpec((1, d), lambda bi, i: (0, 0)),
                  pl.BlockSpec((d, n), lambda bi, i: (0, 0))],
        out_specs=pl.BlockSpec((None, tm, n), lambda bi, i: (bi, i, 0)),
        out_shape=jax.ShapeDtypeStruct((b, ttot, n), F32),
        compiler_params=_cparams(("parallel", "parallel")),
        name="inproj",
    )(xa, mod_l, mod_l, g.reshape(1, d), w)


def _seq_block(step, n_ctx, n_all, reverse):
    if not reverse:
        return step
    return jnp.where(step < n_ctx, n_ctx - 1 - step, n_all - 1 - step + n_ctx)


def _lru_kernel(u_ref, hp_ref, hn_ref, cw_ref, cb_ref, wa_ref, ba_ref, wx_ref, bx_ref, sp_ref,
                o_ref, uc_ref, a_ref, b_ref, h_ref, *, lc, n_ctx, n_all, reverse):
    step = pl.program_id(1)
    blk = _seq_block(step, n_ctx, n_all, reverse)
    has_prev = jnp.logical_and(blk != 0, blk != n_ctx)
    has_next = jnp.logical_and(blk != n_ctx - 1, blk != n_all - 1)

    @pl.when(step == 0)
    def _():
        h_ref[...] = jnp.zeros_like(h_ref)

    w0, w1, w2, w3 = (cw_ref[j:j + 1, :] for j in range(CONV_W))
    cb = cb_ref[...]
    u = u_ref[...]
    uc_ref[...] = (cb + pltpu.roll(u, 2, 0) * w0 + pltpu.roll(u, 1, 0) * w1 + u * w2
                   + pltpu.roll(u, lc - 1, 0) * w3)
    row = lax.broadcasted_iota(jnp.int32, (SUBLANES, 1), 0)
    hp = jnp.where(has_prev, hp_ref[...], 0.0)
    hn = jnp.where(has_next, hn_ref[...], 0.0)
    u0 = u_ref[0:SUBLANES, :]
    u1 = u_ref[SUBLANES:2 * SUBLANES, :]
    uc_ref[0:SUBLANES, :] = (
        cb + jnp.where(row < 2, pltpu.roll(hp, 2, 0), pltpu.roll(u0, 2, 0)) * w0
        + jnp.where(row < 1, pltpu.roll(hp, 1, 0), pltpu.roll(u0, 1, 0)) * w1 + u0 * w2
        + jnp.where(row < SUBLANES - 1, pltpu.roll(u0, SUBLANES - 1, 0),
                    pltpu.roll(u1, SUBLANES - 1, 0)) * w3)
    ul = u_ref[lc - SUBLANES:lc, :]
    um = u_ref[lc - 2 * SUBLANES:lc - SUBLANES, :]
    uc_ref[lc - SUBLANES:lc, :] = (
        cb + jnp.where(row < 2, pltpu.roll(um, 2, 0), pltpu.roll(ul, 2, 0)) * w0
        + jnp.where(row < 1, pltpu.roll(um, 1, 0), pltpu.roll(ul, 1, 0)) * w1 + ul * w2
        + jnp.where(row < SUBLANES - 1, pltpu.roll(ul, SUBLANES - 1, 0),
                    pltpu.roll(hn, SUBLANES - 1, 0)) * w3)

    uc = uc_ref[...]
    ub = uc.astype(BF16)
    r = _sigmoid(jnp.dot(ub, wa_ref[...], preferred_element_type=F32) + ba_ref[...])
    gi = _sigmoid(jnp.dot(ub, wx_ref[...], preferred_element_type=F32) + bx_ref[...])
    log_a = -LRU_C * r * sp_ref[...]
    a = jnp.exp(log_a)
    b = jnp.sqrt(1.0 - jnp.exp(2.0 * log_a)) * (gi * uc)
    sub = lax.broadcasted_iota(jnp.int32, (lc, 1), 0) % SUBLANES
    dist = 1
    while dist < SUBLANES:
        inside = (sub < SUBLANES - dist) if reverse else (sub >= dist)
        shift = (lc - dist) if reverse else dist
        a_sh = jnp.where(inside, pltpu.roll(a, shift, 0), 1.0)
        b_sh = jnp.where(inside, pltpu.roll(b, shift, 0), 0.0)
        b = a * b_sh + b
        a = a * a_sh
        dist *= 2
    a_ref[...] = a
    b_ref[...] = b
    n_grp = lc // SUBLANES

    def body(i, carry):
        t = (n_grp - 1 - i) if reverse else i
        rows = pl.ds(pl.multiple_of(t * SUBLANES, SUBLANES), SUBLANES)
        h = a_ref[rows, :] * carry + b_ref[rows, :]
        o_ref[rows, :] = h
        return h[0:1, :] if reverse else h[SUBLANES - 1:SUBLANES, :]

    h_ref[0:1, :] = lax.fori_loop(0, n_grp, body, h_ref[0:1, :], unroll=4)


def _lru(z, conv_w, conv_b, wa_bd, ba, wx_bd, bx, sp, tc, reverse):
    b, ttot, _ = z.shape
    w = conv_w.shape[1]
    lc = SEQ_BLOCK
    n_all, n_ctx = ttot // lc, tc // lc
    hb = lc // SUBLANES
    n_h = ttot // SUBLANES
    col = 1

    def blk(s):
        return _seq_block(s, n_ctx, n_all, reverse)

    vec = lambda: pl.BlockSpec((1, w), lambda bi, s: (0, 0))
    return pl.pallas_call(
        functools.partial(_lru_kernel, lc=lc, n_ctx=n_ctx, n_all=n_all, reverse=reverse),
        grid=(b, n_all),
        in_specs=[pl.BlockSpec((None, lc, w), lambda bi, s: (bi, blk(s), col)),
                  pl.BlockSpec((None, SUBLANES, w),
                               lambda bi, s: (bi, jnp.maximum(blk(s) * hb - 1, 0), col)),
                  pl.BlockSpec((None, SUBLANES, w),
                               lambda bi, s: (bi, jnp.minimum((blk(s) + 1) * hb, n_h - 1), col)),
                  pl.BlockSpec((CONV_W, w), lambda bi, s: (0, 0)),
                  vec(),
                  pl.BlockSpec((w, w), lambda bi, s: (0, 0)),
                  vec(),
                  pl.BlockSpec((w, w), lambda bi, s: (0, 0)),
                  vec(), vec()],
        out_specs=pl.BlockSpec((None, lc, w), lambda bi, s: (bi, blk(s), 0)),
        out_shape=jax.ShapeDtypeStruct((b, ttot, w), F32),
        scratch_shapes=[pltpu.VMEM((lc, w), F32), pltpu.VMEM((lc, w), F32),
                        pltpu.VMEM((lc, w), F32), pltpu.VMEM((SUBLANES, w), F32)],
        compiler_params=_cparams(("parallel", "arbitrary")),
        name="rglru_rev" if reverse else "rglru_fwd",
    )(z, z, z, conv_w, conv_b.reshape(1, w), wa_bd, ba.reshape(1, w), wx_bd, bx.reshape(1, w),
      sp.reshape(1, w))


def _rope(x, cos, sin_signed):
    lane = lax.broadcasted_iota(jnp.int32, x.shape, 1)
    swapped = jnp.where(lane % 32 < 16, pltpu.roll(x, LANES - 16, 1), pltpu.roll(x, 16, 1))
    return x * cos + swapped * sin_signed


def _pair_layout(x, x_sw, head):
    lo = lax.broadcasted_iota(jnp.int32, x.shape, 1) < HEAD_DIM
    first, second = (x, x_sw) if head == 0 else (x_sw, x)
    return jnp.concatenate([jnp.where(lo, first, 0.0), jnp.where(lo, 0.0, second)],
                           axis=0).astype(BF16)


def _fold_lanes(x, op):
    acc = x[:, 0:LANES]
    for j in range(1, x.shape[1] // LANES):
        acc = op(acc, x[:, j * LANES:(j + 1) * LANES])
    return acc


def _attend(q_block, keys, vals, bias, sink_row, o_ref):
    nk = keys.shape[0]
    keys_sw = pltpu.roll(keys, HEAD_DIM, 1)
    vals_sw = pltpu.roll(vals, HEAD_DIM, 1)
    lo = lax.broadcasted_iota(jnp.int32, (BLOCK_Q, LANES), 1) < HEAD_DIM
    pairs_per_kv = ATT_GROUP // 2
    for kv in range(ATT_KV_HEADS):
        kab = _pair_layout(keys, keys_sw, kv)
        vab = _pair_layout(vals, vals_sw, kv)
        for g in range(kv * pairs_per_kv, (kv + 1) * pairs_per_kv):
            s = lax.dot_general(q_block(g), kab, (((1,), (1,)), ((), ())),
                                preferred_element_type=F32)
            halves = []
            recips = []
            for half in range(2):
                sh = s[:, half * nk:(half + 1) * nk]
                if bias is not None:
                    sh = sh + bias
                sk = sink_row[:, 2 * g + half:2 * g + half + 1]
                m = jnp.maximum(jnp.max(_fold_lanes(sh, jnp.maximum), axis=-1, keepdims=True), sk)
                p = jnp.exp2(sh - m)
                recips.append(1.0 / (jnp.sum(_fold_lanes(p, jnp.add), axis=-1, keepdims=True)
                                     + jnp.exp2(sk - m)))
                halves.append(p.astype(BF16))
            o = jnp.dot(jnp.concatenate(halves, axis=1), vab, preferred_element_type=F32)
            o_ref[:, g * LANES:(g + 1) * LANES] = o * jnp.where(lo, recips[0], recips[1])


def _attn_kernel(q_ref, kp_ref, k0_ref, kn_ref, vp_ref, v0_ref, vn_ref, kc_ref, vc_ref,
                 cq_ref, sq_ref, cp_ref, sp_ref, cn_ref, sn_ref, sink_ref, o_ref, *, n_ctx, n_lat):
    i = pl.program_id(1)
    qscale = HEAD_DIM ** -0.5 * LOG2E
    sink_row = sink_ref[...] * LOG2E

    @pl.when(i < n_ctx)
    def _():
        def q_block(g):
            return (q_ref[:, g * LANES:(g + 1) * LANES] * qscale).astype(BF16)
        _attend(q_block, kc_ref[...], vc_ref[...], None, sink_row, o_ref)

    @pl.when(i >= n_ctx)
    def _():
        n = i - n_ctx
        cq, sq = cq_ref[...], sq_ref[...]

        def q_block(g):
            return (_rope(q_ref[:, g * LANES:(g + 1) * LANES], cq, sq) * qscale).astype(BF16)

        kp = _rope(kp_ref[...], cp_ref[...], sp_ref[...])
        k0 = _rope(k0_ref[...], cq, sq)
        kn = _rope(kn_ref[...], cn_ref[...], sn_ref[...])
        keys = jnp.concatenate([kp, k0, kn, kc_ref[...]], axis=0)
        vals = jnp.concatenate([vp_ref[...], v0_ref[...], vn_ref[...], vc_ref[...]], axis=0)
        nk = keys.shape[0]
        qpos = n * BLOCK_Q + lax.broadcasted_iota(jnp.int32, (BLOCK_Q, nk), 0)
        col = lax.broadcasted_iota(jnp.int32, (BLOCK_Q, nk), 1)
        kpos = (n - 1) * BLOCK_Q + col
        local = (jnp.abs(qpos - kpos) <= WINDOW) & (kpos >= 0) & (kpos < n_lat * BLOCK_Q)
        bias = jnp.where(local | (col >= 3 * BLOCK_Q), 0.0, NEG)
        _attend(q_block, keys, vals, bias, sink_row, o_ref)


def _attention(z, cos_t, sin_t, sink, tc):
    b, ttot, _ = z.shape
    n_all, n_ctx = ttot // BLOCK_Q, tc // BLOCK_Q
    n_lat = n_all - n_ctx
    kcol = (2 * 512 + ATT_HEADS * HEAD_DIM) // LANES
    vcol = kcol + 1
    qcol = 2 * 512 // (ATT_HEADS * HEAD_DIM)

    def prev(i):
        return jnp.maximum(i - 1, 0)

    def nxt(i):
        return jnp.minimum(i + 1, n_all - 1)

    def rows(col, f):
        return pl.BlockSpec((None, BLOCK_Q, LANES), lambda bi, i: (bi, f(i), col))

    def tab(f):
        return pl.BlockSpec((BLOCK_Q, LANES), lambda bi, i: (f(i), 0))

    same = lambda i: i
    return pl.pallas_call(
        functools.partial(_attn_kernel, n_ctx=n_ctx, n_lat=n_lat),
        grid=(b, n_all),
        in_specs=[pl.BlockSpec((None, BLOCK_Q, ATT_HEADS * HEAD_DIM), lambda bi, i: (bi, i, qcol)),
                  rows(kcol, prev), rows(kcol, same), rows(kcol, nxt),
                  rows(vcol, prev), rows(vcol, same), rows(vcol, nxt),
                  pl.BlockSpec((None, tc, LANES), lambda bi, i: (bi, 0, kcol)),
                  pl.BlockSpec((None, tc, LANES), lambda bi, i: (bi, 0, vcol)),
                  tab(same), tab(same), tab(prev), tab(prev), tab(nxt), tab(nxt),
                  pl.BlockSpec((1, LANES), lambda bi, i: (0, 0))],
        out_specs=pl.BlockSpec((None, BLOCK_Q, ATT_HEADS * HEAD_DIM), lambda bi, i: (bi, i, 0)),
        out_shape=jax.ShapeDtypeStruct((b, ttot, ATT_HEADS * HEAD_DIM), F32),
        compiler_params=_cparams(("parallel", "parallel")),
        name="window_gqa",
    )(z, z, z, z, z, z, z, z, z, cos_t, sin_t, cos_t, sin_t, cos_t, sin_t, sink)


def _outproj_ab_kernel(x_ref, mb_ref, mc_ref, lf_ref, lb_ref, g_ref, at_ref, w_ref, o_ref, *, tm, tc):
    i = pl.program_id(1)
    lru = (lf_ref[...] + lb_ref[...]) * _gelu(g_ref[...])
    mix = jnp.concatenate([lru.astype(BF16), at_ref[...].astype(BF16)], axis=1)
    dx = jnp.dot(mix, w_ref[...], preferred_element_type=F32)
    gate = _gate_mod(mb_ref[...], mc_ref[...], 2, i * tm, tc, tm)
    o_ref[...] = x_ref[...] + gate * dx


def _outproj_ab(xa, mod_l, lru_f, lru_b, z, att, w_out, tc, nb):
    b, ttot, d = xa.shape
    w = lru_f.shape[2]
    tm = _row_tile(ttot)
    tile = lambda width, col: pl.BlockSpec((None, tm, width), lambda bi, i: (bi, i, col))
    return pl.pallas_call(
        functools.partial(_outproj_ab_kernel, tm=tm, tc=tc),
        grid=(b, ttot // tm),
        in_specs=[tile(d, 0),
                  pl.BlockSpec((None, N_MOD, d), lambda bi, i: (bi, 0, 0)),
                  pl.BlockSpec((None, N_MOD, d), lambda bi, i: (nb, 0, 0)),
                  tile(w, 0), tile(w, 0), tile(w, 0), tile(att.shape[2], 0),
                  pl.BlockSpec(w_out.shape, lambda bi, i: (0, 0))],
        out_specs=tile(d, 0),
        out_shape=jax.ShapeDtypeStruct(xa.shape, F32),
        compiler_params=_cparams(("parallel", "parallel")),
        name="outproj_ab",
    )(xa, mod_l, mod_l, lru_f, lru_b, z, att, w_out)


def _outproj_cd_kernel(x_ref, mb_ref, mc_ref, y_ref, of_ref, ob_ref, g_ref, wg_ref, bg_ref, gn_ref,
                       w_ref, o_ref, *, tm, tc):
    i = pl.program_id(1)
    y = _gelu(y_ref[...])
    s5 = y * _sigmoid(jnp.dot(y.astype(BF16), wg_ref[...], preferred_element_type=F32) + bg_ref[...])
    o = of_ref[...] + ob_ref[...]
    parts = []
    for h in range(HG_HEADS):
        oh = o[:, h * HG_DK:(h + 1) * HG_DK]
        ms = jnp.mean(oh * oh, axis=-1, keepdims=True)
        parts.append(oh * lax.rsqrt(ms + EPS) * gn_ref[...])
    hg = jnp.concatenate(parts, axis=1) * _silu(g_ref[...])
    mix = jnp.concatenate([s5.astype(BF16), hg.astype(BF16)], axis=1)
    dx = jnp.dot(mix, w_ref[...], preferred_element_type=F32)
    gate = _gate_mod(mb_ref[...], mc_ref[...], 2, i * tm, tc, tm)
    o_ref[...] = x_ref[...] + gate * dx


def _outproj_cd(xa, mod_l, y5, o_f, o_b, z, w_glu, b_glu, hg_norm, w_out, tc, nb):
    b, ttot, d = xa.shape
    w = y5.shape[2]
    tm = _row_tile(ttot)
    gcol = z.shape[2] // w - 1
    tile = lambda width, col: pl.BlockSpec((None, tm, width), lambda bi, i: (bi, i, col))
    return pl.pallas_call(
        functools.partial(_outproj_cd_kernel, tm=tm, tc=tc),
        grid=(b, ttot // tm),
        in_specs=[tile(d, 0),
                  pl.BlockSpec((None, N_MOD, d), lambda bi, i: (bi, 0, 0)),
                  pl.BlockSpec((None, N_MOD, d), lambda bi, i: (nb, 0, 0)),
                  tile(w, 0), tile(w, 0), tile(w, 0), tile(w, gcol),
                  pl.BlockSpec(w_glu.shape, lambda bi, i: (0, 0)),
                  pl.BlockSpec((1, w), lambda bi, i: (0, 0)),
                  pl.BlockSpec((1, HG_DK), lambda bi, i: (0, 0)),
                  pl.BlockSpec(w_out.shape, lambda bi, i: (0, 0))],
        out_specs=tile(d, 0),
        out_shape=jax.ShapeDtypeStruct(xa.shape, F32),
        compiler_params=_cparams(("parallel", "parallel")),
        name="outproj_cd",
    )(xa, mod_l, mod_l, y5, o_f, o_b, z, w_glu, b_glu.reshape(1, w), hg_norm.reshape(1, HG_DK), w_out)


def _ffn_kernel(x_ref, mb_ref, mc_ref, g_ref, w1_ref, w3_ref, w2_ref, o_ref, h_ref, acc_ref, *, tm, tc):
    i = pl.program_id(1)
    k = pl.program_id(2)

    @pl.when(k == 0)
    def _():
        h = _norm_mod(x_ref[...], g_ref[...], mb_ref[...], mc_ref[...], i * tm, tc, 3, 4)
        h_ref[...] = h.astype(BF16)
        acc_ref[...] = jnp.zeros_like(acc_ref)

    h = h_ref[...]
    a = jnp.dot(h, w1_ref[...], preferred_element_type=F32)
    c = jnp.dot(h, w3_ref[...], preferred_element_type=F32)
    acc_ref[...] += jnp.dot((_silu(a) * c).astype(BF16), w2_ref[...], preferred_element_type=F32)

    @pl.when(k == pl.num_programs(2) - 1)
    def _():
        gate = _gate_mod(mb_ref[...], mc_ref[...], 5, i * tm, tc, tm)
        o_ref[...] = x_ref[...] + gate * acc_ref[...]


def _ffn(xa, mod_l, g, w1, w3, w2, tc, nb):
    b, ttot, d = xa.shape
    f = w1.shape[1]
    tm = _row_tile(ttot)
    tf = f // 2 if (f // 2) % LANES == 0 else f
    return pl.pallas_call(
        functools.partial(_ffn_kernel, tm=tm, tc=tc),
        grid=(b, ttot // tm, f // tf),
        in_specs=[pl.BlockSpec((None, tm, d), lambda bi, i, k: (bi, i, 0)),
                  pl.BlockSpec((None, N_MOD, d), lambda bi, i, k: (bi, 0, 0)),
                  pl.BlockSpec((None, N_MOD, d), lambda bi, i, k: (nb, 0, 0)),
                  pl.BlockSpec((1, d), lambda bi, i, k: (0, 0)),
                  pl.BlockSpec((d, tf), lambda bi, i, k: (0, k)),
                  pl.BlockSpec((d, tf), lambda bi, i, k: (0, k)),
                  pl.BlockSpec((tf, d), lambda bi, i, k: (k, 0))],
        out_specs=pl.BlockSpec((None, tm, d), lambda bi, i, k: (bi, i, 0)),
        out_shape=jax.ShapeDtypeStruct(xa.shape, F32),
        scratch_shapes=[pltpu.VMEM((tm, d), BF16), pltpu.VMEM((tm, d), F32)],
        compiler_params=_cparams(("parallel", "parallel", "arbitrary")),
        name="ffn_swiglu",
    )(xa, mod_l, mod_l, g.reshape(1, d), w1, w3, w2)


def _router_kernel(x_ref, mb_ref, mc_ref, g_ref, r_ref, h_ref, wt_ref, cnt_ref, *, tm, tc, n_exp):
    i = pl.program_id(1)
    h = _norm_mod(x_ref[...], g_ref[...], mb_ref[...], mc_ref[...], i * tm, tc, 3, 4)
    h_ref[...] = h.astype(BF16)
    logits = lax.dot_general(r_ref[...], h, (((1,), (1,)), ((), ())), preferred_element_type=F32,
                             precision=lax.Precision.HIGHEST)
    sub = lax.broadcasted_iota(jnp.int32, logits.shape, 0)
    logits = jnp.where(sub < n_exp, logits, NEG)
    m1 = jnp.max(logits, axis=0, keepdims=True)
    i1 = jnp.min(jnp.where(logits == m1, sub, SUBLANES), axis=0, keepdims=True)
    rest = jnp.where(sub == i1, NEG, logits)
    m2 = jnp.max(rest, axis=0, keepdims=True)
    i2 = jnp.min(jnp.where(rest == m2, sub, SUBLANES), axis=0, keepdims=True)
    e2 = jnp.exp(m2 - m1)
    g1 = 1.0 / (1.0 + e2)
    wt = jnp.where(sub == i1, g1, 0.0) + jnp.where(sub == i2, e2 * g1, 0.0)
    wt_ref[...] = wt
    cnt = jnp.sum(jnp.where(wt > 0.0, 1.0, 0.0), axis=1, keepdims=True)
    cnt_ref[...] = jnp.broadcast_to(cnt, cnt_ref.shape).astype(jnp.int32)


def _router(xa, mod_l, g, router_t, tc, nb, n_exp):
    b, ttot, d = xa.shape
    tm = _row_tile(ttot)
    n_t = ttot // tm
    return pl.pallas_call(
        functools.partial(_router_kernel, tm=tm, tc=tc, n_exp=n_exp),
        grid=(b, n_t),
        in_specs=[pl.BlockSpec((None, tm, d), lambda bi, i: (bi, i, 0)),
                  pl.BlockSpec((None, N_MOD, d), lambda bi, i: (bi, 0, 0)),
                  pl.BlockSpec((None, N_MOD, d), lambda bi, i: (nb, 0, 0)),
                  pl.BlockSpec((1, d), lambda bi, i: (0, 0)),
                  pl.BlockSpec((SUBLANES, d), lambda bi, i: (0, 0))],
        out_specs=[pl.BlockSpec((None, tm, d), lambda bi, i: (bi, i, 0)),
                   pl.BlockSpec((None, SUBLANES, tm), lambda bi, i: (bi, 0, i)),
                   pl.BlockSpec((None, None, SUBLANES, LANES), lambda bi, i: (bi, i, 0, 0))],
        out_shape=[jax.ShapeDtypeStruct((b, ttot, d), BF16),
                   jax.ShapeDtypeStruct((b, SUBLANES, ttot), F32),
                   jax.ShapeDtypeStruct((b, n_t, SUBLANES, LANES), jnp.int32)],
        compiler_params=_cparams(("parallel", "parallel")),
        name="moe_router",
    )(xa, mod_l, mod_l, g.reshape(1, d), router_t)


def _moe_kernel(cnt_ref, x_ref, mb_ref, mc_ref, h_ref, wt_ref, w1_ref, w3_ref, w2_ref, o_ref,
                rank_ref, acc_ref, *, tm, tc, n_exp, cap, n_t, group):
    e = pl.program_id(1)
    j = pl.program_id(2)
    tile = pl.program_id(0) * group + j
    i = tile % n_t

    @pl.when(e == 0)
    def _():
        acc_ref[j] = jnp.zeros(acc_ref.shape[1:], F32)
        sel = wt_ref[...] > 0.0
        r_i = lax.broadcasted_iota(jnp.int32, (tm, tm), 0)
        c_i = lax.broadcasted_iota(jnp.int32, (tm, tm), 1)
        before = jnp.where(r_i < c_i, 1.0, 0.0).astype(BF16)
        rank = jnp.dot(jnp.where(sel, 1.0, 0.0).astype(BF16), before, preferred_element_type=F32)
        rank_ref[j] = jnp.where(sel, rank, -1.0)

    n_e = cnt_ref[tile * n_exp + e]
    w_row = wt_ref[pl.ds(e, 1), :]
    r_row = rank_ref[j, pl.ds(e, 1), :]
    for blk in range(-(-tm // cap)):
        @pl.when(n_e > blk * cap)
        def _():
            slot = (lax.broadcasted_iota(jnp.int32, (cap, tm), 0) + blk * cap).astype(F32)
            hit = r_row == slot
            onehot = jnp.where(hit, 1.0, 0.0).astype(BF16)
            xe = jnp.dot(onehot, h_ref[...], preferred_element_type=F32).astype(BF16)
            gate = jnp.sum(jnp.where(hit, w_row, 0.0), axis=1, keepdims=True)
            a = jnp.dot(xe, w1_ref[...], preferred_element_type=F32)
            c = jnp.dot(xe, w3_ref[...], preferred_element_type=F32)
            y = jnp.dot((_silu(a) * c * gate).astype(BF16), w2_ref[...], preferred_element_type=F32)
            acc_ref[j] += lax.dot_general(onehot, y.astype(BF16), (((0,), (0,)), ((), ())),
                                          preferred_element_type=F32)

    @pl.when(e == n_exp - 1)
    def _():
        gate5 = _gate_mod(mb_ref[...], mc_ref[...], 5, i * tm, tc, tm)
        o_ref[...] = x_ref[...] + gate5 * acc_ref[j]


def _moe(xa, mod_l, g, router_t, w1, w3, w2, tc, nb):
    b, ttot, d = xa.shape
    n_exp, _, f = w1.shape
    tm = _row_tile(ttot)
    n_t = ttot // tm
    cap = MOE_CAP
    group = next(gsz for gsz in (4, 2, 1) if (b * n_t) % gsz == 0)
    h, wt, cnt = _router(xa, mod_l, g, router_t, tc, nb, n_exp)
    cnt = cnt[:, :, :n_exp, 0].reshape(-1)

    def cur(p, e, j, c):
        t = p * group + j
        return t // n_t, t % n_t

    def last(p, e, j, c):
        t = p * group + jnp.where(e == n_exp - 1, j, 0)
        return t // n_t, t % n_t

    grid_spec = pltpu.PrefetchScalarGridSpec(
        num_scalar_prefetch=1,
        grid=(b * n_t // group, n_exp, group),
        in_specs=[pl.BlockSpec((None, tm, d), lambda *a: (*last(*a), 0)),
                  pl.BlockSpec((None, N_MOD, d), lambda *a: (cur(*a)[0], 0, 0)),
                  pl.BlockSpec((None, N_MOD, d), lambda *a: (nb, 0, 0)),
                  pl.BlockSpec((None, tm, d), lambda *a: (*cur(*a), 0)),
                  pl.BlockSpec((None, SUBLANES, tm), lambda *a: (cur(*a)[0], 0, cur(*a)[1])),
                  pl.BlockSpec((None, d, f), lambda p, e, j, c: (e, 0, 0)),
                  pl.BlockSpec((None, d, f), lambda p, e, j, c: (e, 0, 0)),
                  pl.BlockSpec((None, f, d), lambda p, e, j, c: (e, 0, 0))],
        out_specs=pl.BlockSpec((None, tm, d), lambda *a: (*last(*a), 0)),
        scratch_shapes=[pltpu.VMEM((group, SUBLANES, tm), F32), pltpu.VMEM((group, tm, d), F32)])
    return pl.pallas_call(
        functools.partial(_moe_kernel, tm=tm, tc=tc, n_exp=n_exp, cap=cap, n_t=n_t, group=group),
        grid_spec=grid_spec,
        out_shape=jax.ShapeDtypeStruct(xa.shape, F32),
        compiler_params=_cparams(("parallel", "arbitrary", "arbitrary")),
        name="moe_experts",
    )(cnt, xa, mod_l, mod_l, h, wt, w1, w3, w2)


def _spread_groups(x, inner, row_group, n_grp):
    cols_in = x.shape[2]
    cols_out = cols_in * n_grp
    q = jnp.arange(cols_out)
    src = (q // (n_grp * inner)) * inner + q % inner
    tile = (jnp.arange(cols_in)[:, None] == src[None, :]).astype(BF16)
    col_group = (q // inner) % n_grp
    y = jnp.dot(x.astype(BF16), tile)
    return jnp.where(row_group[:, None] == col_group[None, :], y, jnp.zeros((), BF16))


def _s5_weights(a_re, a_im, log_step, b_re, b_im, c_re, c_im):
    L = S5_CHUNK
    lr = jnp.minimum(a_re, -1e-4)
    li = a_im
    dt = jnp.exp(log_step)[..., None]
    mag, ang = lr * dt, li * dt
    lbr, lbi = jnp.exp(mag) * jnp.cos(ang), jnp.exp(mag) * jnp.sin(ang)
    zr, zi = lbr - 1.0, lbi
    den = lr * lr + li * li
    fr = (zr * lr + zi * li) / den
    fi = (zi * lr - zr * li) / den
    bbr = fr[..., None] * b_re - fi[..., None] * b_im
    bbi = fr[..., None] * b_im + fi[..., None] * b_re

    def power(p):
        p = p[..., None, None, None].astype(F32)
        return jnp.exp(mag * p) * jnp.cos(ang * p), jnp.exp(mag * p) * jnp.sin(ang * p)

    n_dir, n_grp, n_st = a_re.shape
    gpb = LANES // S5_GROUP
    n_blk = n_grp // gpb
    s = jnp.arange(L)
    in_group = (jnp.arange(L * LANES) % LANES) // S5_GROUP
    st_group = jnp.arange(gpb * n_st) // n_st
    outs = []
    for d in range(n_dir):
        pr, pi = power(jnp.arange(L))
        pr, pi = pr[:, d], pi[:, d]
        cbr = (jnp.einsum('gcn,tgn,gnk->tgck', c_re[d], pr, bbr[d])
               - jnp.einsum('gcn,tgn,gnk->tgck', c_re[d], pi, bbi[d])
               - jnp.einsum('gcn,tgn,gnk->tgck', c_im[d], pr, bbi[d])
               - jnp.einsum('gcn,tgn,gnk->tgck', c_im[d], pi, bbr[d]))
        lag = (s[None, :] - s[:, None]) if d == 0 else (s[:, None] - s[None, :])
        kern = jnp.where((lag >= 0)[..., None, None, None], cbr[jnp.clip(lag, 0, L - 1)], 0.0)
        kern = kern.reshape(L, L, n_blk, gpb, S5_GROUP, S5_GROUP).transpose(2, 0, 3, 5, 1, 4)
        m = _spread_groups(kern.reshape(n_blk, L * LANES, L * S5_GROUP), S5_GROUP, in_group, gpb)
        qr, qi = power((L - 1 - s) if d == 0 else s)
        qr, qi = qr[:, d], qi[:, d]
        str_ = qr[..., None] * bbr[d] - qi[..., None] * bbi[d]
        sti = qr[..., None] * bbi[d] + qi[..., None] * bbr[d]

        def to_state(x):
            x = x.reshape(L, n_blk, gpb, n_st, S5_GROUP).transpose(1, 0, 2, 4, 3)
            return _spread_groups(x.reshape(n_blk, L * LANES, n_st), n_st, in_group, gpb)

        wst = jnp.concatenate([to_state(str_), to_state(sti)], axis=2)
        rr, ri = power((s + 1) if d == 0 else (L - s))
        rr, ri = rr[:, d], ri[:, d]
        wr = c_re[d][None] * rr[:, :, None, :] - c_im[d][None] * ri[:, :, None, :]
        wi = -(c_re[d][None] * ri[:, :, None, :] + c_im[d][None] * rr[:, :, None, :])

        def from_state(x):
            x = x.reshape(L, n_blk, gpb, S5_GROUP, n_st).transpose(1, 2, 4, 0, 3)
            return _spread_groups(x.reshape(n_blk, gpb * n_st, L * S5_GROUP), S5_GROUP, st_group, gpb)

        wout = jnp.concatenate([from_state(wr), from_state(wi)], axis=1)
        ler, lei = jnp.exp(mag[d] * L) * jnp.cos(ang[d] * L), jnp.exp(mag[d] * L) * jnp.sin(ang[d] * L)
        lam_l = jnp.concatenate([ler.reshape(n_blk, 1, gpb * n_st), lei.reshape(n_blk, 1, gpb * n_st)],
                                axis=2)
        outs.append((m, wst, wout, lam_l))
    return tuple(jnp.stack([o[k] for o in outs]) for k in range(4))


def _s5_kernel(u_ref, m_ref, wst_ref, wout_ref, lam_ref, d_ref, o_ref, x_ref, hp_ref, *, rows, rows_ctx):
    L = S5_CHUNK
    dr = pl.program_id(2)
    ns = lam_ref.shape[1] // 2
    u = jnp.concatenate([u_ref[pl.ds(s, rows, stride=L), :] for s in range(L)], axis=1)
    ub = u.astype(BF16)
    x_ref[...] = jnp.dot(ub, wst_ref[...], preferred_element_type=F32)
    lr = lam_ref[:, 0:ns]
    li = lam_ref[:, ns:2 * ns]

    def visit(r, carry):
        hr, hi = carry
        hp_ref[pl.ds(r, 1), 0:ns] = hr
        hp_ref[pl.ds(r, 1), ns:2 * ns] = hi
        xr = x_ref[pl.ds(r, 1), 0:ns]
        xi = x_ref[pl.ds(r, 1), ns:2 * ns]
        return lr * hr - li * hi + xr, lr * hi + li * hr + xi

    zero = (jnp.zeros((1, ns), F32), jnp.zeros((1, ns), F32))

    @pl.when(dr == 0)
    def _():
        lax.fori_loop(0, rows, visit, zero, unroll=4)

    @pl.when(dr == 1)
    def _():
        c = lax.fori_loop(0, rows_ctx, lambda t, c: visit(rows_ctx - 1 - t, c), zero, unroll=4)
        lax.fori_loop(0, rows - rows_ctx, lambda t, c: visit(rows - 1 - t, c), c, unroll=4)

    y = (jnp.dot(ub, m_ref[...], preferred_element_type=F32)
         + jnp.dot(hp_ref[...].astype(BF16), wout_ref[...], preferred_element_type=F32))

    @pl.when(dr == 0)
    def _():
        for s in range(L):
            o_ref[pl.ds(s, rows, stride=L), :] = (y[:, s * LANES:(s + 1) * LANES]
                                                  + d_ref[...] * u[:, s * LANES:(s + 1) * LANES])

    @pl.when(dr == 1)
    def _():
        for s in range(L):
            o_ref[pl.ds(s, rows, stride=L), :] += y[:, s * LANES:(s + 1) * LANES]


def _s5(z, weights, dvec, tc, width):
    b, ttot, _ = z.shape
    m, wst, wout, lam_l = weights
    n_dir, n_blk = m.shape[0], m.shape[1]
    rows, rows_ctx = ttot // S5_CHUNK, tc // S5_CHUNK
    wspec = lambda a: pl.BlockSpec((None, None) + a.shape[2:], lambda bi, j, dr: (dr, j, 0, 0))
    return pl.pallas_call(
        functools.partial(_s5_kernel, rows=rows, rows_ctx=rows_ctx),
        grid=(b, n_blk, n_dir),
        in_specs=[pl.BlockSpec((None, ttot, LANES), lambda bi, j, dr: (bi, 0, j)),
                  wspec(m), wspec(wst), wspec(wout), wspec(lam_l),
                  pl.BlockSpec((1, LANES), lambda bi, j, dr: (0, j))],
        out_specs=pl.BlockSpec((None, ttot, LANES), lambda bi, j, dr: (bi, 0, j)),
        out_shape=jax.ShapeDtypeStruct((b, ttot, width), F32),
        scratch_shapes=[pltpu.VMEM((rows, lam_l.shape[3]), F32), pltpu.VMEM((rows, lam_l.shape[3]), F32)],
        compiler_params=_cparams(("parallel", "parallel", "arbitrary")),
        name="s5_bidir",
    )(z, m, wst, wout, lam_l, dvec.reshape(1, width))


def _gla_kernel(q_ref, f_ref, v_ref, lb_ref, o_ref, st_ref, *, lc, reverse):
    step = pl.program_id(1)

    @pl.when(step == 0)
    def _():
        st_ref[...] = jnp.zeros_like(st_ref)

    c = HG_CHUNK
    r_i = lax.broadcasted_iota(jnp.int32, (c, c), 0)
    c_i = lax.broadcasted_iota(jnp.int32, (c, c), 1)
    keep = (c_i >= r_i) if reverse else (c_i <= r_i)
    tri = jnp.where(keep, 1.0, 0.0).astype(BF16)
    lb = lb_ref[...]
    n_chunks = lc // c
    order = range(n_chunks - 1, -1, -1) if reverse else range(n_chunks)
    states = [st_ref[h] for h in range(HG_HEADS)]
    for ci in order:
        rows = pl.ds(ci * c, c)
        qraw = q_ref[rows, :]
        q = _silu(qraw)
        f = lb + (1.0 - lb) / (1.0 + jnp.exp(-f_ref[rows, :]))
        k = 1.0 - f
        logf = jnp.log(f)
        v = v_ref[rows, :].astype(BF16)
        hi = logf.astype(BF16)
        r1 = logf - hi.astype(F32)
        mid = r1.astype(BF16)
        lo = (r1 - mid.astype(F32)).astype(BF16)
        cum = (jnp.dot(tri, hi, preferred_element_type=F32) + jnp.dot(tri, mid, preferred_element_type=F32)
               + jnp.dot(tri, lo, preferred_element_type=F32))
        total = cum[0:1, :] if reverse else cum[c - 1:c, :]
        mid_ref = cum[c // 2:c // 2 + 1, :]
        q_in = (q * jnp.exp(cum)).astype(BF16)
        k_out = (k * jnp.exp(total - cum)).astype(BF16)
        q_loc = (q * jnp.exp(cum - mid_ref)).astype(BF16)
        k_loc = (k * jnp.exp(mid_ref - cum)).astype(BF16)
        decay = jnp.exp(total)
        for h in range(HG_HEADS):
            sl = slice(h * HG_DK, (h + 1) * HG_DK)
            st = states[h]
            o = lax.dot_general(q_in[:, sl], st.astype(BF16), (((1,), (1,)), ((), ())),
                                preferred_element_type=F32)
            att = lax.dot_general(q_loc[:, sl], k_loc[:, sl], (((1,), (1,)), ((), ())),
                                  preferred_element_type=F32)
            att = jnp.where(keep, att, 0.0)
            o = o + jnp.dot(att.astype(BF16), v[:, sl], preferred_element_type=F32)
            o_ref[rows, sl] = o
            states[h] = st * decay[:, sl] + lax.dot_general(
                v[:, sl], k_out[:, sl], (((0,), (0,)), ((), ())), preferred_element_type=F32)
    for h in range(HG_HEADS):
        st_ref[h] = states[h]


def _gla(z, lb, tc, fcol, reverse):
    b, ttot, _ = z.shape
    w = HG_HEADS * HG_DK
    lc = SEQ_BLOCK
    n_all, n_ctx = ttot // lc, tc // lc

    def blk(s):
        return _seq_block(s, n_ctx, n_all, reverse)

    col = lambda cidx: pl.BlockSpec((None, lc, w), lambda bi, s: (bi, blk(s), cidx))
    return pl.pallas_call(
        functools.partial(_gla_kernel, lc=lc, reverse=reverse),
        grid=(b, n_all),
        in_specs=[col(1), col(fcol), col(4), pl.BlockSpec((1, w), lambda bi, s: (0, 0))],
        out_specs=pl.BlockSpec((None, lc, w), lambda bi, s: (bi, blk(s), 0)),
        out_shape=jax.ShapeDtypeStruct((b, ttot, w), F32),
        scratch_shapes=[pltpu.VMEM((HG_HEADS, HG_DK, HG_DK), F32)],
        compiler_params=_cparams(("parallel", "arbitrary")),
        name="hgrn2_rev" if reverse else "hgrn2_fwd",
    )(z, z, z, lb.reshape(1, w))


def _final_kernel(x_ref, g_ref, o_ref):
    x = x_ref[...]
    ms = jnp.mean(x * x, axis=-1, keepdims=True)
    o_ref[...] = x * lax.rsqrt(ms + EPS) * g_ref[...]


def _final_norm(xa, g, tc):
    b, ttot, d = xa.shape
    t = ttot - tc
    tm = SEQ_BLOCK
    off = tc // tm
    return pl.pallas_call(
        _final_kernel,
        grid=(b, t // tm),
        in_specs=[pl.BlockSpec((None, tm, d), lambda bi, i: (bi, i + off, 0)),
                  pl.BlockSpec((1, d), lambda bi, i: (0, 0))],
        out_specs=pl.BlockSpec((None, tm, d), lambda bi, i: (bi, i, 0)),
        out_shape=jax.ShapeDtypeStruct((b, t, d), F32),
        compiler_params=_cparams(("parallel", "parallel")),
        name="final_norm",
    )(xa, g.reshape(1, d))


def _rope_tables(t, tc):
    pos = jnp.arange(t)
    row = (pos // GRID_W).astype(F32)
    col = (pos % GRID_W).astype(F32)
    inv = ROPE_BASE ** (-jnp.arange(ROPE_FREQS, dtype=F32) / ROPE_FREQS)
    ar, ac = row[:, None] * inv, col[:, None] * inv
    cos = jnp.concatenate([jnp.cos(ar), jnp.cos(ar), jnp.cos(ac), jnp.cos(ac)], axis=1)
    sin = jnp.concatenate([-jnp.sin(ar), jnp.sin(ar), -jnp.sin(ac), jnp.sin(ac)], axis=1)
    cos = jnp.concatenate([jnp.ones((tc, HEAD_DIM), F32), cos], axis=0)
    sin = jnp.concatenate([jnp.zeros((tc, HEAD_DIM), F32), sin], axis=0)
    return jnp.tile(cos, (1, LANES // HEAD_DIM)), jnp.tile(sin, (1, LANES // HEAD_DIM))


def _block_diag_dense(w):
    nblk, h, k = w.shape
    return jnp.einsum('nhk,nm->nhmk', w, jnp.eye(nblk, dtype=w.dtype)).reshape(nblk * h, nblk * k)


def kernel(x, c, ctx, c_ctx, w_mod, b_mod, norm_mix, norm_ffn, final_norm, w_in_ab, lru_conv_w, lru_conv_b, lru_wa, lru_ba, lru_wx, lru_bx, lru_lam, attn_sink, w_out_ab, ffn_w1, ffn_w3, ffn_w2, w_in_cd, s5_a_re, s5_a_im, s5_log_step, s5_b_re, s5_b_im, s5_c_re, s5_c_im, s5_d, s5_w_glu, s5_b_glu, hg_lb_raw, hg_norm, w_out_cd, moe_router, moe_w1, moe_w3, moe_w2):
    nb, t, d = x.shape
    tc = ctx.shape[1]
    depth = w_mod.shape[0]
    assert tc % SEQ_BLOCK == 0 and t % SEQ_BLOCK == 0 and t % GRID_W == 0

    xa = jnp.concatenate([ctx, x], axis=1)
    mod_rows = -(-(nb + 1) // SUBLANES) * SUBLANES
    cvec = jnp.zeros((mod_rows, d), F32).at[:nb].set(c).at[nb].set(c_ctx)
    mod = _modulation(cvec, w_mod, b_mod).reshape(depth, mod_rows, N_MOD, d)

    cos_t, sin_t = _rope_tables(t, tc)
    lb_soft = jax.nn.softmax(hg_lb_raw.astype(F32), axis=0)
    lb_table = jnp.cumsum(lb_soft, axis=0) - lb_soft[0:1]
    n_exp = moe_router.shape[2]

    for l in range(depth):
        j = l // 2
        mod_l = mod[l]
        if l % 2 == 0:
            z = _inproj(xa, mod_l, norm_mix[l], w_in_ab[j].astype(BF16), tc, nb)
            sp = jax.nn.softplus(-lru_lam[j])
            lru = []
            for dr, rev in enumerate((False, True)):
                lru.append(_lru(z, lru_conv_w[j], lru_conv_b[j],
                                _block_diag_dense(lru_wa[j, dr]).astype(BF16), lru_ba[j, dr],
                                _block_diag_dense(lru_wx[j, dr]).astype(BF16), lru_bx[j, dr],
                                sp[dr], tc, rev))
            sink = jnp.zeros((1, LANES), F32).at[0, :ATT_HEADS].set(attn_sink[j])
            att = _attention(z, cos_t, sin_t, sink, tc)
            xa = _outproj_ab(xa, mod_l, lru[0], lru[1], z, att, w_out_ab[j].astype(BF16), tc, nb)
            xa = _ffn(xa, mod_l, norm_ffn[l], ffn_w1[j].astype(BF16), ffn_w3[j].astype(BF16),
                      ffn_w2[j].astype(BF16), tc, nb)
        else:
            z = _inproj(xa, mod_l, norm_mix[l], w_in_cd[j].astype(BF16), tc, nb)
            s5w = _s5_weights(s5_a_re[j], s5_a_im[j], s5_log_step[j], s5_b_re[j], s5_b_im[j],
                              s5_c_re[j], s5_c_im[j])
            y5 = _s5(z, s5w, s5_d[j], tc, s5_d.shape[1])
            o_f = _gla(z, lb_table[j], tc, 2, False)
            o_b = _gla(z, lb_table[j], tc, 3, True)
            xa = _outproj_cd(xa, mod_l, y5, o_f, o_b, z, s5_w_glu[j].astype(BF16), s5_b_glu[j],
                             hg_norm[j], w_out_cd[j].astype(BF16), tc, nb)
            router_t = jnp.zeros((SUBLANES, d), F32).at[:n_exp].set(moe_router[j].T)
            xa = _moe(xa, mod_l, norm_ffn[l], router_t, moe_w1[j].astype(BF16),
                      moe_w3[j].astype(BF16), moe_w2[j].astype(BF16), tc, nb)
    return _final_norm(xa, final_norm, tc)
```

```python
import functools
import math

import jax
import jax.numpy as jnp
from jax import lax
from jax.experimental import pallas as pl
from jax.experimental.pallas import tpu as pltpu

F32 = jnp.float32
BF16 = jnp.bfloat16

EPS = 1e-6
GRID_W = 64
LRU_BLOCKS = 8
LRU_C = 8.0
CONV_W = 4
ATT_HEADS = 8
ATT_KV_HEADS = 2
ATT_GROUP = ATT_HEADS // ATT_KV_HEADS
HEAD_DIM = 64
WINDOW = 128
BLOCK_Q = 128
ROPE_FREQS = HEAD_DIM // 4
ROPE_BASE = 10000.0
S5_GROUP = 16
S5_STATE = 64
S5_CHUNK = 8
HG_HEADS = 4
HG_DK = 128
HG_CHUNK = 64
SEQ_BLOCK = 256
N_MOD = 6
MOE_CAP = 224
LANES = 128
SUBLANES = 8
VMEM_LIMIT = 56 * 1024 * 1024
NEG = -1e30
LOG2E = math.log2(math.e)


def _cparams(sem):
    return pltpu.CompilerParams(dimension_semantics=sem, vmem_limit_bytes=VMEM_LIMIT)


def _row_tile(ttot):
    for tm in (768, 1024, 512, 256):
        if ttot % tm == 0:
            return tm
    raise ValueError(f"unsupported token count {ttot}")


def _sigmoid(x):
    return 0.5 * jnp.tanh(0.5 * x) + 0.5


def _silu(x):
    return x * _sigmoid(x)


def _gelu(x):
    return 0.5 * x * (1.0 + jnp.tanh(math.sqrt(2.0 / math.pi) * (x + 0.044715 * (x * x * x))))


def _norm_mod(x, g, mb, mc, row0, tc, shift_idx, scale_idx):
    ms = jnp.mean(x * x, axis=-1, keepdims=True)
    y = x * lax.rsqrt(ms + EPS) * g
    rows = row0 + lax.broadcasted_iota(jnp.int32, (x.shape[0], 1), 0)
    is_ctx = rows < tc
    scale = jnp.where(is_ctx, mc[scale_idx:scale_idx + 1], mb[scale_idx:scale_idx + 1])
    shift = jnp.where(is_ctx, mc[shift_idx:shift_idx + 1], mb[shift_idx:shift_idx + 1])
    return y * (1.0 + scale) + shift


def _gate_mod(mb, mc, idx, row0, tc, n):
    rows = row0 + lax.broadcasted_iota(jnp.int32, (n, 1), 0)
    return jnp.where(rows < tc, mc[idx:idx + 1], mb[idx:idx + 1])


def _mod_kernel(c_ref, w_ref, b_ref, o_ref):
    s = _silu(c_ref[...])
    o_ref[...] = jnp.dot(s, w_ref[...], preferred_element_type=F32,
                         precision=lax.Precision.HIGHEST) + b_ref[...]


def _modulation(cvec, w_mod, b_mod):
    depth, d, n = w_mod.shape
    tn = 1536 if n % 1536 == 0 else n
    rows = cvec.shape[0]
    return pl.pallas_call(
        _mod_kernel,
        grid=(depth, n // tn),
        in_specs=[pl.BlockSpec((rows, d), lambda l, j: (0, 0)),
                  pl.BlockSpec((None, d, tn), lambda l, j: (l, 0, j)),
                  pl.BlockSpec((None, 1, tn), lambda l, j: (l, 0, j))],
        out_specs=pl.BlockSpec((None, rows, tn), lambda l, j: (l, 0, j)),
        out_shape=jax.ShapeDtypeStruct((depth, rows, n), F32),
        compiler_params=_cparams(("arbitrary", "arbitrary")),
        name="modulation",
    )(cvec, w_mod, b_mod.reshape(depth, 1, n))


def _inproj_kernel(x_ref, mb_ref, mc_ref, g_ref, w_ref, o_ref, *, tm, tc):
    i = pl.program_id(1)
    h = _norm_mod(x_ref[...], g_ref[...], mb_ref[...], mc_ref[...], i * tm, tc, 0, 1)
    o_ref[...] = jnp.dot(h.astype(BF16), w_ref[...], preferred_element_type=F32)


def _inproj(xa, mod_l, g, w, tc, nb):
    b, ttot, d = xa.shape
    n = w.shape[1]
    tm = _row_tile(ttot)
    return pl.pallas_call(
        functools.partial(_inproj_kernel, tm=tm, tc=tc),
        grid=(b, ttot // tm),
        in_specs=[pl.BlockSpec((None, tm, d), lambda bi, i: (bi, i, 0)),
                  pl.BlockSpec((None, N_MOD, d), lambda bi, i: (bi, 0, 0)),
                  pl.BlockSpec((None, N_MOD, d), lambda bi, i: (nb, 0, 0)),
                  pl.BlockSpec((1, d), lambda bi, i: (0, 0)),
                  pl.BlockSpec((d, n), lambda bi, i: (0, 0))],
        out_specs=pl.BlockSpec((None, tm, n), lambda bi, i: (bi, i, 0)),
        out_shape=jax.ShapeDtypeStruct((b, ttot, n), F32),
        compiler_params=_cparams(("parallel", "parallel")),
        name="inproj",
    )(xa, mod_l, mod_l, g.reshape(1, d), w)


def _seq_block(step, n_ctx, n_all, reverse):
    if not reverse:
        return step
    return jnp.where(step < n_ctx, n_ctx - 1 - step, n_all - 1 - step + n_ctx)


def _lru_kernel(u_ref, hp_ref, hn_ref, cw_ref, cb_ref, wa_ref, ba_ref, wx_ref, bx_ref, sp_ref,
                o_ref, uc_ref, a_ref, b_ref, h_ref, *, lc, n_ctx, n_all, reverse):
    step = pl.program_id(1)
    blk = _seq_block(step, n_ctx, n_all, reverse)
    has_prev = jnp.logical_and(blk != 0, blk != n_ctx)
    has_next = jnp.logical_and(blk != n_ctx - 1, blk != n_all - 1)

    @pl.when(step == 0)
    def _():
        h_ref[...] = jnp.zeros_like(h_ref)

    w0, w1, w2, w3 = (cw_ref[j:j + 1, :] for j in range(CONV_W))
    cb = cb_ref[...]
    u = u_ref[...]
    uc_ref[...] = (cb + pltpu.roll(u, 2, 0) * w0 + pltpu.roll(u, 1, 0) * w1 + u * w2
                   + pltpu.roll(u, lc - 1, 0) * w3)
    row = lax.broadcasted_iota(jnp.int32, (SUBLANES, 1), 0)
    hp = jnp.where(has_prev, hp_ref[...], 0.0)
    hn = jnp.where(has_next, hn_ref[...], 0.0)
    u0 = u_ref[0:SUBLANES, :]
    u1 = u_ref[SUBLANES:2 * SUBLANES, :]
    uc_ref[0:SUBLANES, :] = (
        cb + jnp.where(row < 2, pltpu.roll(hp, 2, 0), pltpu.roll(u0, 2, 0)) * w0
        + jnp.where(row < 1, pltpu.roll(hp, 1, 0), pltpu.roll(u0, 1, 0)) * w1 + u0 * w2
        + jnp.where(row < SUBLANES - 1, pltpu.roll(u0, SUBLANES - 1, 0),
                    pltpu.roll(u1, SUBLANES - 1, 0)) * w3)
    ul = u_ref[lc - SUBLANES:lc, :]
    um = u_ref[lc - 2 * SUBLANES:lc - SUBLANES, :]
    uc_ref[lc - SUBLANES:lc, :] = (
        cb + jnp.where(row < 2, pltpu.roll(um, 2, 0), pltpu.roll(ul, 2, 0)) * w0
        + jnp.where(row < 1, pltpu.roll(um, 1, 0), pltpu.roll(ul, 1, 0)) * w1 + ul * w2
        + jnp.where(row < SUBLANES - 1, pltpu.roll(ul, SUBLANES - 1, 0),
                    pltpu.roll(hn, SUBLANES - 1, 0)) * w3)

    uc = uc_ref[...]
    ub = uc.astype(BF16)
    r = _sigmoid(jnp.dot(ub, wa_ref[...], preferred_element_type=F32) + ba_ref[...])
    gi = _sigmoid(jnp.dot(ub, wx_ref[...], preferred_element_type=F32) + bx_ref[...])
    log_a = -LRU_C * r * sp_ref[...]
    a = jnp.exp(log_a)
    b = jnp.sqrt(1.0 - jnp.exp(2.0 * log_a)) * (gi * uc)
    sub = lax.broadcasted_iota(jnp.int32, (lc, 1), 0) % SUBLANES
    dist = 1
    while dist < SUBLANES:
        inside = (sub < SUBLANES - dist) if reverse else (sub >= dist)
        shift = (lc - dist) if reverse else dist
        a_sh = jnp.where(inside, pltpu.roll(a, shift, 0), 1.0)
        b_sh = jnp.where(inside, pltpu.roll(b, shift, 0), 0.0)
        b = a * b_sh + b
        a = a * a_sh
        dist *= 2
    a_ref[...] = a
    b_ref[...] = b
    n_grp = lc // SUBLANES

    def body(i, carry):
        t = (n_grp - 1 - i) if reverse else i
        rows = pl.ds(pl.multiple_of(t * SUBLANES, SUBLANES), SUBLANES)
        h = a_ref[rows, :] * carry + b_ref[rows, :]
        b_ref[rows, :] = h
        return h[0:1, :] if reverse else h[SUBLANES - 1:SUBLANES, :]

    h_ref[0:1, :] = lax.fori_loop(0, n_grp, body, h_ref[0:1, :], unroll=4)
    o_ref[...] = b_ref[...].astype(o_ref.dtype)


def _lru(z, conv_w, conv_b, wa_bd, ba, wx_bd, bx, sp, tc, reverse):
    b, ttot, _ = z.shape
    w = conv_w.shape[1]
    lc = SEQ_BLOCK
    n_all, n_ctx = ttot // lc, tc // lc
    hb = lc // SUBLANES
    n_h = ttot // SUBLANES
    col = 1

    def blk(s):
        return _seq_block(s, n_ctx, n_all, reverse)

    vec = lambda: pl.BlockSpec((1, w), lambda bi, s: (0, 0))
    return pl.pallas_call(
        functools.partial(_lru_kernel, lc=lc, n_ctx=n_ctx, n_all=n_all, reverse=reverse),
        grid=(b, n_all),
        in_specs=[pl.BlockSpec((None, lc, w), lambda bi, s: (bi, blk(s), col)),
                  pl.BlockSpec((None, SUBLANES, w),
                               lambda bi, s: (bi, jnp.maximum(blk(s) * hb - 1, 0), col)),
                  pl.BlockSpec((None, SUBLANES, w),
                               lambda bi, s: (bi, jnp.minimum((blk(s) + 1) * hb, n_h - 1), col)),
                  pl.BlockSpec((CONV_W, w), lambda bi, s: (0, 0)),
                  vec(),
                  pl.BlockSpec((w, w), lambda bi, s: (0, 0)),
                  vec(),
                  pl.BlockSpec((w, w), lambda bi, s: (0, 0)),
                  vec(), vec()],
        out_specs=pl.BlockSpec((None, lc, w), lambda bi, s: (bi, blk(s), 0)),
        out_shape=jax.ShapeDtypeStruct((b, ttot, w), BF16),
        scratch_shapes=[pltpu.VMEM((lc, w), F32), pltpu.VMEM((lc, w), F32),
                        pltpu.VMEM((lc, w), F32), pltpu.VMEM((SUBLANES, w), F32)],
        compiler_params=_cparams(("parallel", "arbitrary")),
        name="rglru_rev" if reverse else "rglru_fwd",
    )(z, z, z, conv_w, conv_b.reshape(1, w), wa_bd, ba.reshape(1, w), wx_bd, bx.reshape(1, w),
      sp.reshape(1, w))


def _rope(x, cos, sin_signed):
    lane = lax.broadcasted_iota(jnp.int32, x.shape, 1)
    swapped = jnp.where(lane % 32 < 16, pltpu.roll(x, LANES - 16, 1), pltpu.roll(x, 16, 1))
    return x * cos + swapped * sin_signed


def _pair_layout(x, x_sw, head):
    lo = lax.broadcasted_iota(jnp.int32, x.shape, 1) < HEAD_DIM
    first, second = (x, x_sw) if head == 0 else (x_sw, x)
    return jnp.concatenate([jnp.where(lo, first, 0.0), jnp.where(lo, 0.0, second)],
                           axis=0).astype(BF16)


def _fold_lanes(x, op):
    acc = x[:, 0:LANES]
    for j in range(1, x.shape[1] // LANES):
        acc = op(acc, x[:, j * LANES:(j + 1) * LANES])
    return acc


def _attend(q_block, keys, vals, bias, sink_row, o_ref):
    nk = keys.shape[0]
    keys_sw = pltpu.roll(keys, HEAD_DIM, 1)
    vals_sw = pltpu.roll(vals, HEAD_DIM, 1)
    lo = lax.broadcasted_iota(jnp.int32, (BLOCK_Q, LANES), 1) < HEAD_DIM
    pairs_per_kv = ATT_GROUP // 2
    for kv in range(ATT_KV_HEADS):
        kab = _pair_layout(keys, keys_sw, kv)
        vab = _pair_layout(vals, vals_sw, kv)
        for g in range(kv * pairs_per_kv, (kv + 1) * pairs_per_kv):
            s = lax.dot_general(q_block(g), kab, (((1,), (1,)), ((), ())),
                                preferred_element_type=F32)
            halves = []
            recips = []
            for half in range(2):
                sh = s[:, half * nk:(half + 1) * nk]
                if bias is not None:
                    sh = sh + bias
                sk = sink_row[:, 2 * g + half:2 * g + half + 1]
                m = jnp.maximum(jnp.max(_fold_lanes(sh, jnp.maximum), axis=-1, keepdims=True), sk)
                p = jnp.exp2(sh - m)
                recips.append(1.0 / (jnp.sum(_fold_lanes(p, jnp.add), axis=-1, keepdims=True)
                                     + jnp.exp2(sk - m)))
                halves.append(p.astype(BF16))
            o = jnp.dot(jnp.concatenate(halves, axis=1), vab, preferred_element_type=F32)
            o_ref[:, g * LANES:(g + 1) * LANES] = (o * jnp.where(lo, recips[0], recips[1])
                                                   ).astype(o_ref.dtype)


def _attn_kernel(q_ref, kp_ref, k0_ref, kn_ref, vp_ref, v0_ref, vn_ref, kc_ref, vc_ref,
                 cq_ref, sq_ref, cp_ref, sp_ref, cn_ref, sn_ref, sink_ref, o_ref, *, n_ctx, n_lat):
    i = pl.program_id(1)
    qscale = HEAD_DIM ** -0.5 * LOG2E
    sink_row = sink_ref[...] * LOG2E

    @pl.when(i < n_ctx)
    def _():
        def q_block(g):
            return (q_ref[:, g * LANES:(g + 1) * LANES] * qscale).astype(BF16)
        _attend(q_block, kc_ref[...], vc_ref[...], None, sink_row, o_ref)

    @pl.when(i >= n_ctx)
    def _():
        n = i - n_ctx
        cq, sq = cq_ref[...], sq_ref[...]

        def q_block(g):
            return (_rope(q_ref[:, g * LANES:(g + 1) * LANES], cq, sq) * qscale).astype(BF16)

        kp = _rope(kp_ref[...], cp_ref[...], sp_ref[...])
        k0 = _rope(k0_ref[...], cq, sq)
        kn = _rope(kn_ref[...], cn_ref[...], sn_ref[...])
        keys = jnp.concatenate([kp, k0, kn, kc_ref[...]], axis=0)
        vals = jnp.concatenate([vp_ref[...], v0_ref[...], vn_ref[...], vc_ref[...]], axis=0)
        nk = keys.shape[0]
        qpos = n * BLOCK_Q + lax.broadcasted_iota(jnp.int32, (BLOCK_Q, nk), 0)
        col = lax.broadcasted_iota(jnp.int32, (BLOCK_Q, nk), 1)
        kpos = (n - 1) * BLOCK_Q + col
        local = (jnp.abs(qpos - kpos) <= WINDOW) & (kpos >= 0) & (kpos < n_lat * BLOCK_Q)
        bias = jnp.where(local | (col >= 3 * BLOCK_Q), 0.0, NEG)
        _attend(q_block, keys, vals, bias, sink_row, o_ref)


def _attention(z, cos_t, sin_t, sink, tc):
    b, ttot, _ = z.shape
    n_all, n_ctx = ttot // BLOCK_Q, tc // BLOCK_Q
    n_lat = n_all - n_ctx
    kcol = (2 * 512 + ATT_HEADS * HEAD_DIM) // LANES
    vcol = kcol + 1
    qcol = 2 * 512 // (ATT_HEADS * HEAD_DIM)

    def prev(i):
        return jnp.maximum(i - 1, 0)

    def nxt(i):
        return jnp.minimum(i + 1, n_all - 1)

    def rows(col, f):
        return pl.BlockSpec((None, BLOCK_Q, LANES), lambda bi, i: (bi, f(i), col))

    def tab(f):
        return pl.BlockSpec((BLOCK_Q, LANES), lambda bi, i: (f(i), 0))

    same = lambda i: i
    return pl.pallas_call(
        functools.partial(_attn_kernel, n_ctx=n_ctx, n_lat=n_lat),
        grid=(b, n_all),
        in_specs=[pl.BlockSpec((None, BLOCK_Q, ATT_HEADS * HEAD_DIM), lambda bi, i: (bi, i, qcol)),
                  rows(kcol, prev), rows(kcol, same), rows(kcol, nxt),
                  rows(vcol, prev), rows(vcol, same), rows(vcol, nxt),
                  pl.BlockSpec((None, tc, LANES), lambda bi, i: (bi, 0, kcol)),
                  pl.BlockSpec((None, tc, LANES), lambda bi, i: (bi, 0, vcol)),
                  tab(same), tab(same), tab(prev), tab(prev), tab(nxt), tab(nxt),
                  pl.BlockSpec((1, LANES), lambda bi, i: (0, 0))],
        out_specs=pl.BlockSpec((None, BLOCK_Q, ATT_HEADS * HEAD_DIM), lambda bi, i: (bi, i, 0)),
        out_shape=jax.ShapeDtypeStruct((b, ttot, ATT_HEADS * HEAD_DIM), BF16),
        compiler_params=_cparams(("parallel", "parallel")),
        name="window_gqa",
    )(z, z, z, z, z, z, z, z, z, cos_t, sin_t, cos_t, sin_t, cos_t, sin_t, sink)


def _outproj_ab_kernel(x_ref, mb_ref, mc_ref, lf_ref, lb_ref, g_ref, at_ref, w_ref, o_ref, *, tm, tc):
    i = pl.program_id(1)
    lru = (lf_ref[...].astype(F32) + lb_ref[...].astype(F32)) * _gelu(g_ref[...])
    mix = jnp.concatenate([lru.astype(BF16), at_ref[...]], axis=1)
    dx = jnp.dot(mix, w_ref[...], preferred_element_type=F32)
    gate = _gate_mod(mb_ref[...], mc_ref[...], 2, i * tm, tc, tm)
    o_ref[...] = x_ref[...] + gate * dx


def _outproj_ab(xa, mod_l, lru_f, lru_b, z, att, w_out, tc, nb):
    b, ttot, d = xa.shape
    w = lru_f.shape[2]
    tm = _row_tile(ttot)
    tile = lambda width, col: pl.BlockSpec((None, tm, width), lambda bi, i: (bi, i, col))
    return pl.pallas_call(
        functools.partial(_outproj_ab_kernel, tm=tm, tc=tc),
        grid=(b, ttot // tm),
        in_specs=[tile(d, 0),
                  pl.BlockSpec((None, N_MOD, d), lambda bi, i: (bi, 0, 0)),
                  pl.BlockSpec((None, N_MOD, d), lambda bi, i: (nb, 0, 0)),
                  tile(w, 0), tile(w, 0), tile(w, 0), tile(att.shape[2], 0),
                  pl.BlockSpec(w_out.shape, lambda bi, i: (0, 0))],
        out_specs=tile(d, 0),
        out_shape=jax.ShapeDtypeStruct(xa.shape, F32),
        compiler_params=_cparams(("parallel", "parallel")),
        name="outproj_ab",
    )(xa, mod_l, mod_l, lru_f, lru_b, z, att, w_out)


def _outproj_cd_kernel(x_ref, mb_ref, mc_ref, y_ref, of_ref, ob_ref, g_ref, wg_ref, bg_ref, gn_ref,
                       w_ref, o_ref, *, tm, tc):
    i = pl.program_id(1)
    y = _gelu(y_ref[...])
    s5 = y * _sigmoid(jnp.dot(y.astype(BF16), wg_ref[...], preferred_element_type=F32) + bg_ref[...])
    o = of_ref[...].astype(F32) + ob_ref[...].astype(F32)
    parts = []
    for h in range(HG_HEADS):
        oh = o[:, h * HG_DK:(h + 1) * HG_DK]
        ms = jnp.mean(oh * oh, axis=-1, keepdims=True)
        parts.append(oh * lax.rsqrt(ms + EPS) * gn_ref[...])
    hg = jnp.concatenate(parts, axis=1) * _silu(g_ref[...])
    mix = jnp.concatenate([s5.astype(BF16), hg.astype(BF16)], axis=1)
    dx = jnp.dot(mix, w_ref[...], preferred_element_type=F32)
    gate = _gate_mod(mb_ref[...], mc_ref[...], 2, i * tm, tc, tm)
    o_ref[...] = x_ref[...] + gate * dx


def _outproj_cd(xa, mod_l, y5, o_f, o_b, z, w_glu, b_glu, hg_norm, w_out, tc, nb):
    b, ttot, d = xa.shape
    w = y5.shape[2]
    tm = _row_tile(ttot)
    gcol = z.shape[2] // w - 1
    tile = lambda width, col: pl.BlockSpec((None, tm, width), lambda bi, i: (bi, i, col))
    return pl.pallas_call(
        functools.partial(_outproj_cd_kernel, tm=tm, tc=tc),
        grid=(b, ttot // tm),
        in_specs=[tile(d, 0),
                  pl.BlockSpec((None, N_MOD, d), lambda bi, i: (bi, 0, 0)),
                  pl.BlockSpec((None, N_MOD, d), lambda bi, i: (nb, 0, 0)),
                  tile(w, 0), tile(w, 0), tile(w, 0), tile(w, gcol),
                  pl.BlockSpec(w_glu.shape, lambda bi, i: (0, 0)),
                  pl.BlockSpec((1, w), lambda bi, i: (0, 0)),
                  pl.BlockSpec((1, HG_DK), lambda bi, i: (0, 0)),
                  pl.BlockSpec(w_out.shape, lambda bi, i: (0, 0))],
        out_specs=tile(d, 0),
        out_shape=jax.ShapeDtypeStruct(xa.shape, F32),
        compiler_params=_cparams(("parallel", "parallel")),
        name="outproj_cd",
    )(xa, mod_l, mod_l, y5, o_f, o_b, z, w_glu, b_glu.reshape(1, w), hg_norm.reshape(1, HG_DK), w_out)


def _ffn_kernel(x_ref, mb_ref, mc_ref, g_ref, w1_ref, w3_ref, w2_ref, o_ref, h_ref, acc_ref, *, tm, tc):
    i = pl.program_id(1)
    k = pl.program_id(2)

    @pl.when(k == 0)
    def _():
        h = _norm_mod(x_ref[...], g_ref[...], mb_ref[...], mc_ref[...], i * tm, tc, 3, 4)
        h_ref[...] = h.astype(BF16)
        acc_ref[...] = jnp.zeros_like(acc_ref)

    h = h_ref[...]
    a = jnp.dot(h, w1_ref[...], preferred_element_type=F32)
    c = jnp.dot(h, w3_ref[...], preferred_element_type=F32)
    acc_ref[...] += jnp.dot((_silu(a) * c).astype(BF16), w2_ref[...], preferred_element_type=F32)

    @pl.when(k == pl.num_programs(2) - 1)
    def _():
        gate = _gate_mod(mb_ref[...], mc_ref[...], 5, i * tm, tc, tm)
        o_ref[...] = x_ref[...] + gate * acc_ref[...]


def _ffn(xa, mod_l, g, w1, w3, w2, tc, nb):
    b, ttot, d = xa.shape
    f = w1.shape[1]
    tm = _row_tile(ttot)
    tf = f // 2 if (f // 2) % LANES == 0 else f
    return pl.pallas_call(
        functools.partial(_ffn_kernel, tm=tm, tc=tc),
        grid=(b, ttot // tm, f // tf),
        in_specs=[pl.BlockSpec((None, tm, d), lambda bi, i, k: (bi, i, 0)),
                  pl.BlockSpec((None, N_MOD, d), lambda bi, i, k: (bi, 0, 0)),
                  pl.BlockSpec((None, N_MOD, d), lambda bi, i, k: (nb, 0, 0)),
                  pl.BlockSpec((1, d), lambda bi, i, k: (0, 0)),
                  pl.BlockSpec((d, tf), lambda bi, i, k: (0, k)),
                  pl.BlockSpec((d, tf), lambda bi, i, k: (0, k)),
                  pl.BlockSpec((tf, d), lambda bi, i, k: (k, 0))],
        out_specs=pl.BlockSpec((None, tm, d), lambda bi, i, k: (bi, i, 0)),
        out_shape=jax.ShapeDtypeStruct(xa.shape, F32),
        scratch_shapes=[pltpu.VMEM((tm, d), BF16), pltpu.VMEM((tm, d), F32)],
        compiler_params=_cparams(("parallel", "parallel", "arbitrary")),
        name="ffn_swiglu",
    )(xa, mod_l, mod_l, g.reshape(1, d), w1, w3, w2)


def _router_kernel(x_ref, mb_ref, mc_ref, g_ref, r_ref, h_ref, wt_ref, cnt_ref, *, tm, tc, n_exp):
    i = pl.program_id(1)
    h = _norm_mod(x_ref[...], g_ref[...], mb_ref[...], mc_ref[...], i * tm, tc, 3, 4)
    h_ref[...] = h.astype(BF16)
    logits = lax.dot_general(r_ref[...], h, (((1,), (1,)), ((), ())), preferred_element_type=F32,
                             precision=lax.Precision.HIGHEST)
    sub = lax.broadcasted_iota(jnp.int32, logits.shape, 0)
    logits = jnp.where(sub < n_exp, logits, NEG)
    m1 = jnp.max(logits, axis=0, keepdims=True)
    i1 = jnp.min(jnp.where(logits == m1, sub, SUBLANES), axis=0, keepdims=True)
    rest = jnp.where(sub == i1, NEG, logits)
    m2 = jnp.max(rest, axis=0, keepdims=True)
    i2 = jnp.min(jnp.where(rest == m2, sub, SUBLANES), axis=0, keepdims=True)
    e2 = jnp.exp(m2 - m1)
    g1 = 1.0 / (1.0 + e2)
    wt = jnp.where(sub == i1, g1, 0.0) + jnp.where(sub == i2, e2 * g1, 0.0)
    wt_ref[...] = wt
    cnt = jnp.sum(jnp.where(wt > 0.0, 1.0, 0.0), axis=1, keepdims=True)
    cnt_ref[...] = jnp.broadcast_to(cnt, cnt_ref.shape).astype(jnp.int32)


def _router(xa, mod_l, g, router_t, tc, nb, n_exp):
    b, ttot, d = xa.shape
    tm = _row_tile(ttot)
    n_t = ttot // tm
    return pl.pallas_call(
        functools.partial(_router_kernel, tm=tm, tc=tc, n_exp=n_exp),
        grid=(b, n_t),
        in_specs=[pl.BlockSpec((None, tm, d), lambda bi, i: (bi, i, 0)),
                  pl.BlockSpec((None, N_MOD, d), lambda bi, i: (bi, 0, 0)),
                  pl.BlockSpec((None, N_MOD, d), lambda bi, i: (nb, 0, 0)),
                  pl.BlockSpec((1, d), lambda bi, i: (0, 0)),
                  pl.BlockSpec((SUBLANES, d), lambda bi, i: (0, 0))],
        out_specs=[pl.BlockSpec((tm, d), lambda bi, i: (bi * n_t + i, 0)),
                   pl.BlockSpec((SUBLANES, tm), lambda bi, i: (0, bi * n_t + i)),
                   pl.BlockSpec((None, SUBLANES, LANES), lambda bi, i: (bi * n_t + i, 0, 0))],
        out_shape=[jax.ShapeDtypeStruct((b * ttot, d), BF16),
                   jax.ShapeDtypeStruct((SUBLANES, b * ttot), F32),
                   jax.ShapeDtypeStruct((b * n_t, SUBLANES, LANES), jnp.int32)],
        compiler_params=_cparams(("parallel", "parallel")),
        name="moe_router",
    )(xa, mod_l, mod_l, g.reshape(1, d), router_t)


def _moe_kernel(cnt_ref, x_ref, *refs, tm, tc, n_exp, cap, n_t, group):
    mb_refs = refs[:group]
    mc_ref, h_ref, wt_ref, w1_ref, w3_ref, w2_ref, o_ref, rank_ref = refs[group:]
    p = pl.program_id(0)
    e = pl.program_id(1)
    f = w1_ref.shape[1]
    half = (f // LANES + 1) // 2 * LANES
    f_split = [(0, half), (half, f)] if 0 < half < f else [(0, f)]

    @pl.when(e == 0)
    def _():
        o_ref[...] = jnp.zeros_like(o_ref)
        r_i = lax.broadcasted_iota(jnp.int32, (tm, tm), 0)
        c_i = lax.broadcasted_iota(jnp.int32, (tm, tm), 1)
        before = jnp.where(r_i < c_i, 1.0, 0.0).astype(BF16)
        for j in range(group):
            sel = wt_ref[:, j * tm:(j + 1) * tm] > 0.0
            rank = jnp.dot(jnp.where(sel, 1.0, 0.0).astype(BF16), before, preferred_element_type=F32)
            rank_ref[j] = jnp.where(sel, rank, -1.0)

    counts = [cnt_ref[(p * group + j) * n_exp + e] for j in range(group)]

    def expert_pass(parts):
        onehots, xs, gates = [], [], []
        for j, blk in parts:
            r_row = rank_ref[j, pl.ds(e, 1), :]
            w_row = wt_ref[pl.ds(e, 1), j * tm:(j + 1) * tm]
            slot = (lax.broadcasted_iota(jnp.int32, (cap, tm), 0) + blk * cap).astype(F32)
            hit = r_row == slot
            onehot = jnp.where(hit, 1.0, 0.0).astype(BF16)
            onehots.append(onehot)
            xs.append(jnp.dot(onehot, h_ref[j * tm:(j + 1) * tm, :],
                              preferred_element_type=F32).astype(BF16))
            gates.append(jnp.sum(jnp.where(hit, w_row, 0.0), axis=1, keepdims=True))
        xe = jnp.concatenate(xs, axis=0) if len(parts) > 1 else xs[0]
        gate = jnp.concatenate(gates, axis=0) if len(parts) > 1 else gates[0]
        y = None
        for f0, f1 in f_split:
            a = jnp.dot(xe, w1_ref[:, f0:f1], preferred_element_type=F32)
            c = jnp.dot(xe, w3_ref[:, f0:f1], preferred_element_type=F32)
            part = jnp.dot((_silu(a) * c * gate).astype(BF16), w2_ref[f0:f1, :],
                           preferred_element_type=F32)
            y = part if y is None else y + part
        y = y.astype(BF16)
        for n, (j, blk) in enumerate(parts):
            o_ref[j * tm:(j + 1) * tm, :] += lax.dot_general(
                onehots[n], y[n * cap:(n + 1) * cap], (((0,), (0,)), ((), ())),
                preferred_element_type=F32)

    any_routed = counts[0] > 0
    for j in range(1, group):
        any_routed = jnp.logical_or(any_routed, counts[j] > 0)

    @pl.when(any_routed)
    def _():
        expert_pass([(j, 0) for j in range(group)])

    for blk in range(1, -(-tm // cap)):
        for j in range(group):
            @pl.when(counts[j] > blk * cap)
            def _():
                expert_pass([(j, blk)])

    @pl.when(e == n_exp - 1)
    def _():
        for j in range(group):
            i = (p * group + j) % n_t
            gate5 = _gate_mod(mb_refs[j][...], mc_ref[...], 5, i * tm, tc, tm)
            rows = slice(j * tm, (j + 1) * tm)
            o_ref[rows, :] = x_ref[rows, :] + gate5 * o_ref[rows, :]


def _moe(xa, mod_l, g, router_t, w1, w3, w2, tc, nb):
    b, ttot, d = xa.shape
    n_exp, _, f = w1.shape
    tm = _row_tile(ttot)
    n_t = ttot // tm
    cap = MOE_CAP
    group = 2 if (b * n_t) % 2 == 0 else 1
    h, wt, cnt = _router(xa, mod_l, g, router_t, tc, nb, n_exp)
    cnt = cnt[:, :n_exp, 0].reshape(-1)
    rows = group * tm
    mods = [pl.BlockSpec((None, N_MOD, d), functools.partial(
        lambda p, e, c, j: ((p * group + j) // n_t, 0, 0), j=j)) for j in range(group)]
    grid_spec = pltpu.PrefetchScalarGridSpec(
        num_scalar_prefetch=1,
        grid=(b * n_t // group, n_exp),
        in_specs=[pl.BlockSpec((rows, d), lambda p, e, c: (p, 0), pipeline_mode=pl.Buffered(1))]
        + mods
        + [pl.BlockSpec((None, N_MOD, d), lambda p, e, c: (nb, 0, 0)),
           pl.BlockSpec((rows, d), lambda p, e, c: (p, 0)),
           pl.BlockSpec((SUBLANES, rows), lambda p, e, c: (0, p)),
           pl.BlockSpec((None, d, f), lambda p, e, c: (e, 0, 0)),
           pl.BlockSpec((None, d, f), lambda p, e, c: (e, 0, 0)),
           pl.BlockSpec((None, f, d), lambda p, e, c: (e, 0, 0))],
        out_specs=pl.BlockSpec((rows, d), lambda p, e, c: (p, 0)),
        scratch_shapes=[pltpu.VMEM((group, SUBLANES, tm), F32)])
    out = pl.pallas_call(
        functools.partial(_moe_kernel, tm=tm, tc=tc, n_exp=n_exp, cap=cap, n_t=n_t, group=group),
        grid_spec=grid_spec,
        out_shape=jax.ShapeDtypeStruct((b * ttot, d), F32),
        compiler_params=_cparams(("parallel", "arbitrary")),
        name="moe_experts",
    )(cnt, xa.reshape(b * ttot, d), *([mod_l] * group), mod_l, h, wt, w1, w3, w2)
    return out.reshape(b, ttot, d)


def _spread_groups(x, inner, row_group, n_grp):
    cols_in = x.shape[2]
    cols_out = cols_in * n_grp
    q = jnp.arange(cols_out)
    src = (q // (n_grp * inner)) * inner + q % inner
    tile = (jnp.arange(cols_in)[:, None] == src[None, :]).astype(BF16)
    col_group = (q // inner) % n_grp
    y = jnp.dot(x.astype(BF16), tile)
    return jnp.where(row_group[:, None] == col_group[None, :], y, jnp.zeros((), BF16))


def _s5_weights(a_re, a_im, log_step, b_re, b_im, c_re, c_im):
    L = S5_CHUNK
    lr = jnp.minimum(a_re, -1e-4)
    li = a_im
    dt = jnp.exp(log_step)[..., None]
    mag, ang = lr * dt, li * dt
    lbr, lbi = jnp.exp(mag) * jnp.cos(ang), jnp.exp(mag) * jnp.sin(ang)
    zr, zi = lbr - 1.0, lbi
    den = lr * lr + li * li
    fr = (zr * lr + zi * li) / den
    fi = (zi * lr - zr * li) / den
    bbr = fr[..., None] * b_re - fi[..., None] * b_im
    bbi = fr[..., None] * b_im + fi[..., None] * b_re

    def power(p):
        p = p[..., None, None, None].astype(F32)
        return jnp.exp(mag * p) * jnp.cos(ang * p), jnp.exp(mag * p) * jnp.sin(ang * p)

    n_dir, n_grp, n_st = a_re.shape
    gpb = LANES // S5_GROUP
    n_blk = n_grp // gpb
    s = jnp.arange(L)
    in_group = (jnp.arange(L * LANES) % LANES) // S5_GROUP
    st_group = jnp.arange(gpb * n_st) // n_st
    outs = []
    for d in range(n_dir):
        pr, pi = power(jnp.arange(L))
        pr, pi = pr[:, d], pi[:, d]
        cbr = (jnp.einsum('gcn,tgn,gnk->tgck', c_re[d], pr, bbr[d])
               - jnp.einsum('gcn,tgn,gnk->tgck', c_re[d], pi, bbi[d])
               - jnp.einsum('gcn,tgn,gnk->tgck', c_im[d], pr, bbi[d])
               - jnp.einsum('gcn,tgn,gnk->tgck', c_im[d], pi, bbr[d]))
        lag = (s[None, :] - s[:, None]) if d == 0 else (s[:, None] - s[None, :])
        kern = jnp.where((lag >= 0)[..., None, None, None], cbr[jnp.clip(lag, 0, L - 1)], 0.0)
        kern = kern.reshape(L, L, n_blk, gpb, S5_GROUP, S5_GROUP).transpose(2, 0, 3, 5, 1, 4)
        m = _spread_groups(kern.reshape(n_blk, L * LANES, L * S5_GROUP), S5_GROUP, in_group, gpb)
        qr, qi = power((L - 1 - s) if d == 0 else s)
        qr, qi = qr[:, d], qi[:, d]
        str_ = qr[..., None] * bbr[d] - qi[..., None] * bbi[d]
        sti = qr[..., None] * bbi[d] + qi[..., None] * bbr[d]

        def to_state(x):
            x = x.reshape(L, n_blk, gpb, n_st, S5_GROUP).transpose(1, 0, 2, 4, 3)
            return _spread_groups(x.reshape(n_blk, L * LANES, n_st), n_st, in_group, gpb)

        wst = jnp.concatenate([to_state(str_), to_state(sti)], axis=2)
        rr, ri = power((s + 1) if d == 0 else (L - s))
        rr, ri = rr[:, d], ri[:, d]
        wr = c_re[d][None] * rr[:, :, None, :] - c_im[d][None] * ri[:, :, None, :]
        wi = -(c_re[d][None] * ri[:, :, None, :] + c_im[d][None] * rr[:, :, None, :])

        def from_state(x):
            x = x.reshape(L, n_blk, gpb, S5_GROUP, n_st).transpose(1, 2, 4, 0, 3)
            return _spread_groups(x.reshape(n_blk, gpb * n_st, L * S5_GROUP), S5_GROUP, st_group, gpb)

        wout = jnp.concatenate([from_state(wr), from_state(wi)], axis=1)
        ler, lei = jnp.exp(mag[d] * L) * jnp.cos(ang[d] * L), jnp.exp(mag[d] * L) * jnp.sin(ang[d] * L)
        lam_l = jnp.concatenate([ler.reshape(n_blk, 1, gpb * n_st), lei.reshape(n_blk, 1, gpb * n_st)],
                                axis=2)
        outs.append((m, wst, wout, lam_l))
    return tuple(jnp.stack([o[k] for o in outs]) for k in range(4))


def _s5_kernel(u_ref, m_ref, wst_ref, wout_ref, lam_ref, d_ref, o_ref, x_ref, hp_ref, *, rows, rows_ctx):
    L = S5_CHUNK
    dr = pl.program_id(2)
    ns = lam_ref.shape[1] // 2
    u = jnp.concatenate([u_ref[pl.ds(s, rows, stride=L), :] for s in range(L)], axis=1)
    ub = u.astype(BF16)
    x_ref[...] = jnp.dot(ub, wst_ref[...], preferred_element_type=F32)
    lr = lam_ref[:, 0:ns]
    li = lam_ref[:, ns:2 * ns]

    def visit(r, carry):
        hr, hi = carry
        hp_ref[pl.ds(r, 1), 0:ns] = hr
        hp_ref[pl.ds(r, 1), ns:2 * ns] = hi
        xr = x_ref[pl.ds(r, 1), 0:ns]
        xi = x_ref[pl.ds(r, 1), ns:2 * ns]
        return lr * hr - li * hi + xr, lr * hi + li * hr + xi

    zero = (jnp.zeros((1, ns), F32), jnp.zeros((1, ns), F32))

    @pl.when(dr == 0)
    def _():
        lax.fori_loop(0, rows, visit, zero, unroll=4)

    @pl.when(dr == 1)
    def _():
        c = lax.fori_loop(0, rows_ctx, lambda t, c: visit(rows_ctx - 1 - t, c), zero, unroll=4)
        lax.fori_loop(0, rows - rows_ctx, lambda t, c: visit(rows - 1 - t, c), c, unroll=4)

    y = (jnp.dot(ub, m_ref[...], preferred_element_type=F32)
         + jnp.dot(hp_ref[...].astype(BF16), wout_ref[...], preferred_element_type=F32))

    @pl.when(dr == 0)
    def _():
        for s in range(L):
            o_ref[pl.ds(s, rows, stride=L), :] = (y[:, s * LANES:(s + 1) * LANES]
                                                  + d_ref[...] * u[:, s * LANES:(s + 1) * LANES])

    @pl.when(dr == 1)
    def _():
        for s in range(L):
            o_ref[pl.ds(s, rows, stride=L), :] += y[:, s * LANES:(s + 1) * LANES]


def _s5(z, weights, dvec, tc, width):
    b, ttot, _ = z.shape
    m, wst, wout, lam_l = weights
    n_dir, n_blk = m.shape[0], m.shape[1]
    rows, rows_ctx = ttot // S5_CHUNK, tc // S5_CHUNK
    wspec = lambda a: pl.BlockSpec((None, None) + a.shape[2:], lambda bi, j, dr: (dr, j, 0, 0))
    return pl.pallas_call(
        functools.partial(_s5_kernel, rows=rows, rows_ctx=rows_ctx),
        grid=(b, n_blk, n_dir),
        in_specs=[pl.BlockSpec((None, ttot, LANES), lambda bi, j, dr: (bi, 0, j)),
                  wspec(m), wspec(wst), wspec(wout), wspec(lam_l),
                  pl.BlockSpec((1, LANES), lambda bi, j, dr: (0, j))],
        out_specs=pl.BlockSpec((None, ttot, LANES), lambda bi, j, dr: (bi, 0, j)),
        out_shape=jax.ShapeDtypeStruct((b, ttot, width), F32),
        scratch_shapes=[pltpu.VMEM((rows, lam_l.shape[3]), F32), pltpu.VMEM((rows, lam_l.shape[3]), F32)],
        compiler_params=_cparams(("parallel", "parallel", "arbitrary")),
        name="s5_bidir",
    )(z, m, wst, wout, lam_l, dvec.reshape(1, width))


def _gla_kernel(q_ref, f_ref, v_ref, lb_ref, o_ref, st_ref, *, lc, reverse):
    step = pl.program_id(1)

    @pl.when(step == 0)
    def _():
        st_ref[...] = jnp.zeros_like(st_ref)

    c = HG_CHUNK
    r_i = lax.broadcasted_iota(jnp.int32, (c, c), 0)
    c_i = lax.broadcasted_iota(jnp.int32, (c, c), 1)
    keep = (c_i >= r_i) if reverse else (c_i <= r_i)
    tri = jnp.where(keep, 1.0, 0.0).astype(BF16)
    lb = lb_ref[...]
    n_chunks = lc // c
    order = range(n_chunks - 1, -1, -1) if reverse else range(n_chunks)
    states = [st_ref[h] for h in range(HG_HEADS)]
    for ci in order:
        rows = pl.ds(ci * c, c)
        qraw = q_ref[rows, :]
        q = _silu(qraw)
        f = lb + (1.0 - lb) / (1.0 + jnp.exp(-f_ref[rows, :]))
        k = 1.0 - f
        logf = jnp.log(f)
        v = v_ref[rows, :].astype(BF16)
        hi = logf.astype(BF16)
        r1 = logf - hi.astype(F32)
        mid = r1.astype(BF16)
        lo = (r1 - mid.astype(F32)).astype(BF16)
        cum = (jnp.dot(tri, hi, preferred_element_type=F32) + jnp.dot(tri, mid, preferred_element_type=F32)
               + jnp.dot(tri, lo, preferred_element_type=F32))
        total = cum[0:1, :] if reverse else cum[c - 1:c, :]
        mid_ref = cum[c // 2:c // 2 + 1, :]
        q_in = (q * jnp.exp(cum)).astype(BF16)
        k_out = (k * jnp.exp(total - cum)).astype(BF16)
        q_loc = (q * jnp.exp(cum - mid_ref)).astype(BF16)
        k_loc = (k * jnp.exp(mid_ref - cum)).astype(BF16)
        decay = jnp.exp(total)
        for h in range(HG_HEADS):
            sl = slice(h * HG_DK, (h + 1) * HG_DK)
            st = states[h]
            o = lax.dot_general(q_in[:, sl], st.astype(BF16), (((1,), (1,)), ((), ())),
                                preferred_element_type=F32)
            att = lax.dot_general(q_loc[:, sl], k_loc[:, sl], (((1,), (1,)), ((), ())),
                                  preferred_element_type=F32)
            att = jnp.where(keep, att, 0.0)
            o = o + jnp.dot(att.astype(BF16), v[:, sl], preferred_element_type=F32)
            o_ref[rows, sl] = o.astype(o_ref.dtype)
            states[h] = st * decay[:, sl] + lax.dot_general(
                v[:, sl], k_out[:, sl], (((0,), (0,)), ((), ())), preferred_element_type=F32)
    for h in range(HG_HEADS):
        st_ref[h] = states[h]


def _gla(z, lb, tc, fcol, reverse):
    b, ttot, _ = z.shape
    w = HG_HEADS * HG_DK
    lc = SEQ_BLOCK
    n_all, n_ctx = ttot // lc, tc // lc

    def blk(s):
        return _seq_block(s, n_ctx, n_all, reverse)

    col = lambda cidx: pl.BlockSpec((None, lc, w), lambda bi, s: (bi, blk(s), cidx))
    return pl.pallas_call(
        functools.partial(_gla_kernel, lc=lc, reverse=reverse),
        grid=(b, n_all),
        in_specs=[col(1), col(fcol), col(4), pl.BlockSpec((1, w), lambda bi, s: (0, 0))],
        out_specs=pl.BlockSpec((None, lc, w), lambda bi, s: (bi, blk(s), 0)),
        out_shape=jax.ShapeDtypeStruct((b, ttot, w), BF16),
        scratch_shapes=[pltpu.VMEM((HG_HEADS, HG_DK, HG_DK), F32)],
        compiler_params=_cparams(("parallel", "arbitrary")),
        name="hgrn2_rev" if reverse else "hgrn2_fwd",
    )(z, z, z, lb.reshape(1, w))


def _final_kernel(x_ref, g_ref, o_ref):
    x = x_ref[...]
    ms = jnp.mean(x * x, axis=-1, keepdims=True)
    o_ref[...] = x * lax.rsqrt(ms + EPS) * g_ref[...]


def _final_norm(xa, g, tc):
    b, ttot, d = xa.shape
    t = ttot - tc
    tm = SEQ_BLOCK
    off = tc // tm
    return pl.pallas_call(
        _final_kernel,
        grid=(b, t // tm),
        in_specs=[pl.BlockSpec((None, tm, d), lambda bi, i: (bi, i + off, 0)),
                  pl.BlockSpec((1, d), lambda bi, i: (0, 0))],
        out_specs=pl.BlockSpec((None, tm, d), lambda bi, i: (bi, i, 0)),
        out_shape=jax.ShapeDtypeStruct((b, t, d), F32),
        compiler_params=_cparams(("parallel", "parallel")),
        name="final_norm",
    )(xa, g.reshape(1, d))


def _rope_tables(t, tc):
    pos = jnp.arange(t)
    row = (pos // GRID_W).astype(F32)
    col = (pos % GRID_W).astype(F32)
    inv = ROPE_BASE ** (-jnp.arange(ROPE_FREQS, dtype=F32) / ROPE_FREQS)
    ar, ac = row[:, None] * inv, col[:, None] * inv
    cos = jnp.concatenate([jnp.cos(ar), jnp.cos(ar), jnp.cos(ac), jnp.cos(ac)], axis=1)
    sin = jnp.concatenate([-jnp.sin(ar), jnp.sin(ar), -jnp.sin(ac), jnp.sin(ac)], axis=1)
    cos = jnp.concatenate([jnp.ones((tc, HEAD_DIM), F32), cos], axis=0)
    sin = jnp.concatenate([jnp.zeros((tc, HEAD_DIM), F32), sin], axis=0)
    return jnp.tile(cos, (1, LANES // HEAD_DIM)), jnp.tile(sin, (1, LANES // HEAD_DIM))


def _block_diag_dense(w):
    nblk, h, k = w.shape
    return jnp.einsum('nhk,nm->nhmk', w, jnp.eye(nblk, dtype=w.dtype)).reshape(nblk * h, nblk * k)


def kernel(x, c, ctx, c_ctx, w_mod, b_mod, norm_mix, norm_ffn, final_norm, w_in_ab, lru_conv_w, lru_conv_b, lru_wa, lru_ba, lru_wx, lru_bx, lru_lam, attn_sink, w_out_ab, ffn_w1, ffn_w3, ffn_w2, w_in_cd, s5_a_re, s5_a_im, s5_log_step, s5_b_re, s5_b_im, s5_c_re, s5_c_im, s5_d, s5_w_glu, s5_b_glu, hg_lb_raw, hg_norm, w_out_cd, moe_router, moe_w1, moe_w3, moe_w2):
    nb, t, d = x.shape
    tc = ctx.shape[1]
    depth = w_mod.shape[0]
    assert tc % SEQ_BLOCK == 0 and t % SEQ_BLOCK == 0 and t % GRID_W == 0

    xa = jnp.concatenate([ctx, x], axis=1)
    mod_rows = -(-(nb + 1) // SUBLANES) * SUBLANES
    cvec = jnp.zeros((mod_rows, d), F32).at[:nb].set(c).at[nb].set(c_ctx)
    mod = _modulation(cvec, w_mod, b_mod).reshape(depth, mod_rows, N_MOD, d)

    cos_t, sin_t = _rope_tables(t, tc)
    lb_soft = jax.nn.softmax(hg_lb_raw.astype(F32), axis=0)
    lb_table = jnp.cumsum(lb_soft, axis=0) - lb_soft[0:1]
    n_exp = moe_router.shape[2]

    for l in range(depth):
        j = l // 2
        mod_l = mod[l]
        if l % 2 == 0:
            z = _inproj(xa, mod_l, norm_mix[l], w_in_ab[j].astype(BF16), tc, nb)
            sp = jax.nn.softplus(-lru_lam[j])
            lru = []
            for dr, rev in enumerate((False, True)):
                lru.append(_lru(z, lru_conv_w[j], lru_conv_b[j],
                                _block_diag_dense(lru_wa[j, dr]).astype(BF16), lru_ba[j, dr],
                                _block_diag_dense(lru_wx[j, dr]).astype(BF16), lru_bx[j, dr],
                                sp[dr], tc, rev))
            sink = jnp.zeros((1, LANES), F32).at[0, :ATT_HEADS].set(attn_sink[j])
            att = _attention(z, cos_t, sin_t, sink, tc)
            xa = _outproj_ab(xa, mod_l, lru[0], lru[1], z, att, w_out_ab[j].astype(BF16), tc, nb)
            xa = _ffn(xa, mod_l, norm_ffn[l], ffn_w1[j].astype(BF16), ffn_w3[j].astype(BF16),
                      ffn_w2[j].astype(BF16), tc, nb)
        else:
            z = _inproj(xa, mod_l, norm_mix[l], w_in_cd[j].astype(BF16), tc, nb)
            s5w = _s5_weights(s5_a_re[j], s5_a_im[j], s5_log_step[j], s5_b_re[j], s5_b_im[j],
                              s5_c_re[j], s5_c_im[j])
            y5 = _s5(z, s5w, s5_d[j], tc, s5_d.shape[1])
            o_f = _gla(z, lb_table[j], tc, 2, False)
            o_b = _gla(z, lb_table[j], tc, 3, True)
            xa = _outproj_cd(xa, mod_l, y5, o_f, o_b, z, s5_w_glu[j].astype(BF16), s5_b_glu[j],
                             hg_norm[j], w_out_cd[j].astype(BF16), tc, nb)
            router_t = jnp.zeros((SUBLANES, d), F32).at[:n_exp].set(moe_router[j].T)
            xa = _moe(xa, mod_l, norm_ffn[l], router_t, moe_w1[j].astype(BF16),
                      moe_w3[j].astype(BF16), moe_w2[j].astype(BF16), tc, nb)
    return _final_norm(xa, final_norm, tc)
```

```python
import functools
import math

import jax
import jax.numpy as jnp
from jax import lax
from jax.experimental import pallas as pl
from jax.experimental.pallas import tpu as pltpu

F32 = jnp.float32
BF16 = jnp.bfloat16

EPS = 1e-6
GRID_W = 64
LRU_BLOCKS = 8
LRU_C = 8.0
CONV_W = 4
ATT_HEADS = 8
ATT_KV_HEADS = 2
ATT_GROUP = ATT_HEADS // ATT_KV_HEADS
HEAD_DIM = 64
WINDOW = 128
BLOCK_Q = 128
ROPE_FREQS = HEAD_DIM // 4
ROPE_BASE = 10000.0
S5_GROUP = 16
S5_STATE = 64
S5_CHUNK = 8
HG_HEADS = 4
HG_DK = 128
HG_CHUNK = 64
SEQ_BLOCK = 256
N_MOD = 6
MOE_PASS_ROWS = (256, 384, 512, 640)
LANES = 128
SUBLANES = 8
VMEM_LIMIT = 56 * 1024 * 1024
NEG = -1e30
LOG2E = math.log2(math.e)


def _cparams(sem):
    return pltpu.CompilerParams(dimension_semantics=sem, vmem_limit_bytes=VMEM_LIMIT)


def _row_tile(ttot):
    for tm in (768, 1024, 512, 256):
        if ttot % tm == 0:
            return tm
    raise ValueError(f"unsupported token count {ttot}")


def _sigmoid(x):
    return 0.5 * jnp.tanh(0.5 * x) + 0.5


def _silu(x):
    return x * _sigmoid(x)


def _gelu(x):
    return 0.5 * x * (1.0 + jnp.tanh(math.sqrt(2.0 / math.pi) * (x + 0.044715 * (x * x * x))))


def _norm_mod(x, g, mb, mc, row0, tc, shift_idx, scale_idx):
    ms = jnp.mean(x * x, axis=-1, keepdims=True)
    y = x * lax.rsqrt(ms + EPS) * g
    rows = row0 + lax.broadcasted_iota(jnp.int32, (x.shape[0], 1), 0)
    is_ctx = rows < tc
    scale = jnp.where(is_ctx, mc[scale_idx:scale_idx + 1], mb[scale_idx:scale_idx + 1])
    shift = jnp.where(is_ctx, mc[shift_idx:shift_idx + 1], mb[shift_idx:shift_idx + 1])
    return y * (1.0 + scale) + shift


def _gate_mod(mb, mc, idx, row0, tc, n):
    rows = row0 + lax.broadcasted_iota(jnp.int32, (n, 1), 0)
    return jnp.where(rows < tc, mc[idx:idx + 1], mb[idx:idx + 1])


def _mod_kernel(c_ref, w_ref, b_ref, o_ref):
    s = _silu(c_ref[...])
    o_ref[...] = jnp.dot(s, w_ref[...], preferred_element_type=F32,
                         precision=lax.Precision.HIGHEST) + b_ref[...]


def _modulation(cvec, w_mod, b_mod):
    depth, d, n = w_mod.shape
    tn = 1536 if n % 1536 == 0 else n
    rows = cvec.shape[0]
    return pl.pallas_call(
        _mod_kernel,
        grid=(depth, n // tn),
        in_specs=[pl.BlockSpec((rows, d), lambda l, j: (0, 0)),
                  pl.BlockSpec((None, d, tn), lambda l, j: (l, 0, j)),
                  pl.BlockSpec((None, 1, tn), lambda l, j: (l, 0, j))],
        out_specs=pl.BlockSpec((None, rows, tn), lambda l, j: (l, 0, j)),
        out_shape=jax.ShapeDtypeStruct((depth, rows, n), F32),
        compiler_params=_cparams(("arbitrary", "arbitrary")),
        name="modulation",
    )(cvec, w_mod, b_mod.reshape(depth, 1, n))


def _inproj_kernel(x_ref, mb_ref, mc_ref, g_ref, w_ref, o_ref, *, tm, tc):
    i = pl.program_id(1)
    h = _norm_mod(x_ref[...], g_ref[...], mb_ref[...], mc_ref[...], i * tm, tc, 0, 1)
    o_ref[...] = jnp.dot(h.astype(BF16), w_ref[...], preferred_element_type=F32)


def _inproj(xa, mod_l, g, w, tc, nb):
    b, ttot, d = xa.shape
    n = w.shape[1]
    tm = _row_tile(ttot)
    return pl.pallas_call(
        functools.partial(_inproj_kernel, tm=tm, tc=tc),
        grid=(b, ttot // tm),
        in_specs=[pl.BlockSpec((None, tm, d), lambda bi, i: (bi, i, 0)),
                  pl.BlockSpec((None, N_MOD, d), lambda bi, i: (bi, 0, 0)),
                  pl.BlockSpec((None, N_MOD, d), lambda bi, i: (nb, 0, 0)),
                  pl.BlockSpec((1, d), lambda bi, i: (0, 0)),
                  pl.BlockSpec((d, n), lambda bi, i: (0, 0))],
        out_specs=pl.BlockSpec((None, tm, n), lambda bi, i: (bi, i, 0)),
        out_shape=jax.ShapeDtypeStruct((b, ttot, n), F32),
        compiler_params=_cparams(("parallel", "parallel")),
        name="inproj",
    )(xa, mod_l, mod_l, g.reshape(1, d), w)


def _seq_block(step, n_ctx, n_all, reverse):
    if not reverse:
        return step
    return jnp.where(step < n_ctx, n_ctx - 1 - step, n_all - 1 - step + n_ctx)


def _lru_kernel(u_ref, hp_ref, hn_ref, cw_ref, cb_ref, wa_ref, ba_ref, wx_ref, bx_ref, sp_ref,
                o_ref, uc_ref, a_ref, b_ref, h_ref, *, lc, n_ctx, n_all, reverse):
    step = pl.program_id(1)
    blk = _seq_block(step, n_ctx, n_all, reverse)
    has_prev = jnp.logical_and(blk != 0, blk != n_ctx)
    has_next = jnp.logical_and(blk != n_ctx - 1, blk != n_all - 1)

    @pl.when(step == 0)
    def _():
        h_ref[...] = jnp.zeros_like(h_ref)

    w0, w1, w2, w3 = (cw_ref[j:j + 1, :] for j in range(CONV_W))
    cb = cb_ref[...]
    u = u_ref[...]
    uc_ref[...] = (cb + pltpu.roll(u, 2, 0) * w0 + pltpu.roll(u, 1, 0) * w1 + u * w2
                   + pltpu.roll(u, lc - 1, 0) * w3)
    row = lax.broadcasted_iota(jnp.int32, (SUBLANES, 1), 0)
    hp = jnp.where(has_prev, hp_ref[...], 0.0)
    hn = jnp.where(has_next, hn_ref[...], 0.0)
    u0 = u_ref[0:SUBLANES, :]
    u1 = u_ref[SUBLANES:2 * SUBLANES, :]
    uc_ref[0:SUBLANES, :] = (
        cb + jnp.where(row < 2, pltpu.roll(hp, 2, 0), pltpu.roll(u0, 2, 0)) * w0
        + jnp.where(row < 1, pltpu.roll(hp, 1, 0), pltpu.roll(u0, 1, 0)) * w1 + u0 * w2
        + jnp.where(row < SUBLANES - 1, pltpu.roll(u0, SUBLANES - 1, 0),
                    pltpu.roll(u1, SUBLANES - 1, 0)) * w3)
    ul = u_ref[lc - SUBLANES:lc, :]
    um = u_ref[lc - 2 * SUBLANES:lc - SUBLANES, :]
    uc_ref[lc - SUBLANES:lc, :] = (
        cb + jnp.where(row < 2, pltpu.roll(um, 2, 0), pltpu.roll(ul, 2, 0)) * w0
        + jnp.where(row < 1, pltpu.roll(um, 1, 0), pltpu.roll(ul, 1, 0)) * w1 + ul * w2
        + jnp.where(row < SUBLANES - 1, pltpu.roll(ul, SUBLANES - 1, 0),
                    pltpu.roll(hn, SUBLANES - 1, 0)) * w3)

    uc = uc_ref[...]
    ub = uc.astype(BF16)
    r = _sigmoid(jnp.dot(ub, wa_ref[...], preferred_element_type=F32) + ba_ref[...])
    gi = _sigmoid(jnp.dot(ub, wx_ref[...], preferred_element_type=F32) + bx_ref[...])
    log_a = -LRU_C * r * sp_ref[...]
    a = jnp.exp(log_a)
    b = jnp.sqrt(1.0 - jnp.exp(2.0 * log_a)) * (gi * uc)
    sub = lax.broadcasted_iota(jnp.int32, (lc, 1), 0) % SUBLANES
    dist = 1
    while dist < SUBLANES:
        inside = (sub < SUBLANES - dist) if reverse else (sub >= dist)
        shift = (lc - dist) if reverse else dist
        a_sh = jnp.where(inside, pltpu.roll(a, shift, 0), 1.0)
        b_sh = jnp.where(inside, pltpu.roll(b, shift, 0), 0.0)
        b = a * b_sh + b
        a = a * a_sh
        dist *= 2
    a_ref[...] = a
    b_ref[...] = b
    n_grp = lc // SUBLANES

    def body(i, carry):
        t = (n_grp - 1 - i) if reverse else i
        rows = pl.ds(pl.multiple_of(t * SUBLANES, SUBLANES), SUBLANES)
        h = a_ref[rows, :] * carry + b_ref[rows, :]
        b_ref[rows, :] = h
        return h[0:1, :] if reverse else h[SUBLANES - 1:SUBLANES, :]

    h_ref[0:1, :] = lax.fori_loop(0, n_grp, body, h_ref[0:1, :], unroll=4)
    o_ref[...] = b_ref[...].astype(o_ref.dtype)


def _lru(z, conv_w, conv_b, wa_bd, ba, wx_bd, bx, sp, tc, reverse):
    b, ttot, _ = z.shape
    w = conv_w.shape[1]
    lc = SEQ_BLOCK
    n_all, n_ctx = ttot // lc, tc // lc
    hb = lc // SUBLANES
    n_h = ttot // SUBLANES
    col = 1

    def blk(s):
        return _seq_block(s, n_ctx, n_all, reverse)

    vec = lambda: pl.BlockSpec((1, w), lambda bi, s: (0, 0))
    return pl.pallas_call(
        functools.partial(_lru_kernel, lc=lc, n_ctx=n_ctx, n_all=n_all, reverse=reverse),
        grid=(b, n_all),
        in_specs=[pl.BlockSpec((None, lc, w), lambda bi, s: (bi, blk(s), col)),
                  pl.BlockSpec((None, SUBLANES, w),
                               lambda bi, s: (bi, jnp.maximum(blk(s) * hb - 1, 0), col)),
                  pl.BlockSpec((None, SUBLANES, w),
                               lambda bi, s: (bi, jnp.minimum((blk(s) + 1) * hb, n_h - 1), col)),
                  pl.BlockSpec((CONV_W, w), lambda bi, s: (0, 0)),
                  vec(),
                  pl.BlockSpec((w, w), lambda bi, s: (0, 0)),
                  vec(),
                  pl.BlockSpec((w, w), lambda bi, s: (0, 0)),
                  vec(), vec()],
        out_specs=pl.BlockSpec((None, lc, w), lambda bi, s: (bi, blk(s), 0)),
        out_shape=jax.ShapeDtypeStruct((b, ttot, w), BF16),
        scratch_shapes=[pltpu.VMEM((lc, w), F32), pltpu.VMEM((lc, w), F32),
                        pltpu.VMEM((lc, w), F32), pltpu.VMEM((SUBLANES, w), F32)],
        compiler_params=_cparams(("parallel", "arbitrary")),
        name="rglru_rev" if reverse else "rglru_fwd",
    )(z, z, z, conv_w, conv_b.reshape(1, w), wa_bd, ba.reshape(1, w), wx_bd, bx.reshape(1, w),
      sp.reshape(1, w))


def _rope(x, cos, sin_signed):
    lane = lax.broadcasted_iota(jnp.int32, x.shape, 1)
    swapped = jnp.where(lane % 32 < 16, pltpu.roll(x, LANES - 16, 1), pltpu.roll(x, 16, 1))
    return x * cos + swapped * sin_signed


def _pair_layout(x, x_sw, head):
    lo = lax.broadcasted_iota(jnp.int32, x.shape, 1) < HEAD_DIM
    first, second = (x, x_sw) if head == 0 else (x_sw, x)
    return jnp.concatenate([jnp.where(lo, first, 0.0), jnp.where(lo, 0.0, second)],
                           axis=0).astype(BF16)


def _fold_lanes(x, op):
    acc = x[:, 0:LANES]
    for j in range(1, x.shape[1] // LANES):
        acc = op(acc, x[:, j * LANES:(j + 1) * LANES])
    return acc


def _attend(q_block, keys, vals, bias, sink_row, o_ref):
    nk = keys.shape[0]
    keys_sw = pltpu.roll(keys, HEAD_DIM, 1)
    vals_sw = pltpu.roll(vals, HEAD_DIM, 1)
    lo = lax.broadcasted_iota(jnp.int32, (BLOCK_Q, LANES), 1) < HEAD_DIM
    pairs_per_kv = ATT_GROUP // 2
    for kv in range(ATT_KV_HEADS):
        kab = _pair_layout(keys, keys_sw, kv)
        vab = _pair_layout(vals, vals_sw, kv)
        for g in range(kv * pairs_per_kv, (kv + 1) * pairs_per_kv):
            s = lax.dot_general(q_block(g), kab, (((1,), (1,)), ((), ())),
                                preferred_element_type=F32)
            halves = []
            recips = []
            for half in range(2):
                sh = s[:, half * nk:(half + 1) * nk]
                if bias is not None:
                    sh = sh + bias
                sk = sink_row[:, 2 * g + half:2 * g + half + 1]
                m = jnp.maximum(jnp.max(_fold_lanes(sh, jnp.maximum), axis=-1, keepdims=True), sk)
                p = jnp.exp2(sh - m)
                recips.append(1.0 / (jnp.sum(_fold_lanes(p, jnp.add), axis=-1, keepdims=True)
                                     + jnp.exp2(sk - m)))
                halves.append(p.astype(BF16))
            o = jnp.dot(jnp.concatenate(halves, axis=1), vab, preferred_element_type=F32)
            o_ref[:, g * LANES:(g + 1) * LANES] = (o * jnp.where(lo, recips[0], recips[1])
                                                   ).astype(o_ref.dtype)


def _attn_kernel(q_ref, kp_ref, k0_ref, kn_ref, vp_ref, v0_ref, vn_ref, kc_ref, vc_ref,
                 cq_ref, sq_ref, cp_ref, sp_ref, cn_ref, sn_ref, sink_ref, o_ref, *, n_ctx, n_lat):
    i = pl.program_id(1)
    qscale = HEAD_DIM ** -0.5 * LOG2E
    sink_row = sink_ref[...] * LOG2E

    @pl.when(i < n_ctx)
    def _():
        def q_block(g):
            return (q_ref[:, g * LANES:(g + 1) * LANES] * qscale).astype(BF16)
        _attend(q_block, kc_ref[...], vc_ref[...], None, sink_row, o_ref)

    @pl.when(i >= n_ctx)
    def _():
        n = i - n_ctx
        cq, sq = cq_ref[...], sq_ref[...]

        def q_block(g):
            return (_rope(q_ref[:, g * LANES:(g + 1) * LANES], cq, sq) * qscale).astype(BF16)

        kp = _rope(kp_ref[...], cp_ref[...], sp_ref[...])
        k0 = _rope(k0_ref[...], cq, sq)
        kn = _rope(kn_ref[...], cn_ref[...], sn_ref[...])
        keys = jnp.concatenate([kp, k0, kn, kc_ref[...]], axis=0)
        vals = jnp.concatenate([vp_ref[...], v0_ref[...], vn_ref[...], vc_ref[...]], axis=0)
        nk = keys.shape[0]
        qpos = n * BLOCK_Q + lax.broadcasted_iota(jnp.int32, (BLOCK_Q, nk), 0)
        col = lax.broadcasted_iota(jnp.int32, (BLOCK_Q, nk), 1)
        kpos = (n - 1) * BLOCK_Q + col
        local = (jnp.abs(qpos - kpos) <= WINDOW) & (kpos >= 0) & (kpos < n_lat * BLOCK_Q)
        bias = jnp.where(local | (col >= 3 * BLOCK_Q), 0.0, NEG)
        _attend(q_block, keys, vals, bias, sink_row, o_ref)


def _attention(z, cos_t, sin_t, sink, tc):
    b, ttot, _ = z.shape
    n_all, n_ctx = ttot // BLOCK_Q, tc // BLOCK_Q
    n_lat = n_all - n_ctx
    kcol = (2 * 512 + ATT_HEADS * HEAD_DIM) // LANES
    vcol = kcol + 1
    qcol = 2 * 512 // (ATT_HEADS * HEAD_DIM)

    def prev(i):
        return jnp.maximum(i - 1, 0)

    def nxt(i):
        return jnp.minimum(i + 1, n_all - 1)

    def rows(col, f):
        return pl.BlockSpec((None, BLOCK_Q, LANES), lambda bi, i: (bi, f(i), col))

    def tab(f):
        return pl.BlockSpec((BLOCK_Q, LANES), lambda bi, i: (f(i), 0))

    same = lambda i: i
    return pl.pallas_call(
        functools.partial(_attn_kernel, n_ctx=n_ctx, n_lat=n_lat),
        grid=(b, n_all),
        in_specs=[pl.BlockSpec((None, BLOCK_Q, ATT_HEADS * HEAD_DIM), lambda bi, i: (bi, i, qcol)),
                  rows(kcol, prev), rows(kcol, same), rows(kcol, nxt),
                  rows(vcol, prev), rows(vcol, same), rows(vcol, nxt),
                  pl.BlockSpec((None, tc, LANES), lambda bi, i: (bi, 0, kcol)),
                  pl.BlockSpec((None, tc, LANES), lambda bi, i: (bi, 0, vcol)),
                  tab(same), tab(same), tab(prev), tab(prev), tab(nxt), tab(nxt),
                  pl.BlockSpec((1, LANES), lambda bi, i: (0, 0))],
        out_specs=pl.BlockSpec((None, BLOCK_Q, ATT_HEADS * HEAD_DIM), lambda bi, i: (bi, i, 0)),
        out_shape=jax.ShapeDtypeStruct((b, ttot, ATT_HEADS * HEAD_DIM), BF16),
        compiler_params=_cparams(("parallel", "parallel")),
        name="window_gqa",
    )(z, z, z, z, z, z, z, z, z, cos_t, sin_t, cos_t, sin_t, cos_t, sin_t, sink)


def _outproj_ab_kernel(x_ref, mb_ref, mc_ref, lf_ref, lb_ref, g_ref, at_ref, w_ref, o_ref, *, tm, tc):
    i = pl.program_id(1)
    lru = (lf_ref[...].astype(F32) + lb_ref[...].astype(F32)) * _gelu(g_ref[...])
    mix = jnp.concatenate([lru.astype(BF16), at_ref[...]], axis=1)
    dx = jnp.dot(mix, w_ref[...], preferred_element_type=F32)
    gate = _gate_mod(mb_ref[...], mc_ref[...], 2, i * tm, tc, tm)
    o_ref[...] = x_ref[...] + gate * dx


def _outproj_ab(xa, mod_l, lru_f, lru_b, z, att, w_out, tc, nb):
    b, ttot, d = xa.shape
    w = lru_f.shape[2]
    tm = _row_tile(ttot)
    tile = lambda width, col: pl.BlockSpec((None, tm, width), lambda bi, i: (bi, i, col))
    return pl.pallas_call(
        functools.partial(_outproj_ab_kernel, tm=tm, tc=tc),
        grid=(b, ttot // tm),
        in_specs=[tile(d, 0),
                  pl.BlockSpec((None, N_MOD, d), lambda bi, i: (bi, 0, 0)),
                  pl.BlockSpec((None, N_MOD, d), lambda bi, i: (nb, 0, 0)),
                  tile(w, 0), tile(w, 0), tile(w, 0), tile(att.shape[2], 0),
                  pl.BlockSpec(w_out.shape, lambda bi, i: (0, 0))],
        out_specs=tile(d, 0),
        out_shape=jax.ShapeDtypeStruct(xa.shape, F32),
        compiler_params=_cparams(("parallel", "parallel")),
        name="outproj_ab",
    )(xa, mod_l, mod_l, lru_f, lru_b, z, att, w_out)


def _outproj_cd_kernel(x_ref, mb_ref, mc_ref, y_ref, of_ref, ob_ref, g_ref, wg_ref, bg_ref, gn_ref,
                       w_ref, o_ref, *, tm, tc):
    i = pl.program_id(1)
    y = _gelu(y_ref[...])
    s5 = y * _sigmoid(jnp.dot(y.astype(BF16), wg_ref[...], preferred_element_type=F32) + bg_ref[...])
    o = of_ref[...].astype(F32) + ob_ref[...].astype(F32)
    parts = []
    for h in range(HG_HEADS):
        oh = o[:, h * HG_DK:(h + 1) * HG_DK]
        ms = jnp.mean(oh * oh, axis=-1, keepdims=True)
        parts.append(oh * lax.rsqrt(ms + EPS) * gn_ref[...])
    hg = jnp.concatenate(parts, axis=1) * _silu(g_ref[...])
    mix = jnp.concatenate([s5.astype(BF16), hg.astype(BF16)], axis=1)
    dx = jnp.dot(mix, w_ref[...], preferred_element_type=F32)
    gate = _gate_mod(mb_ref[...], mc_ref[...], 2, i * tm, tc, tm)
    o_ref[...] = x_ref[...] + gate * dx


def _outproj_cd(xa, mod_l, y5, o_f, o_b, z, w_glu, b_glu, hg_norm, w_out, tc, nb):
    b, ttot, d = xa.shape
    w = y5.shape[2]
    tm = _row_tile(ttot)
    gcol = z.shape[2] // w - 1
    tile = lambda width, col: pl.BlockSpec((None, tm, width), lambda bi, i: (bi, i, col))
    return pl.pallas_call(
        functools.partial(_outproj_cd_kernel, tm=tm, tc=tc),
        grid=(b, ttot // tm),
        in_specs=[tile(d, 0),
                  pl.BlockSpec((None, N_MOD, d), lambda bi, i: (bi, 0, 0)),
                  pl.BlockSpec((None, N_MOD, d), lambda bi, i: (nb, 0, 0)),
                  tile(w, 0), tile(w, 0), tile(w, 0), tile(w, gcol),
                  pl.BlockSpec(w_glu.shape, lambda bi, i: (0, 0)),
                  pl.BlockSpec((1, w), lambda bi, i: (0, 0)),
                  pl.BlockSpec((1, HG_DK), lambda bi, i: (0, 0)),
                  pl.BlockSpec(w_out.shape, lambda bi, i: (0, 0))],
        out_specs=tile(d, 0),
        out_shape=jax.ShapeDtypeStruct(xa.shape, F32),
        compiler_params=_cparams(("parallel", "parallel")),
        name="outproj_cd",
    )(xa, mod_l, mod_l, y5, o_f, o_b, z, w_glu, b_glu.reshape(1, w), hg_norm.reshape(1, HG_DK), w_out)


def _ffn_kernel(x_ref, mb_ref, mc_ref, g_ref, w1_ref, w3_ref, w2_ref, o_ref, h_ref, acc_ref, *, tm, tc):
    i = pl.program_id(1)
    k = pl.program_id(2)

    @pl.when(k == 0)
    def _():
        h = _norm_mod(x_ref[...], g_ref[...], mb_ref[...], mc_ref[...], i * tm, tc, 3, 4)
        h_ref[...] = h.astype(BF16)
        acc_ref[...] = jnp.zeros_like(acc_ref)

    h = h_ref[...]
    a = jnp.dot(h, w1_ref[...], preferred_element_type=F32)
    c = jnp.dot(h, w3_ref[...], preferred_element_type=F32)
    acc_ref[...] += jnp.dot((_silu(a) * c).astype(BF16), w2_ref[...], preferred_element_type=F32)

    @pl.when(k == pl.num_programs(2) - 1)
    def _():
        gate = _gate_mod(mb_ref[...], mc_ref[...], 5, i * tm, tc, tm)
        o_ref[...] = x_ref[...] + gate * acc_ref[...]


def _ffn(xa, mod_l, g, w1, w3, w2, tc, nb):
    b, ttot, d = xa.shape
    f = w1.shape[1]
    tm = _row_tile(ttot)
    tf = f // 2 if (f // 2) % LANES == 0 else f
    return pl.pallas_call(
        functools.partial(_ffn_kernel, tm=tm, tc=tc),
        grid=(b, ttot // tm, f // tf),
        in_specs=[pl.BlockSpec((None, tm, d), lambda bi, i, k: (bi, i, 0)),
                  pl.BlockSpec((None, N_MOD, d), lambda bi, i, k: (bi, 0, 0)),
                  pl.BlockSpec((None, N_MOD, d), lambda bi, i, k: (nb, 0, 0)),
                  pl.BlockSpec((1, d), lambda bi, i, k: (0, 0)),
                  pl.BlockSpec((d, tf), lambda bi, i, k: (0, k)),
                  pl.BlockSpec((d, tf), lambda bi, i, k: (0, k)),
                  pl.BlockSpec((tf, d), lambda bi, i, k: (k, 0))],
        out_specs=pl.BlockSpec((None, tm, d), lambda bi, i, k: (bi, i, 0)),
        out_shape=jax.ShapeDtypeStruct(xa.shape, F32),
        scratch_shapes=[pltpu.VMEM((tm, d), BF16), pltpu.VMEM((tm, d), F32)],
        compiler_params=_cparams(("parallel", "parallel", "arbitrary")),
        name="ffn_swiglu",
    )(xa, mod_l, mod_l, g.reshape(1, d), w1, w3, w2)


def _router_kernel(x_ref, mb_ref, mc_ref, g_ref, r_ref, h_ref, wt_ref, cnt_ref, *, tm, tc, n_exp):
    i = pl.program_id(1)
    h = _norm_mod(x_ref[...], g_ref[...], mb_ref[...], mc_ref[...], i * tm, tc, 3, 4)
    h_ref[...] = h.astype(BF16)
    logits = lax.dot_general(r_ref[...], h, (((1,), (1,)), ((), ())), preferred_element_type=F32,
                             precision=lax.Precision.HIGHEST)
    sub = lax.broadcasted_iota(jnp.int32, logits.shape, 0)
    logits = jnp.where(sub < n_exp, logits, NEG)
    m1 = jnp.max(logits, axis=0, keepdims=True)
    i1 = jnp.min(jnp.where(logits == m1, sub, SUBLANES), axis=0, keepdims=True)
    rest = jnp.where(sub == i1, NEG, logits)
    m2 = jnp.max(rest, axis=0, keepdims=True)
    i2 = jnp.min(jnp.where(rest == m2, sub, SUBLANES), axis=0, keepdims=True)
    e2 = jnp.exp(m2 - m1)
    g1 = 1.0 / (1.0 + e2)
    wt = jnp.where(sub == i1, g1, 0.0) + jnp.where(sub == i2, e2 * g1, 0.0)
    wt_ref[...] = wt
    cnt = jnp.sum(jnp.where(wt > 0.0, 1.0, 0.0), axis=1, keepdims=True)
    cnt_ref[...] = jnp.broadcast_to(cnt, cnt_ref.shape).astype(jnp.int32)


def _router(xa, mod_l, g, router_t, tc, nb, n_exp):
    b, ttot, d = xa.shape
    tm = _row_tile(ttot)
    n_t = ttot // tm
    return pl.pallas_call(
        functools.partial(_router_kernel, tm=tm, tc=tc, n_exp=n_exp),
        grid=(b, n_t),
        in_specs=[pl.BlockSpec((None, tm, d), lambda bi, i: (bi, i, 0)),
                  pl.BlockSpec((None, N_MOD, d), lambda bi, i: (bi, 0, 0)),
                  pl.BlockSpec((None, N_MOD, d), lambda bi, i: (nb, 0, 0)),
                  pl.BlockSpec((1, d), lambda bi, i: (0, 0)),
                  pl.BlockSpec((SUBLANES, d), lambda bi, i: (0, 0))],
        out_specs=[pl.BlockSpec((tm, d), lambda bi, i: (bi * n_t + i, 0)),
                   pl.BlockSpec((SUBLANES, tm), lambda bi, i: (0, bi * n_t + i)),
                   pl.BlockSpec((None, SUBLANES, LANES), lambda bi, i: (bi * n_t + i, 0, 0))],
        out_shape=[jax.ShapeDtypeStruct((b * ttot, d), BF16),
                   jax.ShapeDtypeStruct((SUBLANES, b * ttot), F32),
                   jax.ShapeDtypeStruct((b * n_t, SUBLANES, LANES), jnp.int32)],
        compiler_params=_cparams(("parallel", "parallel")),
        name="moe_router",
    )(xa, mod_l, mod_l, g.reshape(1, d), router_t)


def _moe_kernel(cnt_ref, x_ref, *refs, tm, tc, n_exp, n_t, group):
    mb_refs = refs[:group]
    mc_ref, h_ref, wt_ref, w1_ref, w3_ref, w2_ref, o_ref, rank_ref = refs[group:]
    p = pl.program_id(0)
    e = pl.program_id(1)
    f = w1_ref.shape[1]
    half = (f // LANES + 1) // 2 * LANES
    f_split = [(0, half), (half, f)] if 0 < half < f else [(0, f)]

    @pl.when(e == 0)
    def _():
        o_ref[...] = jnp.zeros_like(o_ref)
        r_i = lax.broadcasted_iota(jnp.int32, (tm, tm), 0)
        c_i = lax.broadcasted_iota(jnp.int32, (tm, tm), 1)
        before = jnp.where(r_i < c_i, 1.0, 0.0).astype(BF16)
        for j in range(group):
            sel = wt_ref[:, j * tm:(j + 1) * tm] > 0.0
            rank = jnp.dot(jnp.where(sel, 1.0, 0.0).astype(BF16), before, preferred_element_type=F32)
            rank_ref[j] = jnp.where(sel, rank, -1.0)

    counts = [cnt_ref[(p * group + j) * n_exp + e] for j in range(group)]

    offsets = [jnp.int32(0)]
    for j in range(group - 1):
        offsets.append(offsets[-1] + counts[j])
    total = offsets[-1] + counts[-1]

    def expert_pass(m_rows, base):
        slot = (lax.broadcasted_iota(jnp.int32, (m_rows, tm), 0) + base).astype(F32)
        onehots = []
        xe = None
        gate = None
        for j in range(group):
            r_row = rank_ref[j, pl.ds(e, 1), :]
            r_row = jnp.where(r_row >= 0.0, r_row + offsets[j].astype(F32), -1.0)
            w_row = wt_ref[pl.ds(e, 1), j * tm:(j + 1) * tm]
            hit = r_row == slot
            onehot = jnp.where(hit, 1.0, 0.0).astype(BF16)
            onehots.append(onehot)
            xj = jnp.dot(onehot, h_ref[j * tm:(j + 1) * tm, :], preferred_element_type=F32)
            gj = jnp.sum(jnp.where(hit, w_row, 0.0), axis=1, keepdims=True)
            xe = xj if xe is None else xe + xj
            gate = gj if gate is None else gate + gj
        xe = xe.astype(BF16)
        y = None
        for f0, f1 in f_split:
            a = jnp.dot(xe, w1_ref[:, f0:f1], preferred_element_type=F32)
            c = jnp.dot(xe, w3_ref[:, f0:f1], preferred_element_type=F32)
            part = jnp.dot((_silu(a) * c * gate).astype(BF16), w2_ref[f0:f1, :],
                           preferred_element_type=F32)
            y = part if y is None else y + part
        y = y.astype(BF16)
        for j in range(group):
            o_ref[j * tm:(j + 1) * tm, :] += lax.dot_general(
                onehots[j], y, (((0,), (0,)), ((), ())), preferred_element_type=F32)

    lower = 0
    for m_rows in MOE_PASS_ROWS[:-1]:
        @pl.when(jnp.logical_and(total > lower, total <= m_rows))
        def _(m_rows=m_rows):
            expert_pass(m_rows, 0)
        lower = m_rows
    big = MOE_PASS_ROWS[-1]
    for base in range(0, group * tm, big):
        @pl.when(total > max(base, lower))
        def _(base=base):
            expert_pass(big, base)

    @pl.when(e == n_exp - 1)
    def _():
        for j in range(group):
            i = (p * group + j) % n_t
            gate5 = _gate_mod(mb_refs[j][...], mc_ref[...], 5, i * tm, tc, tm)
            rows = slice(j * tm, (j + 1) * tm)
            o_ref[rows, :] = x_ref[rows, :] + gate5 * o_ref[rows, :]


def _moe(xa, mod_l, g, router_t, w1, w3, w2, tc, nb):
    b, ttot, d = xa.shape
    n_exp, _, f = w1.shape
    tm = _row_tile(ttot)
    n_t = ttot // tm
    group = 2 if (b * n_t) % 2 == 0 else 1
    h, wt, cnt = _router(xa, mod_l, g, router_t, tc, nb, n_exp)
    cnt = cnt[:, :n_exp, 0].reshape(-1)
    rows = group * tm
    mods = [pl.BlockSpec((None, N_MOD, d), functools.partial(
        lambda p, e, c, j: ((p * group + j) // n_t, 0, 0), j=j)) for j in range(group)]
    grid_spec = pltpu.PrefetchScalarGridSpec(
        num_scalar_prefetch=1,
        grid=(b * n_t // group, n_exp),
        in_specs=[pl.BlockSpec((rows, d), lambda p, e, c: (p, 0), pipeline_mode=pl.Buffered(1))]
        + mods
        + [pl.BlockSpec((None, N_MOD, d), lambda p, e, c: (nb, 0, 0)),
           pl.BlockSpec((rows, d), lambda p, e, c: (p, 0)),
           pl.BlockSpec((SUBLANES, rows), lambda p, e, c: (0, p)),
           pl.BlockSpec((None, d, f), lambda p, e, c: (e, 0, 0)),
           pl.BlockSpec((None, d, f), lambda p, e, c: (e, 0, 0)),
           pl.BlockSpec((None, f, d), lambda p, e, c: (e, 0, 0))],
        out_specs=pl.BlockSpec((rows, d), lambda p, e, c: (p, 0)),
        scratch_shapes=[pltpu.VMEM((group, SUBLANES, tm), F32)])
    out = pl.pallas_call(
        functools.partial(_moe_kernel, tm=tm, tc=tc, n_exp=n_exp, n_t=n_t, group=group),
        grid_spec=grid_spec,
        out_shape=jax.ShapeDtypeStruct((b * ttot, d), F32),
        compiler_params=_cparams(("parallel", "arbitrary")),
        name="moe_experts",
    )(cnt, xa.reshape(b * ttot, d), *([mod_l] * group), mod_l, h, wt, w1, w3, w2)
    return out.reshape(b, ttot, d)


def _spread_groups(x, inner, row_group, n_grp):
    cols_in = x.shape[2]
    cols_out = cols_in * n_grp
    q = jnp.arange(cols_out)
    src = (q // (n_grp * inner)) * inner + q % inner
    tile = (jnp.arange(cols_in)[:, None] == src[None, :]).astype(BF16)
    col_group = (q // inner) % n_grp
    y = jnp.dot(x.astype(BF16), tile)
    return jnp.where(row_group[:, None] == col_group[None, :], y, jnp.zeros((), BF16))


def _s5_weights(a_re, a_im, log_step, b_re, b_im, c_re, c_im):
    L = S5_CHUNK
    lr = jnp.minimum(a_re, -1e-4)
    li = a_im
    dt = jnp.exp(log_step)[..., None]
    mag, ang = lr * dt, li * dt
    lbr, lbi = jnp.exp(mag) * jnp.cos(ang), jnp.exp(mag) * jnp.sin(ang)
    zr, zi = lbr - 1.0, lbi
    den = lr * lr + li * li
    fr = (zr * lr + zi * li) / den
    fi = (zi * lr - zr * li) / den
    bbr = fr[..., None] * b_re - fi[..., None] * b_im
    bbi = fr[..., None] * b_im + fi[..., None] * b_re

    def power(p):
        p = p[..., None, None, None].astype(F32)
        return jnp.exp(mag * p) * jnp.cos(ang * p), jnp.exp(mag * p) * jnp.sin(ang * p)

    n_dir, n_grp, n_st = a_re.shape
    gpb = LANES // S5_GROUP
    n_blk = n_grp // gpb
    s = jnp.arange(L)
    in_group = (jnp.arange(L * LANES) % LANES) // S5_GROUP
    st_group = jnp.arange(gpb * n_st) // n_st
    outs = []
    for d in range(n_dir):
        pr, pi = power(jnp.arange(L))
        pr, pi = pr[:, d], pi[:, d]
        cbr = (jnp.einsum('gcn,tgn,gnk->tgck', c_re[d], pr, bbr[d])
               - jnp.einsum('gcn,tgn,gnk->tgck', c_re[d], pi, bbi[d])
               - jnp.einsum('gcn,tgn,gnk->tgck', c_im[d], pr, bbi[d])
               - jnp.einsum('gcn,tgn,gnk->tgck', c_im[d], pi, bbr[d]))
        lag = (s[None, :] - s[:, None]) if d == 0 else (s[:, None] - s[None, :])
        kern = jnp.where((lag >= 0)[..., None, None, None], cbr[jnp.clip(lag, 0, L - 1)], 0.0)
        kern = kern.reshape(L, L, n_blk, gpb, S5_GROUP, S5_GROUP).transpose(2, 0, 3, 5, 1, 4)
        m = _spread_groups(kern.reshape(n_blk, L * LANES, L * S5_GROUP), S5_GROUP, in_group, gpb)
        qr, qi = power((L - 1 - s) if d == 0 else s)
        qr, qi = qr[:, d], qi[:, d]
        str_ = qr[..., None] * bbr[d] - qi[..., None] * bbi[d]
        sti = qr[..., None] * bbi[d] + qi[..., None] * bbr[d]

        def to_state(x):
            x = x.reshape(L, n_blk, gpb, n_st, S5_GROUP).transpose(1, 0, 2, 4, 3)
            return _spread_groups(x.reshape(n_blk, L * LANES, n_st), n_st, in_group, gpb)

        wst = jnp.concatenate([to_state(str_), to_state(sti)], axis=2)
        rr, ri = power((s + 1) if d == 0 else (L - s))
        rr, ri = rr[:, d], ri[:, d]
        wr = c_re[d][None] * rr[:, :, None, :] - c_im[d][None] * ri[:, :, None, :]
        wi = -(c_re[d][None] * ri[:, :, None, :] + c_im[d][None] * rr[:, :, None, :])

        def from_state(x):
            x = x.reshape(L, n_blk, gpb, S5_GROUP, n_st).transpose(1, 2, 4, 0, 3)
            return _spread_groups(x.reshape(n_blk, gpb * n_st, L * S5_GROUP), S5_GROUP, st_group, gpb)

        wout = jnp.concatenate([from_state(wr), from_state(wi)], axis=1)
        ler, lei = jnp.exp(mag[d] * L) * jnp.cos(ang[d] * L), jnp.exp(mag[d] * L) * jnp.sin(ang[d] * L)
        lam_l = jnp.concatenate([ler.reshape(n_blk, 1, gpb * n_st), lei.reshape(n_blk, 1, gpb * n_st)],
                                axis=2)
        outs.append((m, wst, wout, lam_l))
    return tuple(jnp.stack([o[k] for o in outs]) for k in range(4))


def _s5_kernel(u_ref, m_ref, wst_ref, wout_ref, lam_ref, d_ref, o_ref, x_ref, hp_ref, *, rows, rows_ctx):
    L = S5_CHUNK
    dr = pl.program_id(2)
    ns = lam_ref.shape[1] // 2
    u = jnp.concatenate([u_ref[pl.ds(s, rows, stride=L), :] for s in range(L)], axis=1)
    ub = u.astype(BF16)
    x_ref[...] = jnp.dot(ub, wst_ref[...], preferred_element_type=F32)
    lr = lam_ref[:, 0:ns]
    li = lam_ref[:, ns:2 * ns]

    def visit(r, carry):
        hr, hi = carry
        hp_ref[pl.ds(r, 1), 0:ns] = hr
        hp_ref[pl.ds(r, 1), ns:2 * ns] = hi
        xr = x_ref[pl.ds(r, 1), 0:ns]
        xi = x_ref[pl.ds(r, 1), ns:2 * ns]
        return lr * hr - li * hi + xr, lr * hi + li * hr + xi

    zero = (jnp.zeros((1, ns), F32), jnp.zeros((1, ns), F32))

    @pl.when(dr == 0)
    def _():
        lax.fori_loop(0, rows, visit, zero, unroll=4)

    @pl.when(dr == 1)
    def _():
        c = lax.fori_loop(0, rows_ctx, lambda t, c: visit(rows_ctx - 1 - t, c), zero, unroll=4)
        lax.fori_loop(0, rows - rows_ctx, lambda t, c: visit(rows - 1 - t, c), c, unroll=4)

    y = (jnp.dot(ub, m_ref[...], preferred_element_type=F32)
         + jnp.dot(hp_ref[...].astype(BF16), wout_ref[...], preferred_element_type=F32))

    @pl.when(dr == 0)
    def _():
        for s in range(L):
            o_ref[pl.ds(s, rows, stride=L), :] = (y[:, s * LANES:(s + 1) * LANES]
                                                  + d_ref[...] * u[:, s * LANES:(s + 1) * LANES])

    @pl.when(dr == 1)
    def _():
        for s in range(L):
            o_ref[pl.ds(s, rows, stride=L), :] += y[:, s * LANES:(s + 1) * LANES]


def _s5(z, weights, dvec, tc, width):
    b, ttot, _ = z.shape
    m, wst, wout, lam_l = weights
    n_dir, n_blk = m.shape[0], m.shape[1]
    rows, rows_ctx = ttot // S5_CHUNK, tc // S5_CHUNK
    wspec = lambda a: pl.BlockSpec((None, None) + a.shape[2:], lambda bi, j, dr: (dr, j, 0, 0))
    return pl.pallas_call(
        functools.partial(_s5_kernel, rows=rows, rows_ctx=rows_ctx),
        grid=(b, n_blk, n_dir),
        in_specs=[pl.BlockSpec((None, ttot, LANES), lambda bi, j, dr: (bi, 0, j)),
                  wspec(m), wspec(wst), wspec(wout), wspec(lam_l),
                  pl.BlockSpec((1, LANES), lambda bi, j, dr: (0, j))],
        out_specs=pl.BlockSpec((None, ttot, LANES), lambda bi, j, dr: (bi, 0, j)),
        out_shape=jax.ShapeDtypeStruct((b, ttot, width), F32),
        scratch_shapes=[pltpu.VMEM((rows, lam_l.shape[3]), F32), pltpu.VMEM((rows, lam_l.shape[3]), F32)],
        compiler_params=_cparams(("parallel", "parallel", "arbitrary")),
        name="s5_bidir",
    )(z, m, wst, wout, lam_l, dvec.reshape(1, width))


def _gla_kernel(q_ref, f_ref, v_ref, lb_ref, o_ref, st_ref, *, lc, reverse):
    step = pl.program_id(1)

    @pl.when(step == 0)
    def _():
        st_ref[...] = jnp.zeros_like(st_ref)

    c = HG_CHUNK
    r_i = lax.broadcasted_iota(jnp.int32, (c, c), 0)
    c_i = lax.broadcasted_iota(jnp.int32, (c, c), 1)
    keep = (c_i >= r_i) if reverse else (c_i <= r_i)
    tri = jnp.where(keep, 1.0, 0.0).astype(BF16)
    lb = lb_ref[...]
    n_chunks = lc // c
    order = range(n_chunks - 1, -1, -1) if reverse else range(n_chunks)
    states = [st_ref[h] for h in range(HG_HEADS)]
    for ci in order:
        rows = pl.ds(ci * c, c)
        qraw = q_ref[rows, :]
        q = _silu(qraw)
        f = lb + (1.0 - lb) / (1.0 + jnp.exp(-f_ref[rows, :]))
        k = 1.0 - f
        logf = jnp.log(f)
        v = v_ref[rows, :].astype(BF16)
        hi = logf.astype(BF16)
        r1 = logf - hi.astype(F32)
        mid = r1.astype(BF16)
        lo = (r1 - mid.astype(F32)).astype(BF16)
        cum = (jnp.dot(tri, hi, preferred_element_type=F32) + jnp.dot(tri, mid, preferred_element_type=F32)
               + jnp.dot(tri, lo, preferred_element_type=F32))
        total = cum[0:1, :] if reverse else cum[c - 1:c, :]
        mid_ref = cum[c // 2:c // 2 + 1, :]
        q_in = (q * jnp.exp(cum)).astype(BF16)
        k_out = (k * jnp.exp(total - cum)).astype(BF16)
        q_loc = (q * jnp.exp(cum - mid_ref)).astype(BF16)
        k_loc = (k * jnp.exp(mid_ref - cum)).astype(BF16)
        decay = jnp.exp(total)
        for h in range(HG_HEADS):
            sl = slice(h * HG_DK, (h + 1) * HG_DK)
            st = states[h]
            o = lax.dot_general(q_in[:, sl], st.astype(BF16), (((1,), (1,)), ((), ())),
                                preferred_element_type=F32)
            att = lax.dot_general(q_loc[:, sl], k_loc[:, sl], (((1,), (1,)), ((), ())),
                                  preferred_element_type=F32)
            att = jnp.where(keep, att, 0.0)
            o = o + jnp.dot(att.astype(BF16), v[:, sl], preferred_element_type=F32)
            o_ref[rows, sl] = o.astype(o_ref.dtype)
            states[h] = st * decay[:, sl] + lax.dot_general(
                v[:, sl], k_out[:, sl], (((0,), (0,)), ((), ())), preferred_element_type=F32)
    for h in range(HG_HEADS):
        st_ref[h] = states[h]


def _gla(z, lb, tc, fcol, reverse):
    b, ttot, _ = z.shape
    w = HG_HEADS * HG_DK
    lc = SEQ_BLOCK
    n_all, n_ctx = ttot // lc, tc // lc

    def blk(s):
        return _seq_block(s, n_ctx, n_all, reverse)

    col = lambda cidx: pl.BlockSpec((None, lc, w), lambda bi, s: (bi, blk(s), cidx))
    return pl.pallas_call(
        functools.partial(_gla_kernel, lc=lc, reverse=reverse),
        grid=(b, n_all),
        in_specs=[col(1), col(fcol), col(4), pl.BlockSpec((1, w), lambda bi, s: (0, 0))],
        out_specs=pl.BlockSpec((None, lc, w), lambda bi, s: (bi, blk(s), 0)),
        out_shape=jax.ShapeDtypeStruct((b, ttot, w), BF16),
        scratch_shapes=[pltpu.VMEM((HG_HEADS, HG_DK, HG_DK), F32)],
        compiler_params=_cparams(("parallel", "arbitrary")),
        name="hgrn2_rev" if reverse else "hgrn2_fwd",
    )(z, z, z, lb.reshape(1, w))


def _final_kernel(x_ref, g_ref, o_ref):
    x = x_ref[...]
    ms = jnp.mean(x * x, axis=-1, keepdims=True)
    o_ref[...] = x * lax.rsqrt(ms + EPS) * g_ref[...]


def _final_norm(xa, g, tc):
    b, ttot, d = xa.shape
    t = ttot - tc
    tm = SEQ_BLOCK
    off = tc // tm
    return pl.pallas_call(
        _final_kernel,
        grid=(b, t // tm),
        in_specs=[pl.BlockSpec((None, tm, d), lambda bi, i: (bi, i + off, 0)),
                  pl.BlockSpec((1, d), lambda bi, i: (0, 0))],
        out_specs=pl.BlockSpec((None, tm, d), lambda bi, i: (bi, i, 0)),
        out_shape=jax.ShapeDtypeStruct((b, t, d), F32),
        compiler_params=_cparams(("parallel", "parallel")),
        name="final_norm",
    )(xa, g.reshape(1, d))


def _rope_tables(t, tc):
    pos = jnp.arange(t)
    row = (pos // GRID_W).astype(F32)
    col = (pos % GRID_W).astype(F32)
    inv = ROPE_BASE ** (-jnp.arange(ROPE_FREQS, dtype=F32) / ROPE_FREQS)
    ar, ac = row[:, None] * inv, col[:, None] * inv
    cos = jnp.concatenate([jnp.cos(ar), jnp.cos(ar), jnp.cos(ac), jnp.cos(ac)], axis=1)
    sin = jnp.concatenate([-jnp.sin(ar), jnp.sin(ar), -jnp.sin(ac), jnp.sin(ac)], axis=1)
    cos = jnp.concatenate([jnp.ones((tc, HEAD_DIM), F32), cos], axis=0)
    sin = jnp.concatenate([jnp.zeros((tc, HEAD_DIM), F32), sin], axis=0)
    return jnp.tile(cos, (1, LANES // HEAD_DIM)), jnp.tile(sin, (1, LANES // HEAD_DIM))


def _block_diag_dense(w):
    nblk, h, k = w.shape
    return jnp.einsum('nhk,nm->nhmk', w, jnp.eye(nblk, dtype=w.dtype)).reshape(nblk * h, nblk * k)


def kernel(x, c, ctx, c_ctx, w_mod, b_mod, norm_mix, norm_ffn, final_norm, w_in_ab, lru_conv_w, lru_conv_b, lru_wa, lru_ba, lru_wx, lru_bx, lru_lam, attn_sink, w_out_ab, ffn_w1, ffn_w3, ffn_w2, w_in_cd, s5_a_re, s5_a_im, s5_log_step, s5_b_re, s5_b_im, s5_c_re, s5_c_im, s5_d, s5_w_glu, s5_b_glu, hg_lb_raw, hg_norm, w_out_cd, moe_router, moe_w1, moe_w3, moe_w2):
    nb, t, d = x.shape
    tc = ctx.shape[1]
    depth = w_mod.shape[0]
    assert tc % SEQ_BLOCK == 0 and t % SEQ_BLOCK == 0 and t % GRID_W == 0

    xa = jnp.concatenate([ctx, x], axis=1)
    mod_rows = -(-(nb + 1) // SUBLANES) * SUBLANES
    cvec = jnp.zeros((mod_rows, d), F32).at[:nb].set(c).at[nb].set(c_ctx)
    mod = _modulation(cvec, w_mod, b_mod).reshape(depth, mod_rows, N_MOD, d)

    cos_t, sin_t = _rope_tables(t, tc)
    lb_soft = jax.nn.softmax(hg_lb_raw.astype(F32), axis=0)
    lb_table = jnp.cumsum(lb_soft, axis=0) - lb_soft[0:1]
    n_exp = moe_router.shape[2]

    for l in range(depth):
        j = l // 2
        mod_l = mod[l]
        if l % 2 == 0:
            z = _inproj(xa, mod_l, norm_mix[l], w_in_ab[j].astype(BF16), tc, nb)
            sp = jax.nn.softplus(-lru_lam[j])
            lru = []
            for dr, rev in enumerate((False, True)):
                lru.append(_lru(z, lru_conv_w[j], lru_conv_b[j],
                                _block_diag_dense(lru_wa[j, dr]).astype(BF16), lru_ba[j, dr],
                                _block_diag_dense(lru_wx[j, dr]).astype(BF16), lru_bx[j, dr],
                                sp[dr], tc, rev))
            sink = jnp.zeros((1, LANES), F32).at[0, :ATT_HEADS].set(attn_sink[j])
            att = _attention(z, cos_t, sin_t, sink, tc)
            xa = _outproj_ab(xa, mod_l, lru[0], lru[1], z, att, w_out_ab[j].astype(BF16), tc, nb)
            xa = _ffn(xa, mod_l, norm_ffn[l], ffn_w1[j].astype(BF16), ffn_w3[j].astype(BF16),
                      ffn_w2[j].astype(BF16), tc, nb)
        else:
            z = _inproj(xa, mod_l, norm_mix[l], w_in_cd[j].astype(BF16), tc, nb)
            s5w = _s5_weights(s5_a_re[j], s5_a_im[j], s5_log_step[j], s5_b_re[j], s5_b_im[j],
                              s5_c_re[j], s5_c_im[j])
            y5 = _s5(z, s5w, s5_d[j], tc, s5_d.shape[1])
            o_f = _gla(z, lb_table[j], tc, 2, False)
            o_b = _gla(z, lb_table[j], tc, 3, True)
            xa = _outproj_cd(xa, mod_l, y5, o_f, o_b, z, s5_w_glu[j].astype(BF16), s5_b_glu[j],
                             hg_norm[j], w_out_cd[j].astype(BF16), tc, nb)
            router_t = jnp.zeros((SUBLANES, d), F32).at[:n_exp].set(moe_router[j].T)
            xa = _moe(xa, mod_l, norm_ffn[l], router_t, moe_w1[j].astype(BF16),
                      moe_w3[j].astype(BF16), moe_w2[j].astype(BF16), tc, nb)
    return _final_norm(xa, final_norm, tc)
```

```python
import functools
import math

import jax
import jax.numpy as jnp
from jax import lax
from jax.experimental import pallas as pl
from jax.experimental.pallas import tpu as pltpu

F32 = jnp.float32
BF16 = jnp.bfloat16

EPS = 1e-6
GRID_W = 64
LRU_BLOCKS = 8
LRU_C = 8.0
CONV_W = 4
ATT_HEADS = 8
ATT_KV_HEADS = 2
ATT_GROUP = ATT_HEADS // ATT_KV_HEADS
HEAD_DIM = 64
WINDOW = 128
BLOCK_Q = 128
ROPE_FREQS = HEAD_DIM // 4
ROPE_BASE = 10000.0
S5_GROUP = 16
S5_STATE = 64
S5_CHUNK = 8
HG_HEADS = 4
HG_DK = 128
HG_CHUNK = 64
SEQ_BLOCK = 256
N_MOD = 6
MOE_PASS_ROWS = 256
LANES = 128
SUBLANES = 8
VMEM_LIMIT = 56 * 1024 * 1024
NEG = -1e30
LOG2E = math.log2(math.e)


def _cparams(sem):
    return pltpu.CompilerParams(dimension_semantics=sem, vmem_limit_bytes=VMEM_LIMIT)


def _row_tile(ttot):
    for tm in (768, 1024, 512, 256):
        if ttot % tm == 0:
            return tm
    raise ValueError(f"unsupported token count {ttot}")


def _sigmoid(x):
    return 0.5 * jnp.tanh(0.5 * x) + 0.5


def _silu(x):
    return x * _sigmoid(x)


def _gelu(x):
    return 0.5 * x * (1.0 + jnp.tanh(math.sqrt(2.0 / math.pi) * (x + 0.044715 * (x * x * x))))


def _norm_mod(x, g, mb, mc, row0, tc, shift_idx, scale_idx):
    ms = jnp.mean(x * x, axis=-1, keepdims=True)
    y = x * lax.rsqrt(ms + EPS) * g
    rows = row0 + lax.broadcasted_iota(jnp.int32, (x.shape[0], 1), 0)
    is_ctx = rows < tc
    scale = jnp.where(is_ctx, mc[scale_idx:scale_idx + 1], mb[scale_idx:scale_idx + 1])
    shift = jnp.where(is_ctx, mc[shift_idx:shift_idx + 1], mb[shift_idx:shift_idx + 1])
    return y * (1.0 + scale) + shift


def _gate_mod(mb, mc, idx, row0, tc, n):
    rows = row0 + lax.broadcasted_iota(jnp.int32, (n, 1), 0)
    return jnp.where(rows < tc, mc[idx:idx + 1], mb[idx:idx + 1])


def _mod_kernel(c_ref, w_ref, b_ref, o_ref):
    s = _silu(c_ref[...])
    o_ref[...] = jnp.dot(s, w_ref[...], preferred_element_type=F32,
                         precision=lax.Precision.HIGHEST) + b_ref[...]


def _modulation(cvec, w_mod, b_mod):
    depth, d, n = w_mod.shape
    tn = 1536 if n % 1536 == 0 else n
    rows = cvec.shape[0]
    return pl.pallas_call(
        _mod_kernel,
        grid=(depth, n // tn),
        in_specs=[pl.BlockSpec((rows, d), lambda l, j: (0, 0)),
                  pl.BlockSpec((None, d, tn), lambda l, j: (l, 0, j)),
                  pl.BlockSpec((None, 1, tn), lambda l, j: (l, 0, j))],
        out_specs=pl.BlockSpec((None, rows, tn), lambda l, j: (l, 0, j)),
        out_shape=jax.ShapeDtypeStruct((depth, rows, n), F32),
        compiler_params=_cparams(("arbitrary", "arbitrary")),
        name="modulation",
    )(cvec, w_mod, b_mod.reshape(depth, 1, n))


def _inproj_kernel(x_ref, mb_ref, mc_ref, g_ref, w_ref, o_ref, *, tm, tc):
    i = pl.program_id(1)
    h = _norm_mod(x_ref[...], g_ref[...], mb_ref[...], mc_ref[...], i * tm, tc, 0, 1)
    o_ref[...] = jnp.dot(h.astype(BF16), w_ref[...], preferred_element_type=F32)


def _inproj(xa, mod_l, g, w, tc, nb):
    b, ttot, d = xa.shape
    n = w.shape[1]
    tm = _row_tile(ttot)
    return pl.pallas_call(
        functools.partial(_inproj_kernel, tm=tm, tc=tc),
        grid=(b, ttot // tm),
        in_specs=[pl.BlockSpec((None, tm, d), lambda bi, i: (bi, i, 0)),
                  pl.BlockSpec((None, N_MOD, d), lambda bi, i: (bi, 0, 0)),
                  pl.BlockSpec((None, N_MOD, d), lambda bi, i: (nb, 0, 0)),
                  pl.BlockSpec((1, d), lambda bi, i: (0, 0)),
                  pl.BlockSpec((d, n), lambda bi, i: (0, 0))],
        out_specs=pl.BlockSpec((None, tm, n), lambda bi, i: (bi, i, 0)),
        out_shape=jax.ShapeDtypeStruct((b, ttot, n), F32),
        compiler_params=_cparams(("parallel", "parallel")),
        name="inproj",
    )(xa, mod_l, mod_l, g.reshape(1, d), w)


def _seq_block(step, n_ctx, n_all, reverse):
    if not reverse:
        return step
    return jnp.where(step < n_ctx, n_ctx - 1 - step, n_all - 1 - step + n_ctx)


def _lru_kernel(u_ref, hp_ref, hn_ref, cw_ref, cb_ref, wa_ref, ba_ref, wx_ref, bx_ref, sp_ref,
                o_ref, uc_ref, a_ref, b_ref, h_ref, *, lc, n_ctx, n_all, reverse):
    step = pl.program_id(1)
    blk = _seq_block(step, n_ctx, n_all, reverse)
    has_prev = jnp.logical_and(blk != 0, blk != n_ctx)
    has_next = jnp.logical_and(blk != n_ctx - 1, blk != n_all - 1)

    @pl.when(step == 0)
    def _():
        h_ref[...] = jnp.zeros_like(h_ref)

    w0, w1, w2, w3 = (cw_ref[j:j + 1, :] for j in range(CONV_W))
    cb = cb_ref[...]
    u = u_ref[...]
    uc_ref[...] = (cb + pltpu.roll(u, 2, 0) * w0 + pltpu.roll(u, 1, 0) * w1 + u * w2
                   + pltpu.roll(u, lc - 1, 0) * w3)
    row = lax.broadcasted_iota(jnp.int32, (SUBLANES, 1), 0)
    hp = jnp.where(has_prev, hp_ref[...], 0.0)
    hn = jnp.where(has_next, hn_ref[...], 0.0)
    u0 = u_ref[0:SUBLANES, :]
    u1 = u_ref[SUBLANES:2 * SUBLANES, :]
    uc_ref[0:SUBLANES, :] = (
        cb + jnp.where(row < 2, pltpu.roll(hp, 2, 0), pltpu.roll(u0, 2, 0)) * w0
        + jnp.where(row < 1, pltpu.roll(hp, 1, 0), pltpu.roll(u0, 1, 0)) * w1 + u0 * w2
        + jnp.where(row < SUBLANES - 1, pltpu.roll(u0, SUBLANES - 1, 0),
                    pltpu.roll(u1, SUBLANES - 1, 0)) * w3)
    ul = u_ref[lc - SUBLANES:lc, :]
    um = u_ref[lc - 2 * SUBLANES:lc - SUBLANES, :]
    uc_ref[lc - SUBLANES:lc, :] = (
        cb + jnp.where(row < 2, pltpu.roll(um, 2, 0), pltpu.roll(ul, 2, 0)) * w0
        + jnp.where(row < 1, pltpu.roll(um, 1, 0), pltpu.roll(ul, 1, 0)) * w1 + ul * w2
        + jnp.where(row < SUBLANES - 1, pltpu.roll(ul, SUBLANES - 1, 0),
                    pltpu.roll(hn, SUBLANES - 1, 0)) * w3)

    uc = uc_ref[...]
    ub = uc.astype(BF16)
    r = _sigmoid(jnp.dot(ub, wa_ref[...], preferred_element_type=F32) + ba_ref[...])
    gi = _sigmoid(jnp.dot(ub, wx_ref[...], preferred_element_type=F32) + bx_ref[...])
    log_a = -LRU_C * r * sp_ref[...]
    a = jnp.exp(log_a)
    b = jnp.sqrt(1.0 - jnp.exp(2.0 * log_a)) * (gi * uc)
    sub = lax.broadcasted_iota(jnp.int32, (lc, 1), 0) % SUBLANES
    dist = 1
    while dist < SUBLANES:
        inside = (sub < SUBLANES - dist) if reverse else (sub >= dist)
        shift = (lc - dist) if reverse else dist
        a_sh = jnp.where(inside, pltpu.roll(a, shift, 0), 1.0)
        b_sh = jnp.where(inside, pltpu.roll(b, shift, 0), 0.0)
        b = a * b_sh + b
        a = a * a_sh
        dist *= 2
    a_ref[...] = a
    b_ref[...] = b
    n_grp = lc // SUBLANES

    def body(i, carry):
        t = (n_grp - 1 - i) if reverse else i
        rows = pl.ds(pl.multiple_of(t * SUBLANES, SUBLANES), SUBLANES)
        h = a_ref[rows, :] * carry + b_ref[rows, :]
        b_ref[rows, :] = h
        return h[0:1, :] if reverse else h[SUBLANES - 1:SUBLANES, :]

    h_ref[0:1, :] = lax.fori_loop(0, n_grp, body, h_ref[0:1, :], unroll=4)
    o_ref[...] = b_ref[...].astype(o_ref.dtype)


def _lru(z, conv_w, conv_b, wa_bd, ba, wx_bd, bx, sp, tc, reverse):
    b, ttot, _ = z.shape
    w = conv_w.shape[1]
    lc = SEQ_BLOCK
    n_all, n_ctx = ttot // lc, tc // lc
    hb = lc // SUBLANES
    n_h = ttot // SUBLANES
    col = 1

    def blk(s):
        return _seq_block(s, n_ctx, n_all, reverse)

    vec = lambda: pl.BlockSpec((1, w), lambda bi, s: (0, 0))
    return pl.pallas_call(
        functools.partial(_lru_kernel, lc=lc, n_ctx=n_ctx, n_all=n_all, reverse=reverse),
        grid=(b, n_all),
        in_specs=[pl.BlockSpec((None, lc, w), lambda bi, s: (bi, blk(s), col)),
                  pl.BlockSpec((None, SUBLANES, w),
                               lambda bi, s: (bi, jnp.maximum(blk(s) * hb - 1, 0), col)),
                  pl.BlockSpec((None, SUBLANES, w),
                               lambda bi, s: (bi, jnp.minimum((blk(s) + 1) * hb, n_h - 1), col)),
                  pl.BlockSpec((CONV_W, w), lambda bi, s: (0, 0)),
                  vec(),
                  pl.BlockSpec((w, w), lambda bi, s: (0, 0)),
                  vec(),
                  pl.BlockSpec((w, w), lambda bi, s: (0, 0)),
                  vec(), vec()],
        out_specs=pl.BlockSpec((None, lc, w), lambda bi, s: (bi, blk(s), 0)),
        out_shape=jax.ShapeDtypeStruct((b, ttot, w), BF16),
        scratch_shapes=[pltpu.VMEM((lc, w), F32), pltpu.VMEM((lc, w), F32),
                        pltpu.VMEM((lc, w), F32), pltpu.VMEM((SUBLANES, w), F32)],
        compiler_params=_cparams(("parallel", "arbitrary")),
        name="rglru_rev" if reverse else "rglru_fwd",
    )(z, z, z, conv_w, conv_b.reshape(1, w), wa_bd, ba.reshape(1, w), wx_bd, bx.reshape(1, w),
      sp.reshape(1, w))


def _rope(x, cos, sin_signed):
    lane = lax.broadcasted_iota(jnp.int32, x.shape, 1)
    swapped = jnp.where(lane % 32 < 16, pltpu.roll(x, LANES - 16, 1), pltpu.roll(x, 16, 1))
    return x * cos + swapped * sin_signed


def _pair_layout(x, x_sw, head):
    lo = lax.broadcasted_iota(jnp.int32, x.shape, 1) < HEAD_DIM
    first, second = (x, x_sw) if head == 0 else (x_sw, x)
    return jnp.concatenate([jnp.where(lo, first, 0.0), jnp.where(lo, 0.0, second)],
                           axis=0).astype(BF16)


def _fold_lanes(x, op):
    acc = x[:, 0:LANES]
    for j in range(1, x.shape[1] // LANES):
        acc = op(acc, x[:, j * LANES:(j + 1) * LANES])
    return acc


def _attend(q_block, keys, vals, bias, sink_row, o_ref):
    nk = keys.shape[0]
    keys_sw = pltpu.roll(keys, HEAD_DIM, 1)
    vals_sw = pltpu.roll(vals, HEAD_DIM, 1)
    lo = lax.broadcasted_iota(jnp.int32, (BLOCK_Q, LANES), 1) < HEAD_DIM
    pairs_per_kv = ATT_GROUP // 2
    for kv in range(ATT_KV_HEADS):
        kab = _pair_layout(keys, keys_sw, kv)
        vab = _pair_layout(vals, vals_sw, kv)
        for g in range(kv * pairs_per_kv, (kv + 1) * pairs_per_kv):
            s = lax.dot_general(q_block(g), kab, (((1,), (1,)), ((), ())),
                                preferred_element_type=F32)
            halves = []
            recips = []
            for half in range(2):
                sh = s[:, half * nk:(half + 1) * nk]
                if bias is not None:
                    sh = sh + bias
                sk = sink_row[:, 2 * g + half:2 * g + half + 1]
                m = jnp.maximum(jnp.max(_fold_lanes(sh, jnp.maximum), axis=-1, keepdims=True), sk)
                p = jnp.exp2(sh - m)
                recips.append(1.0 / (jnp.sum(_fold_lanes(p, jnp.add), axis=-1, keepdims=True)
                                     + jnp.exp2(sk - m)))
                halves.append(p.astype(BF16))
            o = jnp.dot(jnp.concatenate(halves, axis=1), vab, preferred_element_type=F32)
            o_ref[:, g * LANES:(g + 1) * LANES] = (o * jnp.where(lo, recips[0], recips[1])
                                                   ).astype(o_ref.dtype)


def _attn_kernel(q_ref, kp_ref, k0_ref, kn_ref, vp_ref, v0_ref, vn_ref, kc_ref, vc_ref,
                 cq_ref, sq_ref, cp_ref, sp_ref, cn_ref, sn_ref, sink_ref, o_ref, *, n_ctx, n_lat):
    i = pl.program_id(1)
    qscale = HEAD_DIM ** -0.5 * LOG2E
    sink_row = sink_ref[...] * LOG2E

    @pl.when(i < n_ctx)
    def _():
        def q_block(g):
            return (q_ref[:, g * LANES:(g + 1) * LANES] * qscale).astype(BF16)
        _attend(q_block, kc_ref[...], vc_ref[...], None, sink_row, o_ref)

    @pl.when(i >= n_ctx)
    def _():
        n = i - n_ctx
        cq, sq = cq_ref[...], sq_ref[...]

        def q_block(g):
            return (_rope(q_ref[:, g * LANES:(g + 1) * LANES], cq, sq) * qscale).astype(BF16)

        kp = _rope(kp_ref[...], cp_ref[...], sp_ref[...])
        k0 = _rope(k0_ref[...], cq, sq)
        kn = _rope(kn_ref[...], cn_ref[...], sn_ref[...])
        keys = jnp.concatenate([kp, k0, kn, kc_ref[...]], axis=0)
        vals = jnp.concatenate([vp_ref[...], v0_ref[...], vn_ref[...], vc_ref[...]], axis=0)
        nk = keys.shape[0]
        qpos = n * BLOCK_Q + lax.broadcasted_iota(jnp.int32, (BLOCK_Q, nk), 0)
        col = lax.broadcasted_iota(jnp.int32, (BLOCK_Q, nk), 1)
        kpos = (n - 1) * BLOCK_Q + col
        local = (jnp.abs(qpos - kpos) <= WINDOW) & (kpos >= 0) & (kpos < n_lat * BLOCK_Q)
        bias = jnp.where(local | (col >= 3 * BLOCK_Q), 0.0, NEG)
        _attend(q_block, keys, vals, bias, sink_row, o_ref)


def _attention(z, cos_t, sin_t, sink, tc):
    b, ttot, _ = z.shape
    n_all, n_ctx = ttot // BLOCK_Q, tc // BLOCK_Q
    n_lat = n_all - n_ctx
    kcol = (2 * 512 + ATT_HEADS * HEAD_DIM) // LANES
    vcol = kcol + 1
    qcol = 2 * 512 // (ATT_HEADS * HEAD_DIM)

    def prev(i):
        return jnp.maximum(i - 1, 0)

    def nxt(i):
        return jnp.minimum(i + 1, n_all - 1)

    def rows(col, f):
        return pl.BlockSpec((None, BLOCK_Q, LANES), lambda bi, i: (bi, f(i), col))

    def tab(f):
        return pl.BlockSpec((BLOCK_Q, LANES), lambda bi, i: (f(i), 0))

    same = lambda i: i
    return pl.pallas_call(
        functools.partial(_attn_kernel, n_ctx=n_ctx, n_lat=n_lat),
        grid=(b, n_all),
        in_specs=[pl.BlockSpec((None, BLOCK_Q, ATT_HEADS * HEAD_DIM), lambda bi, i: (bi, i, qcol)),
                  rows(kcol, prev), rows(kcol, same), rows(kcol, nxt),
                  rows(vcol, prev), rows(vcol, same), rows(vcol, nxt),
                  pl.BlockSpec((None, tc, LANES), lambda bi, i: (bi, 0, kcol)),
                  pl.BlockSpec((None, tc, LANES), lambda bi, i: (bi, 0, vcol)),
                  tab(same), tab(same), tab(prev), tab(prev), tab(nxt), tab(nxt),
                  pl.BlockSpec((1, LANES), lambda bi, i: (0, 0))],
        out_specs=pl.BlockSpec((None, BLOCK_Q, ATT_HEADS * HEAD_DIM), lambda bi, i: (bi, i, 0)),
        out_shape=jax.ShapeDtypeStruct((b, ttot, ATT_HEADS * HEAD_DIM), BF16),
        compiler_params=_cparams(("parallel", "parallel")),
        name="window_gqa",
    )(z, z, z, z, z, z, z, z, z, cos_t, sin_t, cos_t, sin_t, cos_t, sin_t, sink)


def _outproj_ab_kernel(x_ref, mb_ref, mc_ref, lf_ref, lb_ref, g_ref, at_ref, w_ref, o_ref, *, tm, tc):
    i = pl.program_id(1)
    lru = (lf_ref[...].astype(F32) + lb_ref[...].astype(F32)) * _gelu(g_ref[...])
    mix = jnp.concatenate([lru.astype(BF16), at_ref[...]], axis=1)
    dx = jnp.dot(mix, w_ref[...], preferred_element_type=F32)
    gate = _gate_mod(mb_ref[...], mc_ref[...], 2, i * tm, tc, tm)
    o_ref[...] = x_ref[...] + gate * dx


def _outproj_ab(xa, mod_l, lru_f, lru_b, z, att, w_out, tc, nb):
    b, ttot, d = xa.shape
    w = lru_f.shape[2]
    tm = _row_tile(ttot)
    tile = lambda width, col: pl.BlockSpec((None, tm, width), lambda bi, i: (bi, i, col))
    return pl.pallas_call(
        functools.partial(_outproj_ab_kernel, tm=tm, tc=tc),
        grid=(b, ttot // tm),
        in_specs=[tile(d, 0),
                  pl.BlockSpec((None, N_MOD, d), lambda bi, i: (bi, 0, 0)),
                  pl.BlockSpec((None, N_MOD, d), lambda bi, i: (nb, 0, 0)),
                  tile(w, 0), tile(w, 0), tile(w, 0), tile(att.shape[2], 0),
                  pl.BlockSpec(w_out.shape, lambda bi, i: (0, 0))],
        out_specs=tile(d, 0),
        out_shape=jax.ShapeDtypeStruct(xa.shape, F32),
        compiler_params=_cparams(("parallel", "parallel")),
        name="outproj_ab",
    )(xa, mod_l, mod_l, lru_f, lru_b, z, att, w_out)


def _outproj_cd_kernel(x_ref, mb_ref, mc_ref, y_ref, of_ref, ob_ref, g_ref, wg_ref, bg_ref, gn_ref,
                       w_ref, o_ref, *, tm, tc):
    i = pl.program_id(1)
    y = _gelu(y_ref[...])
    s5 = y * _sigmoid(jnp.dot(y.astype(BF16), wg_ref[...], preferred_element_type=F32) + bg_ref[...])
    o = of_ref[...].astype(F32) + ob_ref[...].astype(F32)
    parts = []
    for h in range(HG_HEADS):
        oh = o[:, h * HG_DK:(h + 1) * HG_DK]
        ms = jnp.mean(oh * oh, axis=-1, keepdims=True)
        parts.append(oh * lax.rsqrt(ms + EPS) * gn_ref[...])
    hg = jnp.concatenate(parts, axis=1) * _silu(g_ref[...])
    mix = jnp.concatenate([s5.astype(BF16), hg.astype(BF16)], axis=1)
    dx = jnp.dot(mix, w_ref[...], preferred_element_type=F32)
    gate = _gate_mod(mb_ref[...], mc_ref[...], 2, i * tm, tc, tm)
    o_ref[...] = x_ref[...] + gate * dx


def _outproj_cd(xa, mod_l, y5, o_f, o_b, z, w_glu, b_glu, hg_norm, w_out, tc, nb):
    b, ttot, d = xa.shape
    w = y5.shape[2]
    tm = _row_tile(ttot)
    gcol = z.shape[2] // w - 1
    tile = lambda width, col: pl.BlockSpec((None, tm, width), lambda bi, i: (bi, i, col))
    return pl.pallas_call(
        functools.partial(_outproj_cd_kernel, tm=tm, tc=tc),
        grid=(b, ttot // tm),
        in_specs=[tile(d, 0),
                  pl.BlockSpec((None, N_MOD, d), lambda bi, i: (bi, 0, 0)),
                  pl.BlockSpec((None, N_MOD, d), lambda bi, i: (nb, 0, 0)),
                  tile(w, 0), tile(w, 0), tile(w, 0), tile(w, gcol),
                  pl.BlockSpec(w_glu.shape, lambda bi, i: (0, 0)),
                  pl.BlockSpec((1, w), lambda bi, i: (0, 0)),
                  pl.BlockSpec((1, HG_DK), lambda bi, i: (0, 0)),
                  pl.BlockSpec(w_out.shape, lambda bi, i: (0, 0))],
        out_specs=tile(d, 0),
        out_shape=jax.ShapeDtypeStruct(xa.shape, F32),
        compiler_params=_cparams(("parallel", "parallel")),
        name="outproj_cd",
    )(xa, mod_l, mod_l, y5, o_f, o_b, z, w_glu, b_glu.reshape(1, w), hg_norm.reshape(1, HG_DK), w_out)


def _ffn_kernel(x_ref, mb_ref, mc_ref, g_ref, w1_ref, w3_ref, w2_ref, o_ref, h_ref, acc_ref, *, tm, tc):
    i = pl.program_id(1)
    k = pl.program_id(2)

    @pl.when(k == 0)
    def _():
        h = _norm_mod(x_ref[...], g_ref[...], mb_ref[...], mc_ref[...], i * tm, tc, 3, 4)
        h_ref[...] = h.astype(BF16)
        acc_ref[...] = jnp.zeros_like(acc_ref)

    h = h_ref[...]
    a = jnp.dot(h, w1_ref[...], preferred_element_type=F32)
    c = jnp.dot(h, w3_ref[...], preferred_element_type=F32)
    acc_ref[...] += jnp.dot((_silu(a) * c).astype(BF16), w2_ref[...], preferred_element_type=F32)

    @pl.when(k == pl.num_programs(2) - 1)
    def _():
        gate = _gate_mod(mb_ref[...], mc_ref[...], 5, i * tm, tc, tm)
        o_ref[...] = x_ref[...] + gate * acc_ref[...]


def _ffn(xa, mod_l, g, w1, w3, w2, tc, nb):
    b, ttot, d = xa.shape
    f = w1.shape[1]
    tm = _row_tile(ttot)
    tf = f // 2 if (f // 2) % LANES == 0 else f
    return pl.pallas_call(
        functools.partial(_ffn_kernel, tm=tm, tc=tc),
        grid=(b, ttot // tm, f // tf),
        in_specs=[pl.BlockSpec((None, tm, d), lambda bi, i, k: (bi, i, 0)),
                  pl.BlockSpec((None, N_MOD, d), lambda bi, i, k: (bi, 0, 0)),
                  pl.BlockSpec((None, N_MOD, d), lambda bi, i, k: (nb, 0, 0)),
                  pl.BlockSpec((1, d), lambda bi, i, k: (0, 0)),
                  pl.BlockSpec((d, tf), lambda bi, i, k: (0, k)),
                  pl.BlockSpec((d, tf), lambda bi, i, k: (0, k)),
                  pl.BlockSpec((tf, d), lambda bi, i, k: (k, 0))],
        out_specs=pl.BlockSpec((None, tm, d), lambda bi, i, k: (bi, i, 0)),
        out_shape=jax.ShapeDtypeStruct(xa.shape, F32),
        scratch_shapes=[pltpu.VMEM((tm, d), BF16), pltpu.VMEM((tm, d), F32)],
        compiler_params=_cparams(("parallel", "parallel", "arbitrary")),
        name="ffn_swiglu",
    )(xa, mod_l, mod_l, g.reshape(1, d), w1, w3, w2)


def _router_kernel(x_ref, mb_ref, mc_ref, g_ref, r_ref, h_ref, wt_ref, cnt_ref, *, tm, tc, n_exp):
    i = pl.program_id(1)
    h = _norm_mod(x_ref[...], g_ref[...], mb_ref[...], mc_ref[...], i * tm, tc, 3, 4)
    h_ref[...] = h.astype(BF16)
    logits = lax.dot_general(r_ref[...], h, (((1,), (1,)), ((), ())), preferred_element_type=F32,
                             precision=lax.Precision.HIGHEST)
    sub = lax.broadcasted_iota(jnp.int32, logits.shape, 0)
    logits = jnp.where(sub < n_exp, logits, NEG)
    m1 = jnp.max(logits, axis=0, keepdims=True)
    i1 = jnp.min(jnp.where(logits == m1, sub, SUBLANES), axis=0, keepdims=True)
    rest = jnp.where(sub == i1, NEG, logits)
    m2 = jnp.max(rest, axis=0, keepdims=True)
    i2 = jnp.min(jnp.where(rest == m2, sub, SUBLANES), axis=0, keepdims=True)
    e2 = jnp.exp(m2 - m1)
    g1 = 1.0 / (1.0 + e2)
    wt = jnp.where(sub == i1, g1, 0.0) + jnp.where(sub == i2, e2 * g1, 0.0)
    wt_ref[...] = wt
    cnt = jnp.sum(jnp.where(wt > 0.0, 1.0, 0.0), axis=1, keepdims=True)
    cnt_ref[...] = jnp.broadcast_to(cnt, cnt_ref.shape).astype(jnp.int32)


def _router(xa, mod_l, g, router_t, tc, nb, n_exp):
    b, ttot, d = xa.shape
    tm = _row_tile(ttot)
    n_t = ttot // tm
    return pl.pallas_call(
        functools.partial(_router_kernel, tm=tm, tc=tc, n_exp=n_exp),
        grid=(b, n_t),
        in_specs=[pl.BlockSpec((None, tm, d), lambda bi, i: (bi, i, 0)),
                  pl.BlockSpec((None, N_MOD, d), lambda bi, i: (bi, 0, 0)),
                  pl.BlockSpec((None, N_MOD, d), lambda bi, i: (nb, 0, 0)),
                  pl.BlockSpec((1, d), lambda bi, i: (0, 0)),
                  pl.BlockSpec((SUBLANES, d), lambda bi, i: (0, 0))],
        out_specs=[pl.BlockSpec((tm, d), lambda bi, i: (bi * n_t + i, 0)),
                   pl.BlockSpec((SUBLANES, tm), lambda bi, i: (0, bi * n_t + i)),
                   pl.BlockSpec((None, SUBLANES, LANES), lambda bi, i: (bi * n_t + i, 0, 0))],
        out_shape=[jax.ShapeDtypeStruct((b * ttot, d), BF16),
                   jax.ShapeDtypeStruct((SUBLANES, b * ttot), F32),
                   jax.ShapeDtypeStruct((b * n_t, SUBLANES, LANES), jnp.int32)],
        compiler_params=_cparams(("parallel", "parallel")),
        name="moe_router",
    )(xa, mod_l, mod_l, g.reshape(1, d), router_t)


def _moe_kernel(cnt_ref, x_ref, *refs, tm, tc, n_exp, n_t, group):
    mb_refs = refs[:group]
    mc_ref, h_ref, wt_ref, w1_ref, w3_ref, w2_ref, o_ref, rank_ref = refs[group:]
    p = pl.program_id(0)
    e = pl.program_id(1)
    f = w1_ref.shape[1]
    half = (f // LANES + 1) // 2 * LANES
    f_split = [(0, half), (half, f)] if 0 < half < f else [(0, f)]

    @pl.when(e == 0)
    def _():
        o_ref[...] = jnp.zeros_like(o_ref)
        r_i = lax.broadcasted_iota(jnp.int32, (tm, tm), 0)
        c_i = lax.broadcasted_iota(jnp.int32, (tm, tm), 1)
        before = jnp.where(r_i < c_i, 1.0, 0.0).astype(BF16)
        for j in range(group):
            sel = wt_ref[:, j * tm:(j + 1) * tm] > 0.0
            rank = jnp.dot(jnp.where(sel, 1.0, 0.0).astype(BF16), before, preferred_element_type=F32)
            rank_ref[j] = jnp.where(sel, rank, -1.0)

    counts = [cnt_ref[(p * group + j) * n_exp + e] for j in range(group)]

    offsets = [jnp.int32(0)]
    for j in range(group - 1):
        offsets.append(offsets[-1] + counts[j])
    total = offsets[-1] + counts[-1]

    def expert_pass(m_rows, base):
        slot = (lax.broadcasted_iota(jnp.int32, (m_rows, tm), 0) + base).astype(F32)
        onehots = []
        xe = None
        gate = None
        for j in range(group):
            r_row = rank_ref[j, pl.ds(e, 1), :]
            r_row = jnp.where(r_row >= 0.0, r_row + offsets[j].astype(F32), -1.0)
            w_row = wt_ref[pl.ds(e, 1), j * tm:(j + 1) * tm]
            hit = r_row == slot
            onehot = jnp.where(hit, 1.0, 0.0).astype(BF16)
            onehots.append(onehot)
            xj = jnp.dot(onehot, h_ref[j * tm:(j + 1) * tm, :], preferred_element_type=F32)
            gj = jnp.sum(jnp.where(hit, w_row, 0.0), axis=1, keepdims=True)
            xe = xj if xe is None else xe + xj
            gate = gj if gate is None else gate + gj
        xe = xe.astype(BF16)
        y = None
        for f0, f1 in f_split:
            a = jnp.dot(xe, w1_ref[:, f0:f1], preferred_element_type=F32)
            c = jnp.dot(xe, w3_ref[:, f0:f1], preferred_element_type=F32)
            part = jnp.dot((_silu(a) * c * gate).astype(BF16), w2_ref[f0:f1, :],
                           preferred_element_type=F32)
            y = part if y is None else y + part
        y = y.astype(BF16)
        for j in range(group):
            o_ref[j * tm:(j + 1) * tm, :] += lax.dot_general(
                onehots[j], y, (((0,), (0,)), ((), ())), preferred_element_type=F32)

    n_full = total // MOE_PASS_ROWS

    def full_pass(it, carry):
        expert_pass(MOE_PASS_ROWS, it * MOE_PASS_ROWS)
        return carry

    lax.fori_loop(0, n_full, full_pass, 0)
    rest = total - n_full * MOE_PASS_ROWS

    @pl.when(rest > MOE_PASS_ROWS // 2)
    def _():
        expert_pass(MOE_PASS_ROWS, n_full * MOE_PASS_ROWS)

    @pl.when(jnp.logical_and(rest > 0, rest <= MOE_PASS_ROWS // 2))
    def _():
        expert_pass(MOE_PASS_ROWS // 2, n_full * MOE_PASS_ROWS)

    @pl.when(e == n_exp - 1)
    def _():
        for j in range(group):
            i = (p * group + j) % n_t
            gate5 = _gate_mod(mb_refs[j][...], mc_ref[...], 5, i * tm, tc, tm)
            rows = slice(j * tm, (j + 1) * tm)
            o_ref[rows, :] = x_ref[rows, :] + gate5 * o_ref[rows, :]


def _moe(xa, mod_l, g, router_t, w1, w3, w2, tc, nb):
    b, ttot, d = xa.shape
    n_exp, _, f = w1.shape
    tm = _row_tile(ttot)
    n_t = ttot // tm
    group = 2 if (b * n_t) % 2 == 0 else 1
    h, wt, cnt = _router(xa, mod_l, g, router_t, tc, nb, n_exp)
    cnt = cnt[:, :n_exp, 0].reshape(-1)
    rows = group * tm
    mods = [pl.BlockSpec((None, N_MOD, d), functools.partial(
        lambda p, e, c, j: ((p * group + j) // n_t, 0, 0), j=j)) for j in range(group)]
    grid_spec = pltpu.PrefetchScalarGridSpec(
        num_scalar_prefetch=1,
        grid=(b * n_t // group, n_exp),
        in_specs=[pl.BlockSpec((rows, d), lambda p, e, c: (p, 0), pipeline_mode=pl.Buffered(1))]
        + mods
        + [pl.BlockSpec((None, N_MOD, d), lambda p, e, c: (nb, 0, 0)),
           pl.BlockSpec((rows, d), lambda p, e, c: (p, 0)),
           pl.BlockSpec((SUBLANES, rows), lambda p, e, c: (0, p)),
           pl.BlockSpec((None, d, f), lambda p, e, c: (e, 0, 0)),
           pl.BlockSpec((None, d, f), lambda p, e, c: (e, 0, 0)),
           pl.BlockSpec((None, f, d), lambda p, e, c: (e, 0, 0))],
        out_specs=pl.BlockSpec((rows, d), lambda p, e, c: (p, 0)),
        scratch_shapes=[pltpu.VMEM((group, SUBLANES, tm), F32)])
    out = pl.pallas_call(
        functools.partial(_moe_kernel, tm=tm, tc=tc, n_exp=n_exp, n_t=n_t, group=group),
        grid_spec=grid_spec,
        out_shape=jax.ShapeDtypeStruct((b * ttot, d), F32),
        compiler_params=_cparams(("parallel", "arbitrary")),
        name="moe_experts",
    )(cnt, xa.reshape(b * ttot, d), *([mod_l] * group), mod_l, h, wt, w1, w3, w2)
    return out.reshape(b, ttot, d)


def _spread_groups(x, inner, row_group, n_grp):
    cols_in = x.shape[2]
    cols_out = cols_in * n_grp
    q = jnp.arange(cols_out)
    src = (q // (n_grp * inner)) * inner + q % inner
    tile = (jnp.arange(cols_in)[:, None] == src[None, :]).astype(BF16)
    col_group = (q // inner) % n_grp
    y = jnp.dot(x.astype(BF16), tile)
    return jnp.where(row_group[:, None] == col_group[None, :], y, jnp.zeros((), BF16))


def _s5_weights(a_re, a_im, log_step, b_re, b_im, c_re, c_im):
    L = S5_CHUNK
    lr = jnp.minimum(a_re, -1e-4)
    li = a_im
    dt = jnp.exp(log_step)[..., None]
    mag, ang = lr * dt, li * dt
    lbr, lbi = jnp.exp(mag) * jnp.cos(ang), jnp.exp(mag) * jnp.sin(ang)
    zr, zi = lbr - 1.0, lbi
    den = lr * lr + li * li
    fr = (zr * lr + zi * li) / den
    fi = (zi * lr - zr * li) / den
    bbr = fr[..., None] * b_re - fi[..., None] * b_im
    bbi = fr[..., None] * b_im + fi[..., None] * b_re

    def power(p):
        p = p[..., None, None, None].astype(F32)
        return jnp.exp(mag * p) * jnp.cos(ang * p), jnp.exp(mag * p) * jnp.sin(ang * p)

    n_dir, n_grp, n_st = a_re.shape
    gpb = LANES // S5_GROUP
    n_blk = n_grp // gpb
    s = jnp.arange(L)
    in_group = (jnp.arange(L * LANES) % LANES) // S5_GROUP
    st_group = jnp.arange(gpb * n_st) // n_st
    outs = []
    for d in range(n_dir):
        pr, pi = power(jnp.arange(L))
        pr, pi = pr[:, d], pi[:, d]
        cbr = (jnp.einsum('gcn,tgn,gnk->tgck', c_re[d], pr, bbr[d])
               - jnp.einsum('gcn,tgn,gnk->tgck', c_re[d], pi, bbi[d])
               - jnp.einsum('gcn,tgn,gnk->tgck', c_im[d], pr, bbi[d])
               - jnp.einsum('gcn,tgn,gnk->tgck', c_im[d], pi, bbr[d]))
        lag = (s[None, :] - s[:, None]) if d == 0 else (s[:, None] - s[None, :])
        kern = jnp.where((lag >= 0)[..., None, None, None], cbr[jnp.clip(lag, 0, L - 1)], 0.0)
        kern = kern.reshape(L, L, n_blk, gpb, S5_GROUP, S5_GROUP).transpose(2, 0, 3, 5, 1, 4)
        m = _spread_groups(kern.reshape(n_blk, L * LANES, L * S5_GROUP), S5_GROUP, in_group, gpb)
        qr, qi = power((L - 1 - s) if d == 0 else s)
        qr, qi = qr[:, d], qi[:, d]
        str_ = qr[..., None] * bbr[d] - qi[..., None] * bbi[d]
        sti = qr[..., None] * bbi[d] + qi[..., None] * bbr[d]

        def to_state(x):
            x = x.reshape(L, n_blk, gpb, n_st, S5_GROUP).transpose(1, 0, 2, 4, 3)
            return _spread_groups(x.reshape(n_blk, L * LANES, n_st), n_st, in_group, gpb)

        wst = jnp.concatenate([to_state(str_), to_state(sti)], axis=2)
        rr, ri = power((s + 1) if d == 0 else (L - s))
        rr, ri = rr[:, d], ri[:, d]
        wr = c_re[d][None] * rr[:, :, None, :] - c_im[d][None] * ri[:, :, None, :]
        wi = -(c_re[d][None] * ri[:, :, None, :] + c_im[d][None] * rr[:, :, None, :])

        def from_state(x):
            x = x.reshape(L, n_blk, gpb, S5_GROUP, n_st).transpose(1, 2, 4, 0, 3)
            return _spread_groups(x.reshape(n_blk, gpb * n_st, L * S5_GROUP), S5_GROUP, st_group, gpb)

        wout = jnp.concatenate([from_state(wr), from_state(wi)], axis=1)
        ler, lei = jnp.exp(mag[d] * L) * jnp.cos(ang[d] * L), jnp.exp(mag[d] * L) * jnp.sin(ang[d] * L)
        lam_l = jnp.concatenate([ler.reshape(n_blk, 1, gpb * n_st), lei.reshape(n_blk, 1, gpb * n_st)],
                                axis=2)
        outs.append((m, wst, wout, lam_l))
    return tuple(jnp.stack([o[k] for o in outs]) for k in range(4))


def _s5_kernel(u_ref, m_ref, wst_ref, wout_ref, lam_ref, d_ref, o_ref, x_ref, hp_ref, *, rows, rows_ctx):
    L = S5_CHUNK
    dr = pl.program_id(2)
    ns = lam_ref.shape[1] // 2
    u = jnp.concatenate([u_ref[pl.ds(s, rows, stride=L), :] for s in range(L)], axis=1)
    ub = u.astype(BF16)
    x_ref[...] = jnp.dot(ub, wst_ref[...], preferred_element_type=F32)
    lr = lam_ref[:, 0:ns]
    li = lam_ref[:, ns:2 * ns]

    def visit(r, carry):
        hr, hi = carry
        hp_ref[pl.ds(r, 1), 0:ns] = hr
        hp_ref[pl.ds(r, 1), ns:2 * ns] = hi
        xr = x_ref[pl.ds(r, 1), 0:ns]
        xi = x_ref[pl.ds(r, 1), ns:2 * ns]
        return lr * hr - li * hi + xr, lr * hi + li * hr + xi

    zero = (jnp.zeros((1, ns), F32), jnp.zeros((1, ns), F32))

    @pl.when(dr == 0)
    def _():
        lax.fori_loop(0, rows, visit, zero, unroll=4)

    @pl.when(dr == 1)
    def _():
        c = lax.fori_loop(0, rows_ctx, lambda t, c: visit(rows_ctx - 1 - t, c), zero, unroll=4)
        lax.fori_loop(0, rows - rows_ctx, lambda t, c: visit(rows - 1 - t, c), c, unroll=4)

    y = (jnp.dot(ub, m_ref[...], preferred_element_type=F32)
         + jnp.dot(hp_ref[...].astype(BF16), wout_ref[...], preferred_element_type=F32))

    @pl.when(dr == 0)
    def _():
        for s in range(L):
            o_ref[pl.ds(s, rows, stride=L), :] = (y[:, s * LANES:(s + 1) * LANES]
                                                  + d_ref[...] * u[:, s * LANES:(s + 1) * LANES])

    @pl.when(dr == 1)
    def _():
        for s in range(L):
            o_ref[pl.ds(s, rows, stride=L), :] += y[:, s * LANES:(s + 1) * LANES]


def _s5(z, weights, dvec, tc, width):
    b, ttot, _ = z.shape
    m, wst, wout, lam_l = weights
    n_dir, n_blk = m.shape[0], m.shape[1]
    rows, rows_ctx = ttot // S5_CHUNK, tc // S5_CHUNK
    wspec = lambda a: pl.BlockSpec((None, None) + a.shape[2:], lambda bi, j, dr: (dr, j, 0, 0))
    return pl.pallas_call(
        functools.partial(_s5_kernel, rows=rows, rows_ctx=rows_ctx),
        grid=(b, n_blk, n_dir),
        in_specs=[pl.BlockSpec((None, ttot, LANES), lambda bi, j, dr: (bi, 0, j)),
                  wspec(m), wspec(wst), wspec(wout), wspec(lam_l),
                  pl.BlockSpec((1, LANES), lambda bi, j, dr: (0, j))],
        out_specs=pl.BlockSpec((None, ttot, LANES), lambda bi, j, dr: (bi, 0, j)),
        out_shape=jax.ShapeDtypeStruct((b, ttot, width), F32),
        scratch_shapes=[pltpu.VMEM((rows, lam_l.shape[3]), F32), pltpu.VMEM((rows, lam_l.shape[3]), F32)],
        compiler_params=_cparams(("parallel", "parallel", "arbitrary")),
        name="s5_bidir",
    )(z, m, wst, wout, lam_l, dvec.reshape(1, width))


def _gla_kernel(qf_ref, ff_ref, vf_ref, qb_ref, fb_ref, vb_ref, lb_ref, of_ref, ob_ref,
                stf_ref, stb_ref, *, lc):
    step = pl.program_id(1)

    @pl.when(step == 0)
    def _():
        stf_ref[...] = jnp.zeros_like(stf_ref)
        stb_ref[...] = jnp.zeros_like(stb_ref)

    _gla_block(qf_ref, ff_ref, vf_ref, lb_ref, of_ref, stf_ref, lc=lc, reverse=False)
    _gla_block(qb_ref, fb_ref, vb_ref, lb_ref, ob_ref, stb_ref, lc=lc, reverse=True)


def _gla_block(q_ref, f_ref, v_ref, lb_ref, o_ref, st_ref, *, lc, reverse):
    c = HG_CHUNK
    n_chunks = lc // c
    r_i = lax.broadcasted_iota(jnp.int32, (lc, lc), 0)
    c_i = lax.broadcasted_iota(jnp.int32, (lc, lc), 1)
    same_chunk = (r_i // c) == (c_i // c)
    keep = same_chunk & ((c_i >= r_i) if reverse else (c_i <= r_i))
    tri = jnp.where(keep, 1.0, 0.0).astype(BF16)
    lb = lb_ref[...]
    q = _silu(q_ref[...])
    f = lb + (1.0 - lb) / (1.0 + jnp.exp(-f_ref[...]))
    k = 1.0 - f
    logf = jnp.log(f)
    v = v_ref[...].astype(BF16)
    hi = logf.astype(BF16)
    r1 = logf - hi.astype(F32)
    mid = r1.astype(BF16)
    lo = (r1 - mid.astype(F32)).astype(BF16)
    cum = (jnp.dot(tri, hi, preferred_element_type=F32) + jnp.dot(tri, mid, preferred_element_type=F32)
           + jnp.dot(tri, lo, preferred_element_type=F32))
    totals = [cum[ci * c:ci * c + 1, :] if reverse else cum[(ci + 1) * c - 1:(ci + 1) * c, :]
              for ci in range(n_chunks)]
    w = cum.shape[1]
    total = jnp.concatenate([jnp.broadcast_to(t, (c, w)) for t in totals], axis=0)
    centre = jnp.concatenate([jnp.broadcast_to(cum[ci * c + c // 2:ci * c + c // 2 + 1, :], (c, w))
                              for ci in range(n_chunks)], axis=0)
    q_in = (q * jnp.exp(cum)).astype(BF16)
    k_out = (k * jnp.exp(total - cum)).astype(BF16)
    q_loc = (q * jnp.exp(cum - centre)).astype(BF16)
    k_loc = (k * jnp.exp(centre - cum)).astype(BF16)
    decays = [jnp.exp(t) for t in totals]
    order = range(n_chunks - 1, -1, -1) if reverse else range(n_chunks)
    for h in range(HG_HEADS):
        sl = slice(h * HG_DK, (h + 1) * HG_DK)
        att = lax.dot_general(q_loc[:, sl], k_loc[:, sl], (((1,), (1,)), ((), ())),
                              preferred_element_type=F32)
        att = jnp.where(keep, att, 0.0)
        o_loc = jnp.dot(att.astype(BF16), v[:, sl], preferred_element_type=F32)
        incs = [lax.dot_general(v[ci * c:(ci + 1) * c, sl], k_out[ci * c:(ci + 1) * c, sl],
                                (((0,), (0,)), ((), ())), preferred_element_type=F32)
                for ci in range(n_chunks)]
        st = st_ref[h]
        for ci in order:
            rows = slice(ci * c, (ci + 1) * c)
            o = o_loc[rows] + lax.dot_general(q_in[rows, sl], st.astype(BF16),
                                              (((1,), (1,)), ((), ())), preferred_element_type=F32)
            o_ref[rows, sl] = o.astype(o_ref.dtype)
            st = st * decays[ci][:, sl] + incs[ci]
        st_ref[h] = st


def _gla(z, lb, tc):
    b, ttot, _ = z.shape
    w = HG_HEADS * HG_DK
    lc = SEQ_BLOCK
    n_all, n_ctx = ttot // lc, tc // lc

    def col(cidx, reverse):
        return pl.BlockSpec((None, lc, w),
                            lambda bi, s: (bi, _seq_block(s, n_ctx, n_all, reverse), cidx))

    out = jax.ShapeDtypeStruct((b, ttot, w), BF16)
    state = pltpu.VMEM((HG_HEADS, HG_DK, HG_DK), F32)
    return pl.pallas_call(
        functools.partial(_gla_kernel, lc=lc),
        grid=(b, n_all),
        in_specs=[col(1, False), col(2, False), col(4, False),
                  col(1, True), col(3, True), col(4, True),
                  pl.BlockSpec((1, w), lambda bi, s: (0, 0))],
        out_specs=[col(0, False), col(0, True)],
        out_shape=[out, out],
        scratch_shapes=[state, state],
        compiler_params=_cparams(("parallel", "arbitrary")),
        name="hgrn2_bidir",
    )(z, z, z, z, z, z, lb.reshape(1, w))


def _final_kernel(x_ref, g_ref, o_ref):
    x = x_ref[...]
    ms = jnp.mean(x * x, axis=-1, keepdims=True)
    o_ref[...] = x * lax.rsqrt(ms + EPS) * g_ref[...]


def _final_norm(xa, g, tc):
    b, ttot, d = xa.shape
    t = ttot - tc
    tm = SEQ_BLOCK
    off = tc // tm
    return pl.pallas_call(
        _final_kernel,
        grid=(b, t // tm),
        in_specs=[pl.BlockSpec((None, tm, d), lambda bi, i: (bi, i + off, 0)),
                  pl.BlockSpec((1, d), lambda bi, i: (0, 0))],
        out_specs=pl.BlockSpec((None, tm, d), lambda bi, i: (bi, i, 0)),
        out_shape=jax.ShapeDtypeStruct((b, t, d), F32),
        compiler_params=_cparams(("parallel", "parallel")),
        name="final_norm",
    )(xa, g.reshape(1, d))


def _rope_tables(t, tc):
    pos = jnp.arange(t)
    row = (pos // GRID_W).astype(F32)
    col = (pos % GRID_W).astype(F32)
    inv = ROPE_BASE ** (-jnp.arange(ROPE_FREQS, dtype=F32) / ROPE_FREQS)
    ar, ac = row[:, None] * inv, col[:, None] * inv
    cos = jnp.concatenate([jnp.cos(ar), jnp.cos(ar), jnp.cos(ac), jnp.cos(ac)], axis=1)
    sin = jnp.concatenate([-jnp.sin(ar), jnp.sin(ar), -jnp.sin(ac), jnp.sin(ac)], axis=1)
    cos = jnp.concatenate([jnp.ones((tc, HEAD_DIM), F32), cos], axis=0)
    sin = jnp.concatenate([jnp.zeros((tc, HEAD_DIM), F32), sin], axis=0)
    return jnp.tile(cos, (1, LANES // HEAD_DIM)), jnp.tile(sin, (1, LANES // HEAD_DIM))


def _block_diag_dense(w):
    nblk, h, k = w.shape
    return jnp.einsum('nhk,nm->nhmk', w, jnp.eye(nblk, dtype=w.dtype)).reshape(nblk * h, nblk * k)


def kernel(x, c, ctx, c_ctx, w_mod, b_mod, norm_mix, norm_ffn, final_norm, w_in_ab, lru_conv_w, lru_conv_b, lru_wa, lru_ba, lru_wx, lru_bx, lru_lam, attn_sink, w_out_ab, ffn_w1, ffn_w3, ffn_w2, w_in_cd, s5_a_re, s5_a_im, s5_log_step, s5_b_re, s5_b_im, s5_c_re, s5_c_im, s5_d, s5_w_glu, s5_b_glu, hg_lb_raw, hg_norm, w_out_cd, moe_router, moe_w1, moe_w3, moe_w2):
    nb, t, d = x.shape
    tc = ctx.shape[1]
    depth = w_mod.shape[0]
    assert tc % SEQ_BLOCK == 0 and t % SEQ_BLOCK == 0 and t % GRID_W == 0

    xa = jnp.concatenate([ctx, x], axis=1)
    mod_rows = -(-(nb + 1) // SUBLANES) * SUBLANES
    cvec = jnp.zeros((mod_rows, d), F32).at[:nb].set(c).at[nb].set(c_ctx)
    mod = _modulation(cvec, w_mod, b_mod).reshape(depth, mod_rows, N_MOD, d)

    cos_t, sin_t = _rope_tables(t, tc)
    lb_soft = jax.nn.softmax(hg_lb_raw.astype(F32), axis=0)
    lb_table = jnp.cumsum(lb_soft, axis=0) - lb_soft[0:1]
    n_exp = moe_router.shape[2]

    for l in range(depth):
        j = l // 2
        mod_l = mod[l]
        if l % 2 == 0:
            z = _inproj(xa, mod_l, norm_mix[l], w_in_ab[j].astype(BF16), tc, nb)
            sp = jax.nn.softplus(-lru_lam[j])
            lru = []
            for dr, rev in enumerate((False, True)):
                lru.append(_lru(z, lru_conv_w[j], lru_conv_b[j],
                                _block_diag_dense(lru_wa[j, dr]).astype(BF16), lru_ba[j, dr],
                                _block_diag_dense(lru_wx[j, dr]).astype(BF16), lru_bx[j, dr],
                                sp[dr], tc, rev))
            sink = jnp.zeros((1, LANES), F32).at[0, :ATT_HEADS].set(attn_sink[j])
            att = _attention(z, cos_t, sin_t, sink, tc)
            xa = _outproj_ab(xa, mod_l, lru[0], lru[1], z, att, w_out_ab[j].astype(BF16), tc, nb)
            xa = _ffn(xa, mod_l, norm_ffn[l], ffn_w1[j].astype(BF16), ffn_w3[j].astype(BF16),
                      ffn_w2[j].astype(BF16), tc, nb)
        else:
            z = _inproj(xa, mod_l, norm_mix[l], w_in_cd[j].astype(BF16), tc, nb)
            s5w = _s5_weights(s5_a_re[j], s5_a_im[j], s5_log_step[j], s5_b_re[j], s5_b_im[j],
                              s5_c_re[j], s5_c_im[j])
            y5 = _s5(z, s5w, s5_d[j], tc, s5_d.shape[1])
            o_f, o_b = _gla(z, lb_table[j], tc)
            xa = _outproj_cd(xa, mod_l, y5, o_f, o_b, z, s5_w_glu[j].astype(BF16), s5_b_glu[j],
                             hg_norm[j], w_out_cd[j].astype(BF16), tc, nb)
            router_t = jnp.zeros((SUBLANES, d), F32).at[:n_exp].set(moe_router[j].T)
            xa = _moe(xa, mod_l, norm_ffn[l], router_t, moe_w1[j].astype(BF16),
                      moe_w3[j].astype(BF16), moe_w2[j].astype(BF16), tc, nb)
    return _final_norm(xa, final_norm, tc)
```

```python
import functools
import math

import jax
import jax.numpy as jnp
from jax import lax
from jax.experimental import pallas as pl
from jax.experimental.pallas import tpu as pltpu

F32 = jnp.float32
BF16 = jnp.bfloat16

EPS = 1e-6
GRID_W = 64
LRU_BLOCKS = 8
LRU_C = 8.0
CONV_W = 4
ATT_HEADS = 8
ATT_KV_HEADS = 2
ATT_GROUP = ATT_HEADS // ATT_KV_HEADS
HEAD_DIM = 64
WINDOW = 128
BLOCK_Q = 128
ROPE_FREQS = HEAD_DIM // 4
ROPE_BASE = 10000.0
S5_GROUP = 16
S5_STATE = 64
S5_CHUNK = 8
HG_HEADS = 4
HG_DK = 128
HG_CHUNK = 64
SEQ_BLOCK = 256
N_MOD = 6
MOE_PASS_ROWS = 256
LANES = 128
SUBLANES = 8
VMEM_LIMIT = 56 * 1024 * 1024
NEG = -1e30
LOG2E = math.log2(math.e)


def _cparams(sem):
    return pltpu.CompilerParams(dimension_semantics=sem, vmem_limit_bytes=VMEM_LIMIT)


def _row_tile(ttot):
    for tm in (768, 1024, 512, 256):
        if ttot % tm == 0:
            return tm
    raise ValueError(f"unsupported token count {ttot}")


def _sigmoid(x):
    return 0.5 * jnp.tanh(0.5 * x) + 0.5


def _silu(x):
    return x * _sigmoid(x)


def _gelu(x):
    return 0.5 * x * (1.0 + jnp.tanh(math.sqrt(2.0 / math.pi) * (x + 0.044715 * (x * x * x))))


def _norm_mod(x, g, mb, mc, row0, tc, shift_idx, scale_idx):
    ms = jnp.mean(x * x, axis=-1, keepdims=True)
    y = x * lax.rsqrt(ms + EPS) * g
    rows = row0 + lax.broadcasted_iota(jnp.int32, (x.shape[0], 1), 0)
    is_ctx = rows < tc
    scale = jnp.where(is_ctx, mc[scale_idx:scale_idx + 1], mb[scale_idx:scale_idx + 1])
    shift = jnp.where(is_ctx, mc[shift_idx:shift_idx + 1], mb[shift_idx:shift_idx + 1])
    return y * (1.0 + scale) + shift


def _gate_mod(mb, mc, idx, row0, tc, n):
    rows = row0 + lax.broadcasted_iota(jnp.int32, (n, 1), 0)
    return jnp.where(rows < tc, mc[idx:idx + 1], mb[idx:idx + 1])


def _mod_kernel(c_ref, w_ref, b_ref, o_ref):
    s = _silu(c_ref[...])
    o_ref[...] = jnp.dot(s, w_ref[...], preferred_element_type=F32,
                         precision=lax.Precision.HIGHEST) + b_ref[...]


def _modulation(cvec, w_mod, b_mod):
    depth, d, n = w_mod.shape
    tn = 1536 if n % 1536 == 0 else n
    rows = cvec.shape[0]
    return pl.pallas_call(
        _mod_kernel,
        grid=(depth, n // tn),
        in_specs=[pl.BlockSpec((rows, d), lambda l, j: (0, 0)),
                  pl.BlockSpec((None, d, tn), lambda l, j: (l, 0, j)),
                  pl.BlockSpec((None, 1, tn), lambda l, j: (l, 0, j))],
        out_specs=pl.BlockSpec((None, rows, tn), lambda l, j: (l, 0, j)),
        out_shape=jax.ShapeDtypeStruct((depth, rows, n), F32),
        compiler_params=_cparams(("arbitrary", "arbitrary")),
        name="modulation",
    )(cvec, w_mod, b_mod.reshape(depth, 1, n))


def _inproj_kernel(x_ref, mb_ref, mc_ref, g_ref, w_ref, o_ref, *, tm, tc):
    i = pl.program_id(1)
    h = _norm_mod(x_ref[...], g_ref[...], mb_ref[...], mc_ref[...], i * tm, tc, 0, 1)
    o_ref[...] = jnp.dot(h.astype(BF16), w_ref[...], preferred_element_type=F32)


def _inproj(xa, mod_l, g, w, tc, nb):
    b, ttot, d = xa.shape
    n = w.shape[1]
    tm = _row_tile(ttot)
    return pl.pallas_call(
        functools.partial(_inproj_kernel, tm=tm, tc=tc),
        grid=(b, ttot // tm),
        in_specs=[pl.BlockSpec((None, tm, d), lambda bi, i: (bi, i, 0)),
                  pl.BlockSpec((None, N_MOD, d), lambda bi, i: (bi, 0, 0)),
                  pl.BlockSpec((None, N_MOD, d), lambda bi, i: (nb, 0, 0)),
                  pl.BlockSpec((1, d), lambda bi, i: (0, 0)),
                  pl.BlockSpec((d, n), lambda bi, i: (0, 0))],
        out_specs=pl.BlockSpec((None, tm, n), lambda bi, i: (bi, i, 0)),
        out_shape=jax.ShapeDtypeStruct((b, ttot, n), F32),
        compiler_params=_cparams(("parallel", "parallel")),
        name="inproj",
    )(xa, mod_l, mod_l, g.reshape(1, d), w)


def _seq_block(step, n_ctx, n_all, reverse):
    if not reverse:
        return step
    return jnp.where(step < n_ctx, n_ctx - 1 - step, n_all - 1 - step + n_ctx)


def _lru_kernel(u_ref, hp_ref, hn_ref, cw_ref, cb_ref, wa_ref, ba_ref, wx_ref, bx_ref, sp_ref,
                o_ref, uc_ref, a_ref, b_ref, h_ref, *, lc, n_ctx, n_all, reverse):
    step = pl.program_id(1)
    blk = _seq_block(step, n_ctx, n_all, reverse)
    has_prev = jnp.logical_and(blk != 0, blk != n_ctx)
    has_next = jnp.logical_and(blk != n_ctx - 1, blk != n_all - 1)

    @pl.when(step == 0)
    def _():
        h_ref[...] = jnp.zeros_like(h_ref)

    w0, w1, w2, w3 = (cw_ref[j:j + 1, :] for j in range(CONV_W))
    cb = cb_ref[...]
    u = u_ref[...]
    uc_ref[...] = (cb + pltpu.roll(u, 2, 0) * w0 + pltpu.roll(u, 1, 0) * w1 + u * w2
                   + pltpu.roll(u, lc - 1, 0) * w3)
    row = lax.broadcasted_iota(jnp.int32, (SUBLANES, 1), 0)
    hp = jnp.where(has_prev, hp_ref[...], 0.0)
    hn = jnp.where(has_next, hn_ref[...], 0.0)
    u0 = u_ref[0:SUBLANES, :]
    u1 = u_ref[SUBLANES:2 * SUBLANES, :]
    uc_ref[0:SUBLANES, :] = (
        cb + jnp.where(row < 2, pltpu.roll(hp, 2, 0), pltpu.roll(u0, 2, 0)) * w0
        + jnp.where(row < 1, pltpu.roll(hp, 1, 0), pltpu.roll(u0, 1, 0)) * w1 + u0 * w2
        + jnp.where(row < SUBLANES - 1, pltpu.roll(u0, SUBLANES - 1, 0),
                    pltpu.roll(u1, SUBLANES - 1, 0)) * w3)
    ul = u_ref[lc - SUBLANES:lc, :]
    um = u_ref[lc - 2 * SUBLANES:lc - SUBLANES, :]
    uc_ref[lc - SUBLANES:lc, :] = (
        cb + jnp.where(row < 2, pltpu.roll(um, 2, 0), pltpu.roll(ul, 2, 0)) * w0
        + jnp.where(row < 1, pltpu.roll(um, 1, 0), pltpu.roll(ul, 1, 0)) * w1 + ul * w2
        + jnp.where(row < SUBLANES - 1, pltpu.roll(ul, SUBLANES - 1, 0),
                    pltpu.roll(hn, SUBLANES - 1, 0)) * w3)

    uc = uc_ref[...]
    ub = uc.astype(BF16)
    r = _sigmoid(jnp.dot(ub, wa_ref[...], preferred_element_type=F32) + ba_ref[...])
    gi = _sigmoid(jnp.dot(ub, wx_ref[...], preferred_element_type=F32) + bx_ref[...])
    log_a = -LRU_C * r * sp_ref[...]
    a = jnp.exp(log_a)
    b = jnp.sqrt(1.0 - jnp.exp(2.0 * log_a)) * (gi * uc)
    sub = lax.broadcasted_iota(jnp.int32, (lc, 1), 0) % SUBLANES
    dist = 1
    while dist < SUBLANES:
        inside = (sub < SUBLANES - dist) if reverse else (sub >= dist)
        shift = (lc - dist) if reverse else dist
        a_sh = jnp.where(inside, pltpu.roll(a, shift, 0), 1.0)
        b_sh = jnp.where(inside, pltpu.roll(b, shift, 0), 0.0)
        b = a * b_sh + b
        a = a * a_sh
        dist *= 2
    a_ref[...] = a
    b_ref[...] = b
    n_grp = lc // SUBLANES

    def body(i, carry):
        t = (n_grp - 1 - i) if reverse else i
        rows = pl.ds(pl.multiple_of(t * SUBLANES, SUBLANES), SUBLANES)
        h = a_ref[rows, :] * carry + b_ref[rows, :]
        b_ref[rows, :] = h
        return h[0:1, :] if reverse else h[SUBLANES - 1:SUBLANES, :]

    h_ref[0:1, :] = lax.fori_loop(0, n_grp, body, h_ref[0:1, :], unroll=4)
    o_ref[...] = b_ref[...].astype(o_ref.dtype)


def _lru(z, conv_w, conv_b, wa_bd, ba, wx_bd, bx, sp, tc, reverse):
    b, ttot, _ = z.shape
    w = conv_w.shape[1]
    lc = SEQ_BLOCK
    n_all, n_ctx = ttot // lc, tc // lc
    hb = lc // SUBLANES
    n_h = ttot // SUBLANES
    col = 1

    def blk(s):
        return _seq_block(s, n_ctx, n_all, reverse)

    vec = lambda: pl.BlockSpec((1, w), lambda bi, s: (0, 0))
    return pl.pallas_call(
        functools.partial(_lru_kernel, lc=lc, n_ctx=n_ctx, n_all=n_all, reverse=reverse),
        grid=(b, n_all),
        in_specs=[pl.BlockSpec((None, lc, w), lambda bi, s: (bi, blk(s), col)),
                  pl.BlockSpec((None, SUBLANES, w),
                               lambda bi, s: (bi, jnp.maximum(blk(s) * hb - 1, 0), col)),
                  pl.BlockSpec((None, SUBLANES, w),
                               lambda bi, s: (bi, jnp.minimum((blk(s) + 1) * hb, n_h - 1), col)),
                  pl.BlockSpec((CONV_W, w), lambda bi, s: (0, 0)),
                  vec(),
                  pl.BlockSpec((w, w), lambda bi, s: (0, 0)),
                  vec(),
                  pl.BlockSpec((w, w), lambda bi, s: (0, 0)),
                  vec(), vec()],
        out_specs=pl.BlockSpec((None, lc, w), lambda bi, s: (bi, blk(s), 0)),
        out_shape=jax.ShapeDtypeStruct((b, ttot, w), BF16),
        scratch_shapes=[pltpu.VMEM((lc, w), F32), pltpu.VMEM((lc, w), F32),
                        pltpu.VMEM((lc, w), F32), pltpu.VMEM((SUBLANES, w), F32)],
        compiler_params=_cparams(("parallel", "arbitrary")),
        name="rglru_rev" if reverse else "rglru_fwd",
    )(z, z, z, conv_w, conv_b.reshape(1, w), wa_bd, ba.reshape(1, w), wx_bd, bx.reshape(1, w),
      sp.reshape(1, w))


def _rope(x, cos, sin_signed):
    lane = lax.broadcasted_iota(jnp.int32, x.shape, 1)
    swapped = jnp.where(lane % 32 < 16, pltpu.roll(x, LANES - 16, 1), pltpu.roll(x, 16, 1))
    return x * cos + swapped * sin_signed


def _pair_layout(x, x_sw, head):
    lo = lax.broadcasted_iota(jnp.int32, x.shape, 1) < HEAD_DIM
    first, second = (x, x_sw) if head == 0 else (x_sw, x)
    return jnp.concatenate([jnp.where(lo, first, 0.0), jnp.where(lo, 0.0, second)],
                           axis=0).astype(BF16)


def _attend(q_block, keys, vals, bias, sink_row, o_ref):
    nk = keys.shape[0]
    keys_sw = pltpu.roll(keys, HEAD_DIM, 1)
    vals_t = vals.T
    top = lax.broadcasted_iota(jnp.int32, (LANES, BLOCK_Q), 0) < HEAD_DIM
    zero_t = jnp.zeros((HEAD_DIM, nk), F32)
    pairs_per_kv = ATT_GROUP // 2
    for kv in range(ATT_KV_HEADS):
        kab = _pair_layout(keys, keys_sw, kv)
        vt = vals_t[kv * HEAD_DIM:(kv + 1) * HEAD_DIM, :]
        vab_t = jnp.concatenate([jnp.concatenate([vt, zero_t], axis=1),
                                 jnp.concatenate([zero_t, vt], axis=1)], axis=0).astype(BF16)
        for g in range(kv * pairs_per_kv, (kv + 1) * pairs_per_kv):
            s_t = lax.dot_general(kab, q_block(g), (((1,), (1,)), ((), ())),
                                  preferred_element_type=F32)
            halves = []
            recips = []
            for half in range(2):
                sh = s_t[half * nk:(half + 1) * nk, :]
                if bias is not None:
                    sh = sh + bias
                sk = sink_row[:, 2 * g + half:2 * g + half + 1]
                m = jnp.maximum(jnp.max(sh, axis=0, keepdims=True), sk)
                p = jnp.exp2(sh - m)
                recips.append(1.0 / (jnp.sum(p, axis=0, keepdims=True) + jnp.exp2(sk - m)))
                halves.append(p.astype(BF16))
            o_t = jnp.dot(vab_t, jnp.concatenate(halves, axis=0), preferred_element_type=F32)
            o_t = o_t * jnp.where(top, recips[0], recips[1])
            o_ref[:, g * LANES:(g + 1) * LANES] = o_t.T.astype(o_ref.dtype)


def _attn_kernel(q_ref, kp_ref, k0_ref, kn_ref, vp_ref, v0_ref, vn_ref, kc_ref, vc_ref,
                 cq_ref, sq_ref, cp_ref, sp_ref, cn_ref, sn_ref, sink_ref, o_ref, *, n_ctx, n_lat):
    i = pl.program_id(1)
    qscale = HEAD_DIM ** -0.5 * LOG2E
    sink_row = sink_ref[...] * LOG2E

    @pl.when(i < n_ctx)
    def _():
        def q_block(g):
            return (q_ref[:, g * LANES:(g + 1) * LANES] * qscale).astype(BF16)
        _attend(q_block, kc_ref[...], vc_ref[...], None, sink_row, o_ref)

    @pl.when(i >= n_ctx)
    def _():
        n = i - n_ctx
        cq, sq = cq_ref[...], sq_ref[...]

        def q_block(g):
            return (_rope(q_ref[:, g * LANES:(g + 1) * LANES], cq, sq) * qscale).astype(BF16)

        kp = _rope(kp_ref[...], cp_ref[...], sp_ref[...])
        k0 = _rope(k0_ref[...], cq, sq)
        kn = _rope(kn_ref[...], cn_ref[...], sn_ref[...])
        keys = jnp.concatenate([kp, k0, kn, kc_ref[...]], axis=0)
        vals = jnp.concatenate([vp_ref[...], v0_ref[...], vn_ref[...], vc_ref[...]], axis=0)
        nk = keys.shape[0]
        qpos = n * BLOCK_Q + lax.broadcasted_iota(jnp.int32, (nk, BLOCK_Q), 1)
        key = lax.broadcasted_iota(jnp.int32, (nk, BLOCK_Q), 0)
        kpos = (n - 1) * BLOCK_Q + key
        local = (jnp.abs(qpos - kpos) <= WINDOW) & (kpos >= 0) & (kpos < n_lat * BLOCK_Q)
        bias = jnp.where(local | (key >= 3 * BLOCK_Q), 0.0, NEG)
        _attend(q_block, keys, vals, bias, sink_row, o_ref)


def _attention(z, cos_t, sin_t, sink, tc):
    b, ttot, _ = z.shape
    n_all, n_ctx = ttot // BLOCK_Q, tc // BLOCK_Q
    n_lat = n_all - n_ctx
    kcol = (2 * 512 + ATT_HEADS * HEAD_DIM) // LANES
    vcol = kcol + 1
    qcol = 2 * 512 // (ATT_HEADS * HEAD_DIM)

    def prev(i):
        return jnp.maximum(i - 1, 0)

    def nxt(i):
        return jnp.minimum(i + 1, n_all - 1)

    def rows(col, f):
        return pl.BlockSpec((None, BLOCK_Q, LANES), lambda bi, i: (bi, f(i), col))

    def tab(f):
        return pl.BlockSpec((BLOCK_Q, LANES), lambda bi, i: (f(i), 0))

    same = lambda i: i
    return pl.pallas_call(
        functools.partial(_attn_kernel, n_ctx=n_ctx, n_lat=n_lat),
        grid=(b, n_all),
        in_specs=[pl.BlockSpec((None, BLOCK_Q, ATT_HEADS * HEAD_DIM), lambda bi, i: (bi, i, qcol)),
                  rows(kcol, prev), rows(kcol, same), rows(kcol, nxt),
                  rows(vcol, prev), rows(vcol, same), rows(vcol, nxt),
                  pl.BlockSpec((None, tc, LANES), lambda bi, i: (bi, 0, kcol)),
                  pl.BlockSpec((None, tc, LANES), lambda bi, i: (bi, 0, vcol)),
                  tab(same), tab(same), tab(prev), tab(prev), tab(nxt), tab(nxt),
                  pl.BlockSpec((1, LANES), lambda bi, i: (0, 0))],
        out_specs=pl.BlockSpec((None, BLOCK_Q, ATT_HEADS * HEAD_DIM), lambda bi, i: (bi, i, 0)),
        out_shape=jax.ShapeDtypeStruct((b, ttot, ATT_HEADS * HEAD_DIM), BF16),
        compiler_params=_cparams(("parallel", "parallel")),
        name="window_gqa",
    )(z, z, z, z, z, z, z, z, z, cos_t, sin_t, cos_t, sin_t, cos_t, sin_t, sink)


def _outproj_ab_kernel(x_ref, mb_ref, mc_ref, lf_ref, lb_ref, g_ref, at_ref, w_ref, o_ref, *, tm, tc):
    i = pl.program_id(1)
    lru = (lf_ref[...].astype(F32) + lb_ref[...].astype(F32)) * _gelu(g_ref[...])
    mix = jnp.concatenate([lru.astype(BF16), at_ref[...]], axis=1)
    dx = jnp.dot(mix, w_ref[...], preferred_element_type=F32)
    gate = _gate_mod(mb_ref[...], mc_ref[...], 2, i * tm, tc, tm)
    o_ref[...] = x_ref[...] + gate * dx


def _outproj_ab(xa, mod_l, lru_f, lru_b, z, att, w_out, tc, nb):
    b, ttot, d = xa.shape
    w = lru_f.shape[2]
    tm = _row_tile(ttot)
    tile = lambda width, col: pl.BlockSpec((None, tm, width), lambda bi, i: (bi, i, col))
    return pl.pallas_call(
        functools.partial(_outproj_ab_kernel, tm=tm, tc=tc),
        grid=(b, ttot // tm),
        in_specs=[tile(d, 0),
                  pl.BlockSpec((None, N_MOD, d), lambda bi, i: (bi, 0, 0)),
                  pl.BlockSpec((None, N_MOD, d), lambda bi, i: (nb, 0, 0)),
                  tile(w, 0), tile(w, 0), tile(w, 0), tile(att.shape[2], 0),
                  pl.BlockSpec(w_out.shape, lambda bi, i: (0, 0))],
        out_specs=tile(d, 0),
        out_shape=jax.ShapeDtypeStruct(xa.shape, F32),
        compiler_params=_cparams(("parallel", "parallel")),
        name="outproj_ab",
    )(xa, mod_l, mod_l, lru_f, lru_b, z, att, w_out)


def _outproj_cd_kernel(x_ref, mb_ref, mc_ref, y_ref, of_ref, ob_ref, g_ref, wg_ref, bg_ref, gn_ref,
                       w_ref, o_ref, *, tm, tc):
    i = pl.program_id(1)
    y = _gelu(y_ref[...])
    s5 = y * _sigmoid(jnp.dot(y.astype(BF16), wg_ref[...], preferred_element_type=F32) + bg_ref[...])
    o = of_ref[...].astype(F32) + ob_ref[...].astype(F32)
    parts = []
    for h in range(HG_HEADS):
        oh = o[:, h * HG_DK:(h + 1) * HG_DK]
        ms = jnp.mean(oh * oh, axis=-1, keepdims=True)
        parts.append(oh * lax.rsqrt(ms + EPS) * gn_ref[...])
    hg = jnp.concatenate(parts, axis=1) * _silu(g_ref[...])
    mix = jnp.concatenate([s5.astype(BF16), hg.astype(BF16)], axis=1)
    dx = jnp.dot(mix, w_ref[...], preferred_element_type=F32)
    gate = _gate_mod(mb_ref[...], mc_ref[...], 2, i * tm, tc, tm)
    o_ref[...] = x_ref[...] + gate * dx


def _outproj_cd(xa, mod_l, y5, o_f, o_b, z, w_glu, b_glu, hg_norm, w_out, tc, nb):
    b, ttot, d = xa.shape
    w = y5.shape[2]
    tm = _row_tile(ttot)
    gcol = z.shape[2] // w - 1
    tile = lambda width, col: pl.BlockSpec((None, tm, width), lambda bi, i: (bi, i, col))
    return pl.pallas_call(
        functools.partial(_outproj_cd_kernel, tm=tm, tc=tc),
        grid=(b, ttot // tm),
        in_specs=[tile(d, 0),
                  pl.BlockSpec((None, N_MOD, d), lambda bi, i: (bi, 0, 0)),
                  pl.BlockSpec((None, N_MOD, d), lambda bi, i: (nb, 0, 0)),
                  tile(w, 0), tile(w, 0), tile(w, 0), tile(w, gcol),
                  pl.BlockSpec(w_glu.shape, lambda bi, i: (0, 0)),
                  pl.BlockSpec((1, w), lambda bi, i: (0, 0)),
                  pl.BlockSpec((1, HG_DK), lambda bi, i: (0, 0)),
                  pl.BlockSpec(w_out.shape, lambda bi, i: (0, 0))],
        out_specs=tile(d, 0),
        out_shape=jax.ShapeDtypeStruct(xa.shape, F32),
        compiler_params=_cparams(("parallel", "parallel")),
        name="outproj_cd",
    )(xa, mod_l, mod_l, y5, o_f, o_b, z, w_glu, b_glu.reshape(1, w), hg_norm.reshape(1, HG_DK), w_out)


def _ffn_kernel(x_ref, mb_ref, mc_ref, g_ref, w1_ref, w3_ref, w2_ref, o_ref, h_ref, acc_ref, *, tm, tc):
    i = pl.program_id(1)
    k = pl.program_id(2)

    @pl.when(k == 0)
    def _():
        h = _norm_mod(x_ref[...], g_ref[...], mb_ref[...], mc_ref[...], i * tm, tc, 3, 4)
        h_ref[...] = h.astype(BF16)
        acc_ref[...] = jnp.zeros_like(acc_ref)

    h = h_ref[...]
    a = jnp.dot(h, w1_ref[...], preferred_element_type=F32)
    c = jnp.dot(h, w3_ref[...], preferred_element_type=F32)
    acc_ref[...] += jnp.dot((_silu(a) * c).astype(BF16), w2_ref[...], preferred_element_type=F32)

    @pl.when(k == pl.num_programs(2) - 1)
    def _():
        gate = _gate_mod(mb_ref[...], mc_ref[...], 5, i * tm, tc, tm)
        o_ref[...] = x_ref[...] + gate * acc_ref[...]


def _ffn(xa, mod_l, g, w1, w3, w2, tc, nb):
    b, ttot, d = xa.shape
    f = w1.shape[1]
    tm = _row_tile(ttot)
    tf = f // 2 if (f // 2) % LANES == 0 else f
    return pl.pallas_call(
        functools.partial(_ffn_kernel, tm=tm, tc=tc),
        grid=(b, ttot // tm, f // tf),
        in_specs=[pl.BlockSpec((None, tm, d), lambda bi, i, k: (bi, i, 0)),
                  pl.BlockSpec((None, N_MOD, d), lambda bi, i, k: (bi, 0, 0)),
                  pl.BlockSpec((None, N_MOD, d), lambda bi, i, k: (nb, 0, 0)),
                  pl.BlockSpec((1, d), lambda bi, i, k: (0, 0)),
                  pl.BlockSpec((d, tf), lambda bi, i, k: (0, k)),
                  pl.BlockSpec((d, tf), lambda bi, i, k: (0, k)),
                  pl.BlockSpec((tf, d), lambda bi, i, k: (k, 0))],
        out_specs=pl.BlockSpec((None, tm, d), lambda bi, i, k: (bi, i, 0)),
        out_shape=jax.ShapeDtypeStruct(xa.shape, F32),
        scratch_shapes=[pltpu.VMEM((tm, d), BF16), pltpu.VMEM((tm, d), F32)],
        compiler_params=_cparams(("parallel", "parallel", "arbitrary")),
        name="ffn_swiglu",
    )(xa, mod_l, mod_l, g.reshape(1, d), w1, w3, w2)


def _router_kernel(x_ref, mb_ref, mc_ref, g_ref, r_ref, h_ref, wt_ref, cnt_ref, *, tm, tc, n_exp):
    i = pl.program_id(1)
    h = _norm_mod(x_ref[...], g_ref[...], mb_ref[...], mc_ref[...], i * tm, tc, 3, 4)
    h_ref[...] = h.astype(BF16)
    logits = lax.dot_general(r_ref[...], h, (((1,), (1,)), ((), ())), preferred_element_type=F32,
                             precision=lax.Precision.HIGHEST)
    sub = lax.broadcasted_iota(jnp.int32, logits.shape, 0)
    logits = jnp.where(sub < n_exp, logits, NEG)
    m1 = jnp.max(logits, axis=0, keepdims=True)
    i1 = jnp.min(jnp.where(logits == m1, sub, SUBLANES), axis=0, keepdims=True)
    rest = jnp.where(sub == i1, NEG, logits)
    m2 = jnp.max(rest, axis=0, keepdims=True)
    i2 = jnp.min(jnp.where(rest == m2, sub, SUBLANES), axis=0, keepdims=True)
    e2 = jnp.exp(m2 - m1)
    g1 = 1.0 / (1.0 + e2)
    wt = jnp.where(sub == i1, g1, 0.0) + jnp.where(sub == i2, e2 * g1, 0.0)
    wt_ref[...] = wt
    cnt = jnp.sum(jnp.where(wt > 0.0, 1.0, 0.0), axis=1, keepdims=True)
    cnt_ref[...] = jnp.broadcast_to(cnt, cnt_ref.shape).astype(jnp.int32)


def _router(xa, mod_l, g, router_t, tc, nb, n_exp):
    b, ttot, d = xa.shape
    tm = _row_tile(ttot)
    n_t = ttot // tm
    return pl.pallas_call(
        functools.partial(_router_kernel, tm=tm, tc=tc, n_exp=n_exp),
        grid=(b, n_t),
        in_specs=[pl.BlockSpec((None, tm, d), lambda bi, i: (bi, i, 0)),
                  pl.BlockSpec((None, N_MOD, d), lambda bi, i: (bi, 0, 0)),
                  pl.BlockSpec((None, N_MOD, d), lambda bi, i: (nb, 0, 0)),
                  pl.BlockSpec((1, d), lambda bi, i: (0, 0)),
                  pl.BlockSpec((SUBLANES, d), lambda bi, i: (0, 0))],
        out_specs=[pl.BlockSpec((tm, d), lambda bi, i: (bi * n_t + i, 0)),
                   pl.BlockSpec((SUBLANES, tm), lambda bi, i: (0, bi * n_t + i)),
                   pl.BlockSpec((None, SUBLANES, LANES), lambda bi, i: (bi * n_t + i, 0, 0))],
        out_shape=[jax.ShapeDtypeStruct((b * ttot, d), BF16),
                   jax.ShapeDtypeStruct((SUBLANES, b * ttot), F32),
                   jax.ShapeDtypeStruct((b * n_t, SUBLANES, LANES), jnp.int32)],
        compiler_params=_cparams(("parallel", "parallel")),
        name="moe_router",
    )(xa, mod_l, mod_l, g.reshape(1, d), router_t)


def _moe_kernel(cnt_ref, x_ref, *refs, tm, tc, n_exp, n_t, group):
    mb_refs = refs[:group]
    mc_ref, h_ref, wt_ref, w1_ref, w3_ref, w2_ref, o_ref, rank_ref = refs[group:]
    p = pl.program_id(0)
    e = pl.program_id(1)
    f = w1_ref.shape[1]
    half = (f // LANES + 1) // 2 * LANES
    f_split = [(0, half), (half, f)] if 0 < half < f else [(0, f)]

    @pl.when(e == 0)
    def _():
        o_ref[...] = jnp.zeros_like(o_ref)
        r_i = lax.broadcasted_iota(jnp.int32, (tm, tm), 0)
        c_i = lax.broadcasted_iota(jnp.int32, (tm, tm), 1)
        before = jnp.where(r_i < c_i, 1.0, 0.0).astype(BF16)
        for j in range(group):
            sel = wt_ref[:, j * tm:(j + 1) * tm] > 0.0
            rank = jnp.dot(jnp.where(sel, 1.0, 0.0).astype(BF16), before, preferred_element_type=F32)
            rank_ref[j] = jnp.where(sel, rank, -1.0)

    counts = [cnt_ref[(p * group + j) * n_exp + e] for j in range(group)]

    offsets = [jnp.int32(0)]
    for j in range(group - 1):
        offsets.append(offsets[-1] + counts[j])
    total = offsets[-1] + counts[-1]

    def expert_pass(m_rows, base):
        slot = (lax.broadcasted_iota(jnp.int32, (m_rows, tm), 0) + base).astype(F32)
        onehots = []
        xe = None
        gate = None
        for j in range(group):
            r_row = rank_ref[j, pl.ds(e, 1), :]
            r_row = jnp.where(r_row >= 0.0, r_row + offsets[j].astype(F32), -1.0)
            w_row = wt_ref[pl.ds(e, 1), j * tm:(j + 1) * tm]
            hit = r_row == slot
            onehot = jnp.where(hit, 1.0, 0.0).astype(BF16)
            onehots.append(onehot)
            xj = jnp.dot(onehot, h_ref[j * tm:(j + 1) * tm, :], preferred_element_type=F32)
            gj = jnp.sum(jnp.where(hit, w_row, 0.0), axis=1, keepdims=True)
            xe = xj if xe is None else xe + xj
            gate = gj if gate is None else gate + gj
        xe = xe.astype(BF16)
        y = None
        for f0, f1 in f_split:
            a = jnp.dot(xe, w1_ref[:, f0:f1], preferred_element_type=F32)
            c = jnp.dot(xe, w3_ref[:, f0:f1], preferred_element_type=F32)
            part = jnp.dot((_silu(a) * c * gate).astype(BF16), w2_ref[f0:f1, :],
                           preferred_element_type=F32)
            y = part if y is None else y + part
        y = y.astype(BF16)
        for j in range(group):
            o_ref[j * tm:(j + 1) * tm, :] += lax.dot_general(
                onehots[j], y, (((0,), (0,)), ((), ())), preferred_element_type=F32)

    n_full = total // MOE_PASS_ROWS

    def full_pass(it, carry):
        expert_pass(MOE_PASS_ROWS, it * MOE_PASS_ROWS)
        return carry

    lax.fori_loop(0, n_full, full_pass, 0)
    rest = total - n_full * MOE_PASS_ROWS

    @pl.when(rest > MOE_PASS_ROWS // 2)
    def _():
        expert_pass(MOE_PASS_ROWS, n_full * MOE_PASS_ROWS)

    @pl.when(jnp.logical_and(rest > 0, rest <= MOE_PASS_ROWS // 2))
    def _():
        expert_pass(MOE_PASS_ROWS // 2, n_full * MOE_PASS_ROWS)

    @pl.when(e == n_exp - 1)
    def _():
        for j in range(group):
            i = (p * group + j) % n_t
            gate5 = _gate_mod(mb_refs[j][...], mc_ref[...], 5, i * tm, tc, tm)
            rows = slice(j * tm, (j + 1) * tm)
            o_ref[rows, :] = x_ref[rows, :] + gate5 * o_ref[rows, :]


def _moe(xa, mod_l, g, router_t, w1, w3, w2, tc, nb):
    b, ttot, d = xa.shape
    n_exp, _, f = w1.shape
    tm = _row_tile(ttot)
    n_t = ttot // tm
    group = 2 if (b * n_t) % 2 == 0 else 1
    h, wt, cnt = _router(xa, mod_l, g, router_t, tc, nb, n_exp)
    cnt = cnt[:, :n_exp, 0].reshape(-1)
    rows = group * tm
    mods = [pl.BlockSpec((None, N_MOD, d), functools.partial(
        lambda p, e, c, j: ((p * group + j) // n_t, 0, 0), j=j)) for j in range(group)]
    grid_spec = pltpu.PrefetchScalarGridSpec(
        num_scalar_prefetch=1,
        grid=(b * n_t // group, n_exp),
        in_specs=[pl.BlockSpec((rows, d), lambda p, e, c: (p, 0), pipeline_mode=pl.Buffered(1))]
        + mods
        + [pl.BlockSpec((None, N_MOD, d), lambda p, e, c: (nb, 0, 0)),
           pl.BlockSpec((rows, d), lambda p, e, c: (p, 0)),
           pl.BlockSpec((SUBLANES, rows), lambda p, e, c: (0, p)),
           pl.BlockSpec((None, d, f), lambda p, e, c: (e, 0, 0)),
           pl.BlockSpec((None, d, f), lambda p, e, c: (e, 0, 0)),
           pl.BlockSpec((None, f, d), lambda p, e, c: (e, 0, 0))],
        out_specs=pl.BlockSpec((rows, d), lambda p, e, c: (p, 0)),
        scratch_shapes=[pltpu.VMEM((group, SUBLANES, tm), F32)])
    out = pl.pallas_call(
        functools.partial(_moe_kernel, tm=tm, tc=tc, n_exp=n_exp, n_t=n_t, group=group),
        grid_spec=grid_spec,
        out_shape=jax.ShapeDtypeStruct((b * ttot, d), F32),
        compiler_params=_cparams(("parallel", "arbitrary")),
        name="moe_experts",
    )(cnt, xa.reshape(b * ttot, d), *([mod_l] * group), mod_l, h, wt, w1, w3, w2)
    return out.reshape(b, ttot, d)


def _spread_groups(x, inner, row_group, n_grp):
    cols_in = x.shape[2]
    cols_out = cols_in * n_grp
    q = jnp.arange(cols_out)
    src = (q // (n_grp * inner)) * inner + q % inner
    tile = (jnp.arange(cols_in)[:, None] == src[None, :]).astype(BF16)
    col_group = (q // inner) % n_grp
    y = jnp.dot(x.astype(BF16), tile)
    return jnp.where(row_group[:, None] == col_group[None, :], y, jnp.zeros((), BF16))


def _s5_weights(a_re, a_im, log_step, b_re, b_im, c_re, c_im):
    L = S5_CHUNK
    lr = jnp.minimum(a_re, -1e-4)
    li = a_im
    dt = jnp.exp(log_step)[..., None]
    mag, ang = lr * dt, li * dt
    lbr, lbi = jnp.exp(mag) * jnp.cos(ang), jnp.exp(mag) * jnp.sin(ang)
    zr, zi = lbr - 1.0, lbi
    den = lr * lr + li * li
    fr = (zr * lr + zi * li) / den
    fi = (zi * lr - zr * li) / den
    bbr = fr[..., None] * b_re - fi[..., None] * b_im
    bbi = fr[..., None] * b_im + fi[..., None] * b_re

    def power(p):
        p = p[..., None, None, None].astype(F32)
        return jnp.exp(mag * p) * jnp.cos(ang * p), jnp.exp(mag * p) * jnp.sin(ang * p)

    n_dir, n_grp, n_st = a_re.shape
    gpb = LANES // S5_GROUP
    n_blk = n_grp // gpb
    s = jnp.arange(L)
    in_group = (jnp.arange(L * LANES) % LANES) // S5_GROUP
    st_group = jnp.arange(gpb * n_st) // n_st
    outs = []
    for d in range(n_dir):
        pr, pi = power(jnp.arange(L))
        pr, pi = pr[:, d], pi[:, d]
        cbr = (jnp.einsum('gcn,tgn,gnk->tgck', c_re[d], pr, bbr[d])
               - jnp.einsum('gcn,tgn,gnk->tgck', c_re[d], pi, bbi[d])
               - jnp.einsum('gcn,tgn,gnk->tgck', c_im[d], pr, bbi[d])
               - jnp.einsum('gcn,tgn,gnk->tgck', c_im[d], pi, bbr[d]))
        lag = (s[None, :] - s[:, None]) if d == 0 else (s[:, None] - s[None, :])
        kern = jnp.where((lag >= 0)[..., None, None, None], cbr[jnp.clip(lag, 0, L - 1)], 0.0)
        kern = kern.reshape(L, L, n_blk, gpb, S5_GROUP, S5_GROUP).transpose(2, 0, 3, 5, 1, 4)
        m = _spread_groups(kern.reshape(n_blk, L * LANES, L * S5_GROUP), S5_GROUP, in_group, gpb)
        qr, qi = power((L - 1 - s) if d == 0 else s)
        qr, qi = qr[:, d], qi[:, d]
        str_ = qr[..., None] * bbr[d] - qi[..., None] * bbi[d]
        sti = qr[..., None] * bbi[d] + qi[..., None] * bbr[d]

        def to_state(x):
            x = x.reshape(L, n_blk, gpb, n_st, S5_GROUP).transpose(1, 0, 2, 4, 3)
            return _spread_groups(x.reshape(n_blk, L * LANES, n_st), n_st, in_group, gpb)

        wst = jnp.concatenate([to_state(str_), to_state(sti)], axis=2)
        rr, ri = power((s + 1) if d == 0 else (L - s))
        rr, ri = rr[:, d], ri[:, d]
        wr = c_re[d][None] * rr[:, :, None, :] - c_im[d][None] * ri[:, :, None, :]
        wi = -(c_re[d][None] * ri[:, :, None, :] + c_im[d][None] * rr[:, :, None, :])

        def from_state(x):
            x = x.reshape(L, n_blk, gpb, S5_GROUP, n_st).transpose(1, 2, 4, 0, 3)
            return _spread_groups(x.reshape(n_blk, gpb * n_st, L * S5_GROUP), S5_GROUP, st_group, gpb)

        wout = jnp.concatenate([from_state(wr), from_state(wi)], axis=1)
        ler, lei = jnp.exp(mag[d] * L) * jnp.cos(ang[d] * L), jnp.exp(mag[d] * L) * jnp.sin(ang[d] * L)
        lam_l = jnp.concatenate([ler.reshape(n_blk, 1, gpb * n_st), lei.reshape(n_blk, 1, gpb * n_st)],
                                axis=2)
        outs.append((m, wst, wout, lam_l))
    return tuple(jnp.stack([o[k] for o in outs]) for k in range(4))


def _s5_kernel(u_ref, m_ref, wst_ref, wout_ref, lam_ref, d_ref, o_ref, x_ref, hp_ref, *, rows, rows_ctx):
    L = S5_CHUNK
    dr = pl.program_id(2)
    ns = lam_ref.shape[1] // 2
    u = jnp.concatenate([u_ref[pl.ds(s, rows, stride=L), :] for s in range(L)], axis=1)
    ub = u.astype(BF16)
    x_ref[...] = jnp.dot(ub, wst_ref[...], preferred_element_type=F32)
    lr = lam_ref[:, 0:ns]
    li = lam_ref[:, ns:2 * ns]

    def visit(r, carry):
        hr, hi = carry
        hp_ref[pl.ds(r, 1), 0:ns] = hr
        hp_ref[pl.ds(r, 1), ns:2 * ns] = hi
        xr = x_ref[pl.ds(r, 1), 0:ns]
        xi = x_ref[pl.ds(r, 1), ns:2 * ns]
        return lr * hr - li * hi + xr, lr * hi + li * hr + xi

    zero = (jnp.zeros((1, ns), F32), jnp.zeros((1, ns), F32))

    @pl.when(dr == 0)
    def _():
        lax.fori_loop(0, rows, visit, zero, unroll=4)

    @pl.when(dr == 1)
    def _():
        c = lax.fori_loop(0, rows_ctx, lambda t, c: visit(rows_ctx - 1 - t, c), zero, unroll=4)
        lax.fori_loop(0, rows - rows_ctx, lambda t, c: visit(rows - 1 - t, c), c, unroll=4)

    y = (jnp.dot(ub, m_ref[...], preferred_element_type=F32)
         + jnp.dot(hp_ref[...].astype(BF16), wout_ref[...], preferred_element_type=F32))

    @pl.when(dr == 0)
    def _():
        for s in range(L):
            o_ref[pl.ds(s, rows, stride=L), :] = (y[:, s * LANES:(s + 1) * LANES]
                                                  + d_ref[...] * u[:, s * LANES:(s + 1) * LANES])

    @pl.when(dr == 1)
    def _():
        for s in range(L):
            o_ref[pl.ds(s, rows, stride=L), :] += y[:, s * LANES:(s + 1) * LANES]


def _s5(z, weights, dvec, tc, width):
    b, ttot, _ = z.shape
    m, wst, wout, lam_l = weights
    n_dir, n_blk = m.shape[0], m.shape[1]
    rows, rows_ctx = ttot // S5_CHUNK, tc // S5_CHUNK
    wspec = lambda a: pl.BlockSpec((None, None) + a.shape[2:], lambda bi, j, dr: (dr, j, 0, 0))
    return pl.pallas_call(
        functools.partial(_s5_kernel, rows=rows, rows_ctx=rows_ctx),
        grid=(b, n_blk, n_dir),
        in_specs=[pl.BlockSpec((None, ttot, LANES), lambda bi, j, dr: (bi, 0, j)),
                  wspec(m), wspec(wst), wspec(wout), wspec(lam_l),
                  pl.BlockSpec((1, LANES), lambda bi, j, dr: (0, j))],
        out_specs=pl.BlockSpec((None, ttot, LANES), lambda bi, j, dr: (bi, 0, j)),
        out_shape=jax.ShapeDtypeStruct((b, ttot, width), F32),
        scratch_shapes=[pltpu.VMEM((rows, lam_l.shape[3]), F32), pltpu.VMEM((rows, lam_l.shape[3]), F32)],
        compiler_params=_cparams(("parallel", "parallel", "arbitrary")),
        name="s5_bidir",
    )(z, m, wst, wout, lam_l, dvec.reshape(1, width))


def _gla_kernel(qf_ref, ff_ref, vf_ref, qb_ref, fb_ref, vb_ref, lb_ref, of_ref, ob_ref,
                stf_ref, stb_ref, *, lc):
    step = pl.program_id(1)

    @pl.when(step == 0)
    def _():
        stf_ref[...] = jnp.zeros_like(stf_ref)
        stb_ref[...] = jnp.zeros_like(stb_ref)

    _gla_block(qf_ref, ff_ref, vf_ref, lb_ref, of_ref, stf_ref, lc=lc, reverse=False)
    _gla_block(qb_ref, fb_ref, vb_ref, lb_ref, ob_ref, stb_ref, lc=lc, reverse=True)


def _gla_block(q_ref, f_ref, v_ref, lb_ref, o_ref, st_ref, *, lc, reverse):
    c = HG_CHUNK
    n_chunks = lc // c
    r_i = lax.broadcasted_iota(jnp.int32, (lc, lc), 0)
    c_i = lax.broadcasted_iota(jnp.int32, (lc, lc), 1)
    same_chunk = (r_i // c) == (c_i // c)
    keep = same_chunk & ((c_i >= r_i) if reverse else (c_i <= r_i))
    tri = jnp.where(keep, 1.0, 0.0).astype(BF16)
    lb = lb_ref[...]
    q = _silu(q_ref[...])
    f = lb + (1.0 - lb) / (1.0 + jnp.exp(-f_ref[...]))
    k = 1.0 - f
    logf = jnp.log(f)
    v = v_ref[...].astype(BF16)
    hi = logf.astype(BF16)
    r1 = logf - hi.astype(F32)
    mid = r1.astype(BF16)
    lo = (r1 - mid.astype(F32)).astype(BF16)
    cum = (jnp.dot(tri, hi, preferred_element_type=F32) + jnp.dot(tri, mid, preferred_element_type=F32)
           + jnp.dot(tri, lo, preferred_element_type=F32))
    totals = [cum[ci * c:ci * c + 1, :] if reverse else cum[(ci + 1) * c - 1:(ci + 1) * c, :]
              for ci in range(n_chunks)]
    w = cum.shape[1]
    total = jnp.concatenate([jnp.broadcast_to(t, (c, w)) for t in totals], axis=0)
    centre = jnp.concatenate([jnp.broadcast_to(cum[ci * c + c // 2:ci * c + c // 2 + 1, :], (c, w))
                              for ci in range(n_chunks)], axis=0)
    q_in = (q * jnp.exp(cum)).astype(BF16)
    k_out = (k * jnp.exp(total - cum)).astype(BF16)
    q_loc = (q * jnp.exp(cum - centre)).astype(BF16)
    k_loc = (k * jnp.exp(centre - cum)).astype(BF16)
    decays = [jnp.exp(t) for t in totals]
    order = range(n_chunks - 1, -1, -1) if reverse else range(n_chunks)
    for h in range(HG_HEADS):
        sl = slice(h * HG_DK, (h + 1) * HG_DK)
        att = lax.dot_general(q_loc[:, sl], k_loc[:, sl], (((1,), (1,)), ((), ())),
                              preferred_element_type=F32)
        att = jnp.where(keep, att, 0.0)
        o_loc = jnp.dot(att.astype(BF16), v[:, sl], preferred_element_type=F32)
        incs = [lax.dot_general(v[ci * c:(ci + 1) * c, sl], k_out[ci * c:(ci + 1) * c, sl],
                                (((0,), (0,)), ((), ())), preferred_element_type=F32)
                for ci in range(n_chunks)]
        st = st_ref[h]
        for ci in order:
            rows = slice(ci * c, (ci + 1) * c)
            o = o_loc[rows] + lax.dot_general(q_in[rows, sl], st.astype(BF16),
                                              (((1,), (1,)), ((), ())), preferred_element_type=F32)
            o_ref[rows, sl] = o.astype(o_ref.dtype)
            st = st * decays[ci][:, sl] + incs[ci]
        st_ref[h] = st


def _gla(z, lb, tc):
    b, ttot, _ = z.shape
    w = HG_HEADS * HG_DK
    lc = SEQ_BLOCK
    n_all, n_ctx = ttot // lc, tc // lc

    def col(cidx, reverse):
        return pl.BlockSpec((None, lc, w),
                            lambda bi, s: (bi, _seq_block(s, n_ctx, n_all, reverse), cidx))

    out = jax.ShapeDtypeStruct((b, ttot, w), BF16)
    state = pltpu.VMEM((HG_HEADS, HG_DK, HG_DK), F32)
    return pl.pallas_call(
        functools.partial(_gla_kernel, lc=lc),
        grid=(b, n_all),
        in_specs=[col(1, False), col(2, False), col(4, False),
                  col(1, True), col(3, True), col(4, True),
                  pl.BlockSpec((1, w), lambda bi, s: (0, 0))],
        out_specs=[col(0, False), col(0, True)],
        out_shape=[out, out],
        scratch_shapes=[state, state],
        compiler_params=_cparams(("parallel", "arbitrary")),
        name="hgrn2_bidir",
    )(z, z, z, z, z, z, lb.reshape(1, w))


def _final_kernel(x_ref, g_ref, o_ref):
    x = x_ref[...]
    ms = jnp.mean(x * x, axis=-1, keepdims=True)
    o_ref[...] = x * lax.rsqrt(ms + EPS) * g_ref[...]


def _final_norm(xa, g, tc):
    b, ttot, d = xa.shape
    t = ttot - tc
    tm = SEQ_BLOCK
    off = tc // tm
    return pl.pallas_call(
        _final_kernel,
        grid=(b, t // tm),
        in_specs=[pl.BlockSpec((None, tm, d), lambda bi, i: (bi, i + off, 0)),
                  pl.BlockSpec((1, d), lambda bi, i: (0, 0))],
        out_specs=pl.BlockSpec((None, tm, d), lambda bi, i: (bi, i, 0)),
        out_shape=jax.ShapeDtypeStruct((b, t, d), F32),
        compiler_params=_cparams(("parallel", "parallel")),
        name="final_norm",
    )(xa, g.reshape(1, d))


def _rope_tables(t, tc):
    pos = jnp.arange(t)
    row = (pos // GRID_W).astype(F32)
    col = (pos % GRID_W).astype(F32)
    inv = ROPE_BASE ** (-jnp.arange(ROPE_FREQS, dtype=F32) / ROPE_FREQS)
    ar, ac = row[:, None] * inv, col[:, None] * inv
    cos = jnp.concatenate([jnp.cos(ar), jnp.cos(ar), jnp.cos(ac), jnp.cos(ac)], axis=1)
    sin = jnp.concatenate([-jnp.sin(ar), jnp.sin(ar), -jnp.sin(ac), jnp.sin(ac)], axis=1)
    cos = jnp.concatenate([jnp.ones((tc, HEAD_DIM), F32), cos], axis=0)
    sin = jnp.concatenate([jnp.zeros((tc, HEAD_DIM), F32), sin], axis=0)
    return jnp.tile(cos, (1, LANES // HEAD_DIM)), jnp.tile(sin, (1, LANES // HEAD_DIM))


def _block_diag_dense(w):
    nblk, h, k = w.shape
    return jnp.einsum('nhk,nm->nhmk', w, jnp.eye(nblk, dtype=w.dtype)).reshape(nblk * h, nblk * k)


def kernel(x, c, ctx, c_ctx, w_mod, b_mod, norm_mix, norm_ffn, final_norm, w_in_ab, lru_conv_w, lru_conv_b, lru_wa, lru_ba, lru_wx, lru_bx, lru_lam, attn_sink, w_out_ab, ffn_w1, ffn_w3, ffn_w2, w_in_cd, s5_a_re, s5_a_im, s5_log_step, s5_b_re, s5_b_im, s5_c_re, s5_c_im, s5_d, s5_w_glu, s5_b_glu, hg_lb_raw, hg_norm, w_out_cd, moe_router, moe_w1, moe_w3, moe_w2):
    nb, t, d = x.shape
    tc = ctx.shape[1]
    depth = w_mod.shape[0]
    assert tc % SEQ_BLOCK == 0 and t % SEQ_BLOCK == 0 and t % GRID_W == 0

    xa = jnp.concatenate([ctx, x], axis=1)
    mod_rows = -(-(nb + 1) // SUBLANES) * SUBLANES
    cvec = jnp.zeros((mod_rows, d), F32).at[:nb].set(c).at[nb].set(c_ctx)
    mod = _modulation(cvec, w_mod, b_mod).reshape(depth, mod_rows, N_MOD, d)

    cos_t, sin_t = _rope_tables(t, tc)
    lb_soft = jax.nn.softmax(hg_lb_raw.astype(F32), axis=0)
    lb_table = jnp.cumsum(lb_soft, axis=0) - lb_soft[0:1]
    n_exp = moe_router.shape[2]

    for l in range(depth):
        j = l // 2
        mod_l = mod[l]
        if l % 2 == 0:
            z = _inproj(xa, mod_l, norm_mix[l], w_in_ab[j].astype(BF16), tc, nb)
            sp = jax.nn.softplus(-lru_lam[j])
            lru = []
            for dr, rev in enumerate((False, True)):
                lru.append(_lru(z, lru_conv_w[j], lru_conv_b[j],
                                _block_diag_dense(lru_wa[j, dr]).astype(BF16), lru_ba[j, dr],
                                _block_diag_dense(lru_wx[j, dr]).astype(BF16), lru_bx[j, dr],
                                sp[dr], tc, rev))
            sink = jnp.zeros((1, LANES), F32).at[0, :ATT_HEADS].set(attn_sink[j])
            att = _attention(z, cos_t, sin_t, sink, tc)
            xa = _outproj_ab(xa, mod_l, lru[0], lru[1], z, att, w_out_ab[j].astype(BF16), tc, nb)
            xa = _ffn(xa, mod_l, norm_ffn[l], ffn_w1[j].astype(BF16), ffn_w3[j].astype(BF16),
                      ffn_w2[j].astype(BF16), tc, nb)
        else:
            z = _inproj(xa, mod_l, norm_mix[l], w_in_cd[j].astype(BF16), tc, nb)
            s5w = _s5_weights(s5_a_re[j], s5_a_im[j], s5_log_step[j], s5_b_re[j], s5_b_im[j],
                              s5_c_re[j], s5_c_im[j])
            y5 = _s5(z, s5w, s5_d[j], tc, s5_d.shape[1])
            o_f, o_b = _gla(z, lb_table[j], tc)
            xa = _outproj_cd(xa, mod_l, y5, o_f, o_b, z, s5_w_glu[j].astype(BF16), s5_b_glu[j],
                             hg_norm[j], w_out_cd[j].astype(BF16), tc, nb)
            router_t = jnp.zeros((SUBLANES, d), F32).at[:n_exp].set(moe_router[j].T)
            xa = _moe(xa, mod_l, norm_ffn[l], router_t, moe_w1[j].astype(BF16),
                      moe_w3[j].astype(BF16), moe_w2[j].astype(BF16), tc, nb)
    return _final_norm(xa, final_norm, tc)
```

```python
import functools
import math

import jax
import jax.numpy as jnp
from jax import lax
from jax.experimental import pallas as pl
from jax.experimental.pallas import tpu as pltpu

F32 = jnp.float32
BF16 = jnp.bfloat16

EPS = 1e-6
GRID_W = 64
LRU_BLOCKS = 8
LRU_C = 8.0
CONV_W = 4
ATT_HEADS = 8
ATT_KV_HEADS = 2
ATT_GROUP = ATT_HEADS // ATT_KV_HEADS
HEAD_DIM = 64
WINDOW = 128
BLOCK_Q = 128
ROPE_FREQS = HEAD_DIM // 4
ROPE_BASE = 10000.0
S5_GROUP = 16
S5_STATE = 64
S5_CHUNK = 8
HG_HEADS = 4
HG_DK = 128
HG_CHUNK = 64
SEQ_BLOCK = 256
N_MOD = 6
MOE_PASS_ROWS = 256
LANES = 128
SUBLANES = 8
VMEM_LIMIT = 56 * 1024 * 1024
NEG = -1e30
LOG2E = math.log2(math.e)


def _cparams(sem):
    return pltpu.CompilerParams(dimension_semantics=sem, vmem_limit_bytes=VMEM_LIMIT)


def _row_tile(ttot):
    for tm in (768, 1024, 512, 256):
        if ttot % tm == 0:
            return tm
    raise ValueError(f"unsupported token count {ttot}")


def _sigmoid(x):
    return 0.5 * jnp.tanh(0.5 * x) + 0.5


def _silu(x):
    return x * _sigmoid(x)


def _gelu(x):
    return 0.5 * x * (1.0 + jnp.tanh(math.sqrt(2.0 / math.pi) * (x + 0.044715 * (x * x * x))))


def _norm_mod(x, g, mb, mc, row0, tc, shift_idx, scale_idx):
    ms = jnp.mean(x * x, axis=-1, keepdims=True)
    y = x * lax.rsqrt(ms + EPS) * g
    rows = row0 + lax.broadcasted_iota(jnp.int32, (x.shape[0], 1), 0)
    is_ctx = rows < tc
    scale = jnp.where(is_ctx, mc[scale_idx:scale_idx + 1], mb[scale_idx:scale_idx + 1])
    shift = jnp.where(is_ctx, mc[shift_idx:shift_idx + 1], mb[shift_idx:shift_idx + 1])
    return y * (1.0 + scale) + shift


def _gate_mod(mb, mc, idx, row0, tc, n):
    rows = row0 + lax.broadcasted_iota(jnp.int32, (n, 1), 0)
    return jnp.where(rows < tc, mc[idx:idx + 1], mb[idx:idx + 1])


def _mod_kernel(c_ref, w_ref, b_ref, o_ref):
    s = _silu(c_ref[...])
    o_ref[...] = jnp.dot(s, w_ref[...], preferred_element_type=F32,
                         precision=lax.Precision.HIGHEST) + b_ref[...]


def _modulation(cvec, w_mod, b_mod):
    depth, d, n = w_mod.shape
    tn = 1536 if n % 1536 == 0 else n
    rows = cvec.shape[0]
    return pl.pallas_call(
        _mod_kernel,
        grid=(depth, n // tn),
        in_specs=[pl.BlockSpec((rows, d), lambda l, j: (0, 0)),
                  pl.BlockSpec((None, d, tn), lambda l, j: (l, 0, j)),
                  pl.BlockSpec((None, 1, tn), lambda l, j: (l, 0, j))],
        out_specs=pl.BlockSpec((None, rows, tn), lambda l, j: (l, 0, j)),
        out_shape=jax.ShapeDtypeStruct((depth, rows, n), F32),
        compiler_params=_cparams(("arbitrary", "arbitrary")),
        name="modulation",
    )(cvec, w_mod, b_mod.reshape(depth, 1, n))


def _inproj_kernel(x_ref, mb_ref, mc_ref, g_ref, w_ref, o_ref, *, tm, tc):
    i = pl.program_id(1)
    h = _norm_mod(x_ref[...], g_ref[...], mb_ref[...], mc_ref[...], i * tm, tc, 0, 1)
    o_ref[...] = jnp.dot(h.astype(BF16), w_ref[...], preferred_element_type=F32)


def _inproj(xa, mod_l, g, w, tc, nb):
    b, ttot, d = xa.shape
    n = w.shape[1]
    tm = _row_tile(ttot)
    return pl.pallas_call(
        functools.partial(_inproj_kernel, tm=tm, tc=tc),
        grid=(b, ttot // tm),
        in_specs=[pl.BlockSpec((None, tm, d), lambda bi, i: (bi, i, 0)),
                  pl.BlockSpec((None, N_MOD, d), lambda bi, i: (bi, 0, 0)),
                  pl.BlockSpec((None, N_MOD, d), lambda bi, i: (nb, 0, 0)),
                  pl.BlockSpec((1, d), lambda bi, i: (0, 0)),
                  pl.BlockSpec((d, n), lambda bi, i: (0, 0))],
        out_specs=pl.BlockSpec((None, tm, n), lambda bi, i: (bi, i, 0)),
        out_shape=jax.ShapeDtypeStruct((b, ttot, n), F32),
        compiler_params=_cparams(("parallel", "parallel")),
        name="inproj",
    )(xa, mod_l, mod_l, g.reshape(1, d), w)


def _seq_block(step, n_ctx, n_all, reverse):
    if not reverse:
        return step
    return jnp.where(step < n_ctx, n_ctx - 1 - step, n_all - 1 - step + n_ctx)


def _lru_kernel(*refs, lc, n_ctx, n_all):
    cw_ref, cb_ref = refs[:2]
    n_in = 8
    ins = [refs[2 + d * n_in:2 + (d + 1) * n_in] for d in range(2)]
    outs = refs[2 + 2 * n_in:4 + 2 * n_in]
    scratch = refs[4 + 2 * n_in:]
    for d in range(2):
        _lru_block(*ins[d], cw_ref, cb_ref, outs[d], *scratch[4 * d:4 * d + 4],
                   lc=lc, n_ctx=n_ctx, n_all=n_all, reverse=bool(d))


def _lru_block(u_ref, hp_ref, hn_ref, wa_ref, ba_ref, wx_ref, bx_ref, sp_ref, cw_ref, cb_ref,
               o_ref, uc_ref, a_ref, b_ref, h_ref, *, lc, n_ctx, n_all, reverse):
    step = pl.program_id(1)
    blk = _seq_block(step, n_ctx, n_all, reverse)
    has_prev = jnp.logical_and(blk != 0, blk != n_ctx)
    has_next = jnp.logical_and(blk != n_ctx - 1, blk != n_all - 1)

    @pl.when(step == 0)
    def _():
        h_ref[...] = jnp.zeros_like(h_ref)

    w0, w1, w2, w3 = (cw_ref[j:j + 1, :] for j in range(CONV_W))
    cb = cb_ref[...]
    u = u_ref[...]
    uc_ref[...] = (cb + pltpu.roll(u, 2, 0) * w0 + pltpu.roll(u, 1, 0) * w1 + u * w2
                   + pltpu.roll(u, lc - 1, 0) * w3)
    row = lax.broadcasted_iota(jnp.int32, (SUBLANES, 1), 0)
    hp = jnp.where(has_prev, hp_ref[...], 0.0)
    hn = jnp.where(has_next, hn_ref[...], 0.0)
    u0 = u_ref[0:SUBLANES, :]
    u1 = u_ref[SUBLANES:2 * SUBLANES, :]
    uc_ref[0:SUBLANES, :] = (
        cb + jnp.where(row < 2, pltpu.roll(hp, 2, 0), pltpu.roll(u0, 2, 0)) * w0
        + jnp.where(row < 1, pltpu.roll(hp, 1, 0), pltpu.roll(u0, 1, 0)) * w1 + u0 * w2
        + jnp.where(row < SUBLANES - 1, pltpu.roll(u0, SUBLANES - 1, 0),
                    pltpu.roll(u1, SUBLANES - 1, 0)) * w3)
    ul = u_ref[lc - SUBLANES:lc, :]
    um = u_ref[lc - 2 * SUBLANES:lc - SUBLANES, :]
    uc_ref[lc - SUBLANES:lc, :] = (
        cb + jnp.where(row < 2, pltpu.roll(um, 2, 0), pltpu.roll(ul, 2, 0)) * w0
        + jnp.where(row < 1, pltpu.roll(um, 1, 0), pltpu.roll(ul, 1, 0)) * w1 + ul * w2
        + jnp.where(row < SUBLANES - 1, pltpu.roll(ul, SUBLANES - 1, 0),
                    pltpu.roll(hn, SUBLANES - 1, 0)) * w3)

    uc = uc_ref[...]
    ub = uc.astype(BF16)
    r = _sigmoid(jnp.dot(ub, wa_ref[...], preferred_element_type=F32) + ba_ref[...])
    gi = _sigmoid(jnp.dot(ub, wx_ref[...], preferred_element_type=F32) + bx_ref[...])
    log_a = -LRU_C * r * sp_ref[...]
    a = jnp.exp(log_a)
    b = jnp.sqrt(1.0 - a * a) * (gi * uc)
    sub = lax.broadcasted_iota(jnp.int32, (lc, 1), 0) % SUBLANES
    dist = 1
    while dist < SUBLANES:
        inside = (sub < SUBLANES - dist) if reverse else (sub >= dist)
        shift = (lc - dist) if reverse else dist
        a_sh = jnp.where(inside, pltpu.roll(a, shift, 0), 1.0)
        b_sh = jnp.where(inside, pltpu.roll(b, shift, 0), 0.0)
        b = a * b_sh + b
        a = a * a_sh
        dist *= 2
    a_ref[...] = a
    b_ref[...] = b
    n_grp = lc // SUBLANES

    def body(i, carry):
        t = (n_grp - 1 - i) if reverse else i
        rows = pl.ds(pl.multiple_of(t * SUBLANES, SUBLANES), SUBLANES)
        h = a_ref[rows, :] * carry + b_ref[rows, :]
        b_ref[rows, :] = h
        return h[0:1, :] if reverse else h[SUBLANES - 1:SUBLANES, :]

    h_ref[0:1, :] = lax.fori_loop(0, n_grp, body, h_ref[0:1, :], unroll=4)
    o_ref[...] = b_ref[...].astype(o_ref.dtype)


def _lru(z, conv_w, conv_b, wa_bd, ba, wx_bd, bx, sp, tc):
    b, ttot, _ = z.shape
    w = conv_w.shape[1]
    lc = SEQ_BLOCK
    n_all, n_ctx = ttot // lc, tc // lc
    hb = lc // SUBLANES
    n_h = ttot // SUBLANES
    col = 1

    def direction(d):
        def blk(s):
            return _seq_block(s, n_ctx, n_all, bool(d))

        vec = pl.BlockSpec((None, 1, w), lambda bi, s: (d, 0, 0))
        mat = pl.BlockSpec((None, w, w), lambda bi, s: (d, 0, 0))
        specs = [pl.BlockSpec((None, lc, w), lambda bi, s: (bi, blk(s), col)),
                 pl.BlockSpec((None, SUBLANES, w),
                              lambda bi, s: (bi, jnp.maximum(blk(s) * hb - 1, 0), col)),
                 pl.BlockSpec((None, SUBLANES, w),
                              lambda bi, s: (bi, jnp.minimum((blk(s) + 1) * hb, n_h - 1), col)),
                 mat, vec, mat, vec, vec]
        return specs, pl.BlockSpec((None, lc, w), lambda bi, s: (bi, blk(s), 0))

    (in_f, out_f), (in_r, out_r) = direction(0), direction(1)
    out = jax.ShapeDtypeStruct((b, ttot, w), BF16)
    per_dir = [pltpu.VMEM((lc, w), F32)] * 3 + [pltpu.VMEM((SUBLANES, w), F32)]
    vecs = lambda x: x.reshape(2, 1, w)
    args = (z, z, z, wa_bd, vecs(ba), wx_bd, vecs(bx), vecs(sp))
    return pl.pallas_call(
        functools.partial(_lru_kernel, lc=lc, n_ctx=n_ctx, n_all=n_all),
        grid=(b, n_all),
        in_specs=[pl.BlockSpec((CONV_W, w), lambda bi, s: (0, 0)),
                  pl.BlockSpec((1, w), lambda bi, s: (0, 0))] + in_f + in_r,
        out_specs=[out_f, out_r],
        out_shape=[out, out],
        scratch_shapes=per_dir + per_dir,
        compiler_params=_cparams(("parallel", "arbitrary")),
        name="rglru_bidir",
    )(conv_w, conv_b.reshape(1, w), *args, *args)


def _rope(x, cos, sin_signed):
    lane = lax.broadcasted_iota(jnp.int32, x.shape, 1)
    swapped = jnp.where(lane % 32 < 16, pltpu.roll(x, LANES - 16, 1), pltpu.roll(x, 16, 1))
    return x * cos + swapped * sin_signed


def _pair_layout(x, x_sw, head):
    lo = lax.broadcasted_iota(jnp.int32, x.shape, 1) < HEAD_DIM
    first, second = (x, x_sw) if head == 0 else (x_sw, x)
    return jnp.concatenate([jnp.where(lo, first, 0.0), jnp.where(lo, 0.0, second)],
                           axis=0).astype(BF16)


def _attend(q_block, keys, vals, bias, sink_row, o_ref):
    nk = keys.shape[0]
    keys_sw = pltpu.roll(keys, HEAD_DIM, 1)
    vals_t = vals.T
    top = lax.broadcasted_iota(jnp.int32, (LANES, BLOCK_Q), 0) < HEAD_DIM
    zero_t = jnp.zeros((HEAD_DIM, nk), F32)
    pairs_per_kv = ATT_GROUP // 2
    for kv in range(ATT_KV_HEADS):
        kab = _pair_layout(keys, keys_sw, kv)
        vt = vals_t[kv * HEAD_DIM:(kv + 1) * HEAD_DIM, :]
        vab_t = jnp.concatenate([jnp.concatenate([vt, zero_t], axis=1),
                                 jnp.concatenate([zero_t, vt], axis=1)], axis=0).astype(BF16)
        for g in range(kv * pairs_per_kv, (kv + 1) * pairs_per_kv):
            s_t = lax.dot_general(kab, q_block(g), (((1,), (1,)), ((), ())),
                                  preferred_element_type=F32)
            halves = []
            recips = []
            for half in range(2):
                sh = s_t[half * nk:(half + 1) * nk, :]
                if bias is not None:
                    sh = sh + bias
                sk = sink_row[:, 2 * g + half:2 * g + half + 1]
                m = jnp.maximum(jnp.max(sh, axis=0, keepdims=True), sk)
                p = jnp.exp2(sh - m)
                recips.append(1.0 / (jnp.sum(p, axis=0, keepdims=True) + jnp.exp2(sk - m)))
                halves.append(p.astype(BF16))
            o_t = jnp.dot(vab_t, jnp.concatenate(halves, axis=0), preferred_element_type=F32)
            o_t = o_t * jnp.where(top, recips[0], recips[1])
            o_ref[:, g * LANES:(g + 1) * LANES] = o_t.T.astype(o_ref.dtype)


def _attn_kernel(q_ref, kp_ref, k0_ref, kn_ref, vp_ref, v0_ref, vn_ref, kc_ref, vc_ref,
                 cq_ref, sq_ref, cp_ref, sp_ref, cn_ref, sn_ref, sink_ref, o_ref, *, n_ctx, n_lat):
    i = pl.program_id(1)
    qscale = HEAD_DIM ** -0.5 * LOG2E
    sink_row = sink_ref[...] * LOG2E

    @pl.when(i < n_ctx)
    def _():
        def q_block(g):
            return (q_ref[:, g * LANES:(g + 1) * LANES] * qscale).astype(BF16)
        _attend(q_block, kc_ref[...], vc_ref[...], None, sink_row, o_ref)

    @pl.when(i >= n_ctx)
    def _():
        n = i - n_ctx
        cq, sq = cq_ref[...], sq_ref[...]

        def q_block(g):
            return (_rope(q_ref[:, g * LANES:(g + 1) * LANES], cq, sq) * qscale).astype(BF16)

        kp = _rope(kp_ref[...], cp_ref[...], sp_ref[...])
        k0 = _rope(k0_ref[...], cq, sq)
        kn = _rope(kn_ref[...], cn_ref[...], sn_ref[...])
        keys = jnp.concatenate([kp, k0, kn, kc_ref[...]], axis=0)
        vals = jnp.concatenate([vp_ref[...], v0_ref[...], vn_ref[...], vc_ref[...]], axis=0)
        nk = keys.shape[0]
        qpos = n * BLOCK_Q + lax.broadcasted_iota(jnp.int32, (nk, BLOCK_Q), 1)
        key = lax.broadcasted_iota(jnp.int32, (nk, BLOCK_Q), 0)
        kpos = (n - 1) * BLOCK_Q + key
        local = (jnp.abs(qpos - kpos) <= WINDOW) & (kpos >= 0) & (kpos < n_lat * BLOCK_Q)
        bias = jnp.where(local | (key >= 3 * BLOCK_Q), 0.0, NEG)
        _attend(q_block, keys, vals, bias, sink_row, o_ref)


def _attention(z, cos_t, sin_t, sink, tc):
    b, ttot, _ = z.shape
    n_all, n_ctx = ttot // BLOCK_Q, tc // BLOCK_Q
    n_lat = n_all - n_ctx
    kcol = (2 * 512 + ATT_HEADS * HEAD_DIM) // LANES
    vcol = kcol + 1
    qcol = 2 * 512 // (ATT_HEADS * HEAD_DIM)

    def prev(i):
        return jnp.maximum(i - 1, 0)

    def nxt(i):
        return jnp.minimum(i + 1, n_all - 1)

    def rows(col, f):
        return pl.BlockSpec((None, BLOCK_Q, LANES), lambda bi, i: (bi, f(i), col))

    def tab(f):
        return pl.BlockSpec((BLOCK_Q, LANES), lambda bi, i: (f(i), 0))

    same = lambda i: i
    return pl.pallas_call(
        functools.partial(_attn_kernel, n_ctx=n_ctx, n_lat=n_lat),
        grid=(b, n_all),
        in_specs=[pl.BlockSpec((None, BLOCK_Q, ATT_HEADS * HEAD_DIM), lambda bi, i: (bi, i, qcol)),
                  rows(kcol, prev), rows(kcol, same), rows(kcol, nxt),
                  rows(vcol, prev), rows(vcol, same), rows(vcol, nxt),
                  pl.BlockSpec((None, tc, LANES), lambda bi, i: (bi, 0, kcol)),
                  pl.BlockSpec((None, tc, LANES), lambda bi, i: (bi, 0, vcol)),
                  tab(same), tab(same), tab(prev), tab(prev), tab(nxt), tab(nxt),
                  pl.BlockSpec((1, LANES), lambda bi, i: (0, 0))],
        out_specs=pl.BlockSpec((None, BLOCK_Q, ATT_HEADS * HEAD_DIM), lambda bi, i: (bi, i, 0)),
        out_shape=jax.ShapeDtypeStruct((b, ttot, ATT_HEADS * HEAD_DIM), BF16),
        compiler_params=_cparams(("parallel", "parallel")),
        name="window_gqa",
    )(z, z, z, z, z, z, z, z, z, cos_t, sin_t, cos_t, sin_t, cos_t, sin_t, sink)


def _outproj_ab_kernel(x_ref, mb_ref, mc_ref, lf_ref, lb_ref, g_ref, at_ref, w_ref, o_ref, *, tm, tc):
    i = pl.program_id(1)
    lru = (lf_ref[...].astype(F32) + lb_ref[...].astype(F32)) * _gelu(g_ref[...])
    mix = jnp.concatenate([lru.astype(BF16), at_ref[...]], axis=1)
    dx = jnp.dot(mix, w_ref[...], preferred_element_type=F32)
    gate = _gate_mod(mb_ref[...], mc_ref[...], 2, i * tm, tc, tm)
    o_ref[...] = x_ref[...] + gate * dx


def _outproj_ab(xa, mod_l, lru_f, lru_b, z, att, w_out, tc, nb):
    b, ttot, d = xa.shape
    w = lru_f.shape[2]
    tm = _row_tile(ttot)
    tile = lambda width, col: pl.BlockSpec((None, tm, width), lambda bi, i: (bi, i, col))
    return pl.pallas_call(
        functools.partial(_outproj_ab_kernel, tm=tm, tc=tc),
        grid=(b, ttot // tm),
        in_specs=[tile(d, 0),
                  pl.BlockSpec((None, N_MOD, d), lambda bi, i: (bi, 0, 0)),
                  pl.BlockSpec((None, N_MOD, d), lambda bi, i: (nb, 0, 0)),
                  tile(w, 0), tile(w, 0), tile(w, 0), tile(att.shape[2], 0),
                  pl.BlockSpec(w_out.shape, lambda bi, i: (0, 0))],
        out_specs=tile(d, 0),
        out_shape=jax.ShapeDtypeStruct(xa.shape, F32),
        compiler_params=_cparams(("parallel", "parallel")),
        name="outproj_ab",
    )(xa, mod_l, mod_l, lru_f, lru_b, z, att, w_out)


def _outproj_cd_kernel(x_ref, mb_ref, mc_ref, y_ref, of_ref, ob_ref, g_ref, wg_ref, bg_ref, gn_ref,
                       w_ref, o_ref, *, tm, tc):
    i = pl.program_id(1)
    y = _gelu(y_ref[...])
    s5 = y * _sigmoid(jnp.dot(y.astype(BF16), wg_ref[...], preferred_element_type=F32) + bg_ref[...])
    o = of_ref[...].astype(F32) + ob_ref[...].astype(F32)
    parts = []
    for h in range(HG_HEADS):
        oh = o[:, h * HG_DK:(h + 1) * HG_DK]
        ms = jnp.mean(oh * oh, axis=-1, keepdims=True)
        parts.append(oh * lax.rsqrt(ms + EPS) * gn_ref[...])
    hg = jnp.concatenate(parts, axis=1) * _silu(g_ref[...])
    mix = jnp.concatenate([s5.astype(BF16), hg.astype(BF16)], axis=1)
    dx = jnp.dot(mix, w_ref[...], preferred_element_type=F32)
    gate = _gate_mod(mb_ref[...], mc_ref[...], 2, i * tm, tc, tm)
    o_ref[...] = x_ref[...] + gate * dx


def _outproj_cd(xa, mod_l, y5, o_f, o_b, z, w_glu, b_glu, hg_norm, w_out, tc, nb):
    b, ttot, d = xa.shape
    w = y5.shape[2]
    tm = _row_tile(ttot)
    gcol = z.shape[2] // w - 1
    tile = lambda width, col: pl.BlockSpec((None, tm, width), lambda bi, i: (bi, i, col))
    return pl.pallas_call(
        functools.partial(_outproj_cd_kernel, tm=tm, tc=tc),
        grid=(b, ttot // tm),
        in_specs=[tile(d, 0),
                  pl.BlockSpec((None, N_MOD, d), lambda bi, i: (bi, 0, 0)),
                  pl.BlockSpec((None, N_MOD, d), lambda bi, i: (nb, 0, 0)),
                  tile(w, 0), tile(w, 0), tile(w, 0), tile(w, gcol),
                  pl.BlockSpec(w_glu.shape, lambda bi, i: (0, 0)),
                  pl.BlockSpec((1, w), lambda bi, i: (0, 0)),
                  pl.BlockSpec((1, HG_DK), lambda bi, i: (0, 0)),
                  pl.BlockSpec(w_out.shape, lambda bi, i: (0, 0))],
        out_specs=tile(d, 0),
        out_shape=jax.ShapeDtypeStruct(xa.shape, F32),
        compiler_params=_cparams(("parallel", "parallel")),
        name="outproj_cd",
    )(xa, mod_l, mod_l, y5, o_f, o_b, z, w_glu, b_glu.reshape(1, w), hg_norm.reshape(1, HG_DK), w_out)


def _ffn_kernel(x_ref, mb_ref, mc_ref, g_ref, w1_ref, w3_ref, w2_ref, o_ref, h_ref, acc_ref, *, tm, tc):
    i = pl.program_id(1)
    k = pl.program_id(2)

    @pl.when(k == 0)
    def _():
        h = _norm_mod(x_ref[...], g_ref[...], mb_ref[...], mc_ref[...], i * tm, tc, 3, 4)
        h_ref[...] = h.astype(BF16)
        acc_ref[...] = jnp.zeros_like(acc_ref)

    h = h_ref[...]
    a = jnp.dot(h, w1_ref[...], preferred_element_type=F32)
    c = jnp.dot(h, w3_ref[...], preferred_element_type=F32)
    acc_ref[...] += jnp.dot((_silu(a) * c).astype(BF16), w2_ref[...], preferred_element_type=F32)

    @pl.when(k == pl.num_programs(2) - 1)
    def _():
        gate = _gate_mod(mb_ref[...], mc_ref[...], 5, i * tm, tc, tm)
        o_ref[...] = x_ref[...] + gate * acc_ref[...]


def _ffn(xa, mod_l, g, w1, w3, w2, tc, nb):
    b, ttot, d = xa.shape
    f = w1.shape[1]
    tm = _row_tile(ttot)
    tf = f // 2 if (f // 2) % LANES == 0 else f
    return pl.pallas_call(
        functools.partial(_ffn_kernel, tm=tm, tc=tc),
        grid=(b, ttot // tm, f // tf),
        in_specs=[pl.BlockSpec((None, tm, d), lambda bi, i, k: (bi, i, 0)),
                  pl.BlockSpec((None, N_MOD, d), lambda bi, i, k: (bi, 0, 0)),
                  pl.BlockSpec((None, N_MOD, d), lambda bi, i, k: (nb, 0, 0)),
                  pl.BlockSpec((1, d), lambda bi, i, k: (0, 0)),
                  pl.BlockSpec((d, tf), lambda bi, i, k: (0, k)),
                  pl.BlockSpec((d, tf), lambda bi, i, k: (0, k)),
                  pl.BlockSpec((tf, d), lambda bi, i, k: (k, 0))],
        out_specs=pl.BlockSpec((None, tm, d), lambda bi, i, k: (bi, i, 0)),
        out_shape=jax.ShapeDtypeStruct(xa.shape, F32),
        scratch_shapes=[pltpu.VMEM((tm, d), BF16), pltpu.VMEM((tm, d), F32)],
        compiler_params=_cparams(("parallel", "parallel", "arbitrary")),
        name="ffn_swiglu",
    )(xa, mod_l, mod_l, g.reshape(1, d), w1, w3, w2)


def _router_kernel(x_ref, mb_ref, mc_ref, g_ref, r_ref, h_ref, wt_ref, cnt_ref, *, tm, tc, n_exp):
    i = pl.program_id(1)
    h = _norm_mod(x_ref[...], g_ref[...], mb_ref[...], mc_ref[...], i * tm, tc, 3, 4)
    h_ref[...] = h.astype(BF16)
    logits = lax.dot_general(r_ref[...], h, (((1,), (1,)), ((), ())), preferred_element_type=F32,
                             precision=lax.Precision.HIGHEST)
    sub = lax.broadcasted_iota(jnp.int32, logits.shape, 0)
    logits = jnp.where(sub < n_exp, logits, NEG)
    m1 = jnp.max(logits, axis=0, keepdims=True)
    i1 = jnp.min(jnp.where(logits == m1, sub, SUBLANES), axis=0, keepdims=True)
    rest = jnp.where(sub == i1, NEG, logits)
    m2 = jnp.max(rest, axis=0, keepdims=True)
    i2 = jnp.min(jnp.where(rest == m2, sub, SUBLANES), axis=0, keepdims=True)
    e2 = jnp.exp(m2 - m1)
    g1 = 1.0 / (1.0 + e2)
    wt = jnp.where(sub == i1, g1, 0.0) + jnp.where(sub == i2, e2 * g1, 0.0)
    wt_ref[...] = wt
    cnt = jnp.sum(jnp.where(wt > 0.0, 1.0, 0.0), axis=1, keepdims=True)
    cnt_ref[...] = jnp.broadcast_to(cnt, cnt_ref.shape).astype(jnp.int32)


def _router(xa, mod_l, g, router_t, tc, nb, n_exp):
    b, ttot, d = xa.shape
    tm = _row_tile(ttot)
    n_t = ttot // tm
    return pl.pallas_call(
        functools.partial(_router_kernel, tm=tm, tc=tc, n_exp=n_exp),
        grid=(b, n_t),
        in_specs=[pl.BlockSpec((None, tm, d), lambda bi, i: (bi, i, 0)),
                  pl.BlockSpec((None, N_MOD, d), lambda bi, i: (bi, 0, 0)),
                  pl.BlockSpec((None, N_MOD, d), lambda bi, i: (nb, 0, 0)),
                  pl.BlockSpec((1, d), lambda bi, i: (0, 0)),
                  pl.BlockSpec((SUBLANES, d), lambda bi, i: (0, 0))],
        out_specs=[pl.BlockSpec((tm, d), lambda bi, i: (bi * n_t + i, 0)),
                   pl.BlockSpec((SUBLANES, tm), lambda bi, i: (0, bi * n_t + i)),
                   pl.BlockSpec((None, SUBLANES, LANES), lambda bi, i: (bi * n_t + i, 0, 0))],
        out_shape=[jax.ShapeDtypeStruct((b * ttot, d), BF16),
                   jax.ShapeDtypeStruct((SUBLANES, b * ttot), F32),
                   jax.ShapeDtypeStruct((b * n_t, SUBLANES, LANES), jnp.int32)],
        compiler_params=_cparams(("parallel", "parallel")),
        name="moe_router",
    )(xa, mod_l, mod_l, g.reshape(1, d), router_t)


def _moe_kernel(cnt_ref, x_ref, *refs, tm, tc, n_exp, n_t, group):
    mb_refs = refs[:group]
    mc_ref, h_ref, wt_ref, w1_ref, w3_ref, w2_ref, o_ref, rank_ref = refs[group:]
    p = pl.program_id(0)
    e = pl.program_id(1)
    f = w1_ref.shape[1]
    half = (f // LANES + 1) // 2 * LANES
    f_split = [(0, half), (half, f)] if 0 < half < f else [(0, f)]

    @pl.when(e == 0)
    def _():
        o_ref[...] = jnp.zeros_like(o_ref)
        r_i = lax.broadcasted_iota(jnp.int32, (tm, tm), 0)
        c_i = lax.broadcasted_iota(jnp.int32, (tm, tm), 1)
        before = jnp.where(r_i < c_i, 1.0, 0.0).astype(BF16)
        for j in range(group):
            sel = wt_ref[:, j * tm:(j + 1) * tm] > 0.0
            rank = jnp.dot(jnp.where(sel, 1.0, 0.0).astype(BF16), before, preferred_element_type=F32)
            rank_ref[j] = jnp.where(sel, rank, -1.0)

    counts = [cnt_ref[(p * group + j) * n_exp + e] for j in range(group)]

    offsets = [jnp.int32(0)]
    for j in range(group - 1):
        offsets.append(offsets[-1] + counts[j])
    total = offsets[-1] + counts[-1]

    def expert_pass(m_rows, base):
        slot = (lax.broadcasted_iota(jnp.int32, (m_rows, tm), 0) + base).astype(F32)
        onehots = []
        xe = None
        gate = None
        for j in range(group):
            r_row = rank_ref[j, pl.ds(e, 1), :]
            r_row = jnp.where(r_row >= 0.0, r_row + offsets[j].astype(F32), -1.0)
            w_row = wt_ref[pl.ds(e, 1), j * tm:(j + 1) * tm]
            hit = r_row == slot
            onehot = jnp.where(hit, 1.0, 0.0).astype(BF16)
            onehots.append(onehot)
            xj = jnp.dot(onehot, h_ref[j * tm:(j + 1) * tm, :], preferred_element_type=F32)
            gj = jnp.sum(jnp.where(hit, w_row, 0.0), axis=1, keepdims=True)
            xe = xj if xe is None else xe + xj
            gate = gj if gate is None else gate + gj
        xe = xe.astype(BF16)
        y = None
        for f0, f1 in f_split:
            a = jnp.dot(xe, w1_ref[:, f0:f1], preferred_element_type=F32)
            c = jnp.dot(xe, w3_ref[:, f0:f1], preferred_element_type=F32)
            part = jnp.dot((_silu(a) * c * gate).astype(BF16), w2_ref[f0:f1, :],
                           preferred_element_type=F32)
            y = part if y is None else y + part
        y = y.astype(BF16)
        for j in range(group):
            o_ref[j * tm:(j + 1) * tm, :] += lax.dot_general(
                onehots[j], y, (((0,), (0,)), ((), ())), preferred_element_type=F32)

    n_full = total // MOE_PASS_ROWS

    def full_pass(it, carry):
        expert_pass(MOE_PASS_ROWS, it * MOE_PASS_ROWS)
        return carry

    lax.fori_loop(0, n_full, full_pass, 0)
    rest = total - n_full * MOE_PASS_ROWS

    @pl.when(rest > MOE_PASS_ROWS // 2)
    def _():
        expert_pass(MOE_PASS_ROWS, n_full * MOE_PASS_ROWS)

    @pl.when(jnp.logical_and(rest > 0, rest <= MOE_PASS_ROWS // 2))
    def _():
        expert_pass(MOE_PASS_ROWS // 2, n_full * MOE_PASS_ROWS)

    @pl.when(e == n_exp - 1)
    def _():
        for j in range(group):
            i = (p * group + j) % n_t
            gate5 = _gate_mod(mb_refs[j][...], mc_ref[...], 5, i * tm, tc, tm)
            rows = slice(j * tm, (j + 1) * tm)
            o_ref[rows, :] = x_ref[rows, :] + gate5 * o_ref[rows, :]


def _moe(xa, mod_l, g, router_t, w1, w3, w2, tc, nb):
    b, ttot, d = xa.shape
    n_exp, _, f = w1.shape
    tm = _row_tile(ttot)
    n_t = ttot // tm
    group = 2 if (b * n_t) % 2 == 0 else 1
    h, wt, cnt = _router(xa, mod_l, g, router_t, tc, nb, n_exp)
    cnt = cnt[:, :n_exp, 0].reshape(-1)
    rows = group * tm
    mods = [pl.BlockSpec((None, N_MOD, d), functools.partial(
        lambda p, e, c, j: ((p * group + j) // n_t, 0, 0), j=j)) for j in range(group)]
    grid_spec = pltpu.PrefetchScalarGridSpec(
        num_scalar_prefetch=1,
        grid=(b * n_t // group, n_exp),
        in_specs=[pl.BlockSpec((rows, d), lambda p, e, c: (p, 0), pipeline_mode=pl.Buffered(1))]
        + mods
        + [pl.BlockSpec((None, N_MOD, d), lambda p, e, c: (nb, 0, 0)),
           pl.BlockSpec((rows, d), lambda p, e, c: (p, 0)),
           pl.BlockSpec((SUBLANES, rows), lambda p, e, c: (0, p)),
           pl.BlockSpec((None, d, f), lambda p, e, c: (e, 0, 0)),
           pl.BlockSpec((None, d, f), lambda p, e, c: (e, 0, 0)),
           pl.BlockSpec((None, f, d), lambda p, e, c: (e, 0, 0))],
        out_specs=pl.BlockSpec((rows, d), lambda p, e, c: (p, 0)),
        scratch_shapes=[pltpu.VMEM((group, SUBLANES, tm), F32)])
    out = pl.pallas_call(
        functools.partial(_moe_kernel, tm=tm, tc=tc, n_exp=n_exp, n_t=n_t, group=group),
        grid_spec=grid_spec,
        out_shape=jax.ShapeDtypeStruct((b * ttot, d), F32),
        compiler_params=_cparams(("parallel", "arbitrary")),
        name="moe_experts",
    )(cnt, xa.reshape(b * ttot, d), *([mod_l] * group), mod_l, h, wt, w1, w3, w2)
    return out.reshape(b, ttot, d)


def _spread_groups(x, inner, row_group, n_grp):
    cols_in = x.shape[2]
    cols_out = cols_in * n_grp
    q = jnp.arange(cols_out)
    src = (q // (n_grp * inner)) * inner + q % inner
    tile = (jnp.arange(cols_in)[:, None] == src[None, :]).astype(BF16)
    col_group = (q // inner) % n_grp
    y = jnp.dot(x.astype(BF16), tile)
    return jnp.where(row_group[:, None] == col_group[None, :], y, jnp.zeros((), BF16))


def _s5_weights(a_re, a_im, log_step, b_re, b_im, c_re, c_im):
    L = S5_CHUNK
    lr = jnp.minimum(a_re, -1e-4)
    li = a_im
    dt = jnp.exp(log_step)[..., None]
    mag, ang = lr * dt, li * dt
    lbr, lbi = jnp.exp(mag) * jnp.cos(ang), jnp.exp(mag) * jnp.sin(ang)
    zr, zi = lbr - 1.0, lbi
    den = lr * lr + li * li
    fr = (zr * lr + zi * li) / den
    fi = (zi * lr - zr * li) / den
    bbr = fr[..., None] * b_re - fi[..., None] * b_im
    bbi = fr[..., None] * b_im + fi[..., None] * b_re

    def power(p):
        p = p[..., None, None, None].astype(F32)
        return jnp.exp(mag * p) * jnp.cos(ang * p), jnp.exp(mag * p) * jnp.sin(ang * p)

    n_dir, n_grp, n_st = a_re.shape
    gpb = LANES // S5_GROUP
    n_blk = n_grp // gpb
    s = jnp.arange(L)
    in_group = (jnp.arange(L * LANES) % LANES) // S5_GROUP
    st_group = jnp.arange(gpb * n_st) // n_st
    outs = []
    for d in range(n_dir):
        pr, pi = power(jnp.arange(L))
        pr, pi = pr[:, d], pi[:, d]
        cbr = (jnp.einsum('gcn,tgn,gnk->tgck', c_re[d], pr, bbr[d])
               - jnp.einsum('gcn,tgn,gnk->tgck', c_re[d], pi, bbi[d])
               - jnp.einsum('gcn,tgn,gnk->tgck', c_im[d], pr, bbi[d])
               - jnp.einsum('gcn,tgn,gnk->tgck', c_im[d], pi, bbr[d]))
        lag = (s[None, :] - s[:, None]) if d == 0 else (s[:, None] - s[None, :])
        kern = jnp.where((lag >= 0)[..., None, None, None], cbr[jnp.clip(lag, 0, L - 1)], 0.0)
        kern = kern.reshape(L, L, n_blk, gpb, S5_GROUP, S5_GROUP).transpose(2, 0, 3, 5, 1, 4)
        m = _spread_groups(kern.reshape(n_blk, L * LANES, L * S5_GROUP), S5_GROUP, in_group, gpb)
        qr, qi = power((L - 1 - s) if d == 0 else s)
        qr, qi = qr[:, d], qi[:, d]
        str_ = qr[..., None] * bbr[d] - qi[..., None] * bbi[d]
        sti = qr[..., None] * bbi[d] + qi[..., None] * bbr[d]

        def to_state(x):
            x = x.reshape(L, n_blk, gpb, n_st, S5_GROUP).transpose(1, 0, 2, 4, 3)
            return _spread_groups(x.reshape(n_blk, L * LANES, n_st), n_st, in_group, gpb)

        wst = jnp.concatenate([to_state(str_), to_state(sti)], axis=2)
        rr, ri = power((s + 1) if d == 0 else (L - s))
        rr, ri = rr[:, d], ri[:, d]
        wr = c_re[d][None] * rr[:, :, None, :] - c_im[d][None] * ri[:, :, None, :]
        wi = -(c_re[d][None] * ri[:, :, None, :] + c_im[d][None] * rr[:, :, None, :])

        def from_state(x):
            x = x.reshape(L, n_blk, gpb, S5_GROUP, n_st).transpose(1, 2, 4, 0, 3)
            return _spread_groups(x.reshape(n_blk, gpb * n_st, L * S5_GROUP), S5_GROUP, st_group, gpb)

        wout = jnp.concatenate([from_state(wr), from_state(wi)], axis=1)
        ler, lei = jnp.exp(mag[d] * L) * jnp.cos(ang[d] * L), jnp.exp(mag[d] * L) * jnp.sin(ang[d] * L)
        lam_l = jnp.concatenate([ler.reshape(n_blk, 1, gpb * n_st), lei.reshape(n_blk, 1, gpb * n_st)],
                                axis=2)
        outs.append((m, wst, wout, lam_l))
    return tuple(jnp.stack([o[k] for o in outs]) for k in range(4))


def _s5_kernel(u_ref, m_ref, wst_ref, wout_ref, lam_ref, d_ref, o_ref, x_ref, hp_ref, *, rows, rows_ctx):
    L = S5_CHUNK
    dr = pl.program_id(2)
    ns = lam_ref.shape[1] // 2
    u = jnp.concatenate([u_ref[pl.ds(s, rows, stride=L), :] for s in range(L)], axis=1)
    ub = u.astype(BF16)
    x_ref[...] = jnp.dot(ub, wst_ref[...], preferred_element_type=F32)
    lr = lam_ref[:, 0:ns]
    li = lam_ref[:, ns:2 * ns]

    def visit(r, carry):
        hr, hi = carry
        hp_ref[pl.ds(r, 1), 0:ns] = hr
        hp_ref[pl.ds(r, 1), ns:2 * ns] = hi
        xr = x_ref[pl.ds(r, 1), 0:ns]
        xi = x_ref[pl.ds(r, 1), ns:2 * ns]
        return lr * hr - li * hi + xr, lr * hi + li * hr + xi

    zero = (jnp.zeros((1, ns), F32), jnp.zeros((1, ns), F32))

    @pl.when(dr == 0)
    def _():
        lax.fori_loop(0, rows, visit, zero, unroll=4)

    @pl.when(dr == 1)
    def _():
        c = lax.fori_loop(0, rows_ctx, lambda t, c: visit(rows_ctx - 1 - t, c), zero, unroll=4)
        lax.fori_loop(0, rows - rows_ctx, lambda t, c: visit(rows - 1 - t, c), c, unroll=4)

    y = (jnp.dot(ub, m_ref[...], preferred_element_type=F32)
         + jnp.dot(hp_ref[...].astype(BF16), wout_ref[...], preferred_element_type=F32))

    @pl.when(dr == 0)
    def _():
        for s in range(L):
            o_ref[pl.ds(s, rows, stride=L), :] = (y[:, s * LANES:(s + 1) * LANES]
                                                  + d_ref[...] * u[:, s * LANES:(s + 1) * LANES])

    @pl.when(dr == 1)
    def _():
        for s in range(L):
            o_ref[pl.ds(s, rows, stride=L), :] += y[:, s * LANES:(s + 1) * LANES]


def _s5(z, weights, dvec, tc, width):
    b, ttot, _ = z.shape
    m, wst, wout, lam_l = weights
    n_dir, n_blk = m.shape[0], m.shape[1]
    rows, rows_ctx = ttot // S5_CHUNK, tc // S5_CHUNK
    wspec = lambda a: pl.BlockSpec((None, None) + a.shape[2:], lambda bi, j, dr: (dr, j, 0, 0))
    return pl.pallas_call(
        functools.partial(_s5_kernel, rows=rows, rows_ctx=rows_ctx),
        grid=(b, n_blk, n_dir),
        in_specs=[pl.BlockSpec((None, ttot, LANES), lambda bi, j, dr: (bi, 0, j)),
                  wspec(m), wspec(wst), wspec(wout), wspec(lam_l),
                  pl.BlockSpec((1, LANES), lambda bi, j, dr: (0, j))],
        out_specs=pl.BlockSpec((None, ttot, LANES), lambda bi, j, dr: (bi, 0, j)),
        out_shape=jax.ShapeDtypeStruct((b, ttot, width), F32),
        scratch_shapes=[pltpu.VMEM((rows, lam_l.shape[3]), F32), pltpu.VMEM((rows, lam_l.shape[3]), F32)],
        compiler_params=_cparams(("parallel", "parallel", "arbitrary")),
        name="s5_bidir",
    )(z, m, wst, wout, lam_l, dvec.reshape(1, width))


def _gla_kernel(qf_ref, ff_ref, vf_ref, qb_ref, fb_ref, vb_ref, lb_ref, of_ref, ob_ref,
                stf_ref, stb_ref, *, lc):
    step = pl.program_id(1)

    @pl.when(step == 0)
    def _():
        stf_ref[...] = jnp.zeros_like(stf_ref)
        stb_ref[...] = jnp.zeros_like(stb_ref)

    _gla_block(qf_ref, ff_ref, vf_ref, lb_ref, of_ref, stf_ref, lc=lc, reverse=False)
    _gla_block(qb_ref, fb_ref, vb_ref, lb_ref, ob_ref, stb_ref, lc=lc, reverse=True)


def _gla_block(q_ref, f_ref, v_ref, lb_ref, o_ref, st_ref, *, lc, reverse):
    c = HG_CHUNK
    n_chunks = lc // c
    r_i = lax.broadcasted_iota(jnp.int32, (lc, lc), 0)
    c_i = lax.broadcasted_iota(jnp.int32, (lc, lc), 1)
    same_chunk = (r_i // c) == (c_i // c)
    keep = same_chunk & ((c_i >= r_i) if reverse else (c_i <= r_i))
    tri = jnp.where(keep, 1.0, 0.0).astype(BF16)
    lb = lb_ref[...]
    q = _silu(q_ref[...])
    f = lb + (1.0 - lb) / (1.0 + jnp.exp(-f_ref[...]))
    k = 1.0 - f
    logf = jnp.log(f)
    v = v_ref[...].astype(BF16)
    hi = logf.astype(BF16)
    r1 = logf - hi.astype(F32)
    mid = r1.astype(BF16)
    lo = (r1 - mid.astype(F32)).astype(BF16)
    cum = (jnp.dot(tri, hi, preferred_element_type=F32) + jnp.dot(tri, mid, preferred_element_type=F32)
           + jnp.dot(tri, lo, preferred_element_type=F32))
    totals = [cum[ci * c:ci * c + 1, :] if reverse else cum[(ci + 1) * c - 1:(ci + 1) * c, :]
              for ci in range(n_chunks)]
    w = cum.shape[1]
    total = jnp.concatenate([jnp.broadcast_to(t, (c, w)) for t in totals], axis=0)
    centre = jnp.concatenate([jnp.broadcast_to(cum[ci * c + c // 2:ci * c + c // 2 + 1, :], (c, w))
                              for ci in range(n_chunks)], axis=0)
    q_in = (q * jnp.exp(cum)).astype(BF16)
    k_out = (k * jnp.exp(total - cum)).astype(BF16)
    q_loc = (q * jnp.exp(cum - centre)).astype(BF16)
    k_loc = (k * jnp.exp(centre - cum)).astype(BF16)
    decays = [jnp.exp(t) for t in totals]
    order = range(n_chunks - 1, -1, -1) if reverse else range(n_chunks)
    for h in range(HG_HEADS):
        sl = slice(h * HG_DK, (h + 1) * HG_DK)
        att = lax.dot_general(q_loc[:, sl], k_loc[:, sl], (((1,), (1,)), ((), ())),
                              preferred_element_type=F32)
        att = jnp.where(keep, att, 0.0)
        o_loc = jnp.dot(att.astype(BF16), v[:, sl], preferred_element_type=F32)
        incs = [lax.dot_general(v[ci * c:(ci + 1) * c, sl], k_out[ci * c:(ci + 1) * c, sl],
                                (((0,), (0,)), ((), ())), preferred_element_type=F32)
                for ci in range(n_chunks)]
        st = st_ref[h]
        for ci in order:
            rows = slice(ci * c, (ci + 1) * c)
            o = o_loc[rows] + lax.dot_general(q_in[rows, sl], st.astype(BF16),
                                              (((1,), (1,)), ((), ())), preferred_element_type=F32)
            o_ref[rows, sl] = o.astype(o_ref.dtype)
            st = st * decays[ci][:, sl] + incs[ci]
        st_ref[h] = st


def _gla(z, lb, tc):
    b, ttot, _ = z.shape
    w = HG_HEADS * HG_DK
    lc = SEQ_BLOCK
    n_all, n_ctx = ttot // lc, tc // lc

    def col(cidx, reverse):
        return pl.BlockSpec((None, lc, w),
                            lambda bi, s: (bi, _seq_block(s, n_ctx, n_all, reverse), cidx))

    out = jax.ShapeDtypeStruct((b, ttot, w), BF16)
    state = pltpu.VMEM((HG_HEADS, HG_DK, HG_DK), F32)
    return pl.pallas_call(
        functools.partial(_gla_kernel, lc=lc),
        grid=(b, n_all),
        in_specs=[col(1, False), col(2, False), col(4, False),
                  col(1, True), col(3, True), col(4, True),
                  pl.BlockSpec((1, w), lambda bi, s: (0, 0))],
        out_specs=[col(0, False), col(0, True)],
        out_shape=[out, out],
        scratch_shapes=[state, state],
        compiler_params=_cparams(("parallel", "arbitrary")),
        name="hgrn2_bidir",
    )(z, z, z, z, z, z, lb.reshape(1, w))


def _final_kernel(x_ref, g_ref, o_ref):
    x = x_ref[...]
    ms = jnp.mean(x * x, axis=-1, keepdims=True)
    o_ref[...] = x * lax.rsqrt(ms + EPS) * g_ref[...]


def _final_norm(xa, g, tc):
    b, ttot, d = xa.shape
    t = ttot - tc
    tm = SEQ_BLOCK
    off = tc // tm
    return pl.pallas_call(
        _final_kernel,
        grid=(b, t // tm),
        in_specs=[pl.BlockSpec((None, tm, d), lambda bi, i: (bi, i + off, 0)),
                  pl.BlockSpec((1, d), lambda bi, i: (0, 0))],
        out_specs=pl.BlockSpec((None, tm, d), lambda bi, i: (bi, i, 0)),
        out_shape=jax.ShapeDtypeStruct((b, t, d), F32),
        compiler_params=_cparams(("parallel", "parallel")),
        name="final_norm",
    )(xa, g.reshape(1, d))


def _rope_tables(t, tc):
    pos = jnp.arange(t)
    row = (pos // GRID_W).astype(F32)
    col = (pos % GRID_W).astype(F32)
    inv = ROPE_BASE ** (-jnp.arange(ROPE_FREQS, dtype=F32) / ROPE_FREQS)
    ar, ac = row[:, None] * inv, col[:, None] * inv
    cos = jnp.concatenate([jnp.cos(ar), jnp.cos(ar), jnp.cos(ac), jnp.cos(ac)], axis=1)
    sin = jnp.concatenate([-jnp.sin(ar), jnp.sin(ar), -jnp.sin(ac), jnp.sin(ac)], axis=1)
    cos = jnp.concatenate([jnp.ones((tc, HEAD_DIM), F32), cos], axis=0)
    sin = jnp.concatenate([jnp.zeros((tc, HEAD_DIM), F32), sin], axis=0)
    return jnp.tile(cos, (1, LANES // HEAD_DIM)), jnp.tile(sin, (1, LANES // HEAD_DIM))


def _block_diag_dense(w):
    nblk, h, k = w.shape
    return jnp.einsum('nhk,nm->nhmk', w, jnp.eye(nblk, dtype=w.dtype)).reshape(nblk * h, nblk * k)


def kernel(x, c, ctx, c_ctx, w_mod, b_mod, norm_mix, norm_ffn, final_norm, w_in_ab, lru_conv_w, lru_conv_b, lru_wa, lru_ba, lru_wx, lru_bx, lru_lam, attn_sink, w_out_ab, ffn_w1, ffn_w3, ffn_w2, w_in_cd, s5_a_re, s5_a_im, s5_log_step, s5_b_re, s5_b_im, s5_c_re, s5_c_im, s5_d, s5_w_glu, s5_b_glu, hg_lb_raw, hg_norm, w_out_cd, moe_router, moe_w1, moe_w3, moe_w2):
    nb, t, d = x.shape
    tc = ctx.shape[1]
    depth = w_mod.shape[0]
    assert tc % SEQ_BLOCK == 0 and t % SEQ_BLOCK == 0 and t % GRID_W == 0

    xa = jnp.concatenate([ctx, x], axis=1)
    mod_rows = -(-(nb + 1) // SUBLANES) * SUBLANES
    cvec = jnp.zeros((mod_rows, d), F32).at[:nb].set(c).at[nb].set(c_ctx)
    mod = _modulation(cvec, w_mod, b_mod).reshape(depth, mod_rows, N_MOD, d)

    cos_t, sin_t = _rope_tables(t, tc)
    lb_soft = jax.nn.softmax(hg_lb_raw.astype(F32), axis=0)
    lb_table = jnp.cumsum(lb_soft, axis=0) - lb_soft[0:1]
    n_exp = moe_router.shape[2]

    for l in range(depth):
        j = l // 2
        mod_l = mod[l]
        if l % 2 == 0:
            z = _inproj(xa, mod_l, norm_mix[l], w_in_ab[j].astype(BF16), tc, nb)
            lru = _lru(z, lru_conv_w[j], lru_conv_b[j],
                       jnp.stack([_block_diag_dense(lru_wa[j, dr]) for dr in range(2)]).astype(BF16),
                       lru_ba[j],
                       jnp.stack([_block_diag_dense(lru_wx[j, dr]) for dr in range(2)]).astype(BF16),
                       lru_bx[j], jax.nn.softplus(-lru_lam[j]), tc)
            sink = jnp.zeros((1, LANES), F32).at[0, :ATT_HEADS].set(attn_sink[j])
            att = _attention(z, cos_t, sin_t, sink, tc)
            xa = _outproj_ab(xa, mod_l, lru[0], lru[1], z, att, w_out_ab[j].astype(BF16), tc, nb)
            xa = _ffn(xa, mod_l, norm_ffn[l], ffn_w1[j].astype(BF16), ffn_w3[j].astype(BF16),
                      ffn_w2[j].astype(BF16), tc, nb)
        else:
            z = _inproj(xa, mod_l, norm_mix[l], w_in_cd[j].astype(BF16), tc, nb)
            s5w = _s5_weights(s5_a_re[j], s5_a_im[j], s5_log_step[j], s5_b_re[j], s5_b_im[j],
                              s5_c_re[j], s5_c_im[j])
            y5 = _s5(z, s5w, s5_d[j], tc, s5_d.shape[1])
            o_f, o_b = _gla(z, lb_table[j], tc)
            xa = _outproj_cd(xa, mod_l, y5, o_f, o_b, z, s5_w_glu[j].astype(BF16), s5_b_glu[j],
                             hg_norm[j], w_out_cd[j].astype(BF16), tc, nb)
            router_t = jnp.zeros((SUBLANES, d), F32).at[:n_exp].set(moe_router[j].T)
            xa = _moe(xa, mod_l, norm_ffn[l], router_t, moe_w1[j].astype(BF16),
                      moe_w3[j].astype(BF16), moe_w2[j].astype(BF16), tc, nb)
    return _final_norm(xa, final_norm, tc)
```

```python
import functools
import math

import jax
import jax.numpy as jnp
from jax import lax
from jax.experimental import pallas as pl
from jax.experimental.pallas import tpu as pltpu

F32 = jnp.float32
BF16 = jnp.bfloat16

EPS = 1e-6
GRID_W = 64
LRU_BLOCKS = 8
LRU_C = 8.0
CONV_W = 4
ATT_HEADS = 8
ATT_KV_HEADS = 2
ATT_GROUP = ATT_HEADS // ATT_KV_HEADS
HEAD_DIM = 64
WINDOW = 128
BLOCK_Q = 128
ROPE_FREQS = HEAD_DIM // 4
ROPE_BASE = 10000.0
S5_GROUP = 16
S5_STATE = 64
S5_CHUNK = 8
HG_HEADS = 4
HG_DK = 128
HG_CHUNK = 64
SEQ_BLOCK = 256
N_MOD = 6
MOE_PASS_ROWS = 256
LANES = 128
SUBLANES = 8
VMEM_LIMIT = 56 * 1024 * 1024
NEG = -1e30
LOG2E = math.log2(math.e)


def _cparams(sem):
    return pltpu.CompilerParams(dimension_semantics=sem, vmem_limit_bytes=VMEM_LIMIT)


def _row_tile(ttot):
    for tm in (768, 1024, 512, 256):
        if ttot % tm == 0:
            return tm
    raise ValueError(f"unsupported token count {ttot}")


def _tall_row_tile(ttot):
    for tm in range(min(ttot, 1088) // 16 * 16, 15, -16):
        if ttot % tm == 0:
            return tm
    raise ValueError(f"unsupported token count {ttot}")


def _sigmoid(x):
    return 0.5 * jnp.tanh(0.5 * x) + 0.5


def _silu(x):
    return x * _sigmoid(x)


def _gelu(x):
    return 0.5 * x * (1.0 + jnp.tanh(math.sqrt(2.0 / math.pi) * (x + 0.044715 * (x * x * x))))


def _norm_mod(x, g, mb, mc, row0, tc, shift_idx, scale_idx):
    ms = jnp.mean(x * x, axis=-1, keepdims=True)
    y = x * lax.rsqrt(ms + EPS) * g
    rows = row0 + lax.broadcasted_iota(jnp.int32, (x.shape[0], 1), 0)
    is_ctx = rows < tc
    scale = jnp.where(is_ctx, mc[scale_idx:scale_idx + 1], mb[scale_idx:scale_idx + 1])
    shift = jnp.where(is_ctx, mc[shift_idx:shift_idx + 1], mb[shift_idx:shift_idx + 1])
    return y * (1.0 + scale) + shift


def _gate_mod(mb, mc, idx, row0, tc, n):
    rows = row0 + lax.broadcasted_iota(jnp.int32, (n, 1), 0)
    return jnp.where(rows < tc, mc[idx:idx + 1], mb[idx:idx + 1])


def _mod_kernel(c_ref, w_ref, b_ref, o_ref):
    s = _silu(c_ref[...])
    o_ref[...] = jnp.dot(s, w_ref[...], preferred_element_type=F32,
                         precision=lax.Precision.HIGHEST) + b_ref[...]


def _modulation(cvec, w_mod, b_mod):
    depth, d, n = w_mod.shape
    tn = 1536 if n % 1536 == 0 else n
    rows = cvec.shape[0]
    return pl.pallas_call(
        _mod_kernel,
        grid=(depth, n // tn),
        in_specs=[pl.BlockSpec((rows, d), lambda l, j: (0, 0)),
                  pl.BlockSpec((None, d, tn), lambda l, j: (l, 0, j)),
                  pl.BlockSpec((None, 1, tn), lambda l, j: (l, 0, j))],
        out_specs=pl.BlockSpec((None, rows, tn), lambda l, j: (l, 0, j)),
        out_shape=jax.ShapeDtypeStruct((depth, rows, n), F32),
        compiler_params=_cparams(("arbitrary", "arbitrary")),
        name="modulation",
    )(cvec, w_mod, b_mod.reshape(depth, 1, n))


def _inproj_kernel(x_ref, mb_ref, mc_ref, g_ref, w_ref, o_ref, *, tm, tc):
    i = pl.program_id(1)
    h = _norm_mod(x_ref[...], g_ref[...], mb_ref[...], mc_ref[...], i * tm, tc, 0, 1)
    o_ref[...] = jnp.dot(h.astype(BF16), w_ref[...], preferred_element_type=F32)


def _inproj(xa, mod_l, g, w, tc, nb):
    b, ttot, d = xa.shape
    n = w.shape[1]
    tm = _row_tile(ttot)
    return pl.pallas_call(
        functools.partial(_inproj_kernel, tm=tm, tc=tc),
        grid=(b, ttot // tm),
        in_specs=[pl.BlockSpec((None, tm, d), lambda bi, i: (bi, i, 0)),
                  pl.BlockSpec((None, N_MOD, d), lambda bi, i: (bi, 0, 0)),
                  pl.BlockSpec((None, N_MOD, d), lambda bi, i: (nb, 0, 0)),
                  pl.BlockSpec((1, d), lambda bi, i: (0, 0)),
                  pl.BlockSpec((d, n), lambda bi, i: (0, 0))],
        out_specs=pl.BlockSpec((None, tm, n), lambda bi, i: (bi, i, 0)),
        out_shape=jax.ShapeDtypeStruct((b, ttot, n), F32),
        compiler_params=_cparams(("parallel", "parallel")),
        name="inproj",
    )(xa, mod_l, mod_l, g.reshape(1, d), w)


def _seq_block(step, n_ctx, n_all, reverse):
    if not reverse:
        return step
    return jnp.where(step < n_ctx, n_ctx - 1 - step, n_all - 1 - step + n_ctx)


def _lru_kernel(*refs, lc, n_ctx, n_all):
    cw_ref, cb_ref = refs[:2]
    n_in = 8
    ins = [refs[2 + d * n_in:2 + (d + 1) * n_in] for d in range(2)]
    outs = refs[2 + 2 * n_in:4 + 2 * n_in]
    scratch = refs[4 + 2 * n_in:]
    for d in range(2):
        _lru_block(*ins[d], cw_ref, cb_ref, outs[d], *scratch[4 * d:4 * d + 4],
                   lc=lc, n_ctx=n_ctx, n_all=n_all, reverse=bool(d))


def _lru_block(u_ref, hp_ref, hn_ref, wa_ref, ba_ref, wx_ref, bx_ref, sp_ref, cw_ref, cb_ref,
               o_ref, uc_ref, a_ref, b_ref, h_ref, *, lc, n_ctx, n_all, reverse):
    step = pl.program_id(1)
    blk = _seq_block(step, n_ctx, n_all, reverse)
    has_prev = jnp.logical_and(blk != 0, blk != n_ctx)
    has_next = jnp.logical_and(blk != n_ctx - 1, blk != n_all - 1)

    @pl.when(step == 0)
    def _():
        h_ref[...] = jnp.zeros_like(h_ref)

    w0, w1, w2, w3 = (cw_ref[j:j + 1, :] for j in range(CONV_W))
    cb = cb_ref[...]
    u = u_ref[...]
    uc_ref[...] = (cb + pltpu.roll(u, 2, 0) * w0 + pltpu.roll(u, 1, 0) * w1 + u * w2
                   + pltpu.roll(u, lc - 1, 0) * w3)
    row = lax.broadcasted_iota(jnp.int32, (SUBLANES, 1), 0)
    hp = jnp.where(has_prev, hp_ref[...], 0.0)
    hn = jnp.where(has_next, hn_ref[...], 0.0)
    u0 = u_ref[0:SUBLANES, :]
    u1 = u_ref[SUBLANES:2 * SUBLANES, :]
    uc_ref[0:SUBLANES, :] = (
        cb + jnp.where(row < 2, pltpu.roll(hp, 2, 0), pltpu.roll(u0, 2, 0)) * w0
        + jnp.where(row < 1, pltpu.roll(hp, 1, 0), pltpu.roll(u0, 1, 0)) * w1 + u0 * w2
        + jnp.where(row < SUBLANES - 1, pltpu.roll(u0, SUBLANES - 1, 0),
                    pltpu.roll(u1, SUBLANES - 1, 0)) * w3)
    ul = u_ref[lc - SUBLANES:lc, :]
    um = u_ref[lc - 2 * SUBLANES:lc - SUBLANES, :]
    uc_ref[lc - SUBLANES:lc, :] = (
        cb + jnp.where(row < 2, pltpu.roll(um, 2, 0), pltpu.roll(ul, 2, 0)) * w0
        + jnp.where(row < 1, pltpu.roll(um, 1, 0), pltpu.roll(ul, 1, 0)) * w1 + ul * w2
        + jnp.where(row < SUBLANES - 1, pltpu.roll(ul, SUBLANES - 1, 0),
                    pltpu.roll(hn, SUBLANES - 1, 0)) * w3)

    uc = uc_ref[...]
    ub = uc.astype(BF16)
    r = _sigmoid(jnp.dot(ub, wa_ref[...], preferred_element_type=F32) + ba_ref[...])
    gi = _sigmoid(jnp.dot(ub, wx_ref[...], preferred_element_type=F32) + bx_ref[...])
    log_a = -LRU_C * r * sp_ref[...]
    a = jnp.exp(log_a)
    b = jnp.sqrt(1.0 - a * a) * (gi * uc)
    sub = lax.broadcasted_iota(jnp.int32, (lc, 1), 0) % SUBLANES
    dist = 1
    while dist < SUBLANES:
        inside = (sub < SUBLANES - dist) if reverse else (sub >= dist)
        shift = (lc - dist) if reverse else dist
        a_sh = jnp.where(inside, pltpu.roll(a, shift, 0), 1.0)
        b_sh = jnp.where(inside, pltpu.roll(b, shift, 0), 0.0)
        b = a * b_sh + b
        a = a * a_sh
        dist *= 2
    a_ref[...] = a
    b_ref[...] = b
    n_grp = lc // SUBLANES

    def body(i, carry):
        t = (n_grp - 1 - i) if reverse else i
        rows = pl.ds(pl.multiple_of(t * SUBLANES, SUBLANES), SUBLANES)
        h = a_ref[rows, :] * carry + b_ref[rows, :]
        b_ref[rows, :] = h
        return h[0:1, :] if reverse else h[SUBLANES - 1:SUBLANES, :]

    h_ref[0:1, :] = lax.fori_loop(0, n_grp, body, h_ref[0:1, :], unroll=4)
    o_ref[...] = b_ref[...].astype(o_ref.dtype)


def _lru(z, conv_w, conv_b, wa_bd, ba, wx_bd, bx, sp, tc):
    b, ttot, _ = z.shape
    w = conv_w.shape[1]
    lc = SEQ_BLOCK
    n_all, n_ctx = ttot // lc, tc // lc
    hb = lc // SUBLANES
    n_h = ttot // SUBLANES
    col = 1

    def direction(d):
        def blk(s):
            return _seq_block(s, n_ctx, n_all, bool(d))

        vec = pl.BlockSpec((None, 1, w), lambda bi, s: (d, 0, 0))
        mat = pl.BlockSpec((None, w, w), lambda bi, s: (d, 0, 0))
        specs = [pl.BlockSpec((None, lc, w), lambda bi, s: (bi, blk(s), col)),
                 pl.BlockSpec((None, SUBLANES, w),
                              lambda bi, s: (bi, jnp.maximum(blk(s) * hb - 1, 0), col)),
                 pl.BlockSpec((None, SUBLANES, w),
                              lambda bi, s: (bi, jnp.minimum((blk(s) + 1) * hb, n_h - 1), col)),
                 mat, vec, mat, vec, vec]
        return specs, pl.BlockSpec((None, lc, w), lambda bi, s: (bi, blk(s), 0))

    (in_f, out_f), (in_r, out_r) = direction(0), direction(1)
    out = jax.ShapeDtypeStruct((b, ttot, w), BF16)
    per_dir = [pltpu.VMEM((lc, w), F32)] * 3 + [pltpu.VMEM((SUBLANES, w), F32)]
    vecs = lambda x: x.reshape(2, 1, w)
    args = (z, z, z, wa_bd, vecs(ba), wx_bd, vecs(bx), vecs(sp))
    return pl.pallas_call(
        functools.partial(_lru_kernel, lc=lc, n_ctx=n_ctx, n_all=n_all),
        grid=(b, n_all),
        in_specs=[pl.BlockSpec((CONV_W, w), lambda bi, s: (0, 0)),
                  pl.BlockSpec((1, w), lambda bi, s: (0, 0))] + in_f + in_r,
        out_specs=[out_f, out_r],
        out_shape=[out, out],
        scratch_shapes=per_dir + per_dir,
        compiler_params=_cparams(("parallel", "arbitrary")),
        name="rglru_bidir",
    )(conv_w, conv_b.reshape(1, w), *args, *args)


def _rope(x, cos, sin_signed):
    lane = lax.broadcasted_iota(jnp.int32, x.shape, 1)
    swapped = jnp.where(lane % 32 < 16, pltpu.roll(x, LANES - 16, 1), pltpu.roll(x, 16, 1))
    return x * cos + swapped * sin_signed


def _pair_layout(x, x_sw, head):
    lo = lax.broadcasted_iota(jnp.int32, x.shape, 1) < HEAD_DIM
    first, second = (x, x_sw) if head == 0 else (x_sw, x)
    return jnp.concatenate([jnp.where(lo, first, 0.0), jnp.where(lo, 0.0, second)],
                           axis=0).astype(BF16)


def _attend(q_block, keys, vals, bias, sink_row, o_ref):
    nk = keys.shape[0]
    keys_sw = pltpu.roll(keys, HEAD_DIM, 1)
    vals_t = vals.T
    top = lax.broadcasted_iota(jnp.int32, (LANES, BLOCK_Q), 0) < HEAD_DIM
    zero_t = jnp.zeros((HEAD_DIM, nk), F32)
    pairs_per_kv = ATT_GROUP // 2
    for kv in range(ATT_KV_HEADS):
        kab = _pair_layout(keys, keys_sw, kv)
        vt = vals_t[kv * HEAD_DIM:(kv + 1) * HEAD_DIM, :]
        vab_t = jnp.concatenate([jnp.concatenate([vt, zero_t], axis=1),
                                 jnp.concatenate([zero_t, vt], axis=1)], axis=0).astype(BF16)
        for g in range(kv * pairs_per_kv, (kv + 1) * pairs_per_kv):
            s_t = lax.dot_general(kab, q_block(g), (((1,), (1,)), ((), ())),
                                  preferred_element_type=F32)
            halves = []
            recips = []
            for half in range(2):
                sh = s_t[half * nk:(half + 1) * nk, :]
                if bias is not None:
                    sh = sh + bias
                sk = sink_row[:, 2 * g + half:2 * g + half + 1]
                m = jnp.maximum(jnp.max(sh, axis=0, keepdims=True), sk)
                p = jnp.exp2(sh - m)
                recips.append(1.0 / (jnp.sum(p, axis=0, keepdims=True) + jnp.exp2(sk - m)))
                halves.append(p.astype(BF16))
            o_t = jnp.dot(vab_t, jnp.concatenate(halves, axis=0), preferred_element_type=F32)
            o_t = o_t * jnp.where(top, recips[0], recips[1])
            o_ref[:, g * LANES:(g + 1) * LANES] = o_t.T.astype(o_ref.dtype)


def _attn_kernel(q_ref, kp_ref, k0_ref, kn_ref, vp_ref, v0_ref, vn_ref, kc_ref, vc_ref,
                 cq_ref, sq_ref, cp_ref, sp_ref, cn_ref, sn_ref, sink_ref, o_ref, *, n_ctx, n_lat):
    i = pl.program_id(1)
    qscale = HEAD_DIM ** -0.5 * LOG2E
    sink_row = sink_ref[...] * LOG2E

    @pl.when(i < n_ctx)
    def _():
        def q_block(g):
            return (q_ref[:, g * LANES:(g + 1) * LANES] * qscale).astype(BF16)
        _attend(q_block, kc_ref[...], vc_ref[...], None, sink_row, o_ref)

    @pl.when(i >= n_ctx)
    def _():
        n = i - n_ctx
        cq, sq = cq_ref[...], sq_ref[...]

        def q_block(g):
            return (_rope(q_ref[:, g * LANES:(g + 1) * LANES], cq, sq) * qscale).astype(BF16)

        kp = _rope(kp_ref[...], cp_ref[...], sp_ref[...])
        k0 = _rope(k0_ref[...], cq, sq)
        kn = _rope(kn_ref[...], cn_ref[...], sn_ref[...])
        keys = jnp.concatenate([kp, k0, kn, kc_ref[...]], axis=0)
        vals = jnp.concatenate([vp_ref[...], v0_ref[...], vn_ref[...], vc_ref[...]], axis=0)
        nk = keys.shape[0]
        qpos = n * BLOCK_Q + lax.broadcasted_iota(jnp.int32, (nk, BLOCK_Q), 1)
        key = lax.broadcasted_iota(jnp.int32, (nk, BLOCK_Q), 0)
        kpos = (n - 1) * BLOCK_Q + key
        local = (jnp.abs(qpos - kpos) <= WINDOW) & (kpos >= 0) & (kpos < n_lat * BLOCK_Q)
        bias = jnp.where(local | (key >= 3 * BLOCK_Q), 0.0, NEG)
        _attend(q_block, keys, vals, bias, sink_row, o_ref)


def _attention(z, cos_t, sin_t, sink, tc):
    b, ttot, _ = z.shape
    n_all, n_ctx = ttot // BLOCK_Q, tc // BLOCK_Q
    n_lat = n_all - n_ctx
    kcol = (2 * 512 + ATT_HEADS * HEAD_DIM) // LANES
    vcol = kcol + 1
    qcol = 2 * 512 // (ATT_HEADS * HEAD_DIM)

    def prev(i):
        return jnp.maximum(i - 1, 0)

    def nxt(i):
        return jnp.minimum(i + 1, n_all - 1)

    def rows(col, f):
        return pl.BlockSpec((None, BLOCK_Q, LANES), lambda bi, i: (bi, f(i), col))

    def tab(f):
        return pl.BlockSpec((BLOCK_Q, LANES), lambda bi, i: (f(i), 0))

    same = lambda i: i
    return pl.pallas_call(
        functools.partial(_attn_kernel, n_ctx=n_ctx, n_lat=n_lat),
        grid=(b, n_all),
        in_specs=[pl.BlockSpec((None, BLOCK_Q, ATT_HEADS * HEAD_DIM), lambda bi, i: (bi, i, qcol)),
                  rows(kcol, prev), rows(kcol, same), rows(kcol, nxt),
                  rows(vcol, prev), rows(vcol, same), rows(vcol, nxt),
                  pl.BlockSpec((None, tc, LANES), lambda bi, i: (bi, 0, kcol)),
                  pl.BlockSpec((None, tc, LANES), lambda bi, i: (bi, 0, vcol)),
                  tab(same), tab(same), tab(prev), tab(prev), tab(nxt), tab(nxt),
                  pl.BlockSpec((1, LANES), lambda bi, i: (0, 0))],
        out_specs=pl.BlockSpec((None, BLOCK_Q, ATT_HEADS * HEAD_DIM), lambda bi, i: (bi, i, 0)),
        out_shape=jax.ShapeDtypeStruct((b, ttot, ATT_HEADS * HEAD_DIM), BF16),
        compiler_params=_cparams(("parallel", "parallel")),
        name="window_gqa",
    )(z, z, z, z, z, z, z, z, z, cos_t, sin_t, cos_t, sin_t, cos_t, sin_t, sink)


def _outproj_ab_kernel(x_ref, mb_ref, mc_ref, lf_ref, lb_ref, g_ref, at_ref, w_ref, o_ref, *, tm, tc):
    i = pl.program_id(1)
    lru = (lf_ref[...].astype(F32) + lb_ref[...].astype(F32)) * _gelu(g_ref[...])
    mix = jnp.concatenate([lru.astype(BF16), at_ref[...]], axis=1)
    dx = jnp.dot(mix, w_ref[...], preferred_element_type=F32)
    gate = _gate_mod(mb_ref[...], mc_ref[...], 2, i * tm, tc, tm)
    o_ref[...] = x_ref[...] + gate * dx


def _outproj_ab(xa, mod_l, lru_f, lru_b, z, att, w_out, tc, nb):
    b, ttot, d = xa.shape
    w = lru_f.shape[2]
    tm = _row_tile(ttot)
    tile = lambda width, col: pl.BlockSpec((None, tm, width), lambda bi, i: (bi, i, col))
    return pl.pallas_call(
        functools.partial(_outproj_ab_kernel, tm=tm, tc=tc),
        grid=(b, ttot // tm),
        in_specs=[tile(d, 0),
                  pl.BlockSpec((None, N_MOD, d), lambda bi, i: (bi, 0, 0)),
                  pl.BlockSpec((None, N_MOD, d), lambda bi, i: (nb, 0, 0)),
                  tile(w, 0), tile(w, 0), tile(w, 0), tile(att.shape[2], 0),
                  pl.BlockSpec(w_out.shape, lambda bi, i: (0, 0))],
        out_specs=tile(d, 0),
        out_shape=jax.ShapeDtypeStruct(xa.shape, F32),
        compiler_params=_cparams(("parallel", "parallel")),
        name="outproj_ab",
    )(xa, mod_l, mod_l, lru_f, lru_b, z, att, w_out)


def _outproj_cd_kernel(x_ref, mb_ref, mc_ref, y_ref, of_ref, ob_ref, g_ref, wg_ref, bg_ref, gn_ref,
                       w_ref, o_ref, *, tm, tc):
    i = pl.program_id(1)
    y = _gelu(y_ref[...])
    s5 = y * _sigmoid(jnp.dot(y.astype(BF16), wg_ref[...], preferred_element_type=F32) + bg_ref[...])
    o = of_ref[...].astype(F32) + ob_ref[...].astype(F32)
    parts = []
    for h in range(HG_HEADS):
        oh = o[:, h * HG_DK:(h + 1) * HG_DK]
        ms = jnp.mean(oh * oh, axis=-1, keepdims=True)
        parts.append(oh * lax.rsqrt(ms + EPS) * gn_ref[...])
    hg = jnp.concatenate(parts, axis=1) * _silu(g_ref[...])
    mix = jnp.concatenate([s5.astype(BF16), hg.astype(BF16)], axis=1)
    dx = jnp.dot(mix, w_ref[...], preferred_element_type=F32)
    gate = _gate_mod(mb_ref[...], mc_ref[...], 2, i * tm, tc, tm)
    o_ref[...] = x_ref[...] + gate * dx


def _outproj_cd(xa, mod_l, y5, o_f, o_b, z, w_glu, b_glu, hg_norm, w_out, tc, nb):
    b, ttot, d = xa.shape
    w = y5.shape[2]
    tm = _row_tile(ttot)
    gcol = z.shape[2] // w - 1
    tile = lambda width, col: pl.BlockSpec((None, tm, width), lambda bi, i: (bi, i, col))
    return pl.pallas_call(
        functools.partial(_outproj_cd_kernel, tm=tm, tc=tc),
        grid=(b, ttot // tm),
        in_specs=[tile(d, 0),
                  pl.BlockSpec((None, N_MOD, d), lambda bi, i: (bi, 0, 0)),
                  pl.BlockSpec((None, N_MOD, d), lambda bi, i: (nb, 0, 0)),
                  tile(w, 0), tile(w, 0), tile(w, 0), tile(w, gcol),
                  pl.BlockSpec(w_glu.shape, lambda bi, i: (0, 0)),
                  pl.BlockSpec((1, w), lambda bi, i: (0, 0)),
                  pl.BlockSpec((1, HG_DK), lambda bi, i: (0, 0)),
                  pl.BlockSpec(w_out.shape, lambda bi, i: (0, 0))],
        out_specs=tile(d, 0),
        out_shape=jax.ShapeDtypeStruct(xa.shape, F32),
        compiler_params=_cparams(("parallel", "parallel")),
        name="outproj_cd",
    )(xa, mod_l, mod_l, y5, o_f, o_b, z, w_glu, b_glu.reshape(1, w), hg_norm.reshape(1, HG_DK), w_out)


def _ffn_kernel(x_ref, mb_ref, mc_ref, g_ref, w1_ref, w3_ref, w2_ref, o_ref, h_ref, acc_ref, *, tm, tc):
    i = pl.program_id(1)
    k = pl.program_id(2)

    @pl.when(k == 0)
    def _():
        h = _norm_mod(x_ref[...], g_ref[...], mb_ref[...], mc_ref[...], i * tm, tc, 3, 4)
        h_ref[...] = h.astype(BF16)
        acc_ref[...] = jnp.zeros_like(acc_ref)

    h = h_ref[...]
    tf = w1_ref.shape[1]
    half = (tf // LANES + 1) // 2 * LANES
    y = None
    for f0, f1 in ([(0, half), (half, tf)] if 0 < half < tf else [(0, tf)]):
        a = jnp.dot(h, w1_ref[:, f0:f1], preferred_element_type=F32)
        c = jnp.dot(h, w3_ref[:, f0:f1], preferred_element_type=F32)
        part = jnp.dot((_silu(a) * c).astype(BF16), w2_ref[f0:f1, :], preferred_element_type=F32)
        y = part if y is None else y + part
    acc_ref[...] += y

    @pl.when(k == pl.num_programs(2) - 1)
    def _():
        gate = _gate_mod(mb_ref[...], mc_ref[...], 5, i * tm, tc, tm)
        o_ref[...] = x_ref[...] + gate * acc_ref[...]


def _ffn(xa, mod_l, g, w1, w3, w2, layer, tc, nb):
    b, ttot, d = xa.shape
    f = w1.shape[2]
    tm = _tall_row_tile(ttot)
    tf = f // 2 if (f // 2) % LANES == 0 else f
    return pl.pallas_call(
        functools.partial(_ffn_kernel, tm=tm, tc=tc),
        grid=(b, ttot // tm, f // tf),
        in_specs=[pl.BlockSpec((None, tm, d), lambda bi, i, k: (bi, i, 0)),
                  pl.BlockSpec((None, N_MOD, d), lambda bi, i, k: (bi, 0, 0)),
                  pl.BlockSpec((None, N_MOD, d), lambda bi, i, k: (nb, 0, 0)),
                  pl.BlockSpec((1, d), lambda bi, i, k: (0, 0)),
                  pl.BlockSpec((None, d, tf), lambda bi, i, k: (layer, 0, k)),
                  pl.BlockSpec((None, d, tf), lambda bi, i, k: (layer, 0, k)),
                  pl.BlockSpec((None, tf, d), lambda bi, i, k: (layer, k, 0))],
        out_specs=pl.BlockSpec((None, tm, d), lambda bi, i, k: (bi, i, 0)),
        out_shape=jax.ShapeDtypeStruct(xa.shape, F32),
        scratch_shapes=[pltpu.VMEM((tm, d), BF16), pltpu.VMEM((tm, d), F32)],
        compiler_params=_cparams(("parallel", "parallel", "arbitrary")),
        name="ffn_swiglu",
    )(xa, mod_l, mod_l, g.reshape(1, d), w1, w3, w2)


def _router_kernel(x_ref, mb_ref, mc_ref, g_ref, r_ref, h_ref, wt_ref, cnt_ref, *, tm, tc, n_exp):
    i = pl.program_id(1)
    h = _norm_mod(x_ref[...], g_ref[...], mb_ref[...], mc_ref[...], i * tm, tc, 3, 4)
    nt = (((1,), (1,)), ((), ()))
    h_hi = h.astype(BF16)
    h_lo = (h - h_hi.astype(F32)).astype(BF16)
    r = r_ref[...]
    r_hi = r.astype(BF16)
    r_lo = (r - r_hi.astype(F32)).astype(BF16)
    logits = (lax.dot_general(r_hi, h_hi, nt, preferred_element_type=F32)
              + lax.dot_general(r_lo, h_hi, nt, preferred_element_type=F32)
              + lax.dot_general(r_hi, h_lo, nt, preferred_element_type=F32))
    sub = lax.broadcasted_iota(jnp.int32, logits.shape, 0)
    h_ref[...] = h_hi
    logits = jnp.where(sub < n_exp, logits, NEG)
    m1 = jnp.max(logits, axis=0, keepdims=True)
    i1 = jnp.min(jnp.where(logits == m1, sub, SUBLANES), axis=0, keepdims=True)
    rest = jnp.where(sub == i1, NEG, logits)
    m2 = jnp.max(rest, axis=0, keepdims=True)
    i2 = jnp.min(jnp.where(rest == m2, sub, SUBLANES), axis=0, keepdims=True)
    e2 = jnp.exp(m2 - m1)
    g1 = 1.0 / (1.0 + e2)
    wt = jnp.where(sub == i1, g1, 0.0) + jnp.where(sub == i2, e2 * g1, 0.0)
    wt_ref[...] = wt
    cnt = jnp.sum(jnp.where(wt > 0.0, 1.0, 0.0), axis=1, keepdims=True)
    cnt_ref[...] = jnp.broadcast_to(cnt, cnt_ref.shape).astype(jnp.int32)


def _router(xa, mod_l, g, router_t, tc, nb, n_exp):
    b, ttot, d = xa.shape
    tm = _row_tile(ttot)
    n_t = ttot // tm
    return pl.pallas_call(
        functools.partial(_router_kernel, tm=tm, tc=tc, n_exp=n_exp),
        grid=(b, n_t),
        in_specs=[pl.BlockSpec((None, tm, d), lambda bi, i: (bi, i, 0)),
                  pl.BlockSpec((None, N_MOD, d), lambda bi, i: (bi, 0, 0)),
                  pl.BlockSpec((None, N_MOD, d), lambda bi, i: (nb, 0, 0)),
                  pl.BlockSpec((1, d), lambda bi, i: (0, 0)),
                  pl.BlockSpec((SUBLANES, d), lambda bi, i: (0, 0))],
        out_specs=[pl.BlockSpec((tm, d), lambda bi, i: (bi * n_t + i, 0)),
                   pl.BlockSpec((SUBLANES, tm), lambda bi, i: (0, bi * n_t + i)),
                   pl.BlockSpec((None, SUBLANES, LANES), lambda bi, i: (bi * n_t + i, 0, 0))],
        out_shape=[jax.ShapeDtypeStruct((b * ttot, d), BF16),
                   jax.ShapeDtypeStruct((SUBLANES, b * ttot), F32),
                   jax.ShapeDtypeStruct((b * n_t, SUBLANES, LANES), jnp.int32)],
        compiler_params=_cparams(("parallel", "parallel")),
        name="moe_router",
    )(xa, mod_l, mod_l, g.reshape(1, d), router_t)


def _moe_kernel(cnt_ref, x_ref, *refs, tm, tc, n_exp, n_t, group):
    mb_refs = refs[:group]
    mc_ref, h_ref, wt_ref, w1_ref, w3_ref, w2_ref, o_ref, rank_ref = refs[group:]
    p = pl.program_id(0)
    e = pl.program_id(1)
    f = w1_ref.shape[1]
    half = (f // LANES + 1) // 2 * LANES
    f_split = [(0, half), (half, f)] if 0 < half < f else [(0, f)]

    @pl.when(e == 0)
    def _():
        o_ref[...] = jnp.zeros_like(o_ref)
        r_i = lax.broadcasted_iota(jnp.int32, (tm, tm), 0)
        c_i = lax.broadcasted_iota(jnp.int32, (tm, tm), 1)
        before = jnp.where(r_i < c_i, 1.0, 0.0).astype(BF16)
        for j in range(group):
            sel = wt_ref[:, j * tm:(j + 1) * tm] > 0.0
            rank = jnp.dot(jnp.where(sel, 1.0, 0.0).astype(BF16), before, preferred_element_type=F32)
            rank_ref[j] = jnp.where(sel, rank, -1.0)

    counts = [cnt_ref[(p * group + j) * n_exp + e] for j in range(group)]

    offsets = [jnp.int32(0)]
    for j in range(group - 1):
        offsets.append(offsets[-1] + counts[j])
    total = offsets[-1] + counts[-1]

    def expert_pass(m_rows, base):
        slot = (lax.broadcasted_iota(jnp.int32, (m_rows, tm), 0) + base).astype(F32)
        onehots = []
        xe = None
        gate = None
        for j in range(group):
            r_row = rank_ref[j, pl.ds(e, 1), :]
            r_row = jnp.where(r_row >= 0.0, r_row + offsets[j].astype(F32), -1.0)
            w_row = wt_ref[pl.ds(e, 1), j * tm:(j + 1) * tm]
            hit = r_row == slot
            onehot = jnp.where(hit, 1.0, 0.0).astype(BF16)
            onehots.append(onehot)
            xj = jnp.dot(onehot, h_ref[j * tm:(j + 1) * tm, :], preferred_element_type=F32)
            gj = jnp.sum(jnp.where(hit, w_row, 0.0), axis=1, keepdims=True)
            xe = xj if xe is None else xe + xj
            gate = gj if gate is None else gate + gj
        xe = xe.astype(BF16)
        y = None
        for f0, f1 in f_split:
            a = jnp.dot(xe, w1_ref[:, f0:f1], preferred_element_type=F32)
            c = jnp.dot(xe, w3_ref[:, f0:f1], preferred_element_type=F32)
            part = jnp.dot((_silu(a) * c * gate).astype(BF16), w2_ref[f0:f1, :],
                           preferred_element_type=F32)
            y = part if y is None else y + part
        y = y.astype(BF16)
        for j in range(group):
            o_ref[j * tm:(j + 1) * tm, :] += lax.dot_general(
                onehots[j], y, (((0,), (0,)), ((), ())), preferred_element_type=F32)

    n_full = total // MOE_PASS_ROWS

    def full_pass(it, carry):
        expert_pass(MOE_PASS_ROWS, it * MOE_PASS_ROWS)
        return carry

    lax.fori_loop(0, n_full, full_pass, 0)
    rest = total - n_full * MOE_PASS_ROWS

    @pl.when(rest > MOE_PASS_ROWS // 2)
    def _():
        expert_pass(MOE_PASS_ROWS, n_full * MOE_PASS_ROWS)

    @pl.when(jnp.logical_and(rest > 0, rest <= MOE_PASS_ROWS // 2))
    def _():
        expert_pass(MOE_PASS_ROWS // 2, n_full * MOE_PASS_ROWS)

    @pl.when(e == n_exp - 1)
    def _():
        for j in range(group):
            i = (p * group + j) % n_t
            gate5 = _gate_mod(mb_refs[j][...], mc_ref[...], 5, i * tm, tc, tm)
            rows = slice(j * tm, (j + 1) * tm)
            o_ref[rows, :] = x_ref[rows, :] + gate5 * o_ref[rows, :]


def _moe(xa, mod_l, g, router_t, w1, w3, w2, layer, tc, nb):
    b, ttot, d = xa.shape
    _, n_exp, _, f = w1.shape
    tm = _row_tile(ttot)
    n_t = ttot // tm
    group = 2 if (b * n_t) % 2 == 0 else 1
    h, wt, cnt = _router(xa, mod_l, g, router_t, tc, nb, n_exp)
    cnt = cnt[:, :n_exp, 0].reshape(-1)
    rows = group * tm
    mods = [pl.BlockSpec((None, N_MOD, d), functools.partial(
        lambda p, e, c, j: ((p * group + j) // n_t, 0, 0), j=j)) for j in range(group)]
    grid_spec = pltpu.PrefetchScalarGridSpec(
        num_scalar_prefetch=1,
        grid=(b * n_t // group, n_exp),
        in_specs=[pl.BlockSpec((rows, d), lambda p, e, c: (p, 0), pipeline_mode=pl.Buffered(1))]
        + mods
        + [pl.BlockSpec((None, N_MOD, d), lambda p, e, c: (nb, 0, 0)),
           pl.BlockSpec((rows, d), lambda p, e, c: (p, 0)),
           pl.BlockSpec((SUBLANES, rows), lambda p, e, c: (0, p)),
           pl.BlockSpec((None, None, d, f), lambda p, e, c: (layer, e, 0, 0)),
           pl.BlockSpec((None, None, d, f), lambda p, e, c: (layer, e, 0, 0)),
           pl.BlockSpec((None, None, f, d), lambda p, e, c: (layer, e, 0, 0))],
        out_specs=pl.BlockSpec((rows, d), lambda p, e, c: (p, 0)),
        scratch_shapes=[pltpu.VMEM((group, SUBLANES, tm), F32)])
    out = pl.pallas_call(
        functools.partial(_moe_kernel, tm=tm, tc=tc, n_exp=n_exp, n_t=n_t, group=group),
        grid_spec=grid_spec,
        out_shape=jax.ShapeDtypeStruct((b * ttot, d), F32),
        compiler_params=_cparams(("parallel", "arbitrary")),
        name="moe_experts",
    )(cnt, xa.reshape(b * ttot, d), *([mod_l] * group), mod_l, h, wt, w1, w3, w2)
    return out.reshape(b, ttot, d)


def _spread_groups(x, inner, row_group, n_grp):
    cols_in = x.shape[2]
    cols_out = cols_in * n_grp
    q = jnp.arange(cols_out)
    src = (q // (n_grp * inner)) * inner + q % inner
    tile = (jnp.arange(cols_in)[:, None] == src[None, :]).astype(BF16)
    col_group = (q // inner) % n_grp
    y = jnp.dot(x.astype(BF16), tile)
    return jnp.where(row_group[:, None] == col_group[None, :], y, jnp.zeros((), BF16))


def _s5_weights(a_re, a_im, log_step, b_re, b_im, c_re, c_im):
    L = S5_CHUNK
    lr = jnp.minimum(a_re, -1e-4)
    li = a_im
    dt = jnp.exp(log_step)[..., None]
    mag, ang = lr * dt, li * dt
    lbr, lbi = jnp.exp(mag) * jnp.cos(ang), jnp.exp(mag) * jnp.sin(ang)
    zr, zi = lbr - 1.0, lbi
    den = lr * lr + li * li
    fr = (zr * lr + zi * li) / den
    fi = (zi * lr - zr * li) / den
    bbr = fr[..., None] * b_re - fi[..., None] * b_im
    bbi = fr[..., None] * b_im + fi[..., None] * b_re

    def power(p):
        p = p[..., None, None, None].astype(F32)
        return jnp.exp(mag * p) * jnp.cos(ang * p), jnp.exp(mag * p) * jnp.sin(ang * p)

    n_dir, n_grp, n_st = a_re.shape
    gpb = LANES // S5_GROUP
    n_blk = n_grp // gpb
    s = jnp.arange(L)
    in_group = (jnp.arange(L * LANES) % LANES) // S5_GROUP
    st_group = jnp.arange(gpb * n_st) // n_st
    outs = []
    for d in range(n_dir):
        pr, pi = power(jnp.arange(L))
        pr, pi = pr[:, d], pi[:, d]
        cbr = (jnp.einsum('gcn,tgn,gnk->tgck', c_re[d], pr, bbr[d])
               - jnp.einsum('gcn,tgn,gnk->tgck', c_re[d], pi, bbi[d])
               - jnp.einsum('gcn,tgn,gnk->tgck', c_im[d], pr, bbi[d])
               - jnp.einsum('gcn,tgn,gnk->tgck', c_im[d], pi, bbr[d]))
        lag = (s[None, :] - s[:, None]) if d == 0 else (s[:, None] - s[None, :])
        kern = jnp.where((lag >= 0)[..., None, None, None], cbr[jnp.clip(lag, 0, L - 1)], 0.0)
        kern = kern.reshape(L, L, n_blk, gpb, S5_GROUP, S5_GROUP).transpose(2, 0, 3, 5, 1, 4)
        m = _spread_groups(kern.reshape(n_blk, L * LANES, L * S5_GROUP), S5_GROUP, in_group, gpb)
        qr, qi = power((L - 1 - s) if d == 0 else s)
        qr, qi = qr[:, d], qi[:, d]
        str_ = qr[..., None] * bbr[d] - qi[..., None] * bbi[d]
        sti = qr[..., None] * bbi[d] + qi[..., None] * bbr[d]

        def to_state(x):
            x = x.reshape(L, n_blk, gpb, n_st, S5_GROUP).transpose(1, 0, 2, 4, 3)
            return _spread_groups(x.reshape(n_blk, L * LANES, n_st), n_st, in_group, gpb)

        wst = jnp.concatenate([to_state(str_), to_state(sti)], axis=2)
        rr, ri = power((s + 1) if d == 0 else (L - s))
        rr, ri = rr[:, d], ri[:, d]
        wr = c_re[d][None] * rr[:, :, None, :] - c_im[d][None] * ri[:, :, None, :]
        wi = -(c_re[d][None] * ri[:, :, None, :] + c_im[d][None] * rr[:, :, None, :])

        def from_state(x):
            x = x.reshape(L, n_blk, gpb, S5_GROUP, n_st).transpose(1, 2, 4, 0, 3)
            return _spread_groups(x.reshape(n_blk, gpb * n_st, L * S5_GROUP), S5_GROUP, st_group, gpb)

        wout = jnp.concatenate([from_state(wr), from_state(wi)], axis=1)
        ler, lei = jnp.exp(mag[d] * L) * jnp.cos(ang[d] * L), jnp.exp(mag[d] * L) * jnp.sin(ang[d] * L)
        lam_l = jnp.concatenate([ler.reshape(n_blk, 1, gpb * n_st), lei.reshape(n_blk, 1, gpb * n_st)],
                                axis=2)
        outs.append((m, wst, wout, lam_l))
    return tuple(jnp.stack([o[k] for o in outs]) for k in range(4))


def _s5_kernel(u_ref, m_ref, wst_ref, wout_ref, lam_ref, d_ref, o_ref, x_ref, hp_ref, *, rows, rows_ctx):
    L = S5_CHUNK
    dr = pl.program_id(2)
    ns = lam_ref.shape[1] // 2
    u = jnp.concatenate([u_ref[pl.ds(s, rows, stride=L), :] for s in range(L)], axis=1)
    ub = u.astype(BF16)
    x_ref[...] = jnp.dot(ub, wst_ref[...], preferred_element_type=F32)
    lr = lam_ref[:, 0:ns]
    li = lam_ref[:, ns:2 * ns]

    def visit(r, carry):
        hr, hi = carry
        hp_ref[pl.ds(r, 1), 0:ns] = hr
        hp_ref[pl.ds(r, 1), ns:2 * ns] = hi
        xr = x_ref[pl.ds(r, 1), 0:ns]
        xi = x_ref[pl.ds(r, 1), ns:2 * ns]
        return lr * hr - li * hi + xr, lr * hi + li * hr + xi

    zero = (jnp.zeros((1, ns), F32), jnp.zeros((1, ns), F32))

    @pl.when(dr == 0)
    def _():
        lax.fori_loop(0, rows, visit, zero, unroll=4)

    @pl.when(dr == 1)
    def _():
        c = lax.fori_loop(0, rows_ctx, lambda t, c: visit(rows_ctx - 1 - t, c), zero, unroll=4)
        lax.fori_loop(0, rows - rows_ctx, lambda t, c: visit(rows - 1 - t, c), c, unroll=4)

    y = (jnp.dot(ub, m_ref[...], preferred_element_type=F32)
         + jnp.dot(hp_ref[...].astype(BF16), wout_ref[...], preferred_element_type=F32))

    @pl.when(dr == 0)
    def _():
        for s in range(L):
            o_ref[pl.ds(s, rows, stride=L), :] = (y[:, s * LANES:(s + 1) * LANES]
                                                  + d_ref[...] * u[:, s * LANES:(s + 1) * LANES])

    @pl.when(dr == 1)
    def _():
        for s in range(L):
            o_ref[pl.ds(s, rows, stride=L), :] += y[:, s * LANES:(s + 1) * LANES]


def _s5(z, weights, dvec, tc, width):
    b, ttot, _ = z.shape
    m, wst, wout, lam_l = weights
    n_dir, n_blk = m.shape[0], m.shape[1]
    rows, rows_ctx = ttot // S5_CHUNK, tc // S5_CHUNK
    wspec = lambda a: pl.BlockSpec((None, None) + a.shape[2:], lambda bi, j, dr: (dr, j, 0, 0))
    return pl.pallas_call(
        functools.partial(_s5_kernel, rows=rows, rows_ctx=rows_ctx),
        grid=(b, n_blk, n_dir),
        in_specs=[pl.BlockSpec((None, ttot, LANES), lambda bi, j, dr: (bi, 0, j)),
                  wspec(m), wspec(wst), wspec(wout), wspec(lam_l),
                  pl.BlockSpec((1, LANES), lambda bi, j, dr: (0, j))],
        out_specs=pl.BlockSpec((None, ttot, LANES), lambda bi, j, dr: (bi, 0, j)),
        out_shape=jax.ShapeDtypeStruct((b, ttot, width), F32),
        scratch_shapes=[pltpu.VMEM((rows, lam_l.shape[3]), F32), pltpu.VMEM((rows, lam_l.shape[3]), F32)],
        compiler_params=_cparams(("parallel", "parallel", "arbitrary")),
        name="s5_bidir",
    )(z, m, wst, wout, lam_l, dvec.reshape(1, width))


def _gla_kernel(qf_ref, ff_ref, vf_ref, qb_ref, fb_ref, vb_ref, lb_ref, of_ref, ob_ref,
                stf_ref, stb_ref, *, lc):
    step = pl.program_id(1)

    @pl.when(step == 0)
    def _():
        stf_ref[...] = jnp.zeros_like(stf_ref)
        stb_ref[...] = jnp.zeros_like(stb_ref)

    _gla_block(qf_ref, ff_ref, vf_ref, lb_ref, of_ref, stf_ref, lc=lc, reverse=False)
    _gla_block(qb_ref, fb_ref, vb_ref, lb_ref, ob_ref, stb_ref, lc=lc, reverse=True)


def _gla_block(q_ref, f_ref, v_ref, lb_ref, o_ref, st_ref, *, lc, reverse):
    c = HG_CHUNK
    n_chunks = lc // c
    r_i = lax.broadcasted_iota(jnp.int32, (lc, lc), 0)
    c_i = lax.broadcasted_iota(jnp.int32, (lc, lc), 1)
    same_chunk = (r_i // c) == (c_i // c)
    keep = same_chunk & ((c_i >= r_i) if reverse else (c_i <= r_i))
    tri = jnp.where(keep, 1.0, 0.0).astype(BF16)
    lb = lb_ref[...]
    q = _silu(q_ref[...])
    f = lb + (1.0 - lb) / (1.0 + jnp.exp(-f_ref[...]))
    k = 1.0 - f
    logf = jnp.log(f)
    v = v_ref[...].astype(BF16)
    hi = logf.astype(BF16)
    r1 = logf - hi.astype(F32)
    mid = r1.astype(BF16)
    lo = (r1 - mid.astype(F32)).astype(BF16)
    cum = (jnp.dot(tri, hi, preferred_element_type=F32) + jnp.dot(tri, mid, preferred_element_type=F32)
           + jnp.dot(tri, lo, preferred_element_type=F32))
    totals = [cum[ci * c:ci * c + 1, :] if reverse else cum[(ci + 1) * c - 1:(ci + 1) * c, :]
              for ci in range(n_chunks)]
    w = cum.shape[1]
    total = jnp.concatenate([jnp.broadcast_to(t, (c, w)) for t in totals], axis=0)
    centre = jnp.concatenate([jnp.broadcast_to(cum[ci * c + c // 2:ci * c + c // 2 + 1, :], (c, w))
                              for ci in range(n_chunks)], axis=0)
    q_in = (q * jnp.exp(cum)).astype(BF16)
    k_out = (k * jnp.exp(total - cum)).astype(BF16)
    q_loc = (q * jnp.exp(cum - centre)).astype(BF16)
    k_loc = (k * jnp.exp(centre - cum)).astype(BF16)
    decays = [jnp.exp(t) for t in totals]
    order = range(n_chunks - 1, -1, -1) if reverse else range(n_chunks)
    for h in range(HG_HEADS):
        sl = slice(h * HG_DK, (h + 1) * HG_DK)
        att = lax.dot_general(q_loc[:, sl], k_loc[:, sl], (((1,), (1,)), ((), ())),
                              preferred_element_type=F32)
        att = jnp.where(keep, att, 0.0)
        o_loc = jnp.dot(att.astype(BF16), v[:, sl], preferred_element_type=F32)
        incs = [lax.dot_general(v[ci * c:(ci + 1) * c, sl], k_out[ci * c:(ci + 1) * c, sl],
                                (((0,), (0,)), ((), ())), preferred_element_type=F32)
                for ci in range(n_chunks)]
        st = st_ref[h]
        for ci in order:
            rows = slice(ci * c, (ci + 1) * c)
            o = o_loc[rows] + lax.dot_general(q_in[rows, sl], st.astype(BF16),
                                              (((1,), (1,)), ((), ())), preferred_element_type=F32)
            o_ref[rows, sl] = o.astype(o_ref.dtype)
            st = st * decays[ci][:, sl] + incs[ci]
        st_ref[h] = st


def _gla(z, lb, tc):
    b, ttot, _ = z.shape
    w = HG_HEADS * HG_DK
    lc = SEQ_BLOCK
    n_all, n_ctx = ttot // lc, tc // lc

    def col(cidx, reverse):
        return pl.BlockSpec((None, lc, w),
                            lambda bi, s: (bi, _seq_block(s, n_ctx, n_all, reverse), cidx))

    out = jax.ShapeDtypeStruct((b, ttot, w), BF16)
    state = pltpu.VMEM((HG_HEADS, HG_DK, HG_DK), F32)
    return pl.pallas_call(
        functools.partial(_gla_kernel, lc=lc),
        grid=(b, n_all),
        in_specs=[col(1, False), col(2, False), col(4, False),
                  col(1, True), col(3, True), col(4, True),
                  pl.BlockSpec((1, w), lambda bi, s: (0, 0))],
        out_specs=[col(0, False), col(0, True)],
        out_shape=[out, out],
        scratch_shapes=[state, state],
        compiler_params=_cparams(("parallel", "arbitrary")),
        name="hgrn2_bidir",
    )(z, z, z, z, z, z, lb.reshape(1, w))


def _final_kernel(x_ref, g_ref, o_ref):
    x = x_ref[0]
    ms = jnp.mean(x * x, axis=-1, keepdims=True)
    o_ref[...] = x * lax.rsqrt(ms + EPS) * g_ref[...]


def _final_norm(xa, g, tc):
    b, ttot, d = xa.shape
    t = ttot - tc
    tm = next(rows for rows in (1024, 512, SEQ_BLOCK) if t % rows == 0)
    return pl.pallas_call(
        _final_kernel,
        grid=(b, t // tm),
        in_specs=[pl.BlockSpec((pl.Element(1), pl.Element(tm), pl.Element(d)),
                               lambda bi, i: (bi, pl.multiple_of(tc + i * tm, SUBLANES), 0)),
                  pl.BlockSpec((1, d), lambda bi, i: (0, 0))],
        out_specs=pl.BlockSpec((None, tm, d), lambda bi, i: (bi, i, 0)),
        out_shape=jax.ShapeDtypeStruct((b, t, d), F32),
        compiler_params=_cparams(("parallel", "parallel")),
        name="final_norm",
    )(xa, g.reshape(1, d))


def _rope_tables(t, tc):
    pos = jnp.arange(t)
    row = (pos // GRID_W).astype(F32)
    col = (pos % GRID_W).astype(F32)
    inv = ROPE_BASE ** (-jnp.arange(ROPE_FREQS, dtype=F32) / ROPE_FREQS)
    ar, ac = row[:, None] * inv, col[:, None] * inv
    cos = jnp.concatenate([jnp.cos(ar), jnp.cos(ar), jnp.cos(ac), jnp.cos(ac)], axis=1)
    sin = jnp.concatenate([-jnp.sin(ar), jnp.sin(ar), -jnp.sin(ac), jnp.sin(ac)], axis=1)
    cos = jnp.concatenate([jnp.ones((tc, HEAD_DIM), F32), cos], axis=0)
    sin = jnp.concatenate([jnp.zeros((tc, HEAD_DIM), F32), sin], axis=0)
    return jnp.tile(cos, (1, LANES // HEAD_DIM)), jnp.tile(sin, (1, LANES // HEAD_DIM))


def _block_diag_dense(w):
    nblk, h, k = w.shape
    return jnp.einsum('nhk,nm->nhmk', w, jnp.eye(nblk, dtype=w.dtype)).reshape(nblk * h, nblk * k)


def kernel(x, c, ctx, c_ctx, w_mod, b_mod, norm_mix, norm_ffn, final_norm, w_in_ab, lru_conv_w, lru_conv_b, lru_wa, lru_ba, lru_wx, lru_bx, lru_lam, attn_sink, w_out_ab, ffn_w1, ffn_w3, ffn_w2, w_in_cd, s5_a_re, s5_a_im, s5_log_step, s5_b_re, s5_b_im, s5_c_re, s5_c_im, s5_d, s5_w_glu, s5_b_glu, hg_lb_raw, hg_norm, w_out_cd, moe_router, moe_w1, moe_w3, moe_w2):
    nb, t, d = x.shape
    tc = ctx.shape[1]
    depth = w_mod.shape[0]
    assert tc % SEQ_BLOCK == 0 and t % SEQ_BLOCK == 0 and t % GRID_W == 0

    xa = jnp.concatenate([ctx, x], axis=1)
    mod_rows = -(-(nb + 1) // SUBLANES) * SUBLANES
    cvec = jnp.zeros((mod_rows, d), F32).at[:nb].set(c).at[nb].set(c_ctx)
    mod = _modulation(cvec, w_mod, b_mod).reshape(depth, mod_rows, N_MOD, d)

    cos_t, sin_t = _rope_tables(t, tc)
    lb_soft = jax.nn.softmax(hg_lb_raw.astype(F32), axis=0)
    lb_table = jnp.cumsum(lb_soft, axis=0) - lb_soft[0:1]
    n_exp = moe_router.shape[2]
    ffn_w = [w.astype(BF16) for w in (ffn_w1, ffn_w3, ffn_w2)]
    moe_w = [w.astype(BF16) for w in (moe_w1, moe_w3, moe_w2)]

    for l in range(depth):
        j = l // 2
        mod_l = mod[l]
        if l % 2 == 0:
            z = _inproj(xa, mod_l, norm_mix[l], w_in_ab[j].astype(BF16), tc, nb)
            lru = _lru(z, lru_conv_w[j], lru_conv_b[j],
                       jnp.stack([_block_diag_dense(lru_wa[j, dr]) for dr in range(2)]).astype(BF16),
                       lru_ba[j],
                       jnp.stack([_block_diag_dense(lru_wx[j, dr]) for dr in range(2)]).astype(BF16),
                       lru_bx[j], jax.nn.softplus(-lru_lam[j]), tc)
            sink = jnp.zeros((1, LANES), F32).at[0, :ATT_HEADS].set(attn_sink[j])
            att = _attention(z, cos_t, sin_t, sink, tc)
            xa = _outproj_ab(xa, mod_l, lru[0], lru[1], z, att, w_out_ab[j].astype(BF16), tc, nb)
            xa = _ffn(xa, mod_l, norm_ffn[l], ffn_w[0], ffn_w[1], ffn_w[2], j, tc, nb)
        else:
            z = _inproj(xa, mod_l, norm_mix[l], w_in_cd[j].astype(BF16), tc, nb)
            s5w = _s5_weights(s5_a_re[j], s5_a_im[j], s5_log_step[j], s5_b_re[j], s5_b_im[j],
                              s5_c_re[j], s5_c_im[j])
            y5 = _s5(z, s5w, s5_d[j], tc, s5_d.shape[1])
            o_f, o_b = _gla(z, lb_table[j], tc)
            xa = _outproj_cd(xa, mod_l, y5, o_f, o_b, z, s5_w_glu[j].astype(BF16), s5_b_glu[j],
                             hg_norm[j], w_out_cd[j].astype(BF16), tc, nb)
            router_t = jnp.zeros((SUBLANES, d), F32).at[:n_exp].set(moe_router[j].T)
            xa = _moe(xa, mod_l, norm_ffn[l], router_t, moe_w[0], moe_w[1], moe_w[2], j, tc, nb)
    return _final_norm(xa, final_norm, tc)
```

```python
import functools
import math

import jax
import jax.numpy as jnp
from jax import lax
from jax.experimental import pallas as pl
from jax.experimental.pallas import tpu as pltpu

F32 = jnp.float32
BF16 = jnp.bfloat16

EPS = 1e-6
GRID_W = 64
LRU_BLOCKS = 8
LRU_C = 8.0
CONV_W = 4
ATT_HEADS = 8
ATT_KV_HEADS = 2
ATT_GROUP = ATT_HEADS // ATT_KV_HEADS
HEAD_DIM = 64
WINDOW = 128
BLOCK_Q = 128
ROPE_FREQS = HEAD_DIM // 4
ROPE_BASE = 10000.0
S5_GROUP = 16
S5_STATE = 64
S5_CHUNK = 8
HG_HEADS = 4
HG_DK = 128
HG_CHUNK = 64
SEQ_BLOCK = 256
N_MOD = 6
MOE_PASS_ROWS = 256
LANES = 128
SUBLANES = 8
MXU_TILE = 256
VMEM_LIMIT = 56 * 1024 * 1024
NEG = -1e30
LOG2E = math.log2(math.e)


def _cparams(sem):
    return pltpu.CompilerParams(dimension_semantics=sem, vmem_limit_bytes=VMEM_LIMIT)


def _row_tile(ttot):
    for tm in (768, 1024, 512, 256):
        if ttot % tm == 0:
            return tm
    raise ValueError(f"unsupported token count {ttot}")


def _tall_row_tile(ttot):
    for tm in range(min(ttot, 1088) // 16 * 16, 15, -16):
        if ttot % tm == 0:
            return tm
    raise ValueError(f"unsupported token count {ttot}")


def _sigmoid(x):
    return 0.5 * jnp.tanh(0.5 * x) + 0.5


def _silu(x):
    return x * _sigmoid(x)


def _gelu(x):
    return 0.5 * x * (1.0 + jnp.tanh(math.sqrt(2.0 / math.pi) * (x + 0.044715 * (x * x * x))))


def _norm_mod(x, g, mb, mc, row0, tc, shift_idx, scale_idx):
    ms = jnp.mean(x * x, axis=-1, keepdims=True)
    y = x * lax.rsqrt(ms + EPS) * g
    rows = row0 + lax.broadcasted_iota(jnp.int32, (x.shape[0], 1), 0)
    is_ctx = rows < tc
    scale = jnp.where(is_ctx, mc[scale_idx:scale_idx + 1], mb[scale_idx:scale_idx + 1])
    shift = jnp.where(is_ctx, mc[shift_idx:shift_idx + 1], mb[shift_idx:shift_idx + 1])
    return y * (1.0 + scale) + shift


def _gate_mod(mb, mc, idx, row0, tc, n):
    rows = row0 + lax.broadcasted_iota(jnp.int32, (n, 1), 0)
    return jnp.where(rows < tc, mc[idx:idx + 1], mb[idx:idx + 1])


def _mod_kernel(c_ref, w_ref, b_ref, o_ref):
    s = _silu(c_ref[...])
    o_ref[...] = jnp.dot(s, w_ref[...], preferred_element_type=F32,
                         precision=lax.Precision.HIGHEST) + b_ref[...]


def _modulation(cvec, w_mod, b_mod):
    depth, d, n = w_mod.shape
    tn = 1536 if n % 1536 == 0 else n
    rows = cvec.shape[0]
    return pl.pallas_call(
        _mod_kernel,
        grid=(depth, n // tn),
        in_specs=[pl.BlockSpec((rows, d), lambda l, j: (0, 0)),
                  pl.BlockSpec((None, d, tn), lambda l, j: (l, 0, j)),
                  pl.BlockSpec((None, 1, tn), lambda l, j: (l, 0, j))],
        out_specs=pl.BlockSpec((None, rows, tn), lambda l, j: (l, 0, j)),
        out_shape=jax.ShapeDtypeStruct((depth, rows, n), F32),
        compiler_params=_cparams(("arbitrary", "arbitrary")),
        name="modulation",
    )(cvec, w_mod, b_mod.reshape(depth, 1, n))


def _inproj_kernel(x_ref, mb_ref, mc_ref, g_ref, w_ref, o_ref, *, tm, tc):
    i = pl.program_id(1)
    h = _norm_mod(x_ref[...], g_ref[...], mb_ref[...], mc_ref[...], i * tm, tc, 0, 1)
    o_ref[...] = jnp.dot(h.astype(BF16), w_ref[...], preferred_element_type=F32)


def _inproj(xa, mod_l, g, w, tc, nb):
    b, ttot, d = xa.shape
    n = w.shape[1]
    tm = _row_tile(ttot)
    return pl.pallas_call(
        functools.partial(_inproj_kernel, tm=tm, tc=tc),
        grid=(b, ttot // tm),
        in_specs=[pl.BlockSpec((None, tm, d), lambda bi, i: (bi, i, 0)),
                  pl.BlockSpec((None, N_MOD, d), lambda bi, i: (bi, 0, 0)),
                  pl.BlockSpec((None, N_MOD, d), lambda bi, i: (nb, 0, 0)),
                  pl.BlockSpec((1, d), lambda bi, i: (0, 0)),
                  pl.BlockSpec((d, n), lambda bi, i: (0, 0))],
        out_specs=pl.BlockSpec((None, tm, n), lambda bi, i: (bi, i, 0)),
        out_shape=jax.ShapeDtypeStruct((b, ttot, n), F32),
        compiler_params=_cparams(("parallel", "parallel")),
        name="inproj",
    )(xa, mod_l, mod_l, g.reshape(1, d), w)


def _seq_block(step, n_ctx, n_all, reverse):
    if not reverse:
        return step
    return jnp.where(step < n_ctx, n_ctx - 1 - step, n_all - 1 - step + n_ctx)


def _lru_kernel(*refs, lc, n_ctx, n_all):
    cw_ref, cb_ref = refs[:2]
    n_in = 8
    ins = [refs[2 + d * n_in:2 + (d + 1) * n_in] for d in range(2)]
    outs = refs[2 + 2 * n_in:4 + 2 * n_in]
    scratch = refs[4 + 2 * n_in:]
    for d in range(2):
        _lru_block(*ins[d], cw_ref, cb_ref, outs[d], *scratch[4 * d:4 * d + 4],
                   lc=lc, n_ctx=n_ctx, n_all=n_all, reverse=bool(d))


def _lru_block(u_ref, hp_ref, hn_ref, wa_ref, ba_ref, wx_ref, bx_ref, sp_ref, cw_ref, cb_ref,
               o_ref, uc_ref, a_ref, b_ref, h_ref, *, lc, n_ctx, n_all, reverse):
    step = pl.program_id(1)
    blk = _seq_block(step, n_ctx, n_all, reverse)
    has_prev = jnp.logical_and(blk != 0, blk != n_ctx)
    has_next = jnp.logical_and(blk != n_ctx - 1, blk != n_all - 1)

    @pl.when(step == 0)
    def _():
        h_ref[...] = jnp.zeros_like(h_ref)

    w0, w1, w2, w3 = (cw_ref[j:j + 1, :] for j in range(CONV_W))
    cb = cb_ref[...]
    u = u_ref[...]
    uc_ref[...] = (cb + pltpu.roll(u, 2, 0) * w0 + pltpu.roll(u, 1, 0) * w1 + u * w2
                   + pltpu.roll(u, lc - 1, 0) * w3)
    row = lax.broadcasted_iota(jnp.int32, (SUBLANES, 1), 0)
    hp = jnp.where(has_prev, hp_ref[...], 0.0)
    hn = jnp.where(has_next, hn_ref[...], 0.0)
    u0 = u_ref[0:SUBLANES, :]
    u1 = u_ref[SUBLANES:2 * SUBLANES, :]
    uc_ref[0:SUBLANES, :] = (
        cb + jnp.where(row < 2, pltpu.roll(hp, 2, 0), pltpu.roll(u0, 2, 0)) * w0
        + jnp.where(row < 1, pltpu.roll(hp, 1, 0), pltpu.roll(u0, 1, 0)) * w1 + u0 * w2
        + jnp.where(row < SUBLANES - 1, pltpu.roll(u0, SUBLANES - 1, 0),
                    pltpu.roll(u1, SUBLANES - 1, 0)) * w3)
    ul = u_ref[lc - SUBLANES:lc, :]
    um = u_ref[lc - 2 * SUBLANES:lc - SUBLANES, :]
    uc_ref[lc - SUBLANES:lc, :] = (
        cb + jnp.where(row < 2, pltpu.roll(um, 2, 0), pltpu.roll(ul, 2, 0)) * w0
        + jnp.where(row < 1, pltpu.roll(um, 1, 0), pltpu.roll(ul, 1, 0)) * w1 + ul * w2
        + jnp.where(row < SUBLANES - 1, pltpu.roll(ul, SUBLANES - 1, 0),
                    pltpu.roll(hn, SUBLANES - 1, 0)) * w3)

    uc = uc_ref[...]
    ub = uc.astype(BF16)
    r = _sigmoid(jnp.dot(ub, wa_ref[...], preferred_element_type=F32) + ba_ref[...])
    gi = _sigmoid(jnp.dot(ub, wx_ref[...], preferred_element_type=F32) + bx_ref[...])
    log_a = -LRU_C * r * sp_ref[...]
    a = jnp.exp(log_a)
    b = jnp.sqrt(1.0 - a * a) * (gi * uc)
    sub = lax.broadcasted_iota(jnp.int32, (lc, 1), 0) % SUBLANES
    dist = 1
    while dist < SUBLANES:
        inside = (sub < SUBLANES - dist) if reverse else (sub >= dist)
        shift = (lc - dist) if reverse else dist
        a_sh = jnp.where(inside, pltpu.roll(a, shift, 0), 1.0)
        b_sh = jnp.where(inside, pltpu.roll(b, shift, 0), 0.0)
        b = a * b_sh + b
        a = a * a_sh
        dist *= 2
    a_ref[...] = a
    b_ref[...] = b
    n_grp = lc // SUBLANES

    def body(i, carry):
        t = (n_grp - 1 - i) if reverse else i
        rows = pl.ds(pl.multiple_of(t * SUBLANES, SUBLANES), SUBLANES)
        h = a_ref[rows, :] * carry + b_ref[rows, :]
        b_ref[rows, :] = h
        return h[0:1, :] if reverse else h[SUBLANES - 1:SUBLANES, :]

    h_ref[0:1, :] = lax.fori_loop(0, n_grp, body, h_ref[0:1, :], unroll=4)
    o_ref[...] = b_ref[...].astype(o_ref.dtype)


def _lru(z, conv_w, conv_b, wa_bd, ba, wx_bd, bx, sp, tc):
    b, ttot, _ = z.shape
    w = conv_w.shape[1]
    lc = SEQ_BLOCK
    n_all, n_ctx = ttot // lc, tc // lc
    hb = lc // SUBLANES
    n_h = ttot // SUBLANES
    col = 1

    def direction(d):
        def blk(s):
            return _seq_block(s, n_ctx, n_all, bool(d))

        vec = pl.BlockSpec((None, 1, w), lambda bi, s: (d, 0, 0))
        mat = pl.BlockSpec((None, w, w), lambda bi, s: (d, 0, 0))
        specs = [pl.BlockSpec((None, lc, w), lambda bi, s: (bi, blk(s), col)),
                 pl.BlockSpec((None, SUBLANES, w),
                              lambda bi, s: (bi, jnp.maximum(blk(s) * hb - 1, 0), col)),
                 pl.BlockSpec((None, SUBLANES, w),
                              lambda bi, s: (bi, jnp.minimum((blk(s) + 1) * hb, n_h - 1), col)),
                 mat, vec, mat, vec, vec]
        return specs, pl.BlockSpec((None, lc, w), lambda bi, s: (bi, blk(s), 0))

    (in_f, out_f), (in_r, out_r) = direction(0), direction(1)
    out = jax.ShapeDtypeStruct((b, ttot, w), BF16)
    per_dir = [pltpu.VMEM((lc, w), F32)] * 3 + [pltpu.VMEM((SUBLANES, w), F32)]
    vecs = lambda x: x.reshape(2, 1, w)
    args = (z, z, z, wa_bd, vecs(ba), wx_bd, vecs(bx), vecs(sp))
    return pl.pallas_call(
        functools.partial(_lru_kernel, lc=lc, n_ctx=n_ctx, n_all=n_all),
        grid=(b, n_all),
        in_specs=[pl.BlockSpec((CONV_W, w), lambda bi, s: (0, 0)),
                  pl.BlockSpec((1, w), lambda bi, s: (0, 0))] + in_f + in_r,
        out_specs=[out_f, out_r],
        out_shape=[out, out],
        scratch_shapes=per_dir + per_dir,
        compiler_params=_cparams(("parallel", "arbitrary")),
        name="rglru_bidir",
    )(conv_w, conv_b.reshape(1, w), *args, *args)


def _rope(x, cos, sin_signed):
    lane = lax.broadcasted_iota(jnp.int32, x.shape, 1)
    swapped = jnp.where(lane % 32 < 16, pltpu.roll(x, LANES - 16, 1), pltpu.roll(x, 16, 1))
    return x * cos + swapped * sin_signed


def _pair_layout(x, x_sw, head):
    lo = lax.broadcasted_iota(jnp.int32, x.shape, 1) < HEAD_DIM
    first, second = (x, x_sw) if head == 0 else (x_sw, x)
    return jnp.concatenate([jnp.where(lo, first, 0.0), jnp.where(lo, 0.0, second)],
                           axis=0).astype(BF16)


def _attend(q_block, keys, vals, bias, sink_row, o_ref):
    nk = keys.shape[0]
    keys_sw = pltpu.roll(keys, HEAD_DIM, 1)
    vals_t = vals.T
    top = lax.broadcasted_iota(jnp.int32, (LANES, BLOCK_Q), 0) < HEAD_DIM
    zero_t = jnp.zeros((HEAD_DIM, nk), F32)
    pairs_per_kv = ATT_GROUP // 2
    for kv in range(ATT_KV_HEADS):
        kab = _pair_layout(keys, keys_sw, kv)
        vt = vals_t[kv * HEAD_DIM:(kv + 1) * HEAD_DIM, :]
        vab_t = jnp.concatenate([jnp.concatenate([vt, zero_t], axis=1),
                                 jnp.concatenate([zero_t, vt], axis=1)], axis=0).astype(BF16)
        for g in range(kv * pairs_per_kv, (kv + 1) * pairs_per_kv):
            s_t = lax.dot_general(kab, q_block(g), (((1,), (1,)), ((), ())),
                                  preferred_element_type=F32)
            halves = []
            recips = []
            for half in range(2):
                sh = s_t[half * nk:(half + 1) * nk, :]
                if bias is not None:
                    sh = sh + bias
                sk = sink_row[:, 2 * g + half:2 * g + half + 1]
                m = jnp.maximum(jnp.max(sh, axis=0, keepdims=True), sk)
                p = jnp.exp2(sh - m)
                recips.append(1.0 / (jnp.sum(p, axis=0, keepdims=True) + jnp.exp2(sk - m)))
                halves.append(p.astype(BF16))
            o_t = jnp.dot(vab_t, jnp.concatenate(halves, axis=0), preferred_element_type=F32)
            o_t = o_t * jnp.where(top, recips[0], recips[1])
            o_ref[:, g * LANES:(g + 1) * LANES] = o_t.T.astype(o_ref.dtype)


def _attn_kernel(q_ref, kp_ref, k0_ref, kn_ref, vp_ref, v0_ref, vn_ref, kc_ref, vc_ref,
                 cq_ref, sq_ref, cp_ref, sp_ref, cn_ref, sn_ref, sink_ref, o_ref, *, n_ctx, n_lat):
    i = pl.program_id(1)
    qscale = HEAD_DIM ** -0.5 * LOG2E
    sink_row = sink_ref[...] * LOG2E

    @pl.when(i < n_ctx)
    def _():
        def q_block(g):
            return (q_ref[:, g * LANES:(g + 1) * LANES] * qscale).astype(BF16)
        _attend(q_block, kc_ref[...], vc_ref[...], None, sink_row, o_ref)

    @pl.when(i >= n_ctx)
    def _():
        n = i - n_ctx
        cq, sq = cq_ref[...], sq_ref[...]

        def q_block(g):
            return (_rope(q_ref[:, g * LANES:(g + 1) * LANES], cq, sq) * qscale).astype(BF16)

        kp = _rope(kp_ref[...], cp_ref[...], sp_ref[...])
        k0 = _rope(k0_ref[...], cq, sq)
        kn = _rope(kn_ref[...], cn_ref[...], sn_ref[...])
        keys = jnp.concatenate([kp, k0, kn, kc_ref[...]], axis=0)
        vals = jnp.concatenate([vp_ref[...], v0_ref[...], vn_ref[...], vc_ref[...]], axis=0)
        nk = keys.shape[0]
        qpos = n * BLOCK_Q + lax.broadcasted_iota(jnp.int32, (nk, BLOCK_Q), 1)
        key = lax.broadcasted_iota(jnp.int32, (nk, BLOCK_Q), 0)
        kpos = (n - 1) * BLOCK_Q + key
        local = (jnp.abs(qpos - kpos) <= WINDOW) & (kpos >= 0) & (kpos < n_lat * BLOCK_Q)
        bias = jnp.where(local | (key >= 3 * BLOCK_Q), 0.0, NEG)
        _attend(q_block, keys, vals, bias, sink_row, o_ref)


def _attention(z, cos_t, sin_t, sink, tc):
    b, ttot, _ = z.shape
    n_all, n_ctx = ttot // BLOCK_Q, tc // BLOCK_Q
    n_lat = n_all - n_ctx
    kcol = (2 * 512 + ATT_HEADS * HEAD_DIM) // LANES
    vcol = kcol + 1
    qcol = 2 * 512 // (ATT_HEADS * HEAD_DIM)

    def prev(i):
        return jnp.maximum(i - 1, 0)

    def nxt(i):
        return jnp.minimum(i + 1, n_all - 1)

    def rows(col, f):
        return pl.BlockSpec((None, BLOCK_Q, LANES), lambda bi, i: (bi, f(i), col))

    def tab(f):
        return pl.BlockSpec((BLOCK_Q, LANES), lambda bi, i: (f(i), 0))

    same = lambda i: i
    return pl.pallas_call(
        functools.partial(_attn_kernel, n_ctx=n_ctx, n_lat=n_lat),
        grid=(b, n_all),
        in_specs=[pl.BlockSpec((None, BLOCK_Q, ATT_HEADS * HEAD_DIM), lambda bi, i: (bi, i, qcol)),
                  rows(kcol, prev), rows(kcol, same), rows(kcol, nxt),
                  rows(vcol, prev), rows(vcol, same), rows(vcol, nxt),
                  pl.BlockSpec((None, tc, LANES), lambda bi, i: (bi, 0, kcol)),
                  pl.BlockSpec((None, tc, LANES), lambda bi, i: (bi, 0, vcol)),
                  tab(same), tab(same), tab(prev), tab(prev), tab(nxt), tab(nxt),
                  pl.BlockSpec((1, LANES), lambda bi, i: (0, 0))],
        out_specs=pl.BlockSpec((None, BLOCK_Q, ATT_HEADS * HEAD_DIM), lambda bi, i: (bi, i, 0)),
        out_shape=jax.ShapeDtypeStruct((b, ttot, ATT_HEADS * HEAD_DIM), BF16),
        compiler_params=_cparams(("parallel", "parallel")),
        name="window_gqa",
    )(z, z, z, z, z, z, z, z, z, cos_t, sin_t, cos_t, sin_t, cos_t, sin_t, sink)


def _outproj_ab_kernel(x_ref, mb_ref, mc_ref, lf_ref, lb_ref, g_ref, at_ref, w_ref, o_ref, *, tm, tc):
    i = pl.program_id(1)
    lru = (lf_ref[...].astype(F32) + lb_ref[...].astype(F32)) * _gelu(g_ref[...])
    mix = jnp.concatenate([lru.astype(BF16), at_ref[...]], axis=1)
    dx = jnp.dot(mix, w_ref[...], preferred_element_type=F32)
    gate = _gate_mod(mb_ref[...], mc_ref[...], 2, i * tm, tc, tm)
    o_ref[...] = x_ref[...] + gate * dx


def _outproj_ab(xa, mod_l, lru_f, lru_b, z, att, w_out, tc, nb):
    b, ttot, d = xa.shape
    w = lru_f.shape[2]
    tm = _row_tile(ttot)
    tile = lambda width, col: pl.BlockSpec((None, tm, width), lambda bi, i: (bi, i, col))
    return pl.pallas_call(
        functools.partial(_outproj_ab_kernel, tm=tm, tc=tc),
        grid=(b, ttot // tm),
        in_specs=[tile(d, 0),
                  pl.BlockSpec((None, N_MOD, d), lambda bi, i: (bi, 0, 0)),
                  pl.BlockSpec((None, N_MOD, d), lambda bi, i: (nb, 0, 0)),
                  tile(w, 0), tile(w, 0), tile(w, 0), tile(att.shape[2], 0),
                  pl.BlockSpec(w_out.shape, lambda bi, i: (0, 0))],
        out_specs=tile(d, 0),
        out_shape=jax.ShapeDtypeStruct(xa.shape, F32),
        compiler_params=_cparams(("parallel", "parallel")),
        name="outproj_ab",
    )(xa, mod_l, mod_l, lru_f, lru_b, z, att, w_out)


def _outproj_cd_kernel(x_ref, mb_ref, mc_ref, y_ref, of_ref, ob_ref, g_ref, wg_ref, bg_ref, gn_ref,
                       w_ref, o_ref, *, tm, tc):
    i = pl.program_id(1)
    y = _gelu(y_ref[...])
    s5 = y * _sigmoid(jnp.dot(y.astype(BF16), wg_ref[...], preferred_element_type=F32) + bg_ref[...])
    o = of_ref[...].astype(F32) + ob_ref[...].astype(F32)
    parts = []
    for h in range(HG_HEADS):
        oh = o[:, h * HG_DK:(h + 1) * HG_DK]
        ms = jnp.mean(oh * oh, axis=-1, keepdims=True)
        parts.append(oh * lax.rsqrt(ms + EPS) * gn_ref[...])
    hg = jnp.concatenate(parts, axis=1) * _silu(g_ref[...])
    mix = jnp.concatenate([s5.astype(BF16), hg.astype(BF16)], axis=1)
    dx = jnp.dot(mix, w_ref[...], preferred_element_type=F32)
    gate = _gate_mod(mb_ref[...], mc_ref[...], 2, i * tm, tc, tm)
    o_ref[...] = x_ref[...] + gate * dx


def _outproj_cd(xa, mod_l, y5, o_f, o_b, z, w_glu, b_glu, hg_norm, w_out, tc, nb):
    b, ttot, d = xa.shape
    w = y5.shape[2]
    tm = _row_tile(ttot)
    gcol = z.shape[2] // w - 1
    tile = lambda width, col: pl.BlockSpec((None, tm, width), lambda bi, i: (bi, i, col))
    return pl.pallas_call(
        functools.partial(_outproj_cd_kernel, tm=tm, tc=tc),
        grid=(b, ttot // tm),
        in_specs=[tile(d, 0),
                  pl.BlockSpec((None, N_MOD, d), lambda bi, i: (bi, 0, 0)),
                  pl.BlockSpec((None, N_MOD, d), lambda bi, i: (nb, 0, 0)),
                  tile(w, 0), tile(w, 0), tile(w, 0), tile(w, gcol),
                  pl.BlockSpec(w_glu.shape, lambda bi, i: (0, 0)),
                  pl.BlockSpec((1, w), lambda bi, i: (0, 0)),
                  pl.BlockSpec((1, HG_DK), lambda bi, i: (0, 0)),
                  pl.BlockSpec(w_out.shape, lambda bi, i: (0, 0))],
        out_specs=tile(d, 0),
        out_shape=jax.ShapeDtypeStruct(xa.shape, F32),
        compiler_params=_cparams(("parallel", "parallel")),
        name="outproj_cd",
    )(xa, mod_l, mod_l, y5, o_f, o_b, z, w_glu, b_glu.reshape(1, w), hg_norm.reshape(1, HG_DK), w_out)


def _ffn_kernel(x_ref, mb_ref, mc_ref, g_ref, w1_ref, w3_ref, w2_ref, o_ref, h_ref, acc_ref, *, tm, tc):
    i = pl.program_id(1)
    k = pl.program_id(2)

    @pl.when(k == 0)
    def _():
        h = _norm_mod(x_ref[...], g_ref[...], mb_ref[...], mc_ref[...], i * tm, tc, 3, 4)
        h_ref[...] = h.astype(BF16)
        acc_ref[...] = jnp.zeros_like(acc_ref)

    h = h_ref[...]
    tf = w1_ref.shape[1]
    half = (tf // LANES + 1) // 2 * LANES
    y = None
    for f0, f1 in ([(0, half), (half, tf)] if 0 < half < tf else [(0, tf)]):
        a = jnp.dot(h, w1_ref[:, f0:f1], preferred_element_type=F32)
        c = jnp.dot(h, w3_ref[:, f0:f1], preferred_element_type=F32)
        part = jnp.dot((_silu(a) * c).astype(BF16), w2_ref[f0:f1, :], preferred_element_type=F32)
        y = part if y is None else y + part
    acc_ref[...] += y

    @pl.when(k == pl.num_programs(2) - 1)
    def _():
        gate = _gate_mod(mb_ref[...], mc_ref[...], 5, i * tm, tc, tm)
        o_ref[...] = x_ref[...] + gate * acc_ref[...]


def _ffn(xa, mod_l, g, w1, w3, w2, layer, tc, nb):
    b, ttot, d = xa.shape
    f = w1.shape[2]
    tm = _tall_row_tile(ttot)
    tf = f // 2 if (f // 2) % LANES == 0 else f
    return pl.pallas_call(
        functools.partial(_ffn_kernel, tm=tm, tc=tc),
        grid=(b, ttot // tm, f // tf),
        in_specs=[pl.BlockSpec((None, tm, d), lambda bi, i, k: (bi, i, 0)),
                  pl.BlockSpec((None, N_MOD, d), lambda bi, i, k: (bi, 0, 0)),
                  pl.BlockSpec((None, N_MOD, d), lambda bi, i, k: (nb, 0, 0)),
                  pl.BlockSpec((1, d), lambda bi, i, k: (0, 0)),
                  pl.BlockSpec((None, d, tf), lambda bi, i, k: (layer, 0, k)),
                  pl.BlockSpec((None, d, tf), lambda bi, i, k: (layer, 0, k)),
                  pl.BlockSpec((None, tf, d), lambda bi, i, k: (layer, k, 0))],
        out_specs=pl.BlockSpec((None, tm, d), lambda bi, i, k: (bi, i, 0)),
        out_shape=jax.ShapeDtypeStruct(xa.shape, F32),
        scratch_shapes=[pltpu.VMEM((tm, d), BF16), pltpu.VMEM((tm, d), F32)],
        compiler_params=_cparams(("parallel", "parallel", "arbitrary")),
        name="ffn_swiglu",
    )(xa, mod_l, mod_l, g.reshape(1, d), w1, w3, w2)


def _router_kernel(x_ref, mb_ref, mc_ref, g_ref, r_ref, h_ref, wt_ref, cnt_ref, *, tm, tc, n_exp):
    i = pl.program_id(1)
    h = _norm_mod(x_ref[...], g_ref[...], mb_ref[...], mc_ref[...], i * tm, tc, 3, 4)
    nt = (((1,), (1,)), ((), ()))
    h_hi = h.astype(BF16)
    h_lo = (h - h_hi.astype(F32)).astype(BF16)
    r = r_ref[...]
    r_hi = r.astype(BF16)
    r_lo = (r - r_hi.astype(F32)).astype(BF16)
    logits = (lax.dot_general(r_hi, h_hi, nt, preferred_element_type=F32)
              + lax.dot_general(r_lo, h_hi, nt, preferred_element_type=F32)
              + lax.dot_general(r_hi, h_lo, nt, preferred_element_type=F32))
    sub = lax.broadcasted_iota(jnp.int32, logits.shape, 0)
    h_ref[...] = h_hi
    logits = jnp.where(sub < n_exp, logits, NEG)
    m1 = jnp.max(logits, axis=0, keepdims=True)
    i1 = jnp.min(jnp.where(logits == m1, sub, SUBLANES), axis=0, keepdims=True)
    rest = jnp.where(sub == i1, NEG, logits)
    m2 = jnp.max(rest, axis=0, keepdims=True)
    i2 = jnp.min(jnp.where(rest == m2, sub, SUBLANES), axis=0, keepdims=True)
    e2 = jnp.exp(m2 - m1)
    g1 = 1.0 / (1.0 + e2)
    wt = jnp.where(sub == i1, g1, 0.0) + jnp.where(sub == i2, e2 * g1, 0.0)
    wt_ref[...] = wt
    cnt = jnp.sum(jnp.where(wt > 0.0, 1.0, 0.0), axis=1, keepdims=True)
    cnt_ref[...] = jnp.broadcast_to(cnt, cnt_ref.shape).astype(jnp.int32)


def _router(xa, mod_l, g, router_t, tc, nb, n_exp):
    b, ttot, d = xa.shape
    tm = _row_tile(ttot)
    n_t = ttot // tm
    return pl.pallas_call(
        functools.partial(_router_kernel, tm=tm, tc=tc, n_exp=n_exp),
        grid=(b, n_t),
        in_specs=[pl.BlockSpec((None, tm, d), lambda bi, i: (bi, i, 0)),
                  pl.BlockSpec((None, N_MOD, d), lambda bi, i: (bi, 0, 0)),
                  pl.BlockSpec((None, N_MOD, d), lambda bi, i: (nb, 0, 0)),
                  pl.BlockSpec((1, d), lambda bi, i: (0, 0)),
                  pl.BlockSpec((SUBLANES, d), lambda bi, i: (0, 0))],
        out_specs=[pl.BlockSpec((tm, d), lambda bi, i: (bi * n_t + i, 0)),
                   pl.BlockSpec((SUBLANES, tm), lambda bi, i: (0, bi * n_t + i)),
                   pl.BlockSpec((None, SUBLANES, LANES), lambda bi, i: (bi * n_t + i, 0, 0))],
        out_shape=[jax.ShapeDtypeStruct((b * ttot, d), BF16),
                   jax.ShapeDtypeStruct((SUBLANES, b * ttot), F32),
                   jax.ShapeDtypeStruct((b * n_t, SUBLANES, LANES), jnp.int32)],
        compiler_params=_cparams(("parallel", "parallel")),
        name="moe_router",
    )(xa, mod_l, mod_l, g.reshape(1, d), router_t)


def _moe_kernel(cnt_ref, x_ref, *refs, tm, tc, n_exp, n_t, group):
    mb_refs = refs[:group]
    mc_ref, h_ref, wt_ref, w1_ref, w3_ref, w2_ref, o_ref, rank_ref = refs[group:]
    p = pl.program_id(0)
    e = pl.program_id(1)
    f = w1_ref.shape[1]
    half = (f // LANES + 1) // 2 * LANES
    f_split = [(0, half), (half, f)] if 0 < half < f else [(0, f)]

    @pl.when(e == 0)
    def _():
        o_ref[...] = jnp.zeros_like(o_ref)
        r_i = lax.broadcasted_iota(jnp.int32, (tm, tm), 0)
        c_i = lax.broadcasted_iota(jnp.int32, (tm, tm), 1)
        before = jnp.where(r_i < c_i, 1.0, 0.0).astype(BF16)
        for j in range(group):
            sel = wt_ref[:, j * tm:(j + 1) * tm] > 0.0
            rank = jnp.dot(jnp.where(sel, 1.0, 0.0).astype(BF16), before, preferred_element_type=F32)
            rank_ref[j] = jnp.where(sel, rank, -1.0)

    counts = [cnt_ref[(p * group + j) * n_exp + e] for j in range(group)]

    offsets = [jnp.int32(0)]
    for j in range(group - 1):
        offsets.append(offsets[-1] + counts[j])
    total = offsets[-1] + counts[-1]

    def expert_pass(m_rows, base):
        slot = (lax.broadcasted_iota(jnp.int32, (m_rows, tm), 0) + base).astype(F32)
        onehots = []
        xe = None
        gate = None
        for j in range(group):
            r_row = rank_ref[j, pl.ds(e, 1), :]
            r_row = jnp.where(r_row >= 0.0, r_row + offsets[j].astype(F32), -1.0)
            w_row = wt_ref[pl.ds(e, 1), j * tm:(j + 1) * tm]
            hit = r_row == slot
            onehot = jnp.where(hit, 1.0, 0.0).astype(BF16)
            onehots.append(onehot)
            xj = jnp.dot(onehot, h_ref[j * tm:(j + 1) * tm, :], preferred_element_type=F32)
            gj = jnp.sum(jnp.where(hit, w_row, 0.0), axis=1, keepdims=True)
            xe = xj if xe is None else xe + xj
            gate = gj if gate is None else gate + gj
        xe = xe.astype(BF16)
        y = None
        for f0, f1 in f_split:
            a = jnp.dot(xe, w1_ref[:, f0:f1], preferred_element_type=F32)
            c = jnp.dot(xe, w3_ref[:, f0:f1], preferred_element_type=F32)
            part = jnp.dot((_silu(a) * c * gate).astype(BF16), w2_ref[f0:f1, :],
                           preferred_element_type=F32)
            y = part if y is None else y + part
        y = y.astype(BF16)
        for j in range(group):
            o_ref[j * tm:(j + 1) * tm, :] += lax.dot_general(
                onehots[j], y, (((0,), (0,)), ((), ())), preferred_element_type=F32)

    n_full = total // MOE_PASS_ROWS

    def full_pass(it, carry):
        expert_pass(MOE_PASS_ROWS, it * MOE_PASS_ROWS)
        return carry

    lax.fori_loop(0, n_full, full_pass, 0)
    rest = total - n_full * MOE_PASS_ROWS

    @pl.when(rest > MOE_PASS_ROWS // 2)
    def _():
        expert_pass(MOE_PASS_ROWS, n_full * MOE_PASS_ROWS)

    @pl.when(jnp.logical_and(rest > 0, rest <= MOE_PASS_ROWS // 2))
    def _():
        expert_pass(MOE_PASS_ROWS // 2, n_full * MOE_PASS_ROWS)

    @pl.when(e == n_exp - 1)
    def _():
        for j in range(group):
            i = (p * group + j) % n_t
            gate5 = _gate_mod(mb_refs[j][...], mc_ref[...], 5, i * tm, tc, tm)
            rows = slice(j * tm, (j + 1) * tm)
            o_ref[rows, :] = x_ref[rows, :] + gate5 * o_ref[rows, :]


def _moe(xa, mod_l, g, router_t, w1, w3, w2, layer, tc, nb):
    b, ttot, d = xa.shape
    _, n_exp, _, f = w1.shape
    tm = _row_tile(ttot)
    n_t = ttot // tm
    group = 2 if (b * n_t) % 2 == 0 else 1
    h, wt, cnt = _router(xa, mod_l, g, router_t, tc, nb, n_exp)
    cnt = cnt[:, :n_exp, 0].reshape(-1)
    rows = group * tm
    mods = [pl.BlockSpec((None, N_MOD, d), functools.partial(
        lambda p, e, c, j: ((p * group + j) // n_t, 0, 0), j=j)) for j in range(group)]
    grid_spec = pltpu.PrefetchScalarGridSpec(
        num_scalar_prefetch=1,
        grid=(b * n_t // group, n_exp),
        in_specs=[pl.BlockSpec((rows, d), lambda p, e, c: (p, 0), pipeline_mode=pl.Buffered(1))]
        + mods
        + [pl.BlockSpec((None, N_MOD, d), lambda p, e, c: (nb, 0, 0)),
           pl.BlockSpec((rows, d), lambda p, e, c: (p, 0)),
           pl.BlockSpec((SUBLANES, rows), lambda p, e, c: (0, p)),
           pl.BlockSpec((None, None, d, f), lambda p, e, c: (layer, e, 0, 0)),
           pl.BlockSpec((None, None, d, f), lambda p, e, c: (layer, e, 0, 0)),
           pl.BlockSpec((None, None, f, d), lambda p, e, c: (layer, e, 0, 0))],
        out_specs=pl.BlockSpec((rows, d), lambda p, e, c: (p, 0)),
        scratch_shapes=[pltpu.VMEM((group, SUBLANES, tm), F32)])
    out = pl.pallas_call(
        functools.partial(_moe_kernel, tm=tm, tc=tc, n_exp=n_exp, n_t=n_t, group=group),
        grid_spec=grid_spec,
        out_shape=jax.ShapeDtypeStruct((b * ttot, d), F32),
        compiler_params=_cparams(("parallel", "arbitrary")),
        name="moe_experts",
    )(cnt, xa.reshape(b * ttot, d), *([mod_l] * group), mod_l, h, wt, w1, w3, w2)
    return out.reshape(b, ttot, d)


def _spread_groups(x, inner, row_group, n_grp):
    cols_in = x.shape[2]
    cols_out = cols_in * n_grp
    q = jnp.arange(cols_out)
    src = (q // (n_grp * inner)) * inner + q % inner
    tile = (jnp.arange(cols_in)[:, None] == src[None, :]).astype(BF16)
    col_group = (q // inner) % n_grp
    y = jnp.dot(x.astype(BF16), tile)
    return jnp.where(row_group[:, None] == col_group[None, :], y, jnp.zeros((), BF16))


def _s5_weights(a_re, a_im, log_step, b_re, b_im, c_re, c_im):
    L = S5_CHUNK
    lr = jnp.minimum(a_re, -1e-4)
    li = a_im
    dt = jnp.exp(log_step)[..., None]
    mag, ang = lr * dt, li * dt
    lbr, lbi = jnp.exp(mag) * jnp.cos(ang), jnp.exp(mag) * jnp.sin(ang)
    zr, zi = lbr - 1.0, lbi
    den = lr * lr + li * li
    fr = (zr * lr + zi * li) / den
    fi = (zi * lr - zr * li) / den
    bbr = fr[..., None] * b_re - fi[..., None] * b_im
    bbi = fr[..., None] * b_im + fi[..., None] * b_re

    def power(p):
        p = p[..., None, None, None].astype(F32)
        return jnp.exp(mag * p) * jnp.cos(ang * p), jnp.exp(mag * p) * jnp.sin(ang * p)

    n_dir, n_grp, n_st = a_re.shape
    gpb = LANES // S5_GROUP
    n_blk = n_grp // gpb
    s = jnp.arange(L)
    in_group = (jnp.arange(L * LANES) % LANES) // S5_GROUP
    st_group = jnp.arange(gpb * n_st) // n_st
    outs = []
    for d in range(n_dir):
        pr, pi = power(jnp.arange(L))
        pr, pi = pr[:, d], pi[:, d]
        cbr = (jnp.einsum('gcn,tgn,gnk->tgck', c_re[d], pr, bbr[d])
               - jnp.einsum('gcn,tgn,gnk->tgck', c_re[d], pi, bbi[d])
               - jnp.einsum('gcn,tgn,gnk->tgck', c_im[d], pr, bbi[d])
               - jnp.einsum('gcn,tgn,gnk->tgck', c_im[d], pi, bbr[d]))
        lag = (s[None, :] - s[:, None]) if d == 0 else (s[:, None] - s[None, :])
        kern = jnp.where((lag >= 0)[..., None, None, None], cbr[jnp.clip(lag, 0, L - 1)], 0.0)
        kern = kern.reshape(L, L, n_blk, gpb, S5_GROUP, S5_GROUP).transpose(2, 0, 3, 5, 1, 4)
        m = _spread_groups(kern.reshape(n_blk, L * LANES, L * S5_GROUP), S5_GROUP, in_group, gpb)
        qr, qi = power((L - 1 - s) if d == 0 else s)
        qr, qi = qr[:, d], qi[:, d]
        str_ = qr[..., None] * bbr[d] - qi[..., None] * bbi[d]
        sti = qr[..., None] * bbi[d] + qi[..., None] * bbr[d]

        def to_state(x):
            x = x.reshape(L, n_blk, gpb, n_st, S5_GROUP).transpose(1, 0, 2, 4, 3)
            return _spread_groups(x.reshape(n_blk, L * LANES, n_st), n_st, in_group, gpb)

        wst = jnp.concatenate([to_state(str_), to_state(sti)], axis=2)
        rr, ri = power((s + 1) if d == 0 else (L - s))
        rr, ri = rr[:, d], ri[:, d]
        wr = c_re[d][None] * rr[:, :, None, :] - c_im[d][None] * ri[:, :, None, :]
        wi = -(c_re[d][None] * ri[:, :, None, :] + c_im[d][None] * rr[:, :, None, :])

        def from_state(x):
            x = x.reshape(L, n_blk, gpb, S5_GROUP, n_st).transpose(1, 2, 4, 0, 3)
            return _spread_groups(x.reshape(n_blk, gpb * n_st, L * S5_GROUP), S5_GROUP, st_group, gpb)

        wout = jnp.concatenate([from_state(wr), from_state(wi)], axis=1)
        ler, lei = jnp.exp(mag[d] * L) * jnp.cos(ang[d] * L), jnp.exp(mag[d] * L) * jnp.sin(ang[d] * L)
        lam_l = jnp.concatenate([ler.reshape(n_blk, 1, gpb * n_st), lei.reshape(n_blk, 1, gpb * n_st)],
                                axis=2)
        outs.append((m, wst, wout, lam_l))
    return tuple(jnp.stack([o[k] for o in outs]) for k in range(4))


def _s5_kernel(u_ref, m_ref, wst_ref, wout_ref, lam_ref, d_ref, o_ref, x_ref, hp_ref, *, rows, rows_ctx):
    L = S5_CHUNK
    dr = pl.program_id(2)
    ns = lam_ref.shape[1] // 2
    u = jnp.concatenate([u_ref[pl.ds(s, rows, stride=L), :] for s in range(L)], axis=1)
    ub = u.astype(BF16)
    x_ref[...] = jnp.dot(ub, wst_ref[...], preferred_element_type=F32)
    lr = lam_ref[:, 0:ns]
    li = lam_ref[:, ns:2 * ns]

    def visit(r, carry):
        hr, hi = carry
        hp_ref[pl.ds(r, 1), 0:ns] = hr
        hp_ref[pl.ds(r, 1), ns:2 * ns] = hi
        xr = x_ref[pl.ds(r, 1), 0:ns]
        xi = x_ref[pl.ds(r, 1), ns:2 * ns]
        return lr * hr - li * hi + xr, lr * hi + li * hr + xi

    zero = (jnp.zeros((1, ns), F32), jnp.zeros((1, ns), F32))

    @pl.when(dr == 0)
    def _():
        lax.fori_loop(0, rows, visit, zero, unroll=4)

    @pl.when(dr == 1)
    def _():
        c = lax.fori_loop(0, rows_ctx, lambda t, c: visit(rows_ctx - 1 - t, c), zero, unroll=4)
        lax.fori_loop(0, rows - rows_ctx, lambda t, c: visit(rows - 1 - t, c), c, unroll=4)

    y_state = jnp.dot(hp_ref[...].astype(BF16), wout_ref[...], preferred_element_type=F32)
    n_in = L * LANES

    def within_chunk(col0, col1, reverse):
        k0, k1 = (col0, n_in) if reverse else (0, col1)
        return jnp.dot(ub[:, k0:k1], m_ref[k0:k1, col0:col1], preferred_element_type=F32)

    @pl.when(dr == 0)
    def _():
        for c0 in range(0, n_in, MXU_TILE):
            y = within_chunk(c0, c0 + MXU_TILE, False) + y_state[:, c0:c0 + MXU_TILE]
            for s in range(c0 // LANES, (c0 + MXU_TILE) // LANES):
                lanes = slice(s * LANES - c0, (s + 1) * LANES - c0)
                o_ref[pl.ds(s, rows, stride=L), :] = (y[:, lanes]
                                                      + d_ref[...] * u[:, s * LANES:(s + 1) * LANES])

    @pl.when(dr == 1)
    def _():
        for c0 in range(0, n_in, MXU_TILE):
            y = within_chunk(c0, c0 + MXU_TILE, True) + y_state[:, c0:c0 + MXU_TILE]
            for s in range(c0 // LANES, (c0 + MXU_TILE) // LANES):
                lanes = slice(s * LANES - c0, (s + 1) * LANES - c0)
                o_ref[pl.ds(s, rows, stride=L), :] += y[:, lanes]


def _s5(z, weights, layer, dvec, tc, width):
    b, ttot, _ = z.shape
    m, wst, wout, lam_l = weights
    n_dir, n_blk = m.shape[1], m.shape[2]
    rows, rows_ctx = ttot // S5_CHUNK, tc // S5_CHUNK
    wspec = lambda a: pl.BlockSpec((None, None, None) + a.shape[3:],
                                   lambda bi, j, dr: (layer, dr, j, 0, 0))
    return pl.pallas_call(
        functools.partial(_s5_kernel, rows=rows, rows_ctx=rows_ctx),
        grid=(b, n_blk, n_dir),
        in_specs=[pl.BlockSpec((None, ttot, LANES), lambda bi, j, dr: (bi, 0, j)),
                  wspec(m), wspec(wst), wspec(wout), wspec(lam_l),
                  pl.BlockSpec((1, LANES), lambda bi, j, dr: (0, j))],
        out_specs=pl.BlockSpec((None, ttot, LANES), lambda bi, j, dr: (bi, 0, j)),
        out_shape=jax.ShapeDtypeStruct((b, ttot, width), F32),
        scratch_shapes=[pltpu.VMEM((rows, lam_l.shape[4]), F32), pltpu.VMEM((rows, lam_l.shape[4]), F32)],
        compiler_params=_cparams(("parallel", "parallel", "arbitrary")),
        name="s5_bidir",
    )(z, m, wst, wout, lam_l, dvec.reshape(1, width))


def _gla_kernel(qf_ref, ff_ref, vf_ref, qb_ref, fb_ref, vb_ref, lb_ref, of_ref, ob_ref,
                stf_ref, stb_ref, *, lc):
    step = pl.program_id(1)

    @pl.when(step == 0)
    def _():
        stf_ref[...] = jnp.zeros_like(stf_ref)
        stb_ref[...] = jnp.zeros_like(stb_ref)

    _gla_block(qf_ref, ff_ref, vf_ref, lb_ref, of_ref, stf_ref, lc=lc, reverse=False)
    _gla_block(qb_ref, fb_ref, vb_ref, lb_ref, ob_ref, stb_ref, lc=lc, reverse=True)


def _gla_block(q_ref, f_ref, v_ref, lb_ref, o_ref, st_ref, *, lc, reverse):
    c = HG_CHUNK
    n_chunks = lc // c
    r_i = lax.broadcasted_iota(jnp.int32, (lc, lc), 0)
    c_i = lax.broadcasted_iota(jnp.int32, (lc, lc), 1)
    same_chunk = (r_i // c) == (c_i // c)
    keep = same_chunk & ((c_i >= r_i) if reverse else (c_i <= r_i))
    tri = jnp.where(keep, 1.0, 0.0).astype(BF16)
    lb = lb_ref[...]
    q = _silu(q_ref[...])
    f = lb + (1.0 - lb) / (1.0 + jnp.exp(-f_ref[...]))
    k = 1.0 - f
    logf = jnp.log(f)
    v = v_ref[...].astype(BF16)
    hi = logf.astype(BF16)
    lo = (logf - hi.astype(F32)).astype(BF16)
    cum = jnp.dot(tri, hi, preferred_element_type=F32) + jnp.dot(tri, lo, preferred_element_type=F32)
    totals = [cum[ci * c:ci * c + 1, :] if reverse else cum[(ci + 1) * c - 1:(ci + 1) * c, :]
              for ci in range(n_chunks)]
    w = cum.shape[1]
    total = jnp.concatenate([jnp.broadcast_to(t, (c, w)) for t in totals], axis=0)
    centre = jnp.concatenate([jnp.broadcast_to(cum[ci * c + c // 2:ci * c + c // 2 + 1, :], (c, w))
                              for ci in range(n_chunks)], axis=0)
    q_in = (q * jnp.exp(cum)).astype(BF16)
    k_out = (k * jnp.exp(total - cum)).astype(BF16)
    q_loc = (q * jnp.exp(cum - centre)).astype(BF16)
    k_loc = (k * jnp.exp(centre - cum)).astype(BF16)
    decays = [jnp.exp(t) for t in totals]
    order = range(n_chunks - 1, -1, -1) if reverse else range(n_chunks)
    for h in range(HG_HEADS):
        sl = slice(h * HG_DK, (h + 1) * HG_DK)
        att = lax.dot_general(q_loc[:, sl], k_loc[:, sl], (((1,), (1,)), ((), ())),
                              preferred_element_type=F32)
        att = jnp.where(keep, att, 0.0)
        o_loc = jnp.dot(att.astype(BF16), v[:, sl], preferred_element_type=F32)
        incs = [lax.dot_general(v[ci * c:(ci + 1) * c, sl], k_out[ci * c:(ci + 1) * c, sl],
                                (((0,), (0,)), ((), ())), preferred_element_type=F32)
                for ci in range(n_chunks)]
        st = st_ref[h]
        for ci in order:
            rows = slice(ci * c, (ci + 1) * c)
            o = o_loc[rows] + lax.dot_general(q_in[rows, sl], st.astype(BF16),
                                              (((1,), (1,)), ((), ())), preferred_element_type=F32)
            o_ref[rows, sl] = o.astype(o_ref.dtype)
            st = st * decays[ci][:, sl] + incs[ci]
        st_ref[h] = st


def _gla(z, lb, tc):
    b, ttot, _ = z.shape
    w = HG_HEADS * HG_DK
    lc = SEQ_BLOCK
    n_all, n_ctx = ttot // lc, tc // lc

    def col(cidx, reverse):
        return pl.BlockSpec((None, lc, w),
                            lambda bi, s: (bi, _seq_block(s, n_ctx, n_all, reverse), cidx))

    out = jax.ShapeDtypeStruct((b, ttot, w), BF16)
    state = pltpu.VMEM((HG_HEADS, HG_DK, HG_DK), F32)
    return pl.pallas_call(
        functools.partial(_gla_kernel, lc=lc),
        grid=(b, n_all),
        in_specs=[col(1, False), col(2, False), col(4, False),
                  col(1, True), col(3, True), col(4, True),
                  pl.BlockSpec((1, w), lambda bi, s: (0, 0))],
        out_specs=[col(0, False), col(0, True)],
        out_shape=[out, out],
        scratch_shapes=[state, state],
        compiler_params=_cparams(("parallel", "arbitrary")),
        name="hgrn2_bidir",
    )(z, z, z, z, z, z, lb.reshape(1, w))


def _final_kernel(x_ref, g_ref, o_ref):
    x = x_ref[0]
    ms = jnp.mean(x * x, axis=-1, keepdims=True)
    o_ref[...] = x * lax.rsqrt(ms + EPS) * g_ref[...]


def _final_norm(xa, g, tc):
    b, ttot, d = xa.shape
    t = ttot - tc
    tm = next(rows for rows in (1024, 512, SEQ_BLOCK) if t % rows == 0)
    return pl.pallas_call(
        _final_kernel,
        grid=(b, t // tm),
        in_specs=[pl.BlockSpec((pl.Element(1), pl.Element(tm), pl.Element(d)),
                               lambda bi, i: (bi, pl.multiple_of(tc + i * tm, SUBLANES), 0)),
                  pl.BlockSpec((1, d), lambda bi, i: (0, 0))],
        out_specs=pl.BlockSpec((None, tm, d), lambda bi, i: (bi, i, 0)),
        out_shape=jax.ShapeDtypeStruct((b, t, d), F32),
        compiler_params=_cparams(("parallel", "parallel")),
        name="final_norm",
    )(xa, g.reshape(1, d))


def _rope_tables(t, tc):
    pos = jnp.arange(t)
    row = (pos // GRID_W).astype(F32)
    col = (pos % GRID_W).astype(F32)
    inv = ROPE_BASE ** (-jnp.arange(ROPE_FREQS, dtype=F32) / ROPE_FREQS)
    ar, ac = row[:, None] * inv, col[:, None] * inv
    cos = jnp.concatenate([jnp.cos(ar), jnp.cos(ar), jnp.cos(ac), jnp.cos(ac)], axis=1)
    sin = jnp.concatenate([-jnp.sin(ar), jnp.sin(ar), -jnp.sin(ac), jnp.sin(ac)], axis=1)
    cos = jnp.concatenate([jnp.ones((tc, HEAD_DIM), F32), cos], axis=0)
    sin = jnp.concatenate([jnp.zeros((tc, HEAD_DIM), F32), sin], axis=0)
    return jnp.tile(cos, (1, LANES // HEAD_DIM)), jnp.tile(sin, (1, LANES // HEAD_DIM))


def _block_diag_dense(w):
    nblk, h, k = w.shape
    return jnp.einsum('nhk,nm->nhmk', w, jnp.eye(nblk, dtype=w.dtype)).reshape(nblk * h, nblk * k)


def kernel(x, c, ctx, c_ctx, w_mod, b_mod, norm_mix, norm_ffn, final_norm, w_in_ab, lru_conv_w, lru_conv_b, lru_wa, lru_ba, lru_wx, lru_bx, lru_lam, attn_sink, w_out_ab, ffn_w1, ffn_w3, ffn_w2, w_in_cd, s5_a_re, s5_a_im, s5_log_step, s5_b_re, s5_b_im, s5_c_re, s5_c_im, s5_d, s5_w_glu, s5_b_glu, hg_lb_raw, hg_norm, w_out_cd, moe_router, moe_w1, moe_w3, moe_w2):
    nb, t, d = x.shape
    tc = ctx.shape[1]
    depth = w_mod.shape[0]
    assert tc % SEQ_BLOCK == 0 and t % SEQ_BLOCK == 0 and t % GRID_W == 0

    xa = jnp.concatenate([ctx, x], axis=1)
    mod_rows = -(-(nb + 1) // SUBLANES) * SUBLANES
    cvec = jnp.zeros((mod_rows, d), F32).at[:nb].set(c).at[nb].set(c_ctx)
    mod = _modulation(cvec, w_mod, b_mod).reshape(depth, mod_rows, N_MOD, d)

    cos_t, sin_t = _rope_tables(t, tc)
    lb_soft = jax.nn.softmax(hg_lb_raw.astype(F32), axis=0)
    lb_table = jnp.cumsum(lb_soft, axis=0) - lb_soft[0:1]
    n_exp = moe_router.shape[2]
    s5w = jax.vmap(_s5_weights)(s5_a_re, s5_a_im, s5_log_step, s5_b_re, s5_b_im, s5_c_re, s5_c_im)
    ffn_w = [w.astype(BF16) for w in (ffn_w1, ffn_w3, ffn_w2)]
    moe_w = [w.astype(BF16) for w in (moe_w1, moe_w3, moe_w2)]

    for l in range(depth):
        j = l // 2
        mod_l = mod[l]
        if l % 2 == 0:
            z = _inproj(xa, mod_l, norm_mix[l], w_in_ab[j].astype(BF16), tc, nb)
            lru = _lru(z, lru_conv_w[j], lru_conv_b[j],
                       jnp.stack([_block_diag_dense(lru_wa[j, dr]) for dr in range(2)]).astype(BF16),
                       lru_ba[j],
                       jnp.stack([_block_diag_dense(lru_wx[j, dr]) for dr in range(2)]).astype(BF16),
                       lru_bx[j], jax.nn.softplus(-lru_lam[j]), tc)
            sink = jnp.zeros((1, LANES), F32).at[0, :ATT_HEADS].set(attn_sink[j])
            att = _attention(z, cos_t, sin_t, sink, tc)
            xa = _outproj_ab(xa, mod_l, lru[0], lru[1], z, att, w_out_ab[j].astype(BF16), tc, nb)
            xa = _ffn(xa, mod_l, norm_ffn[l], ffn_w[0], ffn_w[1], ffn_w[2], j, tc, nb)
        else:
            z = _inproj(xa, mod_l, norm_mix[l], w_in_cd[j].astype(BF16), tc, nb)
            y5 = _s5(z, s5w, j, s5_d[j], tc, s5_d.shape[1])
            o_f, o_b = _gla(z, lb_table[j], tc)
            xa = _outproj_cd(xa, mod_l, y5, o_f, o_b, z, s5_w_glu[j].astype(BF16), s5_b_glu[j],
                             hg_norm[j], w_out_cd[j].astype(BF16), tc, nb)
            router_t = jnp.zeros((SUBLANES, d), F32).at[:n_exp].set(moe_router[j].T)
            xa = _moe(xa, mod_l, norm_ffn[l], router_t, moe_w[0], moe_w[1], moe_w[2], j, tc, nb)
    return _final_norm(xa, final_norm, tc)
```

```python
import functools
import math

import jax
import jax.numpy as jnp
from jax import lax
from jax.experimental import pallas as pl
from jax.experimental.pallas import tpu as pltpu

F32 = jnp.float32
BF16 = jnp.bfloat16

EPS = 1e-6
GRID_W = 64
LRU_BLOCKS = 8
LRU_C = 8.0
CONV_W = 4
ATT_HEADS = 8
ATT_KV_HEADS = 2
ATT_GROUP = ATT_HEADS // ATT_KV_HEADS
HEAD_DIM = 64
WINDOW = 128
BLOCK_Q = 128
ROPE_FREQS = HEAD_DIM // 4
ROPE_BASE = 10000.0
S5_GROUP = 16
S5_STATE = 64
S5_CHUNK = 8
HG_HEADS = 4
HG_DK = 128
HG_CHUNK = 64
SEQ_BLOCK = 256
N_MOD = 6
MOE_PASS_ROWS = 256
LANES = 128
SUBLANES = 8
MXU_TILE = 256
VMEM_LIMIT = 56 * 1024 * 1024
NEG = -1e30
LOG2E = math.log2(math.e)


def _cparams(sem):
    return pltpu.CompilerParams(dimension_semantics=sem, vmem_limit_bytes=VMEM_LIMIT)


def _row_tile(ttot):
    for tm in (768, 1024, 512, 256):
        if ttot % tm == 0:
            return tm
    raise ValueError(f"unsupported token count {ttot}")


def _tall_row_tile(ttot):
    for tm in range(min(ttot, 1088) // 16 * 16, 15, -16):
        if ttot % tm == 0:
            return tm
    raise ValueError(f"unsupported token count {ttot}")


def _sigmoid(x):
    return 0.5 * jnp.tanh(0.5 * x) + 0.5


def _silu(x):
    return x * _sigmoid(x)


def _gelu(x):
    return 0.5 * x * (1.0 + jnp.tanh(math.sqrt(2.0 / math.pi) * (x + 0.044715 * (x * x * x))))


def _norm_mod(x, g, mb, mc, row0, tc, shift_idx, scale_idx):
    ms = jnp.mean(x * x, axis=-1, keepdims=True)
    y = x * lax.rsqrt(ms + EPS) * g
    rows = row0 + lax.broadcasted_iota(jnp.int32, (x.shape[0], 1), 0)
    is_ctx = rows < tc
    scale = jnp.where(is_ctx, mc[scale_idx:scale_idx + 1], mb[scale_idx:scale_idx + 1])
    shift = jnp.where(is_ctx, mc[shift_idx:shift_idx + 1], mb[shift_idx:shift_idx + 1])
    return y * (1.0 + scale) + shift


def _gate_mod(mb, mc, idx, row0, tc, n):
    rows = row0 + lax.broadcasted_iota(jnp.int32, (n, 1), 0)
    return jnp.where(rows < tc, mc[idx:idx + 1], mb[idx:idx + 1])


def _mod_kernel(c_ref, w_ref, b_ref, o_ref):
    s = _silu(c_ref[...])
    o_ref[...] = jnp.dot(s, w_ref[...], preferred_element_type=F32,
                         precision=lax.Precision.HIGHEST) + b_ref[...]


def _modulation(cvec, w_mod, b_mod):
    depth, d, n = w_mod.shape
    tn = 1536 if n % 1536 == 0 else n
    rows = cvec.shape[0]
    return pl.pallas_call(
        _mod_kernel,
        grid=(depth, n // tn),
        in_specs=[pl.BlockSpec((rows, d), lambda l, j: (0, 0)),
                  pl.BlockSpec((None, d, tn), lambda l, j: (l, 0, j)),
                  pl.BlockSpec((None, 1, tn), lambda l, j: (l, 0, j))],
        out_specs=pl.BlockSpec((None, rows, tn), lambda l, j: (l, 0, j)),
        out_shape=jax.ShapeDtypeStruct((depth, rows, n), F32),
        compiler_params=_cparams(("arbitrary", "arbitrary")),
        name="modulation",
    )(cvec, w_mod, b_mod.reshape(depth, 1, n))


def _inproj_kernel(x_ref, mb_ref, mc_ref, g_ref, w_ref, o_ref, *, tm, tc):
    i = pl.program_id(1)
    h = _norm_mod(x_ref[...], g_ref[...], mb_ref[...], mc_ref[...], i * tm, tc, 0, 1)
    o_ref[...] = jnp.dot(h.astype(BF16), w_ref[...], preferred_element_type=F32)


def _inproj(xa, mod_l, g, w, tc, nb):
    b, ttot, d = xa.shape
    n = w.shape[1]
    tm = _row_tile(ttot)
    return pl.pallas_call(
        functools.partial(_inproj_kernel, tm=tm, tc=tc),
        grid=(b, ttot // tm),
        in_specs=[pl.BlockSpec((None, tm, d), lambda bi, i: (bi, i, 0)),
                  pl.BlockSpec((None, N_MOD, d), lambda bi, i: (bi, 0, 0)),
                  pl.BlockSpec((None, N_MOD, d), lambda bi, i: (nb, 0, 0)),
                  pl.BlockSpec((1, d), lambda bi, i: (0, 0)),
                  pl.BlockSpec((d, n), lambda bi, i: (0, 0))],
        out_specs=pl.BlockSpec((None, tm, n), lambda bi, i: (bi, i, 0)),
        out_shape=jax.ShapeDtypeStruct((b, ttot, n), F32),
        compiler_params=_cparams(("parallel", "parallel")),
        name="inproj",
    )(xa, mod_l, mod_l, g.reshape(1, d), w)


def _seq_block(step, n_ctx, n_all, reverse):
    if not reverse:
        return step
    return jnp.where(step < n_ctx, n_ctx - 1 - step, n_all - 1 - step + n_ctx)


def _lru_kernel(*refs, lc, n_ctx, n_all):
    cw_ref, cb_ref = refs[:2]
    n_in = 8
    ins = [refs[2 + d * n_in:2 + (d + 1) * n_in] for d in range(2)]
    outs = refs[2 + 2 * n_in:4 + 2 * n_in]
    scratch = refs[4 + 2 * n_in:]
    for d in range(2):
        _lru_block(*ins[d], cw_ref, cb_ref, outs[d], *scratch[4 * d:4 * d + 4],
                   lc=lc, n_ctx=n_ctx, n_all=n_all, reverse=bool(d))


def _lru_block(u_ref, hp_ref, hn_ref, wa_ref, ba_ref, wx_ref, bx_ref, sp_ref, cw_ref, cb_ref,
               o_ref, uc_ref, a_ref, b_ref, h_ref, *, lc, n_ctx, n_all, reverse):
    step = pl.program_id(1)
    blk = _seq_block(step, n_ctx, n_all, reverse)
    has_prev = jnp.logical_and(blk != 0, blk != n_ctx)
    has_next = jnp.logical_and(blk != n_ctx - 1, blk != n_all - 1)

    @pl.when(step == 0)
    def _():
        h_ref[...] = jnp.zeros_like(h_ref)

    w0, w1, w2, w3 = (cw_ref[j:j + 1, :] for j in range(CONV_W))
    cb = cb_ref[...]
    u = u_ref[...]
    uc_ref[...] = (cb + pltpu.roll(u, 2, 0) * w0 + pltpu.roll(u, 1, 0) * w1 + u * w2
                   + pltpu.roll(u, lc - 1, 0) * w3)
    row = lax.broadcasted_iota(jnp.int32, (SUBLANES, 1), 0)
    hp = jnp.where(has_prev, hp_ref[...], 0.0)
    hn = jnp.where(has_next, hn_ref[...], 0.0)
    u0 = u_ref[0:SUBLANES, :]
    u1 = u_ref[SUBLANES:2 * SUBLANES, :]
    uc_ref[0:SUBLANES, :] = (
        cb + jnp.where(row < 2, pltpu.roll(hp, 2, 0), pltpu.roll(u0, 2, 0)) * w0
        + jnp.where(row < 1, pltpu.roll(hp, 1, 0), pltpu.roll(u0, 1, 0)) * w1 + u0 * w2
        + jnp.where(row < SUBLANES - 1, pltpu.roll(u0, SUBLANES - 1, 0),
                    pltpu.roll(u1, SUBLANES - 1, 0)) * w3)
    ul = u_ref[lc - SUBLANES:lc, :]
    um = u_ref[lc - 2 * SUBLANES:lc - SUBLANES, :]
    uc_ref[lc - SUBLANES:lc, :] = (
        cb + jnp.where(row < 2, pltpu.roll(um, 2, 0), pltpu.roll(ul, 2, 0)) * w0
        + jnp.where(row < 1, pltpu.roll(um, 1, 0), pltpu.roll(ul, 1, 0)) * w1 + ul * w2
        + jnp.where(row < SUBLANES - 1, pltpu.roll(ul, SUBLANES - 1, 0),
                    pltpu.roll(hn, SUBLANES - 1, 0)) * w3)

    uc = uc_ref[...]
    ub = uc.astype(BF16)
    r = _sigmoid(jnp.dot(ub, wa_ref[...], preferred_element_type=F32) + ba_ref[...])
    gi = _sigmoid(jnp.dot(ub, wx_ref[...], preferred_element_type=F32) + bx_ref[...])
    log_a = -LRU_C * r * sp_ref[...]
    a = jnp.exp(log_a)
    b = jnp.sqrt(1.0 - a * a) * (gi * uc)
    sub = lax.broadcasted_iota(jnp.int32, (lc, 1), 0) % SUBLANES
    dist = 1
    while dist < SUBLANES:
        inside = (sub < SUBLANES - dist) if reverse else (sub >= dist)
        shift = (lc - dist) if reverse else dist
        a_sh = jnp.where(inside, pltpu.roll(a, shift, 0), 1.0)
        b_sh = jnp.where(inside, pltpu.roll(b, shift, 0), 0.0)
        b = a * b_sh + b
        a = a * a_sh
        dist *= 2
    a_ref[...] = a
    b_ref[...] = b
    n_grp = lc // SUBLANES

    def body(i, carry):
        t = (n_grp - 1 - i) if reverse else i
        rows = pl.ds(pl.multiple_of(t * SUBLANES, SUBLANES), SUBLANES)
        h = a_ref[rows, :] * carry + b_ref[rows, :]
        b_ref[rows, :] = h
        return h[0:1, :] if reverse else h[SUBLANES - 1:SUBLANES, :]

    h_ref[0:1, :] = lax.fori_loop(0, n_grp, body, h_ref[0:1, :], unroll=4)
    o_ref[...] = b_ref[...].astype(o_ref.dtype)


def _lru(z, conv_w, conv_b, wa_bd, ba, wx_bd, bx, sp, tc):
    b, ttot, _ = z.shape
    w = conv_w.shape[1]
    lc = SEQ_BLOCK
    n_all, n_ctx = ttot // lc, tc // lc
    hb = lc // SUBLANES
    n_h = ttot // SUBLANES
    col = 1

    def direction(d):
        def blk(s):
            return _seq_block(s, n_ctx, n_all, bool(d))

        vec = pl.BlockSpec((None, 1, w), lambda bi, s: (d, 0, 0))
        mat = pl.BlockSpec((None, w, w), lambda bi, s: (d, 0, 0))
        specs = [pl.BlockSpec((None, lc, w), lambda bi, s: (bi, blk(s), col)),
                 pl.BlockSpec((None, SUBLANES, w),
                              lambda bi, s: (bi, jnp.maximum(blk(s) * hb - 1, 0), col)),
                 pl.BlockSpec((None, SUBLANES, w),
                              lambda bi, s: (bi, jnp.minimum((blk(s) + 1) * hb, n_h - 1), col)),
                 mat, vec, mat, vec, vec]
        return specs, pl.BlockSpec((None, lc, w), lambda bi, s: (bi, blk(s), 0))

    (in_f, out_f), (in_r, out_r) = direction(0), direction(1)
    out = jax.ShapeDtypeStruct((b, ttot, w), BF16)
    per_dir = [pltpu.VMEM((lc, w), F32)] * 3 + [pltpu.VMEM((SUBLANES, w), F32)]
    vecs = lambda x: x.reshape(2, 1, w)
    args = (z, z, z, wa_bd, vecs(ba), wx_bd, vecs(bx), vecs(sp))
    return pl.pallas_call(
        functools.partial(_lru_kernel, lc=lc, n_ctx=n_ctx, n_all=n_all),
        grid=(b, n_all),
        in_specs=[pl.BlockSpec((CONV_W, w), lambda bi, s: (0, 0)),
                  pl.BlockSpec((1, w), lambda bi, s: (0, 0))] + in_f + in_r,
        out_specs=[out_f, out_r],
        out_shape=[out, out],
        scratch_shapes=per_dir + per_dir,
        compiler_params=_cparams(("parallel", "arbitrary")),
        name="rglru_bidir",
    )(conv_w, conv_b.reshape(1, w), *args, *args)


def _rope(x, cos, sin_signed):
    lane = lax.broadcasted_iota(jnp.int32, x.shape, 1)
    swapped = jnp.where(lane % 32 < 16, pltpu.roll(x, LANES - 16, 1), pltpu.roll(x, 16, 1))
    return x * cos + swapped * sin_signed


def _pair_layout(x, x_sw, head):
    lo = lax.broadcasted_iota(jnp.int32, x.shape, 1) < HEAD_DIM
    first, second = (x, x_sw) if head == 0 else (x_sw, x)
    return jnp.concatenate([jnp.where(lo, first, 0.0), jnp.where(lo, 0.0, second)],
                           axis=0).astype(BF16)


def _attend(q_block, keys, vals, bias, sink_row, o_ref):
    nk = keys.shape[0]
    keys_sw = pltpu.roll(keys, HEAD_DIM, 1)
    vals_t = vals.T
    top = lax.broadcasted_iota(jnp.int32, (LANES, BLOCK_Q), 0) < HEAD_DIM
    zero_t = jnp.zeros((HEAD_DIM, nk), F32)
    pairs_per_kv = ATT_GROUP // 2
    for kv in range(ATT_KV_HEADS):
        kab = _pair_layout(keys, keys_sw, kv)
        vt = vals_t[kv * HEAD_DIM:(kv + 1) * HEAD_DIM, :]
        vab_t = jnp.concatenate([jnp.concatenate([vt, zero_t], axis=1),
                                 jnp.concatenate([zero_t, vt], axis=1)], axis=0).astype(BF16)
        for g in range(kv * pairs_per_kv, (kv + 1) * pairs_per_kv):
            s_t = lax.dot_general(kab, q_block(g), (((1,), (1,)), ((), ())),
                                  preferred_element_type=F32)
            halves = []
            recips = []
            for half in range(2):
                sh = s_t[half * nk:(half + 1) * nk, :]
                if bias is not None:
                    sh = sh + bias
                sk = sink_row[:, 2 * g + half:2 * g + half + 1]
                m = jnp.maximum(jnp.max(sh, axis=0, keepdims=True), sk)
                p = jnp.exp2(sh - m)
                recips.append(1.0 / (jnp.sum(p, axis=0, keepdims=True) + jnp.exp2(sk - m)))
                halves.append(p.astype(BF16))
            o_t = jnp.dot(vab_t, jnp.concatenate(halves, axis=0), preferred_element_type=F32)
            o_t = o_t * jnp.where(top, recips[0], recips[1])
            o_ref[:, g * LANES:(g + 1) * LANES] = o_t.T.astype(o_ref.dtype)


def _attn_kernel(q_ref, kp_ref, k0_ref, kn_ref, vp_ref, v0_ref, vn_ref, kc_ref, vc_ref,
                 cq_ref, sq_ref, cp_ref, sp_ref, cn_ref, sn_ref, sink_ref, o_ref, *, n_ctx, n_lat):
    i = pl.program_id(1)
    qscale = HEAD_DIM ** -0.5 * LOG2E
    sink_row = sink_ref[...] * LOG2E

    @pl.when(i < n_ctx)
    def _():
        def q_block(g):
            return (q_ref[:, g * LANES:(g + 1) * LANES] * qscale).astype(BF16)
        _attend(q_block, kc_ref[...], vc_ref[...], None, sink_row, o_ref)

    @pl.when(i >= n_ctx)
    def _():
        n = i - n_ctx
        cq, sq = cq_ref[...], sq_ref[...]

        def q_block(g):
            return (_rope(q_ref[:, g * LANES:(g + 1) * LANES], cq, sq) * qscale).astype(BF16)

        kp = _rope(kp_ref[...], cp_ref[...], sp_ref[...])
        k0 = _rope(k0_ref[...], cq, sq)
        kn = _rope(kn_ref[...], cn_ref[...], sn_ref[...])
        keys = jnp.concatenate([kp, k0, kn, kc_ref[...]], axis=0)
        vals = jnp.concatenate([vp_ref[...], v0_ref[...], vn_ref[...], vc_ref[...]], axis=0)
        nk = keys.shape[0]
        qpos = n * BLOCK_Q + lax.broadcasted_iota(jnp.int32, (nk, BLOCK_Q), 1)
        key = lax.broadcasted_iota(jnp.int32, (nk, BLOCK_Q), 0)
        kpos = (n - 1) * BLOCK_Q + key
        local = (jnp.abs(qpos - kpos) <= WINDOW) & (kpos >= 0) & (kpos < n_lat * BLOCK_Q)
        bias = jnp.where(local | (key >= 3 * BLOCK_Q), 0.0, NEG)
        _attend(q_block, keys, vals, bias, sink_row, o_ref)


def _attention(z, cos_t, sin_t, sink, tc):
    b, ttot, _ = z.shape
    n_all, n_ctx = ttot // BLOCK_Q, tc // BLOCK_Q
    n_lat = n_all - n_ctx
    kcol = (2 * 512 + ATT_HEADS * HEAD_DIM) // LANES
    vcol = kcol + 1
    qcol = 2 * 512 // (ATT_HEADS * HEAD_DIM)

    def prev(i):
        return jnp.maximum(i - 1, 0)

    def nxt(i):
        return jnp.minimum(i + 1, n_all - 1)

    def rows(col, f):
        return pl.BlockSpec((None, BLOCK_Q, LANES), lambda bi, i: (bi, f(i), col))

    def tab(f):
        return pl.BlockSpec((BLOCK_Q, LANES), lambda bi, i: (f(i), 0))

    same = lambda i: i
    return pl.pallas_call(
        functools.partial(_attn_kernel, n_ctx=n_ctx, n_lat=n_lat),
        grid=(b, n_all),
        in_specs=[pl.BlockSpec((None, BLOCK_Q, ATT_HEADS * HEAD_DIM), lambda bi, i: (bi, i, qcol)),
                  rows(kcol, prev), rows(kcol, same), rows(kcol, nxt),
                  rows(vcol, prev), rows(vcol, same), rows(vcol, nxt),
                  pl.BlockSpec((None, tc, LANES), lambda bi, i: (bi, 0, kcol)),
                  pl.BlockSpec((None, tc, LANES), lambda bi, i: (bi, 0, vcol)),
                  tab(same), tab(same), tab(prev), tab(prev), tab(nxt), tab(nxt),
                  pl.BlockSpec((1, LANES), lambda bi, i: (0, 0))],
        out_specs=pl.BlockSpec((None, BLOCK_Q, ATT_HEADS * HEAD_DIM), lambda bi, i: (bi, i, 0)),
        out_shape=jax.ShapeDtypeStruct((b, ttot, ATT_HEADS * HEAD_DIM), BF16),
        compiler_params=_cparams(("parallel", "parallel")),
        name="window_gqa",
    )(z, z, z, z, z, z, z, z, z, cos_t, sin_t, cos_t, sin_t, cos_t, sin_t, sink)


def _outproj_ab_kernel(x_ref, mb_ref, mc_ref, lf_ref, lb_ref, g_ref, at_ref, w_ref, o_ref, *, tm, tc):
    i = pl.program_id(1)
    lru = (lf_ref[...].astype(F32) + lb_ref[...].astype(F32)) * _gelu(g_ref[...])
    mix = jnp.concatenate([lru.astype(BF16), at_ref[...]], axis=1)
    dx = jnp.dot(mix, w_ref[...], preferred_element_type=F32)
    gate = _gate_mod(mb_ref[...], mc_ref[...], 2, i * tm, tc, tm)
    o_ref[...] = x_ref[...] + gate * dx


def _outproj_ab(xa, mod_l, lru_f, lru_b, z, att, w_out, tc, nb):
    b, ttot, d = xa.shape
    w = lru_f.shape[2]
    tm = _row_tile(ttot)
    tile = lambda width, col: pl.BlockSpec((None, tm, width), lambda bi, i: (bi, i, col))
    return pl.pallas_call(
        functools.partial(_outproj_ab_kernel, tm=tm, tc=tc),
        grid=(b, ttot // tm),
        in_specs=[tile(d, 0),
                  pl.BlockSpec((None, N_MOD, d), lambda bi, i: (bi, 0, 0)),
                  pl.BlockSpec((None, N_MOD, d), lambda bi, i: (nb, 0, 0)),
                  tile(w, 0), tile(w, 0), tile(w, 0), tile(att.shape[2], 0),
                  pl.BlockSpec(w_out.shape, lambda bi, i: (0, 0))],
        out_specs=tile(d, 0),
        out_shape=jax.ShapeDtypeStruct(xa.shape, F32),
        compiler_params=_cparams(("parallel", "parallel")),
        name="outproj_ab",
    )(xa, mod_l, mod_l, lru_f, lru_b, z, att, w_out)


def _outproj_cd_kernel(x_ref, mb_ref, mc_ref, y_ref, of_ref, ob_ref, g_ref, wg_ref, bg_ref, gn_ref,
                       w_ref, gf_ref, r_ref, o_ref, h_ref, wt_ref, cnt_ref, *, tm, tc, n_exp):
    i = pl.program_id(1)
    y = _gelu(y_ref[...])
    s5 = y * _sigmoid(jnp.dot(y.astype(BF16), wg_ref[...], preferred_element_type=F32) + bg_ref[...])
    o = of_ref[...].astype(F32) + ob_ref[...].astype(F32)
    parts = []
    for h in range(HG_HEADS):
        oh = o[:, h * HG_DK:(h + 1) * HG_DK]
        ms = jnp.mean(oh * oh, axis=-1, keepdims=True)
        parts.append(oh * lax.rsqrt(ms + EPS) * gn_ref[...])
    hg = jnp.concatenate(parts, axis=1) * _silu(g_ref[...])
    mix = jnp.concatenate([s5.astype(BF16), hg.astype(BF16)], axis=1)
    dx = jnp.dot(mix, w_ref[...], preferred_element_type=F32)
    gate = _gate_mod(mb_ref[...], mc_ref[...], 2, i * tm, tc, tm)
    x_new = x_ref[...] + gate * dx
    o_ref[...] = x_new
    h = _norm_mod(x_new, gf_ref[...], mb_ref[...], mc_ref[...], i * tm, tc, 3, 4)
    _route_tokens(h, r_ref, h_ref, wt_ref, cnt_ref, n_exp)


def _outproj_cd(xa, mod_l, y5, o_f, o_b, z, w_glu, b_glu, hg_norm, w_out, g_ffn, router_t, n_exp,
                tc, nb):
    b, ttot, d = xa.shape
    w = y5.shape[2]
    tm = _row_tile(ttot)
    n_t = ttot // tm
    gcol = z.shape[2] // w - 1
    tile = lambda width, col: pl.BlockSpec((None, tm, width), lambda bi, i: (bi, i, col))
    return pl.pallas_call(
        functools.partial(_outproj_cd_kernel, tm=tm, tc=tc, n_exp=n_exp),
        grid=(b, n_t),
        in_specs=[tile(d, 0),
                  pl.BlockSpec((None, N_MOD, d), lambda bi, i: (bi, 0, 0)),
                  pl.BlockSpec((None, N_MOD, d), lambda bi, i: (nb, 0, 0)),
                  tile(w, 0), tile(w, 0), tile(w, 0), tile(w, gcol),
                  pl.BlockSpec(w_glu.shape, lambda bi, i: (0, 0)),
                  pl.BlockSpec((1, w), lambda bi, i: (0, 0)),
                  pl.BlockSpec((1, HG_DK), lambda bi, i: (0, 0)),
                  pl.BlockSpec(w_out.shape, lambda bi, i: (0, 0)),
                  pl.BlockSpec((1, d), lambda bi, i: (0, 0)),
                  pl.BlockSpec((SUBLANES, d), lambda bi, i: (0, 0))],
        out_specs=[tile(d, 0),
                   pl.BlockSpec((tm, d), lambda bi, i: (bi * n_t + i, 0)),
                   pl.BlockSpec((SUBLANES, tm), lambda bi, i: (0, bi * n_t + i)),
                   pl.BlockSpec((None, SUBLANES, LANES), lambda bi, i: (bi * n_t + i, 0, 0))],
        out_shape=[jax.ShapeDtypeStruct(xa.shape, F32),
                   jax.ShapeDtypeStruct((b * ttot, d), BF16),
                   jax.ShapeDtypeStruct((SUBLANES, b * ttot), F32),
                   jax.ShapeDtypeStruct((b * n_t, SUBLANES, LANES), jnp.int32)],
        compiler_params=_cparams(("parallel", "parallel")),
        name="outproj_cd_route",
    )(xa, mod_l, mod_l, y5, o_f, o_b, z, w_glu, b_glu.reshape(1, w), hg_norm.reshape(1, HG_DK), w_out,
      g_ffn.reshape(1, d), router_t)


def _ffn_kernel(x_ref, mb_ref, mc_ref, g_ref, w1_ref, w3_ref, w2_ref, o_ref, h_ref, acc_ref, *, tm, tc):
    i = pl.program_id(1)
    k = pl.program_id(2)

    @pl.when(k == 0)
    def _():
        h = _norm_mod(x_ref[...], g_ref[...], mb_ref[...], mc_ref[...], i * tm, tc, 3, 4)
        h_ref[...] = h.astype(BF16)
        acc_ref[...] = jnp.zeros_like(acc_ref)

    h = h_ref[...]
    tf = w1_ref.shape[1]
    half = (tf // LANES + 1) // 2 * LANES
    y = None
    for f0, f1 in ([(0, half), (half, tf)] if 0 < half < tf else [(0, tf)]):
        a = jnp.dot(h, w1_ref[:, f0:f1], preferred_element_type=F32)
        c = jnp.dot(h, w3_ref[:, f0:f1], preferred_element_type=F32)
        part = jnp.dot((_silu(a) * c).astype(BF16), w2_ref[f0:f1, :], preferred_element_type=F32)
        y = part if y is None else y + part
    acc_ref[...] += y

    @pl.when(k == pl.num_programs(2) - 1)
    def _():
        gate = _gate_mod(mb_ref[...], mc_ref[...], 5, i * tm, tc, tm)
        o_ref[...] = x_ref[...] + gate * acc_ref[...]


def _ffn(xa, mod_l, g, w1, w3, w2, layer, tc, nb):
    b, ttot, d = xa.shape
    f = w1.shape[2]
    tm = _tall_row_tile(ttot)
    tf = f // 2 if (f // 2) % LANES == 0 else f
    return pl.pallas_call(
        functools.partial(_ffn_kernel, tm=tm, tc=tc),
        grid=(b, ttot // tm, f // tf),
        in_specs=[pl.BlockSpec((None, tm, d), lambda bi, i, k: (bi, i, 0)),
                  pl.BlockSpec((None, N_MOD, d), lambda bi, i, k: (bi, 0, 0)),
                  pl.BlockSpec((None, N_MOD, d), lambda bi, i, k: (nb, 0, 0)),
                  pl.BlockSpec((1, d), lambda bi, i, k: (0, 0)),
                  pl.BlockSpec((None, d, tf), lambda bi, i, k: (layer, 0, k)),
                  pl.BlockSpec((None, d, tf), lambda bi, i, k: (layer, 0, k)),
                  pl.BlockSpec((None, tf, d), lambda bi, i, k: (layer, k, 0))],
        out_specs=pl.BlockSpec((None, tm, d), lambda bi, i, k: (bi, i, 0)),
        out_shape=jax.ShapeDtypeStruct(xa.shape, F32),
        scratch_shapes=[pltpu.VMEM((tm, d), BF16), pltpu.VMEM((tm, d), F32)],
        compiler_params=_cparams(("parallel", "parallel", "arbitrary")),
        name="ffn_swiglu",
    )(xa, mod_l, mod_l, g.reshape(1, d), w1, w3, w2)


def _route_tokens(h, r_ref, h_ref, wt_ref, cnt_ref, n_exp):
    nt = (((1,), (1,)), ((), ()))
    h_hi = h.astype(BF16)
    h_lo = (h - h_hi.astype(F32)).astype(BF16)
    r = r_ref[...]
    r_hi = r.astype(BF16)
    r_lo = (r - r_hi.astype(F32)).astype(BF16)
    logits = (lax.dot_general(r_hi, h_hi, nt, preferred_element_type=F32)
              + lax.dot_general(r_lo, h_hi, nt, preferred_element_type=F32)
              + lax.dot_general(r_hi, h_lo, nt, preferred_element_type=F32))
    sub = lax.broadcasted_iota(jnp.int32, logits.shape, 0)
    h_ref[...] = h_hi
    logits = jnp.where(sub < n_exp, logits, NEG)
    m1 = jnp.max(logits, axis=0, keepdims=True)
    i1 = jnp.min(jnp.where(logits == m1, sub, SUBLANES), axis=0, keepdims=True)
    rest = jnp.where(sub == i1, NEG, logits)
    m2 = jnp.max(rest, axis=0, keepdims=True)
    i2 = jnp.min(jnp.where(rest == m2, sub, SUBLANES), axis=0, keepdims=True)
    e2 = jnp.exp(m2 - m1)
    g1 = 1.0 / (1.0 + e2)
    wt = jnp.where(sub == i1, g1, 0.0) + jnp.where(sub == i2, e2 * g1, 0.0)
    wt_ref[...] = wt
    cnt = jnp.sum(jnp.where(wt > 0.0, 1.0, 0.0), axis=1, keepdims=True)
    cnt_ref[...] = jnp.broadcast_to(cnt, cnt_ref.shape).astype(jnp.int32)


def _moe_kernel(cnt_ref, x_ref, *refs, tm, tc, n_exp, n_t, group):
    mb_refs = refs[:group]
    mc_ref, h_ref, wt_ref, w1_ref, w3_ref, w2_ref, o_ref, rank_ref = refs[group:]
    p = pl.program_id(0)
    e = pl.program_id(1)
    f = w1_ref.shape[1]
    half = (f // LANES + 1) // 2 * LANES
    f_split = [(0, half), (half, f)] if 0 < half < f else [(0, f)]

    @pl.when(e == 0)
    def _():
        o_ref[...] = jnp.zeros_like(o_ref)
        r_i = lax.broadcasted_iota(jnp.int32, (tm, tm), 0)
        c_i = lax.broadcasted_iota(jnp.int32, (tm, tm), 1)
        before = jnp.where(r_i < c_i, 1.0, 0.0).astype(BF16)
        for j in range(group):
            sel = wt_ref[:, j * tm:(j + 1) * tm] > 0.0
            rank = jnp.dot(jnp.where(sel, 1.0, 0.0).astype(BF16), before, preferred_element_type=F32)
            rank_ref[j] = jnp.where(sel, rank, -1.0)

    counts = [cnt_ref[(p * group + j) * n_exp + e] for j in range(group)]

    offsets = [jnp.int32(0)]
    for j in range(group - 1):
        offsets.append(offsets[-1] + counts[j])
    total = offsets[-1] + counts[-1]

    def expert_pass(m_rows, base):
        slot = (lax.broadcasted_iota(jnp.int32, (m_rows, tm), 0) + base).astype(F32)
        onehots = []
        xe = None
        gate = None
        for j in range(group):
            r_row = rank_ref[j, pl.ds(e, 1), :]
            r_row = jnp.where(r_row >= 0.0, r_row + offsets[j].astype(F32), -1.0)
            w_row = wt_ref[pl.ds(e, 1), j * tm:(j + 1) * tm]
            hit = r_row == slot
            onehot = jnp.where(hit, 1.0, 0.0).astype(BF16)
            onehots.append(onehot)
            xj = jnp.dot(onehot, h_ref[j * tm:(j + 1) * tm, :], preferred_element_type=F32)
            gj = jnp.sum(jnp.where(hit, w_row, 0.0), axis=1, keepdims=True)
            xe = xj if xe is None else xe + xj
            gate = gj if gate is None else gate + gj
        xe = xe.astype(BF16)
        y = None
        for f0, f1 in f_split:
            a = jnp.dot(xe, w1_ref[:, f0:f1], preferred_element_type=F32)
            c = jnp.dot(xe, w3_ref[:, f0:f1], preferred_element_type=F32)
            part = jnp.dot((_silu(a) * c * gate).astype(BF16), w2_ref[f0:f1, :],
                           preferred_element_type=F32)
            y = part if y is None else y + part
        y = y.astype(BF16)
        for j in range(group):
            o_ref[j * tm:(j + 1) * tm, :] += lax.dot_general(
                onehots[j], y, (((0,), (0,)), ((), ())), preferred_element_type=F32)

    n_full = total // MOE_PASS_ROWS

    def full_pass(it, carry):
        expert_pass(MOE_PASS_ROWS, it * MOE_PASS_ROWS)
        return carry

    lax.fori_loop(0, n_full, full_pass, 0)
    rest = total - n_full * MOE_PASS_ROWS

    @pl.when(rest > MOE_PASS_ROWS // 2)
    def _():
        expert_pass(MOE_PASS_ROWS, n_full * MOE_PASS_ROWS)

    @pl.when(jnp.logical_and(rest > 0, rest <= MOE_PASS_ROWS // 2))
    def _():
        expert_pass(MOE_PASS_ROWS // 2, n_full * MOE_PASS_ROWS)

    @pl.when(e == n_exp - 1)
    def _():
        for j in range(group):
            i = (p * group + j) % n_t
            gate5 = _gate_mod(mb_refs[j][...], mc_ref[...], 5, i * tm, tc, tm)
            rows = slice(j * tm, (j + 1) * tm)
            o_ref[rows, :] = x_ref[rows, :] + gate5 * o_ref[rows, :]


def _moe(xa, mod_l, routed, w1, w3, w2, layer, tc, nb):
    b, ttot, d = xa.shape
    _, n_exp, _, f = w1.shape
    tm = _row_tile(ttot)
    n_t = ttot // tm
    group = 2 if (b * n_t) % 2 == 0 else 1
    h, wt, cnt = routed
    cnt = cnt[:, :n_exp, 0].reshape(-1)
    rows = group * tm
    mods = [pl.BlockSpec((None, N_MOD, d), functools.partial(
        lambda p, e, c, j: ((p * group + j) // n_t, 0, 0), j=j)) for j in range(group)]
    grid_spec = pltpu.PrefetchScalarGridSpec(
        num_scalar_prefetch=1,
        grid=(b * n_t // group, n_exp),
        in_specs=[pl.BlockSpec((rows, d), lambda p, e, c: (p, 0), pipeline_mode=pl.Buffered(1))]
        + mods
        + [pl.BlockSpec((None, N_MOD, d), lambda p, e, c: (nb, 0, 0)),
           pl.BlockSpec((rows, d), lambda p, e, c: (p, 0)),
           pl.BlockSpec((SUBLANES, rows), lambda p, e, c: (0, p)),
           pl.BlockSpec((None, None, d, f), lambda p, e, c: (layer, e, 0, 0)),
           pl.BlockSpec((None, None, d, f), lambda p, e, c: (layer, e, 0, 0)),
           pl.BlockSpec((None, None, f, d), lambda p, e, c: (layer, e, 0, 0))],
        out_specs=pl.BlockSpec((rows, d), lambda p, e, c: (p, 0)),
        scratch_shapes=[pltpu.VMEM((group, SUBLANES, tm), F32)])
    out = pl.pallas_call(
        functools.partial(_moe_kernel, tm=tm, tc=tc, n_exp=n_exp, n_t=n_t, group=group),
        grid_spec=grid_spec,
        out_shape=jax.ShapeDtypeStruct((b * ttot, d), F32),
        compiler_params=_cparams(("parallel", "arbitrary")),
        name="moe_experts",
    )(cnt, xa.reshape(b * ttot, d), *([mod_l] * group), mod_l, h, wt, w1, w3, w2)
    return out.reshape(b, ttot, d)


def _spread_groups(x, inner, row_group, n_grp):
    cols_in = x.shape[2]
    cols_out = cols_in * n_grp
    q = jnp.arange(cols_out)
    src = (q // (n_grp * inner)) * inner + q % inner
    tile = (jnp.arange(cols_in)[:, None] == src[None, :]).astype(BF16)
    col_group = (q // inner) % n_grp
    y = jnp.dot(x.astype(BF16), tile)
    return jnp.where(row_group[:, None] == col_group[None, :], y, jnp.zeros((), BF16))


def _s5_weights(a_re, a_im, log_step, b_re, b_im, c_re, c_im):
    L = S5_CHUNK
    lr = jnp.minimum(a_re, -1e-4)
    li = a_im
    dt = jnp.exp(log_step)[..., None]
    mag, ang = lr * dt, li * dt
    lbr, lbi = jnp.exp(mag) * jnp.cos(ang), jnp.exp(mag) * jnp.sin(ang)
    zr, zi = lbr - 1.0, lbi
    den = lr * lr + li * li
    fr = (zr * lr + zi * li) / den
    fi = (zi * lr - zr * li) / den
    bbr = fr[..., None] * b_re - fi[..., None] * b_im
    bbi = fr[..., None] * b_im + fi[..., None] * b_re

    def power(p):
        p = p[..., None, None, None].astype(F32)
        return jnp.exp(mag * p) * jnp.cos(ang * p), jnp.exp(mag * p) * jnp.sin(ang * p)

    n_dir, n_grp, n_st = a_re.shape
    gpb = LANES // S5_GROUP
    n_blk = n_grp // gpb
    s = jnp.arange(L)
    in_group = (jnp.arange(L * LANES) % LANES) // S5_GROUP
    st_group = jnp.arange(gpb * n_st) // n_st
    outs = []
    for d in range(n_dir):
        pr, pi = power(jnp.arange(L))
        pr, pi = pr[:, d], pi[:, d]
        cbr = (jnp.einsum('gcn,tgn,gnk->tgck', c_re[d], pr, bbr[d])
               - jnp.einsum('gcn,tgn,gnk->tgck', c_re[d], pi, bbi[d])
               - jnp.einsum('gcn,tgn,gnk->tgck', c_im[d], pr, bbi[d])
               - jnp.einsum('gcn,tgn,gnk->tgck', c_im[d], pi, bbr[d]))
        lag = (s[None, :] - s[:, None]) if d == 0 else (s[:, None] - s[None, :])
        kern = jnp.where((lag >= 0)[..., None, None, None], cbr[jnp.clip(lag, 0, L - 1)], 0.0)
        kern = kern.reshape(L, L, n_blk, gpb, S5_GROUP, S5_GROUP).transpose(2, 0, 3, 5, 1, 4)
        m = _spread_groups(kern.reshape(n_blk, L * LANES, L * S5_GROUP), S5_GROUP, in_group, gpb)
        qr, qi = power((L - 1 - s) if d == 0 else s)
        qr, qi = qr[:, d], qi[:, d]
        str_ = qr[..., None] * bbr[d] - qi[..., None] * bbi[d]
        sti = qr[..., None] * bbi[d] + qi[..., None] * bbr[d]

        def to_state(x):
            x = x.reshape(L, n_blk, gpb, n_st, S5_GROUP).transpose(1, 0, 2, 4, 3)
            return _spread_groups(x.reshape(n_blk, L * LANES, n_st), n_st, in_group, gpb)

        wst = jnp.concatenate([to_state(str_), to_state(sti)], axis=2)
        rr, ri = power((s + 1) if d == 0 else (L - s))
        rr, ri = rr[:, d], ri[:, d]
        wr = c_re[d][None] * rr[:, :, None, :] - c_im[d][None] * ri[:, :, None, :]
        wi = -(c_re[d][None] * ri[:, :, None, :] + c_im[d][None] * rr[:, :, None, :])

        def from_state(x):
            x = x.reshape(L, n_blk, gpb, S5_GROUP, n_st).transpose(1, 2, 4, 0, 3)
            return _spread_groups(x.reshape(n_blk, gpb * n_st, L * S5_GROUP), S5_GROUP, st_group, gpb)

        wout = jnp.concatenate([from_state(wr), from_state(wi)], axis=1)
        ler, lei = jnp.exp(mag[d] * L) * jnp.cos(ang[d] * L), jnp.exp(mag[d] * L) * jnp.sin(ang[d] * L)
        lam_l = jnp.concatenate([ler.reshape(n_blk, 1, gpb * n_st), lei.reshape(n_blk, 1, gpb * n_st)],
                                axis=2)
        outs.append((m, wst, wout, lam_l))
    return tuple(jnp.stack([o[k] for o in outs]) for k in range(4))


def _s5_kernel(u_ref, m_ref, wst_ref, wout_ref, lam_ref, d_ref, o_ref, x_ref, hp_ref, *, rows, rows_ctx):
    L = S5_CHUNK
    dr = pl.program_id(2)
    ns = lam_ref.shape[1] // 2
    u = jnp.concatenate([u_ref[pl.ds(s, rows, stride=L), :] for s in range(L)], axis=1)
    ub = u.astype(BF16)
    x_ref[...] = jnp.dot(ub, wst_ref[...], preferred_element_type=F32)
    lr = lam_ref[:, 0:ns]
    li = lam_ref[:, ns:2 * ns]

    def visit(r, carry):
        hr, hi = carry
        hp_ref[pl.ds(r, 1), 0:ns] = hr
        hp_ref[pl.ds(r, 1), ns:2 * ns] = hi
        xr = x_ref[pl.ds(r, 1), 0:ns]
        xi = x_ref[pl.ds(r, 1), ns:2 * ns]
        return lr * hr - li * hi + xr, lr * hi + li * hr + xi

    zero = (jnp.zeros((1, ns), F32), jnp.zeros((1, ns), F32))

    @pl.when(dr == 0)
    def _():
        lax.fori_loop(0, rows, visit, zero, unroll=4)

    @pl.when(dr == 1)
    def _():
        c = lax.fori_loop(0, rows_ctx, lambda t, c: visit(rows_ctx - 1 - t, c), zero, unroll=4)
        lax.fori_loop(0, rows - rows_ctx, lambda t, c: visit(rows - 1 - t, c), c, unroll=4)

    y_state = jnp.dot(hp_ref[...].astype(BF16), wout_ref[...], preferred_element_type=F32)
    n_in = L * LANES

    def within_chunk(col0, col1, reverse):
        k0, k1 = (col0, n_in) if reverse else (0, col1)
        return jnp.dot(ub[:, k0:k1], m_ref[k0:k1, col0:col1], preferred_element_type=F32)

    @pl.when(dr == 0)
    def _():
        for c0 in range(0, n_in, MXU_TILE):
            y = within_chunk(c0, c0 + MXU_TILE, False) + y_state[:, c0:c0 + MXU_TILE]
            for s in range(c0 // LANES, (c0 + MXU_TILE) // LANES):
                lanes = slice(s * LANES - c0, (s + 1) * LANES - c0)
                o_ref[pl.ds(s, rows, stride=L), :] = (y[:, lanes]
                                                      + d_ref[...] * u[:, s * LANES:(s + 1) * LANES])

    @pl.when(dr == 1)
    def _():
        for c0 in range(0, n_in, MXU_TILE):
            y = within_chunk(c0, c0 + MXU_TILE, True) + y_state[:, c0:c0 + MXU_TILE]
            for s in range(c0 // LANES, (c0 + MXU_TILE) // LANES):
                lanes = slice(s * LANES - c0, (s + 1) * LANES - c0)
                o_ref[pl.ds(s, rows, stride=L), :] += y[:, lanes]


def _s5(z, weights, layer, dvec, tc, width):
    b, ttot, _ = z.shape
    m, wst, wout, lam_l = weights
    n_dir, n_blk = m.shape[1], m.shape[2]
    rows, rows_ctx = ttot // S5_CHUNK, tc // S5_CHUNK
    wspec = lambda a: pl.BlockSpec((None, None, None) + a.shape[3:],
                                   lambda bi, j, dr: (layer, dr, j, 0, 0))
    return pl.pallas_call(
        functools.partial(_s5_kernel, rows=rows, rows_ctx=rows_ctx),
        grid=(b, n_blk, n_dir),
        in_specs=[pl.BlockSpec((None, ttot, LANES), lambda bi, j, dr: (bi, 0, j)),
                  wspec(m), wspec(wst), wspec(wout), wspec(lam_l),
                  pl.BlockSpec((1, LANES), lambda bi, j, dr: (0, j))],
        out_specs=pl.BlockSpec((None, ttot, LANES), lambda bi, j, dr: (bi, 0, j)),
        out_shape=jax.ShapeDtypeStruct((b, ttot, width), F32),
        scratch_shapes=[pltpu.VMEM((rows, lam_l.shape[4]), F32), pltpu.VMEM((rows, lam_l.shape[4]), F32)],
        compiler_params=_cparams(("parallel", "parallel", "arbitrary")),
        name="s5_bidir",
    )(z, m, wst, wout, lam_l, dvec.reshape(1, width))


def _gla_kernel(qf_ref, ff_ref, vf_ref, qb_ref, fb_ref, vb_ref, lb_ref, of_ref, ob_ref,
                stf_ref, stb_ref, *, lc):
    step = pl.program_id(1)

    @pl.when(step == 0)
    def _():
        stf_ref[...] = jnp.zeros_like(stf_ref)
        stb_ref[...] = jnp.zeros_like(stb_ref)

    _gla_block(qf_ref, ff_ref, vf_ref, lb_ref, of_ref, stf_ref, lc=lc, reverse=False)
    _gla_block(qb_ref, fb_ref, vb_ref, lb_ref, ob_ref, stb_ref, lc=lc, reverse=True)


def _gla_block(q_ref, f_ref, v_ref, lb_ref, o_ref, st_ref, *, lc, reverse):
    c = HG_CHUNK
    n_chunks = lc // c
    r_i = lax.broadcasted_iota(jnp.int32, (lc, lc), 0)
    c_i = lax.broadcasted_iota(jnp.int32, (lc, lc), 1)
    same_chunk = (r_i // c) == (c_i // c)
    keep = same_chunk & ((c_i >= r_i) if reverse else (c_i <= r_i))
    tri = jnp.where(keep, 1.0, 0.0).astype(BF16)
    lb = lb_ref[...]
    q = _silu(q_ref[...])
    f = lb + (1.0 - lb) / (1.0 + jnp.exp(-f_ref[...]))
    k = 1.0 - f
    logf = jnp.log(f)
    v = v_ref[...].astype(BF16)
    hi = logf.astype(BF16)
    lo = (logf - hi.astype(F32)).astype(BF16)
    cum = jnp.dot(tri, hi, preferred_element_type=F32) + jnp.dot(tri, lo, preferred_element_type=F32)
    totals = [cum[ci * c:ci * c + 1, :] if reverse else cum[(ci + 1) * c - 1:(ci + 1) * c, :]
              for ci in range(n_chunks)]
    w = cum.shape[1]
    total = jnp.concatenate([jnp.broadcast_to(t, (c, w)) for t in totals], axis=0)
    centre = jnp.concatenate([jnp.broadcast_to(cum[ci * c + c // 2:ci * c + c // 2 + 1, :], (c, w))
                              for ci in range(n_chunks)], axis=0)
    q_in = (q * jnp.exp(cum)).astype(BF16)
    k_out = (k * jnp.exp(total - cum)).astype(BF16)
    q_loc = (q * jnp.exp(cum - centre)).astype(BF16)
    k_loc = (k * jnp.exp(centre - cum)).astype(BF16)
    decays = [jnp.exp(t) for t in totals]
    order = range(n_chunks - 1, -1, -1) if reverse else range(n_chunks)
    for h in range(HG_HEADS):
        sl = slice(h * HG_DK, (h + 1) * HG_DK)
        att = lax.dot_general(q_loc[:, sl], k_loc[:, sl], (((1,), (1,)), ((), ())),
                              preferred_element_type=F32)
        att = jnp.where(keep, att, 0.0)
        o_loc = jnp.dot(att.astype(BF16), v[:, sl], preferred_element_type=F32)
        incs = [lax.dot_general(v[ci * c:(ci + 1) * c, sl], k_out[ci * c:(ci + 1) * c, sl],
                                (((0,), (0,)), ((), ())), preferred_element_type=F32)
                for ci in range(n_chunks)]
        st = st_ref[h]
        for ci in order:
            rows = slice(ci * c, (ci + 1) * c)
            o = o_loc[rows] + lax.dot_general(q_in[rows, sl], st.astype(BF16),
                                              (((1,), (1,)), ((), ())), preferred_element_type=F32)
            o_ref[rows, sl] = o.astype(o_ref.dtype)
            st = st * decays[ci][:, sl] + incs[ci]
        st_ref[h] = st


def _gla(z, lb, tc):
    b, ttot, _ = z.shape
    w = HG_HEADS * HG_DK
    lc = SEQ_BLOCK
    n_all, n_ctx = ttot // lc, tc // lc

    def col(cidx, reverse):
        return pl.BlockSpec((None, lc, w),
                            lambda bi, s: (bi, _seq_block(s, n_ctx, n_all, reverse), cidx))

    out = jax.ShapeDtypeStruct((b, ttot, w), BF16)
    state = pltpu.VMEM((HG_HEADS, HG_DK, HG_DK), F32)
    return pl.pallas_call(
        functools.partial(_gla_kernel, lc=lc),
        grid=(b, n_all),
        in_specs=[col(1, False), col(2, False), col(4, False),
                  col(1, True), col(3, True), col(4, True),
                  pl.BlockSpec((1, w), lambda bi, s: (0, 0))],
        out_specs=[col(0, False), col(0, True)],
        out_shape=[out, out],
        scratch_shapes=[state, state],
        compiler_params=_cparams(("parallel", "arbitrary")),
        name="hgrn2_bidir",
    )(z, z, z, z, z, z, lb.reshape(1, w))


def _final_kernel(x_ref, g_ref, o_ref):
    x = x_ref[0]
    ms = jnp.mean(x * x, axis=-1, keepdims=True)
    o_ref[...] = x * lax.rsqrt(ms + EPS) * g_ref[...]


def _final_norm(xa, g, tc):
    b, ttot, d = xa.shape
    t = ttot - tc
    tm = next(rows for rows in (1024, 512, SEQ_BLOCK) if t % rows == 0)
    return pl.pallas_call(
        _final_kernel,
        grid=(b, t // tm),
        in_specs=[pl.BlockSpec((pl.Element(1), pl.Element(tm), pl.Element(d)),
                               lambda bi, i: (bi, pl.multiple_of(tc + i * tm, SUBLANES), 0)),
                  pl.BlockSpec((1, d), lambda bi, i: (0, 0))],
        out_specs=pl.BlockSpec((None, tm, d), lambda bi, i: (bi, i, 0)),
        out_shape=jax.ShapeDtypeStruct((b, t, d), F32),
        compiler_params=_cparams(("parallel", "parallel")),
        name="final_norm",
    )(xa, g.reshape(1, d))


def _rope_tables(t, tc):
    pos = jnp.arange(t)
    row = (pos // GRID_W).astype(F32)
    col = (pos % GRID_W).astype(F32)
    inv = ROPE_BASE ** (-jnp.arange(ROPE_FREQS, dtype=F32) / ROPE_FREQS)
    ar, ac = row[:, None] * inv, col[:, None] * inv
    cos = jnp.concatenate([jnp.cos(ar), jnp.cos(ar), jnp.cos(ac), jnp.cos(ac)], axis=1)
    sin = jnp.concatenate([-jnp.sin(ar), jnp.sin(ar), -jnp.sin(ac), jnp.sin(ac)], axis=1)
    cos = jnp.concatenate([jnp.ones((tc, HEAD_DIM), F32), cos], axis=0)
    sin = jnp.concatenate([jnp.zeros((tc, HEAD_DIM), F32), sin], axis=0)
    return jnp.tile(cos, (1, LANES // HEAD_DIM)), jnp.tile(sin, (1, LANES // HEAD_DIM))


def _block_diag_dense(w):
    nblk, h, k = w.shape
    return jnp.einsum('nhk,nm->nhmk', w, jnp.eye(nblk, dtype=w.dtype)).reshape(nblk * h, nblk * k)


def kernel(x, c, ctx, c_ctx, w_mod, b_mod, norm_mix, norm_ffn, final_norm, w_in_ab, lru_conv_w, lru_conv_b, lru_wa, lru_ba, lru_wx, lru_bx, lru_lam, attn_sink, w_out_ab, ffn_w1, ffn_w3, ffn_w2, w_in_cd, s5_a_re, s5_a_im, s5_log_step, s5_b_re, s5_b_im, s5_c_re, s5_c_im, s5_d, s5_w_glu, s5_b_glu, hg_lb_raw, hg_norm, w_out_cd, moe_router, moe_w1, moe_w3, moe_w2):
    nb, t, d = x.shape
    tc = ctx.shape[1]
    depth = w_mod.shape[0]
    assert tc % SEQ_BLOCK == 0 and t % SEQ_BLOCK == 0 and t % GRID_W == 0

    xa = jnp.concatenate([ctx, x], axis=1)
    mod_rows = -(-(nb + 1) // SUBLANES) * SUBLANES
    cvec = jnp.zeros((mod_rows, d), F32).at[:nb].set(c).at[nb].set(c_ctx)
    mod = _modulation(cvec, w_mod, b_mod).reshape(depth, mod_rows, N_MOD, d)

    cos_t, sin_t = _rope_tables(t, tc)
    lb_soft = jax.nn.softmax(hg_lb_raw.astype(F32), axis=0)
    lb_table = jnp.cumsum(lb_soft, axis=0) - lb_soft[0:1]
    n_exp = moe_router.shape[2]
    s5w = jax.vmap(_s5_weights)(s5_a_re, s5_a_im, s5_log_step, s5_b_re, s5_b_im, s5_c_re, s5_c_im)
    ffn_w = [w.astype(BF16) for w in (ffn_w1, ffn_w3, ffn_w2)]
    moe_w = [w.astype(BF16) for w in (moe_w1, moe_w3, moe_w2)]

    for l in range(depth):
        j = l // 2
        mod_l = mod[l]
        if l % 2 == 0:
            z = _inproj(xa, mod_l, norm_mix[l], w_in_ab[j].astype(BF16), tc, nb)
            lru = _lru(z, lru_conv_w[j], lru_conv_b[j],
                       jnp.stack([_block_diag_dense(lru_wa[j, dr]) for dr in range(2)]).astype(BF16),
                       lru_ba[j],
                       jnp.stack([_block_diag_dense(lru_wx[j, dr]) for dr in range(2)]).astype(BF16),
                       lru_bx[j], jax.nn.softplus(-lru_lam[j]), tc)
            sink = jnp.zeros((1, LANES), F32).at[0, :ATT_HEADS].set(attn_sink[j])
            att = _attention(z, cos_t, sin_t, sink, tc)
            xa = _outproj_ab(xa, mod_l, lru[0], lru[1], z, att, w_out_ab[j].astype(BF16), tc, nb)
            xa = _ffn(xa, mod_l, norm_ffn[l], ffn_w[0], ffn_w[1], ffn_w[2], j, tc, nb)
        else:
            z = _inproj(xa, mod_l, norm_mix[l], w_in_cd[j].astype(BF16), tc, nb)
            y5 = _s5(z, s5w, j, s5_d[j], tc, s5_d.shape[1])
            o_f, o_b = _gla(z, lb_table[j], tc)
            router_t = jnp.zeros((SUBLANES, d), F32).at[:n_exp].set(moe_router[j].T)
            xa, *routed = _outproj_cd(xa, mod_l, y5, o_f, o_b, z, s5_w_glu[j].astype(BF16),
                                      s5_b_glu[j], hg_norm[j], w_out_cd[j].astype(BF16),
                                      norm_ffn[l], router_t, n_exp, tc, nb)
            xa = _moe(xa, mod_l, routed, moe_w[0], moe_w[1], moe_w[2], j, tc, nb)
    return _final_norm(xa, final_norm, tc)
```

```python
import functools
import math

import jax
import jax.numpy as jnp
from jax import lax
from jax.experimental import pallas as pl
from jax.experimental.pallas import tpu as pltpu

F32 = jnp.float32
BF16 = jnp.bfloat16

EPS = 1e-6
GRID_W = 64
LRU_C = 8.0
CONV_W = 4
ATT_HEADS = 8
ATT_KV_HEADS = 2
ATT_GROUP = ATT_HEADS // ATT_KV_HEADS
HEAD_DIM = 64
WINDOW = 128
BLOCK_Q = 128
ROPE_FREQS = HEAD_DIM // 4
ROPE_BASE = 10000.0
S5_GROUP = 16
S5_CHUNK = 8
HG_HEADS = 4
HG_DK = 128
HG_CHUNK = 64
SEQ_BLOCK = 256
N_MOD = 6
MOE_PASS_ROWS = 256
MOE_TAIL_STEP = 64
LANES = 128
SUBLANES = 8
MXU_TILE = 256
VMEM_LIMIT = 56 * 1024 * 1024
NEG = -1e30
LOG2E = math.log2(math.e)


def _cparams(sem):
    return pltpu.CompilerParams(dimension_semantics=sem, vmem_limit_bytes=VMEM_LIMIT)


def _row_tile(ttot):
    for tm in (768, 1024, 512, 256):
        if ttot % tm == 0:
            return tm
    raise ValueError(f"unsupported token count {ttot}")


def _tall_row_tile(ttot):
    for tm in range(min(ttot, 1088) // 16 * 16, 15, -16):
        if ttot % tm == 0:
            return tm
    raise ValueError(f"unsupported token count {ttot}")


def _sigmoid(x):
    return 0.5 * jnp.tanh(0.5 * x) + 0.5


def _silu(x):
    return x * _sigmoid(x)


def _gelu(x):
    return 0.5 * x * (1.0 + jnp.tanh(math.sqrt(2.0 / math.pi) * (x + 0.044715 * (x * x * x))))


def _norm_mod(x, g, mb, mc, row0, tc, shift_idx, scale_idx):
    ms = jnp.mean(x * x, axis=-1, keepdims=True)
    y = x * lax.rsqrt(ms + EPS) * g
    rows = row0 + lax.broadcasted_iota(jnp.int32, (x.shape[0], 1), 0)
    is_ctx = rows < tc
    scale = jnp.where(is_ctx, mc[scale_idx:scale_idx + 1], mb[scale_idx:scale_idx + 1])
    shift = jnp.where(is_ctx, mc[shift_idx:shift_idx + 1], mb[shift_idx:shift_idx + 1])
    return y * (1.0 + scale) + shift


def _gate_mod(mb, mc, idx, row0, tc, n):
    rows = row0 + lax.broadcasted_iota(jnp.int32, (n, 1), 0)
    return jnp.where(rows < tc, mc[idx:idx + 1], mb[idx:idx + 1])


def _mod_kernel(c_ref, w_ref, b_ref, o_ref):
    s = _silu(c_ref[...])
    o_ref[...] = jnp.dot(s, w_ref[...], preferred_element_type=F32,
                         precision=lax.Precision.HIGHEST) + b_ref[...]


def _modulation(cvec, w_mod, b_mod):
    depth, d, n = w_mod.shape
    tn = 1536 if n % 1536 == 0 else n
    rows = cvec.shape[0]
    return pl.pallas_call(
        _mod_kernel,
        grid=(depth, n // tn),
        in_specs=[pl.BlockSpec((rows, d), lambda l, j: (0, 0)),
                  pl.BlockSpec((None, d, tn), lambda l, j: (l, 0, j)),
                  pl.BlockSpec((None, 1, tn), lambda l, j: (l, 0, j))],
        out_specs=pl.BlockSpec((None, rows, tn), lambda l, j: (l, 0, j)),
        out_shape=jax.ShapeDtypeStruct((depth, rows, n), F32),
        compiler_params=_cparams(("arbitrary", "arbitrary")),
        name="modulation",
    )(cvec, w_mod, b_mod.reshape(depth, 1, n))


def _inproj_kernel(x_ref, mb_ref, mc_ref, g_ref, w_ref, o_ref, *, tm, tc):
    i = pl.program_id(1)
    h = _norm_mod(x_ref[...], g_ref[...], mb_ref[...], mc_ref[...], i * tm, tc, 0, 1)
    o_ref[...] = jnp.dot(h.astype(BF16), w_ref[...], preferred_element_type=F32)


def _inproj(xa, mod_l, g, w, tc, nb):
    b, ttot, d = xa.shape
    n = w.shape[1]
    tm = _row_tile(ttot)
    return pl.pallas_call(
        functools.partial(_inproj_kernel, tm=tm, tc=tc),
        grid=(b, ttot // tm),
        in_specs=[pl.BlockSpec((None, tm, d), lambda bi, i: (bi, i, 0)),
                  pl.BlockSpec((None, N_MOD, d), lambda bi, i: (bi, 0, 0)),
                  pl.BlockSpec((None, N_MOD, d), lambda bi, i: (nb, 0, 0)),
                  pl.BlockSpec((1, d), lambda bi, i: (0, 0)),
                  pl.BlockSpec((d, n), lambda bi, i: (0, 0))],
        out_specs=pl.BlockSpec((None, tm, n), lambda bi, i: (bi, i, 0)),
        out_shape=jax.ShapeDtypeStruct((b, ttot, n), F32),
        compiler_params=_cparams(("parallel", "parallel")),
        name="inproj",
    )(xa, mod_l, mod_l, g.reshape(1, d), w)


def _seq_block(step, n_ctx, n_all, reverse):
    if not reverse:
        return step
    return jnp.where(step < n_ctx, n_ctx - 1 - step, n_all - 1 - step + n_ctx)


def _lru_kernel(*refs, lc, n_ctx, n_all):
    cw_ref, cb_ref = refs[:2]
    n_in = 8
    ins = [refs[2 + d * n_in:2 + (d + 1) * n_in] for d in range(2)]
    outs = refs[2 + 2 * n_in:4 + 2 * n_in]
    scratch = refs[4 + 2 * n_in:]
    for d in range(2):
        _lru_block(*ins[d], cw_ref, cb_ref, outs[d], *scratch[4 * d:4 * d + 4],
                   lc=lc, n_ctx=n_ctx, n_all=n_all, reverse=bool(d))


def _lru_block(u_ref, hp_ref, hn_ref, wa_ref, ba_ref, wx_ref, bx_ref, sp_ref, cw_ref, cb_ref,
               o_ref, uc_ref, a_ref, b_ref, h_ref, *, lc, n_ctx, n_all, reverse):
    step = pl.program_id(1)
    blk = _seq_block(step, n_ctx, n_all, reverse)
    has_prev = jnp.logical_and(blk != 0, blk != n_ctx)
    has_next = jnp.logical_and(blk != n_ctx - 1, blk != n_all - 1)

    @pl.when(step == 0)
    def _():
        h_ref[...] = jnp.zeros_like(h_ref)

    w0, w1, w2, w3 = (cw_ref[j:j + 1, :] for j in range(CONV_W))
    cb = cb_ref[...]
    u = u_ref[...]
    uc_ref[...] = (cb + pltpu.roll(u, 2, 0) * w0 + pltpu.roll(u, 1, 0) * w1 + u * w2
                   + pltpu.roll(u, lc - 1, 0) * w3)
    row = lax.broadcasted_iota(jnp.int32, (SUBLANES, 1), 0)
    hp = jnp.where(has_prev, hp_ref[...], 0.0)
    hn = jnp.where(has_next, hn_ref[...], 0.0)
    u0 = u_ref[0:SUBLANES, :]
    u1 = u_ref[SUBLANES:2 * SUBLANES, :]
    uc_ref[0:SUBLANES, :] = (
        cb + jnp.where(row < 2, pltpu.roll(hp, 2, 0), pltpu.roll(u0, 2, 0)) * w0
        + jnp.where(row < 1, pltpu.roll(hp, 1, 0), pltpu.roll(u0, 1, 0)) * w1 + u0 * w2
        + jnp.where(row < SUBLANES - 1, pltpu.roll(u0, SUBLANES - 1, 0),
                    pltpu.roll(u1, SUBLANES - 1, 0)) * w3)
    ul = u_ref[lc - SUBLANES:lc, :]
    um = u_ref[lc - 2 * SUBLANES:lc - SUBLANES, :]
    uc_ref[lc - SUBLANES:lc, :] = (
        cb + jnp.where(row < 2, pltpu.roll(um, 2, 0), pltpu.roll(ul, 2, 0)) * w0
        + jnp.where(row < 1, pltpu.roll(um, 1, 0), pltpu.roll(ul, 1, 0)) * w1 + ul * w2
        + jnp.where(row < SUBLANES - 1, pltpu.roll(ul, SUBLANES - 1, 0),
                    pltpu.roll(hn, SUBLANES - 1, 0)) * w3)

    uc = uc_ref[...]
    ub = uc.astype(BF16)
    r = _sigmoid(jnp.dot(ub, wa_ref[...], preferred_element_type=F32) + ba_ref[...])
    gi = _sigmoid(jnp.dot(ub, wx_ref[...], preferred_element_type=F32) + bx_ref[...])
    log_a = -LRU_C * r * sp_ref[...]
    a = jnp.exp(log_a)
    b = jnp.sqrt(1.0 - a * a) * (gi * uc)
    sub = lax.broadcasted_iota(jnp.int32, (lc, 1), 0) % SUBLANES
    dist = 1
    while dist < SUBLANES:
        inside = (sub < SUBLANES - dist) if reverse else (sub >= dist)
        shift = (lc - dist) if reverse else dist
        a_sh = jnp.where(inside, pltpu.roll(a, shift, 0), 1.0)
        b_sh = jnp.where(inside, pltpu.roll(b, shift, 0), 0.0)
        b = a * b_sh + b
        a = a * a_sh
        dist *= 2
    a_ref[...] = a
    b_ref[...] = b
    n_grp = lc // SUBLANES

    def body(i, carry):
        t = (n_grp - 1 - i) if reverse else i
        rows = pl.ds(pl.multiple_of(t * SUBLANES, SUBLANES), SUBLANES)
        h = a_ref[rows, :] * carry + b_ref[rows, :]
        b_ref[rows, :] = h
        return h[0:1, :] if reverse else h[SUBLANES - 1:SUBLANES, :]

    h_ref[0:1, :] = lax.fori_loop(0, n_grp, body, h_ref[0:1, :], unroll=4)
    o_ref[...] = b_ref[...].astype(o_ref.dtype)


def _lru(z, conv_w, conv_b, wa_bd, ba, wx_bd, bx, sp, tc):
    b, ttot, _ = z.shape
    w = conv_w.shape[1]
    lc = SEQ_BLOCK
    n_all, n_ctx = ttot // lc, tc // lc
    hb = lc // SUBLANES
    n_h = ttot // SUBLANES
    col = 1

    def direction(d):
        def blk(s):
            return _seq_block(s, n_ctx, n_all, bool(d))

        vec = pl.BlockSpec((None, 1, w), lambda bi, s: (d, 0, 0))
        mat = pl.BlockSpec((None, w, w), lambda bi, s: (d, 0, 0))
        specs = [pl.BlockSpec((None, lc, w), lambda bi, s: (bi, blk(s), col)),
                 pl.BlockSpec((None, SUBLANES, w),
                              lambda bi, s: (bi, jnp.maximum(blk(s) * hb - 1, 0), col)),
                 pl.BlockSpec((None, SUBLANES, w),
                              lambda bi, s: (bi, jnp.minimum((blk(s) + 1) * hb, n_h - 1), col)),
                 mat, vec, mat, vec, vec]
        return specs, pl.BlockSpec((None, lc, w), lambda bi, s: (bi, blk(s), 0))

    (in_f, out_f), (in_r, out_r) = direction(0), direction(1)
    out = jax.ShapeDtypeStruct((b, ttot, w), BF16)
    per_dir = [pltpu.VMEM((lc, w), F32)] * 3 + [pltpu.VMEM((SUBLANES, w), F32)]
    vecs = lambda x: x.reshape(2, 1, w)
    args = (z, z, z, wa_bd, vecs(ba), wx_bd, vecs(bx), vecs(sp))
    return pl.pallas_call(
        functools.partial(_lru_kernel, lc=lc, n_ctx=n_ctx, n_all=n_all),
        grid=(b, n_all),
        in_specs=[pl.BlockSpec((CONV_W, w), lambda bi, s: (0, 0)),
                  pl.BlockSpec((1, w), lambda bi, s: (0, 0))] + in_f + in_r,
        out_specs=[out_f, out_r],
        out_shape=[out, out],
        scratch_shapes=per_dir + per_dir,
        compiler_params=_cparams(("parallel", "arbitrary")),
        name="rglru_bidir",
    )(conv_w, conv_b.reshape(1, w), *args, *args)


def _rope(x, cos, sin_signed):
    lane = lax.broadcasted_iota(jnp.int32, x.shape, 1)
    swapped = jnp.where(lane % 32 < 16, pltpu.roll(x, LANES - 16, 1), pltpu.roll(x, 16, 1))
    return x * cos + swapped * sin_signed


def _pair_layout(x, x_sw, head):
    lo = lax.broadcasted_iota(jnp.int32, x.shape, 1) < HEAD_DIM
    first, second = (x, x_sw) if head == 0 else (x_sw, x)
    return jnp.concatenate([jnp.where(lo, first, 0.0), jnp.where(lo, 0.0, second)],
                           axis=0).astype(BF16)


def _attend(q_block, keys, vals, bias, sink_row, o_ref):
    nk = keys.shape[0]
    keys_sw = pltpu.roll(keys, HEAD_DIM, 1)
    vals_t = vals.T
    top = lax.broadcasted_iota(jnp.int32, (LANES, BLOCK_Q), 0) < HEAD_DIM
    zero_t = jnp.zeros((HEAD_DIM, nk), F32)
    pairs_per_kv = ATT_GROUP // 2
    for kv in range(ATT_KV_HEADS):
        kab = _pair_layout(keys, keys_sw, kv)
        vt = vals_t[kv * HEAD_DIM:(kv + 1) * HEAD_DIM, :]
        vab_t = jnp.concatenate([jnp.concatenate([vt, zero_t], axis=1),
                                 jnp.concatenate([zero_t, vt], axis=1)], axis=0).astype(BF16)
        for g in range(kv * pairs_per_kv, (kv + 1) * pairs_per_kv):
            s_t = lax.dot_general(kab, q_block(g), (((1,), (1,)), ((), ())),
                                  preferred_element_type=F32)
            halves = []
            recips = []
            for half in range(2):
                sh = s_t[half * nk:(half + 1) * nk, :]
                if bias is not None:
                    sh = sh + bias
                sk = sink_row[:, 2 * g + half:2 * g + half + 1]
                m = jnp.maximum(jnp.max(sh, axis=0, keepdims=True), sk)
                p = jnp.exp2(sh - m)
                recips.append(1.0 / (jnp.sum(p, axis=0, keepdims=True) + jnp.exp2(sk - m)))
                halves.append(p.astype(BF16))
            o_t = jnp.dot(vab_t, jnp.concatenate(halves, axis=0), preferred_element_type=F32)
            o_t = o_t * jnp.where(top, recips[0], recips[1])
            o_ref[:, g * LANES:(g + 1) * LANES] = o_t.T.astype(o_ref.dtype)


def _attn_kernel(q_ref, kp_ref, k0_ref, kn_ref, vp_ref, v0_ref, vn_ref, kc_ref, vc_ref,
                 cq_ref, sq_ref, cp_ref, sp_ref, cn_ref, sn_ref, sink_ref, o_ref, *, n_ctx, n_lat):
    i = pl.program_id(1)
    qscale = HEAD_DIM ** -0.5 * LOG2E
    sink_row = sink_ref[...] * LOG2E

    @pl.when(i < n_ctx)
    def _():
        def q_block(g):
            return (q_ref[:, g * LANES:(g + 1) * LANES] * qscale).astype(BF16)
        _attend(q_block, kc_ref[...], vc_ref[...], None, sink_row, o_ref)

    @pl.when(i >= n_ctx)
    def _():
        n = i - n_ctx
        cq, sq = cq_ref[...], sq_ref[...]

        def q_block(g):
            return (_rope(q_ref[:, g * LANES:(g + 1) * LANES], cq, sq) * qscale).astype(BF16)

        kp = _rope(kp_ref[...], cp_ref[...], sp_ref[...])
        k0 = _rope(k0_ref[...], cq, sq)
        kn = _rope(kn_ref[...], cn_ref[...], sn_ref[...])
        keys = jnp.concatenate([kp, k0, kn, kc_ref[...]], axis=0)
        vals = jnp.concatenate([vp_ref[...], v0_ref[...], vn_ref[...], vc_ref[...]], axis=0)
        nk = keys.shape[0]
        qpos = n * BLOCK_Q + lax.broadcasted_iota(jnp.int32, (nk, BLOCK_Q), 1)
        key = lax.broadcasted_iota(jnp.int32, (nk, BLOCK_Q), 0)
        kpos = (n - 1) * BLOCK_Q + key
        local = (jnp.abs(qpos - kpos) <= WINDOW) & (kpos >= 0) & (kpos < n_lat * BLOCK_Q)
        bias = jnp.where(local | (key >= 3 * BLOCK_Q), 0.0, NEG)
        _attend(q_block, keys, vals, bias, sink_row, o_ref)


def _attention(z, cos_t, sin_t, sink, tc):
    b, ttot, _ = z.shape
    n_all, n_ctx = ttot // BLOCK_Q, tc // BLOCK_Q
    n_lat = n_all - n_ctx
    kcol = (2 * 512 + ATT_HEADS * HEAD_DIM) // LANES
    vcol = kcol + 1
    qcol = 2 * 512 // (ATT_HEADS * HEAD_DIM)

    def prev(i):
        return jnp.maximum(i - 1, 0)

    def nxt(i):
        return jnp.minimum(i + 1, n_all - 1)

    def rows(col, f):
        return pl.BlockSpec((None, BLOCK_Q, LANES), lambda bi, i: (bi, f(i), col))

    def tab(f):
        return pl.BlockSpec((BLOCK_Q, LANES), lambda bi, i: (f(i), 0))

    same = lambda i: i
    return pl.pallas_call(
        functools.partial(_attn_kernel, n_ctx=n_ctx, n_lat=n_lat),
        grid=(b, n_all),
        in_specs=[pl.BlockSpec((None, BLOCK_Q, ATT_HEADS * HEAD_DIM), lambda bi, i: (bi, i, qcol)),
                  rows(kcol, prev), rows(kcol, same), rows(kcol, nxt),
                  rows(vcol, prev), rows(vcol, same), rows(vcol, nxt),
                  pl.BlockSpec((None, tc, LANES), lambda bi, i: (bi, 0, kcol)),
                  pl.BlockSpec((None, tc, LANES), lambda bi, i: (bi, 0, vcol)),
                  tab(same), tab(same), tab(prev), tab(prev), tab(nxt), tab(nxt),
                  pl.BlockSpec((1, LANES), lambda bi, i: (0, 0))],
        out_specs=pl.BlockSpec((None, BLOCK_Q, ATT_HEADS * HEAD_DIM), lambda bi, i: (bi, i, 0)),
        out_shape=jax.ShapeDtypeStruct((b, ttot, ATT_HEADS * HEAD_DIM), BF16),
        compiler_params=_cparams(("parallel", "parallel")),
        name="window_gqa",
    )(z, z, z, z, z, z, z, z, z, cos_t, sin_t, cos_t, sin_t, cos_t, sin_t, sink)


def _outproj_ab_kernel(x_ref, mb_ref, mc_ref, lf_ref, lb_ref, g_ref, at_ref, w_ref, o_ref, *, tm, tc):
    i = pl.program_id(1)
    lru = (lf_ref[...].astype(F32) + lb_ref[...].astype(F32)) * _gelu(g_ref[...])
    mix = jnp.concatenate([lru.astype(BF16), at_ref[...]], axis=1)
    dx = jnp.dot(mix, w_ref[...], preferred_element_type=F32)
    gate = _gate_mod(mb_ref[...], mc_ref[...], 2, i * tm, tc, tm)
    o_ref[...] = x_ref[...] + gate * dx


def _outproj_ab(xa, mod_l, lru_f, lru_b, z, att, w_out, tc, nb):
    b, ttot, d = xa.shape
    w = lru_f.shape[2]
    tm = _row_tile(ttot)
    tile = lambda width, col: pl.BlockSpec((None, tm, width), lambda bi, i: (bi, i, col))
    return pl.pallas_call(
        functools.partial(_outproj_ab_kernel, tm=tm, tc=tc),
        grid=(b, ttot // tm),
        in_specs=[tile(d, 0),
                  pl.BlockSpec((None, N_MOD, d), lambda bi, i: (bi, 0, 0)),
                  pl.BlockSpec((None, N_MOD, d), lambda bi, i: (nb, 0, 0)),
                  tile(w, 0), tile(w, 0), tile(w, 0), tile(att.shape[2], 0),
                  pl.BlockSpec(w_out.shape, lambda bi, i: (0, 0))],
        out_specs=tile(d, 0),
        out_shape=jax.ShapeDtypeStruct(xa.shape, F32),
        compiler_params=_cparams(("parallel", "parallel")),
        name="outproj_ab",
    )(xa, mod_l, mod_l, lru_f, lru_b, z, att, w_out)


def _outproj_cd_kernel(x_ref, mb_ref, mc_ref, y_ref, of_ref, ob_ref, g_ref, wg_ref, bg_ref, gn_ref,
                       w_ref, gf_ref, r_ref, o_ref, h_ref, wt_ref, cnt_ref, *, tm, tc, n_exp):
    i = pl.program_id(1)
    y = _gelu(y_ref[...])
    s5 = y * _sigmoid(jnp.dot(y.astype(BF16), wg_ref[...], preferred_element_type=F32) + bg_ref[...])
    o = of_ref[...].astype(F32) + ob_ref[...].astype(F32)
    parts = []
    for h in range(HG_HEADS):
        oh = o[:, h * HG_DK:(h + 1) * HG_DK]
        ms = jnp.mean(oh * oh, axis=-1, keepdims=True)
        parts.append(oh * lax.rsqrt(ms + EPS) * gn_ref[...])
    hg = jnp.concatenate(parts, axis=1) * _silu(g_ref[...])
    mix = jnp.concatenate([s5.astype(BF16), hg.astype(BF16)], axis=1)
    dx = jnp.dot(mix, w_ref[...], preferred_element_type=F32)
    gate = _gate_mod(mb_ref[...], mc_ref[...], 2, i * tm, tc, tm)
    x_new = x_ref[...] + gate * dx
    o_ref[...] = x_new
    h = _norm_mod(x_new, gf_ref[...], mb_ref[...], mc_ref[...], i * tm, tc, 3, 4)
    _route_tokens(h, r_ref, h_ref, wt_ref, cnt_ref, n_exp)


def _outproj_cd(xa, mod_l, y5, o_f, o_b, z, w_glu, b_glu, hg_norm, w_out, g_ffn, router_t, n_exp,
                tc, nb):
    b, ttot, d = xa.shape
    w = y5.shape[2]
    tm = _row_tile(ttot)
    n_t = ttot // tm
    gcol = z.shape[2] // w - 1
    tile = lambda width, col: pl.BlockSpec((None, tm, width), lambda bi, i: (bi, i, col))
    return pl.pallas_call(
        functools.partial(_outproj_cd_kernel, tm=tm, tc=tc, n_exp=n_exp),
        grid=(b, n_t),
        in_specs=[tile(d, 0),
                  pl.BlockSpec((None, N_MOD, d), lambda bi, i: (bi, 0, 0)),
                  pl.BlockSpec((None, N_MOD, d), lambda bi, i: (nb, 0, 0)),
                  tile(w, 0), tile(w, 0), tile(w, 0), tile(w, gcol),
                  pl.BlockSpec(w_glu.shape, lambda bi, i: (0, 0)),
                  pl.BlockSpec((1, w), lambda bi, i: (0, 0)),
                  pl.BlockSpec((1, HG_DK), lambda bi, i: (0, 0)),
                  pl.BlockSpec(w_out.shape, lambda bi, i: (0, 0)),
                  pl.BlockSpec((1, d), lambda bi, i: (0, 0)),
                  pl.BlockSpec((SUBLANES, d), lambda bi, i: (0, 0))],
        out_specs=[tile(d, 0),
                   pl.BlockSpec((tm, d), lambda bi, i: (bi * n_t + i, 0)),
                   pl.BlockSpec((SUBLANES, tm), lambda bi, i: (0, bi * n_t + i)),
                   pl.BlockSpec((None, SUBLANES, LANES), lambda bi, i: (bi * n_t + i, 0, 0))],
        out_shape=[jax.ShapeDtypeStruct(xa.shape, F32),
                   jax.ShapeDtypeStruct((b * ttot, d), BF16),
                   jax.ShapeDtypeStruct((SUBLANES, b * ttot), F32),
                   jax.ShapeDtypeStruct((b * n_t, SUBLANES, LANES), jnp.int32)],
        compiler_params=_cparams(("parallel", "parallel")),
        name="outproj_cd_route",
    )(xa, mod_l, mod_l, y5, o_f, o_b, z, w_glu, b_glu.reshape(1, w), hg_norm.reshape(1, HG_DK), w_out,
      g_ffn.reshape(1, d), router_t)


def _ffn_kernel(x_ref, mb_ref, mc_ref, g_ref, w1_ref, w3_ref, w2_ref, o_ref, h_ref, acc_ref, *, tm, tc):
    i = pl.program_id(1)
    k = pl.program_id(2)

    @pl.when(k == 0)
    def _():
        h = _norm_mod(x_ref[...], g_ref[...], mb_ref[...], mc_ref[...], i * tm, tc, 3, 4)
        h_ref[...] = h.astype(BF16)
        acc_ref[...] = jnp.zeros_like(acc_ref)

    h = h_ref[...]
    tf = w1_ref.shape[1]
    half = (tf // LANES + 1) // 2 * LANES
    y = None
    for f0, f1 in ([(0, half), (half, tf)] if 0 < half < tf else [(0, tf)]):
        a = jnp.dot(h, w1_ref[:, f0:f1], preferred_element_type=F32)
        c = jnp.dot(h, w3_ref[:, f0:f1], preferred_element_type=F32)
        part = jnp.dot((_silu(a) * c).astype(BF16), w2_ref[f0:f1, :], preferred_element_type=F32)
        y = part if y is None else y + part
    acc_ref[...] += y

    @pl.when(k == pl.num_programs(2) - 1)
    def _():
        gate = _gate_mod(mb_ref[...], mc_ref[...], 5, i * tm, tc, tm)
        o_ref[...] = x_ref[...] + gate * acc_ref[...]


def _ffn(xa, mod_l, g, w1, w3, w2, layer, tc, nb):
    b, ttot, d = xa.shape
    f = w1.shape[2]
    tm = _tall_row_tile(ttot)
    tf = f // 2 if (f // 2) % LANES == 0 else f
    return pl.pallas_call(
        functools.partial(_ffn_kernel, tm=tm, tc=tc),
        grid=(b, ttot // tm, f // tf),
        in_specs=[pl.BlockSpec((None, tm, d), lambda bi, i, k: (bi, i, 0)),
                  pl.BlockSpec((None, N_MOD, d), lambda bi, i, k: (bi, 0, 0)),
                  pl.BlockSpec((None, N_MOD, d), lambda bi, i, k: (nb, 0, 0)),
                  pl.BlockSpec((1, d), lambda bi, i, k: (0, 0)),
                  pl.BlockSpec((None, d, tf), lambda bi, i, k: (layer, 0, k)),
                  pl.BlockSpec((None, d, tf), lambda bi, i, k: (layer, 0, k)),
                  pl.BlockSpec((None, tf, d), lambda bi, i, k: (layer, k, 0))],
        out_specs=pl.BlockSpec((None, tm, d), lambda bi, i, k: (bi, i, 0)),
        out_shape=jax.ShapeDtypeStruct(xa.shape, F32),
        scratch_shapes=[pltpu.VMEM((tm, d), BF16), pltpu.VMEM((tm, d), F32)],
        compiler_params=_cparams(("parallel", "parallel", "arbitrary")),
        name="ffn_swiglu",
    )(xa, mod_l, mod_l, g.reshape(1, d), w1, w3, w2)


def _route_tokens(h, r_ref, h_ref, wt_ref, cnt_ref, n_exp):
    nt = (((1,), (1,)), ((), ()))
    h_hi = h.astype(BF16)
    h_lo = (h - h_hi.astype(F32)).astype(BF16)
    r = r_ref[...]
    r_hi = r.astype(BF16)
    r_lo = (r - r_hi.astype(F32)).astype(BF16)
    logits = (lax.dot_general(r_hi, h_hi, nt, preferred_element_type=F32)
              + lax.dot_general(r_lo, h_hi, nt, preferred_element_type=F32)
              + lax.dot_general(r_hi, h_lo, nt, preferred_element_type=F32))
    sub = lax.broadcasted_iota(jnp.int32, logits.shape, 0)
    h_ref[...] = h_hi
    logits = jnp.where(sub < n_exp, logits, NEG)
    m1 = jnp.max(logits, axis=0, keepdims=True)
    i1 = jnp.min(jnp.where(logits == m1, sub, SUBLANES), axis=0, keepdims=True)
    rest = jnp.where(sub == i1, NEG, logits)
    m2 = jnp.max(rest, axis=0, keepdims=True)
    i2 = jnp.min(jnp.where(rest == m2, sub, SUBLANES), axis=0, keepdims=True)
    e2 = jnp.exp(m2 - m1)
    g1 = 1.0 / (1.0 + e2)
    wt = jnp.where(sub == i1, g1, 0.0) + jnp.where(sub == i2, e2 * g1, 0.0)
    wt_ref[...] = wt
    cnt = jnp.sum(jnp.where(wt > 0.0, 1.0, 0.0), axis=1, keepdims=True)
    cnt_ref[...] = jnp.broadcast_to(cnt, cnt_ref.shape).astype(jnp.int32)


def _moe_kernel(cnt_ref, x_ref, *refs, tm, tc, n_exp, n_t, group):
    mb_refs = refs[:group]
    mc_ref, h_ref, wt_ref, w1_ref, w3_ref, w2_ref, o_ref, rank_ref = refs[group:]
    p = pl.program_id(0)
    e = pl.program_id(1)
    f = w1_ref.shape[1]
    half = (f // LANES + 1) // 2 * LANES
    f_split = [(0, half), (half, f)] if 0 < half < f else [(0, f)]

    @pl.when(e == 0)
    def _():
        o_ref[...] = jnp.zeros_like(o_ref)
        r_i = lax.broadcasted_iota(jnp.int32, (tm, tm), 0)
        c_i = lax.broadcasted_iota(jnp.int32, (tm, tm), 1)
        before = jnp.where(r_i < c_i, 1.0, 0.0).astype(BF16)
        for j in range(group):
            sel = wt_ref[:, j * tm:(j + 1) * tm] > 0.0
            rank = jnp.dot(jnp.where(sel, 1.0, 0.0).astype(BF16), before, preferred_element_type=F32)
            rank_ref[j] = jnp.where(sel, rank, -1.0)

    counts = [cnt_ref[(p * group + j) * n_exp + e] for j in range(group)]

    offsets = [jnp.int32(0)]
    for j in range(group - 1):
        offsets.append(offsets[-1] + counts[j])
    total = offsets[-1] + counts[-1]

    def expert_pass(m_rows, base):
        slot = (lax.broadcasted_iota(jnp.int32, (m_rows, tm), 0) + base).astype(F32)
        onehots = []
        xe = None
        gate = None
        for j in range(group):
            r_row = rank_ref[j, pl.ds(e, 1), :]
            r_row = jnp.where(r_row >= 0.0, r_row + offsets[j].astype(F32), -1.0)
            w_row = wt_ref[pl.ds(e, 1), j * tm:(j + 1) * tm]
            hit = r_row == slot
            onehot = jnp.where(hit, 1.0, 0.0).astype(BF16)
            onehots.append(onehot)
            xj = jnp.dot(onehot, h_ref[j * tm:(j + 1) * tm, :], preferred_element_type=F32)
            gj = jnp.sum(jnp.where(hit, w_row, 0.0), axis=1, keepdims=True)
            xe = xj if xe is None else xe + xj
            gate = gj if gate is None else gate + gj
        xe = xe.astype(BF16)
        y = None
        for f0, f1 in f_split:
            a = jnp.dot(xe, w1_ref[:, f0:f1], preferred_element_type=F32)
            c = jnp.dot(xe, w3_ref[:, f0:f1], preferred_element_type=F32)
            part = jnp.dot((_silu(a) * c * gate).astype(BF16), w2_ref[f0:f1, :],
                           preferred_element_type=F32)
            y = part if y is None else y + part
        y = y.astype(BF16)
        for j in range(group):
            o_ref[j * tm:(j + 1) * tm, :] += lax.dot_general(
                onehots[j], y, (((0,), (0,)), ((), ())), preferred_element_type=F32)

    n_full = total // MOE_PASS_ROWS

    def full_pass(it, carry):
        expert_pass(MOE_PASS_ROWS, it * MOE_PASS_ROWS)
        return carry

    lax.fori_loop(0, n_full, full_pass, 0)
    rest = total - n_full * MOE_PASS_ROWS
    for tail in range(MOE_TAIL_STEP, MOE_PASS_ROWS + 1, MOE_TAIL_STEP):
        @pl.when(jnp.logical_and(rest > tail - MOE_TAIL_STEP, rest <= tail))
        def _(tail=tail):
            expert_pass(tail, n_full * MOE_PASS_ROWS)

    @pl.when(e == n_exp - 1)
    def _():
        for j in range(group):
            i = (p * group + j) % n_t
            gate5 = _gate_mod(mb_refs[j][...], mc_ref[...], 5, i * tm, tc, tm)
            rows = slice(j * tm, (j + 1) * tm)
            o_ref[rows, :] = x_ref[rows, :] + gate5 * o_ref[rows, :]


def _moe(xa, mod_l, routed, w1, w3, w2, layer, tc, nb):
    b, ttot, d = xa.shape
    _, n_exp, _, f = w1.shape
    tm = _row_tile(ttot)
    n_t = ttot // tm
    group = 2 if (b * n_t) % 2 == 0 else 1
    h, wt, cnt = routed
    cnt = cnt[:, :n_exp, 0].reshape(-1)
    rows = group * tm
    mods = [pl.BlockSpec((None, N_MOD, d), functools.partial(
        lambda p, e, c, j: ((p * group + j) // n_t, 0, 0), j=j)) for j in range(group)]
    grid_spec = pltpu.PrefetchScalarGridSpec(
        num_scalar_prefetch=1,
        grid=(b * n_t // group, n_exp),
        in_specs=[pl.BlockSpec((rows, d), lambda p, e, c: (p, 0), pipeline_mode=pl.Buffered(1))]
        + mods
        + [pl.BlockSpec((None, N_MOD, d), lambda p, e, c: (nb, 0, 0)),
           pl.BlockSpec((rows, d), lambda p, e, c: (p, 0)),
           pl.BlockSpec((SUBLANES, rows), lambda p, e, c: (0, p)),
           pl.BlockSpec((None, None, d, f), lambda p, e, c: (layer, e, 0, 0)),
           pl.BlockSpec((None, None, d, f), lambda p, e, c: (layer, e, 0, 0)),
           pl.BlockSpec((None, None, f, d), lambda p, e, c: (layer, e, 0, 0))],
        out_specs=pl.BlockSpec((rows, d), lambda p, e, c: (p, 0)),
        scratch_shapes=[pltpu.VMEM((group, SUBLANES, tm), F32)])
    out = pl.pallas_call(
        functools.partial(_moe_kernel, tm=tm, tc=tc, n_exp=n_exp, n_t=n_t, group=group),
        grid_spec=grid_spec,
        out_shape=jax.ShapeDtypeStruct((b * ttot, d), F32),
        compiler_params=_cparams(("parallel", "arbitrary")),
        name="moe_experts",
    )(cnt, xa.reshape(b * ttot, d), *([mod_l] * group), mod_l, h, wt, w1, w3, w2)
    return out.reshape(b, ttot, d)


def _spread_groups(x, inner, row_group, n_grp):
    cols_in = x.shape[2]
    cols_out = cols_in * n_grp
    q = jnp.arange(cols_out)
    src = (q // (n_grp * inner)) * inner + q % inner
    tile = (jnp.arange(cols_in)[:, None] == src[None, :]).astype(BF16)
    col_group = (q // inner) % n_grp
    y = jnp.dot(x.astype(BF16), tile)
    return jnp.where(row_group[:, None] == col_group[None, :], y, jnp.zeros((), BF16))


def _s5_weights(a_re, a_im, log_step, b_re, b_im, c_re, c_im):
    L = S5_CHUNK
    lr = jnp.minimum(a_re, -1e-4)
    li = a_im
    dt = jnp.exp(log_step)[..., None]
    mag, ang = lr * dt, li * dt
    lbr, lbi = jnp.exp(mag) * jnp.cos(ang), jnp.exp(mag) * jnp.sin(ang)
    zr, zi = lbr - 1.0, lbi
    den = lr * lr + li * li
    fr = (zr * lr + zi * li) / den
    fi = (zi * lr - zr * li) / den
    bbr = fr[..., None] * b_re - fi[..., None] * b_im
    bbi = fr[..., None] * b_im + fi[..., None] * b_re

    def power(p):
        p = p[..., None, None, None].astype(F32)
        return jnp.exp(mag * p) * jnp.cos(ang * p), jnp.exp(mag * p) * jnp.sin(ang * p)

    n_dir, n_grp, n_st = a_re.shape
    gpb = LANES // S5_GROUP
    n_blk = n_grp // gpb
    s = jnp.arange(L)
    in_group = (jnp.arange(L * LANES) % LANES) // S5_GROUP
    st_group = jnp.arange(gpb * n_st) // n_st
    outs = []
    for d in range(n_dir):
        pr, pi = power(jnp.arange(L))
        pr, pi = pr[:, d], pi[:, d]
        cbr = (jnp.einsum('gcn,tgn,gnk->tgck', c_re[d], pr, bbr[d])
               - jnp.einsum('gcn,tgn,gnk->tgck', c_re[d], pi, bbi[d])
               - jnp.einsum('gcn,tgn,gnk->tgck', c_im[d], pr, bbi[d])
               - jnp.einsum('gcn,tgn,gnk->tgck', c_im[d], pi, bbr[d]))
        lag = (s[None, :] - s[:, None]) if d == 0 else (s[:, None] - s[None, :])
        kern = jnp.where((lag >= 0)[..., None, None, None], cbr[jnp.clip(lag, 0, L - 1)], 0.0)
        kern = kern.reshape(L, L, n_blk, gpb, S5_GROUP, S5_GROUP).transpose(2, 0, 3, 5, 1, 4)
        m = _spread_groups(kern.reshape(n_blk, L * LANES, L * S5_GROUP), S5_GROUP, in_group, gpb)
        qr, qi = power((L - 1 - s) if d == 0 else s)
        qr, qi = qr[:, d], qi[:, d]
        str_ = qr[..., None] * bbr[d] - qi[..., None] * bbi[d]
        sti = qr[..., None] * bbi[d] + qi[..., None] * bbr[d]

        def to_state(x):
            x = x.reshape(L, n_blk, gpb, n_st, S5_GROUP).transpose(1, 0, 2, 4, 3)
            return _spread_groups(x.reshape(n_blk, L * LANES, n_st), n_st, in_group, gpb)

        wst = jnp.concatenate([to_state(str_), to_state(sti)], axis=2)
        rr, ri = power((s + 1) if d == 0 else (L - s))
        rr, ri = rr[:, d], ri[:, d]
        wr = c_re[d][None] * rr[:, :, None, :] - c_im[d][None] * ri[:, :, None, :]
        wi = -(c_re[d][None] * ri[:, :, None, :] + c_im[d][None] * rr[:, :, None, :])

        def from_state(x):
            x = x.reshape(L, n_blk, gpb, S5_GROUP, n_st).transpose(1, 2, 4, 0, 3)
            return _spread_groups(x.reshape(n_blk, gpb * n_st, L * S5_GROUP), S5_GROUP, st_group, gpb)

        wout = jnp.concatenate([from_state(wr), from_state(wi)], axis=1)
        ler, lei = jnp.exp(mag[d] * L) * jnp.cos(ang[d] * L), jnp.exp(mag[d] * L) * jnp.sin(ang[d] * L)
        lam_l = jnp.concatenate([ler.reshape(n_blk, 1, gpb * n_st), lei.reshape(n_blk, 1, gpb * n_st)],
                                axis=2)
        outs.append((m, wst, wout, lam_l))
    return tuple(jnp.stack([o[k] for o in outs]) for k in range(4))


def _s5_kernel(u_ref, m_ref, wst_ref, wout_ref, lam_ref, d_ref, o_ref, x_ref, hp_ref, *, rows, rows_ctx):
    L = S5_CHUNK
    dr = pl.program_id(2)
    ns = lam_ref.shape[1] // 2
    u = jnp.concatenate([u_ref[pl.ds(s, rows, stride=L), :] for s in range(L)], axis=1)
    ub = u.astype(BF16)
    x_ref[...] = jnp.dot(ub, wst_ref[...], preferred_element_type=F32)
    lr = lam_ref[:, 0:ns]
    li = lam_ref[:, ns:2 * ns]

    def visit(r, carry):
        hr, hi = carry
        hp_ref[pl.ds(r, 1), 0:ns] = hr
        hp_ref[pl.ds(r, 1), ns:2 * ns] = hi
        xr = x_ref[pl.ds(r, 1), 0:ns]
        xi = x_ref[pl.ds(r, 1), ns:2 * ns]
        return lr * hr - li * hi + xr, lr * hi + li * hr + xi

    zero = (jnp.zeros((1, ns), F32), jnp.zeros((1, ns), F32))

    @pl.when(dr == 0)
    def _():
        lax.fori_loop(0, rows, visit, zero, unroll=4)

    @pl.when(dr == 1)
    def _():
        c = lax.fori_loop(0, rows_ctx, lambda t, c: visit(rows_ctx - 1 - t, c), zero, unroll=4)
        lax.fori_loop(0, rows - rows_ctx, lambda t, c: visit(rows - 1 - t, c), c, unroll=4)

    y_state = jnp.dot(hp_ref[...].astype(BF16), wout_ref[...], preferred_element_type=F32)
    n_in = L * LANES

    def within_chunk(col0, col1, reverse):
        k0, k1 = (col0, n_in) if reverse else (0, col1)
        return jnp.dot(ub[:, k0:k1], m_ref[k0:k1, col0:col1], preferred_element_type=F32)

    @pl.when(dr == 0)
    def _():
        for c0 in range(0, n_in, MXU_TILE):
            y = within_chunk(c0, c0 + MXU_TILE, False) + y_state[:, c0:c0 + MXU_TILE]
            for s in range(c0 // LANES, (c0 + MXU_TILE) // LANES):
                lanes = slice(s * LANES - c0, (s + 1) * LANES - c0)
                o_ref[pl.ds(s, rows, stride=L), :] = (y[:, lanes]
                                                      + d_ref[...] * u[:, s * LANES:(s + 1) * LANES])

    @pl.when(dr == 1)
    def _():
        for c0 in range(0, n_in, MXU_TILE):
            y = within_chunk(c0, c0 + MXU_TILE, True) + y_state[:, c0:c0 + MXU_TILE]
            for s in range(c0 // LANES, (c0 + MXU_TILE) // LANES):
                lanes = slice(s * LANES - c0, (s + 1) * LANES - c0)
                o_ref[pl.ds(s, rows, stride=L), :] += y[:, lanes]


def _s5(z, weights, layer, dvec, tc, width):
    b, ttot, _ = z.shape
    m, wst, wout, lam_l = weights
    n_dir, n_blk = m.shape[1], m.shape[2]
    rows, rows_ctx = ttot // S5_CHUNK, tc // S5_CHUNK
    wspec = lambda a: pl.BlockSpec((None, None, None) + a.shape[3:],
                                   lambda bi, j, dr: (layer, dr, j, 0, 0))
    return pl.pallas_call(
        functools.partial(_s5_kernel, rows=rows, rows_ctx=rows_ctx),
        grid=(b, n_blk, n_dir),
        in_specs=[pl.BlockSpec((None, ttot, LANES), lambda bi, j, dr: (bi, 0, j)),
                  wspec(m), wspec(wst), wspec(wout), wspec(lam_l),
                  pl.BlockSpec((1, LANES), lambda bi, j, dr: (0, j))],
        out_specs=pl.BlockSpec((None, ttot, LANES), lambda bi, j, dr: (bi, 0, j)),
        out_shape=jax.ShapeDtypeStruct((b, ttot, width), F32),
        scratch_shapes=[pltpu.VMEM((rows, lam_l.shape[4]), F32), pltpu.VMEM((rows, lam_l.shape[4]), F32)],
        compiler_params=_cparams(("parallel", "parallel", "arbitrary")),
        name="s5_bidir",
    )(z, m, wst, wout, lam_l, dvec.reshape(1, width))


def _gla_kernel(qf_ref, ff_ref, vf_ref, qb_ref, fb_ref, vb_ref, lb_ref, of_ref, ob_ref,
                stf_ref, stb_ref, *, lc):
    step = pl.program_id(1)

    @pl.when(step == 0)
    def _():
        stf_ref[...] = jnp.zeros_like(stf_ref)
        stb_ref[...] = jnp.zeros_like(stb_ref)

    _gla_block(qf_ref, ff_ref, vf_ref, lb_ref, of_ref, stf_ref, lc=lc, reverse=False)
    _gla_block(qb_ref, fb_ref, vb_ref, lb_ref, ob_ref, stb_ref, lc=lc, reverse=True)


def _gla_block(q_ref, f_ref, v_ref, lb_ref, o_ref, st_ref, *, lc, reverse):
    c = HG_CHUNK
    n_chunks = lc // c
    r_i = lax.broadcasted_iota(jnp.int32, (lc, lc), 0)
    c_i = lax.broadcasted_iota(jnp.int32, (lc, lc), 1)
    same_chunk = (r_i // c) == (c_i // c)
    keep = same_chunk & ((c_i >= r_i) if reverse else (c_i <= r_i))
    tri = jnp.where(keep, 1.0, 0.0).astype(BF16)
    lb = lb_ref[...]
    q = _silu(q_ref[...])
    f = lb + (1.0 - lb) / (1.0 + jnp.exp(-f_ref[...]))
    k = 1.0 - f
    logf = jnp.log(f)
    v = v_ref[...].astype(BF16)
    hi = logf.astype(BF16)
    lo = (logf - hi.astype(F32)).astype(BF16)
    cum = jnp.dot(tri, hi, preferred_element_type=F32) + jnp.dot(tri, lo, preferred_element_type=F32)
    totals = [cum[ci * c:ci * c + 1, :] if reverse else cum[(ci + 1) * c - 1:(ci + 1) * c, :]
              for ci in range(n_chunks)]
    w = cum.shape[1]
    total = jnp.concatenate([jnp.broadcast_to(t, (c, w)) for t in totals], axis=0)
    centre = jnp.concatenate([jnp.broadcast_to(cum[ci * c + c // 2:ci * c + c // 2 + 1, :], (c, w))
                              for ci in range(n_chunks)], axis=0)
    q_in = (q * jnp.exp(cum)).astype(BF16)
    k_out = (k * jnp.exp(total - cum)).astype(BF16)
    q_loc = (q * jnp.exp(cum - centre)).astype(BF16)
    k_loc = (k * jnp.exp(centre - cum)).astype(BF16)
    decays = [jnp.exp(t) for t in totals]
    order = range(n_chunks - 1, -1, -1) if reverse else range(n_chunks)
    for h in range(HG_HEADS):
        sl = slice(h * HG_DK, (h + 1) * HG_DK)
        att = lax.dot_general(q_loc[:, sl], k_loc[:, sl], (((1,), (1,)), ((), ())),
                              preferred_element_type=F32)
        att = jnp.where(keep, att, 0.0)
        o_loc = jnp.dot(att.astype(BF16), v[:, sl], preferred_element_type=F32)
        incs = [lax.dot_general(v[ci * c:(ci + 1) * c, sl], k_out[ci * c:(ci + 1) * c, sl],
                                (((0,), (0,)), ((), ())), preferred_element_type=F32)
                for ci in range(n_chunks)]
        st = st_ref[h]
        for ci in order:
            rows = slice(ci * c, (ci + 1) * c)
            o = o_loc[rows] + lax.dot_general(q_in[rows, sl], st.astype(BF16),
                                              (((1,), (1,)), ((), ())), preferred_element_type=F32)
            o_ref[rows, sl] = o.astype(o_ref.dtype)
            st = st * decays[ci][:, sl] + incs[ci]
        st_ref[h] = st


def _gla(z, lb, tc):
    b, ttot, _ = z.shape
    w = HG_HEADS * HG_DK
    lc = SEQ_BLOCK
    n_all, n_ctx = ttot // lc, tc // lc

    def col(cidx, reverse):
        return pl.BlockSpec((None, lc, w),
                            lambda bi, s: (bi, _seq_block(s, n_ctx, n_all, reverse), cidx))

    out = jax.ShapeDtypeStruct((b, ttot, w), BF16)
    state = pltpu.VMEM((HG_HEADS, HG_DK, HG_DK), F32)
    return pl.pallas_call(
        functools.partial(_gla_kernel, lc=lc),
        grid=(b, n_all),
        in_specs=[col(1, False), col(2, False), col(4, False),
                  col(1, True), col(3, True), col(4, True),
                  pl.BlockSpec((1, w), lambda bi, s: (0, 0))],
        out_specs=[col(0, False), col(0, True)],
        out_shape=[out, out],
        scratch_shapes=[state, state],
        compiler_params=_cparams(("parallel", "arbitrary")),
        name="hgrn2_bidir",
    )(z, z, z, z, z, z, lb.reshape(1, w))


def _final_kernel(x_ref, g_ref, o_ref):
    x = x_ref[0]
    ms = jnp.mean(x * x, axis=-1, keepdims=True)
    o_ref[...] = x * lax.rsqrt(ms + EPS) * g_ref[...]


def _final_norm(xa, g, tc):
    b, ttot, d = xa.shape
    t = ttot - tc
    tm = next(rows for rows in (1024, 512, SEQ_BLOCK) if t % rows == 0)
    return pl.pallas_call(
        _final_kernel,
        grid=(b, t // tm),
        in_specs=[pl.BlockSpec((pl.Element(1), pl.Element(tm), pl.Element(d)),
                               lambda bi, i: (bi, pl.multiple_of(tc + i * tm, SUBLANES), 0)),
                  pl.BlockSpec((1, d), lambda bi, i: (0, 0))],
        out_specs=pl.BlockSpec((None, tm, d), lambda bi, i: (bi, i, 0)),
        out_shape=jax.ShapeDtypeStruct((b, t, d), F32),
        compiler_params=_cparams(("parallel", "parallel")),
        name="final_norm",
    )(xa, g.reshape(1, d))


def _rope_tables(t, tc):
    pos = jnp.arange(t)
    row = (pos // GRID_W).astype(F32)
    col = (pos % GRID_W).astype(F32)
    inv = ROPE_BASE ** (-jnp.arange(ROPE_FREQS, dtype=F32) / ROPE_FREQS)
    ar, ac = row[:, None] * inv, col[:, None] * inv
    cos = jnp.concatenate([jnp.cos(ar), jnp.cos(ar), jnp.cos(ac), jnp.cos(ac)], axis=1)
    sin = jnp.concatenate([-jnp.sin(ar), jnp.sin(ar), -jnp.sin(ac), jnp.sin(ac)], axis=1)
    cos = jnp.concatenate([jnp.ones((tc, HEAD_DIM), F32), cos], axis=0)
    sin = jnp.concatenate([jnp.zeros((tc, HEAD_DIM), F32), sin], axis=0)
    return jnp.tile(cos, (1, LANES // HEAD_DIM)), jnp.tile(sin, (1, LANES // HEAD_DIM))


def _block_diag_dense(w):
    nblk, h, k = w.shape
    return jnp.einsum('nhk,nm->nhmk', w, jnp.eye(nblk, dtype=w.dtype)).reshape(nblk * h, nblk * k)


def kernel(x, c, ctx, c_ctx, w_mod, b_mod, norm_mix, norm_ffn, final_norm, w_in_ab, lru_conv_w, lru_conv_b, lru_wa, lru_ba, lru_wx, lru_bx, lru_lam, attn_sink, w_out_ab, ffn_w1, ffn_w3, ffn_w2, w_in_cd, s5_a_re, s5_a_im, s5_log_step, s5_b_re, s5_b_im, s5_c_re, s5_c_im, s5_d, s5_w_glu, s5_b_glu, hg_lb_raw, hg_norm, w_out_cd, moe_router, moe_w1, moe_w3, moe_w2):
    nb, t, d = x.shape
    tc = ctx.shape[1]
    depth = w_mod.shape[0]
    assert tc % SEQ_BLOCK == 0 and t % SEQ_BLOCK == 0 and t % GRID_W == 0

    xa = jnp.concatenate([ctx, x], axis=1)
    mod_rows = -(-(nb + 1) // SUBLANES) * SUBLANES
    cvec = jnp.zeros((mod_rows, d), F32).at[:nb].set(c).at[nb].set(c_ctx)
    mod = _modulation(cvec, w_mod, b_mod).reshape(depth, mod_rows, N_MOD, d)

    cos_t, sin_t = _rope_tables(t, tc)
    lb_soft = jax.nn.softmax(hg_lb_raw.astype(F32), axis=0)
    lb_table = jnp.cumsum(lb_soft, axis=0) - lb_soft[0:1]
    n_exp = moe_router.shape[2]
    s5w = jax.vmap(_s5_weights)(s5_a_re, s5_a_im, s5_log_step, s5_b_re, s5_b_im, s5_c_re, s5_c_im)
    ffn_w = [w.astype(BF16) for w in (ffn_w1, ffn_w3, ffn_w2)]
    moe_w = [w.astype(BF16) for w in (moe_w1, moe_w3, moe_w2)]

    for l in range(depth):
        j = l // 2
        mod_l = mod[l]
        if l % 2 == 0:
            z = _inproj(xa, mod_l, norm_mix[l], w_in_ab[j].astype(BF16), tc, nb)
            lru = _lru(z, lru_conv_w[j], lru_conv_b[j],
                       jnp.stack([_block_diag_dense(lru_wa[j, dr]) for dr in range(2)]).astype(BF16),
                       lru_ba[j],
                       jnp.stack([_block_diag_dense(lru_wx[j, dr]) for dr in range(2)]).astype(BF16),
                       lru_bx[j], jax.nn.softplus(-lru_lam[j]), tc)
            sink = jnp.zeros((1, LANES), F32).at[0, :ATT_HEADS].set(attn_sink[j])
            att = _attention(z, cos_t, sin_t, sink, tc)
            xa = _outproj_ab(xa, mod_l, lru[0], lru[1], z, att, w_out_ab[j].astype(BF16), tc, nb)
            xa = _ffn(xa, mod_l, norm_ffn[l], ffn_w[0], ffn_w[1], ffn_w[2], j, tc, nb)
        else:
            z = _inproj(xa, mod_l, norm_mix[l], w_in_cd[j].astype(BF16), tc, nb)
            y5 = _s5(z, s5w, j, s5_d[j], tc, s5_d.shape[1])
            o_f, o_b = _gla(z, lb_table[j], tc)
            router_t = jnp.zeros((SUBLANES, d), F32).at[:n_exp].set(moe_router[j].T)
            xa, *routed = _outproj_cd(xa, mod_l, y5, o_f, o_b, z, s5_w_glu[j].astype(BF16),
                                      s5_b_glu[j], hg_norm[j], w_out_cd[j].astype(BF16),
                                      norm_ffn[l], router_t, n_exp, tc, nb)
            xa = _moe(xa, mod_l, routed, moe_w[0], moe_w[1], moe_w[2], j, tc, nb)
    return _final_norm(xa, final_norm, tc)
```

```python
import functools
import math

import jax
import jax.numpy as jnp
from jax import lax
from jax.experimental import pallas as pl
from jax.experimental.pallas import tpu as pltpu

F32 = jnp.float32
BF16 = jnp.bfloat16

EPS = 1e-6
GRID_W = 64
LRU_C = 8.0
CONV_W = 4
ATT_HEADS = 8
ATT_KV_HEADS = 2
ATT_GROUP = ATT_HEADS // ATT_KV_HEADS
HEAD_DIM = 64
WINDOW = 128
BLOCK_Q = 128
ROPE_FREQS = HEAD_DIM // 4
ROPE_BASE = 10000.0
S5_GROUP = 16
S5_CHUNK = 8
HG_HEADS = 4
HG_DK = 128
HG_CHUNK = 64
SEQ_BLOCK = 256
N_MOD = 6
MOE_PASS_ROWS = 256
MOE_TAIL_STEP = 64
LANES = 128
SUBLANES = 8
MXU_TILE = 256
VMEM_LIMIT = 56 * 1024 * 1024
NEG = -1e30
LOG2E = math.log2(math.e)


def _cparams(sem):
    return pltpu.CompilerParams(dimension_semantics=sem, vmem_limit_bytes=VMEM_LIMIT)


def _row_tile(ttot):
    for tm in (768, 1024, 512, 256):
        if ttot % tm == 0:
            return tm
    raise ValueError(f"unsupported token count {ttot}")


def _tall_row_tile(ttot):
    for tm in range(min(ttot, 1088) // 16 * 16, 15, -16):
        if ttot % tm == 0:
            return tm
    raise ValueError(f"unsupported token count {ttot}")


def _sigmoid(x):
    return 0.5 * jnp.tanh(0.5 * x) + 0.5


def _silu(x):
    return x * _sigmoid(x)


def _gelu(x):
    return 0.5 * x * (1.0 + jnp.tanh(math.sqrt(2.0 / math.pi) * (x + 0.044715 * (x * x * x))))


def _norm_mod(x, g, mb, mc, row0, tc, shift_idx, scale_idx):
    ms = jnp.mean(x * x, axis=-1, keepdims=True)
    y = x * lax.rsqrt(ms + EPS) * g
    rows = row0 + lax.broadcasted_iota(jnp.int32, (x.shape[0], 1), 0)
    is_ctx = rows < tc
    scale = jnp.where(is_ctx, mc[scale_idx:scale_idx + 1], mb[scale_idx:scale_idx + 1])
    shift = jnp.where(is_ctx, mc[shift_idx:shift_idx + 1], mb[shift_idx:shift_idx + 1])
    return y * (1.0 + scale) + shift


def _gate_mod(mb, mc, idx, row0, tc, n):
    rows = row0 + lax.broadcasted_iota(jnp.int32, (n, 1), 0)
    return jnp.where(rows < tc, mc[idx:idx + 1], mb[idx:idx + 1])


def _mod_kernel(c_ref, w_ref, b_ref, o_ref):
    s = _silu(c_ref[...])
    o_ref[...] = jnp.dot(s, w_ref[...], preferred_element_type=F32,
                         precision=lax.Precision.HIGHEST) + b_ref[...]


def _modulation(cvec, w_mod, b_mod):
    depth, d, n = w_mod.shape
    tn = 1536 if n % 1536 == 0 else n
    rows = cvec.shape[0]
    return pl.pallas_call(
        _mod_kernel,
        grid=(depth, n // tn),
        in_specs=[pl.BlockSpec((rows, d), lambda l, j: (0, 0)),
                  pl.BlockSpec((None, d, tn), lambda l, j: (l, 0, j)),
                  pl.BlockSpec((None, 1, tn), lambda l, j: (l, 0, j))],
        out_specs=pl.BlockSpec((None, rows, tn), lambda l, j: (l, 0, j)),
        out_shape=jax.ShapeDtypeStruct((depth, rows, n), F32),
        compiler_params=_cparams(("arbitrary", "arbitrary")),
        name="modulation",
    )(cvec, w_mod, b_mod.reshape(depth, 1, n))


def _inproj_kernel(x_ref, mb_ref, mc_ref, g_ref, w_ref, o_ref, *, tm, tc):
    i = pl.program_id(1)
    h = _norm_mod(x_ref[...], g_ref[...], mb_ref[...], mc_ref[...], i * tm, tc, 0, 1)
    o_ref[...] = jnp.dot(h.astype(BF16), w_ref[...], preferred_element_type=F32)


def _inproj(xa, mod_l, g, w, tc, nb):
    b, ttot, d = xa.shape
    n = w.shape[1]
    tm = _row_tile(ttot)
    return pl.pallas_call(
        functools.partial(_inproj_kernel, tm=tm, tc=tc),
        grid=(b, ttot // tm),
        in_specs=[pl.BlockSpec((None, tm, d), lambda bi, i: (bi, i, 0)),
                  pl.BlockSpec((None, N_MOD, d), lambda bi, i: (bi, 0, 0)),
                  pl.BlockSpec((None, N_MOD, d), lambda bi, i: (nb, 0, 0)),
                  pl.BlockSpec((1, d), lambda bi, i: (0, 0)),
                  pl.BlockSpec((d, n), lambda bi, i: (0, 0))],
        out_specs=pl.BlockSpec((None, tm, n), lambda bi, i: (bi, i, 0)),
        out_shape=jax.ShapeDtypeStruct((b, ttot, n), F32),
        compiler_params=_cparams(("parallel", "parallel")),
        name="inproj",
    )(xa, mod_l, mod_l, g.reshape(1, d), w)


def _seq_block(step, n_ctx, n_all, reverse):
    if not reverse:
        return step
    return jnp.where(step < n_ctx, n_ctx - 1 - step, n_all - 1 - step + n_ctx)


def _lru_kernel(*refs, lc, n_ctx, n_all):
    cw_ref, cb_ref = refs[:2]
    n_in = 8
    ins = [refs[2 + d * n_in:2 + (d + 1) * n_in] for d in range(2)]
    outs = refs[2 + 2 * n_in:4 + 2 * n_in]
    scratch = refs[4 + 2 * n_in:]
    for d in range(2):
        _lru_block(*ins[d], cw_ref, cb_ref, outs[d], *scratch[4 * d:4 * d + 4],
                   lc=lc, n_ctx=n_ctx, n_all=n_all, reverse=bool(d))


def _lru_block(u_ref, hp_ref, hn_ref, wa_ref, ba_ref, wx_ref, bx_ref, sp_ref, cw_ref, cb_ref,
               o_ref, uc_ref, a_ref, b_ref, h_ref, *, lc, n_ctx, n_all, reverse):
    step = pl.program_id(1)
    blk = _seq_block(step, n_ctx, n_all, reverse)
    has_prev = jnp.logical_and(blk != 0, blk != n_ctx)
    has_next = jnp.logical_and(blk != n_ctx - 1, blk != n_all - 1)

    @pl.when(step == 0)
    def _():
        h_ref[...] = jnp.zeros_like(h_ref)

    w0, w1, w2, w3 = (cw_ref[j:j + 1, :] for j in range(CONV_W))
    cb = cb_ref[...]
    u = u_ref[...]
    uc_ref[...] = (cb + pltpu.roll(u, 2, 0) * w0 + pltpu.roll(u, 1, 0) * w1 + u * w2
                   + pltpu.roll(u, lc - 1, 0) * w3)
    row = lax.broadcasted_iota(jnp.int32, (SUBLANES, 1), 0)
    hp = jnp.where(has_prev, hp_ref[...], 0.0)
    hn = jnp.where(has_next, hn_ref[...], 0.0)
    u0 = u_ref[0:SUBLANES, :]
    u1 = u_ref[SUBLANES:2 * SUBLANES, :]
    uc_ref[0:SUBLANES, :] = (
        cb + jnp.where(row < 2, pltpu.roll(hp, 2, 0), pltpu.roll(u0, 2, 0)) * w0
        + jnp.where(row < 1, pltpu.roll(hp, 1, 0), pltpu.roll(u0, 1, 0)) * w1 + u0 * w2
        + jnp.where(row < SUBLANES - 1, pltpu.roll(u0, SUBLANES - 1, 0),
                    pltpu.roll(u1, SUBLANES - 1, 0)) * w3)
    ul = u_ref[lc - SUBLANES:lc, :]
    um = u_ref[lc - 2 * SUBLANES:lc - SUBLANES, :]
    uc_ref[lc - SUBLANES:lc, :] = (
        cb + jnp.where(row < 2, pltpu.roll(um, 2, 0), pltpu.roll(ul, 2, 0)) * w0
        + jnp.where(row < 1, pltpu.roll(um, 1, 0), pltpu.roll(ul, 1, 0)) * w1 + ul * w2
        + jnp.where(row < SUBLANES - 1, pltpu.roll(ul, SUBLANES - 1, 0),
                    pltpu.roll(hn, SUBLANES - 1, 0)) * w3)

    uc = uc_ref[...]
    ub = uc.astype(BF16)
    r = _sigmoid(jnp.dot(ub, wa_ref[...], preferred_element_type=F32) + ba_ref[...])
    gi = _sigmoid(jnp.dot(ub, wx_ref[...], preferred_element_type=F32) + bx_ref[...])
    log_a = -LRU_C * r * sp_ref[...]
    a = jnp.exp(log_a)
    x = 1.0 - a * a
    b = jnp.where(x > 0.0, x * lax.rsqrt(x), 0.0) * (gi * uc)
    n_grp = lc // SUBLANES
    width = a.shape[1]
    a = a.reshape(n_grp, SUBLANES, width)
    b = b.reshape(n_grp, SUBLANES, width)
    sub = lax.broadcasted_iota(jnp.int32, (1, SUBLANES, 1), 1)
    dist = 1
    while dist < SUBLANES:
        inside = (sub < SUBLANES - dist) if reverse else (sub >= dist)
        shift = (SUBLANES - dist) if reverse else dist
        a_sh = jnp.where(inside, pltpu.roll(a, shift, 1), 1.0)
        b_sh = jnp.where(inside, pltpu.roll(b, shift, 1), 0.0)
        b = a * b_sh + b
        a = a * a_sh
        dist *= 2
    a_ref[...] = a.reshape(lc, width)
    b_ref[...] = b.reshape(lc, width)

    def body(i, carry):
        t = (n_grp - 1 - i) if reverse else i
        rows = pl.ds(pl.multiple_of(t * SUBLANES, SUBLANES), SUBLANES)
        h = a_ref[rows, :] * carry + b_ref[rows, :]
        b_ref[rows, :] = h
        return h[0:1, :] if reverse else h[SUBLANES - 1:SUBLANES, :]

    h_ref[0:1, :] = lax.fori_loop(0, n_grp, body, h_ref[0:1, :], unroll=4)
    o_ref[...] = b_ref[...].astype(o_ref.dtype)


def _lru(z, conv_w, conv_b, wa_bd, ba, wx_bd, bx, sp, tc):
    b, ttot, _ = z.shape
    w = conv_w.shape[1]
    lc = SEQ_BLOCK
    n_all, n_ctx = ttot // lc, tc // lc
    hb = lc // SUBLANES
    n_h = ttot // SUBLANES
    col = 1

    def direction(d):
        def blk(s):
            return _seq_block(s, n_ctx, n_all, bool(d))

        vec = pl.BlockSpec((None, 1, w), lambda bi, s: (d, 0, 0))
        mat = pl.BlockSpec((None, w, w), lambda bi, s: (d, 0, 0))
        specs = [pl.BlockSpec((None, lc, w), lambda bi, s: (bi, blk(s), col)),
                 pl.BlockSpec((None, SUBLANES, w),
                              lambda bi, s: (bi, jnp.maximum(blk(s) * hb - 1, 0), col)),
                 pl.BlockSpec((None, SUBLANES, w),
                              lambda bi, s: (bi, jnp.minimum((blk(s) + 1) * hb, n_h - 1), col)),
                 mat, vec, mat, vec, vec]
        return specs, pl.BlockSpec((None, lc, w), lambda bi, s: (bi, blk(s), 0))

    (in_f, out_f), (in_r, out_r) = direction(0), direction(1)
    out = jax.ShapeDtypeStruct((b, ttot, w), BF16)
    per_dir = [pltpu.VMEM((lc, w), F32)] * 3 + [pltpu.VMEM((SUBLANES, w), F32)]
    vecs = lambda x: x.reshape(2, 1, w)
    args = (z, z, z, wa_bd, vecs(ba), wx_bd, vecs(bx), vecs(sp))
    return pl.pallas_call(
        functools.partial(_lru_kernel, lc=lc, n_ctx=n_ctx, n_all=n_all),
        grid=(b, n_all),
        in_specs=[pl.BlockSpec((CONV_W, w), lambda bi, s: (0, 0)),
                  pl.BlockSpec((1, w), lambda bi, s: (0, 0))] + in_f + in_r,
        out_specs=[out_f, out_r],
        out_shape=[out, out],
        scratch_shapes=per_dir + per_dir,
        compiler_params=_cparams(("parallel", "arbitrary")),
        name="rglru_bidir",
    )(conv_w, conv_b.reshape(1, w), *args, *args)


def _rope(x, cos, sin_signed):
    lane = lax.broadcasted_iota(jnp.int32, x.shape, 1)
    swapped = jnp.where(lane % 32 < 16, pltpu.roll(x, LANES - 16, 1), pltpu.roll(x, 16, 1))
    return x * cos + swapped * sin_signed


def _pair_layout(x, x_sw, head):
    lo = lax.broadcasted_iota(jnp.int32, x.shape, 1) < HEAD_DIM
    first, second = (x, x_sw) if head == 0 else (x_sw, x)
    return jnp.concatenate([jnp.where(lo, first, 0.0), jnp.where(lo, 0.0, second)],
                           axis=0).astype(BF16)


def _attend(q_block, keys, vals, bias, sink_row, o_ref):
    nk = keys.shape[0]
    keys_sw = pltpu.roll(keys, HEAD_DIM, 1)
    vals_t = vals.T
    top = lax.broadcasted_iota(jnp.int32, (LANES, BLOCK_Q), 0) < HEAD_DIM
    zero_t = jnp.zeros((HEAD_DIM, nk), F32)
    pairs_per_kv = ATT_GROUP // 2
    for kv in range(ATT_KV_HEADS):
        kab = _pair_layout(keys, keys_sw, kv)
        vt = vals_t[kv * HEAD_DIM:(kv + 1) * HEAD_DIM, :]
        vab_t = jnp.concatenate([jnp.concatenate([vt, zero_t], axis=1),
                                 jnp.concatenate([zero_t, vt], axis=1)], axis=0).astype(BF16)
        for g in range(kv * pairs_per_kv, (kv + 1) * pairs_per_kv):
            s_t = lax.dot_general(kab, q_block(g), (((1,), (1,)), ((), ())),
                                  preferred_element_type=F32)
            halves = []
            recips = []
            for half in range(2):
                sh = s_t[half * nk:(half + 1) * nk, :]
                if bias is not None:
                    sh = sh + bias
                sk = sink_row[:, 2 * g + half:2 * g + half + 1]
                m = jnp.maximum(jnp.max(sh, axis=0, keepdims=True), sk)
                p = jnp.exp2(sh - m)
                recips.append(1.0 / (jnp.sum(p, axis=0, keepdims=True) + jnp.exp2(sk - m)))
                halves.append(p.astype(BF16))
            o_t = jnp.dot(vab_t, jnp.concatenate(halves, axis=0), preferred_element_type=F32)
            o_t = o_t * jnp.where(top, recips[0], recips[1])
            o_ref[:, g * LANES:(g + 1) * LANES] = o_t.T.astype(o_ref.dtype)


def _attn_kernel(q_ref, kp_ref, k0_ref, kn_ref, vp_ref, v0_ref, vn_ref, kc_ref, vc_ref,
                 cq_ref, sq_ref, cp_ref, sp_ref, cn_ref, sn_ref, sink_ref, o_ref, *, n_ctx, n_lat):
    i = pl.program_id(1)
    qscale = HEAD_DIM ** -0.5 * LOG2E
    sink_row = sink_ref[...] * LOG2E

    @pl.when(i < n_ctx)
    def _():
        def q_block(g):
            return (q_ref[:, g * LANES:(g + 1) * LANES] * qscale).astype(BF16)
        _attend(q_block, kc_ref[...], vc_ref[...], None, sink_row, o_ref)

    @pl.when(i >= n_ctx)
    def _():
        n = i - n_ctx
        cq, sq = cq_ref[...], sq_ref[...]

        def q_block(g):
            return (_rope(q_ref[:, g * LANES:(g + 1) * LANES], cq, sq) * qscale).astype(BF16)

        kp = _rope(kp_ref[...], cp_ref[...], sp_ref[...])
        k0 = _rope(k0_ref[...], cq, sq)
        kn = _rope(kn_ref[...], cn_ref[...], sn_ref[...])
        keys = jnp.concatenate([kp, k0, kn, kc_ref[...]], axis=0)
        vals = jnp.concatenate([vp_ref[...], v0_ref[...], vn_ref[...], vc_ref[...]], axis=0)
        nk = keys.shape[0]
        qpos = n * BLOCK_Q + lax.broadcasted_iota(jnp.int32, (nk, BLOCK_Q), 1)
        key = lax.broadcasted_iota(jnp.int32, (nk, BLOCK_Q), 0)
        kpos = (n - 1) * BLOCK_Q + key
        local = (jnp.abs(qpos - kpos) <= WINDOW) & (kpos >= 0) & (kpos < n_lat * BLOCK_Q)
        bias = jnp.where(local | (key >= 3 * BLOCK_Q), 0.0, NEG)
        _attend(q_block, keys, vals, bias, sink_row, o_ref)


def _attention(z, cos_t, sin_t, sink, tc):
    b, ttot, _ = z.shape
    n_all, n_ctx = ttot // BLOCK_Q, tc // BLOCK_Q
    n_lat = n_all - n_ctx
    kcol = (2 * 512 + ATT_HEADS * HEAD_DIM) // LANES
    vcol = kcol + 1
    qcol = 2 * 512 // (ATT_HEADS * HEAD_DIM)

    def prev(i):
        return jnp.maximum(i - 1, 0)

    def nxt(i):
        return jnp.minimum(i + 1, n_all - 1)

    def rows(col, f):
        return pl.BlockSpec((None, BLOCK_Q, LANES), lambda bi, i: (bi, f(i), col))

    def tab(f):
        return pl.BlockSpec((BLOCK_Q, LANES), lambda bi, i: (f(i), 0))

    same = lambda i: i
    return pl.pallas_call(
        functools.partial(_attn_kernel, n_ctx=n_ctx, n_lat=n_lat),
        grid=(b, n_all),
        in_specs=[pl.BlockSpec((None, BLOCK_Q, ATT_HEADS * HEAD_DIM), lambda bi, i: (bi, i, qcol)),
                  rows(kcol, prev), rows(kcol, same), rows(kcol, nxt),
                  rows(vcol, prev), rows(vcol, same), rows(vcol, nxt),
                  pl.BlockSpec((None, tc, LANES), lambda bi, i: (bi, 0, kcol)),
                  pl.BlockSpec((None, tc, LANES), lambda bi, i: (bi, 0, vcol)),
                  tab(same), tab(same), tab(prev), tab(prev), tab(nxt), tab(nxt),
                  pl.BlockSpec((1, LANES), lambda bi, i: (0, 0))],
        out_specs=pl.BlockSpec((None, BLOCK_Q, ATT_HEADS * HEAD_DIM), lambda bi, i: (bi, i, 0)),
        out_shape=jax.ShapeDtypeStruct((b, ttot, ATT_HEADS * HEAD_DIM), BF16),
        compiler_params=_cparams(("parallel", "parallel")),
        name="window_gqa",
    )(z, z, z, z, z, z, z, z, z, cos_t, sin_t, cos_t, sin_t, cos_t, sin_t, sink)


def _outproj_ab_kernel(x_ref, mb_ref, mc_ref, lf_ref, lb_ref, g_ref, at_ref, w_ref, o_ref, *, tm, tc):
    i = pl.program_id(1)
    lru = (lf_ref[...].astype(F32) + lb_ref[...].astype(F32)) * _gelu(g_ref[...])
    mix = jnp.concatenate([lru.astype(BF16), at_ref[...]], axis=1)
    dx = jnp.dot(mix, w_ref[...], preferred_element_type=F32)
    gate = _gate_mod(mb_ref[...], mc_ref[...], 2, i * tm, tc, tm)
    o_ref[...] = x_ref[...] + gate * dx


def _outproj_ab(xa, mod_l, lru_f, lru_b, z, att, w_out, tc, nb):
    b, ttot, d = xa.shape
    w = lru_f.shape[2]
    tm = _row_tile(ttot)
    tile = lambda width, col: pl.BlockSpec((None, tm, width), lambda bi, i: (bi, i, col))
    return pl.pallas_call(
        functools.partial(_outproj_ab_kernel, tm=tm, tc=tc),
        grid=(b, ttot // tm),
        in_specs=[tile(d, 0),
                  pl.BlockSpec((None, N_MOD, d), lambda bi, i: (bi, 0, 0)),
                  pl.BlockSpec((None, N_MOD, d), lambda bi, i: (nb, 0, 0)),
                  tile(w, 0), tile(w, 0), tile(w, 0), tile(att.shape[2], 0),
                  pl.BlockSpec(w_out.shape, lambda bi, i: (0, 0))],
        out_specs=tile(d, 0),
        out_shape=jax.ShapeDtypeStruct(xa.shape, F32),
        compiler_params=_cparams(("parallel", "parallel")),
        name="outproj_ab",
    )(xa, mod_l, mod_l, lru_f, lru_b, z, att, w_out)


def _outproj_cd_kernel(x_ref, mb_ref, mc_ref, y_ref, of_ref, ob_ref, g_ref, wg_ref, bg_ref, gn_ref,
                       w_ref, gf_ref, r_ref, o_ref, h_ref, wt_ref, cnt_ref, *, tm, tc, n_exp):
    i = pl.program_id(1)
    y = _gelu(y_ref[...])
    s5 = y * _sigmoid(jnp.dot(y.astype(BF16), wg_ref[...], preferred_element_type=F32) + bg_ref[...])
    o = of_ref[...].astype(F32) + ob_ref[...].astype(F32)
    parts = []
    for h in range(HG_HEADS):
        oh = o[:, h * HG_DK:(h + 1) * HG_DK]
        ms = jnp.mean(oh * oh, axis=-1, keepdims=True)
        parts.append(oh * lax.rsqrt(ms + EPS) * gn_ref[...])
    hg = jnp.concatenate(parts, axis=1) * _silu(g_ref[...])
    mix = jnp.concatenate([s5.astype(BF16), hg.astype(BF16)], axis=1)
    dx = jnp.dot(mix, w_ref[...], preferred_element_type=F32)
    gate = _gate_mod(mb_ref[...], mc_ref[...], 2, i * tm, tc, tm)
    x_new = x_ref[...] + gate * dx
    o_ref[...] = x_new
    h = _norm_mod(x_new, gf_ref[...], mb_ref[...], mc_ref[...], i * tm, tc, 3, 4)
    _route_tokens(h, r_ref, h_ref, wt_ref, cnt_ref, n_exp)


def _outproj_cd(xa, mod_l, y5, o_f, o_b, z, w_glu, b_glu, hg_norm, w_out, g_ffn, router_t, n_exp,
                tc, nb):
    b, ttot, d = xa.shape
    w = y5.shape[2]
    tm = _row_tile(ttot)
    n_t = ttot // tm
    gcol = z.shape[2] // w - 1
    tile = lambda width, col: pl.BlockSpec((None, tm, width), lambda bi, i: (bi, i, col))
    return pl.pallas_call(
        functools.partial(_outproj_cd_kernel, tm=tm, tc=tc, n_exp=n_exp),
        grid=(b, n_t),
        in_specs=[tile(d, 0),
                  pl.BlockSpec((None, N_MOD, d), lambda bi, i: (bi, 0, 0)),
                  pl.BlockSpec((None, N_MOD, d), lambda bi, i: (nb, 0, 0)),
                  tile(w, 0), tile(w, 0), tile(w, 0), tile(w, gcol),
                  pl.BlockSpec(w_glu.shape, lambda bi, i: (0, 0)),
                  pl.BlockSpec((1, w), lambda bi, i: (0, 0)),
                  pl.BlockSpec((1, HG_DK), lambda bi, i: (0, 0)),
                  pl.BlockSpec(w_out.shape, lambda bi, i: (0, 0)),
                  pl.BlockSpec((1, d), lambda bi, i: (0, 0)),
                  pl.BlockSpec((SUBLANES, d), lambda bi, i: (0, 0))],
        out_specs=[tile(d, 0),
                   pl.BlockSpec((tm, d), lambda bi, i: (bi * n_t + i, 0)),
                   pl.BlockSpec((SUBLANES, tm), lambda bi, i: (0, bi * n_t + i)),
                   pl.BlockSpec((None, SUBLANES, LANES), lambda bi, i: (bi * n_t + i, 0, 0))],
        out_shape=[jax.ShapeDtypeStruct(xa.shape, F32),
                   jax.ShapeDtypeStruct((b * ttot, d), BF16),
                   jax.ShapeDtypeStruct((SUBLANES, b * ttot), F32),
                   jax.ShapeDtypeStruct((b * n_t, SUBLANES, LANES), jnp.int32)],
        compiler_params=_cparams(("parallel", "parallel")),
        name="outproj_cd_route",
    )(xa, mod_l, mod_l, y5, o_f, o_b, z, w_glu, b_glu.reshape(1, w), hg_norm.reshape(1, HG_DK), w_out,
      g_ffn.reshape(1, d), router_t)


def _ffn_kernel(x_ref, mb_ref, mc_ref, g_ref, w1_ref, w3_ref, w2_ref, o_ref, h_ref, acc_ref, *, tm, tc):
    i = pl.program_id(1)
    k = pl.program_id(2)

    @pl.when(k == 0)
    def _():
        h = _norm_mod(x_ref[...], g_ref[...], mb_ref[...], mc_ref[...], i * tm, tc, 3, 4)
        h_ref[...] = h.astype(BF16)
        acc_ref[...] = jnp.zeros_like(acc_ref)

    h = h_ref[...]
    tf = w1_ref.shape[1]
    half = (tf // LANES + 1) // 2 * LANES
    y = None
    for f0, f1 in ([(0, half), (half, tf)] if 0 < half < tf else [(0, tf)]):
        a = jnp.dot(h, w1_ref[:, f0:f1], preferred_element_type=F32)
        c = jnp.dot(h, w3_ref[:, f0:f1], preferred_element_type=F32)
        part = jnp.dot((_silu(a) * c).astype(BF16), w2_ref[f0:f1, :], preferred_element_type=F32)
        y = part if y is None else y + part
    acc_ref[...] += y

    @pl.when(k == pl.num_programs(2) - 1)
    def _():
        gate = _gate_mod(mb_ref[...], mc_ref[...], 5, i * tm, tc, tm)
        o_ref[...] = x_ref[...] + gate * acc_ref[...]


def _ffn(xa, mod_l, g, w1, w3, w2, layer, tc, nb):
    b, ttot, d = xa.shape
    f = w1.shape[2]
    tm = _tall_row_tile(ttot)
    tf = f // 2 if (f // 2) % LANES == 0 else f
    return pl.pallas_call(
        functools.partial(_ffn_kernel, tm=tm, tc=tc),
        grid=(b, ttot // tm, f // tf),
        in_specs=[pl.BlockSpec((None, tm, d), lambda bi, i, k: (bi, i, 0)),
                  pl.BlockSpec((None, N_MOD, d), lambda bi, i, k: (bi, 0, 0)),
                  pl.BlockSpec((None, N_MOD, d), lambda bi, i, k: (nb, 0, 0)),
                  pl.BlockSpec((1, d), lambda bi, i, k: (0, 0)),
                  pl.BlockSpec((None, d, tf), lambda bi, i, k: (layer, 0, k)),
                  pl.BlockSpec((None, d, tf), lambda bi, i, k: (layer, 0, k)),
                  pl.BlockSpec((None, tf, d), lambda bi, i, k: (layer, k, 0))],
        out_specs=pl.BlockSpec((None, tm, d), lambda bi, i, k: (bi, i, 0)),
        out_shape=jax.ShapeDtypeStruct(xa.shape, F32),
        scratch_shapes=[pltpu.VMEM((tm, d), BF16), pltpu.VMEM((tm, d), F32)],
        compiler_params=_cparams(("parallel", "parallel", "arbitrary")),
        name="ffn_swiglu",
    )(xa, mod_l, mod_l, g.reshape(1, d), w1, w3, w2)


def _route_tokens(h, r_ref, h_ref, wt_ref, cnt_ref, n_exp):
    nt = (((1,), (1,)), ((), ()))
    h_hi = h.astype(BF16)
    h_lo = (h - h_hi.astype(F32)).astype(BF16)
    r = r_ref[...]
    r_hi = r.astype(BF16)
    r_lo = (r - r_hi.astype(F32)).astype(BF16)
    logits = (lax.dot_general(r_hi, h_hi, nt, preferred_element_type=F32)
              + lax.dot_general(r_lo, h_hi, nt, preferred_element_type=F32)
              + lax.dot_general(r_hi, h_lo, nt, preferred_element_type=F32))
    sub = lax.broadcasted_iota(jnp.int32, logits.shape, 0)
    h_ref[...] = h_hi
    logits = jnp.where(sub < n_exp, logits, NEG)
    m1 = jnp.max(logits, axis=0, keepdims=True)
    i1 = jnp.min(jnp.where(logits == m1, sub, SUBLANES), axis=0, keepdims=True)
    rest = jnp.where(sub == i1, NEG, logits)
    m2 = jnp.max(rest, axis=0, keepdims=True)
    i2 = jnp.min(jnp.where(rest == m2, sub, SUBLANES), axis=0, keepdims=True)
    e2 = jnp.exp(m2 - m1)
    g1 = 1.0 / (1.0 + e2)
    wt = jnp.where(sub == i1, g1, 0.0) + jnp.where(sub == i2, e2 * g1, 0.0)
    wt_ref[...] = wt
    cnt = jnp.sum(jnp.where(wt > 0.0, 1.0, 0.0), axis=1, keepdims=True)
    cnt_ref[...] = jnp.broadcast_to(cnt, cnt_ref.shape).astype(jnp.int32)


def _moe_kernel(cnt_ref, x_ref, *refs, tm, tc, n_exp, n_t, group):
    mb_refs = refs[:group]
    mc_ref, h_ref, wt_ref, w1_ref, w3_ref, w2_ref, o_ref, rank_ref = refs[group:]
    p = pl.program_id(0)
    e = pl.program_id(1)
    f = w1_ref.shape[1]
    half = (f // LANES + 1) // 2 * LANES
    f_split = [(0, half), (half, f)] if 0 < half < f else [(0, f)]

    @pl.when(e == 0)
    def _():
        o_ref[...] = jnp.zeros_like(o_ref)
        r_i = lax.broadcasted_iota(jnp.int32, (tm, tm), 0)
        c_i = lax.broadcasted_iota(jnp.int32, (tm, tm), 1)
        before = jnp.where(r_i < c_i, 1.0, 0.0).astype(BF16)
        for j in range(group):
            sel = wt_ref[:, j * tm:(j + 1) * tm] > 0.0
            rank = jnp.dot(jnp.where(sel, 1.0, 0.0).astype(BF16), before, preferred_element_type=F32)
            rank_ref[j] = jnp.where(sel, rank, -1.0)

    counts = [cnt_ref[(p * group + j) * n_exp + e] for j in range(group)]

    offsets = [jnp.int32(0)]
    for j in range(group - 1):
        offsets.append(offsets[-1] + counts[j])
    total = offsets[-1] + counts[-1]

    def expert_pass(m_rows, base):
        slot = (lax.broadcasted_iota(jnp.int32, (m_rows, tm), 0) + base).astype(F32)
        onehots = []
        xe = None
        gate = None
        for j in range(group):
            r_row = rank_ref[j, pl.ds(e, 1), :]
            r_row = jnp.where(r_row >= 0.0, r_row + offsets[j].astype(F32), -1.0)
            w_row = wt_ref[pl.ds(e, 1), j * tm:(j + 1) * tm]
            hit = r_row == slot
            onehot = jnp.where(hit, 1.0, 0.0).astype(BF16)
            onehots.append(onehot)
            xj = jnp.dot(onehot, h_ref[j * tm:(j + 1) * tm, :], preferred_element_type=F32)
            gj = jnp.sum(jnp.where(hit, w_row, 0.0), axis=1, keepdims=True)
            xe = xj if xe is None else xe + xj
            gate = gj if gate is None else gate + gj
        xe = xe.astype(BF16)
        y = None
        for f0, f1 in f_split:
            a = jnp.dot(xe, w1_ref[:, f0:f1], preferred_element_type=F32)
            c = jnp.dot(xe, w3_ref[:, f0:f1], preferred_element_type=F32)
            part = jnp.dot((_silu(a) * c * gate).astype(BF16), w2_ref[f0:f1, :],
                           preferred_element_type=F32)
            y = part if y is None else y + part
        y = y.astype(BF16)
        for j in range(group):
            o_ref[j * tm:(j + 1) * tm, :] += lax.dot_general(
                onehots[j], y, (((0,), (0,)), ((), ())), preferred_element_type=F32)

    n_full = total // MOE_PASS_ROWS

    def full_pass(it, carry):
        expert_pass(MOE_PASS_ROWS, it * MOE_PASS_ROWS)
        return carry

    lax.fori_loop(0, n_full, full_pass, 0)
    rest = total - n_full * MOE_PASS_ROWS
    for tail in range(MOE_TAIL_STEP, MOE_PASS_ROWS + 1, MOE_TAIL_STEP):
        @pl.when(jnp.logical_and(rest > tail - MOE_TAIL_STEP, rest <= tail))
        def _(tail=tail):
            expert_pass(tail, n_full * MOE_PASS_ROWS)

    @pl.when(e == n_exp - 1)
    def _():
        for j in range(group):
            i = (p * group + j) % n_t
            gate5 = _gate_mod(mb_refs[j][...], mc_ref[...], 5, i * tm, tc, tm)
            rows = slice(j * tm, (j + 1) * tm)
            o_ref[rows, :] = x_ref[rows, :] + gate5 * o_ref[rows, :]


def _moe(xa, mod_l, routed, w1, w3, w2, layer, tc, nb):
    b, ttot, d = xa.shape
    _, n_exp, _, f = w1.shape
    tm = _row_tile(ttot)
    n_t = ttot // tm
    group = 2 if (b * n_t) % 2 == 0 else 1
    h, wt, cnt = routed
    cnt = cnt[:, :n_exp, 0].reshape(-1)
    rows = group * tm
    mods = [pl.BlockSpec((None, N_MOD, d), functools.partial(
        lambda p, e, c, j: ((p * group + j) // n_t, 0, 0), j=j)) for j in range(group)]
    grid_spec = pltpu.PrefetchScalarGridSpec(
        num_scalar_prefetch=1,
        grid=(b * n_t // group, n_exp),
        in_specs=[pl.BlockSpec((rows, d), lambda p, e, c: (p, 0), pipeline_mode=pl.Buffered(1))]
        + mods
        + [pl.BlockSpec((None, N_MOD, d), lambda p, e, c: (nb, 0, 0)),
           pl.BlockSpec((rows, d), lambda p, e, c: (p, 0)),
           pl.BlockSpec((SUBLANES, rows), lambda p, e, c: (0, p)),
           pl.BlockSpec((None, None, d, f), lambda p, e, c: (layer, e, 0, 0)),
           pl.BlockSpec((None, None, d, f), lambda p, e, c: (layer, e, 0, 0)),
           pl.BlockSpec((None, None, f, d), lambda p, e, c: (layer, e, 0, 0))],
        out_specs=pl.BlockSpec((rows, d), lambda p, e, c: (p, 0)),
        scratch_shapes=[pltpu.VMEM((group, SUBLANES, tm), F32)])
    out = pl.pallas_call(
        functools.partial(_moe_kernel, tm=tm, tc=tc, n_exp=n_exp, n_t=n_t, group=group),
        grid_spec=grid_spec,
        out_shape=jax.ShapeDtypeStruct((b * ttot, d), F32),
        compiler_params=_cparams(("parallel", "arbitrary")),
        name="moe_experts",
    )(cnt, xa.reshape(b * ttot, d), *([mod_l] * group), mod_l, h, wt, w1, w3, w2)
    return out.reshape(b, ttot, d)


def _spread_groups(x, inner, row_group, n_grp):
    cols_in = x.shape[2]
    cols_out = cols_in * n_grp
    q = jnp.arange(cols_out)
    src = (q // (n_grp * inner)) * inner + q % inner
    tile = (jnp.arange(cols_in)[:, None] == src[None, :]).astype(BF16)
    col_group = (q // inner) % n_grp
    y = jnp.dot(x.astype(BF16), tile)
    return jnp.where(row_group[:, None] == col_group[None, :], y, jnp.zeros((), BF16))


def _s5_weights(a_re, a_im, log_step, b_re, b_im, c_re, c_im):
    L = S5_CHUNK
    lr = jnp.minimum(a_re, -1e-4)
    li = a_im
    dt = jnp.exp(log_step)[..., None]
    mag, ang = lr * dt, li * dt
    lbr, lbi = jnp.exp(mag) * jnp.cos(ang), jnp.exp(mag) * jnp.sin(ang)
    zr, zi = lbr - 1.0, lbi
    den = lr * lr + li * li
    fr = (zr * lr + zi * li) / den
    fi = (zi * lr - zr * li) / den
    bbr = fr[..., None] * b_re - fi[..., None] * b_im
    bbi = fr[..., None] * b_im + fi[..., None] * b_re

    def power(p):
        p = p[..., None, None, None].astype(F32)
        return jnp.exp(mag * p) * jnp.cos(ang * p), jnp.exp(mag * p) * jnp.sin(ang * p)

    n_dir, n_grp, n_st = a_re.shape
    gpb = LANES // S5_GROUP
    n_blk = n_grp // gpb
    s = jnp.arange(L)
    in_group = (jnp.arange(L * LANES) % LANES) // S5_GROUP
    st_group = jnp.arange(gpb * n_st) // n_st
    outs = []
    for d in range(n_dir):
        pr, pi = power(jnp.arange(L))
        pr, pi = pr[:, d], pi[:, d]
        cbr = (jnp.einsum('gcn,tgn,gnk->tgck', c_re[d], pr, bbr[d])
               - jnp.einsum('gcn,tgn,gnk->tgck', c_re[d], pi, bbi[d])
               - jnp.einsum('gcn,tgn,gnk->tgck', c_im[d], pr, bbi[d])
               - jnp.einsum('gcn,tgn,gnk->tgck', c_im[d], pi, bbr[d]))
        lag = (s[None, :] - s[:, None]) if d == 0 else (s[:, None] - s[None, :])
        kern = jnp.where((lag >= 0)[..., None, None, None], cbr[jnp.clip(lag, 0, L - 1)], 0.0)
        kern = kern.reshape(L, L, n_blk, gpb, S5_GROUP, S5_GROUP).transpose(2, 0, 3, 5, 1, 4)
        m = _spread_groups(kern.reshape(n_blk, L * LANES, L * S5_GROUP), S5_GROUP, in_group, gpb)
        qr, qi = power((L - 1 - s) if d == 0 else s)
        qr, qi = qr[:, d], qi[:, d]
        str_ = qr[..., None] * bbr[d] - qi[..., None] * bbi[d]
        sti = qr[..., None] * bbi[d] + qi[..., None] * bbr[d]

        def to_state(x):
            x = x.reshape(L, n_blk, gpb, n_st, S5_GROUP).transpose(1, 0, 2, 4, 3)
            return _spread_groups(x.reshape(n_blk, L * LANES, n_st), n_st, in_group, gpb)

        wst = jnp.concatenate([to_state(str_), to_state(sti)], axis=2)
        rr, ri = power((s + 1) if d == 0 else (L - s))
        rr, ri = rr[:, d], ri[:, d]
        wr = c_re[d][None] * rr[:, :, None, :] - c_im[d][None] * ri[:, :, None, :]
        wi = -(c_re[d][None] * ri[:, :, None, :] + c_im[d][None] * rr[:, :, None, :])

        def from_state(x):
            x = x.reshape(L, n_blk, gpb, S5_GROUP, n_st).transpose(1, 2, 4, 0, 3)
            return _spread_groups(x.reshape(n_blk, gpb * n_st, L * S5_GROUP), S5_GROUP, st_group, gpb)

        wout = jnp.concatenate([from_state(wr), from_state(wi)], axis=1)
        ler, lei = jnp.exp(mag[d] * L) * jnp.cos(ang[d] * L), jnp.exp(mag[d] * L) * jnp.sin(ang[d] * L)
        lam_l = jnp.concatenate([ler.reshape(n_blk, 1, gpb * n_st), lei.reshape(n_blk, 1, gpb * n_st)],
                                axis=2)
        outs.append((m, wst, wout, lam_l))
    return tuple(jnp.stack([o[k] for o in outs]) for k in range(4))


def _s5_kernel(u_ref, m_ref, wst_ref, wout_ref, lam_ref, d_ref, o_ref, x_ref, hp_ref, *, rows, rows_ctx):
    L = S5_CHUNK
    dr = pl.program_id(2)
    ns = lam_ref.shape[1] // 2
    u = jnp.concatenate([u_ref[pl.ds(s, rows, stride=L), :] for s in range(L)], axis=1)
    ub = u.astype(BF16)
    x_ref[...] = jnp.dot(ub, wst_ref[...], preferred_element_type=F32)
    lr = lam_ref[:, 0:ns]
    li = lam_ref[:, ns:2 * ns]

    def visit(r, carry):
        hr, hi = carry
        hp_ref[pl.ds(r, 1), 0:ns] = hr
        hp_ref[pl.ds(r, 1), ns:2 * ns] = hi
        xr = x_ref[pl.ds(r, 1), 0:ns]
        xi = x_ref[pl.ds(r, 1), ns:2 * ns]
        return lr * hr - li * hi + xr, lr * hi + li * hr + xi

    zero = (jnp.zeros((1, ns), F32), jnp.zeros((1, ns), F32))

    @pl.when(dr == 0)
    def _():
        lax.fori_loop(0, rows, visit, zero, unroll=4)

    @pl.when(dr == 1)
    def _():
        c = lax.fori_loop(0, rows_ctx, lambda t, c: visit(rows_ctx - 1 - t, c), zero, unroll=4)
        lax.fori_loop(0, rows - rows_ctx, lambda t, c: visit(rows - 1 - t, c), c, unroll=4)

    y_state = jnp.dot(hp_ref[...].astype(BF16), wout_ref[...], preferred_element_type=F32)
    n_in = L * LANES

    def within_chunk(col0, col1, reverse):
        k0, k1 = (col0, n_in) if reverse else (0, col1)
        return jnp.dot(ub[:, k0:k1], m_ref[k0:k1, col0:col1], preferred_element_type=F32)

    @pl.when(dr == 0)
    def _():
        for c0 in range(0, n_in, MXU_TILE):
            y = within_chunk(c0, c0 + MXU_TILE, False) + y_state[:, c0:c0 + MXU_TILE]
            for s in range(c0 // LANES, (c0 + MXU_TILE) // LANES):
                lanes = slice(s * LANES - c0, (s + 1) * LANES - c0)
                o_ref[pl.ds(s, rows, stride=L), :] = (y[:, lanes]
                                                      + d_ref[...] * u[:, s * LANES:(s + 1) * LANES])

    @pl.when(dr == 1)
    def _():
        for c0 in range(0, n_in, MXU_TILE):
            y = within_chunk(c0, c0 + MXU_TILE, True) + y_state[:, c0:c0 + MXU_TILE]
            for s in range(c0 // LANES, (c0 + MXU_TILE) // LANES):
                lanes = slice(s * LANES - c0, (s + 1) * LANES - c0)
                o_ref[pl.ds(s, rows, stride=L), :] += y[:, lanes]


def _s5(z, weights, layer, dvec, tc, width):
    b, ttot, _ = z.shape
    m, wst, wout, lam_l = weights
    n_dir, n_blk = m.shape[1], m.shape[2]
    rows, rows_ctx = ttot // S5_CHUNK, tc // S5_CHUNK
    wspec = lambda a: pl.BlockSpec((None, None, None) + a.shape[3:],
                                   lambda bi, j, dr: (layer, dr, j, 0, 0))
    return pl.pallas_call(
        functools.partial(_s5_kernel, rows=rows, rows_ctx=rows_ctx),
        grid=(b, n_blk, n_dir),
        in_specs=[pl.BlockSpec((None, ttot, LANES), lambda bi, j, dr: (bi, 0, j)),
                  wspec(m), wspec(wst), wspec(wout), wspec(lam_l),
                  pl.BlockSpec((1, LANES), lambda bi, j, dr: (0, j))],
        out_specs=pl.BlockSpec((None, ttot, LANES), lambda bi, j, dr: (bi, 0, j)),
        out_shape=jax.ShapeDtypeStruct((b, ttot, width), F32),
        scratch_shapes=[pltpu.VMEM((rows, lam_l.shape[4]), F32), pltpu.VMEM((rows, lam_l.shape[4]), F32)],
        compiler_params=_cparams(("parallel", "parallel", "arbitrary")),
        name="s5_bidir",
    )(z, m, wst, wout, lam_l, dvec.reshape(1, width))


def _gla_kernel(qf_ref, ff_ref, vf_ref, qb_ref, fb_ref, vb_ref, lb_ref, of_ref, ob_ref,
                stf_ref, stb_ref, *, lc):
    step = pl.program_id(1)

    @pl.when(step == 0)
    def _():
        stf_ref[...] = jnp.zeros_like(stf_ref)
        stb_ref[...] = jnp.zeros_like(stb_ref)

    _gla_block(qf_ref, ff_ref, vf_ref, lb_ref, of_ref, stf_ref, lc=lc, reverse=False)
    _gla_block(qb_ref, fb_ref, vb_ref, lb_ref, ob_ref, stb_ref, lc=lc, reverse=True)


def _gla_block(q_ref, f_ref, v_ref, lb_ref, o_ref, st_ref, *, lc, reverse):
    c = HG_CHUNK
    n_chunks = lc // c
    r_i = lax.broadcasted_iota(jnp.int32, (lc, lc), 0)
    c_i = lax.broadcasted_iota(jnp.int32, (lc, lc), 1)
    same_chunk = (r_i // c) == (c_i // c)
    keep = same_chunk & ((c_i >= r_i) if reverse else (c_i <= r_i))
    tri = jnp.where(keep, 1.0, 0.0).astype(BF16)
    lb = lb_ref[...]
    q = _silu(q_ref[...])
    f = lb + (1.0 - lb) / (1.0 + jnp.exp(-f_ref[...]))
    k = 1.0 - f
    logf = jnp.log(f)
    v = v_ref[...].astype(BF16)
    hi = logf.astype(BF16)
    lo = (logf - hi.astype(F32)).astype(BF16)
    cum = jnp.dot(tri, hi, preferred_element_type=F32) + jnp.dot(tri, lo, preferred_element_type=F32)
    totals = [cum[ci * c:ci * c + 1, :] if reverse else cum[(ci + 1) * c - 1:(ci + 1) * c, :]
              for ci in range(n_chunks)]
    w = cum.shape[1]
    total = jnp.concatenate([jnp.broadcast_to(t, (c, w)) for t in totals], axis=0)
    centre = jnp.concatenate([jnp.broadcast_to(cum[ci * c + c // 2:ci * c + c // 2 + 1, :], (c, w))
                              for ci in range(n_chunks)], axis=0)
    q_in = (q * jnp.exp(cum)).astype(BF16)
    k_out = (k * jnp.exp(total - cum)).astype(BF16)
    q_loc = (q * jnp.exp(cum - centre)).astype(BF16)
    k_loc = (k * jnp.exp(centre - cum)).astype(BF16)
    decays = [jnp.exp(t) for t in totals]
    order = range(n_chunks - 1, -1, -1) if reverse else range(n_chunks)
    for h in range(HG_HEADS):
        sl = slice(h * HG_DK, (h + 1) * HG_DK)
        att = lax.dot_general(q_loc[:, sl], k_loc[:, sl], (((1,), (1,)), ((), ())),
                              preferred_element_type=F32)
        att = jnp.where(keep, att, 0.0)
        o_loc = jnp.dot(att.astype(BF16), v[:, sl], preferred_element_type=F32)
        incs = [lax.dot_general(v[ci * c:(ci + 1) * c, sl], k_out[ci * c:(ci + 1) * c, sl],
                                (((0,), (0,)), ((), ())), preferred_element_type=F32)
                for ci in range(n_chunks)]
        st = st_ref[h]
        for ci in order:
            rows = slice(ci * c, (ci + 1) * c)
            o = o_loc[rows] + lax.dot_general(q_in[rows, sl], st.astype(BF16),
                                              (((1,), (1,)), ((), ())), preferred_element_type=F32)
            o_ref[rows, sl] = o.astype(o_ref.dtype)
            st = st * decays[ci][:, sl] + incs[ci]
        st_ref[h] = st


def _gla(z, lb, tc):
    b, ttot, _ = z.shape
    w = HG_HEADS * HG_DK
    lc = SEQ_BLOCK
    n_all, n_ctx = ttot // lc, tc // lc

    def col(cidx, reverse):
        return pl.BlockSpec((None, lc, w),
                            lambda bi, s: (bi, _seq_block(s, n_ctx, n_all, reverse), cidx))

    out = jax.ShapeDtypeStruct((b, ttot, w), BF16)
    state = pltpu.VMEM((HG_HEADS, HG_DK, HG_DK), F32)
    return pl.pallas_call(
        functools.partial(_gla_kernel, lc=lc),
        grid=(b, n_all),
        in_specs=[col(1, False), col(2, False), col(4, False),
                  col(1, True), col(3, True), col(4, True),
                  pl.BlockSpec((1, w), lambda bi, s: (0, 0))],
        out_specs=[col(0, False), col(0, True)],
        out_shape=[out, out],
        scratch_shapes=[state, state],
        compiler_params=_cparams(("parallel", "arbitrary")),
        name="hgrn2_bidir",
    )(z, z, z, z, z, z, lb.reshape(1, w))


def _final_kernel(x_ref, g_ref, o_ref):
    x = x_ref[0]
    ms = jnp.mean(x * x, axis=-1, keepdims=True)
    o_ref[...] = x * lax.rsqrt(ms + EPS) * g_ref[...]


def _final_norm(xa, g, tc):
    b, ttot, d = xa.shape
    t = ttot - tc
    tm = next(rows for rows in (1024, 512, SEQ_BLOCK) if t % rows == 0)
    return pl.pallas_call(
        _final_kernel,
        grid=(b, t // tm),
        in_specs=[pl.BlockSpec((pl.Element(1), pl.Element(tm), pl.Element(d)),
                               lambda bi, i: (bi, pl.multiple_of(tc + i * tm, SUBLANES), 0)),
                  pl.BlockSpec((1, d), lambda bi, i: (0, 0))],
        out_specs=pl.BlockSpec((None, tm, d), lambda bi, i: (bi, i, 0)),
        out_shape=jax.ShapeDtypeStruct((b, t, d), F32),
        compiler_params=_cparams(("parallel", "parallel")),
        name="final_norm",
    )(xa, g.reshape(1, d))


def _rope_tables(t, tc):
    pos = jnp.arange(t)
    row = (pos // GRID_W).astype(F32)
    col = (pos % GRID_W).astype(F32)
    inv = ROPE_BASE ** (-jnp.arange(ROPE_FREQS, dtype=F32) / ROPE_FREQS)
    ar, ac = row[:, None] * inv, col[:, None] * inv
    cos = jnp.concatenate([jnp.cos(ar), jnp.cos(ar), jnp.cos(ac), jnp.cos(ac)], axis=1)
    sin = jnp.concatenate([-jnp.sin(ar), jnp.sin(ar), -jnp.sin(ac), jnp.sin(ac)], axis=1)
    cos = jnp.concatenate([jnp.ones((tc, HEAD_DIM), F32), cos], axis=0)
    sin = jnp.concatenate([jnp.zeros((tc, HEAD_DIM), F32), sin], axis=0)
    return jnp.tile(cos, (1, LANES // HEAD_DIM)), jnp.tile(sin, (1, LANES // HEAD_DIM))


def _block_diag_dense(w):
    nblk, h, k = w.shape
    return jnp.einsum('nhk,nm->nhmk', w, jnp.eye(nblk, dtype=w.dtype)).reshape(nblk * h, nblk * k)


def kernel(x, c, ctx, c_ctx, w_mod, b_mod, norm_mix, norm_ffn, final_norm, w_in_ab, lru_conv_w, lru_conv_b, lru_wa, lru_ba, lru_wx, lru_bx, lru_lam, attn_sink, w_out_ab, ffn_w1, ffn_w3, ffn_w2, w_in_cd, s5_a_re, s5_a_im, s5_log_step, s5_b_re, s5_b_im, s5_c_re, s5_c_im, s5_d, s5_w_glu, s5_b_glu, hg_lb_raw, hg_norm, w_out_cd, moe_router, moe_w1, moe_w3, moe_w2):
    nb, t, d = x.shape
    tc = ctx.shape[1]
    depth = w_mod.shape[0]
    assert tc % SEQ_BLOCK == 0 and t % SEQ_BLOCK == 0 and t % GRID_W == 0

    xa = jnp.concatenate([ctx, x], axis=1)
    mod_rows = -(-(nb + 1) // SUBLANES) * SUBLANES
    cvec = jnp.zeros((mod_rows, d), F32).at[:nb].set(c).at[nb].set(c_ctx)
    mod = _modulation(cvec, w_mod, b_mod).reshape(depth, mod_rows, N_MOD, d)

    cos_t, sin_t = _rope_tables(t, tc)
    lb_soft = jax.nn.softmax(hg_lb_raw.astype(F32), axis=0)
    lb_table = jnp.cumsum(lb_soft, axis=0) - lb_soft[0:1]
    n_exp = moe_router.shape[2]
    s5w = jax.vmap(_s5_weights)(s5_a_re, s5_a_im, s5_log_step, s5_b_re, s5_b_im, s5_c_re, s5_c_im)
    ffn_w = [w.astype(BF16) for w in (ffn_w1, ffn_w3, ffn_w2)]
    moe_w = [w.astype(BF16) for w in (moe_w1, moe_w3, moe_w2)]

    for l in range(depth):
        j = l // 2
        mod_l = mod[l]
        if l % 2 == 0:
            z = _inproj(xa, mod_l, norm_mix[l], w_in_ab[j].astype(BF16), tc, nb)
            lru = _lru(z, lru_conv_w[j], lru_conv_b[j],
                       jnp.stack([_block_diag_dense(lru_wa[j, dr]) for dr in range(2)]).astype(BF16),
                       lru_ba[j],
                       jnp.stack([_block_diag_dense(lru_wx[j, dr]) for dr in range(2)]).astype(BF16),
                       lru_bx[j], jax.nn.softplus(-lru_lam[j]), tc)
            sink = jnp.zeros((1, LANES), F32).at[0, :ATT_HEADS].set(attn_sink[j])
            att = _attention(z, cos_t, sin_t, sink, tc)
            xa = _outproj_ab(xa, mod_l, lru[0], lru[1], z, att, w_out_ab[j].astype(BF16), tc, nb)
            xa = _ffn(xa, mod_l, norm_ffn[l], ffn_w[0], ffn_w[1], ffn_w[2], j, tc, nb)
        else:
            z = _inproj(xa, mod_l, norm_mix[l], w_in_cd[j].astype(BF16), tc, nb)
            y5 = _s5(z, s5w, j, s5_d[j], tc, s5_d.shape[1])
            o_f, o_b = _gla(z, lb_table[j], tc)
            router_t = jnp.zeros((SUBLANES, d), F32).at[:n_exp].set(moe_router[j].T)
            xa, *routed = _outproj_cd(xa, mod_l, y5, o_f, o_b, z, s5_w_glu[j].astype(BF16),
                                      s5_b_glu[j], hg_norm[j], w_out_cd[j].astype(BF16),
                                      norm_ffn[l], router_t, n_exp, tc, nb)
            xa = _moe(xa, mod_l, routed, moe_w[0], moe_w[1], moe_w[2], j, tc, nb)
    return _final_norm(xa, final_norm, tc)
```

```python
import functools
import math

import jax
import jax.numpy as jnp
from jax import lax
from jax.experimental import pallas as pl
from jax.experimental.pallas import tpu as pltpu

F32 = jnp.float32
BF16 = jnp.bfloat16

EPS = 1e-6
GRID_W = 64
LRU_C = 8.0
CONV_W = 4
ATT_HEADS = 8
ATT_KV_HEADS = 2
ATT_GROUP = ATT_HEADS // ATT_KV_HEADS
HEAD_DIM = 64
WINDOW = 128
BLOCK_Q = 128
ROPE_FREQS = HEAD_DIM // 4
ROPE_BASE = 10000.0
S5_GROUP = 16
S5_CHUNK = 8
HG_HEADS = 4
HG_DK = 128
HG_CHUNK = 64
SEQ_BLOCK = 256
N_MOD = 6
MOE_PASS_ROWS = 256
MOE_TAIL_STEP = 64
LANES = 128
SUBLANES = 8
MXU_TILE = 256
VMEM_LIMIT = 56 * 1024 * 1024
NEG = -1e30
LOG2E = math.log2(math.e)


def _cparams(sem):
    return pltpu.CompilerParams(dimension_semantics=sem, vmem_limit_bytes=VMEM_LIMIT)


def _row_tile(ttot):
    for tm in (768, 1024, 512, 256):
        if ttot % tm == 0:
            return tm
    raise ValueError(f"unsupported token count {ttot}")


def _tall_row_tile(ttot):
    for tm in range(min(ttot, 1088) // 16 * 16, 15, -16):
        if ttot % tm == 0:
            return tm
    raise ValueError(f"unsupported token count {ttot}")


def _sigmoid(x):
    return 0.5 * jnp.tanh(0.5 * x) + 0.5


def _silu(x):
    return x * _sigmoid(x)


def _gelu(x):
    return 0.5 * x * (1.0 + jnp.tanh(math.sqrt(2.0 / math.pi) * (x + 0.044715 * (x * x * x))))


def _norm_mod(x, g, mb, mc, row0, tc, shift_idx, scale_idx):
    ms = jnp.mean(x * x, axis=-1, keepdims=True)
    y = x * lax.rsqrt(ms + EPS) * g
    rows = row0 + lax.broadcasted_iota(jnp.int32, (x.shape[0], 1), 0)
    is_ctx = rows < tc
    scale = jnp.where(is_ctx, mc[scale_idx:scale_idx + 1], mb[scale_idx:scale_idx + 1])
    shift = jnp.where(is_ctx, mc[shift_idx:shift_idx + 1], mb[shift_idx:shift_idx + 1])
    return y * (1.0 + scale) + shift


def _gate_mod(mb, mc, idx, row0, tc, n):
    rows = row0 + lax.broadcasted_iota(jnp.int32, (n, 1), 0)
    return jnp.where(rows < tc, mc[idx:idx + 1], mb[idx:idx + 1])


def _mod_kernel(c_ref, w_ref, b_ref, o_ref):
    s = _silu(c_ref[...])
    o_ref[...] = jnp.dot(s, w_ref[...], preferred_element_type=F32,
                         precision=lax.Precision.HIGHEST) + b_ref[...]


def _modulation(cvec, w_mod, b_mod):
    depth, d, n = w_mod.shape
    tn = 1536 if n % 1536 == 0 else n
    rows = cvec.shape[0]
    return pl.pallas_call(
        _mod_kernel,
        grid=(depth, n // tn),
        in_specs=[pl.BlockSpec((rows, d), lambda l, j: (0, 0)),
                  pl.BlockSpec((None, d, tn), lambda l, j: (l, 0, j)),
                  pl.BlockSpec((None, 1, tn), lambda l, j: (l, 0, j))],
        out_specs=pl.BlockSpec((None, rows, tn), lambda l, j: (l, 0, j)),
        out_shape=jax.ShapeDtypeStruct((depth, rows, n), F32),
        compiler_params=_cparams(("arbitrary", "arbitrary")),
        name="modulation",
    )(cvec, w_mod, b_mod.reshape(depth, 1, n))


def _inproj_kernel(x_ref, mb_ref, mc_ref, g_ref, w_ref, o_ref, *, tm, tc):
    i = pl.program_id(1)
    h = _norm_mod(x_ref[...], g_ref[...], mb_ref[...], mc_ref[...], i * tm, tc, 0, 1)
    o_ref[...] = jnp.dot(h.astype(BF16), w_ref[...], preferred_element_type=F32)


def _inproj(xa, mod_l, g, w, tc, nb):
    b, ttot, d = xa.shape
    n = w.shape[1]
    tm = _row_tile(ttot)
    return pl.pallas_call(
        functools.partial(_inproj_kernel, tm=tm, tc=tc),
        grid=(b, ttot // tm),
        in_specs=[pl.BlockSpec((None, tm, d), lambda bi, i: (bi, i, 0)),
                  pl.BlockSpec((None, N_MOD, d), lambda bi, i: (bi, 0, 0)),
                  pl.BlockSpec((None, N_MOD, d), lambda bi, i: (nb, 0, 0)),
                  pl.BlockSpec((1, d), lambda bi, i: (0, 0)),
                  pl.BlockSpec((d, n), lambda bi, i: (0, 0))],
        out_specs=pl.BlockSpec((None, tm, n), lambda bi, i: (bi, i, 0)),
        out_shape=jax.ShapeDtypeStruct((b, ttot, n), F32),
        compiler_params=_cparams(("parallel", "parallel")),
        name="inproj",
    )(xa, mod_l, mod_l, g.reshape(1, d), w)


def _seq_block(step, n_ctx, n_all, reverse):
    if not reverse:
        return step
    return jnp.where(step < n_ctx, n_ctx - 1 - step, n_all - 1 - step + n_ctx)


def _lru_kernel(*refs, lc, n_ctx, n_all):
    cw_ref, cb_ref = refs[:2]
    n_in = 8
    ins = [refs[2 + d * n_in:2 + (d + 1) * n_in] for d in range(2)]
    outs = refs[2 + 2 * n_in:4 + 2 * n_in]
    scratch = refs[4 + 2 * n_in:]
    for d in range(2):
        _lru_block(*ins[d], cw_ref, cb_ref, outs[d], *scratch[4 * d:4 * d + 4],
                   lc=lc, n_ctx=n_ctx, n_all=n_all, reverse=bool(d))


def _lru_block(u_ref, hp_ref, hn_ref, wa_ref, ba_ref, wx_ref, bx_ref, sp_ref, cw_ref, cb_ref,
               o_ref, uc_ref, a_ref, b_ref, h_ref, *, lc, n_ctx, n_all, reverse):
    step = pl.program_id(1)
    blk = _seq_block(step, n_ctx, n_all, reverse)
    has_prev = jnp.logical_and(blk != 0, blk != n_ctx)
    has_next = jnp.logical_and(blk != n_ctx - 1, blk != n_all - 1)

    @pl.when(step == 0)
    def _():
        h_ref[...] = jnp.zeros_like(h_ref)

    w0, w1, w2, w3 = (cw_ref[j:j + 1, :] for j in range(CONV_W))
    cb = cb_ref[...]
    u = u_ref[...]
    uc_ref[...] = (cb + pltpu.roll(u, 2, 0) * w0 + pltpu.roll(u, 1, 0) * w1 + u * w2
                   + pltpu.roll(u, lc - 1, 0) * w3)
    row = lax.broadcasted_iota(jnp.int32, (SUBLANES, 1), 0)
    hp = jnp.where(has_prev, hp_ref[...], 0.0)
    hn = jnp.where(has_next, hn_ref[...], 0.0)
    u0 = u_ref[0:SUBLANES, :]
    u1 = u_ref[SUBLANES:2 * SUBLANES, :]
    uc_ref[0:SUBLANES, :] = (
        cb + jnp.where(row < 2, pltpu.roll(hp, 2, 0), pltpu.roll(u0, 2, 0)) * w0
        + jnp.where(row < 1, pltpu.roll(hp, 1, 0), pltpu.roll(u0, 1, 0)) * w1 + u0 * w2
        + jnp.where(row < SUBLANES - 1, pltpu.roll(u0, SUBLANES - 1, 0),
                    pltpu.roll(u1, SUBLANES - 1, 0)) * w3)
    ul = u_ref[lc - SUBLANES:lc, :]
    um = u_ref[lc - 2 * SUBLANES:lc - SUBLANES, :]
    uc_ref[lc - SUBLANES:lc, :] = (
        cb + jnp.where(row < 2, pltpu.roll(um, 2, 0), pltpu.roll(ul, 2, 0)) * w0
        + jnp.where(row < 1, pltpu.roll(um, 1, 0), pltpu.roll(ul, 1, 0)) * w1 + ul * w2
        + jnp.where(row < SUBLANES - 1, pltpu.roll(ul, SUBLANES - 1, 0),
                    pltpu.roll(hn, SUBLANES - 1, 0)) * w3)

    uc = uc_ref[...]
    ub = uc.astype(BF16)
    r = _sigmoid(jnp.dot(ub, wa_ref[...], preferred_element_type=F32) + ba_ref[...])
    gi = _sigmoid(jnp.dot(ub, wx_ref[...], preferred_element_type=F32) + bx_ref[...])
    log_a = -LRU_C * r * sp_ref[...]
    a = jnp.exp(log_a)
    x = 1.0 - a * a
    b = jnp.where(x > 0.0, x * lax.rsqrt(x), 0.0) * (gi * uc)
    n_grp = lc // SUBLANES
    width = a.shape[1]
    a = a.reshape(n_grp, SUBLANES, width)
    b = b.reshape(n_grp, SUBLANES, width)
    sub = lax.broadcasted_iota(jnp.int32, (1, SUBLANES, 1), 1)
    dist = 1
    while dist < SUBLANES:
        inside = (sub < SUBLANES - dist) if reverse else (sub >= dist)
        shift = (SUBLANES - dist) if reverse else dist
        a_sh = jnp.where(inside, pltpu.roll(a, shift, 1), 1.0)
        b_sh = jnp.where(inside, pltpu.roll(b, shift, 1), 0.0)
        b = a * b_sh + b
        a = a * a_sh
        dist *= 2
    a_ref[...] = a.reshape(lc, width)
    b_ref[...] = b.reshape(lc, width)

    def body(i, carry):
        t = (n_grp - 1 - i) if reverse else i
        rows = pl.ds(pl.multiple_of(t * SUBLANES, SUBLANES), SUBLANES)
        h = a_ref[rows, :] * carry + b_ref[rows, :]
        b_ref[rows, :] = h
        return h[0:1, :] if reverse else h[SUBLANES - 1:SUBLANES, :]

    h_ref[0:1, :] = lax.fori_loop(0, n_grp, body, h_ref[0:1, :], unroll=4)
    o_ref[...] = b_ref[...].astype(o_ref.dtype)


def _lru(z, conv_w, conv_b, wa_bd, ba, wx_bd, bx, sp, tc):
    b, ttot, _ = z.shape
    w = conv_w.shape[1]
    lc = SEQ_BLOCK
    n_all, n_ctx = ttot // lc, tc // lc
    hb = lc // SUBLANES
    n_h = ttot // SUBLANES
    col = 1

    def direction(d):
        def blk(s):
            return _seq_block(s, n_ctx, n_all, bool(d))

        vec = pl.BlockSpec((None, 1, w), lambda bi, s: (d, 0, 0))
        mat = pl.BlockSpec((None, w, w), lambda bi, s: (d, 0, 0))
        specs = [pl.BlockSpec((None, lc, w), lambda bi, s: (bi, blk(s), col)),
                 pl.BlockSpec((None, SUBLANES, w),
                              lambda bi, s: (bi, jnp.maximum(blk(s) * hb - 1, 0), col)),
                 pl.BlockSpec((None, SUBLANES, w),
                              lambda bi, s: (bi, jnp.minimum((blk(s) + 1) * hb, n_h - 1), col)),
                 mat, vec, mat, vec, vec]
        return specs, pl.BlockSpec((None, lc, w), lambda bi, s: (bi, blk(s), 0))

    (in_f, out_f), (in_r, out_r) = direction(0), direction(1)
    out = jax.ShapeDtypeStruct((b, ttot, w), BF16)
    per_dir = [pltpu.VMEM((lc, w), F32)] * 3 + [pltpu.VMEM((SUBLANES, w), F32)]
    vecs = lambda x: x.reshape(2, 1, w)
    args = (z, z, z, wa_bd, vecs(ba), wx_bd, vecs(bx), vecs(sp))
    return pl.pallas_call(
        functools.partial(_lru_kernel, lc=lc, n_ctx=n_ctx, n_all=n_all),
        grid=(b, n_all),
        in_specs=[pl.BlockSpec((CONV_W, w), lambda bi, s: (0, 0)),
                  pl.BlockSpec((1, w), lambda bi, s: (0, 0))] + in_f + in_r,
        out_specs=[out_f, out_r],
        out_shape=[out, out],
        scratch_shapes=per_dir + per_dir,
        compiler_params=_cparams(("parallel", "arbitrary")),
        name="rglru_bidir",
    )(conv_w, conv_b.reshape(1, w), *args, *args)


def _rope(x, cos, sin_signed):
    lane = lax.broadcasted_iota(jnp.int32, x.shape, 1)
    swapped = jnp.where(lane % 32 < 16, pltpu.roll(x, LANES - 16, 1), pltpu.roll(x, 16, 1))
    return x * cos + swapped * sin_signed


def _pair_layout(x, x_sw, head):
    lo = lax.broadcasted_iota(jnp.int32, x.shape, 1) < HEAD_DIM
    first, second = (x, x_sw) if head == 0 else (x_sw, x)
    return jnp.concatenate([jnp.where(lo, first, 0.0), jnp.where(lo, 0.0, second)],
                           axis=0).astype(BF16)


def _attend(q_block, keys, vals, bias, sink_row, o_ref):
    nk = keys.shape[0]
    keys_sw = pltpu.roll(keys, HEAD_DIM, 1)
    vals_t = vals.T
    top = lax.broadcasted_iota(jnp.int32, (LANES, BLOCK_Q), 0) < HEAD_DIM
    zero_t = jnp.zeros((HEAD_DIM, nk), F32)
    pairs_per_kv = ATT_GROUP // 2
    for kv in range(ATT_KV_HEADS):
        kab = _pair_layout(keys, keys_sw, kv)
        vt = vals_t[kv * HEAD_DIM:(kv + 1) * HEAD_DIM, :]
        vab_t = jnp.concatenate([jnp.concatenate([vt, zero_t], axis=1),
                                 jnp.concatenate([zero_t, vt], axis=1)], axis=0).astype(BF16)
        for g in range(kv * pairs_per_kv, (kv + 1) * pairs_per_kv):
            s_t = lax.dot_general(kab, q_block(g), (((1,), (1,)), ((), ())),
                                  preferred_element_type=F32)
            halves = []
            recips = []
            for half in range(2):
                sh = s_t[half * nk:(half + 1) * nk, :]
                if bias is not None:
                    sh = sh + bias
                sk = sink_row[:, 2 * g + half:2 * g + half + 1]
                m = jnp.maximum(jnp.max(sh, axis=0, keepdims=True), sk)
                p = jnp.exp2(sh - m)
                recips.append(1.0 / (jnp.sum(p, axis=0, keepdims=True) + jnp.exp2(sk - m)))
                halves.append(p.astype(BF16))
            o_t = jnp.dot(vab_t, jnp.concatenate(halves, axis=0), preferred_element_type=F32)
            o_t = o_t * jnp.where(top, recips[0], recips[1])
            o_ref[:, g * LANES:(g + 1) * LANES] = o_t.T.astype(o_ref.dtype)


def _attn_kernel(q_ref, kp_ref, k0_ref, kn_ref, vp_ref, v0_ref, vn_ref, kc_ref, vc_ref,
                 cq_ref, sq_ref, cp_ref, sp_ref, cn_ref, sn_ref, sink_ref, o_ref, *, n_ctx, n_lat):
    i = pl.program_id(1)
    qscale = HEAD_DIM ** -0.5 * LOG2E
    sink_row = sink_ref[...] * LOG2E

    @pl.when(i < n_ctx)
    def _():
        def q_block(g):
            return (q_ref[:, g * LANES:(g + 1) * LANES] * qscale).astype(BF16)
        _attend(q_block, kc_ref[...], vc_ref[...], None, sink_row, o_ref)

    @pl.when(i >= n_ctx)
    def _():
        n = i - n_ctx
        cq, sq = cq_ref[...], sq_ref[...]

        def q_block(g):
            return (_rope(q_ref[:, g * LANES:(g + 1) * LANES], cq, sq) * qscale).astype(BF16)

        kp = _rope(kp_ref[...], cp_ref[...], sp_ref[...])
        k0 = _rope(k0_ref[...], cq, sq)
        kn = _rope(kn_ref[...], cn_ref[...], sn_ref[...])
        keys = jnp.concatenate([kp, k0, kn, kc_ref[...]], axis=0)
        vals = jnp.concatenate([vp_ref[...], v0_ref[...], vn_ref[...], vc_ref[...]], axis=0)
        nk = keys.shape[0]
        qpos = n * BLOCK_Q + lax.broadcasted_iota(jnp.int32, (nk, BLOCK_Q), 1)
        key = lax.broadcasted_iota(jnp.int32, (nk, BLOCK_Q), 0)
        kpos = (n - 1) * BLOCK_Q + key
        local = (jnp.abs(qpos - kpos) <= WINDOW) & (kpos >= 0) & (kpos < n_lat * BLOCK_Q)
        bias = jnp.where(local | (key >= 3 * BLOCK_Q), 0.0, NEG)
        _attend(q_block, keys, vals, bias, sink_row, o_ref)


def _attention(z, cos_t, sin_t, sink, tc):
    b, ttot, _ = z.shape
    n_all, n_ctx = ttot // BLOCK_Q, tc // BLOCK_Q
    n_lat = n_all - n_ctx
    kcol = (2 * 512 + ATT_HEADS * HEAD_DIM) // LANES
    vcol = kcol + 1
    qcol = 2 * 512 // (ATT_HEADS * HEAD_DIM)

    def prev(i):
        return jnp.maximum(i - 1, 0)

    def nxt(i):
        return jnp.minimum(i + 1, n_all - 1)

    def rows(col, f):
        return pl.BlockSpec((None, BLOCK_Q, LANES), lambda bi, i: (bi, f(i), col))

    def tab(f):
        return pl.BlockSpec((BLOCK_Q, LANES), lambda bi, i: (f(i), 0))

    same = lambda i: i
    return pl.pallas_call(
        functools.partial(_attn_kernel, n_ctx=n_ctx, n_lat=n_lat),
        grid=(b, n_all),
        in_specs=[pl.BlockSpec((None, BLOCK_Q, ATT_HEADS * HEAD_DIM), lambda bi, i: (bi, i, qcol)),
                  rows(kcol, prev), rows(kcol, same), rows(kcol, nxt),
                  rows(vcol, prev), rows(vcol, same), rows(vcol, nxt),
                  pl.BlockSpec((None, tc, LANES), lambda bi, i: (bi, 0, kcol)),
                  pl.BlockSpec((None, tc, LANES), lambda bi, i: (bi, 0, vcol)),
                  tab(same), tab(same), tab(prev), tab(prev), tab(nxt), tab(nxt),
                  pl.BlockSpec((1, LANES), lambda bi, i: (0, 0))],
        out_specs=pl.BlockSpec((None, BLOCK_Q, ATT_HEADS * HEAD_DIM), lambda bi, i: (bi, i, 0)),
        out_shape=jax.ShapeDtypeStruct((b, ttot, ATT_HEADS * HEAD_DIM), BF16),
        compiler_params=_cparams(("parallel", "parallel")),
        name="window_gqa",
    )(z, z, z, z, z, z, z, z, z, cos_t, sin_t, cos_t, sin_t, cos_t, sin_t, sink)


def _outproj_ab_kernel(x_ref, mb_ref, mc_ref, lf_ref, lb_ref, g_ref, at_ref, w_ref, o_ref, *, tm, tc):
    i = pl.program_id(1)
    lru = (lf_ref[...].astype(F32) + lb_ref[...].astype(F32)) * _gelu(g_ref[...])
    mix = jnp.concatenate([lru.astype(BF16), at_ref[...]], axis=1)
    dx = jnp.dot(mix, w_ref[...], preferred_element_type=F32)
    gate = _gate_mod(mb_ref[...], mc_ref[...], 2, i * tm, tc, tm)
    o_ref[...] = x_ref[...] + gate * dx


def _outproj_ab(xa, mod_l, lru_f, lru_b, z, att, w_out, tc, nb):
    b, ttot, d = xa.shape
    w = lru_f.shape[2]
    tm = _row_tile(ttot)
    tile = lambda width, col: pl.BlockSpec((None, tm, width), lambda bi, i: (bi, i, col))
    return pl.pallas_call(
        functools.partial(_outproj_ab_kernel, tm=tm, tc=tc),
        grid=(b, ttot // tm),
        in_specs=[tile(d, 0),
                  pl.BlockSpec((None, N_MOD, d), lambda bi, i: (bi, 0, 0)),
                  pl.BlockSpec((None, N_MOD, d), lambda bi, i: (nb, 0, 0)),
                  tile(w, 0), tile(w, 0), tile(w, 0), tile(att.shape[2], 0),
                  pl.BlockSpec(w_out.shape, lambda bi, i: (0, 0))],
        out_specs=tile(d, 0),
        out_shape=jax.ShapeDtypeStruct(xa.shape, F32),
        compiler_params=_cparams(("parallel", "parallel")),
        name="outproj_ab",
    )(xa, mod_l, mod_l, lru_f, lru_b, z, att, w_out)


def _outproj_cd_kernel(x_ref, mb_ref, mc_ref, y_ref, of_ref, ob_ref, g_ref, wg_ref, bg_ref, gn_ref,
                       w_ref, gf_ref, r_ref, o_ref, h_ref, wt_ref, cnt_ref, *, tm, tc, n_exp):
    i = pl.program_id(1)
    y = _gelu(y_ref[...])
    s5 = y * _sigmoid(jnp.dot(y.astype(BF16), wg_ref[...], preferred_element_type=F32) + bg_ref[...])
    o = of_ref[...].astype(F32) + ob_ref[...].astype(F32)
    parts = []
    for h in range(HG_HEADS):
        oh = o[:, h * HG_DK:(h + 1) * HG_DK]
        ms = jnp.mean(oh * oh, axis=-1, keepdims=True)
        parts.append(oh * lax.rsqrt(ms + EPS) * gn_ref[...])
    hg = jnp.concatenate(parts, axis=1) * _silu(g_ref[...])
    mix = jnp.concatenate([s5.astype(BF16), hg.astype(BF16)], axis=1)
    dx = jnp.dot(mix, w_ref[...], preferred_element_type=F32)
    gate = _gate_mod(mb_ref[...], mc_ref[...], 2, i * tm, tc, tm)
    x_new = x_ref[...] + gate * dx
    o_ref[...] = x_new
    h = _norm_mod(x_new, gf_ref[...], mb_ref[...], mc_ref[...], i * tm, tc, 3, 4)
    _route_tokens(h, r_ref, h_ref, wt_ref, cnt_ref, n_exp)


def _outproj_cd(xa, mod_l, y5, o_f, o_b, z, w_glu, b_glu, hg_norm, w_out, g_ffn, router_t, n_exp,
                tc, nb):
    b, ttot, d = xa.shape
    w = y5.shape[2]
    tm = _row_tile(ttot)
    n_t = ttot // tm
    gcol = z.shape[2] // w - 1
    tile = lambda width, col: pl.BlockSpec((None, tm, width), lambda bi, i: (bi, i, col))
    return pl.pallas_call(
        functools.partial(_outproj_cd_kernel, tm=tm, tc=tc, n_exp=n_exp),
        grid=(b, n_t),
        in_specs=[tile(d, 0),
                  pl.BlockSpec((None, N_MOD, d), lambda bi, i: (bi, 0, 0)),
                  pl.BlockSpec((None, N_MOD, d), lambda bi, i: (nb, 0, 0)),
                  tile(w, 0), tile(w, 0), tile(w, 0), tile(w, gcol),
                  pl.BlockSpec(w_glu.shape, lambda bi, i: (0, 0)),
                  pl.BlockSpec((1, w), lambda bi, i: (0, 0)),
                  pl.BlockSpec((1, HG_DK), lambda bi, i: (0, 0)),
                  pl.BlockSpec(w_out.shape, lambda bi, i: (0, 0)),
                  pl.BlockSpec((1, d), lambda bi, i: (0, 0)),
                  pl.BlockSpec((SUBLANES, d), lambda bi, i: (0, 0))],
        out_specs=[tile(d, 0),
                   pl.BlockSpec((tm, d), lambda bi, i: (bi * n_t + i, 0)),
                   pl.BlockSpec((SUBLANES, tm), lambda bi, i: (0, bi * n_t + i)),
                   pl.BlockSpec((None, SUBLANES, LANES), lambda bi, i: (bi * n_t + i, 0, 0))],
        out_shape=[jax.ShapeDtypeStruct(xa.shape, F32),
                   jax.ShapeDtypeStruct((b * ttot, d), BF16),
                   jax.ShapeDtypeStruct((SUBLANES, b * ttot), F32),
                   jax.ShapeDtypeStruct((b * n_t, SUBLANES, LANES), jnp.int32)],
        compiler_params=_cparams(("parallel", "parallel")),
        name="outproj_cd_route",
    )(xa, mod_l, mod_l, y5, o_f, o_b, z, w_glu, b_glu.reshape(1, w), hg_norm.reshape(1, HG_DK), w_out,
      g_ffn.reshape(1, d), router_t)


def _ffn_kernel(x_ref, mb_ref, mc_ref, g_ref, w1_ref, w3_ref, w2_ref, o_ref, h_ref, acc_ref, *, tm, tc):
    i = pl.program_id(1)
    k = pl.program_id(2)

    @pl.when(k == 0)
    def _():
        h = _norm_mod(x_ref[...], g_ref[...], mb_ref[...], mc_ref[...], i * tm, tc, 3, 4)
        h_ref[...] = h.astype(BF16)
        acc_ref[...] = jnp.zeros_like(acc_ref)

    h = h_ref[...]
    tf = w1_ref.shape[1]
    half = (tf // LANES + 1) // 2 * LANES
    y = None
    for f0, f1 in ([(0, half), (half, tf)] if 0 < half < tf else [(0, tf)]):
        a = jnp.dot(h, w1_ref[:, f0:f1], preferred_element_type=F32)
        c = jnp.dot(h, w3_ref[:, f0:f1], preferred_element_type=F32)
        part = jnp.dot((_silu(a) * c).astype(BF16), w2_ref[f0:f1, :], preferred_element_type=F32)
        y = part if y is None else y + part
    acc_ref[...] += y

    @pl.when(k == pl.num_programs(2) - 1)
    def _():
        gate = _gate_mod(mb_ref[...], mc_ref[...], 5, i * tm, tc, tm)
        o_ref[...] = x_ref[...] + gate * acc_ref[...]


def _ffn(xa, mod_l, g, w1, w3, w2, layer, tc, nb):
    b, ttot, d = xa.shape
    f = w1.shape[2]
    tm = _tall_row_tile(ttot)
    tf = f // 2 if (f // 2) % LANES == 0 else f
    return pl.pallas_call(
        functools.partial(_ffn_kernel, tm=tm, tc=tc),
        grid=(b, ttot // tm, f // tf),
        in_specs=[pl.BlockSpec((None, tm, d), lambda bi, i, k: (bi, i, 0)),
                  pl.BlockSpec((None, N_MOD, d), lambda bi, i, k: (bi, 0, 0)),
                  pl.BlockSpec((None, N_MOD, d), lambda bi, i, k: (nb, 0, 0)),
                  pl.BlockSpec((1, d), lambda bi, i, k: (0, 0)),
                  pl.BlockSpec((None, d, tf), lambda bi, i, k: (layer, 0, k)),
                  pl.BlockSpec((None, d, tf), lambda bi, i, k: (layer, 0, k)),
                  pl.BlockSpec((None, tf, d), lambda bi, i, k: (layer, k, 0))],
        out_specs=pl.BlockSpec((None, tm, d), lambda bi, i, k: (bi, i, 0)),
        out_shape=jax.ShapeDtypeStruct(xa.shape, F32),
        scratch_shapes=[pltpu.VMEM((tm, d), BF16), pltpu.VMEM((tm, d), F32)],
        compiler_params=_cparams(("parallel", "parallel", "arbitrary")),
        name="ffn_swiglu",
    )(xa, mod_l, mod_l, g.reshape(1, d), w1, w3, w2)


def _route_tokens(h, r_ref, h_ref, wt_ref, cnt_ref, n_exp):
    nt = (((1,), (1,)), ((), ()))
    h_hi = h.astype(BF16)
    h_lo = (h - h_hi.astype(F32)).astype(BF16)
    r = r_ref[...]
    r_hi = r.astype(BF16)
    r_lo = (r - r_hi.astype(F32)).astype(BF16)
    logits = (lax.dot_general(r_hi, h_hi, nt, preferred_element_type=F32)
              + lax.dot_general(r_lo, h_hi, nt, preferred_element_type=F32)
              + lax.dot_general(r_hi, h_lo, nt, preferred_element_type=F32))
    sub = lax.broadcasted_iota(jnp.int32, logits.shape, 0)
    h_ref[...] = h_hi
    logits = jnp.where(sub < n_exp, logits, NEG)
    m1 = jnp.max(logits, axis=0, keepdims=True)
    i1 = jnp.min(jnp.where(logits == m1, sub, SUBLANES), axis=0, keepdims=True)
    rest = jnp.where(sub == i1, NEG, logits)
    m2 = jnp.max(rest, axis=0, keepdims=True)
    i2 = jnp.min(jnp.where(rest == m2, sub, SUBLANES), axis=0, keepdims=True)
    e2 = jnp.exp(m2 - m1)
    g1 = 1.0 / (1.0 + e2)
    wt = jnp.where(sub == i1, g1, 0.0) + jnp.where(sub == i2, e2 * g1, 0.0)
    wt_ref[...] = wt
    cnt = jnp.sum(jnp.where(wt > 0.0, 1.0, 0.0), axis=1, keepdims=True)
    cnt_ref[...] = jnp.broadcast_to(cnt, cnt_ref.shape).astype(jnp.int32)


def _moe_kernel(cnt_ref, x_ref, *refs, tm, tc, n_exp, n_t, group):
    mb_refs = refs[:group]
    mc_ref, h_ref, wt_ref, w1_ref, w3_ref, w2_ref, o_ref, rank_ref = refs[group:]
    p = pl.program_id(0)
    e = pl.program_id(1)
    f = w1_ref.shape[1]
    half = (f // LANES + 1) // 2 * LANES
    f_split = [(0, half), (half, f)] if 0 < half < f else [(0, f)]

    @pl.when(e == 0)
    def _():
        o_ref[...] = jnp.zeros_like(o_ref)
        r_i = lax.broadcasted_iota(jnp.int32, (tm, tm), 0)
        c_i = lax.broadcasted_iota(jnp.int32, (tm, tm), 1)
        before = jnp.where(r_i < c_i, 1.0, 0.0).astype(BF16)
        for j in range(group):
            sel = wt_ref[:, j * tm:(j + 1) * tm] > 0.0
            rank = jnp.dot(jnp.where(sel, 1.0, 0.0).astype(BF16), before, preferred_element_type=F32)
            rank_ref[j] = jnp.where(sel, rank, -1.0)

    counts = [cnt_ref[(p * group + j) * n_exp + e] for j in range(group)]

    offsets = [jnp.int32(0)]
    for j in range(group - 1):
        offsets.append(offsets[-1] + counts[j])
    total = offsets[-1] + counts[-1]

    def expert_pass(m_rows, base):
        slot = (lax.broadcasted_iota(jnp.int32, (m_rows, tm), 0) + base).astype(F32)
        onehots = []
        xe = None
        gate = None
        for j in range(group):
            r_row = rank_ref[j, pl.ds(e, 1), :]
            r_row = jnp.where(r_row >= 0.0, r_row + offsets[j].astype(F32), -1.0)
            w_row = wt_ref[pl.ds(e, 1), j * tm:(j + 1) * tm]
            hit = r_row == slot
            onehot = jnp.where(hit, 1.0, 0.0).astype(BF16)
            onehots.append(onehot)
            xj = jnp.dot(onehot, h_ref[j * tm:(j + 1) * tm, :], preferred_element_type=F32)
            gj = jnp.sum(jnp.where(hit, w_row, 0.0), axis=1, keepdims=True)
            xe = xj if xe is None else xe + xj
            gate = gj if gate is None else gate + gj
        xe = xe.astype(BF16)
        y = None
        for f0, f1 in f_split:
            a = jnp.dot(xe, w1_ref[:, f0:f1], preferred_element_type=F32)
            c = jnp.dot(xe, w3_ref[:, f0:f1], preferred_element_type=F32)
            part = jnp.dot((_silu(a) * c * gate).astype(BF16), w2_ref[f0:f1, :],
                           preferred_element_type=F32)
            y = part if y is None else y + part
        y = y.astype(BF16)
        for j in range(group):
            o_ref[j * tm:(j + 1) * tm, :] += lax.dot_general(
                onehots[j], y, (((0,), (0,)), ((), ())), preferred_element_type=F32)

    n_full = total // MOE_PASS_ROWS

    def full_pass(it, carry):
        expert_pass(MOE_PASS_ROWS, it * MOE_PASS_ROWS)
        return carry

    lax.fori_loop(0, n_full, full_pass, 0)
    rest = total - n_full * MOE_PASS_ROWS
    for tail in range(MOE_TAIL_STEP, MOE_PASS_ROWS + 1, MOE_TAIL_STEP):
        @pl.when(jnp.logical_and(rest > tail - MOE_TAIL_STEP, rest <= tail))
        def _(tail=tail):
            expert_pass(tail, n_full * MOE_PASS_ROWS)

    @pl.when(e == n_exp - 1)
    def _():
        for j in range(group):
            i = (p * group + j) % n_t
            gate5 = _gate_mod(mb_refs[j][...], mc_ref[...], 5, i * tm, tc, tm)
            rows = slice(j * tm, (j + 1) * tm)
            o_ref[rows, :] = x_ref[rows, :] + gate5 * o_ref[rows, :]


def _moe(xa, mod_l, routed, w1, w3, w2, layer, tc, nb):
    b, ttot, d = xa.shape
    _, n_exp, _, f = w1.shape
    tm = _row_tile(ttot)
    n_t = ttot // tm
    group = 2 if (b * n_t) % 2 == 0 else 1
    h, wt, cnt = routed
    cnt = cnt[:, :n_exp, 0].reshape(-1)
    rows = group * tm
    mods = [pl.BlockSpec((None, N_MOD, d), functools.partial(
        lambda p, e, c, j: ((p * group + j) // n_t, 0, 0), j=j)) for j in range(group)]
    grid_spec = pltpu.PrefetchScalarGridSpec(
        num_scalar_prefetch=1,
        grid=(b * n_t // group, n_exp),
        in_specs=[pl.BlockSpec((rows, d), lambda p, e, c: (p, 0), pipeline_mode=pl.Buffered(1))]
        + mods
        + [pl.BlockSpec((None, N_MOD, d), lambda p, e, c: (nb, 0, 0)),
           pl.BlockSpec((rows, d), lambda p, e, c: (p, 0)),
           pl.BlockSpec((SUBLANES, rows), lambda p, e, c: (0, p)),
           pl.BlockSpec((None, None, d, f), lambda p, e, c: (layer, e, 0, 0)),
           pl.BlockSpec((None, None, d, f), lambda p, e, c: (layer, e, 0, 0)),
           pl.BlockSpec((None, None, f, d), lambda p, e, c: (layer, e, 0, 0))],
        out_specs=pl.BlockSpec((rows, d), lambda p, e, c: (p, 0)),
        scratch_shapes=[pltpu.VMEM((group, SUBLANES, tm), F32)])
    out = pl.pallas_call(
        functools.partial(_moe_kernel, tm=tm, tc=tc, n_exp=n_exp, n_t=n_t, group=group),
        grid_spec=grid_spec,
        out_shape=jax.ShapeDtypeStruct((b * ttot, d), F32),
        compiler_params=_cparams(("parallel", "arbitrary")),
        name="moe_experts",
    )(cnt, xa.reshape(b * ttot, d), *([mod_l] * group), mod_l, h, wt, w1, w3, w2)
    return out.reshape(b, ttot, d)


def _spread_groups(x, inner, row_group, n_grp):
    cols_in = x.shape[2]
    cols_out = cols_in * n_grp
    q = jnp.arange(cols_out)
    src = (q // (n_grp * inner)) * inner + q % inner
    tile = (jnp.arange(cols_in)[:, None] == src[None, :]).astype(BF16)
    col_group = (q // inner) % n_grp
    y = jnp.dot(x.astype(BF16), tile)
    return jnp.where(row_group[:, None] == col_group[None, :], y, jnp.zeros((), BF16))


def _s5_weights(a_re, a_im, log_step, b_re, b_im, c_re, c_im):
    L = S5_CHUNK
    lr = jnp.minimum(a_re, -1e-4)
    li = a_im
    dt = jnp.exp(log_step)[..., None]
    mag, ang = lr * dt, li * dt
    lbr, lbi = jnp.exp(mag) * jnp.cos(ang), jnp.exp(mag) * jnp.sin(ang)
    zr, zi = lbr - 1.0, lbi
    den = lr * lr + li * li
    fr = (zr * lr + zi * li) / den
    fi = (zi * lr - zr * li) / den
    bbr = fr[..., None] * b_re - fi[..., None] * b_im
    bbi = fr[..., None] * b_im + fi[..., None] * b_re

    def power(p):
        p = p[..., None, None, None].astype(F32)
        return jnp.exp(mag * p) * jnp.cos(ang * p), jnp.exp(mag * p) * jnp.sin(ang * p)

    n_dir, n_grp, n_st = a_re.shape
    gpb = LANES // S5_GROUP
    n_blk = n_grp // gpb
    s = jnp.arange(L)
    in_group = (jnp.arange(L * LANES) % LANES) // S5_GROUP
    st_group = jnp.arange(gpb * n_st) // n_st
    outs = []
    for d in range(n_dir):
        pr, pi = power(jnp.arange(L))
        pr, pi = pr[:, d], pi[:, d]
        cbr = (jnp.einsum('gcn,tgn,gnk->tgck', c_re[d], pr, bbr[d])
               - jnp.einsum('gcn,tgn,gnk->tgck', c_re[d], pi, bbi[d])
               - jnp.einsum('gcn,tgn,gnk->tgck', c_im[d], pr, bbi[d])
               - jnp.einsum('gcn,tgn,gnk->tgck', c_im[d], pi, bbr[d]))
        lag = (s[None, :] - s[:, None]) if d == 0 else (s[:, None] - s[None, :])
        kern = jnp.where((lag >= 0)[..., None, None, None], cbr[jnp.clip(lag, 0, L - 1)], 0.0)
        kern = kern.reshape(L, L, n_blk, gpb, S5_GROUP, S5_GROUP).transpose(2, 0, 3, 5, 1, 4)
        m = _spread_groups(kern.reshape(n_blk, L * LANES, L * S5_GROUP), S5_GROUP, in_group, gpb)
        qr, qi = power((L - 1 - s) if d == 0 else s)
        qr, qi = qr[:, d], qi[:, d]
        str_ = qr[..., None] * bbr[d] - qi[..., None] * bbi[d]
        sti = qr[..., None] * bbi[d] + qi[..., None] * bbr[d]

        def to_state(x):
            x = x.reshape(L, n_blk, gpb, n_st, S5_GROUP).transpose(1, 0, 2, 4, 3)
            return _spread_groups(x.reshape(n_blk, L * LANES, n_st), n_st, in_group, gpb)

        wst = jnp.concatenate([to_state(str_), to_state(sti)], axis=2)
        rr, ri = power((s + 1) if d == 0 else (L - s))
        rr, ri = rr[:, d], ri[:, d]
        wr = c_re[d][None] * rr[:, :, None, :] - c_im[d][None] * ri[:, :, None, :]
        wi = -(c_re[d][None] * ri[:, :, None, :] + c_im[d][None] * rr[:, :, None, :])

        def from_state(x):
            x = x.reshape(L, n_blk, gpb, S5_GROUP, n_st).transpose(1, 2, 4, 0, 3)
            return _spread_groups(x.reshape(n_blk, gpb * n_st, L * S5_GROUP), S5_GROUP, st_group, gpb)

        wout = jnp.concatenate([from_state(wr), from_state(wi)], axis=1)
        ler, lei = jnp.exp(mag[d] * L) * jnp.cos(ang[d] * L), jnp.exp(mag[d] * L) * jnp.sin(ang[d] * L)
        lam_l = jnp.concatenate([ler.reshape(n_blk, 1, gpb * n_st), lei.reshape(n_blk, 1, gpb * n_st)],
                                axis=2)
        outs.append((m, wst, wout, lam_l))
    return tuple(jnp.stack([o[k] for o in outs]) for k in range(4))


def _s5_kernel(u_ref, m_ref, wst_ref, wout_ref, lam_ref, d_ref, o_ref, x_ref, hp_ref, *, rows, rows_ctx):
    L = S5_CHUNK
    dr = pl.program_id(2)
    ns = lam_ref.shape[1] // 2
    u = jnp.concatenate([u_ref[pl.ds(s, rows, stride=L), :] for s in range(L)], axis=1)
    ub = u.astype(BF16)
    x_ref[...] = jnp.dot(ub, wst_ref[...], preferred_element_type=F32)
    lr = lam_ref[:, 0:ns]
    li = lam_ref[:, ns:2 * ns]

    def visit(r, carry):
        hr, hi = carry
        hp_ref[pl.ds(r, 1), 0:ns] = hr
        hp_ref[pl.ds(r, 1), ns:2 * ns] = hi
        xr = x_ref[pl.ds(r, 1), 0:ns]
        xi = x_ref[pl.ds(r, 1), ns:2 * ns]
        return lr * hr - li * hi + xr, lr * hi + li * hr + xi

    zero = (jnp.zeros((1, ns), F32), jnp.zeros((1, ns), F32))

    @pl.when(dr == 0)
    def _():
        lax.fori_loop(0, rows, visit, zero, unroll=4)

    @pl.when(dr == 1)
    def _():
        c = lax.fori_loop(0, rows_ctx, lambda t, c: visit(rows_ctx - 1 - t, c), zero, unroll=4)
        lax.fori_loop(0, rows - rows_ctx, lambda t, c: visit(rows - 1 - t, c), c, unroll=4)

    y_state = jnp.dot(hp_ref[...].astype(BF16), wout_ref[...], preferred_element_type=F32)
    n_in = L * LANES

    def within_chunk(col0, col1, reverse):
        k0, k1 = (col0, n_in) if reverse else (0, col1)
        return jnp.dot(ub[:, k0:k1], m_ref[k0:k1, col0:col1], preferred_element_type=F32)

    @pl.when(dr == 0)
    def _():
        for c0 in range(0, n_in, MXU_TILE):
            y = within_chunk(c0, c0 + MXU_TILE, False) + y_state[:, c0:c0 + MXU_TILE]
            for s in range(c0 // LANES, (c0 + MXU_TILE) // LANES):
                lanes = slice(s * LANES - c0, (s + 1) * LANES - c0)
                o_ref[pl.ds(s, rows, stride=L), :] = (y[:, lanes]
                                                      + d_ref[...] * u[:, s * LANES:(s + 1) * LANES])

    @pl.when(dr == 1)
    def _():
        for c0 in range(0, n_in, MXU_TILE):
            y = within_chunk(c0, c0 + MXU_TILE, True) + y_state[:, c0:c0 + MXU_TILE]
            for s in range(c0 // LANES, (c0 + MXU_TILE) // LANES):
                lanes = slice(s * LANES - c0, (s + 1) * LANES - c0)
                o_ref[pl.ds(s, rows, stride=L), :] += y[:, lanes]


def _s5(z, weights, layer, dvec, tc, width):
    b, ttot, _ = z.shape
    m, wst, wout, lam_l = weights
    n_dir, n_blk = m.shape[1], m.shape[2]
    rows, rows_ctx = ttot // S5_CHUNK, tc // S5_CHUNK
    wspec = lambda a: pl.BlockSpec((None, None, None) + a.shape[3:],
                                   lambda bi, j, dr: (layer, dr, j, 0, 0))
    return pl.pallas_call(
        functools.partial(_s5_kernel, rows=rows, rows_ctx=rows_ctx),
        grid=(b, n_blk, n_dir),
        in_specs=[pl.BlockSpec((None, ttot, LANES), lambda bi, j, dr: (bi, 0, j)),
                  wspec(m), wspec(wst), wspec(wout), wspec(lam_l),
                  pl.BlockSpec((1, LANES), lambda bi, j, dr: (0, j))],
        out_specs=pl.BlockSpec((None, ttot, LANES), lambda bi, j, dr: (bi, 0, j)),
        out_shape=jax.ShapeDtypeStruct((b, ttot, width), F32),
        scratch_shapes=[pltpu.VMEM((rows, lam_l.shape[4]), F32), pltpu.VMEM((rows, lam_l.shape[4]), F32)],
        compiler_params=_cparams(("parallel", "parallel", "arbitrary")),
        name="s5_bidir",
    )(z, m, wst, wout, lam_l, dvec.reshape(1, width))


def _gla_kernel(qf_ref, ff_ref, vf_ref, qb_ref, fb_ref, vb_ref, lb_ref, of_ref, ob_ref,
                stf_ref, stb_ref, *, lc):
    step = pl.program_id(1)

    @pl.when(step == 0)
    def _():
        stf_ref[...] = jnp.zeros_like(stf_ref)
        stb_ref[...] = jnp.zeros_like(stb_ref)

    _gla_block(qf_ref, ff_ref, vf_ref, lb_ref, of_ref, stf_ref, lc=lc, reverse=False)
    _gla_block(qb_ref, fb_ref, vb_ref, lb_ref, ob_ref, stb_ref, lc=lc, reverse=True)


def _gla_block(q_ref, f_ref, v_ref, lb_ref, o_ref, st_ref, *, lc, reverse):
    c = HG_CHUNK
    n_chunks = lc // c
    r_i = lax.broadcasted_iota(jnp.int32, (lc, lc), 0)
    c_i = lax.broadcasted_iota(jnp.int32, (lc, lc), 1)
    same_chunk = (r_i // c) == (c_i // c)
    keep = same_chunk & ((c_i >= r_i) if reverse else (c_i <= r_i))
    tri = jnp.where(keep, 1.0, 0.0).astype(BF16)
    lb = lb_ref[...]
    q = _silu(q_ref[...])
    f = lb + (1.0 - lb) / (1.0 + jnp.exp(-f_ref[...]))
    k = 1.0 - f
    logf = jnp.log(f)
    v = v_ref[...].astype(BF16)
    hi = logf.astype(BF16)
    lo = (logf - hi.astype(F32)).astype(BF16)
    cum = jnp.dot(tri, hi, preferred_element_type=F32) + jnp.dot(tri, lo, preferred_element_type=F32)
    totals = [cum[ci * c:ci * c + 1, :] if reverse else cum[(ci + 1) * c - 1:(ci + 1) * c, :]
              for ci in range(n_chunks)]
    w = cum.shape[1]
    total = jnp.concatenate([jnp.broadcast_to(t, (c, w)) for t in totals], axis=0)
    centre = jnp.concatenate([jnp.broadcast_to(cum[ci * c + c // 2:ci * c + c // 2 + 1, :], (c, w))
                              for ci in range(n_chunks)], axis=0)
    q_in = (q * jnp.exp(cum)).astype(BF16)
    k_out = (k * jnp.exp(total - cum)).astype(BF16)
    q_loc = (q * jnp.exp(cum - centre)).astype(BF16)
    k_loc = (k * jnp.exp(centre - cum)).astype(BF16)
    decays = [jnp.exp(t) for t in totals]
    order = range(n_chunks - 1, -1, -1) if reverse else range(n_chunks)
    heads = [slice(h * HG_DK, (h + 1) * HG_DK) for h in range(HG_HEADS)]
    nt = (((1,), (1,)), ((), ()))
    tn = (((0,), (0,)), ((), ()))
    atts = [lax.dot_general(q_loc[:, sl], k_loc[:, sl], nt, preferred_element_type=F32)
            for sl in heads]
    atts = [jnp.where(keep, att, 0.0).astype(BF16) for att in atts]
    o_locs = [jnp.dot(att, v[:, sl], preferred_element_type=F32) for att, sl in zip(atts, heads)]
    incs = [[lax.dot_general(v[ci * c:(ci + 1) * c, sl], k_out[ci * c:(ci + 1) * c, sl], tn,
                             preferred_element_type=F32) for ci in range(n_chunks)] for sl in heads]
    states = [st_ref[h] for h in range(HG_HEADS)]
    for ci in order:
        rows = slice(ci * c, (ci + 1) * c)
        for h, sl in enumerate(heads):
            o = o_locs[h][rows] + lax.dot_general(q_in[rows, sl], states[h].astype(BF16), nt,
                                                  preferred_element_type=F32)
            o_ref[rows, sl] = o.astype(o_ref.dtype)
            states[h] = states[h] * decays[ci][:, sl] + incs[h][ci]
    for h in range(HG_HEADS):
        st_ref[h] = states[h]


def _gla(z, lb, tc):
    b, ttot, _ = z.shape
    w = HG_HEADS * HG_DK
    lc = SEQ_BLOCK
    n_all, n_ctx = ttot // lc, tc // lc

    def col(cidx, reverse):
        return pl.BlockSpec((None, lc, w),
                            lambda bi, s: (bi, _seq_block(s, n_ctx, n_all, reverse), cidx))

    out = jax.ShapeDtypeStruct((b, ttot, w), BF16)
    state = pltpu.VMEM((HG_HEADS, HG_DK, HG_DK), F32)
    return pl.pallas_call(
        functools.partial(_gla_kernel, lc=lc),
        grid=(b, n_all),
        in_specs=[col(1, False), col(2, False), col(4, False),
                  col(1, True), col(3, True), col(4, True),
                  pl.BlockSpec((1, w), lambda bi, s: (0, 0))],
        out_specs=[col(0, False), col(0, True)],
        out_shape=[out, out],
        scratch_shapes=[state, state],
        compiler_params=_cparams(("parallel", "arbitrary")),
        name="hgrn2_bidir",
    )(z, z, z, z, z, z, lb.reshape(1, w))


def _final_kernel(x_ref, g_ref, o_ref):
    x = x_ref[0]
    ms = jnp.mean(x * x, axis=-1, keepdims=True)
    o_ref[...] = x * lax.rsqrt(ms + EPS) * g_ref[...]


def _final_norm(xa, g, tc):
    b, ttot, d = xa.shape
    t = ttot - tc
    tm = next(rows for rows in (1024, 512, SEQ_BLOCK) if t % rows == 0)
    return pl.pallas_call(
        _final_kernel,
        grid=(b, t // tm),
        in_specs=[pl.BlockSpec((pl.Element(1), pl.Element(tm), pl.Element(d)),
                               lambda bi, i: (bi, pl.multiple_of(tc + i * tm, SUBLANES), 0)),
                  pl.BlockSpec((1, d), lambda bi, i: (0, 0))],
        out_specs=pl.BlockSpec((None, tm, d), lambda bi, i: (bi, i, 0)),
        out_shape=jax.ShapeDtypeStruct((b, t, d), F32),
        compiler_params=_cparams(("parallel", "parallel")),
        name="final_norm",
    )(xa, g.reshape(1, d))


def _rope_tables(t, tc):
    pos = jnp.arange(t)
    row = (pos // GRID_W).astype(F32)
    col = (pos % GRID_W).astype(F32)
    inv = ROPE_BASE ** (-jnp.arange(ROPE_FREQS, dtype=F32) / ROPE_FREQS)
    ar, ac = row[:, None] * inv, col[:, None] * inv
    cos = jnp.concatenate([jnp.cos(ar), jnp.cos(ar), jnp.cos(ac), jnp.cos(ac)], axis=1)
    sin = jnp.concatenate([-jnp.sin(ar), jnp.sin(ar), -jnp.sin(ac), jnp.sin(ac)], axis=1)
    cos = jnp.concatenate([jnp.ones((tc, HEAD_DIM), F32), cos], axis=0)
    sin = jnp.concatenate([jnp.zeros((tc, HEAD_DIM), F32), sin], axis=0)
    return jnp.tile(cos, (1, LANES // HEAD_DIM)), jnp.tile(sin, (1, LANES // HEAD_DIM))


def _block_diag_dense(w):
    nblk, h, k = w.shape
    return jnp.einsum('nhk,nm->nhmk', w, jnp.eye(nblk, dtype=w.dtype)).reshape(nblk * h, nblk * k)


def kernel(x, c, ctx, c_ctx, w_mod, b_mod, norm_mix, norm_ffn, final_norm, w_in_ab, lru_conv_w, lru_conv_b, lru_wa, lru_ba, lru_wx, lru_bx, lru_lam, attn_sink, w_out_ab, ffn_w1, ffn_w3, ffn_w2, w_in_cd, s5_a_re, s5_a_im, s5_log_step, s5_b_re, s5_b_im, s5_c_re, s5_c_im, s5_d, s5_w_glu, s5_b_glu, hg_lb_raw, hg_norm, w_out_cd, moe_router, moe_w1, moe_w3, moe_w2):
    nb, t, d = x.shape
    tc = ctx.shape[1]
    depth = w_mod.shape[0]
    assert tc % SEQ_BLOCK == 0 and t % SEQ_BLOCK == 0 and t % GRID_W == 0

    xa = jnp.concatenate([ctx, x], axis=1)
    mod_rows = -(-(nb + 1) // SUBLANES) * SUBLANES
    cvec = jnp.zeros((mod_rows, d), F32).at[:nb].set(c).at[nb].set(c_ctx)
    mod = _modulation(cvec, w_mod, b_mod).reshape(depth, mod_rows, N_MOD, d)

    cos_t, sin_t = _rope_tables(t, tc)
    lb_soft = jax.nn.softmax(hg_lb_raw.astype(F32), axis=0)
    lb_table = jnp.cumsum(lb_soft, axis=0) - lb_soft[0:1]
    n_exp = moe_router.shape[2]
    s5w = jax.vmap(_s5_weights)(s5_a_re, s5_a_im, s5_log_step, s5_b_re, s5_b_im, s5_c_re, s5_c_im)
    ffn_w = [w.astype(BF16) for w in (ffn_w1, ffn_w3, ffn_w2)]
    moe_w = [w.astype(BF16) for w in (moe_w1, moe_w3, moe_w2)]

    for l in range(depth):
        j = l // 2
        mod_l = mod[l]
        if l % 2 == 0:
            z = _inproj(xa, mod_l, norm_mix[l], w_in_ab[j].astype(BF16), tc, nb)
            lru = _lru(z, lru_conv_w[j], lru_conv_b[j],
                       jnp.stack([_block_diag_dense(lru_wa[j, dr]) for dr in range(2)]).astype(BF16),
                       lru_ba[j],
                       jnp.stack([_block_diag_dense(lru_wx[j, dr]) for dr in range(2)]).astype(BF16),
                       lru_bx[j], jax.nn.softplus(-lru_lam[j]), tc)
            sink = jnp.zeros((1, LANES), F32).at[0, :ATT_HEADS].set(attn_sink[j])
            att = _attention(z, cos_t, sin_t, sink, tc)
            xa = _outproj_ab(xa, mod_l, lru[0], lru[1], z, att, w_out_ab[j].astype(BF16), tc, nb)
            xa = _ffn(xa, mod_l, norm_ffn[l], ffn_w[0], ffn_w[1], ffn_w[2], j, tc, nb)
        else:
            z = _inproj(xa, mod_l, norm_mix[l], w_in_cd[j].astype(BF16), tc, nb)
            y5 = _s5(z, s5w, j, s5_d[j], tc, s5_d.shape[1])
            o_f, o_b = _gla(z, lb_table[j], tc)
            router_t = jnp.zeros((SUBLANES, d), F32).at[:n_exp].set(moe_router[j].T)
            xa, *routed = _outproj_cd(xa, mod_l, y5, o_f, o_b, z, s5_w_glu[j].astype(BF16),
                                      s5_b_glu[j], hg_norm[j], w_out_cd[j].astype(BF16),
                                      norm_ffn[l], router_t, n_exp, tc, nb)
            xa = _moe(xa, mod_l, routed, moe_w[0], moe_w[1], moe_w[2], j, tc, nb)
    return _final_norm(xa, final_norm, tc)
```

```python
import functools
import math

import jax
import jax.numpy as jnp
from jax import lax
from jax.experimental import pallas as pl
from jax.experimental.pallas import tpu as pltpu

F32 = jnp.float32
BF16 = jnp.bfloat16

EPS = 1e-6
GRID_W = 64
LRU_C = 8.0
CONV_W = 4
ATT_HEADS = 8
ATT_KV_HEADS = 2
ATT_GROUP = ATT_HEADS // ATT_KV_HEADS
HEAD_DIM = 64
WINDOW = 128
BLOCK_Q = 128
ROPE_FREQS = HEAD_DIM // 4
ROPE_BASE = 10000.0
S5_GROUP = 16
S5_CHUNK = 8
HG_HEADS = 4
HG_DK = 128
HG_CHUNK = 64
SEQ_BLOCK = 256
N_MOD = 6
MOE_PASS_ROWS = 256
MOE_TAIL_STEP = 64
LANES = 128
SUBLANES = 8
MXU_TILE = 256
VMEM_LIMIT = 56 * 1024 * 1024
NEG = -1e30
LOG2E = math.log2(math.e)


def _cparams(sem):
    return pltpu.CompilerParams(dimension_semantics=sem, vmem_limit_bytes=VMEM_LIMIT)


def _row_tile(ttot):
    for tm in (768, 1024, 512, 256):
        if ttot % tm == 0:
            return tm
    raise ValueError(f"unsupported token count {ttot}")


def _tall_row_tile(ttot):
    for tm in range(min(ttot, 1088) // 16 * 16, 15, -16):
        if ttot % tm == 0:
            return tm
    raise ValueError(f"unsupported token count {ttot}")


def _sigmoid(x):
    return 0.5 * jnp.tanh(0.5 * x) + 0.5


def _silu(x):
    return x * _sigmoid(x)


def _gelu(x):
    return 0.5 * x * (1.0 + jnp.tanh(math.sqrt(2.0 / math.pi) * (x + 0.044715 * (x * x * x))))


def _norm_mod(x, g, mb, mc, row0, tc, shift_idx, scale_idx):
    ms = jnp.mean(x * x, axis=-1, keepdims=True)
    y = x * lax.rsqrt(ms + EPS) * g
    rows = row0 + lax.broadcasted_iota(jnp.int32, (x.shape[0], 1), 0)
    is_ctx = rows < tc
    scale = jnp.where(is_ctx, mc[scale_idx:scale_idx + 1], mb[scale_idx:scale_idx + 1])
    shift = jnp.where(is_ctx, mc[shift_idx:shift_idx + 1], mb[shift_idx:shift_idx + 1])
    return y * (1.0 + scale) + shift


def _gate_mod(mb, mc, idx, row0, tc, n):
    rows = row0 + lax.broadcasted_iota(jnp.int32, (n, 1), 0)
    return jnp.where(rows < tc, mc[idx:idx + 1], mb[idx:idx + 1])


def _mod_kernel(c_ref, w_ref, b_ref, o_ref):
    s = _silu(c_ref[...])
    o_ref[...] = jnp.dot(s, w_ref[...], preferred_element_type=F32,
                         precision=lax.Precision.HIGHEST) + b_ref[...]


def _modulation(cvec, w_mod, b_mod):
    depth, d, n = w_mod.shape
    tn = 1536 if n % 1536 == 0 else n
    rows = cvec.shape[0]
    return pl.pallas_call(
        _mod_kernel,
        grid=(depth, n // tn),
        in_specs=[pl.BlockSpec((rows, d), lambda l, j: (0, 0)),
                  pl.BlockSpec((None, d, tn), lambda l, j: (l, 0, j)),
                  pl.BlockSpec((None, 1, tn), lambda l, j: (l, 0, j))],
        out_specs=pl.BlockSpec((None, rows, tn), lambda l, j: (l, 0, j)),
        out_shape=jax.ShapeDtypeStruct((depth, rows, n), F32),
        compiler_params=_cparams(("arbitrary", "arbitrary")),
        name="modulation",
    )(cvec, w_mod, b_mod.reshape(depth, 1, n))


def _inproj_kernel(x_ref, mb_ref, mc_ref, g_ref, w_ref, o_ref, *, tm, tc):
    i = pl.program_id(1)
    h = _norm_mod(x_ref[...], g_ref[...], mb_ref[...], mc_ref[...], i * tm, tc, 0, 1)
    o_ref[...] = jnp.dot(h.astype(BF16), w_ref[...], preferred_element_type=F32)


def _inproj(xa, mod_l, g, w, tc, nb):
    b, ttot, d = xa.shape
    n = w.shape[1]
    tm = _row_tile(ttot)
    return pl.pallas_call(
        functools.partial(_inproj_kernel, tm=tm, tc=tc),
        grid=(b, ttot // tm),
        in_specs=[pl.BlockSpec((None, tm, d), lambda bi, i: (bi, i, 0)),
                  pl.BlockSpec((None, N_MOD, d), lambda bi, i: (bi, 0, 0)),
                  pl.BlockSpec((None, N_MOD, d), lambda bi, i: (nb, 0, 0)),
                  pl.BlockSpec((1, d), lambda bi, i: (0, 0)),
                  pl.BlockSpec((d, n), lambda bi, i: (0, 0))],
        out_specs=pl.BlockSpec((None, tm, n), lambda bi, i: (bi, i, 0)),
        out_shape=jax.ShapeDtypeStruct((b, ttot, n), F32),
        compiler_params=_cparams(("parallel", "parallel")),
        name="inproj",
    )(xa, mod_l, mod_l, g.reshape(1, d), w)


def _seq_block(step, n_ctx, n_all, reverse):
    if not reverse:
        return step
    return jnp.where(step < n_ctx, n_ctx - 1 - step, n_all - 1 - step + n_ctx)


def _lru_kernel(*refs, lc, n_ctx, n_all):
    cw_ref, cb_ref = refs[:2]
    n_in = 8
    ins = [refs[2 + d * n_in:2 + (d + 1) * n_in] for d in range(2)]
    outs = refs[2 + 2 * n_in:4 + 2 * n_in]
    scratch = refs[4 + 2 * n_in:]
    for d in range(2):
        _lru_block(*ins[d], cw_ref, cb_ref, outs[d], *scratch[4 * d:4 * d + 4],
                   lc=lc, n_ctx=n_ctx, n_all=n_all, reverse=bool(d))


def _lru_block(u_ref, hp_ref, hn_ref, wa_ref, ba_ref, wx_ref, bx_ref, sp_ref, cw_ref, cb_ref,
               o_ref, uc_ref, a_ref, b_ref, h_ref, *, lc, n_ctx, n_all, reverse):
    step = pl.program_id(1)
    blk = _seq_block(step, n_ctx, n_all, reverse)
    has_prev = jnp.logical_and(blk != 0, blk != n_ctx)
    has_next = jnp.logical_and(blk != n_ctx - 1, blk != n_all - 1)

    @pl.when(step == 0)
    def _():
        h_ref[...] = jnp.zeros_like(h_ref)

    w0, w1, w2, w3 = (cw_ref[j:j + 1, :] for j in range(CONV_W))
    cb = cb_ref[...]
    u = u_ref[...]
    uc_ref[...] = (cb + pltpu.roll(u, 2, 0) * w0 + pltpu.roll(u, 1, 0) * w1 + u * w2
                   + pltpu.roll(u, lc - 1, 0) * w3)
    row = lax.broadcasted_iota(jnp.int32, (SUBLANES, 1), 0)
    hp = jnp.where(has_prev, hp_ref[...], 0.0)
    hn = jnp.where(has_next, hn_ref[...], 0.0)
    u0 = u_ref[0:SUBLANES, :]
    u1 = u_ref[SUBLANES:2 * SUBLANES, :]
    uc_ref[0:SUBLANES, :] = (
        cb + jnp.where(row < 2, pltpu.roll(hp, 2, 0), pltpu.roll(u0, 2, 0)) * w0
        + jnp.where(row < 1, pltpu.roll(hp, 1, 0), pltpu.roll(u0, 1, 0)) * w1 + u0 * w2
        + jnp.where(row < SUBLANES - 1, pltpu.roll(u0, SUBLANES - 1, 0),
                    pltpu.roll(u1, SUBLANES - 1, 0)) * w3)
    ul = u_ref[lc - SUBLANES:lc, :]
    um = u_ref[lc - 2 * SUBLANES:lc - SUBLANES, :]
    uc_ref[lc - SUBLANES:lc, :] = (
        cb + jnp.where(row < 2, pltpu.roll(um, 2, 0), pltpu.roll(ul, 2, 0)) * w0
        + jnp.where(row < 1, pltpu.roll(um, 1, 0), pltpu.roll(ul, 1, 0)) * w1 + ul * w2
        + jnp.where(row < SUBLANES - 1, pltpu.roll(ul, SUBLANES - 1, 0),
                    pltpu.roll(hn, SUBLANES - 1, 0)) * w3)

    uc = uc_ref[...]
    ub = uc.astype(BF16)
    r = _sigmoid(jnp.dot(ub, wa_ref[...], preferred_element_type=F32) + ba_ref[...])
    gi = _sigmoid(jnp.dot(ub, wx_ref[...], preferred_element_type=F32) + bx_ref[...])
    log_a = -LRU_C * r * sp_ref[...]
    a = jnp.exp(log_a)
    x = 1.0 - a * a
    b = jnp.where(x > 0.0, x * lax.rsqrt(x), 0.0) * (gi * uc)
    n_grp = lc // SUBLANES
    width = a.shape[1]
    a = a.reshape(n_grp, SUBLANES, width)
    b = b.reshape(n_grp, SUBLANES, width)
    sub = lax.broadcasted_iota(jnp.int32, (1, SUBLANES, 1), 1)
    dist = 1
    while dist < SUBLANES:
        inside = (sub < SUBLANES - dist) if reverse else (sub >= dist)
        shift = (SUBLANES - dist) if reverse else dist
        a_sh = jnp.where(inside, pltpu.roll(a, shift, 1), 1.0)
        b_sh = jnp.where(inside, pltpu.roll(b, shift, 1), 0.0)
        b = a * b_sh + b
        a = a * a_sh
        dist *= 2
    a_ref[...] = a.reshape(lc, width)
    b_ref[...] = b.reshape(lc, width)

    def body(i, carry):
        t = (n_grp - 1 - i) if reverse else i
        rows = pl.ds(pl.multiple_of(t * SUBLANES, SUBLANES), SUBLANES)
        h = a_ref[rows, :] * carry + b_ref[rows, :]
        b_ref[rows, :] = h
        return h[0:1, :] if reverse else h[SUBLANES - 1:SUBLANES, :]

    h_ref[0:1, :] = lax.fori_loop(0, n_grp, body, h_ref[0:1, :], unroll=4)
    o_ref[...] = b_ref[...].astype(o_ref.dtype)


def _lru(z, conv_w, conv_b, wa_bd, ba, wx_bd, bx, sp, tc):
    b, ttot, _ = z.shape
    w = conv_w.shape[1]
    lc = SEQ_BLOCK
    n_all, n_ctx = ttot // lc, tc // lc
    hb = lc // SUBLANES
    n_h = ttot // SUBLANES
    col = 1

    def direction(d):
        def blk(s):
            return _seq_block(s, n_ctx, n_all, bool(d))

        vec = pl.BlockSpec((None, 1, w), lambda bi, s: (d, 0, 0))
        mat = pl.BlockSpec((None, w, w), lambda bi, s: (d, 0, 0))
        specs = [pl.BlockSpec((None, lc, w), lambda bi, s: (bi, blk(s), col)),
                 pl.BlockSpec((None, SUBLANES, w),
                              lambda bi, s: (bi, jnp.maximum(blk(s) * hb - 1, 0), col)),
                 pl.BlockSpec((None, SUBLANES, w),
                              lambda bi, s: (bi, jnp.minimum((blk(s) + 1) * hb, n_h - 1), col)),
                 mat, vec, mat, vec, vec]
        return specs, pl.BlockSpec((None, lc, w), lambda bi, s: (bi, blk(s), 0))

    (in_f, out_f), (in_r, out_r) = direction(0), direction(1)
    out = jax.ShapeDtypeStruct((b, ttot, w), BF16)
    per_dir = [pltpu.VMEM((lc, w), F32)] * 3 + [pltpu.VMEM((SUBLANES, w), F32)]
    vecs = lambda x: x.reshape(2, 1, w)
    args = (z, z, z, wa_bd, vecs(ba), wx_bd, vecs(bx), vecs(sp))
    return pl.pallas_call(
        functools.partial(_lru_kernel, lc=lc, n_ctx=n_ctx, n_all=n_all),
        grid=(b, n_all),
        in_specs=[pl.BlockSpec((CONV_W, w), lambda bi, s: (0, 0)),
                  pl.BlockSpec((1, w), lambda bi, s: (0, 0))] + in_f + in_r,
        out_specs=[out_f, out_r],
        out_shape=[out, out],
        scratch_shapes=per_dir + per_dir,
        compiler_params=_cparams(("parallel", "arbitrary")),
        name="rglru_bidir",
    )(conv_w, conv_b.reshape(1, w), *args, *args)


def _rope(x, cos, sin_signed):
    lane = lax.broadcasted_iota(jnp.int32, x.shape, 1)
    swapped = jnp.where(lane % 32 < 16, pltpu.roll(x, LANES - 16, 1), pltpu.roll(x, 16, 1))
    return x * cos + swapped * sin_signed


def _pair_layout(x, x_sw, head):
    lo = lax.broadcasted_iota(jnp.int32, x.shape, 1) < HEAD_DIM
    first, second = (x, x_sw) if head == 0 else (x_sw, x)
    return jnp.concatenate([jnp.where(lo, first, 0.0), jnp.where(lo, 0.0, second)],
                           axis=0).astype(BF16)


def _attend(q_block, keys, vals, bias, sink_row, o_ref):
    nk = keys.shape[0]
    keys_sw = pltpu.roll(keys, HEAD_DIM, 1)
    vals_t = vals.T
    top = lax.broadcasted_iota(jnp.int32, (LANES, BLOCK_Q), 0) < HEAD_DIM
    zero_t = jnp.zeros((HEAD_DIM, nk), F32)
    pairs_per_kv = ATT_GROUP // 2
    n_pairs = ATT_KV_HEADS * pairs_per_kv
    nt = (((1,), (1,)), ((), ()))
    kabs, vabs = [], []
    for kv in range(ATT_KV_HEADS):
        kabs.append(_pair_layout(keys, keys_sw, kv))
        vt = vals_t[kv * HEAD_DIM:(kv + 1) * HEAD_DIM, :]
        vabs.append(jnp.concatenate([jnp.concatenate([vt, zero_t], axis=1),
                                     jnp.concatenate([zero_t, vt], axis=1)], axis=0).astype(BF16))
    scores = [lax.dot_general(kabs[g // pairs_per_kv], q_block(g), nt, preferred_element_type=F32)
              for g in range(n_pairs)]
    probs, scales = [], []
    for g in range(n_pairs):
        halves, recips = [], []
        for half in range(2):
            sh = scores[g][half * nk:(half + 1) * nk, :]
            if bias is not None:
                sh = sh + bias
            sk = sink_row[:, 2 * g + half:2 * g + half + 1]
            m = jnp.maximum(jnp.max(sh, axis=0, keepdims=True), sk)
            p = jnp.exp2(sh - m)
            recips.append(1.0 / (jnp.sum(p, axis=0, keepdims=True) + jnp.exp2(sk - m)))
            halves.append(p.astype(BF16))
        probs.append(jnp.concatenate(halves, axis=0))
        scales.append(jnp.where(top, recips[0], recips[1]))
    outs = [jnp.dot(vabs[g // pairs_per_kv], probs[g], preferred_element_type=F32)
            for g in range(n_pairs)]
    for g in range(n_pairs):
        o_ref[:, g * LANES:(g + 1) * LANES] = (outs[g] * scales[g]).T.astype(o_ref.dtype)


def _attn_kernel(q_ref, kp_ref, k0_ref, kn_ref, vp_ref, v0_ref, vn_ref, kc_ref, vc_ref,
                 cq_ref, sq_ref, cp_ref, sp_ref, cn_ref, sn_ref, sink_ref, o_ref, *, n_ctx, n_lat):
    i = pl.program_id(1)
    qscale = HEAD_DIM ** -0.5 * LOG2E
    sink_row = sink_ref[...] * LOG2E

    @pl.when(i < n_ctx)
    def _():
        def q_block(g):
            return (q_ref[:, g * LANES:(g + 1) * LANES] * qscale).astype(BF16)
        _attend(q_block, kc_ref[...], vc_ref[...], None, sink_row, o_ref)

    @pl.when(i >= n_ctx)
    def _():
        n = i - n_ctx
        cq, sq = cq_ref[...], sq_ref[...]

        def q_block(g):
            return (_rope(q_ref[:, g * LANES:(g + 1) * LANES], cq, sq) * qscale).astype(BF16)

        kp = _rope(kp_ref[...], cp_ref[...], sp_ref[...])
        k0 = _rope(k0_ref[...], cq, sq)
        kn = _rope(kn_ref[...], cn_ref[...], sn_ref[...])
        keys = jnp.concatenate([kp, k0, kn, kc_ref[...]], axis=0)
        vals = jnp.concatenate([vp_ref[...], v0_ref[...], vn_ref[...], vc_ref[...]], axis=0)
        nk = keys.shape[0]
        qpos = n * BLOCK_Q + lax.broadcasted_iota(jnp.int32, (nk, BLOCK_Q), 1)
        key = lax.broadcasted_iota(jnp.int32, (nk, BLOCK_Q), 0)
        kpos = (n - 1) * BLOCK_Q + key
        local = (jnp.abs(qpos - kpos) <= WINDOW) & (kpos >= 0) & (kpos < n_lat * BLOCK_Q)
        bias = jnp.where(local | (key >= 3 * BLOCK_Q), 0.0, NEG)
        _attend(q_block, keys, vals, bias, sink_row, o_ref)


def _attention(z, cos_t, sin_t, sink, tc):
    b, ttot, _ = z.shape
    n_all, n_ctx = ttot // BLOCK_Q, tc // BLOCK_Q
    n_lat = n_all - n_ctx
    kcol = (2 * 512 + ATT_HEADS * HEAD_DIM) // LANES
    vcol = kcol + 1
    qcol = 2 * 512 // (ATT_HEADS * HEAD_DIM)

    def prev(i):
        return jnp.maximum(i - 1, 0)

    def nxt(i):
        return jnp.minimum(i + 1, n_all - 1)

    def rows(col, f):
        return pl.BlockSpec((None, BLOCK_Q, LANES), lambda bi, i: (bi, f(i), col))

    def tab(f):
        return pl.BlockSpec((BLOCK_Q, LANES), lambda bi, i: (f(i), 0))

    same = lambda i: i
    return pl.pallas_call(
        functools.partial(_attn_kernel, n_ctx=n_ctx, n_lat=n_lat),
        grid=(b, n_all),
        in_specs=[pl.BlockSpec((None, BLOCK_Q, ATT_HEADS * HEAD_DIM), lambda bi, i: (bi, i, qcol)),
                  rows(kcol, prev), rows(kcol, same), rows(kcol, nxt),
                  rows(vcol, prev), rows(vcol, same), rows(vcol, nxt),
                  pl.BlockSpec((None, tc, LANES), lambda bi, i: (bi, 0, kcol)),
                  pl.BlockSpec((None, tc, LANES), lambda bi, i: (bi, 0, vcol)),
                  tab(same), tab(same), tab(prev), tab(prev), tab(nxt), tab(nxt),
                  pl.BlockSpec((1, LANES), lambda bi, i: (0, 0))],
        out_specs=pl.BlockSpec((None, BLOCK_Q, ATT_HEADS * HEAD_DIM), lambda bi, i: (bi, i, 0)),
        out_shape=jax.ShapeDtypeStruct((b, ttot, ATT_HEADS * HEAD_DIM), BF16),
        compiler_params=_cparams(("parallel", "parallel")),
        name="window_gqa",
    )(z, z, z, z, z, z, z, z, z, cos_t, sin_t, cos_t, sin_t, cos_t, sin_t, sink)


def _outproj_ab_kernel(x_ref, mb_ref, mc_ref, lf_ref, lb_ref, g_ref, at_ref, w_ref, o_ref, *, tm, tc):
    i = pl.program_id(1)
    lru = (lf_ref[...].astype(F32) + lb_ref[...].astype(F32)) * _gelu(g_ref[...])
    mix = jnp.concatenate([lru.astype(BF16), at_ref[...]], axis=1)
    dx = jnp.dot(mix, w_ref[...], preferred_element_type=F32)
    gate = _gate_mod(mb_ref[...], mc_ref[...], 2, i * tm, tc, tm)
    o_ref[...] = x_ref[...] + gate * dx


def _outproj_ab(xa, mod_l, lru_f, lru_b, z, att, w_out, tc, nb):
    b, ttot, d = xa.shape
    w = lru_f.shape[2]
    tm = _row_tile(ttot)
    tile = lambda width, col: pl.BlockSpec((None, tm, width), lambda bi, i: (bi, i, col))
    return pl.pallas_call(
        functools.partial(_outproj_ab_kernel, tm=tm, tc=tc),
        grid=(b, ttot // tm),
        in_specs=[tile(d, 0),
                  pl.BlockSpec((None, N_MOD, d), lambda bi, i: (bi, 0, 0)),
                  pl.BlockSpec((None, N_MOD, d), lambda bi, i: (nb, 0, 0)),
                  tile(w, 0), tile(w, 0), tile(w, 0), tile(att.shape[2], 0),
                  pl.BlockSpec(w_out.shape, lambda bi, i: (0, 0))],
        out_specs=tile(d, 0),
        out_shape=jax.ShapeDtypeStruct(xa.shape, F32),
        compiler_params=_cparams(("parallel", "parallel")),
        name="outproj_ab",
    )(xa, mod_l, mod_l, lru_f, lru_b, z, att, w_out)


def _outproj_cd_kernel(x_ref, mb_ref, mc_ref, y_ref, of_ref, ob_ref, g_ref, wg_ref, bg_ref, gn_ref,
                       w_ref, gf_ref, r_ref, o_ref, h_ref, wt_ref, cnt_ref, *, tm, tc, n_exp):
    i = pl.program_id(1)
    y = _gelu(y_ref[...])
    s5 = y * _sigmoid(jnp.dot(y.astype(BF16), wg_ref[...], preferred_element_type=F32) + bg_ref[...])
    o = of_ref[...].astype(F32) + ob_ref[...].astype(F32)
    parts = []
    for h in range(HG_HEADS):
        oh = o[:, h * HG_DK:(h + 1) * HG_DK]
        ms = jnp.mean(oh * oh, axis=-1, keepdims=True)
        parts.append(oh * lax.rsqrt(ms + EPS) * gn_ref[...])
    hg = jnp.concatenate(parts, axis=1) * _silu(g_ref[...])
    mix = jnp.concatenate([s5.astype(BF16), hg.astype(BF16)], axis=1)
    dx = jnp.dot(mix, w_ref[...], preferred_element_type=F32)
    gate = _gate_mod(mb_ref[...], mc_ref[...], 2, i * tm, tc, tm)
    x_new = x_ref[...] + gate * dx
    o_ref[...] = x_new
    h = _norm_mod(x_new, gf_ref[...], mb_ref[...], mc_ref[...], i * tm, tc, 3, 4)
    _route_tokens(h, r_ref, h_ref, wt_ref, cnt_ref, n_exp)


def _outproj_cd(xa, mod_l, y5, o_f, o_b, z, w_glu, b_glu, hg_norm, w_out, g_ffn, router_t, n_exp,
                tc, nb):
    b, ttot, d = xa.shape
    w = y5.shape[2]
    tm = _row_tile(ttot)
    n_t = ttot // tm
    gcol = z.shape[2] // w - 1
    tile = lambda width, col: pl.BlockSpec((None, tm, width), lambda bi, i: (bi, i, col))
    return pl.pallas_call(
        functools.partial(_outproj_cd_kernel, tm=tm, tc=tc, n_exp=n_exp),
        grid=(b, n_t),
        in_specs=[tile(d, 0),
                  pl.BlockSpec((None, N_MOD, d), lambda bi, i: (bi, 0, 0)),
                  pl.BlockSpec((None, N_MOD, d), lambda bi, i: (nb, 0, 0)),
                  tile(w, 0), tile(w, 0), tile(w, 0), tile(w, gcol),
                  pl.BlockSpec(w_glu.shape, lambda bi, i: (0, 0)),
                  pl.BlockSpec((1, w), lambda bi, i: (0, 0)),
                  pl.BlockSpec((1, HG_DK), lambda bi, i: (0, 0)),
                  pl.BlockSpec(w_out.shape, lambda bi, i: (0, 0)),
                  pl.BlockSpec((1, d), lambda bi, i: (0, 0)),
                  pl.BlockSpec((SUBLANES, d), lambda bi, i: (0, 0))],
        out_specs=[tile(d, 0),
                   pl.BlockSpec((tm, d), lambda bi, i: (bi * n_t + i, 0)),
                   pl.BlockSpec((SUBLANES, tm), lambda bi, i: (0, bi * n_t + i)),
                   pl.BlockSpec((None, SUBLANES, LANES), lambda bi, i: (bi * n_t + i, 0, 0))],
        out_shape=[jax.ShapeDtypeStruct(xa.shape, F32),
                   jax.ShapeDtypeStruct((b * ttot, d), BF16),
                   jax.ShapeDtypeStruct((SUBLANES, b * ttot), F32),
                   jax.ShapeDtypeStruct((b * n_t, SUBLANES, LANES), jnp.int32)],
        compiler_params=_cparams(("parallel", "parallel")),
        name="outproj_cd_route",
    )(xa, mod_l, mod_l, y5, o_f, o_b, z, w_glu, b_glu.reshape(1, w), hg_norm.reshape(1, HG_DK), w_out,
      g_ffn.reshape(1, d), router_t)


def _ffn_kernel(x_ref, mb_ref, mc_ref, g_ref, w1_ref, w3_ref, w2_ref, o_ref, h_ref, acc_ref, *, tm, tc):
    i = pl.program_id(1)
    k = pl.program_id(2)

    @pl.when(k == 0)
    def _():
        h = _norm_mod(x_ref[...], g_ref[...], mb_ref[...], mc_ref[...], i * tm, tc, 3, 4)
        h_ref[...] = h.astype(BF16)
        acc_ref[...] = jnp.zeros_like(acc_ref)

    h = h_ref[...]
    tf = w1_ref.shape[1]
    half = (tf // LANES + 1) // 2 * LANES
    y = None
    for f0, f1 in ([(0, half), (half, tf)] if 0 < half < tf else [(0, tf)]):
        a = jnp.dot(h, w1_ref[:, f0:f1], preferred_element_type=F32)
        c = jnp.dot(h, w3_ref[:, f0:f1], preferred_element_type=F32)
        part = jnp.dot((_silu(a) * c).astype(BF16), w2_ref[f0:f1, :], preferred_element_type=F32)
        y = part if y is None else y + part
    acc_ref[...] += y

    @pl.when(k == pl.num_programs(2) - 1)
    def _():
        gate = _gate_mod(mb_ref[...], mc_ref[...], 5, i * tm, tc, tm)
        o_ref[...] = x_ref[...] + gate * acc_ref[...]


def _ffn(xa, mod_l, g, w1, w3, w2, layer, tc, nb):
    b, ttot, d = xa.shape
    f = w1.shape[2]
    tm = _tall_row_tile(ttot)
    tf = f // 2 if (f // 2) % LANES == 0 else f
    return pl.pallas_call(
        functools.partial(_ffn_kernel, tm=tm, tc=tc),
        grid=(b, ttot // tm, f // tf),
        in_specs=[pl.BlockSpec((None, tm, d), lambda bi, i, k: (bi, i, 0)),
                  pl.BlockSpec((None, N_MOD, d), lambda bi, i, k: (bi, 0, 0)),
                  pl.BlockSpec((None, N_MOD, d), lambda bi, i, k: (nb, 0, 0)),
                  pl.BlockSpec((1, d), lambda bi, i, k: (0, 0)),
                  pl.BlockSpec((None, d, tf), lambda bi, i, k: (layer, 0, k)),
                  pl.BlockSpec((None, d, tf), lambda bi, i, k: (layer, 0, k)),
                  pl.BlockSpec((None, tf, d), lambda bi, i, k: (layer, k, 0))],
        out_specs=pl.BlockSpec((None, tm, d), lambda bi, i, k: (bi, i, 0)),
        out_shape=jax.ShapeDtypeStruct(xa.shape, F32),
        scratch_shapes=[pltpu.VMEM((tm, d), BF16), pltpu.VMEM((tm, d), F32)],
        compiler_params=_cparams(("parallel", "parallel", "arbitrary")),
        name="ffn_swiglu",
    )(xa, mod_l, mod_l, g.reshape(1, d), w1, w3, w2)


def _route_tokens(h, r_ref, h_ref, wt_ref, cnt_ref, n_exp):
    nt = (((1,), (1,)), ((), ()))
    h_hi = h.astype(BF16)
    h_lo = (h - h_hi.astype(F32)).astype(BF16)
    r = r_ref[...]
    r_hi = r.astype(BF16)
    r_lo = (r - r_hi.astype(F32)).astype(BF16)
    logits = (lax.dot_general(r_hi, h_hi, nt, preferred_element_type=F32)
              + lax.dot_general(r_lo, h_hi, nt, preferred_element_type=F32)
              + lax.dot_general(r_hi, h_lo, nt, preferred_element_type=F32))
    sub = lax.broadcasted_iota(jnp.int32, logits.shape, 0)
    h_ref[...] = h_hi
    logits = jnp.where(sub < n_exp, logits, NEG)
    m1 = jnp.max(logits, axis=0, keepdims=True)
    i1 = jnp.min(jnp.where(logits == m1, sub, SUBLANES), axis=0, keepdims=True)
    rest = jnp.where(sub == i1, NEG, logits)
    m2 = jnp.max(rest, axis=0, keepdims=True)
    i2 = jnp.min(jnp.where(rest == m2, sub, SUBLANES), axis=0, keepdims=True)
    e2 = jnp.exp(m2 - m1)
    g1 = 1.0 / (1.0 + e2)
    wt = jnp.where(sub == i1, g1, 0.0) + jnp.where(sub == i2, e2 * g1, 0.0)
    wt_ref[...] = wt
    cnt = jnp.sum(jnp.where(wt > 0.0, 1.0, 0.0), axis=1, keepdims=True)
    cnt_ref[...] = jnp.broadcast_to(cnt, cnt_ref.shape).astype(jnp.int32)


def _moe_kernel(cnt_ref, x_ref, *refs, tm, tc, n_exp, n_t, group):
    mb_refs = refs[:group]
    mc_ref, h_ref, wt_ref, w1_ref, w3_ref, w2_ref, o_ref, rank_ref = refs[group:]
    p = pl.program_id(0)
    e = pl.program_id(1)
    f = w1_ref.shape[1]
    half = (f // LANES + 1) // 2 * LANES
    f_split = [(0, half), (half, f)] if 0 < half < f else [(0, f)]

    @pl.when(e == 0)
    def _():
        o_ref[...] = jnp.zeros_like(o_ref)
        r_i = lax.broadcasted_iota(jnp.int32, (tm, tm), 0)
        c_i = lax.broadcasted_iota(jnp.int32, (tm, tm), 1)
        before = jnp.where(r_i < c_i, 1.0, 0.0).astype(BF16)
        for j in range(group):
            sel = wt_ref[:, j * tm:(j + 1) * tm] > 0.0
            rank = jnp.dot(jnp.where(sel, 1.0, 0.0).astype(BF16), before, preferred_element_type=F32)
            rank_ref[j] = jnp.where(sel, rank, -1.0)

    counts = [cnt_ref[(p * group + j) * n_exp + e] for j in range(group)]

    offsets = [jnp.int32(0)]
    for j in range(group - 1):
        offsets.append(offsets[-1] + counts[j])
    total = offsets[-1] + counts[-1]

    def expert_pass(m_rows, base):
        slot = (lax.broadcasted_iota(jnp.int32, (m_rows, tm), 0) + base).astype(F32)
        onehots = []
        xe = None
        gate = None
        for j in range(group):
            r_row = rank_ref[j, pl.ds(e, 1), :]
            r_row = jnp.where(r_row >= 0.0, r_row + offsets[j].astype(F32), -1.0)
            w_row = wt_ref[pl.ds(e, 1), j * tm:(j + 1) * tm]
            hit = r_row == slot
            onehot = jnp.where(hit, 1.0, 0.0).astype(BF16)
            onehots.append(onehot)
            xj = jnp.dot(onehot, h_ref[j * tm:(j + 1) * tm, :], preferred_element_type=F32)
            gj = jnp.sum(jnp.where(hit, w_row, 0.0), axis=1, keepdims=True)
            xe = xj if xe is None else xe + xj
            gate = gj if gate is None else gate + gj
        xe = xe.astype(BF16)
        y = None
        for f0, f1 in f_split:
            a = jnp.dot(xe, w1_ref[:, f0:f1], preferred_element_type=F32)
            c = jnp.dot(xe, w3_ref[:, f0:f1], preferred_element_type=F32)
            part = jnp.dot((_silu(a) * c * gate).astype(BF16), w2_ref[f0:f1, :],
                           preferred_element_type=F32)
            y = part if y is None else y + part
        y = y.astype(BF16)
        for j in range(group):
            o_ref[j * tm:(j + 1) * tm, :] += lax.dot_general(
                onehots[j], y, (((0,), (0,)), ((), ())), preferred_element_type=F32)

    n_full = total // MOE_PASS_ROWS

    def full_pass(it, carry):
        expert_pass(MOE_PASS_ROWS, it * MOE_PASS_ROWS)
        return carry

    lax.fori_loop(0, n_full, full_pass, 0)
    rest = total - n_full * MOE_PASS_ROWS
    for tail in range(MOE_TAIL_STEP, MOE_PASS_ROWS + 1, MOE_TAIL_STEP):
        @pl.when(jnp.logical_and(rest > tail - MOE_TAIL_STEP, rest <= tail))
        def _(tail=tail):
            expert_pass(tail, n_full * MOE_PASS_ROWS)

    @pl.when(e == n_exp - 1)
    def _():
        for j in range(group):
            i = (p * group + j) % n_t
            gate5 = _gate_mod(mb_refs[j][...], mc_ref[...], 5, i * tm, tc, tm)
            rows = slice(j * tm, (j + 1) * tm)
            o_ref[rows, :] = x_ref[rows, :] + gate5 * o_ref[rows, :]


def _moe(xa, mod_l, routed, w1, w3, w2, layer, tc, nb):
    b, ttot, d = xa.shape
    _, n_exp, _, f = w1.shape
    tm = _row_tile(ttot)
    n_t = ttot // tm
    group = 2 if (b * n_t) % 2 == 0 else 1
    h, wt, cnt = routed
    cnt = cnt[:, :n_exp, 0].reshape(-1)
    rows = group * tm
    mods = [pl.BlockSpec((None, N_MOD, d), functools.partial(
        lambda p, e, c, j: ((p * group + j) // n_t, 0, 0), j=j)) for j in range(group)]
    grid_spec = pltpu.PrefetchScalarGridSpec(
        num_scalar_prefetch=1,
        grid=(b * n_t // group, n_exp),
        in_specs=[pl.BlockSpec((rows, d), lambda p, e, c: (p, 0), pipeline_mode=pl.Buffered(1))]
        + mods
        + [pl.BlockSpec((None, N_MOD, d), lambda p, e, c: (nb, 0, 0)),
           pl.BlockSpec((rows, d), lambda p, e, c: (p, 0)),
           pl.BlockSpec((SUBLANES, rows), lambda p, e, c: (0, p)),
           pl.BlockSpec((None, None, d, f), lambda p, e, c: (layer, e, 0, 0)),
           pl.BlockSpec((None, None, d, f), lambda p, e, c: (layer, e, 0, 0)),
           pl.BlockSpec((None, None, f, d), lambda p, e, c: (layer, e, 0, 0))],
        out_specs=pl.BlockSpec((rows, d), lambda p, e, c: (p, 0)),
        scratch_shapes=[pltpu.VMEM((group, SUBLANES, tm), F32)])
    out = pl.pallas_call(
        functools.partial(_moe_kernel, tm=tm, tc=tc, n_exp=n_exp, n_t=n_t, group=group),
        grid_spec=grid_spec,
        out_shape=jax.ShapeDtypeStruct((b * ttot, d), F32),
        compiler_params=_cparams(("parallel", "arbitrary")),
        name="moe_experts",
    )(cnt, xa.reshape(b * ttot, d), *([mod_l] * group), mod_l, h, wt, w1, w3, w2)
    return out.reshape(b, ttot, d)


def _spread_groups(x, inner, row_group, n_grp):
    cols_in = x.shape[2]
    cols_out = cols_in * n_grp
    q = jnp.arange(cols_out)
    src = (q // (n_grp * inner)) * inner + q % inner
    tile = (jnp.arange(cols_in)[:, None] == src[None, :]).astype(BF16)
    col_group = (q // inner) % n_grp
    y = jnp.dot(x.astype(BF16), tile)
    return jnp.where(row_group[:, None] == col_group[None, :], y, jnp.zeros((), BF16))


def _s5_weights(a_re, a_im, log_step, b_re, b_im, c_re, c_im):
    L = S5_CHUNK
    lr = jnp.minimum(a_re, -1e-4)
    li = a_im
    dt = jnp.exp(log_step)[..., None]
    mag, ang = lr * dt, li * dt
    lbr, lbi = jnp.exp(mag) * jnp.cos(ang), jnp.exp(mag) * jnp.sin(ang)
    zr, zi = lbr - 1.0, lbi
    den = lr * lr + li * li
    fr = (zr * lr + zi * li) / den
    fi = (zi * lr - zr * li) / den
    bbr = fr[..., None] * b_re - fi[..., None] * b_im
    bbi = fr[..., None] * b_im + fi[..., None] * b_re

    def power(p):
        p = p[..., None, None, None].astype(F32)
        return jnp.exp(mag * p) * jnp.cos(ang * p), jnp.exp(mag * p) * jnp.sin(ang * p)

    n_dir, n_grp, n_st = a_re.shape
    gpb = LANES // S5_GROUP
    n_blk = n_grp // gpb
    s = jnp.arange(L)
    in_group = (jnp.arange(L * LANES) % LANES) // S5_GROUP
    st_group = jnp.arange(gpb * n_st) // n_st
    outs = []
    for d in range(n_dir):
        pr, pi = power(jnp.arange(L))
        pr, pi = pr[:, d], pi[:, d]
        cbr = (jnp.einsum('gcn,tgn,gnk->tgck', c_re[d], pr, bbr[d])
               - jnp.einsum('gcn,tgn,gnk->tgck', c_re[d], pi, bbi[d])
               - jnp.einsum('gcn,tgn,gnk->tgck', c_im[d], pr, bbi[d])
               - jnp.einsum('gcn,tgn,gnk->tgck', c_im[d], pi, bbr[d]))
        lag = (s[None, :] - s[:, None]) if d == 0 else (s[:, None] - s[None, :])
        kern = jnp.where((lag >= 0)[..., None, None, None], cbr[jnp.clip(lag, 0, L - 1)], 0.0)
        kern = kern.reshape(L, L, n_blk, gpb, S5_GROUP, S5_GROUP).transpose(2, 0, 3, 5, 1, 4)
        m = _spread_groups(kern.reshape(n_blk, L * LANES, L * S5_GROUP), S5_GROUP, in_group, gpb)
        qr, qi = power((L - 1 - s) if d == 0 else s)
        qr, qi = qr[:, d], qi[:, d]
        str_ = qr[..., None] * bbr[d] - qi[..., None] * bbi[d]
        sti = qr[..., None] * bbi[d] + qi[..., None] * bbr[d]

        def to_state(x):
            x = x.reshape(L, n_blk, gpb, n_st, S5_GROUP).transpose(1, 0, 2, 4, 3)
            return _spread_groups(x.reshape(n_blk, L * LANES, n_st), n_st, in_group, gpb)

        wst = jnp.concatenate([to_state(str_), to_state(sti)], axis=2)
        rr, ri = power((s + 1) if d == 0 else (L - s))
        rr, ri = rr[:, d], ri[:, d]
        wr = c_re[d][None] * rr[:, :, None, :] - c_im[d][None] * ri[:, :, None, :]
        wi = -(c_re[d][None] * ri[:, :, None, :] + c_im[d][None] * rr[:, :, None, :])

        def from_state(x):
            x = x.reshape(L, n_blk, gpb, S5_GROUP, n_st).transpose(1, 2, 4, 0, 3)
            return _spread_groups(x.reshape(n_blk, gpb * n_st, L * S5_GROUP), S5_GROUP, st_group, gpb)

        wout = jnp.concatenate([from_state(wr), from_state(wi)], axis=1)
        ler, lei = jnp.exp(mag[d] * L) * jnp.cos(ang[d] * L), jnp.exp(mag[d] * L) * jnp.sin(ang[d] * L)
        lam_l = jnp.concatenate([ler.reshape(n_blk, 1, gpb * n_st), lei.reshape(n_blk, 1, gpb * n_st)],
                                axis=2)
        outs.append((m, wst, wout, lam_l))
    return tuple(jnp.stack([o[k] for o in outs]) for k in range(4))


def _s5_kernel(u_ref, m_ref, wst_ref, wout_ref, lam_ref, d_ref, o_ref, x_ref, hp_ref, *, rows, rows_ctx):
    L = S5_CHUNK
    dr = pl.program_id(2)
    ns = lam_ref.shape[1] // 2
    u = jnp.concatenate([u_ref[pl.ds(s, rows, stride=L), :] for s in range(L)], axis=1)
    ub = u.astype(BF16)
    x_ref[...] = jnp.dot(ub, wst_ref[...], preferred_element_type=F32)
    lr = lam_ref[:, 0:ns]
    li = lam_ref[:, ns:2 * ns]

    def visit(r, carry):
        hr, hi = carry
        hp_ref[pl.ds(r, 1), 0:ns] = hr
        hp_ref[pl.ds(r, 1), ns:2 * ns] = hi
        xr = x_ref[pl.ds(r, 1), 0:ns]
        xi = x_ref[pl.ds(r, 1), ns:2 * ns]
        return lr * hr - li * hi + xr, lr * hi + li * hr + xi

    zero = (jnp.zeros((1, ns), F32), jnp.zeros((1, ns), F32))

    @pl.when(dr == 0)
    def _():
        lax.fori_loop(0, rows, visit, zero, unroll=4)

    @pl.when(dr == 1)
    def _():
        c = lax.fori_loop(0, rows_ctx, lambda t, c: visit(rows_ctx - 1 - t, c), zero, unroll=4)
        lax.fori_loop(0, rows - rows_ctx, lambda t, c: visit(rows - 1 - t, c), c, unroll=4)

    y_state = jnp.dot(hp_ref[...].astype(BF16), wout_ref[...], preferred_element_type=F32)
    n_in = L * LANES

    def within_chunk(col0, col1, reverse):
        k0, k1 = (col0, n_in) if reverse else (0, col1)
        return jnp.dot(ub[:, k0:k1], m_ref[k0:k1, col0:col1], preferred_element_type=F32)

    @pl.when(dr == 0)
    def _():
        for c0 in range(0, n_in, MXU_TILE):
            y = within_chunk(c0, c0 + MXU_TILE, False) + y_state[:, c0:c0 + MXU_TILE]
            for s in range(c0 // LANES, (c0 + MXU_TILE) // LANES):
                lanes = slice(s * LANES - c0, (s + 1) * LANES - c0)
                o_ref[pl.ds(s, rows, stride=L), :] = (y[:, lanes]
                                                      + d_ref[...] * u[:, s * LANES:(s + 1) * LANES])

    @pl.when(dr == 1)
    def _():
        for c0 in range(0, n_in, MXU_TILE):
            y = within_chunk(c0, c0 + MXU_TILE, True) + y_state[:, c0:c0 + MXU_TILE]
            for s in range(c0 // LANES, (c0 + MXU_TILE) // LANES):
                lanes = slice(s * LANES - c0, (s + 1) * LANES - c0)
                o_ref[pl.ds(s, rows, stride=L), :] += y[:, lanes]


def _s5(z, weights, layer, dvec, tc, width):
    b, ttot, _ = z.shape
    m, wst, wout, lam_l = weights
    n_dir, n_blk = m.shape[1], m.shape[2]
    rows, rows_ctx = ttot // S5_CHUNK, tc // S5_CHUNK
    wspec = lambda a: pl.BlockSpec((None, None, None) + a.shape[3:],
                                   lambda bi, j, dr: (layer, dr, j, 0, 0))
    return pl.pallas_call(
        functools.partial(_s5_kernel, rows=rows, rows_ctx=rows_ctx),
        grid=(b, n_blk, n_dir),
        in_specs=[pl.BlockSpec((None, ttot, LANES), lambda bi, j, dr: (bi, 0, j)),
                  wspec(m), wspec(wst), wspec(wout), wspec(lam_l),
                  pl.BlockSpec((1, LANES), lambda bi, j, dr: (0, j))],
        out_specs=pl.BlockSpec((None, ttot, LANES), lambda bi, j, dr: (bi, 0, j)),
        out_shape=jax.ShapeDtypeStruct((b, ttot, width), F32),
        scratch_shapes=[pltpu.VMEM((rows, lam_l.shape[4]), F32), pltpu.VMEM((rows, lam_l.shape[4]), F32)],
        compiler_params=_cparams(("parallel", "parallel", "arbitrary")),
        name="s5_bidir",
    )(z, m, wst, wout, lam_l, dvec.reshape(1, width))


def _gla_kernel(qf_ref, ff_ref, vf_ref, qb_ref, fb_ref, vb_ref, lb_ref, of_ref, ob_ref,
                stf_ref, stb_ref, *, lc):
    step = pl.program_id(1)

    @pl.when(step == 0)
    def _():
        stf_ref[...] = jnp.zeros_like(stf_ref)
        stb_ref[...] = jnp.zeros_like(stb_ref)

    _gla_block(qf_ref, ff_ref, vf_ref, lb_ref, of_ref, stf_ref, lc=lc, reverse=False)
    _gla_block(qb_ref, fb_ref, vb_ref, lb_ref, ob_ref, stb_ref, lc=lc, reverse=True)


def _gla_block(q_ref, f_ref, v_ref, lb_ref, o_ref, st_ref, *, lc, reverse):
    c = HG_CHUNK
    n_chunks = lc // c
    r_i = lax.broadcasted_iota(jnp.int32, (lc, lc), 0)
    c_i = lax.broadcasted_iota(jnp.int32, (lc, lc), 1)
    same_chunk = (r_i // c) == (c_i // c)
    keep = same_chunk & ((c_i >= r_i) if reverse else (c_i <= r_i))
    tri = jnp.where(keep, 1.0, 0.0).astype(BF16)
    lb = lb_ref[...]
    q = _silu(q_ref[...])
    f = lb + (1.0 - lb) / (1.0 + jnp.exp(-f_ref[...]))
    k = 1.0 - f
    logf = jnp.log(f)
    v = v_ref[...].astype(BF16)
    hi = logf.astype(BF16)
    lo = (logf - hi.astype(F32)).astype(BF16)
    cum = jnp.dot(tri, hi, preferred_element_type=F32) + jnp.dot(tri, lo, preferred_element_type=F32)
    totals = [cum[ci * c:ci * c + 1, :] if reverse else cum[(ci + 1) * c - 1:(ci + 1) * c, :]
              for ci in range(n_chunks)]
    w = cum.shape[1]
    total = jnp.concatenate([jnp.broadcast_to(t, (c, w)) for t in totals], axis=0)
    centre = jnp.concatenate([jnp.broadcast_to(cum[ci * c + c // 2:ci * c + c // 2 + 1, :], (c, w))
                              for ci in range(n_chunks)], axis=0)
    q_in = (q * jnp.exp(cum)).astype(BF16)
    k_out = (k * jnp.exp(total - cum)).astype(BF16)
    q_loc = (q * jnp.exp(cum - centre)).astype(BF16)
    k_loc = (k * jnp.exp(centre - cum)).astype(BF16)
    decays = [jnp.exp(t) for t in totals]
    order = range(n_chunks - 1, -1, -1) if reverse else range(n_chunks)
    heads = [slice(h * HG_DK, (h + 1) * HG_DK) for h in range(HG_HEADS)]
    nt = (((1,), (1,)), ((), ()))
    tn = (((0,), (0,)), ((), ()))
    atts = [lax.dot_general(q_loc[:, sl], k_loc[:, sl], nt, preferred_element_type=F32)
            for sl in heads]
    atts = [jnp.where(keep, att, 0.0).astype(BF16) for att in atts]
    o_locs = [jnp.dot(att, v[:, sl], preferred_element_type=F32) for att, sl in zip(atts, heads)]
    incs = [[lax.dot_general(v[ci * c:(ci + 1) * c, sl], k_out[ci * c:(ci + 1) * c, sl], tn,
                             preferred_element_type=F32) for ci in range(n_chunks)] for sl in heads]
    states = [st_ref[h] for h in range(HG_HEADS)]
    for ci in order:
        rows = slice(ci * c, (ci + 1) * c)
        for h, sl in enumerate(heads):
            o = o_locs[h][rows] + lax.dot_general(q_in[rows, sl], states[h].astype(BF16), nt,
                                                  preferred_element_type=F32)
            o_ref[rows, sl] = o.astype(o_ref.dtype)
            states[h] = states[h] * decays[ci][:, sl] + incs[h][ci]
    for h in range(HG_HEADS):
        st_ref[h] = states[h]


def _gla(z, lb, tc):
    b, ttot, _ = z.shape
    w = HG_HEADS * HG_DK
    lc = SEQ_BLOCK
    n_all, n_ctx = ttot // lc, tc // lc

    def col(cidx, reverse):
        return pl.BlockSpec((None, lc, w),
                            lambda bi, s: (bi, _seq_block(s, n_ctx, n_all, reverse), cidx))

    out = jax.ShapeDtypeStruct((b, ttot, w), BF16)
    state = pltpu.VMEM((HG_HEADS, HG_DK, HG_DK), F32)
    return pl.pallas_call(
        functools.partial(_gla_kernel, lc=lc),
        grid=(b, n_all),
        in_specs=[col(1, False), col(2, False), col(4, False),
                  col(1, True), col(3, True), col(4, True),
                  pl.BlockSpec((1, w), lambda bi, s: (0, 0))],
        out_specs=[col(0, False), col(0, True)],
        out_shape=[out, out],
        scratch_shapes=[state, state],
        compiler_params=_cparams(("parallel", "arbitrary")),
        name="hgrn2_bidir",
    )(z, z, z, z, z, z, lb.reshape(1, w))


def _final_kernel(x_ref, g_ref, o_ref):
    x = x_ref[0]
    ms = jnp.mean(x * x, axis=-1, keepdims=True)
    o_ref[...] = x * lax.rsqrt(ms + EPS) * g_ref[...]


def _final_norm(xa, g, tc):
    b, ttot, d = xa.shape
    t = ttot - tc
    tm = next(rows for rows in (1024, 512, SEQ_BLOCK) if t % rows == 0)
    return pl.pallas_call(
        _final_kernel,
        grid=(b, t // tm),
        in_specs=[pl.BlockSpec((pl.Element(1), pl.Element(tm), pl.Element(d)),
                               lambda bi, i: (bi, pl.multiple_of(tc + i * tm, SUBLANES), 0)),
                  pl.BlockSpec((1, d), lambda bi, i: (0, 0))],
        out_specs=pl.BlockSpec((None, tm, d), lambda bi, i: (bi, i, 0)),
        out_shape=jax.ShapeDtypeStruct((b, t, d), F32),
        compiler_params=_cparams(("parallel", "parallel")),
        name="final_norm",
    )(xa, g.reshape(1, d))


def _rope_tables(t, tc):
    pos = jnp.arange(t)
    row = (pos // GRID_W).astype(F32)
    col = (pos % GRID_W).astype(F32)
    inv = ROPE_BASE ** (-jnp.arange(ROPE_FREQS, dtype=F32) / ROPE_FREQS)
    ar, ac = row[:, None] * inv, col[:, None] * inv
    cos = jnp.concatenate([jnp.cos(ar), jnp.cos(ar), jnp.cos(ac), jnp.cos(ac)], axis=1)
    sin = jnp.concatenate([-jnp.sin(ar), jnp.sin(ar), -jnp.sin(ac), jnp.sin(ac)], axis=1)
    cos = jnp.concatenate([jnp.ones((tc, HEAD_DIM), F32), cos], axis=0)
    sin = jnp.concatenate([jnp.zeros((tc, HEAD_DIM), F32), sin], axis=0)
    return jnp.tile(cos, (1, LANES // HEAD_DIM)), jnp.tile(sin, (1, LANES // HEAD_DIM))


def _block_diag_dense(w):
    nblk, h, k = w.shape
    return jnp.einsum('nhk,nm->nhmk', w, jnp.eye(nblk, dtype=w.dtype)).reshape(nblk * h, nblk * k)


def kernel(x, c, ctx, c_ctx, w_mod, b_mod, norm_mix, norm_ffn, final_norm, w_in_ab, lru_conv_w, lru_conv_b, lru_wa, lru_ba, lru_wx, lru_bx, lru_lam, attn_sink, w_out_ab, ffn_w1, ffn_w3, ffn_w2, w_in_cd, s5_a_re, s5_a_im, s5_log_step, s5_b_re, s5_b_im, s5_c_re, s5_c_im, s5_d, s5_w_glu, s5_b_glu, hg_lb_raw, hg_norm, w_out_cd, moe_router, moe_w1, moe_w3, moe_w2):
    nb, t, d = x.shape
    tc = ctx.shape[1]
    depth = w_mod.shape[0]
    assert tc % SEQ_BLOCK == 0 and t % SEQ_BLOCK == 0 and t % GRID_W == 0

    xa = jnp.concatenate([ctx, x], axis=1)
    mod_rows = -(-(nb + 1) // SUBLANES) * SUBLANES
    cvec = jnp.zeros((mod_rows, d), F32).at[:nb].set(c).at[nb].set(c_ctx)
    mod = _modulation(cvec, w_mod, b_mod).reshape(depth, mod_rows, N_MOD, d)

    cos_t, sin_t = _rope_tables(t, tc)
    lb_soft = jax.nn.softmax(hg_lb_raw.astype(F32), axis=0)
    lb_table = jnp.cumsum(lb_soft, axis=0) - lb_soft[0:1]
    n_exp = moe_router.shape[2]
    s5w = jax.vmap(_s5_weights)(s5_a_re, s5_a_im, s5_log_step, s5_b_re, s5_b_im, s5_c_re, s5_c_im)
    ffn_w = [w.astype(BF16) for w in (ffn_w1, ffn_w3, ffn_w2)]
    moe_w = [w.astype(BF16) for w in (moe_w1, moe_w3, moe_w2)]

    for l in range(depth):
        j = l // 2
        mod_l = mod[l]
        if l % 2 == 0:
            z = _inproj(xa, mod_l, norm_mix[l], w_in_ab[j].astype(BF16), tc, nb)
            lru = _lru(z, lru_conv_w[j], lru_conv_b[j],
                       jnp.stack([_block_diag_dense(lru_wa[j, dr]) for dr in range(2)]).astype(BF16),
                       lru_ba[j],
                       jnp.stack([_block_diag_dense(lru_wx[j, dr]) for dr in range(2)]).astype(BF16),
                       lru_bx[j], jax.nn.softplus(-lru_lam[j]), tc)
            sink = jnp.zeros((1, LANES), F32).at[0, :ATT_HEADS].set(attn_sink[j])
            att = _attention(z, cos_t, sin_t, sink, tc)
            xa = _outproj_ab(xa, mod_l, lru[0], lru[1], z, att, w_out_ab[j].astype(BF16), tc, nb)
            xa = _ffn(xa, mod_l, norm_ffn[l], ffn_w[0], ffn_w[1], ffn_w[2], j, tc, nb)
        else:
            z = _inproj(xa, mod_l, norm_mix[l], w_in_cd[j].astype(BF16), tc, nb)
            y5 = _s5(z, s5w, j, s5_d[j], tc, s5_d.shape[1])
            o_f, o_b = _gla(z, lb_table[j], tc)
            router_t = jnp.zeros((SUBLANES, d), F32).at[:n_exp].set(moe_router[j].T)
            xa, *routed = _outproj_cd(xa, mod_l, y5, o_f, o_b, z, s5_w_glu[j].astype(BF16),
                                      s5_b_glu[j], hg_norm[j], w_out_cd[j].astype(BF16),
                                      norm_ffn[l], router_t, n_exp, tc, nb)
            xa = _moe(xa, mod_l, routed, moe_w[0], moe_w[1], moe_w[2], j, tc, nb)
    return _final_norm(xa, final_norm, tc)
```
